```python
import math
import jax
import jax.numpy as jnp
from jax import lax
import numpy as np

D_MODEL = 1024
BATCH = 8
SEQ = 4096
DEPTH = 2

N_BRANCH = 4
BRANCH_W = 512
SSD_HEADS = 8
SSD_HEAD_DIM = 64
SSD_GROUPS = 2
SSD_STATE = 128
SSD_CHUNK = 128
SSD_CONV = 4
SSD_XBC = BRANCH_W + 2 * SSD_GROUPS * SSD_STATE
DN_HEADS = 4
DN_HEAD_DIM = 128
DN_CHUNK = 64
DN_CONV = 4
SG_GROUPS = 4
SG_GROUP_DIM = BRANCH_W // SG_GROUPS
SG_CHUNK = 128
FOX_HEADS = 8
FOX_HEAD_DIM = 64
FOX_BLOCK = 128
D_FF = 4 * D_MODEL
IN_SIZES = (BRANCH_W, SSD_XBC, SSD_HEADS, 3 * BRANCH_W, DN_HEADS, DN_HEADS, BRANCH_W,
            2 * BRANCH_W, 3 * BRANCH_W, FOX_HEADS, N_BRANCH * D_MODEL)
D_IN = sum(IN_SIZES)
LN_EPS = 1e-5
NORM_EPS = 1e-6
DEEPNORM_ALPHA = (2 * DEPTH) ** 0.25
DEEPNORM_BETA = (8 * DEPTH) ** -0.25

kernel_name = "hybrid_ssd_deltanet_sgmlp_fox"


def layer_norm(x, g, b):
    xf = x.astype(jnp.float32)
    mu = jnp.mean(xf, -1, keepdims=True)
    var = jnp.mean(jnp.square(xf - mu), -1, keepdims=True)
    y = (xf - mu) * lax.rsqrt(var + LN_EPS) * g.astype(jnp.float32) + b.astype(jnp.float32)
    return y.astype(x.dtype)


def rms_norm(x, w):
    xf = x.astype(jnp.float32)
    return xf * lax.rsqrt(jnp.mean(xf * xf, -1, keepdims=True) + NORM_EPS) * w.astype(jnp.float32)


def l2_normalize(t):
    return t * lax.rsqrt(jnp.sum(t * t, -1, keepdims=True) + NORM_EPS)


def causal_dwconv(x, w):
    k = w.shape[0]
    return lax.conv_general_dilated(x, w[:, None, :], window_strides=(1,), padding=[(k - 1, 0)],
                                    dimension_numbers=('NWC', 'WIO', 'NWC'),
                                    feature_group_count=x.shape[-1])


def ssd_mixer(z, xbc, dt_raw, conv_w, conv_b, dt_bias, a_log, d_skip, norm_w):
    f32 = jnp.float32
    b, L, _ = z.shape
    nc, Q, G = L // SSD_CHUNK, SSD_CHUNK, SSD_GROUPS
    e = SSD_HEADS // G
    xbc = jax.nn.silu(causal_dwconv(xbc, conv_w) + conv_b)
    xs, bm, cm = jnp.split(xbc.astype(f32), [BRANCH_W, BRANCH_W + G * SSD_STATE], axis=-1)
    dt = jax.nn.softplus(dt_raw.astype(f32) + dt_bias.astype(f32))
    a = -jnp.exp(a_log.astype(f32))
    xh = xs.reshape(b, nc, Q, G, e, SSD_HEAD_DIM)
    xdt = xh * dt.reshape(b, nc, Q, G, e)[..., None]
    bm = bm.reshape(b, nc, Q, G, SSD_STATE)
    cm = cm.reshape(b, nc, Q, G, SSD_STATE)
    a_cs = jnp.cumsum((dt * a).reshape(b, nc, Q, G, e).transpose(0, 3, 4, 1, 2), axis=-1)
    tri = jnp.tril(jnp.ones((Q, Q), bool))
    seg = jnp.exp(jnp.where(tri, a_cs[..., :, None] - a_cs[..., None, :], -jnp.inf))
    cb = jnp.einsum('bclgn,bcsgn->bgcls', cm, bm)
    y_diag = jnp.einsum('bgcls,bgecls,bcsgep->bclgep', cb, seg, xdt)
    decay_to_end = jnp.exp(a_cs[..., -1:] - a_cs)
    states = jnp.einsum('bcsgn,bgecs,bcsgep->cbgepn', bm, decay_to_end, xdt)
    chunk_decay = jnp.exp(a_cs[..., -1]).transpose(3, 0, 1, 2)

    def step(h, inp):
        st, dec = inp
        return h * dec[..., None, None] + st, h

    _, prev = lax.scan(step, jnp.zeros(states.shape[1:], f32), (states, chunk_decay))
    y_off = jnp.einsum('bclgn,cbgepn,bgecl->bclgep', cm, prev, jnp.exp(a_cs))
    y = y_diag + y_off + d_skip.astype(f32).reshape(G, e)[:, :, None] * xh
    y = y.reshape(b, L, BRANCH_W) * jax.nn.silu(z.astype(f32))
    return rms_norm(y, norm_w).astype(z.dtype)


def gated_deltanet_mixer(qkv, beta_raw, a_raw, gate, conv_w, a_log, dt_bias, norm_w):
    f32 = jnp.float32
    b, L, _ = qkv.shape
    H, Dk, C = DN_HEADS, DN_HEAD_DIM, DN_CHUNK
    nc = L // C
    qkv = jax.nn.silu(causal_dwconv(qkv, conv_w)).astype(f32)
    q, k, v = jnp.split(qkv, 3, axis=-1)
    q = l2_normalize(q.reshape(b, L, H, Dk)) * (Dk ** -0.5)
    k = l2_normalize(k.reshape(b, L, H, Dk))
    v = v.reshape(b, L, H, Dk)

    def chunk4(t):
        return t.reshape(b, nc, C, H, Dk).transpose(1, 0, 3, 2, 4)

    def chunk3(t):
        return t.reshape(b, nc, C, H).transpose(1, 0, 3, 2)

    q, k, v = chunk4(q), chunk4(k), chunk4(v)
    beta = chunk3(jax.nn.sigmoid(beta_raw.astype(f32)))
    g = -jnp.exp(a_log.astype(f32)) * jax.nn.softplus(a_raw.astype(f32) + dt_bias.astype(f32))
    g_cs = jnp.cumsum(chunk3(g), axis=-1)
    tri = jnp.tril(jnp.ones((C, C), bool))
    strict = jnp.tril(jnp.ones((C, C), f32), -1)
    gamma = jnp.exp(jnp.where(tri, g_cs[..., :, None] - g_cs[..., None, :], -jnp.inf))
    kb = k * beta[..., None]
    a_mat = jnp.einsum('nbhid,nbhjd->nbhij', kb, k) * gamma * strict
    m = a_mat + jnp.eye(C, dtype=f32)
    rhs = jnp.concatenate([kb * jnp.exp(g_cs)[..., None], v * beta[..., None]], axis=-1)
    sol = lax.linalg.triangular_solve(m, rhs, left_side=True, lower=True, unit_diagonal=True)
    w_c, u_c = jnp.split(sol, 2, axis=-1)
    qg = q * jnp.exp(g_cs)[..., None]
    qk = jnp.einsum('nbhid,nbhjd->nbhij', q, k) * gamma
    k_dec = k * jnp.exp(g_cs[..., -1:] - g_cs)[..., None]
    last = jnp.exp(g_cs[..., -1])

    def step(S, inp):
        qg_c, qk_c, w_i, u_i, kd_c, last_c = inp
        v_new = u_i - jnp.einsum('bhcd,bhde->bhce', w_i, S)
        o = jnp.einsum('bhcd,bhde->bhce', qg_c, S) + jnp.einsum('bhij,bhje->bhie', qk_c, v_new)
        S = S * last_c[..., None, None] + jnp.einsum('bhcd,bhce->bhde', kd_c, v_new)
        return S, o

    S0 = jnp.zeros((b, H, Dk, Dk), f32)
    _, o = lax.scan(step, S0, (qg, qk, w_c, u_c, k_dec, last))
    o = o.transpose(1, 0, 3, 2, 4).reshape(b, L, H, Dk)
    o = rms_norm(o, norm_w) * jax.nn.silu(gate.astype(f32).reshape(b, L, H, Dk))
    return o.reshape(b, L, BRANCH_W).astype(gate.dtype)


def spatial_gating_mixer(uv, ln_g, ln_b, w_s, b_s):
    b, L, _ = uv.shape
    nc = L // SG_CHUNK
    u, v = jnp.split(jax.nn.gelu(uv), 2, axis=-1)
    v = layer_norm(v, ln_g, ln_b).reshape(b, nc, SG_CHUNK, SG_GROUPS, SG_GROUP_DIM)
    w = w_s * jnp.tril(jnp.ones((SG_CHUNK, SG_CHUNK), w_s.dtype))
    v = jnp.einsum('gts,bcsgd->bctgd', w, v) + b_s.T[None, None, :, :, None]
    return u * v.reshape(b, L, BRANCH_W)


def forgetting_attention_mixer(qkv, f_raw, f_bias):
    f32 = jnp.float32
    b, L, _ = qkv.shape
    H, Dh, BLK = FOX_HEADS, FOX_HEAD_DIM, FOX_BLOCK
    nb = L // BLK
    q, k, v = (t.reshape(b, L, H, Dh) for t in jnp.split(qkv, 3, axis=-1))
    c = jnp.cumsum(jax.nn.log_sigmoid(f_raw.astype(f32) + f_bias.astype(f32)), axis=1)
    c_keys = c.transpose(0, 2, 1)
    qb = q.reshape(b, nb, BLK, H, Dh).transpose(1, 0, 2, 3, 4)
    cb = c.reshape(b, nb, BLK, H).transpose(1, 0, 3, 2)
    starts = jnp.arange(nb, dtype=jnp.int32) * BLK
    k_pos = jnp.arange(L, dtype=jnp.int32)
    scale = Dh ** -0.5

    def block(args):
        q_blk, c_blk, start = args
        s = jnp.einsum('bqhd,bkhd->bhqk', q_blk, k).astype(f32) * scale
        s = s + c_blk[..., None] - c_keys[:, :, None, :]
        q_pos = start + jnp.arange(BLK, dtype=jnp.int32)
        s = jnp.where(k_pos[None, :] <= q_pos[:, None], s, -jnp.inf)
        p = jax.nn.softmax(s, axis=-1)
        return jnp.einsum('bhqk,bkhd->bqhd', p.astype(v.dtype), v)

    o = lax.map(block, (qb, cb, starts))
    return o.transpose(1, 0, 2, 3, 4).reshape(b, L, BRANCH_W)


def _fwd_setup_inputs(seed: int = 0) -> dict:
    key = jax.random.key(seed)
    ks = iter(jax.random.split(key, 40))
    f32 = jnp.float32

    def nrm(shape, scale):
        return scale * jax.random.normal(next(ks), shape, f32)

    def gain(shape):
        return 1.0 + nrm(shape, 0.02)

    def dt_bias_init(shape):
        u = jax.random.uniform(next(ks), shape, f32)
        dt = jnp.exp(u * (math.log(0.1) - math.log(0.001)) + math.log(0.001))
        return dt + jnp.log(-jnp.expm1(-dt))

    def a_log_init(shape):
        return jnp.log(jax.random.uniform(next(ks), shape, f32, minval=1.0, maxval=16.0))

    x = nrm((BATCH, SEQ, D_MODEL), 1.0)
    return {
        "x": x,
        "ln_in_g": gain((D_MODEL,)),
        "ln_in_b": nrm((D_MODEL,), 0.02),
        "w_in": nrm((DEPTH, D_MODEL, D_IN), D_MODEL ** -0.5),
        "ssd_conv_w": nrm((DEPTH, SSD_CONV, SSD_XBC), SSD_CONV ** -0.5),
        "ssd_conv_b": nrm((DEPTH, SSD_XBC), 0.02),
        "ssd_dt_bias": dt_bias_init((DEPTH, SSD_HEADS)),
        "ssd_a_log": a_log_init((DEPTH, SSD_HEADS)),
        "ssd_d": gain((DEPTH, SSD_HEADS)),
        "ssd_norm_w": gain((DEPTH, BRANCH_W)),
        "dn_conv_w": nrm((DEPTH, DN_CONV, 3 * BRANCH_W), DN_CONV ** -0.5),
        "dn_a_log": a_log_init((DEPTH, DN_HEADS)),
        "dn_dt_bias": dt_bias_init((DEPTH, DN_HEADS)),
        "dn_norm_w": gain((DEPTH, DN_HEAD_DIM)),
        "sg_ln_g": gain((DEPTH, BRANCH_W)),
        "sg_ln_b": nrm((DEPTH, BRANCH_W), 0.02),
        "sg_w": nrm((DEPTH, SG_GROUPS, SG_CHUNK, SG_CHUNK), SG_CHUNK ** -0.5),
        "sg_b": 1.0 + nrm((DEPTH, SG_GROUPS, SG_CHUNK), 0.1),
        "fox_f_bias": 2.0 + nrm((DEPTH, FOX_HEADS), 0.5),
        "gate_b": nrm((DEPTH, N_BRANCH, D_MODEL), 0.02),
        "w_branch": nrm((DEPTH, N_BRANCH, BRANCH_W, D_MODEL), BRANCH_W ** -0.5),
        "w_out": nrm((DEPTH, D_MODEL, D_MODEL), DEEPNORM_BETA * D_MODEL ** -0.5),
        "ln1_g": gain((DEPTH, D_MODEL)),
        "ln1_b": nrm((DEPTH, D_MODEL), 0.02),
        "w_up": nrm((DEPTH, D_MODEL, D_FF), D_MODEL ** -0.5),
        "w_down": nrm((DEPTH, D_FF, D_MODEL), DEEPNORM_BETA * D_FF ** -0.5),
        "ln2_g": gain((DEPTH, D_MODEL)),
        "ln2_b": nrm((DEPTH, D_MODEL), 0.02),
    }


def _fwd_reference(x, ln_in_g, ln_in_b, w_in, ssd_conv_w, ssd_conv_b, ssd_dt_bias, ssd_a_log, ssd_d,
              ssd_norm_w, dn_conv_w, dn_a_log, dn_dt_bias, dn_norm_w, sg_ln_g, sg_ln_b, sg_w, sg_b,
              fox_f_bias, gate_b, w_branch, w_out, ln1_g, ln1_b, w_up, w_down, ln2_g, ln2_b):
    b, L, _ = x.shape
    splits = np.cumsum(IN_SIZES)[:-1].tolist()
    h = layer_norm(x, ln_in_g, ln_in_b)
    for l in range(DEPTH):
        proj = jnp.einsum('bld,dc->blc', h, w_in[l])
        (ssd_z, ssd_xbc, ssd_dt, dn_qkv, dn_beta, dn_a, dn_gate,
         sg_uv, fox_qkv, fox_f, gate_logits) = jnp.split(proj, splits, axis=-1)
        y_a = ssd_mixer(ssd_z, ssd_xbc, ssd_dt, ssd_conv_w[l], ssd_conv_b[l], ssd_dt_bias[l],
                        ssd_a_log[l], ssd_d[l], ssd_norm_w[l])
        y_b = gated_deltanet_mixer(dn_qkv, dn_beta, dn_a, dn_gate, dn_conv_w[l], dn_a_log[l],
                                   dn_dt_bias[l], dn_norm_w[l])
        y_c = spatial_gating_mixer(sg_uv, sg_ln_g[l], sg_ln_b[l], sg_w[l], sg_b[l])
        y_d = forgetting_attention_mixer(fox_qkv, fox_f, fox_f_bias[l])
        gates = jax.nn.sigmoid(gate_logits.reshape(b, L, N_BRANCH, D_MODEL) + gate_b[l])
        merged = gates[:, :, 0] * jnp.einsum('blc,cd->bld', y_a, w_branch[l, 0])
        merged = merged + gates[:, :, 1] * jnp.einsum('blc,cd->bld', y_b, w_branch[l, 1])
        merged = merged + gates[:, :, 2] * jnp.einsum('blc,cd->bld', y_c, w_branch[l, 2])
        merged = merged + gates[:, :, 3] * jnp.einsum('blc,cd->bld', y_d, w_branch[l, 3])
        mix = jnp.einsum('bld,de->ble', merged, w_out[l])
        h = layer_norm(DEEPNORM_ALPHA * h + mix, ln1_g[l], ln1_b[l])
        ff = jnp.einsum('blf,fd->bld', jnp.square(jax.nn.relu(jnp.einsum('bld,df->blf', h, w_up[l]))), w_down[l])
        h = layer_norm(DEEPNORM_ALPHA * h + ff, ln2_g[l], ln2_b[l])
    return h


import jax as _jax
import jax.numpy as _jnp

TWIN_FORMAT = 'train_step'
FWD_PARAMS = ['x', 'ln_in_g', 'ln_in_b', 'w_in', 'ssd_conv_w', 'ssd_conv_b', 'ssd_dt_bias', 'ssd_a_log', 'ssd_d', 'ssd_norm_w', 'dn_conv_w', 'dn_a_log', 'dn_dt_bias', 'dn_norm_w', 'sg_ln_g', 'sg_ln_b', 'sg_w', 'sg_b', 'fox_f_bias', 'gate_b', 'w_branch', 'w_out', 'ln1_g', 'ln1_b', 'w_up', 'w_down', 'ln2_g', 'ln2_b']
TWIN_WEIGHTS = ['ln_in_g', 'ln_in_b', 'w_in', 'ssd_conv_w', 'ssd_conv_b', 'ssd_dt_bias', 'ssd_a_log', 'ssd_d', 'ssd_norm_w', 'dn_conv_w', 'dn_a_log', 'dn_dt_bias', 'dn_norm_w', 'sg_ln_g', 'sg_ln_b', 'sg_w', 'sg_b', 'fox_f_bias', 'gate_b', 'w_branch', 'w_out', 'ln1_g', 'ln1_b', 'w_up', 'w_down', 'ln2_g', 'ln2_b']
TWIN_DIFF_INPUT = 'x'
TWIN_INPUTS = ['x', 'ln_in_g', 'ln_in_b', 'w_in', 'ssd_conv_w', 'ssd_conv_b', 'ssd_dt_bias', 'ssd_a_log', 'ssd_d', 'ssd_norm_w', 'dn_conv_w', 'dn_a_log', 'dn_dt_bias', 'dn_norm_w', 'sg_ln_g', 'sg_ln_b', 'sg_w', 'sg_b', 'fox_f_bias', 'gate_b', 'w_branch', 'w_out', 'ln1_g', 'ln1_b', 'w_up', 'w_down', 'ln2_g', 'ln2_b', 'loss_target', 'm_ln_in_g', 'm_ln_in_b', 'm_w_in', 'm_ssd_conv_w', 'm_ssd_conv_b', 'm_ssd_dt_bias', 'm_ssd_a_log', 'm_ssd_d', 'm_ssd_norm_w', 'm_dn_conv_w', 'm_dn_a_log', 'm_dn_dt_bias', 'm_dn_norm_w', 'm_sg_ln_g', 'm_sg_ln_b', 'm_sg_w', 'm_sg_b', 'm_fox_f_bias', 'm_gate_b', 'm_w_branch', 'm_w_out', 'm_ln1_g', 'm_ln1_b', 'm_w_up', 'm_w_down', 'm_ln2_g', 'm_ln2_b', 'v_ln_in_g', 'v_ln_in_b', 'v_w_in', 'v_ssd_conv_w', 'v_ssd_conv_b', 'v_ssd_dt_bias', 'v_ssd_a_log', 'v_ssd_d', 'v_ssd_norm_w', 'v_dn_conv_w', 'v_dn_a_log', 'v_dn_dt_bias', 'v_dn_norm_w', 'v_sg_ln_g', 'v_sg_ln_b', 'v_sg_w', 'v_sg_b', 'v_fox_f_bias', 'v_gate_b', 'v_w_branch', 'v_w_out', 'v_ln1_g', 'v_ln1_b', 'v_w_up', 'v_w_down', 'v_ln2_g', 'v_ln2_b']
TWIN_OUTPUTS = ['loss', 'grad_x', 'grad_ln_in_g', 'grad_ln_in_b', 'grad_w_in', 'grad_ssd_conv_w', 'grad_ssd_conv_b', 'grad_ssd_dt_bias', 'grad_ssd_a_log', 'grad_ssd_d', 'grad_ssd_norm_w', 'grad_dn_conv_w', 'grad_dn_a_log', 'grad_dn_dt_bias', 'grad_dn_norm_w', 'grad_sg_ln_g', 'grad_sg_ln_b', 'grad_sg_w', 'grad_sg_b', 'grad_fox_f_bias', 'grad_gate_b', 'grad_w_branch', 'grad_w_out', 'grad_ln1_g', 'grad_ln1_b', 'grad_w_up', 'grad_w_down', 'grad_ln2_g', 'grad_ln2_b', 'delta_ln_in_g', 'delta_ln_in_b', 'delta_w_in', 'delta_ssd_conv_w', 'delta_ssd_conv_b', 'delta_ssd_dt_bias', 'delta_ssd_a_log', 'delta_ssd_d', 'delta_ssd_norm_w', 'delta_dn_conv_w', 'delta_dn_a_log', 'delta_dn_dt_bias', 'delta_dn_norm_w', 'delta_sg_ln_g', 'delta_sg_ln_b', 'delta_sg_w', 'delta_sg_b', 'delta_fox_f_bias', 'delta_gate_b', 'delta_w_branch', 'delta_w_out', 'delta_ln1_g', 'delta_ln1_b', 'delta_w_up', 'delta_w_down', 'delta_ln2_g', 'delta_ln2_b', 'new_m_ln_in_g', 'new_m_ln_in_b', 'new_m_w_in', 'new_m_ssd_conv_w', 'new_m_ssd_conv_b', 'new_m_ssd_dt_bias', 'new_m_ssd_a_log', 'new_m_ssd_d', 'new_m_ssd_norm_w', 'new_m_dn_conv_w', 'new_m_dn_a_log', 'new_m_dn_dt_bias', 'new_m_dn_norm_w', 'new_m_sg_ln_g', 'new_m_sg_ln_b', 'new_m_sg_w', 'new_m_sg_b', 'new_m_fox_f_bias', 'new_m_gate_b', 'new_m_w_branch', 'new_m_w_out', 'new_m_ln1_g', 'new_m_ln1_b', 'new_m_w_up', 'new_m_w_down', 'new_m_ln2_g', 'new_m_ln2_b', 'new_v_ln_in_g', 'new_v_ln_in_b', 'new_v_w_in', 'new_v_ssd_conv_w', 'new_v_ssd_conv_b', 'new_v_ssd_dt_bias', 'new_v_ssd_a_log', 'new_v_ssd_d', 'new_v_ssd_norm_w', 'new_v_dn_conv_w', 'new_v_dn_a_log', 'new_v_dn_dt_bias', 'new_v_dn_norm_w', 'new_v_sg_ln_g', 'new_v_sg_ln_b', 'new_v_sg_w', 'new_v_sg_b', 'new_v_fox_f_bias', 'new_v_gate_b', 'new_v_w_branch', 'new_v_w_out', 'new_v_ln1_g', 'new_v_ln1_b', 'new_v_w_up', 'new_v_w_down', 'new_v_ln2_g', 'new_v_ln2_b']
TWIN_LEAF_KINDS = {'loss': 'loss', 'grad_x': 'grad_x', 'grad_ln_in_g': 'grad_w', 'grad_ln_in_b': 'grad_w', 'grad_w_in': 'grad_w', 'grad_ssd_conv_w': 'grad_w', 'grad_ssd_conv_b': 'grad_w', 'grad_ssd_dt_bias': 'grad_w', 'grad_ssd_a_log': 'grad_w', 'grad_ssd_d': 'grad_w', 'grad_ssd_norm_w': 'grad_w', 'grad_dn_conv_w': 'grad_w', 'grad_dn_a_log': 'grad_w', 'grad_dn_dt_bias': 'grad_w', 'grad_dn_norm_w': 'grad_w', 'grad_sg_ln_g': 'grad_w', 'grad_sg_ln_b': 'grad_w', 'grad_sg_w': 'grad_w', 'grad_sg_b': 'grad_w', 'grad_fox_f_bias': 'grad_w', 'grad_gate_b': 'grad_w', 'grad_w_branch': 'grad_w', 'grad_w_out': 'grad_w', 'grad_ln1_g': 'grad_w', 'grad_ln1_b': 'grad_w', 'grad_w_up': 'grad_w', 'grad_w_down': 'grad_w', 'grad_ln2_g': 'grad_w', 'grad_ln2_b': 'grad_w', 'delta_ln_in_g': 'delta_w', 'delta_ln_in_b': 'delta_w', 'delta_w_in': 'delta_w', 'delta_ssd_conv_w': 'delta_w', 'delta_ssd_conv_b': 'delta_w', 'delta_ssd_dt_bias': 'delta_w', 'delta_ssd_a_log': 'delta_w', 'delta_ssd_d': 'delta_w', 'delta_ssd_norm_w': 'delta_w', 'delta_dn_conv_w': 'delta_w', 'delta_dn_a_log': 'delta_w', 'delta_dn_dt_bias': 'delta_w', 'delta_dn_norm_w': 'delta_w', 'delta_sg_ln_g': 'delta_w', 'delta_sg_ln_b': 'delta_w', 'delta_sg_w': 'delta_w', 'delta_sg_b': 'delta_w', 'delta_fox_f_bias': 'delta_w', 'delta_gate_b': 'delta_w', 'delta_w_branch': 'delta_w', 'delta_w_out': 'delta_w', 'delta_ln1_g': 'delta_w', 'delta_ln1_b': 'delta_w', 'delta_w_up': 'delta_w', 'delta_w_down': 'delta_w', 'delta_ln2_g': 'delta_w', 'delta_ln2_b': 'delta_w', 'new_m_ln_in_g': 'new_m', 'new_m_ln_in_b': 'new_m', 'new_m_w_in': 'new_m', 'new_m_ssd_conv_w': 'new_m', 'new_m_ssd_conv_b': 'new_m', 'new_m_ssd_dt_bias': 'new_m', 'new_m_ssd_a_log': 'new_m', 'new_m_ssd_d': 'new_m', 'new_m_ssd_norm_w': 'new_m', 'new_m_dn_conv_w': 'new_m', 'new_m_dn_a_log': 'new_m', 'new_m_dn_dt_bias': 'new_m', 'new_m_dn_norm_w': 'new_m', 'new_m_sg_ln_g': 'new_m', 'new_m_sg_ln_b': 'new_m', 'new_m_sg_w': 'new_m', 'new_m_sg_b': 'new_m', 'new_m_fox_f_bias': 'new_m', 'new_m_gate_b': 'new_m', 'new_m_w_branch': 'new_m', 'new_m_w_out': 'new_m', 'new_m_ln1_g': 'new_m', 'new_m_ln1_b': 'new_m', 'new_m_w_up': 'new_m', 'new_m_w_down': 'new_m', 'new_m_ln2_g': 'new_m', 'new_m_ln2_b': 'new_m', 'new_v_ln_in_g': 'new_v', 'new_v_ln_in_b': 'new_v', 'new_v_w_in': 'new_v', 'new_v_ssd_conv_w': 'new_v', 'new_v_ssd_conv_b': 'new_v', 'new_v_ssd_dt_bias': 'new_v', 'new_v_ssd_a_log': 'new_v', 'new_v_ssd_d': 'new_v', 'new_v_ssd_norm_w': 'new_v', 'new_v_dn_conv_w': 'new_v', 'new_v_dn_a_log': 'new_v', 'new_v_dn_dt_bias': 'new_v', 'new_v_dn_norm_w': 'new_v', 'new_v_sg_ln_g': 'new_v', 'new_v_sg_ln_b': 'new_v', 'new_v_sg_w': 'new_v', 'new_v_sg_b': 'new_v', 'new_v_fox_f_bias': 'new_v', 'new_v_gate_b': 'new_v', 'new_v_w_branch': 'new_v', 'new_v_w_out': 'new_v', 'new_v_ln1_g': 'new_v', 'new_v_ln1_b': 'new_v', 'new_v_w_up': 'new_v', 'new_v_w_down': 'new_v', 'new_v_ln2_g': 'new_v', 'new_v_ln2_b': 'new_v'}


def _forward(args):
    return _fwd_reference(*[args[k] for k in FWD_PARAMS])


def _output_shape():
    out = _jax.eval_shape(lambda: _forward(_fwd_setup_inputs(0)))
    return out.shape, out.dtype

N_MICROBATCH = 1
ADAM_LR = 0.001
ADAM_B1 = 0.9
ADAM_B2 = 0.999
ADAM_EPS = 1e-08
ADAM_WD = 0.01
ADAM_STEP = 10
PER_EXAMPLE_BATCH_AXIS = {'x': 0, 'loss_target': 0}
SHARED_INPUTS = []
_WEIGHT_DTYPES = {'ln_in_g': _jnp.float32, 'ln_in_b': _jnp.float32, 'w_in': _jnp.float32, 'ssd_conv_w': _jnp.float32, 'ssd_conv_b': _jnp.float32, 'ssd_dt_bias': _jnp.float32, 'ssd_a_log': _jnp.float32, 'ssd_d': _jnp.float32, 'ssd_norm_w': _jnp.float32, 'dn_conv_w': _jnp.float32, 'dn_a_log': _jnp.float32, 'dn_dt_bias': _jnp.float32, 'dn_norm_w': _jnp.float32, 'sg_ln_g': _jnp.float32, 'sg_ln_b': _jnp.float32, 'sg_w': _jnp.float32, 'sg_b': _jnp.float32, 'fox_f_bias': _jnp.float32, 'gate_b': _jnp.float32, 'w_branch': _jnp.float32, 'w_out': _jnp.float32, 'ln1_g': _jnp.float32, 'ln1_b': _jnp.float32, 'w_up': _jnp.float32, 'w_down': _jnp.float32, 'ln2_g': _jnp.float32, 'ln2_b': _jnp.float32}
MOMENT_SCALE = {'ln_in_g': 5.862920e-01, 'ln_in_b': 4.507932e-01, 'w_in': 2.460584e-02, 'ssd_conv_w': 4.011579e-02, 'ssd_conv_b': 6.835344e-02, 'ssd_dt_bias': 1.339547e-01, 'ssd_a_log': 1.938144e-01, 'ssd_d': 5.146420e-01, 'ssd_norm_w': 5.423004e-02, 'dn_conv_w': 2.263289e-02, 'dn_a_log': 1.405578e-01, 'dn_dt_bias': 1.389086e-01, 'dn_norm_w': 7.391080e-02, 'sg_ln_g': 2.329099e-02, 'sg_ln_b': 2.292263e-02, 'sg_w': 2.272461e-02, 'sg_b': 3.229750e-02, 'fox_f_bias': 1.020501e-01, 'gate_b': 1.073754e-02, 'w_branch': 2.836361e-02, 'w_out': 1.123837e-01, 'ln1_g': 7.365419e-01, 'ln1_b': 4.800044e-01, 'w_up': 4.275206e-02, 'w_down': 1.952424e-01, 'ln2_g': 2.273233e+01, 'ln2_b': 5.478213e+00}


def _to_microbatches(a, axis):
    t = _jnp.moveaxis(a, axis, 0)
    t = t.reshape((N_MICROBATCH, t.shape[0] // N_MICROBATCH) + t.shape[1:])
    return _jnp.moveaxis(t, 1, axis + 1)


def setup_inputs(seed: int = 0) -> dict:
    inp = _fwd_setup_inputs(seed)
    key = _jax.random.fold_in(_jax.random.key(seed), 7919)
    shape, _ = _output_shape()
    out = dict(inp)
    out["loss_target"] = _jax.random.normal(_jax.random.fold_in(key, 0), shape, _jnp.float32)
    for i, name in enumerate(TWIN_WEIGHTS):
        w = inp[name].astype(_jnp.float32)
        if MOMENT_SCALE is None:
            s = _jnp.sqrt(_jnp.mean(_jnp.square(w)) + 1e-30)
        else:
            s = MOMENT_SCALE[name]
        km, kv = _jax.random.split(_jax.random.fold_in(key, i + 1))
        out[name] = w
        out["m_" + name] = s * _jax.random.normal(km, w.shape, _jnp.float32)
        out["v_" + name] = (s * s) * _jax.random.uniform(kv, w.shape, _jnp.float32, 0.5, 1.5)
    if N_MICROBATCH > 1:
        for name, axis in PER_EXAMPLE_BATCH_AXIS.items():
            out[name] = _to_microbatches(out[name], axis)
    return {'x': out['x'], 'ln_in_g': out['ln_in_g'], 'ln_in_b': out['ln_in_b'], 'w_in': out['w_in'], 'ssd_conv_w': out['ssd_conv_w'], 'ssd_conv_b': out['ssd_conv_b'], 'ssd_dt_bias': out['ssd_dt_bias'], 'ssd_a_log': out['ssd_a_log'], 'ssd_d': out['ssd_d'], 'ssd_norm_w': out['ssd_norm_w'], 'dn_conv_w': out['dn_conv_w'], 'dn_a_log': out['dn_a_log'], 'dn_dt_bias': out['dn_dt_bias'], 'dn_norm_w': out['dn_norm_w'], 'sg_ln_g': out['sg_ln_g'], 'sg_ln_b': out['sg_ln_b'], 'sg_w': out['sg_w'], 'sg_b': out['sg_b'], 'fox_f_bias': out['fox_f_bias'], 'gate_b': out['gate_b'], 'w_branch': out['w_branch'], 'w_out': out['w_out'], 'ln1_g': out['ln1_g'], 'ln1_b': out['ln1_b'], 'w_up': out['w_up'], 'w_down': out['w_down'], 'ln2_g': out['ln2_g'], 'ln2_b': out['ln2_b'], 'loss_target': out['loss_target'], 'm_ln_in_g': out['m_ln_in_g'], 'm_ln_in_b': out['m_ln_in_b'], 'm_w_in': out['m_w_in'], 'm_ssd_conv_w': out['m_ssd_conv_w'], 'm_ssd_conv_b': out['m_ssd_conv_b'], 'm_ssd_dt_bias': out['m_ssd_dt_bias'], 'm_ssd_a_log': out['m_ssd_a_log'], 'm_ssd_d': out['m_ssd_d'], 'm_ssd_norm_w': out['m_ssd_norm_w'], 'm_dn_conv_w': out['m_dn_conv_w'], 'm_dn_a_log': out['m_dn_a_log'], 'm_dn_dt_bias': out['m_dn_dt_bias'], 'm_dn_norm_w': out['m_dn_norm_w'], 'm_sg_ln_g': out['m_sg_ln_g'], 'm_sg_ln_b': out['m_sg_ln_b'], 'm_sg_w': out['m_sg_w'], 'm_sg_b': out['m_sg_b'], 'm_fox_f_bias': out['m_fox_f_bias'], 'm_gate_b': out['m_gate_b'], 'm_w_branch': out['m_w_branch'], 'm_w_out': out['m_w_out'], 'm_ln1_g': out['m_ln1_g'], 'm_ln1_b': out['m_ln1_b'], 'm_w_up': out['m_w_up'], 'm_w_down': out['m_w_down'], 'm_ln2_g': out['m_ln2_g'], 'm_ln2_b': out['m_ln2_b'], 'v_ln_in_g': out['v_ln_in_g'], 'v_ln_in_b': out['v_ln_in_b'], 'v_w_in': out['v_w_in'], 'v_ssd_conv_w': out['v_ssd_conv_w'], 'v_ssd_conv_b': out['v_ssd_conv_b'], 'v_ssd_dt_bias': out['v_ssd_dt_bias'], 'v_ssd_a_log': out['v_ssd_a_log'], 'v_ssd_d': out['v_ssd_d'], 'v_ssd_norm_w': out['v_ssd_norm_w'], 'v_dn_conv_w': out['v_dn_conv_w'], 'v_dn_a_log': out['v_dn_a_log'], 'v_dn_dt_bias': out['v_dn_dt_bias'], 'v_dn_norm_w': out['v_dn_norm_w'], 'v_sg_ln_g': out['v_sg_ln_g'], 'v_sg_ln_b': out['v_sg_ln_b'], 'v_sg_w': out['v_sg_w'], 'v_sg_b': out['v_sg_b'], 'v_fox_f_bias': out['v_fox_f_bias'], 'v_gate_b': out['v_gate_b'], 'v_w_branch': out['v_w_branch'], 'v_w_out': out['v_w_out'], 'v_ln1_g': out['v_ln1_g'], 'v_ln1_b': out['v_ln1_b'], 'v_w_up': out['v_w_up'], 'v_w_down': out['v_w_down'], 'v_ln2_g': out['v_ln2_g'], 'v_ln2_b': out['v_ln2_b']}


def _loss(weights, diff, rest, loss_target):
    with _jax.named_scope("forward"):
        args = {**rest, TWIN_DIFF_INPUT: diff, **{k: w.astype(_WEIGHT_DTYPES[k]) for k, w in weights.items()}}
        y = _forward(args)
    with _jax.named_scope("loss_head"):
        err = _jnp.square(y.astype(_jnp.float32) - loss_target)
        return 0.5 * _jnp.sum(_jnp.mean(err, axis=-1)) if err.ndim else 0.5 * err


def _adamw(w, g, m, v):
    m = ADAM_B1 * m + (1.0 - ADAM_B1) * g
    v = ADAM_B2 * v + (1.0 - ADAM_B2) * _jnp.square(g)
    m_hat = m / (1.0 - ADAM_B1 ** ADAM_STEP)
    v_hat = v / (1.0 - ADAM_B2 ** ADAM_STEP)
    delta = -ADAM_LR * (m_hat / (_jnp.sqrt(v_hat) + ADAM_EPS) + ADAM_WD * w)
    return delta, m, v


def reference(x, ln_in_g, ln_in_b, w_in, ssd_conv_w, ssd_conv_b, ssd_dt_bias, ssd_a_log, ssd_d, ssd_norm_w, dn_conv_w, dn_a_log, dn_dt_bias, dn_norm_w, sg_ln_g, sg_ln_b, sg_w, sg_b, fox_f_bias, gate_b, w_branch, w_out, ln1_g, ln1_b, w_up, w_down, ln2_g, ln2_b, loss_target, m_ln_in_g, m_ln_in_b, m_w_in, m_ssd_conv_w, m_ssd_conv_b, m_ssd_dt_bias, m_ssd_a_log, m_ssd_d, m_ssd_norm_w, m_dn_conv_w, m_dn_a_log, m_dn_dt_bias, m_dn_norm_w, m_sg_ln_g, m_sg_ln_b, m_sg_w, m_sg_b, m_fox_f_bias, m_gate_b, m_w_branch, m_w_out, m_ln1_g, m_ln1_b, m_w_up, m_w_down, m_ln2_g, m_ln2_b, v_ln_in_g, v_ln_in_b, v_w_in, v_ssd_conv_w, v_ssd_conv_b, v_ssd_dt_bias, v_ssd_a_log, v_ssd_d, v_ssd_norm_w, v_dn_conv_w, v_dn_a_log, v_dn_dt_bias, v_dn_norm_w, v_sg_ln_g, v_sg_ln_b, v_sg_w, v_sg_b, v_fox_f_bias, v_gate_b, v_w_branch, v_w_out, v_ln1_g, v_ln1_b, v_w_up, v_w_down, v_ln2_g, v_ln2_b):
    given = dict(x=x, ln_in_g=ln_in_g, ln_in_b=ln_in_b, w_in=w_in, ssd_conv_w=ssd_conv_w, ssd_conv_b=ssd_conv_b, ssd_dt_bias=ssd_dt_bias, ssd_a_log=ssd_a_log, ssd_d=ssd_d, ssd_norm_w=ssd_norm_w, dn_conv_w=dn_conv_w, dn_a_log=dn_a_log, dn_dt_bias=dn_dt_bias, dn_norm_w=dn_norm_w, sg_ln_g=sg_ln_g, sg_ln_b=sg_ln_b, sg_w=sg_w, sg_b=sg_b, fox_f_bias=fox_f_bias, gate_b=gate_b, w_branch=w_branch, w_out=w_out, ln1_g=ln1_g, ln1_b=ln1_b, w_up=w_up, w_down=w_down, ln2_g=ln2_g, ln2_b=ln2_b, loss_target=loss_target, m_ln_in_g=m_ln_in_g, m_ln_in_b=m_ln_in_b, m_w_in=m_w_in, m_ssd_conv_w=m_ssd_conv_w, m_ssd_conv_b=m_ssd_conv_b, m_ssd_dt_bias=m_ssd_dt_bias, m_ssd_a_log=m_ssd_a_log, m_ssd_d=m_ssd_d, m_ssd_norm_w=m_ssd_norm_w, m_dn_conv_w=m_dn_conv_w, m_dn_a_log=m_dn_a_log, m_dn_dt_bias=m_dn_dt_bias, m_dn_norm_w=m_dn_norm_w, m_sg_ln_g=m_sg_ln_g, m_sg_ln_b=m_sg_ln_b, m_sg_w=m_sg_w, m_sg_b=m_sg_b, m_fox_f_bias=m_fox_f_bias, m_gate_b=m_gate_b, m_w_branch=m_w_branch, m_w_out=m_w_out, m_ln1_g=m_ln1_g, m_ln1_b=m_ln1_b, m_w_up=m_w_up, m_w_down=m_w_down, m_ln2_g=m_ln2_g, m_ln2_b=m_ln2_b, v_ln_in_g=v_ln_in_g, v_ln_in_b=v_ln_in_b, v_w_in=v_w_in, v_ssd_conv_w=v_ssd_conv_w, v_ssd_conv_b=v_ssd_conv_b, v_ssd_dt_bias=v_ssd_dt_bias, v_ssd_a_log=v_ssd_a_log, v_ssd_d=v_ssd_d, v_ssd_norm_w=v_ssd_norm_w, v_dn_conv_w=v_dn_conv_w, v_dn_a_log=v_dn_a_log, v_dn_dt_bias=v_dn_dt_bias, v_dn_norm_w=v_dn_norm_w, v_sg_ln_g=v_sg_ln_g, v_sg_ln_b=v_sg_ln_b, v_sg_w=v_sg_w, v_sg_b=v_sg_b, v_fox_f_bias=v_fox_f_bias, v_gate_b=v_gate_b, v_w_branch=v_w_branch, v_w_out=v_w_out, v_ln1_g=v_ln1_g, v_ln1_b=v_ln1_b, v_w_up=v_w_up, v_w_down=v_w_down, v_ln2_g=v_ln2_g, v_ln2_b=v_ln2_b)
    weights = {n: given[n] for n in TWIN_WEIGHTS}
    shared = {n: given[n] for n in SHARED_INPUTS}
    per_example = {n: given[n] for n in ['x']}
    grad_fn = _jax.value_and_grad(_loss, argnums=(0, 1))

    def one_microbatch(ex, loss_target):
        ex = dict(ex)
        diff = ex.pop(TWIN_DIFF_INPUT)
        return grad_fn(weights, diff, {**shared, **ex}, loss_target)

    if N_MICROBATCH == 1:
        loss, (grad_w, grad_x) = one_microbatch(per_example, given["loss_target"])
    else:
        def body(carry, xs):
            loss_sum, grad_sum = carry
            l_k, (gw_k, gx_k) = one_microbatch(xs[0], xs[1])
            with _jax.named_scope("update"):
                return (loss_sum + l_k, _jax.tree.map(_jnp.add, grad_sum, gw_k)), gx_k

        init = (_jnp.zeros((), _jnp.float32), _jax.tree.map(_jnp.zeros_like, weights))
        (loss, grad_w), grad_x = _jax.lax.scan(body, init, (per_example, given["loss_target"]))
    with _jax.named_scope("update"):
        delta_w, new_m, new_v = {}, {}, {}
        for n in TWIN_WEIGHTS:
            delta_w[n], new_m[n], new_v[n] = _adamw(weights[n], grad_w[n], given["m_" + n], given["v_" + n])
    return (loss, grad_x, *[grad_w[n] for n in TWIN_WEIGHTS], *[delta_w[n] for n in TWIN_WEIGHTS],
            *[new_m[n] for n in TWIN_WEIGHTS], *[new_v[n] for n in TWIN_WEIGHTS])
```

```python
import functools
import math

import jax
import jax.numpy as jnp
from jax import lax
from jax.experimental import pallas as pl
from jax.experimental.pallas import tpu as pltpu

F32 = jnp.float32
BF = jnp.bfloat16

D_MODEL = 1024
DEPTH = 2
BRANCH_W = 512
D_FF = 4096
LN_EPS = 1e-5
NORM_EPS = 1e-6
ALPHA = (2 * DEPTH) ** 0.25
N_DEV = 8
LANES = 128
ADAM_LR, ADAM_B1, ADAM_B2, ADAM_EPS, ADAM_WD, ADAM_STEP = 0.001, 0.9, 0.999, 1e-08, 0.01, 10

DT0, BETA0, A0, FF0 = 0, 8, 12, 16
IN_SIZES = (512, 1024, 8, 1536, 4, 4, 512, 1024, 1536, 8, 4096)
_OFF = [0]
for _s in IN_SIZES:
    _OFF.append(_OFF[-1] + _s)
D_IN = _OFF[-1]
SEGS = (("z", 0), ("xbc", 1), ("dnqkv", 3), ("dngate", 6), ("sguv", 7), ("foxqkv", 8), ("gates", 10))
SMALL_SRC = ((2, DT0), (4, BETA0), (5, A0), (9, FF0))

NN = ((1,), (0,))
NT = ((1,), (1,))
TN = ((0,), (0,))
_DIMS = {"nn": NN, "nt": NT, "tn": TN}


def _pcall(body, **kw):
    return pl.pallas_call(body, **kw)


def _S(shape, dtype=F32):
    return jax.ShapeDtypeStruct(tuple(shape), dtype)


def _iota(shape, dim):
    return lax.broadcasted_iota(jnp.int32, shape, dim)


def _dotb(a, b, dims):
    return lax.dot_general(a, b, (dims, ((), ())), preferred_element_type=F32)


def _split2(a):
    ah = a.astype(BF)
    return ah, (a - ah.astype(F32)).astype(BF)


def _split3(a):
    a1 = a.astype(BF)
    r = a - a1.astype(F32)
    a2 = r.astype(BF)
    a3 = (r - a2.astype(F32)).astype(BF)
    return a1, a2, a3


def _mm_raw(a, b, form, mode):
    d = _DIMS[form]
    if mode == "1":
        return _dotb(a.astype(BF), b.astype(BF), d)
    if mode == "3":
        ah, al = _split2(a)
        bh, bl = _split2(b)
        return _dotb(ah, bh, d) + (_dotb(ah, bl, d) + _dotb(al, bh, d))
    if mode == "xa":
        ab = a.astype(BF)
        b1, b2, b3 = _split3(b)
        return _dotb(ab, b1, d) + (_dotb(ab, b2, d) + _dotb(ab, b3, d))
    bb = b.astype(BF)
    a1, a2, a3 = _split3(a)
    return _dotb(a1, bb, d) + (_dotb(a2, bb, d) + _dotb(a3, bb, d))


@functools.partial(jax.custom_vjp, nondiff_argnums=(2, 3))
def mm(a, b, form, mode):
    return _mm_raw(a, b, form, mode)


def _mm_fwd(a, b, form, mode):
    return _mm_raw(a, b, form, mode), (a, b)


_XA_DB = {"nn": "xa", "nt": "xb", "tn": "xa"}
_XB_DA = {"nn": "xb", "nt": "xb", "tn": "xa"}


def _mm_bwd(form, mode, res, g):
    a, b = res
    ma = _XB_DA[form] if mode == "xb" else mode
    mb = _XA_DB[form] if mode == "xa" else mode
    da = db = None
    if mode != "xa":
        da = {"nn": lambda: mm(g, b, "nt", ma), "nt": lambda: mm(g, b, "nn", ma), "tn": lambda: mm(b, g, "nt", ma)}[form]()
    if mode != "xb":
        db = {"nn": lambda: mm(a, g, "tn", mb), "nt": lambda: mm(g, a, "tn", mb), "tn": lambda: mm(a, g, "nn", mb)}[form]()
    if da is None:
        da = jnp.zeros_like(a)
    if db is None:
        db = jnp.zeros_like(b)
    return da, db


mm.defvjp(_mm_fwd, _mm_bwd)


def _silu(x):
    return x * jax.nn.sigmoid(x)


def _ln(x, g, b):
    mu = jnp.mean(x, -1, keepdims=True)
    xc = x - mu
    var = jnp.mean(xc * xc, -1, keepdims=True)
    return xc * lax.rsqrt(var + LN_EPS) * g + b


def _pick(n, cap):
    if n <= cap:
        return n
    best = LANES
    for t in range(LANES, cap + 1, LANES):
        if n % t == 0:
            best = t
    return best


def matmul_w(a, w, l, form, name, add=None, add_scale=1.0):
    M, K = a.shape
    N = w.shape[2] if form == "nn" else w.shape[1]
    tm, tn, tk = min(M, 512), _pick(N, 1024), _pick(K, 1024)
    nk = K // tk

    def body(*refs):
        if add is None:
            a_ref, b_ref, o_ref, acc = refs
        else:
            a_ref, b_ref, d_ref, o_ref, acc = refs
        k = pl.program_id(2)
        p = _dotb(a_ref[...].astype(BF), b_ref[...].astype(BF), _DIMS[form])

        @pl.when(k == 0)
        def _():
            acc[...] = p

        @pl.when(k > 0)
        def _():
            acc[...] += p

        @pl.when(k == nk - 1)
        def _():
            r = acc[...]
            if add is not None:
                r = r + add_scale * d_ref[...]
            o_ref[...] = r

    if form == "nn":
        wspec = pl.BlockSpec((None, tk, tn), lambda j, i, k: (l, k, j))
    else:
        wspec = pl.BlockSpec((None, tn, tk), lambda j, i, k: (l, j, k))
    in_specs = [pl.BlockSpec((tm, tk), lambda j, i, k: (i, k)), wspec]
    args = [a, w]
    if add is not None:
        in_specs.append(pl.BlockSpec((tm, tn), lambda j, i, k: (i, j)))
        args.append(add)
    return _pcall(body, grid=(N // tn, M // tm, nk), in_specs=in_specs,
                  out_specs=pl.BlockSpec((tm, tn), lambda j, i, k: (i, j)), out_shape=_S((M, N)),
                  scratch_shapes=[pltpu.VMEM((tm, tn), F32)], name=name)(*args)


def matmul_tn(a, b, name, b_col0=0, n_cols=None):
    T, M = a.shape
    N = b.shape[1] if n_cols is None else n_cols
    tm, tn, tt = _pick(M, 512), _pick(N, 1024), min(T, 512)
    nt = T // tt
    jb = b_col0 // tn

    def body(a_ref, b_ref, o_ref, acc):
        t = pl.program_id(2)
        p = _dotb(a_ref[...].astype(BF), b_ref[...].astype(BF), TN)

        @pl.when(t == 0)
        def _():
            acc[...] = p

        @pl.when(t > 0)
        def _():
            acc[...] += p

        @pl.when(t == nt - 1)
        def _():
            o_ref[...] = acc[...]

    return _pcall(body, grid=(M // tm, N // tn, nt),
                  in_specs=[pl.BlockSpec((tt, tm), lambda i, j, t: (t, i)), pl.BlockSpec((tt, tn), lambda i, j, t: (t, jb + j))],
                  out_specs=pl.BlockSpec((tm, tn), lambda i, j, t: (i, j)), out_shape=_S((M, N)),
                  scratch_shapes=[pltpu.VMEM((tm, tn), F32)], name=name)(a, b)


def _pieces(v):
    if v.ndim == 3:
        return [v[i] for i in range(v.shape[0])]
    n = v.shape[1] // LANES
    if n <= 1:
        return [v]
    return [v[:, i * LANES:(i + 1) * LANES] for i in range(n)]


def _join(ps, like_ndim):
    if like_ndim == 3:
        return jnp.stack(ps, axis=0)
    return ps[0] if len(ps) == 1 else jnp.concatenate(ps, axis=1)


def scan_fwd(name, f, xs, ps, ys, state_shape, nc, nh=1):
    nx, npar, ny = len(xs), len(ps), len(ys)

    def body(*refs):
        x_refs, p_refs = refs[:nx], refs[nx:nx + npar]
        y_refs = refs[nx + npar:nx + npar + ny]
        st_out, st = refs[nx + npar + ny], refs[nx + npar + ny + 1]
        c, h = pl.program_id(0), pl.program_id(1)

        @pl.when(c == 0)
        def _():
            st[h] = jnp.zeros(state_shape, F32)

        S = st[h]
        st_out[...] = S
        yv, Sn = f([_pieces(r[...]) for r in x_refs], [_pieces(r[...]) for r in p_refs], _pieces(S), h)
        for r, v in zip(y_refs, yv):
            r[...] = _join(v, 2)
        st[h] = _join(Sn, 3)

    in_specs = [pl.BlockSpec(bs, im) for (_, bs, im) in xs]
    in_specs += [pl.BlockSpec(p.shape, (lambda c, h, n=p.ndim: (0,) * n)) for p in ps]
    out_specs = [pl.BlockSpec(bs, im) for (_, bs, im) in ys]
    out_specs.append(pl.BlockSpec((None, None) + tuple(state_shape), lambda c, h: (c, h, 0, 0, 0)))
    out_shape = [_S(s) for (s, _, _) in ys] + [_S((nc, nh) + tuple(state_shape))]
    return _pcall(body, grid=(nc, nh), in_specs=in_specs, out_specs=out_specs, out_shape=out_shape,
                  scratch_shapes=[pltpu.VMEM((nh,) + tuple(state_shape), F32)], name=name)(*[x[0] for x in xs], *ps)


def scan_bwd(name, f, xs, ps, ys, dys, states, state_shape, nc, nh=1, shared=()):
    nx, npar, ny = len(xs), len(ps), len(ys)

    def body(*refs):
        x_refs, p_refs = refs[:nx], refs[nx:nx + npar]
        s_ref = refs[nx + npar]
        dy_refs = refs[nx + npar + 1:nx + npar + 1 + ny]
        o = nx + npar + 1 + ny
        dx_refs, dp_refs, dst = refs[o:o + nx], refs[o + nx:o + nx + npar], refs[o + nx + npar]
        c, h = pl.program_id(0), pl.program_id(1)

        @pl.when(c == 0)
        def _():
            dst[h] = jnp.zeros(state_shape, F32)

        @pl.when((c == 0) & (h == 0))
        def _():
            for r in dp_refs:
                r[...] = jnp.zeros(r.shape, F32)

        xv = [_pieces(r[...]) for r in x_refs]
        pv = [_pieces(r[...]) for r in p_refs]
        _, vjp = jax.vjp(lambda a, b, s: f(a, b, s, h), xv, pv, _pieces(s_ref[...]))
        dxv, dpv, dS = vjp(([_pieces(r[...]) for r in dy_refs], _pieces(dst[h])))
        for i, (r, v) in enumerate(zip(dx_refs, dxv)):
            if i in shared and nh > 1:
                @pl.when(h == 0)
                def _(r=r, v=v):
                    r[...] = _join(v, 2)

                @pl.when(h > 0)
                def _(r=r, v=v):
                    r[...] += _join(v, 2)
            else:
                r[...] = _join(v, 2)
        for r, v in zip(dp_refs, dpv):
            r[...] += _join(v, len(r.shape))
        dst[h] = _join(dS, 3)

    def rev(im):
        return lambda c, h: im(nc - 1 - c, h)

    in_specs = [pl.BlockSpec(bs, rev(im)) for (_, bs, im) in xs]
    in_specs += [pl.BlockSpec(p.shape, (lambda c, h, n=p.ndim: (0,) * n)) for p in ps]
    in_specs.append(pl.BlockSpec((None, None) + tuple(state_shape), lambda c, h: (nc - 1 - c, h, 0, 0, 0)))
    in_specs += [pl.BlockSpec(bs, rev(im)) for (_, bs, im) in ys]
    out_specs = [pl.BlockSpec(bs, rev(im)) for (_, bs, im) in xs]
    out_specs += [pl.BlockSpec(p.shape, (lambda c, h, n=p.ndim: (0,) * n)) for p in ps]
    out_shape = [_S(x[0].shape) for x in xs] + [_S(p.shape) for p in ps]
    return _pcall(body, grid=(nc, nh), in_specs=in_specs, out_specs=out_specs, out_shape=out_shape,
                  scratch_shapes=[pltpu.VMEM((nh,) + tuple(state_shape), F32)], name=name)(
                      *[x[0] for x in xs], *ps, states, *dys)


def _lane():
    return _iota((1, LANES), 1)


def _col(v, idx):
    return jnp.sum(v * (_lane() == idx).astype(F32), axis=1, keepdims=True)


def _last_row(v):
    r = v.shape[0]
    return jnp.sum(v * (_iota((r, 1), 0) == r - 1).astype(F32), axis=0, keepdims=True)


def _tril(n, strict=False):
    r, c = _iota((n, n), 0), _iota((n, n), 1)
    return (r > c) if strict else (r >= c)


def ssd_chunk(xs, ps, S, h):
    zp, xbc, (sm,) = xs
    (bias,), (alog,), (dsk,), nw = ps
    Q = sm.shape[0]
    lane = _lane()
    a128 = jnp.where(lane < 8, -jnp.exp(alog), 0.0)
    dtl = jax.nn.softplus(sm + bias)
    tri = _tril(Q)
    trif = tri.astype(F32)
    cum = mm(trif, dtl * a128, "nn", "xa")
    sel8 = (_iota((8, LANES), 0) == _iota((8, LANES), 1)).astype(F32)
    cum_t = mm(sel8, cum, "nt", "xa")
    m0 = (lane < 64).astype(F32)
    rows0 = (_iota((LANES, 1), 0) < 64).astype(F32)
    yz, Sn, ssq, cb = [], [], 0.0, None
    for pr in range(4):
        g = pr // 2
        Bg, Cg = xbc[4 + g], xbc[6 + g]
        if pr % 2 == 0:
            cb = mm(Cg, Bg, "nt", "1")
        xp, Hp = xbc[pr], S[pr]
        ypair, stpair, rowdec = 0.0, 0.0, 0.0
        for e in range(2):
            hh = 2 * pr + e
            me = m0 if e == 0 else 1.0 - m0
            re = rows0 if e == 0 else 1.0 - rows0
            col = _col(cum, hh)
            row = jnp.sum(cum_t * (_iota((8, 1), 0) == hh).astype(F32), axis=0, keepdims=True)
            xh = xp * me
            xdt = xh * _col(dtl, hh)
            seg = jnp.exp(jnp.where(tri, col - row, -jnp.inf))
            last = _last_row(col)
            ypair = ypair + mm(cb * seg, xdt, "nn", "1") + mm(Cg * jnp.exp(col), Hp, "nt", "1") * me + _col(dsk, hh) * xh
            stpair = stpair + mm(xdt, Bg * jnp.exp(last - col), "tn", "1")
            rowdec = rowdec + jnp.exp(last) * re
        Sn.append(Hp * rowdec + stpair)
        v = ypair * _silu(zp[pr])
        ssq = ssq + jnp.sum(v * v, axis=1, keepdims=True)
        yz.append(v)
    scale = lax.rsqrt(ssq / BRANCH_W + NORM_EPS)
    return [[yz[i] * scale * nw[i] for i in range(4)]], Sn


def dn_chunk(xs, ps, S, h):
    (q,), (k,), (v,), (gate,), (sm,) = xs
    (alog,), (dtb,), (nw,) = ps
    (S0,) = S
    C = q.shape[0]
    lane = _lane()
    qn = q * lax.rsqrt(jnp.sum(q * q, axis=1, keepdims=True) + NORM_EPS) * (LANES ** -0.5)
    kn = k * lax.rsqrt(jnp.sum(k * k, axis=1, keepdims=True) + NORM_EPS)
    beta = _col(jax.nn.sigmoid(sm), BETA0 + h)
    G = jnp.where((lane >= A0) & (lane < A0 + 4), -jnp.exp(alog) * jax.nn.softplus(sm + dtb), 0.0)
    tri = _tril(C)
    gcs = mm(tri.astype(F32), G, "nn", "xa")
    gcol = _col(gcs, A0 + h)
    selr = ((_iota((8, LANES), 0) == 0) & (_iota((8, LANES), 1) == A0 + h)).astype(F32)
    grow = jnp.sum(mm(selr, gcs, "nt", "xa"), axis=0, keepdims=True)
    gamma = jnp.exp(jnp.where(tri, gcol - grow, -jnp.inf))
    kb = kn * beta
    P = -(mm(kb, kn, "nt", "1") * jnp.where(_tril(C, True), gamma, 0.0))
    minv = (_iota((C, C), 0) == _iota((C, C), 1)).astype(F32) + P
    pk = P
    for _ in range(5):
        pk = mm(pk, pk, "nn", "3")
        minv = minv + mm(minv, pk, "nn", "3")
    eg = jnp.exp(gcol)
    w = mm(minv, kb * eg, "nn", "3")
    u = mm(minv, v * beta, "nn", "3")
    glast = _last_row(gcol)
    vnew = u - mm(w, S0, "nn", "1")
    o = mm(qn * eg, S0, "nn", "1") + mm(mm(qn, kn, "nt", "1") * gamma, vnew, "nn", "1")
    Sn = S0 * jnp.exp(glast) + mm(kn * jnp.exp(glast - gcol), vnew, "tn", "1")
    on = o * lax.rsqrt(jnp.mean(o * o, axis=1, keepdims=True) + NORM_EPS) * nw
    return [[on * _silu(gate)]], [Sn]


def sg_chunk(xs, ps, S, h):
    (uv,) = xs
    lng, lnb, W, (bt,) = ps
    u = [jax.nn.gelu(p) for p in uv[:4]]
    v = [jax.nn.gelu(p) for p in uv[4:]]
    mu = sum(jnp.sum(p, axis=1, keepdims=True) for p in v) / BRANCH_W
    vc = [p - mu for p in v]
    var = sum(jnp.sum(p * p, axis=1, keepdims=True) for p in vc) / BRANCH_W
    inv = lax.rsqrt(var + LN_EPS)
    trif = _tril(W[0].shape[0]).astype(F32)
    out = []
    for g in range(4):
        vn = vc[g] * inv * lng[g] + lnb[g]
        out.append(u[g] * (mm(W[g] * trif, vn, "nn", "1") + _col(bt, g)))
    return [out], S


def foxc_chunk(xs, ps, S, h):
    (sm,), ((fb,),), (carry,) = xs[0], ps, S
    lane = _lane()
    ls = jnp.where((lane >= FF0) & (lane < FF0 + 8), jax.nn.log_sigmoid(sm + fb), 0.0)
    Q = sm.shape[0]
    c = mm(_tril(Q).astype(F32), ls, "nn", "xa") + carry
    sel = (_iota((8, LANES), 1) == _iota((8, LANES), 0) + FF0).astype(F32)
    return [[c], [mm(sel, c, "nt", "xa")]], [_last_row(c)]


def _conv_tiles(T, C):
    return min(T, 512), _pick(C, 512)


def conv_fwd(x, w, b, name):
    T, C = x.shape
    tm, cb = _conv_tiles(T, C)

    def body(xp_ref, x_ref, w_ref, b_ref, o_ref):
        i = pl.program_id(1)
        tail = xp_ref[tm - 8:tm, :] * (i > 0).astype(F32)
        e = jnp.concatenate([tail, x_ref[...]], axis=0)
        pre = b_ref[...] + sum(w_ref[k:k + 1, :] * e[5 + k:5 + k + tm, :] for k in range(4))
        o_ref[...] = _silu(pre)

    return _pcall(body, grid=(C // cb, T // tm),
                  in_specs=[pl.BlockSpec((tm, cb), lambda j, i: (jnp.maximum(i - 1, 0), j)), pl.BlockSpec((tm, cb), lambda j, i: (i, j)),
                            pl.BlockSpec((4, cb), lambda j, i: (0, j)), pl.BlockSpec((1, cb), lambda j, i: (0, j))],
                  out_specs=pl.BlockSpec((tm, cb), lambda j, i: (i, j)), out_shape=_S((T, C)), name=name)(x, x, w, b)


def conv_bwd(x, w, b, dact, name):
    T, C = x.shape
    tm, cb = _conv_tiles(T, C)
    nt = T // tm

    def body(xp_ref, x_ref, xn_ref, w_ref, b_ref, d_ref, dn_ref, dx_ref, dw_ref, db_ref):
        i = pl.program_id(1)
        has_prev, has_next = (i > 0).astype(F32), (i < nt - 1).astype(F32)
        e = jnp.concatenate([xp_ref[tm - 8:tm, :] * has_prev, x_ref[...], xn_ref[0:8, :] * has_next], axis=0)
        pre = b_ref[...] + sum(w_ref[k:k + 1, :] * e[5 + k:5 + k + tm + 8, :] for k in range(4))
        de = jnp.concatenate([d_ref[...], dn_ref[0:8, :] * has_next], axis=0)
        sg = jax.nn.sigmoid(pre)
        dpre = de * (sg * (1.0 + pre * (1.0 - sg)))
        dx_ref[...] = sum(w_ref[k:k + 1, :] * dpre[3 - k:3 - k + tm, :] for k in range(4))
        dcur = dpre[0:tm, :]
        dw = jnp.concatenate([jnp.sum(dcur * e[5 + k:5 + k + tm, :], axis=0, keepdims=True) for k in range(4)], axis=0)
        db = jnp.sum(dcur, axis=0, keepdims=True)

        @pl.when(i == 0)
        def _():
            dw_ref[...] = dw
            db_ref[...] = db

        @pl.when(i > 0)
        def _():
            dw_ref[...] += dw
            db_ref[...] += db

    blk = lambda f: pl.BlockSpec((tm, cb), f)
    return _pcall(body, grid=(C // cb, nt),
                  in_specs=[blk(lambda j, i: (jnp.maximum(i - 1, 0), j)), blk(lambda j, i: (i, j)), blk(lambda j, i: (jnp.minimum(i + 1, nt - 1), j)),
                            pl.BlockSpec((4, cb), lambda j, i: (0, j)), pl.BlockSpec((1, cb), lambda j, i: (0, j)),
                            blk(lambda j, i: (i, j)), blk(lambda j, i: (jnp.minimum(i + 1, nt - 1), j))],
                  out_specs=[blk(lambda j, i: (i, j)), pl.BlockSpec((4, cb), lambda j, i: (0, j)), pl.BlockSpec((1, cb), lambda j, i: (0, j))],
                  out_shape=[_S((T, C)), _S((4, C)), _S((1, C))], name=name)(x, x, x, w, b, dact, dact)


FOX_SCALE = 64 ** -0.5


def _fox_scores(q, k, ccol, crow, hp, e, i, j, tq):
    lane = _lane()
    me = ((lane < 64) if e == 0 else (lane >= 64)).astype(F32)
    hh = 2 * hp + e
    cq = jnp.sum(ccol * (lane == FF0 + hh).astype(F32), axis=1, keepdims=True)
    ck = jnp.sum(crow * (_iota((8, 1), 0) == hh).astype(F32), axis=0, keepdims=True)
    s = _dotb((q * me).astype(BF), k.astype(BF), NT) * FOX_SCALE + cq - ck
    ok = (i * tq + _iota((tq, tq), 0)) >= (j * tq + _iota((tq, tq), 1))
    return jnp.where(ok, s, -jnp.inf), me, hh


def fox_fwd(qkv, ccol, crow, name):
    T = qkv.shape[0]
    tq = min(T, 512)
    nq = T // tq

    def body(q_ref, k_ref, v_ref, cc_ref, cr_ref, o_ref, lse_ref, m_s, l_s, acc):
        hp, i, j = pl.program_id(0), pl.program_id(1), pl.program_id(2)

        @pl.when(j == 0)
        def _():
            m_s[...] = jnp.full(m_s.shape, -jnp.inf, F32)
            l_s[...] = jnp.zeros(l_s.shape, F32)
            acc[...] = jnp.zeros(acc.shape, F32)

        @pl.when(j <= i)
        def _():
            q, k, v = q_ref[...], k_ref[...], v_ref[...]
            a = acc[...]
            scale, add = 0.0, 0.0
            for e in range(2):
                s, me, _ = _fox_scores(q, k, cc_ref[...], cr_ref[...], hp, e, i, j, tq)
                m_old = m_s[e]
                m_new = jnp.maximum(m_old, jnp.max(s, axis=1, keepdims=True))
                p = jnp.exp(s - m_new)
                al = jnp.exp(m_old - m_new)
                l_s[e] = al * l_s[e] + jnp.sum(p, axis=1, keepdims=True)
                m_s[e] = m_new
                scale = scale + al * me
                add = add + _dotb(p.astype(BF), (v * me).astype(BF), NN)
            acc[...] = a * scale + add

        @pl.when(j == i)
        def _():
            lane = _lane()
            m0 = (lane < 64).astype(F32)
            o_ref[...] = acc[...] / (l_s[0] * m0 + l_s[1] * (1.0 - m0))
            lse_ref[...] = (m_s[0] + jnp.log(l_s[0])) * (lane == 0).astype(F32) + (m_s[1] + jnp.log(l_s[1])) * (lane == 1).astype(F32)

    qspec = lambda off: pl.BlockSpec((tq, LANES), lambda hp, i, j: (i, off + hp))
    kspec = lambda off: pl.BlockSpec((tq, LANES), lambda hp, i, j: (jnp.minimum(j, i), off + hp))
    return _pcall(body, grid=(4, nq, nq),
                  in_specs=[qspec(0), kspec(4), kspec(8), pl.BlockSpec((tq, LANES), lambda hp, i, j: (i, 0)),
                            pl.BlockSpec((8, tq), lambda hp, i, j: (0, jnp.minimum(j, i)))],
                  out_specs=[pl.BlockSpec((tq, LANES), lambda hp, i, j: (i, hp)), pl.BlockSpec((None, tq, LANES), lambda hp, i, j: (hp, i, 0))],
                  out_shape=[_S((T, BRANCH_W)), _S((4, T, LANES))],
                  scratch_shapes=[pltpu.VMEM((2, tq, 1), F32), pltpu.VMEM((2, tq, 1), F32), pltpu.VMEM((tq, LANES), F32)], name=name)(
                      qkv, qkv, qkv, ccol, crow)


def fox_bwd(qkv, ccol, crow, o, lse, do, name):
    T = qkv.shape[0]
    tq = min(T, 512)
    nq = T // tq

    def body(q_ref, k_ref, v_ref, cc_ref, cr_ref, o_ref, lse_ref, do_ref, dq_ref, dk_ref, dv_ref, dcc_ref, dcr_ref):
        hp, j, ii = pl.program_id(0), pl.program_id(1), pl.program_id(2)
        i = jnp.maximum(ii, j)

        @pl.when((hp == 0) & (j == 0) & (ii == 0))
        def _():
            dcc_ref[...] = jnp.zeros(dcc_ref.shape, F32)
            dcr_ref[...] = jnp.zeros(dcr_ref.shape, F32)

        @pl.when((j == 0) & (ii == 0))
        def _():
            dq_ref[...] = jnp.zeros(dq_ref.shape, F32)

        @pl.when(ii == 0)
        def _():
            dk_ref[...] = jnp.zeros(dk_ref.shape, F32)
            dv_ref[...] = jnp.zeros(dv_ref.shape, F32)

        @pl.when(ii >= j)
        def _():
            q, k, v, dov = q_ref[...], k_ref[...], v_ref[...], do_ref[...]
            lane = _lane()
            dq, dk, dv, dcc, dcr = 0.0, 0.0, 0.0, 0.0, 0.0
            dd = dov * o_ref[...]
            for e in range(2):
                s, me, hh = _fox_scores(q, k, cc_ref[...], cr_ref[...], hp, e, i, j, tq)
                lse_e = jnp.sum(lse_ref[...] * (lane == e).astype(F32), axis=1, keepdims=True)
                p = jnp.exp(s - lse_e)
                dom = (dov * me).astype(BF)
                dp = _dotb(dom, v.astype(BF), NT)
                ds = p * (dp - jnp.sum(dd * me, axis=1, keepdims=True))
                dsb = ds.astype(BF)
                dv = dv + _dotb(p.astype(BF), dom, TN)
                dk = dk + _dotb(dsb, (q * me).astype(BF), TN) * FOX_SCALE
                dq = dq + _dotb(dsb, (k * me).astype(BF), NN) * FOX_SCALE
                dcc = dcc + jnp.sum(ds, axis=1, keepdims=True) * (lane == FF0 + hh).astype(F32)
                dcr = dcr - jnp.sum(ds, axis=0, keepdims=True) * (_iota((8, 1), 0) == hh).astype(F32)
            rows = pl.ds(pl.multiple_of(i * tq, tq), tq)
            dq_ref[rows, :] += dq
            dcc_ref[rows, :] += dcc
            dk_ref[...] += dk
            dv_ref[...] += dv
            dcr_ref[:, pl.ds(pl.multiple_of(j * tq, tq), tq)] += dcr

    ispec = lambda off: pl.BlockSpec((tq, LANES), lambda hp, j, ii: (jnp.maximum(ii, j), off + hp))
    jspec = lambda off: pl.BlockSpec((tq, LANES), lambda hp, j, ii: (j, off + hp))
    return _pcall(body, grid=(4, nq, nq),
                  in_specs=[ispec(0), jspec(4), jspec(8), pl.BlockSpec((tq, LANES), lambda hp, j, ii: (jnp.maximum(ii, j), 0)),
                            pl.BlockSpec((8, tq), lambda hp, j, ii: (0, j)), ispec(0),
                            pl.BlockSpec((None, tq, LANES), lambda hp, j, ii: (hp, jnp.maximum(ii, j), 0)), ispec(0)],
                  out_specs=[pl.BlockSpec((T, LANES), lambda hp, j, ii: (0, hp)), jspec(0), jspec(0),
                             pl.BlockSpec((T, LANES), lambda hp, j, ii: (0, 0)), pl.BlockSpec((8, T), lambda hp, j, ii: (0, 0))],
                  out_shape=[_S((T, BRANCH_W)), _S((T, BRANCH_W)), _S((T, BRANCH_W)), _S((T, LANES)), _S((8, T))], name=name)(
                      qkv, qkv, qkv, ccol, crow, o, lse, do)


def _acc_out(ref, val, first):
    @pl.when(first)
    def _():
        ref[...] = val

    @pl.when(jnp.logical_not(first))
    def _():
        ref[...] += val


def _row(tm, c):
    return pl.BlockSpec((tm, c), lambda i: (i, 0))


def _full(shape):
    return pl.BlockSpec(shape, lambda *_: (0,) * len(shape))


def ln_fwd(x, g, b, name):
    T, C = x.shape
    tm = min(T, 512)

    def body(x_ref, g_ref, b_ref, o_ref):
        o_ref[...] = _ln(x_ref[...], g_ref[...], b_ref[...])

    return _pcall(body, grid=(T // tm,), in_specs=[_row(tm, C), _full((1, C)), _full((1, C))], out_specs=_row(tm, C),
                  out_shape=_S((T, C)), name=name)(x, g, b)


def ln_bwd(x, g, b, dy, name):
    T, C = x.shape
    tm = min(T, 512)

    def body(x_ref, g_ref, b_ref, dy_ref, dx_ref, dg_ref, db_ref):
        _, vjp = jax.vjp(_ln, x_ref[...], g_ref[...], b_ref[...])
        dx, dg, db = vjp(dy_ref[...])
        dx_ref[...] = dx
        first = pl.program_id(0) == 0
        _acc_out(dg_ref, dg, first)
        _acc_out(db_ref, db, first)

    return _pcall(body, grid=(T // tm,), in_specs=[_row(tm, C), _full((1, C)), _full((1, C)), _row(tm, C)],
                  out_specs=[_row(tm, C), _full((1, C)), _full((1, C))], out_shape=[_S((T, C)), _S((1, C)), _S((1, C))], name=name)(x, g, b, dy)


def loss_head(h, target, name):
    T, C = h.shape
    tm = min(T, 512)

    def body(h_ref, t_ref, d_ref, l_ref):
        e = h_ref[...] - t_ref[...]
        d_ref[...] = e * (1.0 / C)
        part = jnp.sum(jnp.sum(e * e, axis=1, keepdims=True), axis=0, keepdims=True) * (0.5 / C)
        _acc_out(l_ref, part, pl.program_id(0) == 0)

    return _pcall(body, grid=(T // tm,), in_specs=[_row(tm, C), _row(tm, C)], out_specs=[_row(tm, C), _full((1, 1))],
                  out_shape=[_S((T, C)), _S((1, 1))], name=name)(h, target)


def add3(a, b, c, name):
    T, C = a.shape
    tm = min(T, 512)

    def body(a_ref, b_ref, c_ref, o_ref):
        o_ref[...] = a_ref[...] + b_ref[...] + c_ref[...]

    return _pcall(body, grid=(T // tm,), in_specs=[_row(tm, C)] * 3, out_specs=_row(tm, C), out_shape=_S((T, C)), name=name)(a, b, c)


def _wb_spec(l):
    return pl.BlockSpec((None, 4, BRANCH_W, D_MODEL), lambda *_: (l, 0, 0, 0))


def merge_fwd(ys, gl, gb, wb, l, name):
    T = gl.shape[0]
    tm = min(T, 256)

    def body(y0, y1, y2, y3, gl_ref, gb_ref, wb_ref, o_ref):
        acc = 0.0
        for i, y in enumerate((y0, y1, y2, y3)):
            z = _dotb(y[...].astype(BF), wb_ref[i], NN)
            g = jax.nn.sigmoid(gl_ref[:, i * D_MODEL:(i + 1) * D_MODEL] + gb_ref[i:i + 1, :])
            acc = acc + g * z
        o_ref[...] = acc

    return _pcall(body, grid=(T // tm,), in_specs=[_row(tm, BRANCH_W)] * 4 + [_row(tm, 4 * D_MODEL), _full((4, D_MODEL)), _wb_spec(l)],
                  out_specs=_row(tm, D_MODEL), out_shape=_S((T, D_MODEL)), name=name)(*ys, gl, gb, wb)


def merge_bwd(ys, gl, gb, wb, l, dm, name):
    T = gl.shape[0]
    tm = min(T, 256)

    def body(y0, y1, y2, y3, gl_ref, gb_ref, wb_ref, dm_ref, d0, d1, d2, d3, dgl_ref, dz_ref, dgb_ref):
        dmv = dm_ref[...]
        first = pl.program_id(0) == 0
        for i, (y, d) in enumerate(zip((y0, y1, y2, y3), (d0, d1, d2, d3))):
            cols = slice(i * D_MODEL, (i + 1) * D_MODEL)
            z = _dotb(y[...].astype(BF), wb_ref[i], NN)
            g = jax.nn.sigmoid(gl_ref[:, cols] + gb_ref[i:i + 1, :])
            dgl = dmv * z * (g * (1.0 - g))
            dz = (g * dmv).astype(BF)
            dgl_ref[:, cols] = dgl
            dz_ref[:, cols] = dz
            d[...] = _dotb(dz, wb_ref[i], NT)
            _acc_out(dgb_ref.at[i:i + 1, :], jnp.sum(dgl, axis=0, keepdims=True), first)

    return _pcall(body, grid=(T // tm,),
                  in_specs=[_row(tm, BRANCH_W)] * 4 + [_row(tm, 4 * D_MODEL), _full((4, D_MODEL)), _wb_spec(l), _row(tm, D_MODEL)],
                  out_specs=[_row(tm, BRANCH_W)] * 4 + [_row(tm, 4 * D_MODEL), _row(tm, 4 * D_MODEL), _full((4, D_MODEL))],
                  out_shape=[_S((T, BRANCH_W))] * 4 + [_S((T, 4 * D_MODEL)), _S((T, 4 * D_MODEL), BF), _S((4, D_MODEL))], name=name)(
                      *ys, gl, gb, wb, dm)


def _wout_spec(l):
    return pl.BlockSpec((None, D_MODEL, D_MODEL), lambda *_: (l, 0, 0))


def out_fwd(merged, h, wout, l, g, b, name):
    T = h.shape[0]
    tm = min(T, 512)

    def body(m_ref, h_ref, w_ref, g_ref, b_ref, u_ref, o_ref):
        u = ALPHA * h_ref[...] + _dotb(m_ref[...].astype(BF), w_ref[...], NN)
        u_ref[...] = u
        o_ref[...] = _ln(u, g_ref[...], b_ref[...])

    C = D_MODEL
    return _pcall(body, grid=(T // tm,), in_specs=[_row(tm, C), _row(tm, C), _wout_spec(l), _full((1, C)), _full((1, C))],
                  out_specs=[_row(tm, C), _row(tm, C)], out_shape=[_S((T, C)), _S((T, C))], name=name)(merged, h, wout, g, b)


def out_bwd(u, dy, g, b, wout, l, name):
    T, C = u.shape
    tm = min(T, 512)

    def body(u_ref, dy_ref, g_ref, b_ref, w_ref, du_ref, dm_ref, dg_ref, db_ref):
        _, vjp = jax.vjp(_ln, u_ref[...], g_ref[...], b_ref[...])
        du, dg, db = vjp(dy_ref[...])
        du_ref[...] = du
        dm_ref[...] = _dotb(du.astype(BF), w_ref[...], NT)
        first = pl.program_id(0) == 0
        _acc_out(dg_ref, dg, first)
        _acc_out(db_ref, db, first)

    return _pcall(body, grid=(T // tm,), in_specs=[_row(tm, C), _row(tm, C), _full((1, C)), _full((1, C)), _wout_spec(l)],
                  out_specs=[_row(tm, C), _row(tm, C), _full((1, C)), _full((1, C))],
                  out_shape=[_S((T, C)), _S((T, C)), _S((1, C)), _S((1, C))], name=name)(u, dy, g, b, wout)


def ff_fwd(h, wup, wdown, l, g, b, name):
    T, C = h.shape
    F = wup.shape[2]
    tm, tf = min(T, 512), 1024
    nf = F // tf

    def body(h_ref, wu_ref, wd_ref, g_ref, b_ref, u_ref, o_ref, acc):
        f = pl.program_id(1)
        a = _dotb(h_ref[...].astype(BF), wu_ref[...], NN)
        r = jnp.square(jnp.maximum(a, 0.0))
        p = _dotb(r.astype(BF), wd_ref[...], NN)
        _acc_out(acc, p, f == 0)

        @pl.when(f == nf - 1)
        def _():
            u = ALPHA * h_ref[...] + acc[...]
            u_ref[...] = u
            o_ref[...] = _ln(u, g_ref[...], b_ref[...])

    row = pl.BlockSpec((tm, C), lambda i, f: (i, 0))
    return _pcall(body, grid=(T // tm, nf),
                  in_specs=[row, pl.BlockSpec((None, C, tf), lambda i, f: (l, 0, f)), pl.BlockSpec((None, tf, C), lambda i, f: (l, f, 0)),
                            _full((1, C)), _full((1, C))],
                  out_specs=[row, row], out_shape=[_S((T, C)), _S((T, C))], scratch_shapes=[pltpu.VMEM((tm, C), F32)], name=name)(h, wup, wdown, g, b)


def ff_bwd(u, dy, h, g, b, wup, wdown, l, name):
    T, C = h.shape
    F = wup.shape[2]
    tm, tf = min(T, 512), 1024
    nf = F // tf

    def body(u_ref, dy_ref, h_ref, g_ref, b_ref, wu_ref, wd_ref, du_ref, dh_ref, da_ref, r_ref, dg_ref, db_ref, du_s, acc):
        i, f = pl.program_id(0), pl.program_id(1)

        @pl.when(f == 0)
        def _():
            _, vjp = jax.vjp(_ln, u_ref[...], g_ref[...], b_ref[...])
            du, dg, db = vjp(dy_ref[...])
            du_s[...] = du
            du_ref[...] = du
            _acc_out(dg_ref, dg, i == 0)
            _acc_out(db_ref, db, i == 0)

        a = _dotb(h_ref[...].astype(BF), wu_ref[...], NN)
        ap = jnp.maximum(a, 0.0)
        dr = _dotb(du_s[...].astype(BF), wd_ref[...], NT)
        da = (dr * (2.0 * ap)).astype(BF)
        da_ref[...] = da
        r_ref[...] = jnp.square(ap).astype(BF)
        _acc_out(acc, _dotb(da, wu_ref[...], NT), f == 0)

        @pl.when(f == nf - 1)
        def _():
            dh_ref[...] = ALPHA * du_s[...] + acc[...]

    row = pl.BlockSpec((tm, C), lambda i, f: (i, 0))
    colf = pl.BlockSpec((tm, tf), lambda i, f: (i, f))
    return _pcall(body, grid=(T // tm, nf),
                  in_specs=[row, row, row, _full((1, C)), _full((1, C)), pl.BlockSpec((None, C, tf), lambda i, f: (l, 0, f)),
                            pl.BlockSpec((None, tf, C), lambda i, f: (l, f, 0))],
                  out_specs=[row, row, colf, colf, _full((1, C)), _full((1, C))],
                  out_shape=[_S((T, C)), _S((T, C)), _S((T, F), BF), _S((T, F), BF), _S((1, C)), _S((1, C))],
                  scratch_shapes=[pltpu.VMEM((tm, C), F32), pltpu.VMEM((tm, C), F32)], name=name)(u, dy, h, g, b, wup, wdown)


MESH_ID = pl.DeviceIdType.MESH
_ANY = pl.BlockSpec(memory_space=pl.ANY)


def _window(ref, ax, idx, n):
    if n < 0:
        return ref
    sel = idx if n == 0 else pl.ds(pl.multiple_of(idx * n, n), n)
    return ref.at[(slice(None),) * ax + (sel,)]


def all_gather_place(shards, axes, name):
    K = len(shards)
    widths = [s.shape[a] for s, a in zip(shards, axes)]
    out_shape = [_S(s.shape[:a] + (N_DEV * s.shape[a],) + s.shape[a + 1:], s.dtype) for s, a in zip(shards, axes)]

    def body(*refs):
        x_refs, o_refs = refs[:K], refs[K:2 * K]
        send_sems, recv_sems, local_sems = refs[2 * K:]
        mx, my, mc = lax.axis_index("x"), lax.axis_index("y"), lax.axis_index("c")
        me, sibling = (mx, my, mc), (mx, my, 1 - mc)
        chips = [(1 - mx, my), (mx, 1 - my), (1 - mx, 1 - my)]

        def win(k, px, py, pc):
            return _window(o_refs[k], axes[k], 4 * px + 2 * py + pc, widths[k])

        def copy(k, slot, block, to, src=None):
            return pltpu.make_async_remote_copy(src_ref=win(k, *block) if src is None else src, dst_ref=win(k, *block),
                                                send_sem=send_sems.at[7 * k + slot], recv_sem=recv_sems.at[7 * k + slot],
                                                device_id=to, device_id_type=MESH_ID)

        mine = [pltpu.make_async_copy(x_refs[k], win(k, *me), local_sems.at[k]) for k in range(K)]
        first = []
        for k in range(K):
            mine[k].start()
            first.append(copy(k, 0, me, sibling, src=x_refs[k]))
            first += [copy(k, 1 + j, me, (*chip, mc), src=x_refs[k]) for j, chip in enumerate(chips)]
        for cp in first:
            cp.start()
        passed = []
        for j, chip in enumerate(chips):
            for k in range(K):
                copy(k, 1 + j, (*chip, mc), me).wait_recv()
                passed.append(copy(k, 4 + j, (*chip, mc), sibling))
                passed[-1].start()
        for k in range(K):
            copy(k, 0, sibling, me).wait_recv()
        for j, chip in enumerate(chips):
            for k in range(K):
                copy(k, 4 + j, (*chip, 1 - mc), me).wait_recv()
        for cp in first + passed:
            cp.wait_send()
        for cp in mine:
            cp.wait()

    return _pcall(body, out_shape=out_shape, in_specs=[_ANY] * K, out_specs=[_ANY] * K,
                  scratch_shapes=[pltpu.SemaphoreType.DMA((7 * K,)), pltpu.SemaphoreType.DMA((7 * K,)), pltpu.SemaphoreType.DMA((K,))],
                  name=name)(*shards)


def all_to_all_place(items, name):
    flat = [(i, li) for i, it in enumerate(items) for li in range(len(it[0]))]
    n_src = len(flat)
    out_shape = [_S((N_DEV, len(it[0])) + tuple(it[3]), F32) for it in items]

    def body(*refs):
        src_refs, o_refs = refs[:n_src], refs[n_src:n_src + len(items)]
        send_sems, recv_sems, local_sems = refs[n_src + len(items):]
        mx, my, mc = lax.axis_index("x"), lax.axis_index("y"), lax.axis_index("c")
        me = 4 * mx + 2 * my + mc
        copies, own = [], []
        for s, (i, li) in enumerate(flat):
            _, ax, n, _ = items[i]
            own.append(pltpu.make_async_copy(_window(src_refs[s], ax, me, n), o_refs[i].at[me, li], local_sems.at[s]))
            own[-1].start()
            for k in range(1, N_DEV):
                px = 1 - mx if k & 4 else mx
                py = 1 - my if k & 2 else my
                pc = 1 - mc if k & 1 else mc
                cp = pltpu.make_async_remote_copy(src_ref=_window(src_refs[s], ax, 4 * px + 2 * py + pc, n), dst_ref=o_refs[i].at[me, li],
                                                  send_sem=send_sems.at[7 * s + k - 1], recv_sem=recv_sems.at[7 * s + k - 1],
                                                  device_id=(px, py, pc), device_id_type=MESH_ID)
                cp.start()
                copies.append(cp)
        for cp in copies:
            cp.wait()
        for cp in own:
            cp.wait()

    return _pcall(body, out_shape=out_shape, in_specs=[_ANY] * n_src, out_specs=[_ANY] * len(items),
                  scratch_shapes=[pltpu.SemaphoreType.DMA((7 * n_src,)), pltpu.SemaphoreType.DMA((7 * n_src,)), pltpu.SemaphoreType.DMA((n_src,))],
                  name=name)(*[s for it in items for s in it[0]])


def reduce_adamw(rcv, w, m, v, name):
    _, A, B, C = rcv.shape
    tb = B
    while tb > 8 and tb * C > (1 << 17):
        tb //= 2

    def body(r_ref, w_ref, m_ref, v_ref, g_ref, d_ref, mo_ref, vo_ref):
        g = r_ref[0]
        for d in range(1, N_DEV):
            g = g + r_ref[d]
        mn = ADAM_B1 * m_ref[...] + (1.0 - ADAM_B1) * g
        vn = ADAM_B2 * v_ref[...] + (1.0 - ADAM_B2) * jnp.square(g)
        m_hat = mn / (1.0 - ADAM_B1 ** ADAM_STEP)
        v_hat = vn / (1.0 - ADAM_B2 ** ADAM_STEP)
        g_ref[...] = g
        d_ref[...] = -ADAM_LR * (m_hat / (jnp.sqrt(v_hat) + ADAM_EPS) + ADAM_WD * w_ref[...])
        mo_ref[...] = mn
        vo_ref[...] = vn

    blk = pl.BlockSpec((None, tb, C), lambda a, i: (a, i, 0))
    return _pcall(body, grid=(A, B // tb), in_specs=[pl.BlockSpec((N_DEV, None, tb, C), lambda a, i: (0, a, i, 0)), blk, blk, blk],
                  out_specs=[blk] * 4, out_shape=[_S((A, B, C))] * 4, name=name)(rcv, w, m, v)


def _w_in_pieces(g0, g1):
    per = D_IN // N_DEV
    return [(d, max(g0, d * per) - d * per, min(g1, (d + 1) * per) - d * per) for d in range(N_DEV) if max(g0, d * per) < min(g1, (d + 1) * per)]


def repack_w_in(w8, name):
    _, L, R, per = w8.shape
    tr = 256

    def cols(x_ref, g0, g1):
        return [x_ref[d, :, a:b] for d, a, b in _w_in_pieces(g0, g1)]

    def body(x_ref, *o_refs):
        for (name_, i), o_ref in zip(SEGS, o_refs):
            o_ref[...] = jnp.concatenate(cols(x_ref, _OFF[i], _OFF[i + 1]), axis=1)
        parts, at = [], 0
        for i, lane0 in SMALL_SRC:
            assert lane0 == at
            parts += cols(x_ref, _OFF[i], _OFF[i + 1])
            at += IN_SIZES[i]
        parts.append(jnp.zeros((tr, LANES - at), w8.dtype))
        o_refs[-1][...] = jnp.concatenate(parts, axis=1)

    widths = [IN_SIZES[i] for _, i in SEGS] + [LANES]
    outs = _pcall(body, grid=(L, R // tr), in_specs=[pl.BlockSpec((N_DEV, None, tr, per), lambda l, r: (0, l, r, 0))],
                  out_specs=[pl.BlockSpec((None, tr, w), lambda l, r: (l, r, 0)) for w in widths],
                  out_shape=[_S((L, R, w), w8.dtype) for w in widths], name=name)(w8)
    return dict(zip(SEG_NAMES, outs))


def repack_dw_in(dseg, name):
    R = dseg["z"].shape[0]
    per = D_IN // N_DEV
    tr = 128
    src = {i: (k, 0) for k, (_, i) in enumerate(SEGS)}
    src.update({i: (len(SEGS), lane0) for i, lane0 in SMALL_SRC})

    def body(*refs):
        s_refs, o_ref = refs[:-1], refs[-1]
        for d in range(N_DEV):
            parts = []
            for i in range(len(IN_SIZES)):
                g0, g1 = max(_OFF[i], d * per), min(_OFF[i + 1], (d + 1) * per)
                if g0 < g1:
                    k, c0 = src[i]
                    parts.append(s_refs[k][:, c0 + g0 - _OFF[i]:c0 + g1 - _OFF[i]])
            o_ref[d] = jnp.concatenate(parts, axis=1)

    arrs = [dseg[n] for n in SEG_NAMES]
    return _pcall(body, grid=(R // tr,), in_specs=[pl.BlockSpec((tr, a.shape[1]), lambda r: (r, 0)) for a in arrs],
                  out_specs=pl.BlockSpec((N_DEV, tr, per), lambda r: (0, r, 0)), out_shape=_S((N_DEV, R, per)), name=name)(*arrs)


WEIGHTS = ("ln_in_g", "ln_in_b", "w_in", "ssd_conv_w", "ssd_conv_b", "ssd_dt_bias", "ssd_a_log", "ssd_d", "ssd_norm_w", "dn_conv_w",
           "dn_a_log", "dn_dt_bias", "dn_norm_w", "sg_ln_g", "sg_ln_b", "sg_w", "sg_b", "fox_f_bias", "gate_b", "w_branch", "w_out",
           "ln1_g", "ln1_b", "w_up", "w_down", "ln2_g", "ln2_b")
SHARDED = {"w_in": 2, "ssd_conv_w": 2, "dn_conv_w": 2, "gate_b": 2, "w_branch": 3, "w_out": 1, "w_up": 2, "w_down": 1}
SLABBED = ("w_in", "dn_conv_w")
MATMUL_WEIGHTS = ("w_in", "w_branch", "w_out", "w_up", "w_down")
REPLICATED = tuple(n for n in WEIGHTS if n not in SHARDED)
SEG_NAMES = tuple(n for n, _ in SEGS) + ("small",)
PACK_COLS = 1024


def _lanes(vec, off):
    return jnp.pad(vec, (off, LANES - off - vec.shape[0]))[None]


def _pack_small(parts):
    flat = jnp.concatenate([q.reshape(-1) for q in parts])
    rows = -(-flat.shape[0] // (PACK_COLS * 64)) * 64
    return jnp.pad(flat, (0, rows * PACK_COLS - flat.shape[0])).reshape(1, rows, PACK_COLS)


def _gather_weights(p):
    names = list(SHARDED)
    shards, axes = [], []
    for n in names:
        s = p[n].astype(BF) if n in MATMUL_WEIGHTS else p[n]
        shards.append(s[None] if n in SLABBED else s)
        axes.append(0 if n in SLABBED else SHARDED[n])
    full = dict(zip(names, all_gather_place(shards, axes, "weights_all_gather")))
    full["w_in"] = repack_w_in(full["w_in"], "w_in_repack")
    full["dn_conv_w"] = jnp.moveaxis(full["dn_conv_w"], 0, 2).reshape(DEPTH, 4, 3 * BRANCH_W)
    return full


def _layer_weights(p, full, l):
    w = {n: full[n] for n in ("w_in", "w_branch", "w_out", "w_up", "w_down")}
    w["ssd_cw"] = full["ssd_conv_w"][l]
    w["ssd_cb"] = p["ssd_conv_b"][l][None]
    w["dn_cw"] = full["dn_conv_w"][l]
    w["dn_cb"] = jnp.zeros((1, 3 * BRANCH_W), F32)
    w["gate_b"] = full["gate_b"][l]
    w["ssd_ps"] = [_lanes(p["ssd_dt_bias"][l], DT0), _lanes(p["ssd_a_log"][l], DT0), _lanes(p["ssd_d"][l], DT0), p["ssd_norm_w"][l][None]]
    w["dn_ps"] = [_lanes(p["dn_a_log"][l], A0), _lanes(p["dn_dt_bias"][l], A0), p["dn_norm_w"][l][None]]
    w["sg_ps"] = [p["sg_ln_g"][l][None], p["sg_ln_b"][l][None], p["sg_w"][l], jnp.pad(p["sg_b"][l].T, ((0, 0), (0, LANES - 4)))]
    w["fox_ps"] = [_lanes(p["fox_f_bias"][l], FF0)]
    for n in ("ln1_g", "ln1_b", "ln2_g", "ln2_b"):
        w[n] = p[n][l][None]
    return w


def _scan_specs(T, a):
    c0 = lambda c, h: (c, 0)
    ssd = dict(f=ssd_chunk, xs=[(a["z"], (128, 512), c0), (a["xbc_act"], (128, 1024), c0), (a["small"], (128, LANES), c0)],
               ys=[((T, BRANCH_W), (128, BRANCH_W), c0)], state=(4, LANES, LANES), nc=T // 128, nh=1, shared=())
    dn = dict(f=dn_chunk, xs=[(a["dn_act"], (64, LANES), lambda c, h: (c, h)), (a["dn_act"], (64, LANES), lambda c, h: (c, 4 + h)),
                              (a["dn_act"], (64, LANES), lambda c, h: (c, 8 + h)), (a["dngate"], (64, LANES), lambda c, h: (c, h)),
                              (a["small"], (64, LANES), c0)],
              ys=[((T, BRANCH_W), (64, LANES), lambda c, h: (c, h))], state=(1, LANES, LANES), nc=T // 64, nh=4, shared=(4,))
    sg = dict(f=sg_chunk, xs=[(a["sguv"], (128, 1024), c0)], ys=[((T, BRANCH_W), (128, BRANCH_W), c0)], state=(1, 8, LANES), nc=T // 128, nh=1, shared=())
    fc = dict(f=foxc_chunk, xs=[(a["small"], (128, LANES), c0)],
              ys=[((T, LANES), (128, LANES), c0), ((8, T), (8, 128), lambda c, h: (0, c))], state=(1, 1, LANES), nc=T // 128, nh=1, shared=())
    return ssd, dn, sg, fc


def _layer_fwd(h, w, l):
    T = h.shape[0]
    a = {"h": h}
    for n in SEG_NAMES:
        a[n] = matmul_w(h, w["w_in"][n], l, "nn", f"proj_{n}_{l}")
    a["xbc_act"] = conv_fwd(a["xbc"], w["ssd_cw"], w["ssd_cb"], f"ssd_conv_{l}")
    a["dn_act"] = conv_fwd(a["dnqkv"], w["dn_cw"], w["dn_cb"], f"dn_conv_{l}")
    ssd, dn, sg, fc = _scan_specs(T, a)
    a["ya"], a["ssd_st"] = scan_fwd(f"ssd_fwd_{l}", ssd["f"], ssd["xs"], w["ssd_ps"], ssd["ys"], ssd["state"], ssd["nc"], ssd["nh"])
    a["yb"], a["dn_st"] = scan_fwd(f"dn_fwd_{l}", dn["f"], dn["xs"], w["dn_ps"], dn["ys"], dn["state"], dn["nc"], dn["nh"])
    a["yc"], a["sg_st"] = scan_fwd(f"sg_fwd_{l}", sg["f"], sg["xs"], w["sg_ps"], sg["ys"], sg["state"], sg["nc"], sg["nh"])
    a["ccol"], a["crow"], a["fc_st"] = scan_fwd(f"foxc_fwd_{l}", fc["f"], fc["xs"], w["fox_ps"], fc["ys"], fc["state"], fc["nc"], fc["nh"])
    a["yd"], a["lse"] = fox_fwd(a["foxqkv"], a["ccol"], a["crow"], f"fox_fwd_{l}")
    a["merged"] = merge_fwd([a["ya"], a["yb"], a["yc"], a["yd"]], a["gates"], w["gate_b"], w["w_branch"], l, f"merge_fwd_{l}")
    a["u1"], a["h1"] = out_fwd(a["merged"], h, w["w_out"], l, w["ln1_g"], w["ln1_b"], f"out_fwd_{l}")
    a["u2"], a["h2"] = ff_fwd(a["h1"], w["w_up"], w["w_down"], l, w["ln2_g"], w["ln2_b"], f"ff_fwd_{l}")
    return a


def _layer_bwd(dh2, a, w, l):
    T = dh2.shape[0]
    g = {}
    du2, dh1, da, r, dg2, db2 = ff_bwd(a["u2"], dh2, a["h1"], w["ln2_g"], w["ln2_b"], w["w_up"], w["w_down"], l, f"ff_bwd_{l}")
    g["ln2_g"], g["ln2_b"] = dg2[0], db2[0]
    g["w_up"] = matmul_tn(a["h1"], da, f"dwup_{l}")
    g["w_down"] = matmul_tn(r, du2, f"dwdown_{l}")
    du1, dmerged, dg1, db1 = out_bwd(a["u1"], dh1, w["ln1_g"], w["ln1_b"], w["w_out"], l, f"out_bwd_{l}")
    g["ln1_g"], g["ln1_b"] = dg1[0], db1[0]
    g["w_out"] = matmul_tn(a["merged"], du1, f"dwout_{l}")
    ys = [a["ya"], a["yb"], a["yc"], a["yd"]]
    dya, dyb, dyc, dyd, dgl, dz, dgb = merge_bwd(ys, a["gates"], w["gate_b"], w["w_branch"], l, dmerged, f"merge_bwd_{l}")
    g["gate_b"] = dgb
    g["w_branch"] = jnp.stack([matmul_tn(ys[i], dz, f"dwb{i}_{l}", b_col0=i * D_MODEL, n_cols=D_MODEL) for i in range(4)])
    ssd, dn, sg, fc = _scan_specs(T, a)
    dz_ssd, dxbc_act, dsm_ssd, d_dtb, d_alog, d_dsk, d_nw = scan_bwd(f"ssd_bwd_{l}", ssd["f"], ssd["xs"], w["ssd_ps"], ssd["ys"], [dya], a["ssd_st"],
                                                                      ssd["state"], ssd["nc"], ssd["nh"])
    g["ssd_dt_bias"], g["ssd_a_log"], g["ssd_d"], g["ssd_norm_w"] = d_dtb[0, DT0:DT0 + 8], d_alog[0, DT0:DT0 + 8], d_dsk[0, DT0:DT0 + 8], d_nw[0]
    dxbc, g["ssd_conv_w"], dcb = conv_bwd(a["xbc"], w["ssd_cw"], w["ssd_cb"], dxbc_act, f"ssd_conv_bwd_{l}")
    g["ssd_conv_b"] = dcb[0]
    dq, dk, dv, ddngate, dsm_dn, d_alog, d_dtb, d_nw = scan_bwd(f"dn_bwd_{l}", dn["f"], dn["xs"], w["dn_ps"], dn["ys"], [dyb], a["dn_st"],
                                                                dn["state"], dn["nc"], dn["nh"], shared=dn["shared"])
    g["dn_a_log"], g["dn_dt_bias"], g["dn_norm_w"] = d_alog[0, A0:A0 + 4], d_dtb[0, A0:A0 + 4], d_nw[0]
    ddn_act = jnp.concatenate([dq[:, :BRANCH_W], dk[:, BRANCH_W:2 * BRANCH_W], dv[:, 2 * BRANCH_W:]], axis=1)
    ddnqkv, g["dn_conv_w"], _ = conv_bwd(a["dnqkv"], w["dn_cw"], w["dn_cb"], ddn_act, f"dn_conv_bwd_{l}")
    dsguv, d_lng, d_lnb, d_w, d_bt = scan_bwd(f"sg_bwd_{l}", sg["f"], sg["xs"], w["sg_ps"], sg["ys"], [dyc], a["sg_st"], sg["state"], sg["nc"], sg["nh"])
    g["sg_ln_g"], g["sg_ln_b"], g["sg_w"], g["sg_b"] = d_lng[0], d_lnb[0], d_w, d_bt[:, :4].T
    dfq, dfk, dfv, dccol, dcrow = fox_bwd(a["foxqkv"], a["ccol"], a["crow"], a["yd"], a["lse"], dyd, f"fox_bwd_{l}")
    dsm_fox, d_fb = scan_bwd(f"foxc_bwd_{l}", fc["f"], fc["xs"], w["fox_ps"], fc["ys"], [dccol, dcrow], a["fc_st"], fc["state"], fc["nc"], fc["nh"])
    g["fox_f_bias"] = d_fb[0, FF0:FF0 + 8]
    dseg = {"z": dz_ssd, "xbc": dxbc, "dnqkv": ddnqkv, "dngate": ddngate, "sguv": dsguv,
            "foxqkv": jnp.concatenate([dfq, dfk, dfv], axis=1), "gates": dgl, "small": add3(dsm_ssd, dsm_dn, dsm_fox, f"dsmall_{l}")}
    dh, scale, dwin = du1, ALPHA, {}
    for n in SEG_NAMES:
        dh = matmul_w(dseg[n], w["w_in"][n], l, "nt", f"dh_{n}_{l}", add=dh, add_scale=scale)
        scale = 1.0
        dwin[n] = matmul_tn(a["h"], dseg[n], f"dwin_{n}_{l}")
    g["w_in"] = repack_dw_in(dwin, f"dw_in_repack_{l}")
    g["dn_conv_w"] = jnp.moveaxis(g["dn_conv_w"].reshape(4, N_DEV, 3 * BRANCH_W // N_DEV), 1, 0)
    return dh, g


def kernel(x, ln_in_g, ln_in_b, w_in, ssd_conv_w, ssd_conv_b, ssd_dt_bias, ssd_a_log, ssd_d, ssd_norm_w, dn_conv_w, dn_a_log, dn_dt_bias, dn_norm_w, sg_ln_g, sg_ln_b, sg_w, sg_b, fox_f_bias, gate_b, w_branch, w_out, ln1_g, ln1_b, w_up, w_down, ln2_g, ln2_b, loss_target, m_ln_in_g, m_ln_in_b, m_w_in, m_ssd_conv_w, m_ssd_conv_b, m_ssd_dt_bias, m_ssd_a_log, m_ssd_d, m_ssd_norm_w, m_dn_conv_w, m_dn_a_log, m_dn_dt_bias, m_dn_norm_w, m_sg_ln_g, m_sg_ln_b, m_sg_w, m_sg_b, m_fox_f_bias, m_gate_b, m_w_branch, m_w_out, m_ln1_g, m_ln1_b, m_w_up, m_w_down, m_ln2_g, m_ln2_b, v_ln_in_g, v_ln_in_b, v_w_in, v_ssd_conv_w, v_ssd_conv_b, v_ssd_dt_bias, v_ssd_a_log, v_ssd_d, v_ssd_norm_w, v_dn_conv_w, v_dn_a_log, v_dn_dt_bias, v_dn_norm_w, v_sg_ln_g, v_sg_ln_b, v_sg_w, v_sg_b, v_fox_f_bias, v_gate_b, v_w_branch, v_w_out, v_ln1_g, v_ln1_b, v_w_up, v_w_down, v_ln2_g, v_ln2_b):
    args = dict(locals())
    p = {n: args[n] for n in WEIGHTS}
    xt, target = x[0], loss_target[0]
    full = _gather_weights(p)
    ws = [_layer_weights(p, full, l) for l in range(DEPTH)]

    h = ln_fwd(xt, ln_in_g[None], ln_in_b[None], "ln_in_fwd")
    acts = []
    for l in range(DEPTH):
        acts.append(_layer_fwd(h, ws[l], l))
        h = acts[-1]["h2"]
    dh, loss = loss_head(h, target, "loss_head")
    loss = lax.psum(loss[0, 0], ("x", "y", "c"))

    layer_grads = [None] * DEPTH
    for l in reversed(range(DEPTH)):
        dh, layer_grads[l] = _layer_bwd(dh, acts[l], ws[l], l)
    grad_x, dg_in, db_in = ln_bwd(xt, ln_in_g[None], ln_in_b[None], dh, "ln_in_bwd")
    small = {n: jnp.stack([layer_grads[l][n] for l in range(DEPTH)]) for n in REPLICATED if n not in ("ln_in_g", "ln_in_b")}
    small["ln_in_g"], small["ln_in_b"] = dg_in[0], db_in[0]

    items = []
    for n in SHARDED:
        srcs, local = [layer_grads[l][n] for l in range(DEPTH)], p[n].shape[1:]
        items.append((srcs, 0, 0, local) if n in SLABBED else (srcs, SHARDED[n] - 1, local[SHARDED[n] - 1], local))
    pack = _pack_small([small[n] for n in REPLICATED])
    items.append(([pack[0]], 0, -1, pack.shape[1:]))
    rcvs = all_to_all_place(items, "grads_all_to_all")

    res = [{}, {}, {}, {}]
    for n, rcv in zip(SHARDED, rcvs):
        shp = p[n].shape
        to3 = lambda t: t.reshape((-1,) + shp[-2:])
        outs = reduce_adamw(rcv.reshape((N_DEV, -1) + shp[-2:]), to3(p[n]), to3(args["m_" + n]), to3(args["v_" + n]), f"adamw_{n}")
        for k in range(4):
            res[k][n] = outs[k].reshape(shp)
    outs = reduce_adamw(rcvs[-1], _pack_small([p[n] for n in REPLICATED]), _pack_small([args["m_" + n] for n in REPLICATED]),
                        _pack_small([args["v_" + n] for n in REPLICATED]), "adamw_replicated")
    off = 0
    for n in REPLICATED:
        shp = p[n].shape
        cnt = math.prod(shp)
        for k in range(4):
            res[k][n] = outs[k].reshape(-1)[off:off + cnt].reshape(shp)
        off += cnt
    return (loss, grad_x[None], *[res[0][n] for n in WEIGHTS], *[res[1][n] for n in WEIGHTS],
            *[res[2][n] for n in WEIGHTS], *[res[3][n] for n in WEIGHTS])
```

```python
import collections
import functools
import math

import jax
import jax.numpy as jnp
from jax import lax
from jax.experimental import pallas as pl
from jax.experimental.pallas import tpu as pltpu

F32 = jnp.float32
BF = jnp.bfloat16

D_MODEL = 1024
DEPTH = 2
BRANCH_W = 512
D_FF = 4096
LN_EPS = 1e-5
NORM_EPS = 1e-6
ALPHA = (2 * DEPTH) ** 0.25
N_DEV = 8
LANES = 128
ADAM_LR, ADAM_B1, ADAM_B2, ADAM_EPS, ADAM_WD, ADAM_STEP = 0.001, 0.9, 0.999, 1e-08, 0.01, 10

DT0, BETA0, A0, FF0 = 0, 8, 12, 16
IN_SIZES = (512, 1024, 8, 1536, 4, 4, 512, 1024, 1536, 8, 4096)
_OFF = [0]
for _s in IN_SIZES:
    _OFF.append(_OFF[-1] + _s)
D_IN = _OFF[-1]
SEGS = (("z", 0), ("xbc", 1), ("dnqkv", 3), ("dngate", 6), ("sguv", 7), ("foxqkv", 8), ("gates", 10))
SMALL_SRC = ((2, DT0), (4, BETA0), (5, A0), (9, FF0))

NN = ((1,), (0,))
NT = ((1,), (1,))
TN = ((0,), (0,))
_DIMS = {"nn": NN, "nt": NT, "tn": TN}


def _pcall(body, **kw):
    return pl.pallas_call(body, **kw)


def _S(shape, dtype=F32):
    return jax.ShapeDtypeStruct(tuple(shape), dtype)


def _iota(shape, dim):
    return lax.broadcasted_iota(jnp.int32, shape, dim)


def _dotb(a, b, dims):
    return lax.dot_general(a, b, (dims, ((), ())), preferred_element_type=F32)


def _split2(a):
    ah = a.astype(BF)
    return ah, (a - ah.astype(F32)).astype(BF)


def _split3(a):
    a1 = a.astype(BF)
    r = a - a1.astype(F32)
    a2 = r.astype(BF)
    a3 = (r - a2.astype(F32)).astype(BF)
    return a1, a2, a3


def _mm_raw(a, b, form, mode):
    d = _DIMS[form]
    if mode == "1":
        return _dotb(a.astype(BF), b.astype(BF), d)
    if mode == "3":
        ah, al = _split2(a)
        bh, bl = _split2(b)
        return _dotb(ah, bh, d) + (_dotb(ah, bl, d) + _dotb(al, bh, d))
    if mode == "xa":
        ab = a.astype(BF)
        b1, b2, b3 = _split3(b)
        return _dotb(ab, b1, d) + (_dotb(ab, b2, d) + _dotb(ab, b3, d))
    bb = b.astype(BF)
    a1, a2, a3 = _split3(a)
    return _dotb(a1, bb, d) + (_dotb(a2, bb, d) + _dotb(a3, bb, d))


@functools.partial(jax.custom_vjp, nondiff_argnums=(2, 3))
def mm(a, b, form, mode):
    return _mm_raw(a, b, form, mode)


def _mm_fwd(a, b, form, mode):
    return _mm_raw(a, b, form, mode), (a, b)


_XA_DB = {"nn": "xa", "nt": "xb", "tn": "xa"}
_XB_DA = {"nn": "xb", "nt": "xb", "tn": "xa"}


def _mm_bwd(form, mode, res, g):
    a, b = res
    ma = _XB_DA[form] if mode == "xb" else mode
    mb = _XA_DB[form] if mode == "xa" else mode
    da = db = None
    if mode != "xa":
        da = {"nn": lambda: mm(g, b, "nt", ma), "nt": lambda: mm(g, b, "nn", ma), "tn": lambda: mm(b, g, "nt", ma)}[form]()
    if mode != "xb":
        db = {"nn": lambda: mm(a, g, "tn", mb), "nt": lambda: mm(g, a, "tn", mb), "tn": lambda: mm(a, g, "nn", mb)}[form]()
    if da is None:
        da = jnp.zeros_like(a)
    if db is None:
        db = jnp.zeros_like(b)
    return da, db


mm.defvjp(_mm_fwd, _mm_bwd)


def _silu(x):
    return x * jax.nn.sigmoid(x)


def _ln(x, g, b):
    mu = jnp.mean(x, -1, keepdims=True)
    xc = x - mu
    var = jnp.mean(xc * xc, -1, keepdims=True)
    return xc * lax.rsqrt(var + LN_EPS) * g + b


def _pick(n, cap):
    if n <= cap:
        return n
    best = LANES
    for t in range(LANES, cap + 1, LANES):
        if n % t == 0:
            best = t
    return best


def matmul_w(a, w, l, form, name, add=None, add_scale=1.0):
    M, K = a.shape
    N = w.shape[2] if form == "nn" else w.shape[1]
    tm, tn, tk = min(M, 512), _pick(N, 1024), _pick(K, 1024)
    nk = K // tk

    def body(*refs):
        if add is None:
            a_ref, b_ref, o_ref, acc = refs
        else:
            a_ref, b_ref, d_ref, o_ref, acc = refs
        k = pl.program_id(2)
        p = _dotb(a_ref[...].astype(BF), b_ref[...].astype(BF), _DIMS[form])

        @pl.when(k == 0)
        def _():
            acc[...] = p

        @pl.when(k > 0)
        def _():
            acc[...] += p

        @pl.when(k == nk - 1)
        def _():
            r = acc[...]
            if add is not None:
                r = r + add_scale * d_ref[...]
            o_ref[...] = r

    if form == "nn":
        wspec = pl.BlockSpec((None, tk, tn), lambda j, i, k: (l, k, j))
    else:
        wspec = pl.BlockSpec((None, tn, tk), lambda j, i, k: (l, j, k))
    in_specs = [pl.BlockSpec((tm, tk), lambda j, i, k: (i, k)), wspec]
    args = [a, w]
    if add is not None:
        in_specs.append(pl.BlockSpec((tm, tn), lambda j, i, k: (i, j)))
        args.append(add)
    return _pcall(body, grid=(N // tn, M // tm, nk), in_specs=in_specs,
                  out_specs=pl.BlockSpec((tm, tn), lambda j, i, k: (i, j)), out_shape=_S((M, N)),
                  scratch_shapes=[pltpu.VMEM((tm, tn), F32)], name=name)(*args)


def matmul_tn(a, b, name, b_col0=0, n_cols=None, out_dtype=F32):
    T, M = a.shape
    N = b.shape[1] if n_cols is None else n_cols
    tm, tn, tt = _pick(M, 512), _pick(N, 1024), min(T, 512)
    nt = T // tt
    jb = b_col0 // tn

    def body(a_ref, b_ref, o_ref, acc):
        t = pl.program_id(2)
        p = _dotb(a_ref[...].astype(BF), b_ref[...].astype(BF), TN)

        @pl.when(t == 0)
        def _():
            acc[...] = p

        @pl.when(t > 0)
        def _():
            acc[...] += p

        @pl.when(t == nt - 1)
        def _():
            o_ref[...] = acc[...].astype(out_dtype)

    return _pcall(body, grid=(M // tm, N // tn, nt),
                  in_specs=[pl.BlockSpec((tt, tm), lambda i, j, t: (t, i)), pl.BlockSpec((tt, tn), lambda i, j, t: (t, jb + j))],
                  out_specs=pl.BlockSpec((tm, tn), lambda i, j, t: (i, j)), out_shape=_S((M, N), out_dtype),
                  scratch_shapes=[pltpu.VMEM((tm, tn), F32)], name=name)(a, b)


def _pieces(v):
    if v.ndim == 3:
        return [v[i] for i in range(v.shape[0])]
    n = v.shape[1] // LANES
    if n <= 1:
        return [v]
    return [v[:, i * LANES:(i + 1) * LANES] for i in range(n)]


def _join(ps, like_ndim):
    if like_ndim == 3:
        return jnp.stack(ps, axis=0)
    return ps[0] if len(ps) == 1 else jnp.concatenate(ps, axis=1)


def scan_fwd(name, f, xs, ps, ys, state_shape, nc, nh=1):
    nx, npar, ny = len(xs), len(ps), len(ys)

    def body(*refs):
        x_refs, p_refs = refs[:nx], refs[nx:nx + npar]
        y_refs = refs[nx + npar:nx + npar + ny]
        st_out, st = refs[nx + npar + ny], refs[nx + npar + ny + 1]
        c, h = pl.program_id(0), pl.program_id(1)

        @pl.when(c == 0)
        def _():
            st[h] = jnp.zeros(state_shape, F32)

        S = st[h]
        st_out[...] = S
        yv, Sn = f([_pieces(r[...]) for r in x_refs], [_pieces(r[...]) for r in p_refs], _pieces(S), h)
        for r, v in zip(y_refs, yv):
            r[...] = _join(v, 2)
        st[h] = _join(Sn, 3)

    in_specs = [pl.BlockSpec(bs, im) for (_, bs, im) in xs]
    in_specs += [pl.BlockSpec(p.shape, (lambda c, h, n=p.ndim: (0,) * n)) for p in ps]
    out_specs = [pl.BlockSpec(bs, im) for (_, bs, im) in ys]
    out_specs.append(pl.BlockSpec((None, None) + tuple(state_shape), lambda c, h: (c, h, 0, 0, 0)))
    out_shape = [_S(s) for (s, _, _) in ys] + [_S((nc, nh) + tuple(state_shape))]
    return _pcall(body, grid=(nc, nh), in_specs=in_specs, out_specs=out_specs, out_shape=out_shape,
                  scratch_shapes=[pltpu.VMEM((nh,) + tuple(state_shape), F32)], name=name)(*[x[0] for x in xs], *ps)


def scan_bwd(name, f, xs, ps, ys, dys, states, state_shape, nc, nh=1, shared=(), rider=None):
    nx, npar, ny = len(xs), len(ps), len(ys)

    def body(*refs):
        x_refs, p_refs = refs[:nx], refs[nx:nx + npar]
        s_ref = refs[nx + npar]
        dy_refs = refs[nx + npar + 1:nx + npar + 1 + ny]
        o = nx + npar + 1 + ny
        dx_refs, dp_refs, dst = refs[o:o + nx], refs[o + nx:o + nx + npar], refs[o + nx + npar]
        c, h = pl.program_id(0), pl.program_id(1)

        @pl.when(c == 0)
        def _():
            dst[h] = jnp.zeros(state_shape, F32)

        @pl.when((c == 0) & (h == 0))
        def _():
            for r in dp_refs:
                r[...] = jnp.zeros(r.shape, F32)

        xv = [_pieces(r[...]) for r in x_refs]
        pv = [_pieces(r[...]) for r in p_refs]
        _, vjp = jax.vjp(lambda a, b, s: f(a, b, s, h), xv, pv, _pieces(s_ref[...]))
        dxv, dpv, dS = vjp(([_pieces(r[...]) for r in dy_refs], _pieces(dst[h])))
        for i, (r, v) in enumerate(zip(dx_refs, dxv)):
            if i in shared and nh > 1:
                @pl.when(h == 0)
                def _(r=r, v=v):
                    r[...] = _join(v, 2)

                @pl.when(h > 0)
                def _(r=r, v=v):
                    r[...] += _join(v, 2)
            else:
                r[...] = _join(v, 2)
        for r, v in zip(dp_refs, dpv):
            r[...] += _join(v, len(r.shape))
        dst[h] = _join(dS, 3)

    def rev(im):
        return lambda c, h: im(nc - 1 - c, h)

    in_specs = [pl.BlockSpec(bs, rev(im)) for (_, bs, im) in xs]
    in_specs += [pl.BlockSpec(p.shape, (lambda c, h, n=p.ndim: (0,) * n)) for p in ps]
    in_specs.append(pl.BlockSpec((None, None) + tuple(state_shape), lambda c, h: (nc - 1 - c, h, 0, 0, 0)))
    in_specs += [pl.BlockSpec(bs, rev(im)) for (_, bs, im) in ys]
    out_specs = [pl.BlockSpec(bs, rev(im)) for (_, bs, im) in xs]
    out_specs += [pl.BlockSpec(p.shape, (lambda c, h, n=p.ndim: (0,) * n)) for p in ps]
    out_shape = [_S(x[0].shape) for x in xs] + [_S(p.shape) for p in ps]
    return hosted_call(body, rider, grid=(nc, nh), in_specs=in_specs, out_specs=out_specs, out_shape=out_shape,
                       scratch_shapes=[pltpu.VMEM((nh,) + tuple(state_shape), F32)], name=name,
                       args=[*[x[0] for x in xs], *ps, states, *dys])


def _lane():
    return _iota((1, LANES), 1)


def _col(v, idx):
    return jnp.sum(v * (_lane() == idx).astype(F32), axis=1, keepdims=True)


def _last_row(v):
    r = v.shape[0]
    return jnp.sum(v * (_iota((r, 1), 0) == r - 1).astype(F32), axis=0, keepdims=True)


def _tril(n, strict=False):
    r, c = _iota((n, n), 0), _iota((n, n), 1)
    return (r > c) if strict else (r >= c)


def ssd_chunk(xs, ps, S, h):
    zp, xbc, (sm,) = xs
    (bias,), (alog,), (dsk,), nw = ps
    Q = sm.shape[0]
    lane = _lane()
    a128 = jnp.where(lane < 8, -jnp.exp(alog), 0.0)
    dtl = jax.nn.softplus(sm + bias)
    tri = _tril(Q)
    trif = tri.astype(F32)
    cum = mm(trif, dtl * a128, "nn", "xa")
    sel8 = (_iota((8, LANES), 0) == _iota((8, LANES), 1)).astype(F32)
    cum_t = mm(sel8, cum, "nt", "xa")
    m0 = (lane < 64).astype(F32)
    rows0 = (_iota((LANES, 1), 0) < 64).astype(F32)
    yz, Sn, ssq, cb = [], [], 0.0, None
    for pr in range(4):
        g = pr // 2
        Bg, Cg = xbc[4 + g], xbc[6 + g]
        if pr % 2 == 0:
            cb = mm(Cg, Bg, "nt", "1")
        xp, Hp = xbc[pr], S[pr]
        ypair, stpair, rowdec = 0.0, 0.0, 0.0
        for e in range(2):
            hh = 2 * pr + e
            me = m0 if e == 0 else 1.0 - m0
            re = rows0 if e == 0 else 1.0 - rows0
            col = _col(cum, hh)
            row = jnp.sum(cum_t * (_iota((8, 1), 0) == hh).astype(F32), axis=0, keepdims=True)
            xh = xp * me
            xdt = xh * _col(dtl, hh)
            seg = jnp.exp(jnp.where(tri, col - row, -jnp.inf))
            last = _last_row(col)
            ypair = ypair + mm(cb * seg, xdt, "nn", "1") + mm(Cg * jnp.exp(col), Hp, "nt", "1") * me + _col(dsk, hh) * xh
            stpair = stpair + mm(xdt, Bg * jnp.exp(last - col), "tn", "1")
            rowdec = rowdec + jnp.exp(last) * re
        Sn.append(Hp * rowdec + stpair)
        v = ypair * _silu(zp[pr])
        ssq = ssq + jnp.sum(v * v, axis=1, keepdims=True)
        yz.append(v)
    scale = lax.rsqrt(ssq / BRANCH_W + NORM_EPS)
    return [[yz[i] * scale * nw[i] for i in range(4)]], Sn


def dn_chunk(xs, ps, S, h):
    act, gate, (sm,) = xs
    (alog,), (dtb,), (nw,) = ps
    lane = _lane()
    G = jnp.where((lane >= A0) & (lane < A0 + 4), -jnp.exp(alog) * jax.nn.softplus(sm + dtb), 0.0)
    tri = _tril(sm.shape[0])
    gcs = mm(tri.astype(F32), G, "nn", "xa")
    sig = jax.nn.sigmoid(sm)
    heads = [_dn_head(act[i], act[4 + i], act[8 + i], gate[i], sig, gcs, tri, nw, S[i], i) for i in range(4)]
    return [[y for y, _ in heads]], [s for _, s in heads]


def _dn_head(q, k, v, gate, sig, gcs, tri, nw, S0, h):
    C = q.shape[0]
    qn = q * lax.rsqrt(jnp.sum(q * q, axis=1, keepdims=True) + NORM_EPS) * (LANES ** -0.5)
    kn = k * lax.rsqrt(jnp.sum(k * k, axis=1, keepdims=True) + NORM_EPS)
    beta = _col(sig, BETA0 + h)
    gcol = _col(gcs, A0 + h)
    selr = ((_iota((8, LANES), 0) == 0) & (_iota((8, LANES), 1) == A0 + h)).astype(F32)
    grow = jnp.sum(mm(selr, gcs, "nt", "xa"), axis=0, keepdims=True)
    gamma = jnp.exp(jnp.where(tri, gcol - grow, -jnp.inf))
    kb = kn * beta
    P = -(mm(kb, kn, "nt", "1") * jnp.where(_tril(C, True), gamma, 0.0))
    minv = (_iota((C, C), 0) == _iota((C, C), 1)).astype(F32) + P
    pk = P
    for _ in range(5):
        pk = mm(pk, pk, "nn", "3")
        minv = minv + mm(minv, pk, "nn", "3")
    eg = jnp.exp(gcol)
    w = mm(minv, kb * eg, "nn", "3")
    u = mm(minv, v * beta, "nn", "3")
    glast = _last_row(gcol)
    vnew = u - mm(w, S0, "nn", "1")
    o = mm(qn * eg, S0, "nn", "1") + mm(mm(qn, kn, "nt", "1") * gamma, vnew, "nn", "1")
    Sn = S0 * jnp.exp(glast) + mm(kn * jnp.exp(glast - gcol), vnew, "tn", "1")
    on = o * lax.rsqrt(jnp.mean(o * o, axis=1, keepdims=True) + NORM_EPS) * nw
    return on * _silu(gate), Sn


def sg_chunk(xs, ps, S, h):
    (uv,) = xs
    lng, lnb, W, (bt,) = ps
    u = [jax.nn.gelu(p) for p in uv[:4]]
    v = [jax.nn.gelu(p) for p in uv[4:]]
    mu = sum(jnp.sum(p, axis=1, keepdims=True) for p in v) / BRANCH_W
    vc = [p - mu for p in v]
    var = sum(jnp.sum(p * p, axis=1, keepdims=True) for p in vc) / BRANCH_W
    inv = lax.rsqrt(var + LN_EPS)
    trif = _tril(W[0].shape[0]).astype(F32)
    out = []
    for g in range(4):
        vn = vc[g] * inv * lng[g] + lnb[g]
        out.append(u[g] * (mm(W[g] * trif, vn, "nn", "1") + _col(bt, g)))
    return [out], S


def foxc_chunk(xs, ps, S, h):
    (sm,), ((fb,),), (carry,) = xs[0], ps, S
    lane = _lane()
    ls = jnp.where((lane >= FF0) & (lane < FF0 + 8), jax.nn.log_sigmoid(sm + fb), 0.0)
    Q = sm.shape[0]
    c = mm(_tril(Q).astype(F32), ls, "nn", "xa") + carry
    sel = (_iota((8, LANES), 1) == _iota((8, LANES), 0) + FF0).astype(F32)
    return [[c], [mm(sel, c, "nt", "xa")]], [_last_row(c)]


def _conv_tiles(T, C):
    return min(T, 512), _pick(C, 512)


def conv_fwd(x, w, b, name):
    T, C = x.shape
    tm, cb = _conv_tiles(T, C)

    def body(xp_ref, x_ref, w_ref, b_ref, o_ref):
        i = pl.program_id(1)
        tail = xp_ref[tm - 8:tm, :] * (i > 0).astype(F32)
        e = jnp.concatenate([tail, x_ref[...]], axis=0)
        pre = b_ref[...] + sum(w_ref[k:k + 1, :] * e[5 + k:5 + k + tm, :] for k in range(4))
        o_ref[...] = _silu(pre)

    return _pcall(body, grid=(C // cb, T // tm),
                  in_specs=[pl.BlockSpec((tm, cb), lambda j, i: (jnp.maximum(i - 1, 0), j)), pl.BlockSpec((tm, cb), lambda j, i: (i, j)),
                            pl.BlockSpec((4, cb), lambda j, i: (0, j)), pl.BlockSpec((1, cb), lambda j, i: (0, j))],
                  out_specs=pl.BlockSpec((tm, cb), lambda j, i: (i, j)), out_shape=_S((T, C)), name=name)(x, x, w, b)


def conv_bwd(x, w, b, dact, name):
    T, C = x.shape
    tm, cb = _conv_tiles(T, C)
    nt = T // tm

    def body(xp_ref, x_ref, xn_ref, w_ref, b_ref, d_ref, dn_ref, dx_ref, dw_ref, db_ref):
        i = pl.program_id(1)
        has_prev, has_next = (i > 0).astype(F32), (i < nt - 1).astype(F32)
        e = jnp.concatenate([xp_ref[tm - 8:tm, :] * has_prev, x_ref[...], xn_ref[0:8, :] * has_next], axis=0)
        pre = b_ref[...] + sum(w_ref[k:k + 1, :] * e[5 + k:5 + k + tm + 8, :] for k in range(4))
        de = jnp.concatenate([d_ref[...], dn_ref[0:8, :] * has_next], axis=0)
        sg = jax.nn.sigmoid(pre)
        dpre = de * (sg * (1.0 + pre * (1.0 - sg)))
        dx_ref[...] = sum(w_ref[k:k + 1, :] * dpre[3 - k:3 - k + tm, :] for k in range(4))
        dcur = dpre[0:tm, :]
        dw = jnp.concatenate([jnp.sum(dcur * e[5 + k:5 + k + tm, :], axis=0, keepdims=True) for k in range(4)], axis=0)
        db = jnp.sum(dcur, axis=0, keepdims=True)

        @pl.when(i == 0)
        def _():
            dw_ref[...] = dw
            db_ref[...] = db

        @pl.when(i > 0)
        def _():
            dw_ref[...] += dw
            db_ref[...] += db

    blk = lambda f: pl.BlockSpec((tm, cb), f)
    return _pcall(body, grid=(C // cb, nt),
                  in_specs=[blk(lambda j, i: (jnp.maximum(i - 1, 0), j)), blk(lambda j, i: (i, j)), blk(lambda j, i: (jnp.minimum(i + 1, nt - 1), j)),
                            pl.BlockSpec((4, cb), lambda j, i: (0, j)), pl.BlockSpec((1, cb), lambda j, i: (0, j)),
                            blk(lambda j, i: (i, j)), blk(lambda j, i: (jnp.minimum(i + 1, nt - 1), j))],
                  out_specs=[blk(lambda j, i: (i, j)), pl.BlockSpec((4, cb), lambda j, i: (0, j)), pl.BlockSpec((1, cb), lambda j, i: (0, j))],
                  out_shape=[_S((T, C)), _S((4, C)), _S((1, C))], name=name)(x, x, x, w, b, dact, dact)


FOX_SCALE = 64 ** -0.5


def _fox_scores(q, k, ccol, crow, hp, e, i, j, tq):
    lane = _lane()
    me = ((lane < 64) if e == 0 else (lane >= 64)).astype(F32)
    hh = 2 * hp + e
    cq = jnp.sum(ccol * (lane == FF0 + hh).astype(F32), axis=1, keepdims=True)
    ck = jnp.sum(crow * (_iota((8, 1), 0) == hh).astype(F32), axis=0, keepdims=True)
    s = _dotb((q * me).astype(BF), k.astype(BF), NT) * FOX_SCALE + cq - ck
    ok = (i * tq + _iota((tq, tq), 0)) >= (j * tq + _iota((tq, tq), 1))
    return jnp.where(ok, s, -jnp.inf), me, hh


def fox_fwd(qkv, ccol, crow, name, rider=None):
    T = qkv.shape[0]
    tq = min(T, 512)
    nq = T // tq

    def body(q_ref, k_ref, v_ref, cc_ref, cr_ref, o_ref, lse_ref, m_s, l_s, acc):
        hp, i, j = pl.program_id(0), pl.program_id(1), pl.program_id(2)

        @pl.when(j == 0)
        def _():
            m_s[...] = jnp.full(m_s.shape, -jnp.inf, F32)
            l_s[...] = jnp.zeros(l_s.shape, F32)
            acc[...] = jnp.zeros(acc.shape, F32)

        @pl.when(j <= i)
        def _():
            q, k, v = q_ref[...], k_ref[...], v_ref[...]
            a = acc[...]
            scale, add = 0.0, 0.0
            for e in range(2):
                s, me, _ = _fox_scores(q, k, cc_ref[...], cr_ref[...], hp, e, i, j, tq)
                m_old = m_s[e]
                m_new = jnp.maximum(m_old, jnp.max(s, axis=1, keepdims=True))
                p = jnp.exp(s - m_new)
                al = jnp.exp(m_old - m_new)
                l_s[e] = al * l_s[e] + jnp.sum(p, axis=1, keepdims=True)
                m_s[e] = m_new
                scale = scale + al * me
                add = add + _dotb(p.astype(BF), (v * me).astype(BF), NN)
            acc[...] = a * scale + add

        @pl.when(j == i)
        def _():
            lane = _lane()
            m0 = (lane < 64).astype(F32)
            o_ref[...] = acc[...] / (l_s[0] * m0 + l_s[1] * (1.0 - m0))
            lse_ref[...] = (m_s[0] + jnp.log(l_s[0])) * (lane == 0).astype(F32) + (m_s[1] + jnp.log(l_s[1])) * (lane == 1).astype(F32)

    qspec = lambda off: pl.BlockSpec((tq, LANES), lambda hp, i, j: (i, off + hp))
    kspec = lambda off: pl.BlockSpec((tq, LANES), lambda hp, i, j: (jnp.minimum(j, i), off + hp))
    return hosted_call(body, rider, grid=(4, nq, nq),
                       in_specs=[qspec(0), kspec(4), kspec(8), pl.BlockSpec((tq, LANES), lambda hp, i, j: (i, 0)),
                                 pl.BlockSpec((8, tq), lambda hp, i, j: (0, jnp.minimum(j, i)))],
                       out_specs=[pl.BlockSpec((tq, LANES), lambda hp, i, j: (i, hp)), pl.BlockSpec((None, tq, LANES), lambda hp, i, j: (hp, i, 0))],
                       out_shape=[_S((T, BRANCH_W)), _S((4, T, LANES))],
                       scratch_shapes=[pltpu.VMEM((2, tq, 1), F32), pltpu.VMEM((2, tq, 1), F32), pltpu.VMEM((tq, LANES), F32)], name=name,
                       args=[qkv, qkv, qkv, ccol, crow])


def fox_bwd(qkv, ccol, crow, o, lse, do, name, rider=None):
    T = qkv.shape[0]
    tq = min(T, 512)
    nq = T // tq

    def body(q_ref, k_ref, v_ref, cc_ref, cr_ref, o_ref, lse_ref, do_ref, dq_ref, dk_ref, dv_ref, dcc_ref, dcr_ref):
        hp, j, ii = pl.program_id(0), pl.program_id(1), pl.program_id(2)
        i = jnp.maximum(ii, j)

        @pl.when((hp == 0) & (j == 0) & (ii == 0))
        def _():
            dcc_ref[...] = jnp.zeros(dcc_ref.shape, F32)
            dcr_ref[...] = jnp.zeros(dcr_ref.shape, F32)

        @pl.when((j == 0) & (ii == 0))
        def _():
            dq_ref[...] = jnp.zeros(dq_ref.shape, F32)

        @pl.when(ii == 0)
        def _():
            dk_ref[...] = jnp.zeros(dk_ref.shape, F32)
            dv_ref[...] = jnp.zeros(dv_ref.shape, F32)

        @pl.when(ii >= j)
        def _():
            q, k, v, dov = q_ref[...], k_ref[...], v_ref[...], do_ref[...]
            lane = _lane()
            dq, dk, dv, dcc, dcr = 0.0, 0.0, 0.0, 0.0, 0.0
            dd = dov * o_ref[...]
            for e in range(2):
                s, me, hh = _fox_scores(q, k, cc_ref[...], cr_ref[...], hp, e, i, j, tq)
                lse_e = jnp.sum(lse_ref[...] * (lane == e).astype(F32), axis=1, keepdims=True)
                p = jnp.exp(s - lse_e)
                dom = (dov * me).astype(BF)
                dp = _dotb(dom, v.astype(BF), NT)
                ds = p * (dp - jnp.sum(dd * me, axis=1, keepdims=True))
                dsb = ds.astype(BF)
                dv = dv + _dotb(p.astype(BF), dom, TN)
                dk = dk + _dotb(dsb, (q * me).astype(BF), TN) * FOX_SCALE
                dq = dq + _dotb(dsb, (k * me).astype(BF), NN) * FOX_SCALE
                dcc = dcc + jnp.sum(ds, axis=1, keepdims=True) * (lane == FF0 + hh).astype(F32)
                dcr = dcr - jnp.sum(ds, axis=0, keepdims=True) * (_iota((8, 1), 0) == hh).astype(F32)
            rows = pl.ds(pl.multiple_of(i * tq, tq), tq)
            dq_ref[rows, :] += dq
            dcc_ref[rows, :] += dcc
            dk_ref[...] += dk
            dv_ref[...] += dv
            dcr_ref[:, pl.ds(pl.multiple_of(j * tq, tq), tq)] += dcr

    ispec = lambda off: pl.BlockSpec((tq, LANES), lambda hp, j, ii: (jnp.maximum(ii, j), off + hp))
    jspec = lambda off: pl.BlockSpec((tq, LANES), lambda hp, j, ii: (j, off + hp))
    return hosted_call(body, rider, grid=(4, nq, nq),
                       in_specs=[ispec(0), jspec(4), jspec(8), pl.BlockSpec((tq, LANES), lambda hp, j, ii: (jnp.maximum(ii, j), 0)),
                                 pl.BlockSpec((8, tq), lambda hp, j, ii: (0, j)), ispec(0),
                                 pl.BlockSpec((None, tq, LANES), lambda hp, j, ii: (hp, jnp.maximum(ii, j), 0)), ispec(0)],
                       out_specs=[pl.BlockSpec((T, LANES), lambda hp, j, ii: (0, hp)), jspec(0), jspec(0),
                                  pl.BlockSpec((T, LANES), lambda hp, j, ii: (0, 0)), pl.BlockSpec((8, T), lambda hp, j, ii: (0, 0))],
                       out_shape=[_S((T, BRANCH_W)), _S((T, BRANCH_W)), _S((T, BRANCH_W)), _S((T, LANES)), _S((8, T))],
                       scratch_shapes=[], name=name, args=[qkv, qkv, qkv, ccol, crow, o, lse, do])


def _acc_out(ref, val, first):
    @pl.when(first)
    def _():
        ref[...] = val

    @pl.when(jnp.logical_not(first))
    def _():
        ref[...] += val


def _row(tm, c):
    return pl.BlockSpec((tm, c), lambda i: (i, 0))


def _full(shape):
    return pl.BlockSpec(shape, lambda *_: (0,) * len(shape))


def ln_fwd(x, g, b, name):
    T, C = x.shape
    tm = min(T, 512)

    def body(x_ref, g_ref, b_ref, o_ref):
        o_ref[...] = _ln(x_ref[...], g_ref[...], b_ref[...])

    return _pcall(body, grid=(T // tm,), in_specs=[_row(tm, C), _full((1, C)), _full((1, C))], out_specs=_row(tm, C),
                  out_shape=_S((T, C)), name=name)(x, g, b)


def ln_bwd(x, g, b, dy, name):
    T, C = x.shape
    tm = min(T, 512)

    def body(x_ref, g_ref, b_ref, dy_ref, dx_ref, dg_ref, db_ref):
        _, vjp = jax.vjp(_ln, x_ref[...], g_ref[...], b_ref[...])
        dx, dg, db = vjp(dy_ref[...])
        dx_ref[...] = dx
        first = pl.program_id(0) == 0
        _acc_out(dg_ref, dg, first)
        _acc_out(db_ref, db, first)

    return _pcall(body, grid=(T // tm,), in_specs=[_row(tm, C), _full((1, C)), _full((1, C)), _row(tm, C)],
                  out_specs=[_row(tm, C), _full((1, C)), _full((1, C))], out_shape=[_S((T, C)), _S((1, C)), _S((1, C))], name=name)(x, g, b, dy)


def loss_head(h, target, name):
    T, C = h.shape
    tm = min(T, 512)

    def body(h_ref, t_ref, d_ref, l_ref):
        e = h_ref[...] - t_ref[...]
        d_ref[...] = e * (1.0 / C)
        part = jnp.sum(jnp.sum(e * e, axis=1, keepdims=True), axis=0, keepdims=True) * (0.5 / C)
        _acc_out(l_ref, part, pl.program_id(0) == 0)

    return _pcall(body, grid=(T // tm,), in_specs=[_row(tm, C), _row(tm, C)], out_specs=[_row(tm, C), _full((1, 1))],
                  out_shape=[_S((T, C)), _S((1, 1))], name=name)(h, target)


def add3(a, b, c, name):
    T, C = a.shape
    tm = min(T, 512)

    def body(a_ref, b_ref, c_ref, o_ref):
        o_ref[...] = a_ref[...] + b_ref[...] + c_ref[...]

    return _pcall(body, grid=(T // tm,), in_specs=[_row(tm, C)] * 3, out_specs=_row(tm, C), out_shape=_S((T, C)), name=name)(a, b, c)


def _wb_spec(l):
    return pl.BlockSpec((None, 4, BRANCH_W, D_MODEL), lambda *_: (l, 0, 0, 0))


def merge_fwd(ys, gl, gb, wb, l, name):
    T = gl.shape[0]
    tm = min(T, 256)

    def body(y0, y1, y2, y3, gl_ref, gb_ref, wb_ref, o_ref):
        acc = 0.0
        for i, y in enumerate((y0, y1, y2, y3)):
            z = _dotb(y[...].astype(BF), wb_ref[i], NN)
            g = jax.nn.sigmoid(gl_ref[:, i * D_MODEL:(i + 1) * D_MODEL] + gb_ref[i:i + 1, :])
            acc = acc + g * z
        o_ref[...] = acc

    return _pcall(body, grid=(T // tm,), in_specs=[_row(tm, BRANCH_W)] * 4 + [_row(tm, 4 * D_MODEL), _full((4, D_MODEL)), _wb_spec(l)],
                  out_specs=_row(tm, D_MODEL), out_shape=_S((T, D_MODEL)), name=name)(*ys, gl, gb, wb)


def merge_bwd(ys, gl, gb, wb, l, dm, name):
    T = gl.shape[0]
    tm = min(T, 256)

    def body(y0, y1, y2, y3, gl_ref, gb_ref, wb_ref, dm_ref, d0, d1, d2, d3, dgl_ref, dz_ref, dgb_ref):
        dmv = dm_ref[...]
        first = pl.program_id(0) == 0
        for i, (y, d) in enumerate(zip((y0, y1, y2, y3), (d0, d1, d2, d3))):
            cols = slice(i * D_MODEL, (i + 1) * D_MODEL)
            z = _dotb(y[...].astype(BF), wb_ref[i], NN)
            g = jax.nn.sigmoid(gl_ref[:, cols] + gb_ref[i:i + 1, :])
            dgl = dmv * z * (g * (1.0 - g))
            dz = (g * dmv).astype(BF)
            dgl_ref[:, cols] = dgl
            dz_ref[:, cols] = dz
            d[...] = _dotb(dz, wb_ref[i], NT)
            _acc_out(dgb_ref.at[i:i + 1, :], jnp.sum(dgl, axis=0, keepdims=True), first)

    return _pcall(body, grid=(T // tm,),
                  in_specs=[_row(tm, BRANCH_W)] * 4 + [_row(tm, 4 * D_MODEL), _full((4, D_MODEL)), _wb_spec(l), _row(tm, D_MODEL)],
                  out_specs=[_row(tm, BRANCH_W)] * 4 + [_row(tm, 4 * D_MODEL), _row(tm, 4 * D_MODEL), _full((4, D_MODEL))],
                  out_shape=[_S((T, BRANCH_W))] * 4 + [_S((T, 4 * D_MODEL)), _S((T, 4 * D_MODEL), BF), _S((4, D_MODEL))], name=name)(
                      *ys, gl, gb, wb, dm)


def _wout_spec(l):
    return pl.BlockSpec((None, D_MODEL, D_MODEL), lambda *_: (l, 0, 0))


def out_fwd(merged, h, wout, l, g, b, name):
    T = h.shape[0]
    tm = min(T, 512)

    def body(m_ref, h_ref, w_ref, g_ref, b_ref, u_ref, o_ref):
        u = ALPHA * h_ref[...] + _dotb(m_ref[...].astype(BF), w_ref[...], NN)
        u_ref[...] = u
        o_ref[...] = _ln(u, g_ref[...], b_ref[...])

    C = D_MODEL
    return _pcall(body, grid=(T // tm,), in_specs=[_row(tm, C), _row(tm, C), _wout_spec(l), _full((1, C)), _full((1, C))],
                  out_specs=[_row(tm, C), _row(tm, C)], out_shape=[_S((T, C)), _S((T, C))], name=name)(merged, h, wout, g, b)


def out_bwd(u, dy, g, b, wout, l, name):
    T, C = u.shape
    tm = min(T, 512)

    def body(u_ref, dy_ref, g_ref, b_ref, w_ref, du_ref, dm_ref, dg_ref, db_ref):
        _, vjp = jax.vjp(_ln, u_ref[...], g_ref[...], b_ref[...])
        du, dg, db = vjp(dy_ref[...])
        du_ref[...] = du
        dm_ref[...] = _dotb(du.astype(BF), w_ref[...], NT)
        first = pl.program_id(0) == 0
        _acc_out(dg_ref, dg, first)
        _acc_out(db_ref, db, first)

    return _pcall(body, grid=(T // tm,), in_specs=[_row(tm, C), _row(tm, C), _full((1, C)), _full((1, C)), _wout_spec(l)],
                  out_specs=[_row(tm, C), _row(tm, C), _full((1, C)), _full((1, C))],
                  out_shape=[_S((T, C)), _S((T, C)), _S((1, C)), _S((1, C))], name=name)(u, dy, g, b, wout)


def ff_fwd(h, wup, wdown, l, g, b, name):
    T, C = h.shape
    F = wup.shape[2]
    tm, tf = min(T, 512), 1024
    nf = F // tf

    def body(h_ref, wu_ref, wd_ref, g_ref, b_ref, u_ref, o_ref, acc):
        f = pl.program_id(1)
        a = _dotb(h_ref[...].astype(BF), wu_ref[...], NN)
        r = jnp.square(jnp.maximum(a, 0.0))
        p = _dotb(r.astype(BF), wd_ref[...], NN)
        _acc_out(acc, p, f == 0)

        @pl.when(f == nf - 1)
        def _():
            u = ALPHA * h_ref[...] + acc[...]
            u_ref[...] = u
            o_ref[...] = _ln(u, g_ref[...], b_ref[...])

    row = pl.BlockSpec((tm, C), lambda i, f: (i, 0))
    return _pcall(body, grid=(T // tm, nf),
                  in_specs=[row, pl.BlockSpec((None, C, tf), lambda i, f: (l, 0, f)), pl.BlockSpec((None, tf, C), lambda i, f: (l, f, 0)),
                            _full((1, C)), _full((1, C))],
                  out_specs=[row, row], out_shape=[_S((T, C)), _S((T, C))], scratch_shapes=[pltpu.VMEM((tm, C), F32)], name=name)(h, wup, wdown, g, b)


def ff_bwd(u, dy, h, g, b, wup, wdown, l, name):
    T, C = h.shape
    F = wup.shape[2]
    tm, tf = min(T, 512), 1024
    nf = F // tf

    def body(u_ref, dy_ref, h_ref, g_ref, b_ref, wu_ref, wd_ref, du_ref, dh_ref, da_ref, r_ref, dg_ref, db_ref, du_s, acc):
        i, f = pl.program_id(0), pl.program_id(1)

        @pl.when(f == 0)
        def _():
            _, vjp = jax.vjp(_ln, u_ref[...], g_ref[...], b_ref[...])
            du, dg, db = vjp(dy_ref[...])
            du_s[...] = du
            du_ref[...] = du
            _acc_out(dg_ref, dg, i == 0)
            _acc_out(db_ref, db, i == 0)

        a = _dotb(h_ref[...].astype(BF), wu_ref[...], NN)
        ap = jnp.maximum(a, 0.0)
        dr = _dotb(du_s[...].astype(BF), wd_ref[...], NT)
        da = (dr * (2.0 * ap)).astype(BF)
        da_ref[...] = da
        r_ref[...] = jnp.square(ap).astype(BF)
        _acc_out(acc, _dotb(da, wu_ref[...], NT), f == 0)

        @pl.when(f == nf - 1)
        def _():
            dh_ref[...] = ALPHA * du_s[...] + acc[...]

    row = pl.BlockSpec((tm, C), lambda i, f: (i, 0))
    colf = pl.BlockSpec((tm, tf), lambda i, f: (i, f))
    return _pcall(body, grid=(T // tm, nf),
                  in_specs=[row, row, row, _full((1, C)), _full((1, C)), pl.BlockSpec((None, C, tf), lambda i, f: (l, 0, f)),
                            pl.BlockSpec((None, tf, C), lambda i, f: (l, f, 0))],
                  out_specs=[row, row, colf, colf, _full((1, C)), _full((1, C))],
                  out_shape=[_S((T, C)), _S((T, C)), _S((T, F), BF), _S((T, F), BF), _S((1, C)), _S((1, C))],
                  scratch_shapes=[pltpu.VMEM((tm, C), F32), pltpu.VMEM((tm, C), F32)], name=name)(u, dy, h, g, b, wup, wdown)


MESH_ID = pl.DeviceIdType.MESH
_ANY = pl.BlockSpec(memory_space=pl.ANY)


def _window(ref, ax, idx, n):
    if n < 0:
        return ref
    sel = idx if n == 0 else pl.ds(pl.multiple_of(idx * n, n), n)
    return ref.at[(slice(None),) * ax + (sel,)]


Rider = collections.namedtuple("Rider", "operands out_shape scratch start wait")


def hosted_call(body, rider, *, grid, in_specs, out_specs, out_shape, scratch_shapes, name, args):
    n_in, n_out, n_scr = len(in_specs), len(out_specs), len(scratch_shapes)
    if rider is None:
        return _pcall(body, grid=grid, in_specs=in_specs, out_specs=out_specs, out_shape=out_shape, scratch_shapes=scratch_shapes, name=name)(*args), []
    ri, ro = len(rider.operands), len(rider.out_shape)

    def wrapped(*refs):
        ins, r_in = refs[:n_in], refs[n_in:n_in + ri]
        o0 = n_in + ri
        outs, r_out = refs[o0:o0 + n_out], refs[o0 + n_out:o0 + n_out + ro]
        s0 = o0 + n_out + ro
        scr, r_scr = refs[s0:s0 + n_scr], refs[s0 + n_scr:]
        ids = [pl.program_id(i) for i in range(len(grid))]
        first = functools.reduce(jnp.logical_and, [i == 0 for i in ids])
        last = functools.reduce(jnp.logical_and, [i == g - 1 for i, g in zip(ids, grid)])

        @pl.when(first)
        def _():
            rider.start(r_in, r_out, r_scr)

        body(*ins, *outs, *scr)

        @pl.when(last)
        def _():
            rider.wait(r_in, r_out, r_scr)

    res = _pcall(wrapped, grid=grid, in_specs=list(in_specs) + [_ANY] * ri, out_specs=list(out_specs) + [_ANY] * ro,
                 out_shape=list(out_shape) + list(rider.out_shape), scratch_shapes=list(scratch_shapes) + list(rider.scratch),
                 name=name)(*args, *rider.operands)
    return res[:n_out], res[n_out:]


def comm_call(rider, name):
    ri = len(rider.operands)

    def body(*refs):
        r_in, r_out, r_scr = refs[:ri], refs[ri:ri + len(rider.out_shape)], refs[ri + len(rider.out_shape):]
        rider.start(r_in, r_out, r_scr)
        rider.wait(r_in, r_out, r_scr)

    return _pcall(body, in_specs=[_ANY] * ri, out_specs=[_ANY] * len(rider.out_shape), out_shape=list(rider.out_shape),
                  scratch_shapes=list(rider.scratch), name=name)(*rider.operands)


def gather_rider(shards, axes):
    K = len(shards)
    widths = [s.shape[a] for s, a in zip(shards, axes)]
    out_shape = [_S(s.shape[:a] + (N_DEV * s.shape[a],) + s.shape[a + 1:], s.dtype) for s, a in zip(shards, axes)]

    def plan(x_refs, o_refs, sems):
        send_sems, recv_sems, local_sems = sems
        mx, my, mc = lax.axis_index("x"), lax.axis_index("y"), lax.axis_index("c")
        me, sibling = (mx, my, mc), (mx, my, 1 - mc)
        chips = [(1 - mx, my), (mx, 1 - my), (1 - mx, 1 - my)]

        def win(k, px, py, pc):
            return _window(o_refs[k], axes[k], 4 * px + 2 * py + pc, widths[k])

        def copy(k, slot, block, to, src=None):
            return pltpu.make_async_remote_copy(src_ref=win(k, *block) if src is None else src, dst_ref=win(k, *block),
                                                send_sem=send_sems.at[7 * k + slot], recv_sem=recv_sems.at[7 * k + slot],
                                                device_id=to, device_id_type=MESH_ID)

        mine = [pltpu.make_async_copy(x_refs[k], win(k, *me), local_sems.at[k]) for k in range(K)]
        first = []
        for k in range(K):
            first.append(copy(k, 0, me, sibling, src=x_refs[k]))
            first += [copy(k, 1 + j, me, (*chip, mc), src=x_refs[k]) for j, chip in enumerate(chips)]
        return me, sibling, chips, copy, mine, first

    def start(x_refs, o_refs, sems):
        _, _, _, _, mine, first = plan(x_refs, o_refs, sems)
        for cp in mine + first:
            cp.start()

    def wait(x_refs, o_refs, sems):
        me, sibling, chips, copy, mine, first = plan(x_refs, o_refs, sems)
        mc = me[2]
        passed = []
        for j, chip in enumerate(chips):
            for k in range(K):
                copy(k, 1 + j, (*chip, mc), me).wait_recv()
                passed.append(copy(k, 4 + j, (*chip, mc), sibling))
                passed[-1].start()
        for k in range(K):
            copy(k, 0, sibling, me).wait_recv()
        for j, chip in enumerate(chips):
            for k in range(K):
                copy(k, 4 + j, (*chip, 1 - mc), me).wait_recv()
        for cp in first + passed:
            cp.wait_send()
        for cp in mine:
            cp.wait()

    scratch = [pltpu.SemaphoreType.DMA((7 * K,)), pltpu.SemaphoreType.DMA((7 * K,)), pltpu.SemaphoreType.DMA((K,))]
    return Rider(list(shards), out_shape, scratch, start, wait)


def exchange_rider(items):
    ns = len(items)
    out_shape = [_S((N_DEV,) + tuple(it[3]), it[0].dtype) for it in items]

    def plan(src_refs, o_refs, sems):
        send_sems, recv_sems, local_sems = sems
        mx, my, mc = lax.axis_index("x"), lax.axis_index("y"), lax.axis_index("c")
        me = 4 * mx + 2 * my + mc
        remote, own = [], []
        for s, (_, ax, n, _) in enumerate(items):
            own.append(pltpu.make_async_copy(_window(src_refs[s], ax, me, n), o_refs[s].at[me], local_sems.at[s]))
            for k in range(1, N_DEV):
                px = 1 - mx if k & 4 else mx
                py = 1 - my if k & 2 else my
                pc = 1 - mc if k & 1 else mc
                remote.append(pltpu.make_async_remote_copy(
                    src_ref=_window(src_refs[s], ax, 4 * px + 2 * py + pc, n), dst_ref=o_refs[s].at[me],
                    send_sem=send_sems.at[7 * s + k - 1], recv_sem=recv_sems.at[7 * s + k - 1],
                    device_id=(px, py, pc), device_id_type=MESH_ID))
        return remote, own

    def start(src_refs, o_refs, sems):
        remote, own = plan(src_refs, o_refs, sems)
        for cp in own + remote:
            cp.start()

    def wait(src_refs, o_refs, sems):
        remote, own = plan(src_refs, o_refs, sems)
        for cp in remote + own:
            cp.wait()

    scratch = [pltpu.SemaphoreType.DMA((7 * ns,)), pltpu.SemaphoreType.DMA((7 * ns,)), pltpu.SemaphoreType.DMA((ns,))]
    return Rider([it[0] for it in items], out_shape, scratch, start, wait)


def reduce_adamw(rcvs, w, m, v, name):
    L = len(rcvs)
    _, A, B, C = rcvs[0].shape
    tb = B
    while tb > 8 and tb * C > (1 << 17):
        tb //= 2

    def body(*refs):
        r_refs, (w_ref, m_ref, v_ref, g_ref, d_ref, mo_ref, vo_ref) = refs[:L], refs[L:]
        for k in range(L):
            @pl.when(pl.program_id(0) == k)
            def _(k=k):
                g = r_refs[k][0].astype(F32)
                for d in range(1, N_DEV):
                    g = g + r_refs[k][d].astype(F32)
                mn = ADAM_B1 * m_ref[...] + (1.0 - ADAM_B1) * g
                vn = ADAM_B2 * v_ref[...] + (1.0 - ADAM_B2) * jnp.square(g)
                m_hat = mn / (1.0 - ADAM_B1 ** ADAM_STEP)
                v_hat = vn / (1.0 - ADAM_B2 ** ADAM_STEP)
                g_ref[...] = g
                d_ref[...] = -ADAM_LR * (m_hat / (jnp.sqrt(v_hat) + ADAM_EPS) + ADAM_WD * w_ref[...])
                mo_ref[...] = mn
                vo_ref[...] = vn

    def rspec(k):
        return pl.BlockSpec((N_DEV, None, tb, C), lambda l, a, i: (0, jnp.where(l == k, a, 0), jnp.where(l == k, i, 0), 0))

    blk = pl.BlockSpec((None, tb, C), lambda l, a, i: (l * A + a, i, 0))
    return _pcall(body, grid=(L, A, B // tb), in_specs=[rspec(k) for k in range(L)] + [blk, blk, blk],
                  out_specs=[blk] * 4, out_shape=[_S((L * A, B, C))] * 4, name=name)(*rcvs, w, m, v)


def _w_in_pieces(g0, g1):
    per = D_IN // N_DEV
    return [(d, max(g0, d * per) - d * per, min(g1, (d + 1) * per) - d * per) for d in range(N_DEV) if max(g0, d * per) < min(g1, (d + 1) * per)]


def repack_w_in(w8, name):
    _, L, R, per = w8.shape
    tr = 256

    def cols(x_ref, g0, g1):
        return [x_ref[d, :, a:b] for d, a, b in _w_in_pieces(g0, g1)]

    def body(x_ref, *o_refs):
        for (name_, i), o_ref in zip(SEGS, o_refs):
            o_ref[...] = jnp.concatenate(cols(x_ref, _OFF[i], _OFF[i + 1]), axis=1)
        parts, at = [], 0
        for i, lane0 in SMALL_SRC:
            assert lane0 == at
            parts += cols(x_ref, _OFF[i], _OFF[i + 1])
            at += IN_SIZES[i]
        parts.append(jnp.zeros((tr, LANES - at), w8.dtype))
        o_refs[-1][...] = jnp.concatenate(parts, axis=1)

    widths = [IN_SIZES[i] for _, i in SEGS] + [LANES]
    outs = _pcall(body, grid=(L, R // tr), in_specs=[pl.BlockSpec((N_DEV, None, tr, per), lambda l, r: (0, l, r, 0))],
                  out_specs=[pl.BlockSpec((None, tr, w), lambda l, r: (l, r, 0)) for w in widths],
                  out_shape=[_S((L, R, w), w8.dtype) for w in widths], name=name)(w8)
    return dict(zip(SEG_NAMES, outs))


def repack_dw_in(dseg, name):
    R = dseg["z"].shape[0]
    per = D_IN // N_DEV
    tr = 128
    src = {i: (k, 0) for k, (_, i) in enumerate(SEGS)}
    src.update({i: (len(SEGS), lane0) for i, lane0 in SMALL_SRC})

    def body(*refs):
        s_refs, o_ref = refs[:-1], refs[-1]
        for d in range(N_DEV):
            parts = []
            for i in range(len(IN_SIZES)):
                g0, g1 = max(_OFF[i], d * per), min(_OFF[i + 1], (d + 1) * per)
                if g0 < g1:
                    k, c0 = src[i]
                    parts.append(s_refs[k][:, c0 + g0 - _OFF[i]:c0 + g1 - _OFF[i]])
            o_ref[d] = jnp.concatenate(parts, axis=1)

    arrs = [dseg[n] for n in SEG_NAMES]
    return _pcall(body, grid=(R // tr,), in_specs=[pl.BlockSpec((tr, a.shape[1]), lambda r: (r, 0)) for a in arrs],
                  out_specs=pl.BlockSpec((N_DEV, tr, per), lambda r: (0, r, 0)), out_shape=_S((N_DEV, R, per), arrs[0].dtype), name=name)(*arrs)


WEIGHTS = ("ln_in_g", "ln_in_b", "w_in", "ssd_conv_w", "ssd_conv_b", "ssd_dt_bias", "ssd_a_log", "ssd_d", "ssd_norm_w", "dn_conv_w",
           "dn_a_log", "dn_dt_bias", "dn_norm_w", "sg_ln_g", "sg_ln_b", "sg_w", "sg_b", "fox_f_bias", "gate_b", "w_branch", "w_out",
           "ln1_g", "ln1_b", "w_up", "w_down", "ln2_g", "ln2_b")
SHARDED = {"w_in": 2, "ssd_conv_w": 2, "dn_conv_w": 2, "gate_b": 2, "w_branch": 3, "w_out": 1, "w_up": 2, "w_down": 1}
SLABBED = ("w_in", "dn_conv_w")
MATMUL_WEIGHTS = ("w_in", "w_branch", "w_out", "w_up", "w_down")
REPLICATED = tuple(n for n in WEIGHTS if n not in SHARDED)
SEG_NAMES = tuple(n for n, _ in SEGS) + ("small",)
PACK_COLS = 1024


def _lanes(vec, off):
    return jnp.pad(vec, (off, LANES - off - vec.shape[0]))[None]


def _pack_small(parts):
    flat = jnp.concatenate([q.reshape(-1) for q in parts])
    rows = -(-flat.shape[0] // (PACK_COLS * 64)) * 64
    return jnp.pad(flat, (0, rows * PACK_COLS - flat.shape[0])).reshape(1, rows, PACK_COLS)


EARLY = ("w_branch", "w_out", "w_up", "w_down", "gate_b")
LATE = ("w_in", "ssd_conv_w", "dn_conv_w")


def _gather_rider(p, l):
    shards, axes = [], []
    for n in SHARDED:
        s = p[n][l:l + 1]
        s = s.astype(BF) if n in MATMUL_WEIGHTS else s
        shards.append(s[None] if n in SLABBED else s)
        axes.append(0 if n in SLABBED else SHARDED[n])
    return gather_rider(shards, axes)


def _exchange_items(g, p, names):
    items = []
    for n in names:
        local = p[n].shape[1:]
        items.append((g[n], 0, 0, local) if n in SLABBED else (g[n], SHARDED[n] - 1, local[SHARDED[n] - 1], local))
    return items


def _layer_weights(p, gathered, l):
    full = dict(zip(SHARDED, gathered))
    w = {n: full[n] for n in ("w_branch", "w_out", "w_up", "w_down")}
    w["w_in"] = repack_w_in(full["w_in"], f"w_in_repack_{l}")
    w["ssd_cw"] = full["ssd_conv_w"][0]
    w["ssd_cb"] = p["ssd_conv_b"][l][None]
    w["dn_cw"] = jnp.moveaxis(full["dn_conv_w"][:, 0], 0, 1).reshape(4, 3 * BRANCH_W)
    w["dn_cb"] = jnp.zeros((1, 3 * BRANCH_W), F32)
    w["gate_b"] = full["gate_b"][0]
    w["ssd_ps"] = [_lanes(p["ssd_dt_bias"][l], DT0), _lanes(p["ssd_a_log"][l], DT0), _lanes(p["ssd_d"][l], DT0), p["ssd_norm_w"][l][None]]
    w["dn_ps"] = [_lanes(p["dn_a_log"][l], A0), _lanes(p["dn_dt_bias"][l], A0), p["dn_norm_w"][l][None]]
    w["sg_ps"] = [p["sg_ln_g"][l][None], p["sg_ln_b"][l][None], p["sg_w"][l], jnp.pad(p["sg_b"][l].T, ((0, 0), (0, LANES - 4)))]
    w["fox_ps"] = [_lanes(p["fox_f_bias"][l], FF0)]
    for n in ("ln1_g", "ln1_b", "ln2_g", "ln2_b"):
        w[n] = p[n][l][None]
    return w


def _scan_specs(T, a):
    c0 = lambda c, h: (c, 0)
    ssd = dict(f=ssd_chunk, xs=[(a["z"], (128, 512), c0), (a["xbc_act"], (128, 1024), c0), (a["small"], (128, LANES), c0)],
               ys=[((T, BRANCH_W), (128, BRANCH_W), c0)], state=(4, LANES, LANES), nc=T // 128, nh=1, shared=())
    dn = dict(f=dn_chunk, xs=[(a["dn_act"], (64, 3 * BRANCH_W), c0), (a["dngate"], (64, BRANCH_W), c0), (a["small"], (64, LANES), c0)],
              ys=[((T, BRANCH_W), (64, BRANCH_W), c0)], state=(4, LANES, LANES), nc=T // 64, nh=1, shared=())
    sg = dict(f=sg_chunk, xs=[(a["sguv"], (128, 1024), c0)], ys=[((T, BRANCH_W), (128, BRANCH_W), c0)], state=(1, 8, LANES), nc=T // 128, nh=1, shared=())
    fc = dict(f=foxc_chunk, xs=[(a["small"], (128, LANES), c0)],
              ys=[((T, LANES), (128, LANES), c0), ((8, T), (8, 128), lambda c, h: (0, c))], state=(1, 1, LANES), nc=T // 128, nh=1, shared=())
    return ssd, dn, sg, fc


def _layer_fwd(h, w, l, rider=None):
    T = h.shape[0]
    a = {"h": h}
    for n in SEG_NAMES:
        a[n] = matmul_w(h, w["w_in"][n], 0, "nn", f"proj_{n}_{l}")
    a["xbc_act"] = conv_fwd(a["xbc"], w["ssd_cw"], w["ssd_cb"], f"ssd_conv_{l}")
    a["dn_act"] = conv_fwd(a["dnqkv"], w["dn_cw"], w["dn_cb"], f"dn_conv_{l}")
    ssd, dn, sg, fc = _scan_specs(T, a)
    a["ya"], a["ssd_st"] = scan_fwd(f"ssd_fwd_{l}", ssd["f"], ssd["xs"], w["ssd_ps"], ssd["ys"], ssd["state"], ssd["nc"], ssd["nh"])
    a["yb"], a["dn_st"] = scan_fwd(f"dn_fwd_{l}", dn["f"], dn["xs"], w["dn_ps"], dn["ys"], dn["state"], dn["nc"], dn["nh"])
    a["yc"], a["sg_st"] = scan_fwd(f"sg_fwd_{l}", sg["f"], sg["xs"], w["sg_ps"], sg["ys"], sg["state"], sg["nc"], sg["nh"])
    a["ccol"], a["crow"], a["fc_st"] = scan_fwd(f"foxc_fwd_{l}", fc["f"], fc["xs"], w["fox_ps"], fc["ys"], fc["state"], fc["nc"], fc["nh"])
    (a["yd"], a["lse"]), carried = fox_fwd(a["foxqkv"], a["ccol"], a["crow"], f"fox_fwd_{l}", rider=rider)
    a["merged"] = merge_fwd([a["ya"], a["yb"], a["yc"], a["yd"]], a["gates"], w["gate_b"], w["w_branch"], 0, f"merge_fwd_{l}")
    a["u1"], a["h1"] = out_fwd(a["merged"], h, w["w_out"], 0, w["ln1_g"], w["ln1_b"], f"out_fwd_{l}")
    a["u2"], a["h2"] = ff_fwd(a["h1"], w["w_up"], w["w_down"], 0, w["ln2_g"], w["ln2_b"], f"ff_fwd_{l}")
    return a, carried


def _layer_bwd(dh2, a, w, l, p, late_above):
    T = dh2.shape[0]
    g = {}
    du2, dh1, da, r, dg2, db2 = ff_bwd(a["u2"], dh2, a["h1"], w["ln2_g"], w["ln2_b"], w["w_up"], w["w_down"], 0, f"ff_bwd_{l}")
    g["ln2_g"], g["ln2_b"] = dg2[0], db2[0]
    g["w_up"] = matmul_tn(a["h1"], da, f"dwup_{l}", out_dtype=BF)
    g["w_down"] = matmul_tn(r, du2, f"dwdown_{l}", out_dtype=BF)
    du1, dmerged, dg1, db1 = out_bwd(a["u1"], dh1, w["ln1_g"], w["ln1_b"], w["w_out"], 0, f"out_bwd_{l}")
    g["ln1_g"], g["ln1_b"] = dg1[0], db1[0]
    g["w_out"] = matmul_tn(a["merged"], du1, f"dwout_{l}", out_dtype=BF)
    ys = [a["ya"], a["yb"], a["yc"], a["yd"]]
    dya, dyb, dyc, dyd, dgl, dz, dgb = merge_bwd(ys, a["gates"], w["gate_b"], w["w_branch"], 0, dmerged, f"merge_bwd_{l}")
    g["gate_b"] = dgb
    g["w_branch"] = jnp.stack([matmul_tn(ys[i], dz, f"dwb{i}_{l}", b_col0=i * D_MODEL, n_cols=D_MODEL, out_dtype=BF) for i in range(4)])
    early = exchange_rider(_exchange_items(g, p, EARLY))
    dn_rider = early if late_above is None else exchange_rider(late_above)
    fox_rider = None if late_above is None else early
    ssd, dn, sg, fc = _scan_specs(T, a)
    (dz_ssd, dxbc_act, dsm_ssd, d_dtb, d_alog, d_dsk, d_nw), _ = scan_bwd(f"ssd_bwd_{l}", ssd["f"], ssd["xs"], w["ssd_ps"], ssd["ys"], [dya], a["ssd_st"],
                                                                           ssd["state"], ssd["nc"], ssd["nh"])
    g["ssd_dt_bias"], g["ssd_a_log"], g["ssd_d"], g["ssd_norm_w"] = d_dtb[0, DT0:DT0 + 8], d_alog[0, DT0:DT0 + 8], d_dsk[0, DT0:DT0 + 8], d_nw[0]
    dxbc, g["ssd_conv_w"], dcb = conv_bwd(a["xbc"], w["ssd_cw"], w["ssd_cb"], dxbc_act, f"ssd_conv_bwd_{l}")
    g["ssd_conv_b"] = dcb[0]
    (ddn_act, ddngate, dsm_dn, d_alog, d_dtb, d_nw), got_dn = scan_bwd(f"dn_bwd_{l}", dn["f"], dn["xs"], w["dn_ps"], dn["ys"], [dyb], a["dn_st"],
                                                                        dn["state"], dn["nc"], dn["nh"], rider=dn_rider)
    g["dn_a_log"], g["dn_dt_bias"], g["dn_norm_w"] = d_alog[0, A0:A0 + 4], d_dtb[0, A0:A0 + 4], d_nw[0]
    ddnqkv, g["dn_conv_w"], _ = conv_bwd(a["dnqkv"], w["dn_cw"], w["dn_cb"], ddn_act, f"dn_conv_bwd_{l}")
    (dsguv, d_lng, d_lnb, d_w, d_bt), _ = scan_bwd(f"sg_bwd_{l}", sg["f"], sg["xs"], w["sg_ps"], sg["ys"], [dyc], a["sg_st"], sg["state"], sg["nc"], sg["nh"])
    g["sg_ln_g"], g["sg_ln_b"], g["sg_w"], g["sg_b"] = d_lng[0], d_lnb[0], d_w, d_bt[:, :4].T
    (dfq, dfk, dfv, dccol, dcrow), got_fox = fox_bwd(a["foxqkv"], a["ccol"], a["crow"], a["yd"], a["lse"], dyd, f"fox_bwd_{l}", rider=fox_rider)
    (dsm_fox, d_fb), _ = scan_bwd(f"foxc_bwd_{l}", fc["f"], fc["xs"], w["fox_ps"], fc["ys"], [dccol, dcrow], a["fc_st"], fc["state"], fc["nc"], fc["nh"])
    g["fox_f_bias"] = d_fb[0, FF0:FF0 + 8]
    dseg = {"z": dz_ssd, "xbc": dxbc, "dnqkv": ddnqkv, "dngate": ddngate, "sguv": dsguv,
            "foxqkv": jnp.concatenate([dfq, dfk, dfv], axis=1), "gates": dgl, "small": add3(dsm_ssd, dsm_dn, dsm_fox, f"dsmall_{l}")}
    dh, scale, dwin = du1, ALPHA, {}
    for n in SEG_NAMES:
        dh = matmul_w(dseg[n], w["w_in"][n], 0, "nt", f"dh_{n}_{l}", add=dh, add_scale=scale)
        scale = 1.0
        dwin[n] = matmul_tn(a["h"], dseg[n], f"dwin_{n}_{l}", out_dtype=BF)
    g["w_in"] = repack_dw_in(dwin, f"dw_in_repack_{l}")
    g["dn_conv_w"] = jnp.moveaxis(g["dn_conv_w"].reshape(4, N_DEV, 3 * BRANCH_W // N_DEV), 1, 0)
    got = {(EARLY, l): got_dn} if late_above is None else {(LATE, l + 1): got_dn, (EARLY, l): got_fox}
    return dh, g, got


def kernel(x, ln_in_g, ln_in_b, w_in, ssd_conv_w, ssd_conv_b, ssd_dt_bias, ssd_a_log, ssd_d, ssd_norm_w, dn_conv_w, dn_a_log, dn_dt_bias, dn_norm_w, sg_ln_g, sg_ln_b, sg_w, sg_b, fox_f_bias, gate_b, w_branch, w_out, ln1_g, ln1_b, w_up, w_down, ln2_g, ln2_b, loss_target, m_ln_in_g, m_ln_in_b, m_w_in, m_ssd_conv_w, m_ssd_conv_b, m_ssd_dt_bias, m_ssd_a_log, m_ssd_d, m_ssd_norm_w, m_dn_conv_w, m_dn_a_log, m_dn_dt_bias, m_dn_norm_w, m_sg_ln_g, m_sg_ln_b, m_sg_w, m_sg_b, m_fox_f_bias, m_gate_b, m_w_branch, m_w_out, m_ln1_g, m_ln1_b, m_w_up, m_w_down, m_ln2_g, m_ln2_b, v_ln_in_g, v_ln_in_b, v_w_in, v_ssd_conv_w, v_ssd_conv_b, v_ssd_dt_bias, v_ssd_a_log, v_ssd_d, v_ssd_norm_w, v_dn_conv_w, v_dn_a_log, v_dn_dt_bias, v_dn_norm_w, v_sg_ln_g, v_sg_ln_b, v_sg_w, v_sg_b, v_fox_f_bias, v_gate_b, v_w_branch, v_w_out, v_ln1_g, v_ln1_b, v_w_up, v_w_down, v_ln2_g, v_ln2_b):
    args = dict(locals())
    p = {n: args[n] for n in WEIGHTS}
    xt, target = x[0], loss_target[0]
    gathered = comm_call(_gather_rider(p, 0), "weights_all_gather_0")
    h = ln_fwd(xt, ln_in_g[None], ln_in_b[None], "ln_in_fwd")
    ws, acts = [], []
    for l in range(DEPTH):
        ws.append(_layer_weights(p, gathered, l))
        a, gathered = _layer_fwd(h, ws[l], l, rider=_gather_rider(p, l + 1) if l + 1 < DEPTH else None)
        acts.append(a)
        h = a["h2"]
    dh, loss = loss_head(h, target, "loss_head")
    loss = lax.psum(loss[0, 0], ("x", "y", "c"))

    layer_grads, got, late = [None] * DEPTH, {}, None
    for l in reversed(range(DEPTH)):
        dh, layer_grads[l], got_l = _layer_bwd(dh, acts[l], ws[l], l, p, late)
        got.update(got_l)
        late = _exchange_items(layer_grads[l], p, LATE)
    grad_x, dg_in, db_in = ln_bwd(xt, ln_in_g[None], ln_in_b[None], dh, "ln_in_bwd")
    small = {n: jnp.stack([layer_grads[l][n] for l in range(DEPTH)]) for n in REPLICATED if n not in ("ln_in_g", "ln_in_b")}
    small["ln_in_g"], small["ln_in_b"] = dg_in[0], db_in[0]
    pack = _pack_small([small[n] for n in REPLICATED])
    last = comm_call(exchange_rider(late + [(pack[0], 0, -1, pack.shape[1:])]), "grads_exchange_last")
    got[(LATE, 0)] = last[:-1]
    rcv = {(n, l): arr for (names, l), arrs in got.items() for n, arr in zip(names, arrs)}

    res = [{}, {}, {}, {}]
    for n in SHARDED:
        shp = p[n].shape
        lead = math.prod(shp[1:-2])
        to3 = lambda t: t.reshape((-1,) + shp[-2:])
        outs = reduce_adamw([rcv[(n, l)].reshape((N_DEV, lead) + shp[-2:]) for l in range(DEPTH)],
                            to3(p[n]), to3(args["m_" + n]), to3(args["v_" + n]), f"adamw_{n}")
        for k in range(4):
            res[k][n] = outs[k].reshape(shp)
    outs = reduce_adamw([last[-1][:, None]], _pack_small([p[n] for n in REPLICATED]), _pack_small([args["m_" + n] for n in REPLICATED]),
                        _pack_small([args["v_" + n] for n in REPLICATED]), "adamw_replicated")
    off = 0
    for n in REPLICATED:
        shp = p[n].shape
        cnt = math.prod(shp)
        for k in range(4):
            res[k][n] = outs[k].reshape(-1)[off:off + cnt].reshape(shp)
        off += cnt
    return (loss, grad_x[None], *[res[0][n] for n in WEIGHTS], *[res[1][n] for n in WEIGHTS],
            *[res[2][n] for n in WEIGHTS], *[res[3][n] for n in WEIGHTS])
```

```python
import collections
import functools
import math

import jax
import jax.numpy as jnp
from jax import lax
from jax.experimental import pallas as pl
from jax.experimental.pallas import tpu as pltpu

F32 = jnp.float32
BF = jnp.bfloat16

D_MODEL = 1024
DEPTH = 2
BRANCH_W = 512
D_FF = 4096
LN_EPS = 1e-5
NORM_EPS = 1e-6
ALPHA = (2 * DEPTH) ** 0.25
N_DEV = 8
LANES = 128
ADAM_LR, ADAM_B1, ADAM_B2, ADAM_EPS, ADAM_WD, ADAM_STEP = 0.001, 0.9, 0.999, 1e-08, 0.01, 10

DT0, BETA0, A0, FF0 = 0, 8, 12, 16
IN_SIZES = (512, 1024, 8, 1536, 4, 4, 512, 1024, 1536, 8, 4096)
_OFF = [0]
for _s in IN_SIZES:
    _OFF.append(_OFF[-1] + _s)
D_IN = _OFF[-1]
SEGS = (("z", 0), ("xbc", 1), ("dnqkv", 3), ("dngate", 6), ("sguv", 7), ("foxqkv", 8), ("gates", 10))
SMALL_SRC = ((2, DT0), (4, BETA0), (5, A0), (9, FF0))

NN = ((1,), (0,))
NT = ((1,), (1,))
TN = ((0,), (0,))
_DIMS = {"nn": NN, "nt": NT, "tn": TN}


def _pcall(body, **kw):
    return pl.pallas_call(body, **kw)


def _S(shape, dtype=F32):
    return jax.ShapeDtypeStruct(tuple(shape), dtype)


def _iota(shape, dim):
    return lax.broadcasted_iota(jnp.int32, shape, dim)


def _dotb(a, b, dims):
    return lax.dot_general(a, b, (dims, ((), ())), preferred_element_type=F32)


def _split2(a):
    ah = a.astype(BF)
    return ah, (a - ah.astype(F32)).astype(BF)


def _split3(a):
    a1 = a.astype(BF)
    r = a - a1.astype(F32)
    a2 = r.astype(BF)
    a3 = (r - a2.astype(F32)).astype(BF)
    return a1, a2, a3


def _mm_raw(a, b, form, mode):
    d = _DIMS[form]
    if mode == "1":
        return _dotb(a.astype(BF), b.astype(BF), d)
    if mode == "3":
        ah, al = _split2(a)
        bh, bl = _split2(b)
        return _dotb(ah, bh, d) + (_dotb(ah, bl, d) + _dotb(al, bh, d))
    if mode == "xa":
        ab = a.astype(BF)
        b1, b2, b3 = _split3(b)
        return _dotb(ab, b1, d) + (_dotb(ab, b2, d) + _dotb(ab, b3, d))
    bb = b.astype(BF)
    a1, a2, a3 = _split3(a)
    return _dotb(a1, bb, d) + (_dotb(a2, bb, d) + _dotb(a3, bb, d))


@functools.partial(jax.custom_vjp, nondiff_argnums=(2, 3))
def mm(a, b, form, mode):
    return _mm_raw(a, b, form, mode)


def _mm_fwd(a, b, form, mode):
    return _mm_raw(a, b, form, mode), (a, b)


_XA_DB = {"nn": "xa", "nt": "xb", "tn": "xa"}
_XB_DA = {"nn": "xb", "nt": "xb", "tn": "xa"}


def _mm_bwd(form, mode, res, g):
    a, b = res
    ma = _XB_DA[form] if mode == "xb" else mode
    mb = _XA_DB[form] if mode == "xa" else mode
    da = db = None
    if mode != "xa":
        da = {"nn": lambda: mm(g, b, "nt", ma), "nt": lambda: mm(g, b, "nn", ma), "tn": lambda: mm(b, g, "nt", ma)}[form]()
    if mode != "xb":
        db = {"nn": lambda: mm(a, g, "tn", mb), "nt": lambda: mm(g, a, "tn", mb), "tn": lambda: mm(a, g, "nn", mb)}[form]()
    if da is None:
        da = jnp.zeros_like(a)
    if db is None:
        db = jnp.zeros_like(b)
    return da, db


mm.defvjp(_mm_fwd, _mm_bwd)


def _silu(x):
    return x * jax.nn.sigmoid(x)


def _ln(x, g, b):
    mu = jnp.mean(x, -1, keepdims=True)
    xc = x - mu
    var = jnp.mean(xc * xc, -1, keepdims=True)
    return xc * lax.rsqrt(var + LN_EPS) * g + b


def _pick(n, cap):
    if n <= cap:
        return n
    best = LANES
    for t in range(LANES, cap + 1, LANES):
        if n % t == 0:
            best = t
    return best


def matmul_w(a, w, l, form, name, add=None, add_scale=1.0):
    M, K = a.shape
    N = w.shape[2] if form == "nn" else w.shape[1]
    tm, tn, tk = min(M, 512), _pick(N, 1024), _pick(K, 1024)
    nk = K // tk

    def body(*refs):
        if add is None:
            a_ref, b_ref, o_ref, acc = refs
        else:
            a_ref, b_ref, d_ref, o_ref, acc = refs
        k = pl.program_id(2)
        p = _dotb(a_ref[...].astype(BF), b_ref[...].astype(BF), _DIMS[form])

        @pl.when(k == 0)
        def _():
            acc[...] = p

        @pl.when(k > 0)
        def _():
            acc[...] += p

        @pl.when(k == nk - 1)
        def _():
            r = acc[...]
            if add is not None:
                r = r + add_scale * d_ref[...]
            o_ref[...] = r

    if form == "nn":
        wspec = pl.BlockSpec((None, tk, tn), lambda j, i, k: (l, k, j))
    else:
        wspec = pl.BlockSpec((None, tn, tk), lambda j, i, k: (l, j, k))
    in_specs = [pl.BlockSpec((tm, tk), lambda j, i, k: (i, k)), wspec]
    args = [a, w]
    if add is not None:
        in_specs.append(pl.BlockSpec((tm, tn), lambda j, i, k: (i, j)))
        args.append(add)
    return _pcall(body, grid=(N // tn, M // tm, nk), in_specs=in_specs,
                  out_specs=pl.BlockSpec((tm, tn), lambda j, i, k: (i, j)), out_shape=_S((M, N)),
                  scratch_shapes=[pltpu.VMEM((tm, tn), F32)], name=name)(*args)


def matmul_tn(a, b, name, b_col0=0, n_cols=None, out_dtype=F32):
    T, M = a.shape
    N = b.shape[1] if n_cols is None else n_cols
    tm, tn, tt = _pick(M, 512), _pick(N, 1024), min(T, 512)
    nt = T // tt
    jb = b_col0 // tn

    def body(a_ref, b_ref, o_ref, acc):
        t = pl.program_id(2)
        p = _dotb(a_ref[...].astype(BF), b_ref[...].astype(BF), TN)

        @pl.when(t == 0)
        def _():
            acc[...] = p

        @pl.when(t > 0)
        def _():
            acc[...] += p

        @pl.when(t == nt - 1)
        def _():
            o_ref[...] = acc[...].astype(out_dtype)

    return _pcall(body, grid=(M // tm, N // tn, nt),
                  in_specs=[pl.BlockSpec((tt, tm), lambda i, j, t: (t, i)), pl.BlockSpec((tt, tn), lambda i, j, t: (t, jb + j))],
                  out_specs=pl.BlockSpec((tm, tn), lambda i, j, t: (i, j)), out_shape=_S((M, N), out_dtype),
                  scratch_shapes=[pltpu.VMEM((tm, tn), F32)], name=name)(a, b)


def _pieces(v):
    if v.ndim == 3:
        return [v[i] for i in range(v.shape[0])]
    n = v.shape[1] // LANES
    if n <= 1:
        return [v]
    return [v[:, i * LANES:(i + 1) * LANES] for i in range(n)]


def _join(ps, like_ndim):
    if like_ndim == 3:
        return jnp.stack(ps, axis=0)
    return ps[0] if len(ps) == 1 else jnp.concatenate(ps, axis=1)


def scan_fwd(name, f, xs, ps, ys, state_shape, nc, nh=1):
    nx, npar, ny = len(xs), len(ps), len(ys)

    def body(*refs):
        x_refs, p_refs = refs[:nx], refs[nx:nx + npar]
        y_refs = refs[nx + npar:nx + npar + ny]
        st_out, st = refs[nx + npar + ny], refs[nx + npar + ny + 1]
        c, h = pl.program_id(0), pl.program_id(1)

        @pl.when(c == 0)
        def _():
            st[h] = jnp.zeros(state_shape, F32)

        S = st[h]
        st_out[...] = S
        yv, Sn = f([_pieces(r[...]) for r in x_refs], [_pieces(r[...]) for r in p_refs], _pieces(S), h)
        for r, v in zip(y_refs, yv):
            r[...] = _join(v, 2)
        st[h] = _join(Sn, 3)

    in_specs = [pl.BlockSpec(bs, im) for (_, bs, im) in xs]
    in_specs += [pl.BlockSpec(p.shape, (lambda c, h, n=p.ndim: (0,) * n)) for p in ps]
    out_specs = [pl.BlockSpec(bs, im) for (_, bs, im) in ys]
    out_specs.append(pl.BlockSpec((None, None) + tuple(state_shape), lambda c, h: (c, h, 0, 0, 0)))
    out_shape = [_S(s) for (s, _, _) in ys] + [_S((nc, nh) + tuple(state_shape))]
    return _pcall(body, grid=(nc, nh), in_specs=in_specs, out_specs=out_specs, out_shape=out_shape,
                  scratch_shapes=[pltpu.VMEM((nh,) + tuple(state_shape), F32)], name=name)(*[x[0] for x in xs], *ps)


def scan_bwd(name, f, xs, ps, ys, dys, states, state_shape, nc, nh=1, shared=(), rider=None):
    nx, npar, ny = len(xs), len(ps), len(ys)

    def body(*refs):
        x_refs, p_refs = refs[:nx], refs[nx:nx + npar]
        s_ref = refs[nx + npar]
        dy_refs = refs[nx + npar + 1:nx + npar + 1 + ny]
        o = nx + npar + 1 + ny
        dx_refs, dp_refs, dst = refs[o:o + nx], refs[o + nx:o + nx + npar], refs[o + nx + npar]
        c, h = pl.program_id(0), pl.program_id(1)

        @pl.when(c == 0)
        def _():
            dst[h] = jnp.zeros(state_shape, F32)

        @pl.when((c == 0) & (h == 0))
        def _():
            for r in dp_refs:
                r[...] = jnp.zeros(r.shape, F32)

        xv = [_pieces(r[...]) for r in x_refs]
        pv = [_pieces(r[...]) for r in p_refs]
        _, vjp = jax.vjp(lambda a, b, s: f(a, b, s, h), xv, pv, _pieces(s_ref[...]))
        dxv, dpv, dS = vjp(([_pieces(r[...]) for r in dy_refs], _pieces(dst[h])))
        for i, (r, v) in enumerate(zip(dx_refs, dxv)):
            if i in shared and nh > 1:
                @pl.when(h == 0)
                def _(r=r, v=v):
                    r[...] = _join(v, 2)

                @pl.when(h > 0)
                def _(r=r, v=v):
                    r[...] += _join(v, 2)
            else:
                r[...] = _join(v, 2)
        for r, v in zip(dp_refs, dpv):
            r[...] += _join(v, len(r.shape))
        dst[h] = _join(dS, 3)

    def rev(im):
        return lambda c, h: im(nc - 1 - c, h)

    in_specs = [pl.BlockSpec(bs, rev(im)) for (_, bs, im) in xs]
    in_specs += [pl.BlockSpec(p.shape, (lambda c, h, n=p.ndim: (0,) * n)) for p in ps]
    in_specs.append(pl.BlockSpec((None, None) + tuple(state_shape), lambda c, h: (nc - 1 - c, h, 0, 0, 0)))
    in_specs += [pl.BlockSpec(bs, rev(im)) for (_, bs, im) in ys]
    out_specs = [pl.BlockSpec(bs, rev(im)) for (_, bs, im) in xs]
    out_specs += [pl.BlockSpec(p.shape, (lambda c, h, n=p.ndim: (0,) * n)) for p in ps]
    out_shape = [_S(x[0].shape) for x in xs] + [_S(p.shape) for p in ps]
    return hosted_call(body, rider, grid=(nc, nh), in_specs=in_specs, out_specs=out_specs, out_shape=out_shape,
                       scratch_shapes=[pltpu.VMEM((nh,) + tuple(state_shape), F32)], name=name,
                       args=[*[x[0] for x in xs], *ps, states, *dys])


def _lane():
    return _iota((1, LANES), 1)


def _col(v, idx):
    return jnp.sum(v * (_lane() == idx).astype(F32), axis=1, keepdims=True)


def _last_row(v):
    r = v.shape[0]
    return jnp.sum(v * (_iota((r, 1), 0) == r - 1).astype(F32), axis=0, keepdims=True)


def _tril(n, strict=False):
    r, c = _iota((n, n), 0), _iota((n, n), 1)
    return (r > c) if strict else (r >= c)


def ssd_chunk(xs, ps, S, h):
    zp, xbc, (sm,) = xs
    (bias,), (alog,), (dsk,), nw = ps
    Q = sm.shape[0]
    lane = _lane()
    a128 = jnp.where(lane < 8, -jnp.exp(alog), 0.0)
    dtl = jax.nn.softplus(sm + bias)
    tri = _tril(Q)
    trif = tri.astype(F32)
    cum = mm(trif, dtl * a128, "nn", "xa")
    sel8 = (_iota((8, LANES), 0) == _iota((8, LANES), 1)).astype(F32)
    cum_t = mm(sel8, cum, "nt", "xa")
    m0 = (lane < 64).astype(F32)
    rows0 = (_iota((LANES, 1), 0) < 64).astype(F32)
    yz, Sn, ssq, cb = [], [], 0.0, None
    for pr in range(4):
        g = pr // 2
        Bg, Cg = xbc[4 + g], xbc[6 + g]
        if pr % 2 == 0:
            cb = mm(Cg, Bg, "nt", "1")
        xp, Hp = xbc[pr], S[pr]
        ypair, stpair, rowdec = 0.0, 0.0, 0.0
        for e in range(2):
            hh = 2 * pr + e
            me = m0 if e == 0 else 1.0 - m0
            re = rows0 if e == 0 else 1.0 - rows0
            col = _col(cum, hh)
            row = jnp.sum(cum_t * (_iota((8, 1), 0) == hh).astype(F32), axis=0, keepdims=True)
            xh = xp * me
            xdt = xh * _col(dtl, hh)
            seg = jnp.exp(jnp.where(tri, col - row, -jnp.inf))
            last = _last_row(col)
            ypair = ypair + mm(cb * seg, xdt, "nn", "1") + mm(Cg * jnp.exp(col), Hp, "nt", "1") * me + _col(dsk, hh) * xh
            stpair = stpair + mm(xdt, Bg * jnp.exp(last - col), "tn", "1")
            rowdec = rowdec + jnp.exp(last) * re
        Sn.append(Hp * rowdec + stpair)
        v = ypair * _silu(zp[pr])
        ssq = ssq + jnp.sum(v * v, axis=1, keepdims=True)
        yz.append(v)
    scale = lax.rsqrt(ssq / BRANCH_W + NORM_EPS)
    return [[yz[i] * scale * nw[i] for i in range(4)]], Sn


def dn_chunk(xs, ps, S, h):
    act, gate, (sm,) = xs
    (alog,), (dtb,), (nw,) = ps
    H = range(4)
    C = sm.shape[0]
    lane = _lane()
    G = jnp.where((lane >= A0) & (lane < A0 + 4), -jnp.exp(alog) * jax.nn.softplus(sm + dtb), 0.0)
    tri, strict = _tril(C), _tril(C, True)
    gcs = mm(tri.astype(F32), G, "nn", "xa")
    sig = jax.nn.sigmoid(sm)
    qn = [act[h] * lax.rsqrt(jnp.sum(act[h] * act[h], axis=1, keepdims=True) + NORM_EPS) * (LANES ** -0.5) for h in H]
    kn = [act[4 + h] * lax.rsqrt(jnp.sum(act[4 + h] * act[4 + h], axis=1, keepdims=True) + NORM_EPS) for h in H]
    beta = [_col(sig, BETA0 + h) for h in H]
    gcol = [_col(gcs, A0 + h) for h in H]
    selr = [((_iota((8, LANES), 0) == 0) & (_iota((8, LANES), 1) == A0 + h)).astype(F32) for h in H]
    grow = [jnp.sum(mm(selr[h], gcs, "nt", "xa"), axis=0, keepdims=True) for h in H]
    gamma = [jnp.exp(jnp.where(tri, gcol[h] - grow[h], -jnp.inf)) for h in H]
    kb = [kn[h] * beta[h] for h in H]
    pk = [-(mm(kb[h], kn[h], "nt", "1") * jnp.where(strict, gamma[h], 0.0)) for h in H]
    eye = (_iota((C, C), 0) == _iota((C, C), 1)).astype(F32)
    minv = [eye + pk[h] for h in H]
    for _ in range(5):
        pk = [mm(pk[h], pk[h], "nn", "3") for h in H]
        minv = [minv[h] + mm(minv[h], pk[h], "nn", "3") for h in H]
    eg = [jnp.exp(gcol[h]) for h in H]
    w = [mm(minv[h], kb[h] * eg[h], "nn", "3") for h in H]
    u = [mm(minv[h], act[8 + h] * beta[h], "nn", "3") for h in H]
    glast = [_last_row(gcol[h]) for h in H]
    vnew = [u[h] - mm(w[h], S[h], "nn", "1") for h in H]
    qk = [mm(qn[h], kn[h], "nt", "1") * gamma[h] for h in H]
    o = [mm(qn[h] * eg[h], S[h], "nn", "1") + mm(qk[h], vnew[h], "nn", "1") for h in H]
    Sn = [S[h] * jnp.exp(glast[h]) + mm(kn[h] * jnp.exp(glast[h] - gcol[h]), vnew[h], "tn", "1") for h in H]
    on = [o[h] * lax.rsqrt(jnp.mean(o[h] * o[h], axis=1, keepdims=True) + NORM_EPS) * nw for h in H]
    return [[on[h] * _silu(gate[h]) for h in H]], Sn


def sg_chunk(xs, ps, S, h):
    (uv,) = xs
    lng, lnb, W, (bt,) = ps
    u = [jax.nn.gelu(p) for p in uv[:4]]
    v = [jax.nn.gelu(p) for p in uv[4:]]
    mu = sum(jnp.sum(p, axis=1, keepdims=True) for p in v) / BRANCH_W
    vc = [p - mu for p in v]
    var = sum(jnp.sum(p * p, axis=1, keepdims=True) for p in vc) / BRANCH_W
    inv = lax.rsqrt(var + LN_EPS)
    trif = _tril(W[0].shape[0]).astype(F32)
    out = []
    for g in range(4):
        vn = vc[g] * inv * lng[g] + lnb[g]
        out.append(u[g] * (mm(W[g] * trif, vn, "nn", "1") + _col(bt, g)))
    return [out], S


def foxc_chunk(xs, ps, S, h):
    (sm,), ((fb,),), (carry,) = xs[0], ps, S
    lane = _lane()
    ls = jnp.where((lane >= FF0) & (lane < FF0 + 8), jax.nn.log_sigmoid(sm + fb), 0.0)
    c = mm(_tril(sm.shape[0]).astype(F32), ls, "nn", "xa") + carry
    return [[c]], [_last_row(c)]


def _conv_tiles(T, C):
    return min(T, 512), _pick(C, 512)


def conv_fwd(x, w, b, name):
    T, C = x.shape
    tm, cb = _conv_tiles(T, C)

    def body(xp_ref, x_ref, w_ref, b_ref, o_ref):
        i = pl.program_id(1)
        tail = xp_ref[tm - 8:tm, :] * (i > 0).astype(F32)
        e = jnp.concatenate([tail, x_ref[...]], axis=0)
        pre = b_ref[...] + sum(w_ref[k:k + 1, :] * e[5 + k:5 + k + tm, :] for k in range(4))
        o_ref[...] = _silu(pre)

    return _pcall(body, grid=(C // cb, T // tm),
                  in_specs=[pl.BlockSpec((tm, cb), lambda j, i: (jnp.maximum(i - 1, 0), j)), pl.BlockSpec((tm, cb), lambda j, i: (i, j)),
                            pl.BlockSpec((4, cb), lambda j, i: (0, j)), pl.BlockSpec((1, cb), lambda j, i: (0, j))],
                  out_specs=pl.BlockSpec((tm, cb), lambda j, i: (i, j)), out_shape=_S((T, C)), name=name)(x, x, w, b)


def conv_bwd(x, w, b, dact, name):
    T, C = x.shape
    tm, cb = _conv_tiles(T, C)
    nt = T // tm

    def body(xp_ref, x_ref, xn_ref, w_ref, b_ref, d_ref, dn_ref, dx_ref, dw_ref, db_ref):
        i = pl.program_id(1)
        has_prev, has_next = (i > 0).astype(F32), (i < nt - 1).astype(F32)
        e = jnp.concatenate([xp_ref[tm - 8:tm, :] * has_prev, x_ref[...], xn_ref[0:8, :] * has_next], axis=0)
        pre = b_ref[...] + sum(w_ref[k:k + 1, :] * e[5 + k:5 + k + tm + 8, :] for k in range(4))
        de = jnp.concatenate([d_ref[...], dn_ref[0:8, :] * has_next], axis=0)
        sg = jax.nn.sigmoid(pre)
        dpre = de * (sg * (1.0 + pre * (1.0 - sg)))
        dx_ref[...] = sum(w_ref[k:k + 1, :] * dpre[3 - k:3 - k + tm, :] for k in range(4))
        dcur = dpre[0:tm, :]
        dw = jnp.concatenate([jnp.sum(dcur * e[5 + k:5 + k + tm, :], axis=0, keepdims=True) for k in range(4)], axis=0)
        db = jnp.sum(dcur, axis=0, keepdims=True)

        @pl.when(i == 0)
        def _():
            dw_ref[...] = dw
            db_ref[...] = db

        @pl.when(i > 0)
        def _():
            dw_ref[...] += dw
            db_ref[...] += db

    blk = lambda f: pl.BlockSpec((tm, cb), f)
    return _pcall(body, grid=(C // cb, nt),
                  in_specs=[blk(lambda j, i: (jnp.maximum(i - 1, 0), j)), blk(lambda j, i: (i, j)), blk(lambda j, i: (jnp.minimum(i + 1, nt - 1), j)),
                            pl.BlockSpec((4, cb), lambda j, i: (0, j)), pl.BlockSpec((1, cb), lambda j, i: (0, j)),
                            blk(lambda j, i: (i, j)), blk(lambda j, i: (jnp.minimum(i + 1, nt - 1), j))],
                  out_specs=[blk(lambda j, i: (i, j)), pl.BlockSpec((4, cb), lambda j, i: (0, j)), pl.BlockSpec((1, cb), lambda j, i: (0, j))],
                  out_shape=[_S((T, C)), _S((4, C)), _S((1, C))], name=name)(x, x, x, w, b, dact, dact)


FOX_SCALE = 64 ** -0.5
LOG2E = 1.4426950408889634


def _spare(e, i):
    return (_lane() == 64 * (1 - e) + i).astype(F32)


def _lanes_of(e):
    lane = _lane()
    return ((lane < 64) if e == 0 else (lane >= 64)).astype(F32)


def _col3(col, e, first):
    c1 = col.astype(BF).astype(F32)
    c2 = (col - c1).astype(BF).astype(F32)
    c3 = (col - c1 - c2).astype(BF).astype(F32)
    return c1 * _spare(e, first) + c2 * _spare(e, first + 1) + c3 * _spare(e, first + 2)


def _ones3(e, first):
    return _spare(e, first) + _spare(e, first + 1) + _spare(e, first + 2)


def _c_col(cc, hh):
    return jnp.sum(cc * (_lane() == FF0 + hh).astype(F32), axis=1, keepdims=True) * LOG2E


def _pair_spec(tq, row_of):
    return pl.BlockSpec((None, 2, tq, LANES), lambda hp, a, b: (hp, 0, row_of(a, b), 0))


def fox_prep(qkv, ccol, name):
    T = qkv.shape[0]
    tq = min(T, 512)

    def body(q_ref, k_ref, v_ref, cc_ref, qa_ref, ka_ref, va_ref):
        hp = pl.program_id(0)
        q, k, v, cc = q_ref[...], k_ref[...], v_ref[...], cc_ref[...]
        for e in range(2):
            me = _lanes_of(e)
            c2 = _c_col(cc, 2 * hp + e)
            qa_ref[e] = (q * me * (FOX_SCALE * LOG2E) + _col3(c2, e, 0) + _ones3(e, 3)).astype(BF)
            ka_ref[e] = (k * me + _ones3(e, 0) + _col3(-c2, e, 3) + _ones3(e, 6)).astype(BF)
            va_ref[e] = (v * me + (1.0 - me)).astype(BF)

    blk = lambda off: pl.BlockSpec((tq, LANES), lambda hp, i: (i, off + hp))
    out = pl.BlockSpec((None, 2, tq, LANES), lambda hp, i: (hp, 0, i, 0))
    return _pcall(body, grid=(4, T // tq), in_specs=[blk(0), blk(4), blk(8), pl.BlockSpec((tq, LANES), lambda hp, i: (i, 0))],
                  out_specs=[out] * 3, out_shape=[_S((4, 2, T, LANES), BF)] * 3, name=name)(qkv, qkv, qkv, ccol)


def fox_fwd(qa, ka, va, name, rider=None):
    T = qa.shape[2]
    tq = min(T, 512)
    nq = T // tq

    def body(qa_ref, ka_ref, va_ref, o_ref, lse_ref, m_s, acc):
        i, j = pl.program_id(1), pl.program_id(2)

        @pl.when(j == 0)
        def _():
            m_s[...] = jnp.full(m_s.shape, -jnp.inf, F32)
            acc[...] = jnp.zeros(acc.shape, F32)

        def step(diagonal):
            for e in range(2):
                s = _dotb(qa_ref[e], ka_ref[e], NT)
                if diagonal:
                    s = jnp.where(_iota((tq, tq), 0) >= _iota((tq, tq), 1), s, -jnp.inf)
                m_old = m_s[e]
                m_new = jnp.maximum(m_old, jnp.max(s, axis=1, keepdims=True))
                p = jnp.exp2(s - m_new)
                m_s[e] = m_new
                acc[e] = acc[e] * jnp.exp2(m_old - m_new) + _dotb(p.astype(BF), va_ref[e], NN)

        @pl.when(j < i)
        def _():
            step(False)

        @pl.when(j == i)
        def _():
            step(True)
            lane = _lane()
            o, lse = 0.0, 0.0
            for e in range(2):
                l = jnp.sum(acc[e] * _spare(e, 0), axis=1, keepdims=True)
                o = o + acc[e] * _lanes_of(e) / l
                lse = lse + (m_s[e] + jnp.log2(l)) * (lane == e).astype(F32)
            o_ref[...] = o
            lse_ref[...] = lse

    kv = _pair_spec(tq, lambda i, j: jnp.minimum(j, i))
    return hosted_call(body, rider, grid=(4, nq, nq), in_specs=[_pair_spec(tq, lambda i, j: i), kv, kv],
                       out_specs=[pl.BlockSpec((tq, LANES), lambda hp, i, j: (i, hp)), pl.BlockSpec((None, tq, LANES), lambda hp, i, j: (hp, i, 0))],
                       out_shape=[_S((T, BRANCH_W)), _S((4, T, LANES))],
                       scratch_shapes=[pltpu.VMEM((2, tq, 1), F32), pltpu.VMEM((2, tq, LANES), F32)], name=name, args=[qa, ka, va])


def fox_prep_bwd(qkv, qa, o, lse, do, name):
    T = qkv.shape[0]
    tq = min(T, 512)

    def body(q_ref, k_ref, qa_ref, o_ref, lse_ref, do_ref, qb_ref, doa_ref, qd_ref, kd_ref):
        q, k, dov = q_ref[...], k_ref[...], do_ref[...]
        dd = dov * o_ref[...]
        lane = _lane()
        for e in range(2):
            me = _lanes_of(e)
            lse_e = jnp.sum(lse_ref[...] * (lane == e).astype(F32), axis=1, keepdims=True)
            qb_ref[e] = (qa_ref[e].astype(F32) + _col3(-lse_e, e, 6)).astype(BF)
            doa_ref[e] = (dov * me + _col3(-jnp.sum(dd * me, axis=1, keepdims=True), e, 0)).astype(BF)
            qd_ref[e] = (q * me * FOX_SCALE + _spare(e, 0)).astype(BF)
            kd_ref[e] = (k * me * FOX_SCALE + _spare(e, 0)).astype(BF)

    blk = lambda off: pl.BlockSpec((tq, LANES), lambda hp, i: (i, off + hp))
    pair = pl.BlockSpec((None, 2, tq, LANES), lambda hp, i: (hp, 0, i, 0))
    return _pcall(body, grid=(4, T // tq),
                  in_specs=[blk(0), blk(4), pair, blk(0), pl.BlockSpec((None, tq, LANES), lambda hp, i: (hp, i, 0)), blk(0)],
                  out_specs=[pair] * 4, out_shape=[_S((4, 2, T, LANES), BF)] * 4, name=name)(qkv, qkv, qa, o, lse, do)


def fox_bwd(qb, ka, va, doa, qd, kd, name, rider=None):
    T = qb.shape[2]
    tq = min(T, 512)
    nq = T // tq

    def body(qb_ref, ka_ref, va_ref, doa_ref, qd_ref, kd_ref, dq_ref, dk_ref, dv_ref, dcc_ref, dq_s, dk_s, dv_s):
        hp, j, ii = pl.program_id(0), pl.program_id(1), pl.program_id(2)
        i = jnp.maximum(ii, j)

        @pl.when((hp == 0) & (j == 0) & (ii == 0))
        def _():
            dcc_ref[...] = jnp.zeros(dcc_ref.shape, F32)

        @pl.when((j == 0) & (ii == 0))
        def _():
            dq_s[...] = jnp.zeros(dq_s.shape, F32)

        @pl.when(ii == 0)
        def _():
            dk_s[...] = jnp.zeros(dk_s.shape, F32)
            dv_s[...] = jnp.zeros(dv_s.shape, F32)

        def step(diagonal):
            rows = pl.ds(pl.multiple_of(i * tq, tq), tq)
            for e in range(2):
                s = _dotb(qb_ref[e], ka_ref[e], NT)
                if diagonal:
                    s = jnp.where(_iota((tq, tq), 0) >= _iota((tq, tq), 1), s, -jnp.inf)
                p = jnp.exp2(s)
                ds = (p * _dotb(doa_ref[e], va_ref[e], NT)).astype(BF)
                dv_s[e] += _dotb(p.astype(BF), doa_ref[e], TN)
                dq_s[e, rows, :] += _dotb(ds, kd_ref[e], NN)
                dk_s[e] += _dotb(ds, qd_ref[e], TN)

        @pl.when(ii > j)
        def _():
            step(False)

        @pl.when(ii == j)
        def _():
            step(True)

        def fold(acc, sign):
            grad, dc = 0.0, 0.0
            for e in range(2):
                a = acc[e]
                grad = grad + a * _lanes_of(e)
                dc = dc + sign * jnp.sum(a * _spare(e, 0), axis=1, keepdims=True) * (_lane() == FF0 + 2 * hp + e).astype(F32)
            return grad, dc

        @pl.when(ii == nq - 1)
        def _():
            grad, dc = fold(dk_s, -1.0)
            dk_ref[...] = grad
            dv_ref[...] = dv_s[0] * _lanes_of(0) + dv_s[1] * _lanes_of(1)
            dcc_ref[pl.ds(pl.multiple_of(j * tq, tq), tq), :] += dc

        @pl.when((j == nq - 1) & (ii == nq - 1))
        def _():
            grad, dc = fold(dq_s, 1.0)
            dq_ref[...] = grad
            dcc_ref[...] += dc

    irow, jrow = _pair_spec(tq, lambda j, ii: jnp.maximum(ii, j)), _pair_spec(tq, lambda j, ii: j)
    jout = pl.BlockSpec((tq, LANES), lambda hp, j, ii: (j, hp))
    return hosted_call(body, rider, grid=(4, nq, nq), in_specs=[irow, jrow, jrow, irow, irow, jrow],
                       out_specs=[pl.BlockSpec((T, LANES), lambda hp, j, ii: (0, hp)), jout, jout, pl.BlockSpec((T, LANES), lambda hp, j, ii: (0, 0))],
                       out_shape=[_S((T, BRANCH_W)), _S((T, BRANCH_W)), _S((T, BRANCH_W)), _S((T, LANES))],
                       scratch_shapes=[pltpu.VMEM((2, T, LANES), F32), pltpu.VMEM((2, tq, LANES), F32), pltpu.VMEM((2, tq, LANES), F32)],
                       name=name, args=[qb, ka, va, doa, qd, kd])


def _acc_out(ref, val, first):
    @pl.when(first)
    def _():
        ref[...] = val

    @pl.when(jnp.logical_not(first))
    def _():
        ref[...] += val


def _row(tm, c):
    return pl.BlockSpec((tm, c), lambda i: (i, 0))


def _full(shape):
    return pl.BlockSpec(shape, lambda *_: (0,) * len(shape))


def ln_fwd(x, g, b, name):
    T, C = x.shape
    tm = min(T, 512)

    def body(x_ref, g_ref, b_ref, o_ref):
        o_ref[...] = _ln(x_ref[...], g_ref[...], b_ref[...])

    return _pcall(body, grid=(T // tm,), in_specs=[_row(tm, C), _full((1, C)), _full((1, C))], out_specs=_row(tm, C),
                  out_shape=_S((T, C)), name=name)(x, g, b)


def ln_bwd(x, g, b, dy, name):
    T, C = x.shape
    tm = min(T, 512)

    def body(x_ref, g_ref, b_ref, dy_ref, dx_ref, dg_ref, db_ref):
        _, vjp = jax.vjp(_ln, x_ref[...], g_ref[...], b_ref[...])
        dx, dg, db = vjp(dy_ref[...])
        dx_ref[...] = dx
        first = pl.program_id(0) == 0
        _acc_out(dg_ref, dg, first)
        _acc_out(db_ref, db, first)

    return _pcall(body, grid=(T // tm,), in_specs=[_row(tm, C), _full((1, C)), _full((1, C)), _row(tm, C)],
                  out_specs=[_row(tm, C), _full((1, C)), _full((1, C))], out_shape=[_S((T, C)), _S((1, C)), _S((1, C))], name=name)(x, g, b, dy)


def loss_head(h, target, name):
    T, C = h.shape
    tm = min(T, 512)

    def body(h_ref, t_ref, d_ref, l_ref):
        e = h_ref[...] - t_ref[...]
        d_ref[...] = e * (1.0 / C)
        part = jnp.sum(jnp.sum(e * e, axis=1, keepdims=True), axis=0, keepdims=True) * (0.5 / C)
        _acc_out(l_ref, part, pl.program_id(0) == 0)

    return _pcall(body, grid=(T // tm,), in_specs=[_row(tm, C), _row(tm, C)], out_specs=[_row(tm, C), _full((1, 1))],
                  out_shape=[_S((T, C)), _S((1, 1))], name=name)(h, target)


def add3(a, b, c, name):
    T, C = a.shape
    tm = min(T, 512)

    def body(a_ref, b_ref, c_ref, o_ref):
        o_ref[...] = a_ref[...] + b_ref[...] + c_ref[...]

    return _pcall(body, grid=(T // tm,), in_specs=[_row(tm, C)] * 3, out_specs=_row(tm, C), out_shape=_S((T, C)), name=name)(a, b, c)


def _wb_spec(l):
    return pl.BlockSpec((None, 4, BRANCH_W, D_MODEL), lambda *_: (l, 0, 0, 0))


def merge_fwd(ys, gl, gb, wb, l, name):
    T = gl.shape[0]
    tm = min(T, 256)

    def body(y0, y1, y2, y3, gl_ref, gb_ref, wb_ref, o_ref):
        acc = 0.0
        for i, y in enumerate((y0, y1, y2, y3)):
            z = _dotb(y[...].astype(BF), wb_ref[i], NN)
            g = jax.nn.sigmoid(gl_ref[:, i * D_MODEL:(i + 1) * D_MODEL] + gb_ref[i:i + 1, :])
            acc = acc + g * z
        o_ref[...] = acc

    return _pcall(body, grid=(T // tm,), in_specs=[_row(tm, BRANCH_W)] * 4 + [_row(tm, 4 * D_MODEL), _full((4, D_MODEL)), _wb_spec(l)],
                  out_specs=_row(tm, D_MODEL), out_shape=_S((T, D_MODEL)), name=name)(*ys, gl, gb, wb)


def merge_bwd(ys, gl, gb, wb, l, dm, name):
    T = gl.shape[0]
    tm = min(T, 256)

    def body(y0, y1, y2, y3, gl_ref, gb_ref, wb_ref, dm_ref, d0, d1, d2, d3, dgl_ref, dz_ref, dgb_ref):
        dmv = dm_ref[...]
        first = pl.program_id(0) == 0
        for i, (y, d) in enumerate(zip((y0, y1, y2, y3), (d0, d1, d2, d3))):
            cols = slice(i * D_MODEL, (i + 1) * D_MODEL)
            z = _dotb(y[...].astype(BF), wb_ref[i], NN)
            g = jax.nn.sigmoid(gl_ref[:, cols] + gb_ref[i:i + 1, :])
            dgl = dmv * z * (g * (1.0 - g))
            dz = (g * dmv).astype(BF)
            dgl_ref[:, cols] = dgl
            dz_ref[:, cols] = dz
            d[...] = _dotb(dz, wb_ref[i], NT)
            _acc_out(dgb_ref.at[i:i + 1, :], jnp.sum(dgl, axis=0, keepdims=True), first)

    return _pcall(body, grid=(T // tm,),
                  in_specs=[_row(tm, BRANCH_W)] * 4 + [_row(tm, 4 * D_MODEL), _full((4, D_MODEL)), _wb_spec(l), _row(tm, D_MODEL)],
                  out_specs=[_row(tm, BRANCH_W)] * 4 + [_row(tm, 4 * D_MODEL), _row(tm, 4 * D_MODEL), _full((4, D_MODEL))],
                  out_shape=[_S((T, BRANCH_W))] * 4 + [_S((T, 4 * D_MODEL)), _S((T, 4 * D_MODEL), BF), _S((4, D_MODEL))], name=name)(
                      *ys, gl, gb, wb, dm)


def _wout_spec(l):
    return pl.BlockSpec((None, D_MODEL, D_MODEL), lambda *_: (l, 0, 0))


def out_fwd(merged, h, wout, l, g, b, name):
    T = h.shape[0]
    tm = min(T, 512)

    def body(m_ref, h_ref, w_ref, g_ref, b_ref, u_ref, o_ref):
        u = ALPHA * h_ref[...] + _dotb(m_ref[...].astype(BF), w_ref[...], NN)
        u_ref[...] = u
        o_ref[...] = _ln(u, g_ref[...], b_ref[...])

    C = D_MODEL
    return _pcall(body, grid=(T // tm,), in_specs=[_row(tm, C), _row(tm, C), _wout_spec(l), _full((1, C)), _full((1, C))],
                  out_specs=[_row(tm, C), _row(tm, C)], out_shape=[_S((T, C)), _S((T, C))], name=name)(merged, h, wout, g, b)


def out_bwd(u, dy, g, b, wout, l, name):
    T, C = u.shape
    tm = min(T, 512)

    def body(u_ref, dy_ref, g_ref, b_ref, w_ref, du_ref, dm_ref, dg_ref, db_ref):
        _, vjp = jax.vjp(_ln, u_ref[...], g_ref[...], b_ref[...])
        du, dg, db = vjp(dy_ref[...])
        du_ref[...] = du
        dm_ref[...] = _dotb(du.astype(BF), w_ref[...], NT)
        first = pl.program_id(0) == 0
        _acc_out(dg_ref, dg, first)
        _acc_out(db_ref, db, first)

    return _pcall(body, grid=(T // tm,), in_specs=[_row(tm, C), _row(tm, C), _full((1, C)), _full((1, C)), _wout_spec(l)],
                  out_specs=[_row(tm, C), _row(tm, C), _full((1, C)), _full((1, C))],
                  out_shape=[_S((T, C)), _S((T, C)), _S((1, C)), _S((1, C))], name=name)(u, dy, g, b, wout)


def ff_fwd(h, wup, wdown, l, g, b, name):
    T, C = h.shape
    F = wup.shape[2]
    tm, tf = min(T, 512), 1024
    nf = F // tf

    def body(h_ref, wu_ref, wd_ref, g_ref, b_ref, u_ref, o_ref, acc):
        f = pl.program_id(1)
        a = _dotb(h_ref[...].astype(BF), wu_ref[...], NN)
        r = jnp.square(jnp.maximum(a, 0.0))
        p = _dotb(r.astype(BF), wd_ref[...], NN)
        _acc_out(acc, p, f == 0)

        @pl.when(f == nf - 1)
        def _():
            u = ALPHA * h_ref[...] + acc[...]
            u_ref[...] = u
            o_ref[...] = _ln(u, g_ref[...], b_ref[...])

    row = pl.BlockSpec((tm, C), lambda i, f: (i, 0))
    return _pcall(body, grid=(T // tm, nf),
                  in_specs=[row, pl.BlockSpec((None, C, tf), lambda i, f: (l, 0, f)), pl.BlockSpec((None, tf, C), lambda i, f: (l, f, 0)),
                            _full((1, C)), _full((1, C))],
                  out_specs=[row, row], out_shape=[_S((T, C)), _S((T, C))], scratch_shapes=[pltpu.VMEM((tm, C), F32)], name=name)(h, wup, wdown, g, b)


def ff_bwd(u, dy, h, g, b, wup, wdown, l, name):
    T, C = h.shape
    F = wup.shape[2]
    tm, tf = min(T, 512), 1024
    nf = F // tf

    def body(u_ref, dy_ref, h_ref, g_ref, b_ref, wu_ref, wd_ref, du_ref, dh_ref, da_ref, r_ref, dg_ref, db_ref, du_s, acc):
        i, f = pl.program_id(0), pl.program_id(1)

        @pl.when(f == 0)
        def _():
            _, vjp = jax.vjp(_ln, u_ref[...], g_ref[...], b_ref[...])
            du, dg, db = vjp(dy_ref[...])
            du_s[...] = du
            du_ref[...] = du
            _acc_out(dg_ref, dg, i == 0)
            _acc_out(db_ref, db, i == 0)

        a = _dotb(h_ref[...].astype(BF), wu_ref[...], NN)
        ap = jnp.maximum(a, 0.0)
        dr = _dotb(du_s[...].astype(BF), wd_ref[...], NT)
        da = (dr * (2.0 * ap)).astype(BF)
        da_ref[...] = da
        r_ref[...] = jnp.square(ap).astype(BF)
        _acc_out(acc, _dotb(da, wu_ref[...], NT), f == 0)

        @pl.when(f == nf - 1)
        def _():
            dh_ref[...] = ALPHA * du_s[...] + acc[...]

    row = pl.BlockSpec((tm, C), lambda i, f: (i, 0))
    colf = pl.BlockSpec((tm, tf), lambda i, f: (i, f))
    return _pcall(body, grid=(T // tm, nf),
                  in_specs=[row, row, row, _full((1, C)), _full((1, C)), pl.BlockSpec((None, C, tf), lambda i, f: (l, 0, f)),
                            pl.BlockSpec((None, tf, C), lambda i, f: (l, f, 0))],
                  out_specs=[row, row, colf, colf, _full((1, C)), _full((1, C))],
                  out_shape=[_S((T, C)), _S((T, C)), _S((T, F), BF), _S((T, F), BF), _S((1, C)), _S((1, C))],
                  scratch_shapes=[pltpu.VMEM((tm, C), F32), pltpu.VMEM((tm, C), F32)], name=name)(u, dy, h, g, b, wup, wdown)


MESH_ID = pl.DeviceIdType.MESH
_ANY = pl.BlockSpec(memory_space=pl.ANY)


def _window(ref, ax, idx, n):
    if n < 0:
        return ref
    sel = idx if n == 0 else pl.ds(pl.multiple_of(idx * n, n), n)
    return ref.at[(slice(None),) * ax + (sel,)]


Rider = collections.namedtuple("Rider", "operands out_shape scratch start wait")


def hosted_call(body, rider, *, grid, in_specs, out_specs, out_shape, scratch_shapes, name, args):
    n_in, n_out, n_scr = len(in_specs), len(out_specs), len(scratch_shapes)
    if rider is None:
        return _pcall(body, grid=grid, in_specs=in_specs, out_specs=out_specs, out_shape=out_shape, scratch_shapes=scratch_shapes, name=name)(*args), []
    ri, ro = len(rider.operands), len(rider.out_shape)

    def wrapped(*refs):
        ins, r_in = refs[:n_in], refs[n_in:n_in + ri]
        o0 = n_in + ri
        outs, r_out = refs[o0:o0 + n_out], refs[o0 + n_out:o0 + n_out + ro]
        s0 = o0 + n_out + ro
        scr, r_scr = refs[s0:s0 + n_scr], refs[s0 + n_scr:]
        ids = [pl.program_id(i) for i in range(len(grid))]
        first = functools.reduce(jnp.logical_and, [i == 0 for i in ids])
        last = functools.reduce(jnp.logical_and, [i == g - 1 for i, g in zip(ids, grid)])

        @pl.when(first)
        def _():
            rider.start(r_in, r_out, r_scr)

        body(*ins, *outs, *scr)

        @pl.when(last)
        def _():
            rider.wait(r_in, r_out, r_scr)

    res = _pcall(wrapped, grid=grid, in_specs=list(in_specs) + [_ANY] * ri, out_specs=list(out_specs) + [_ANY] * ro,
                 out_shape=list(out_shape) + list(rider.out_shape), scratch_shapes=list(scratch_shapes) + list(rider.scratch),
                 name=name)(*args, *rider.operands)
    return res[:n_out], res[n_out:]


def comm_call(rider, name):
    ri = len(rider.operands)

    def body(*refs):
        r_in, r_out, r_scr = refs[:ri], refs[ri:ri + len(rider.out_shape)], refs[ri + len(rider.out_shape):]
        rider.start(r_in, r_out, r_scr)
        rider.wait(r_in, r_out, r_scr)

    return _pcall(body, in_specs=[_ANY] * ri, out_specs=[_ANY] * len(rider.out_shape), out_shape=list(rider.out_shape),
                  scratch_shapes=list(rider.scratch), name=name)(*rider.operands)


def gather_rider(shards, axes):
    K = len(shards)
    widths = [s.shape[a] for s, a in zip(shards, axes)]
    out_shape = [_S(s.shape[:a] + (N_DEV * s.shape[a],) + s.shape[a + 1:], s.dtype) for s, a in zip(shards, axes)]

    def plan(x_refs, o_refs, sems):
        send_sems, recv_sems, local_sems = sems
        mx, my, mc = lax.axis_index("x"), lax.axis_index("y"), lax.axis_index("c")
        me, sibling = (mx, my, mc), (mx, my, 1 - mc)
        chips = [(1 - mx, my), (mx, 1 - my), (1 - mx, 1 - my)]

        def win(k, px, py, pc):
            return _window(o_refs[k], axes[k], 4 * px + 2 * py + pc, widths[k])

        def copy(k, slot, block, to, src=None):
            return pltpu.make_async_remote_copy(src_ref=win(k, *block) if src is None else src, dst_ref=win(k, *block),
                                                send_sem=send_sems.at[7 * k + slot], recv_sem=recv_sems.at[7 * k + slot],
                                                device_id=to, device_id_type=MESH_ID)

        mine = [pltpu.make_async_copy(x_refs[k], win(k, *me), local_sems.at[k]) for k in range(K)]
        first = []
        for k in range(K):
            first.append(copy(k, 0, me, sibling, src=x_refs[k]))
            first += [copy(k, 1 + j, me, (*chip, mc), src=x_refs[k]) for j, chip in enumerate(chips)]
        return me, sibling, chips, copy, mine, first

    def start(x_refs, o_refs, sems):
        _, _, _, _, mine, first = plan(x_refs, o_refs, sems)
        for cp in mine + first:
            cp.start()

    def wait(x_refs, o_refs, sems):
        me, sibling, chips, copy, mine, first = plan(x_refs, o_refs, sems)
        mc = me[2]
        passed = []
        for j, chip in enumerate(chips):
            for k in range(K):
                copy(k, 1 + j, (*chip, mc), me).wait_recv()
                passed.append(copy(k, 4 + j, (*chip, mc), sibling))
                passed[-1].start()
        for k in range(K):
            copy(k, 0, sibling, me).wait_recv()
        for j, chip in enumerate(chips):
            for k in range(K):
                copy(k, 4 + j, (*chip, 1 - mc), me).wait_recv()
        for cp in first + passed:
            cp.wait_send()
        for cp in mine:
            cp.wait()

    scratch = [pltpu.SemaphoreType.DMA((7 * K,)), pltpu.SemaphoreType.DMA((7 * K,)), pltpu.SemaphoreType.DMA((K,))]
    return Rider(list(shards), out_shape, scratch, start, wait)


def exchange_rider(items):
    ns = len(items)
    out_shape = [_S((N_DEV,) + tuple(it[3]), it[0].dtype) for it in items]

    def plan(src_refs, o_refs, sems):
        send_sems, recv_sems, local_sems = sems
        mx, my, mc = lax.axis_index("x"), lax.axis_index("y"), lax.axis_index("c")
        me = 4 * mx + 2 * my + mc
        remote, own = [], []
        for s, (_, ax, n, _) in enumerate(items):
            own.append(pltpu.make_async_copy(_window(src_refs[s], ax, me, n), o_refs[s].at[me], local_sems.at[s]))
            for k in range(1, N_DEV):
                px = 1 - mx if k & 4 else mx
                py = 1 - my if k & 2 else my
                pc = 1 - mc if k & 1 else mc
                remote.append(pltpu.make_async_remote_copy(
                    src_ref=_window(src_refs[s], ax, 4 * px + 2 * py + pc, n), dst_ref=o_refs[s].at[me],
                    send_sem=send_sems.at[7 * s + k - 1], recv_sem=recv_sems.at[7 * s + k - 1],
                    device_id=(px, py, pc), device_id_type=MESH_ID))
        return remote, own

    def start(src_refs, o_refs, sems):
        remote, own = plan(src_refs, o_refs, sems)
        for cp in own + remote:
            cp.start()

    def wait(src_refs, o_refs, sems):
        remote, own = plan(src_refs, o_refs, sems)
        for cp in remote + own:
            cp.wait()

    scratch = [pltpu.SemaphoreType.DMA((7 * ns,)), pltpu.SemaphoreType.DMA((7 * ns,)), pltpu.SemaphoreType.DMA((ns,))]
    return Rider([it[0] for it in items], out_shape, scratch, start, wait)


def reduce_adamw(rcvs, w, m, v, name):
    L = len(rcvs)
    _, A, B, C = rcvs[0].shape
    tb = B
    while tb > 8 and tb * C > (1 << 17):
        tb //= 2

    def body(*refs):
        r_refs, (w_ref, m_ref, v_ref, g_ref, d_ref, mo_ref, vo_ref) = refs[:L], refs[L:]
        for k in range(L):
            @pl.when(pl.program_id(0) == k)
            def _(k=k):
                g = r_refs[k][0].astype(F32)
                for d in range(1, N_DEV):
                    g = g + r_refs[k][d].astype(F32)
                mn = ADAM_B1 * m_ref[...] + (1.0 - ADAM_B1) * g
                vn = ADAM_B2 * v_ref[...] + (1.0 - ADAM_B2) * jnp.square(g)
                m_hat = mn / (1.0 - ADAM_B1 ** ADAM_STEP)
                v_hat = vn / (1.0 - ADAM_B2 ** ADAM_STEP)
                g_ref[...] = g
                d_ref[...] = -ADAM_LR * (m_hat / (jnp.sqrt(v_hat) + ADAM_EPS) + ADAM_WD * w_ref[...])
                mo_ref[...] = mn
                vo_ref[...] = vn

    def rspec(k):
        return pl.BlockSpec((N_DEV, None, tb, C), lambda l, a, i: (0, jnp.where(l == k, a, 0), jnp.where(l == k, i, 0), 0))

    blk = pl.BlockSpec((None, tb, C), lambda l, a, i: (l * A + a, i, 0))
    return _pcall(body, grid=(L, A, B // tb), in_specs=[rspec(k) for k in range(L)] + [blk, blk, blk],
                  out_specs=[blk] * 4, out_shape=[_S((L * A, B, C))] * 4, name=name)(*rcvs, w, m, v)


def _w_in_pieces(g0, g1):
    per = D_IN // N_DEV
    return [(d, max(g0, d * per) - d * per, min(g1, (d + 1) * per) - d * per) for d in range(N_DEV) if max(g0, d * per) < min(g1, (d + 1) * per)]


def repack_w_in(w8, name):
    _, L, R, per = w8.shape
    tr = 256

    def cols(x_ref, g0, g1):
        return [x_ref[d, :, a:b] for d, a, b in _w_in_pieces(g0, g1)]

    def body(x_ref, *o_refs):
        for (name_, i), o_ref in zip(SEGS, o_refs):
            o_ref[...] = jnp.concatenate(cols(x_ref, _OFF[i], _OFF[i + 1]), axis=1)
        parts, at = [], 0
        for i, lane0 in SMALL_SRC:
            assert lane0 == at
            parts += cols(x_ref, _OFF[i], _OFF[i + 1])
            at += IN_SIZES[i]
        parts.append(jnp.zeros((tr, LANES - at), w8.dtype))
        o_refs[-1][...] = jnp.concatenate(parts, axis=1)

    widths = [IN_SIZES[i] for _, i in SEGS] + [LANES]
    outs = _pcall(body, grid=(L, R // tr), in_specs=[pl.BlockSpec((N_DEV, None, tr, per), lambda l, r: (0, l, r, 0))],
                  out_specs=[pl.BlockSpec((None, tr, w), lambda l, r: (l, r, 0)) for w in widths],
                  out_shape=[_S((L, R, w), w8.dtype) for w in widths], name=name)(w8)
    return dict(zip(SEG_NAMES, outs))


def repack_dw_in(dseg, name):
    R = dseg["z"].shape[0]
    per = D_IN // N_DEV
    tr = 128
    src = {i: (k, 0) for k, (_, i) in enumerate(SEGS)}
    src.update({i: (len(SEGS), lane0) for i, lane0 in SMALL_SRC})

    def body(*refs):
        s_refs, o_ref = refs[:-1], refs[-1]
        for d in range(N_DEV):
            parts = []
            for i in range(len(IN_SIZES)):
                g0, g1 = max(_OFF[i], d * per), min(_OFF[i + 1], (d + 1) * per)
                if g0 < g1:
                    k, c0 = src[i]
                    parts.append(s_refs[k][:, c0 + g0 - _OFF[i]:c0 + g1 - _OFF[i]])
            o_ref[d] = jnp.concatenate(parts, axis=1)

    arrs = [dseg[n] for n in SEG_NAMES]
    return _pcall(body, grid=(R // tr,), in_specs=[pl.BlockSpec((tr, a.shape[1]), lambda r: (r, 0)) for a in arrs],
                  out_specs=pl.BlockSpec((N_DEV, tr, per), lambda r: (0, r, 0)), out_shape=_S((N_DEV, R, per), arrs[0].dtype), name=name)(*arrs)


WEIGHTS = ("ln_in_g", "ln_in_b", "w_in", "ssd_conv_w", "ssd_conv_b", "ssd_dt_bias", "ssd_a_log", "ssd_d", "ssd_norm_w", "dn_conv_w",
           "dn_a_log", "dn_dt_bias", "dn_norm_w", "sg_ln_g", "sg_ln_b", "sg_w", "sg_b", "fox_f_bias", "gate_b", "w_branch", "w_out",
           "ln1_g", "ln1_b", "w_up", "w_down", "ln2_g", "ln2_b")
SHARDED = {"w_in": 2, "ssd_conv_w": 2, "dn_conv_w": 2, "gate_b": 2, "w_branch": 3, "w_out": 1, "w_up": 2, "w_down": 1}
SLABBED = ("w_in", "dn_conv_w")
MATMUL_WEIGHTS = ("w_in", "w_branch", "w_out", "w_up", "w_down")
REPLICATED = tuple(n for n in WEIGHTS if n not in SHARDED)
SEG_NAMES = tuple(n for n, _ in SEGS) + ("small",)
PACK_COLS = 1024


def _lanes(vec, off):
    return jnp.pad(vec, (off, LANES - off - vec.shape[0]))[None]


def _pack_small(parts):
    flat = jnp.concatenate([q.reshape(-1) for q in parts])
    rows = -(-flat.shape[0] // (PACK_COLS * 64)) * 64
    return jnp.pad(flat, (0, rows * PACK_COLS - flat.shape[0])).reshape(1, rows, PACK_COLS)


EARLY = ("w_branch", "w_out", "w_up", "w_down", "gate_b")
LATE = ("w_in", "ssd_conv_w", "dn_conv_w")


def _gather_rider(p, l):
    shards, axes = [], []
    for n in SHARDED:
        s = p[n][l:l + 1]
        s = s.astype(BF) if n in MATMUL_WEIGHTS else s
        shards.append(s[None] if n in SLABBED else s)
        axes.append(0 if n in SLABBED else SHARDED[n])
    return gather_rider(shards, axes)


def _exchange_items(g, p, names):
    items = []
    for n in names:
        local = p[n].shape[1:]
        items.append((g[n], 0, 0, local) if n in SLABBED else (g[n], SHARDED[n] - 1, local[SHARDED[n] - 1], local))
    return items


def _layer_weights(p, gathered, l):
    full = dict(zip(SHARDED, gathered))
    w = {n: full[n] for n in ("w_branch", "w_out", "w_up", "w_down")}
    w["w_in"] = repack_w_in(full["w_in"], f"w_in_repack_{l}")
    w["ssd_cw"] = full["ssd_conv_w"][0]
    w["ssd_cb"] = p["ssd_conv_b"][l][None]
    w["dn_cw"] = jnp.moveaxis(full["dn_conv_w"][:, 0], 0, 1).reshape(4, 3 * BRANCH_W)
    w["dn_cb"] = jnp.zeros((1, 3 * BRANCH_W), F32)
    w["gate_b"] = full["gate_b"][0]
    w["ssd_ps"] = [_lanes(p["ssd_dt_bias"][l], DT0), _lanes(p["ssd_a_log"][l], DT0), _lanes(p["ssd_d"][l], DT0), p["ssd_norm_w"][l][None]]
    w["dn_ps"] = [_lanes(p["dn_a_log"][l], A0), _lanes(p["dn_dt_bias"][l], A0), p["dn_norm_w"][l][None]]
    w["sg_ps"] = [p["sg_ln_g"][l][None], p["sg_ln_b"][l][None], p["sg_w"][l], jnp.pad(p["sg_b"][l].T, ((0, 0), (0, LANES - 4)))]
    w["fox_ps"] = [_lanes(p["fox_f_bias"][l], FF0)]
    for n in ("ln1_g", "ln1_b", "ln2_g", "ln2_b"):
        w[n] = p[n][l][None]
    return w


def _scan_specs(T, a):
    c0 = lambda c, h: (c, 0)
    ssd = dict(f=ssd_chunk, xs=[(a["z"], (128, 512), c0), (a["xbc_act"], (128, 1024), c0), (a["small"], (128, LANES), c0)],
               ys=[((T, BRANCH_W), (128, BRANCH_W), c0)], state=(4, LANES, LANES), nc=T // 128, nh=1, shared=())
    dn = dict(f=dn_chunk, xs=[(a["dn_act"], (64, 3 * BRANCH_W), c0), (a["dngate"], (64, BRANCH_W), c0), (a["small"], (64, LANES), c0)],
              ys=[((T, BRANCH_W), (64, BRANCH_W), c0)], state=(4, LANES, LANES), nc=T // 64, nh=1, shared=())
    sg = dict(f=sg_chunk, xs=[(a["sguv"], (128, 1024), c0)], ys=[((T, BRANCH_W), (128, BRANCH_W), c0)], state=(1, 8, LANES), nc=T // 128, nh=1, shared=())
    fc = dict(f=foxc_chunk, xs=[(a["small"], (128, LANES), c0)],
              ys=[((T, LANES), (128, LANES), c0)], state=(1, 1, LANES), nc=T // 128, nh=1, shared=())
    return ssd, dn, sg, fc


def _layer_fwd(h, w, l, rider=None):
    T = h.shape[0]
    a = {"h": h}
    for n in SEG_NAMES:
        a[n] = matmul_w(h, w["w_in"][n], 0, "nn", f"proj_{n}_{l}")
    a["xbc_act"] = conv_fwd(a["xbc"], w["ssd_cw"], w["ssd_cb"], f"ssd_conv_{l}")
    a["dn_act"] = conv_fwd(a["dnqkv"], w["dn_cw"], w["dn_cb"], f"dn_conv_{l}")
    ssd, dn, sg, fc = _scan_specs(T, a)
    a["ya"], a["ssd_st"] = scan_fwd(f"ssd_fwd_{l}", ssd["f"], ssd["xs"], w["ssd_ps"], ssd["ys"], ssd["state"], ssd["nc"], ssd["nh"])
    a["yb"], a["dn_st"] = scan_fwd(f"dn_fwd_{l}", dn["f"], dn["xs"], w["dn_ps"], dn["ys"], dn["state"], dn["nc"], dn["nh"])
    a["yc"], a["sg_st"] = scan_fwd(f"sg_fwd_{l}", sg["f"], sg["xs"], w["sg_ps"], sg["ys"], sg["state"], sg["nc"], sg["nh"])
    a["ccol"], a["fc_st"] = scan_fwd(f"foxc_fwd_{l}", fc["f"], fc["xs"], w["fox_ps"], fc["ys"], fc["state"], fc["nc"], fc["nh"])
    a["fox_qa"], a["fox_ka"], a["fox_va"] = fox_prep(a["foxqkv"], a["ccol"], f"fox_prep_{l}")
    (a["yd"], a["lse"]), carried = fox_fwd(a["fox_qa"], a["fox_ka"], a["fox_va"], f"fox_fwd_{l}", rider=rider)
    a["merged"] = merge_fwd([a["ya"], a["yb"], a["yc"], a["yd"]], a["gates"], w["gate_b"], w["w_branch"], 0, f"merge_fwd_{l}")
    a["u1"], a["h1"] = out_fwd(a["merged"], h, w["w_out"], 0, w["ln1_g"], w["ln1_b"], f"out_fwd_{l}")
    a["u2"], a["h2"] = ff_fwd(a["h1"], w["w_up"], w["w_down"], 0, w["ln2_g"], w["ln2_b"], f"ff_fwd_{l}")
    return a, carried


def _layer_bwd(dh2, a, w, l, p, late_above):
    T = dh2.shape[0]
    g = {}
    du2, dh1, da, r, dg2, db2 = ff_bwd(a["u2"], dh2, a["h1"], w["ln2_g"], w["ln2_b"], w["w_up"], w["w_down"], 0, f"ff_bwd_{l}")
    g["ln2_g"], g["ln2_b"] = dg2[0], db2[0]
    g["w_up"] = matmul_tn(a["h1"], da, f"dwup_{l}", out_dtype=BF)
    g["w_down"] = matmul_tn(r, du2, f"dwdown_{l}", out_dtype=BF)
    du1, dmerged, dg1, db1 = out_bwd(a["u1"], dh1, w["ln1_g"], w["ln1_b"], w["w_out"], 0, f"out_bwd_{l}")
    g["ln1_g"], g["ln1_b"] = dg1[0], db1[0]
    g["w_out"] = matmul_tn(a["merged"], du1, f"dwout_{l}", out_dtype=BF)
    ys = [a["ya"], a["yb"], a["yc"], a["yd"]]
    dya, dyb, dyc, dyd, dgl, dz, dgb = merge_bwd(ys, a["gates"], w["gate_b"], w["w_branch"], 0, dmerged, f"merge_bwd_{l}")
    g["gate_b"] = dgb
    g["w_branch"] = jnp.stack([matmul_tn(ys[i], dz, f"dwb{i}_{l}", b_col0=i * D_MODEL, n_cols=D_MODEL, out_dtype=BF) for i in range(4)])
    early = exchange_rider(_exchange_items(g, p, EARLY))
    dn_rider = early if late_above is None else exchange_rider(late_above)
    fox_rider = None if late_above is None else early
    ssd, dn, sg, fc = _scan_specs(T, a)
    (dz_ssd, dxbc_act, dsm_ssd, d_dtb, d_alog, d_dsk, d_nw), _ = scan_bwd(f"ssd_bwd_{l}", ssd["f"], ssd["xs"], w["ssd_ps"], ssd["ys"], [dya], a["ssd_st"],
                                                                           ssd["state"], ssd["nc"], ssd["nh"])
    g["ssd_dt_bias"], g["ssd_a_log"], g["ssd_d"], g["ssd_norm_w"] = d_dtb[0, DT0:DT0 + 8], d_alog[0, DT0:DT0 + 8], d_dsk[0, DT0:DT0 + 8], d_nw[0]
    dxbc, g["ssd_conv_w"], dcb = conv_bwd(a["xbc"], w["ssd_cw"], w["ssd_cb"], dxbc_act, f"ssd_conv_bwd_{l}")
    g["ssd_conv_b"] = dcb[0]
    (ddn_act, ddngate, dsm_dn, d_alog, d_dtb, d_nw), got_dn = scan_bwd(f"dn_bwd_{l}", dn["f"], dn["xs"], w["dn_ps"], dn["ys"], [dyb], a["dn_st"],
                                                                        dn["state"], dn["nc"], dn["nh"], rider=dn_rider)
    g["dn_a_log"], g["dn_dt_bias"], g["dn_norm_w"] = d_alog[0, A0:A0 + 4], d_dtb[0, A0:A0 + 4], d_nw[0]
    ddnqkv, g["dn_conv_w"], _ = conv_bwd(a["dnqkv"], w["dn_cw"], w["dn_cb"], ddn_act, f"dn_conv_bwd_{l}")
    (dsguv, d_lng, d_lnb, d_w, d_bt), _ = scan_bwd(f"sg_bwd_{l}", sg["f"], sg["xs"], w["sg_ps"], sg["ys"], [dyc], a["sg_st"], sg["state"], sg["nc"], sg["nh"])
    g["sg_ln_g"], g["sg_ln_b"], g["sg_w"], g["sg_b"] = d_lng[0], d_lnb[0], d_w, d_bt[:, :4].T
    qb, doa, qd, kd = fox_prep_bwd(a["foxqkv"], a["fox_qa"], a["yd"], a["lse"], dyd, f"fox_prep_bwd_{l}")
    (dfq, dfk, dfv, dccol), got_fox = fox_bwd(qb, a["fox_ka"], a["fox_va"], doa, qd, kd, f"fox_bwd_{l}", rider=fox_rider)
    (dsm_fox, d_fb), _ = scan_bwd(f"foxc_bwd_{l}", fc["f"], fc["xs"], w["fox_ps"], fc["ys"], [dccol], a["fc_st"], fc["state"], fc["nc"], fc["nh"])
    g["fox_f_bias"] = d_fb[0, FF0:FF0 + 8]
    dseg = {"z": dz_ssd, "xbc": dxbc, "dnqkv": ddnqkv, "dngate": ddngate, "sguv": dsguv,
            "foxqkv": jnp.concatenate([dfq, dfk, dfv], axis=1), "gates": dgl, "small": add3(dsm_ssd, dsm_dn, dsm_fox, f"dsmall_{l}")}
    dh, scale, dwin = du1, ALPHA, {}
    for n in SEG_NAMES:
        dh = matmul_w(dseg[n], w["w_in"][n], 0, "nt", f"dh_{n}_{l}", add=dh, add_scale=scale)
        scale = 1.0
        dwin[n] = matmul_tn(a["h"], dseg[n], f"dwin_{n}_{l}", out_dtype=BF)
    g["w_in"] = repack_dw_in(dwin, f"dw_in_repack_{l}")
    g["dn_conv_w"] = jnp.moveaxis(g["dn_conv_w"].reshape(4, N_DEV, 3 * BRANCH_W // N_DEV), 1, 0)
    got = {(EARLY, l): got_dn} if late_above is None else {(LATE, l + 1): got_dn, (EARLY, l): got_fox}
    return dh, g, got


def kernel(x, ln_in_g, ln_in_b, w_in, ssd_conv_w, ssd_conv_b, ssd_dt_bias, ssd_a_log, ssd_d, ssd_norm_w, dn_conv_w, dn_a_log, dn_dt_bias, dn_norm_w, sg_ln_g, sg_ln_b, sg_w, sg_b, fox_f_bias, gate_b, w_branch, w_out, ln1_g, ln1_b, w_up, w_down, ln2_g, ln2_b, loss_target, m_ln_in_g, m_ln_in_b, m_w_in, m_ssd_conv_w, m_ssd_conv_b, m_ssd_dt_bias, m_ssd_a_log, m_ssd_d, m_ssd_norm_w, m_dn_conv_w, m_dn_a_log, m_dn_dt_bias, m_dn_norm_w, m_sg_ln_g, m_sg_ln_b, m_sg_w, m_sg_b, m_fox_f_bias, m_gate_b, m_w_branch, m_w_out, m_ln1_g, m_ln1_b, m_w_up, m_w_down, m_ln2_g, m_ln2_b, v_ln_in_g, v_ln_in_b, v_w_in, v_ssd_conv_w, v_ssd_conv_b, v_ssd_dt_bias, v_ssd_a_log, v_ssd_d, v_ssd_norm_w, v_dn_conv_w, v_dn_a_log, v_dn_dt_bias, v_dn_norm_w, v_sg_ln_g, v_sg_ln_b, v_sg_w, v_sg_b, v_fox_f_bias, v_gate_b, v_w_branch, v_w_out, v_ln1_g, v_ln1_b, v_w_up, v_w_down, v_ln2_g, v_ln2_b):
    args = dict(locals())
    p = {n: args[n] for n in WEIGHTS}
    xt, target = x[0], loss_target[0]
    gathered = comm_call(_gather_rider(p, 0), "weights_all_gather_0")
    h = ln_fwd(xt, ln_in_g[None], ln_in_b[None], "ln_in_fwd")
    ws, acts = [], []
    for l in range(DEPTH):
        ws.append(_layer_weights(p, gathered, l))
        a, gathered = _layer_fwd(h, ws[l], l, rider=_gather_rider(p, l + 1) if l + 1 < DEPTH else None)
        acts.append(a)
        h = a["h2"]
    dh, loss = loss_head(h, target, "loss_head")
    loss = lax.psum(loss[0, 0], ("x", "y", "c"))

    layer_grads, got, late = [None] * DEPTH, {}, None
    for l in reversed(range(DEPTH)):
        dh, layer_grads[l], got_l = _layer_bwd(dh, acts[l], ws[l], l, p, late)
        got.update(got_l)
        late = _exchange_items(layer_grads[l], p, LATE)
    grad_x, dg_in, db_in = ln_bwd(xt, ln_in_g[None], ln_in_b[None], dh, "ln_in_bwd")
    small = {n: jnp.stack([layer_grads[l][n] for l in range(DEPTH)]) for n in REPLICATED if n not in ("ln_in_g", "ln_in_b")}
    small["ln_in_g"], small["ln_in_b"] = dg_in[0], db_in[0]
    pack = _pack_small([small[n] for n in REPLICATED])
    last = comm_call(exchange_rider(late + [(pack[0], 0, -1, pack.shape[1:])]), "grads_exchange_last")
    got[(LATE, 0)] = last[:-1]
    rcv = {(n, l): arr for (names, l), arrs in got.items() for n, arr in zip(names, arrs)}

    res = [{}, {}, {}, {}]
    for n in SHARDED:
        shp = p[n].shape
        lead = math.prod(shp[1:-2])
        to3 = lambda t: t.reshape((-1,) + shp[-2:])
        outs = reduce_adamw([rcv[(n, l)].reshape((N_DEV, lead) + shp[-2:]) for l in range(DEPTH)],
                            to3(p[n]), to3(args["m_" + n]), to3(args["v_" + n]), f"adamw_{n}")
        for k in range(4):
            res[k][n] = outs[k].reshape(shp)
    outs = reduce_adamw([last[-1][:, None]], _pack_small([p[n] for n in REPLICATED]), _pack_small([args["m_" + n] for n in REPLICATED]),
                        _pack_small([args["v_" + n] for n in REPLICATED]), "adamw_replicated")
    off = 0
    for n in REPLICATED:
        shp = p[n].shape
        cnt = math.prod(shp)
        for k in range(4):
            res[k][n] = outs[k].reshape(-1)[off:off + cnt].reshape(shp)
        off += cnt
    return (loss, grad_x[None], *[res[0][n] for n in WEIGHTS], *[res[1][n] for n in WEIGHTS],
            *[res[2][n] for n in WEIGHTS], *[res[3][n] for n in WEIGHTS])
```

```python
import collections
import functools
import math

import jax
import jax.numpy as jnp
from jax import lax
from jax.experimental import pallas as pl
from jax.experimental.pallas import tpu as pltpu

F32 = jnp.float32
BF = jnp.bfloat16

D_MODEL = 1024
DEPTH = 2
BRANCH_W = 512
D_FF = 4096
LN_EPS = 1e-5
NORM_EPS = 1e-6
ALPHA = (2 * DEPTH) ** 0.25
N_DEV = 8
LANES = 128
ADAM_LR, ADAM_B1, ADAM_B2, ADAM_EPS, ADAM_WD, ADAM_STEP = 0.001, 0.9, 0.999, 1e-08, 0.01, 10

DT0, BETA0, A0, FF0 = 0, 8, 12, 16
IN_SIZES = (512, 1024, 8, 1536, 4, 4, 512, 1024, 1536, 8, 4096)
_OFF = [0]
for _s in IN_SIZES:
    _OFF.append(_OFF[-1] + _s)
D_IN = _OFF[-1]
SEGS = (("z", 0), ("xbc", 1), ("dnqkv", 3), ("dngate", 6), ("sguv", 7), ("foxqkv", 8), ("gates", 10))
SMALL_SRC = ((2, DT0), (4, BETA0), (5, A0), (9, FF0))

NN = ((1,), (0,))
NT = ((1,), (1,))
TN = ((0,), (0,))
_DIMS = {"nn": NN, "nt": NT, "tn": TN}


def _pcall(body, **kw):
    return pl.pallas_call(body, **kw)


def _S(shape, dtype=F32):
    return jax.ShapeDtypeStruct(tuple(shape), dtype)


def _iota(shape, dim):
    return lax.broadcasted_iota(jnp.int32, shape, dim)


def _dotb(a, b, dims):
    return lax.dot_general(a, b, (dims, ((), ())), preferred_element_type=F32)


def _split2(a):
    ah = a.astype(BF)
    return ah, (a - ah.astype(F32)).astype(BF)


def _split3(a):
    a1 = a.astype(BF)
    r = a - a1.astype(F32)
    a2 = r.astype(BF)
    a3 = (r - a2.astype(F32)).astype(BF)
    return a1, a2, a3


def _mm_raw(a, b, form, mode):
    d = _DIMS[form]
    if mode == "1":
        return _dotb(a.astype(BF), b.astype(BF), d)
    if mode == "3":
        ah, al = _split2(a)
        bh, bl = _split2(b)
        return _dotb(ah, bh, d) + (_dotb(ah, bl, d) + _dotb(al, bh, d))
    if mode == "xa":
        ab = a.astype(BF)
        b1, b2, b3 = _split3(b)
        return _dotb(ab, b1, d) + (_dotb(ab, b2, d) + _dotb(ab, b3, d))
    bb = b.astype(BF)
    a1, a2, a3 = _split3(a)
    return _dotb(a1, bb, d) + (_dotb(a2, bb, d) + _dotb(a3, bb, d))


@functools.partial(jax.custom_vjp, nondiff_argnums=(2, 3))
def mm(a, b, form, mode):
    return _mm_raw(a, b, form, mode)


def _mm_fwd(a, b, form, mode):
    return _mm_raw(a, b, form, mode), (a, b)


_XA_DB = {"nn": "xa", "nt": "xb", "tn": "xa"}
_XB_DA = {"nn": "xb", "nt": "xb", "tn": "xa"}


def _mm_bwd(form, mode, res, g):
    a, b = res
    ma = _XB_DA[form] if mode == "xb" else mode
    mb = _XA_DB[form] if mode == "xa" else mode
    da = db = None
    if mode != "xa":
        da = {"nn": lambda: mm(g, b, "nt", ma), "nt": lambda: mm(g, b, "nn", ma), "tn": lambda: mm(b, g, "nt", ma)}[form]()
    if mode != "xb":
        db = {"nn": lambda: mm(a, g, "tn", mb), "nt": lambda: mm(g, a, "tn", mb), "tn": lambda: mm(a, g, "nn", mb)}[form]()
    if da is None:
        da = jnp.zeros_like(a)
    if db is None:
        db = jnp.zeros_like(b)
    return da, db


mm.defvjp(_mm_fwd, _mm_bwd)


def _silu(x):
    return x * jax.nn.sigmoid(x)


def _ln(x, g, b):
    mu = jnp.mean(x, -1, keepdims=True)
    xc = x - mu
    var = jnp.mean(xc * xc, -1, keepdims=True)
    return xc * lax.rsqrt(var + LN_EPS) * g + b


def _pick(n, cap):
    if n <= cap:
        return n
    best = LANES
    for t in range(LANES, cap + 1, LANES):
        if n % t == 0:
            best = t
    return best


def transpose_bf16(a, name):
    T, C = a.shape
    tt = min(T, 512)

    def body(a_ref, o_ref):
        o_ref[...] = a_ref[...].T.astype(BF)

    return _pcall(body, grid=(T // tt,), in_specs=[pl.BlockSpec((tt, C), lambda t: (t, 0))], out_specs=pl.BlockSpec((C, tt), lambda t: (0, t)),
                  out_shape=_S((C, T), BF), name=name)(a)


def matmul_w(a, w, l, form, name, add=None, add_scale=1.0, out_dtype=F32):
    M, K = a.shape
    N = w.shape[2] if form == "nn" else w.shape[1]
    tm, tn, tk = min(M, 512), _pick(N, 1024), _pick(K, 1024)
    nk = K // tk

    def body(*refs):
        if add is None:
            a_ref, b_ref, o_ref, acc = refs
        else:
            a_ref, b_ref, d_ref, o_ref, acc = refs
        k = pl.program_id(2)
        p = _dotb(a_ref[...].astype(BF), b_ref[...].astype(BF), _DIMS[form])

        @pl.when(k == 0)
        def _():
            acc[...] = p

        @pl.when(k > 0)
        def _():
            acc[...] += p

        @pl.when(k == nk - 1)
        def _():
            r = acc[...]
            if add is not None:
                r = r + add_scale * d_ref[...]
            o_ref[...] = r.astype(out_dtype)

    if form == "nn":
        wspec = pl.BlockSpec((None, tk, tn), lambda j, i, k: (l, k, j))
    else:
        wspec = pl.BlockSpec((None, tn, tk), lambda j, i, k: (l, j, k))
    in_specs = [pl.BlockSpec((tm, tk), lambda j, i, k: (i, k)), wspec]
    args = [a, w]
    if add is not None:
        in_specs.append(pl.BlockSpec((tm, tn), lambda j, i, k: (i, j)))
        args.append(add)
    return _pcall(body, grid=(N // tn, M // tm, nk), in_specs=in_specs,
                  out_specs=pl.BlockSpec((tm, tn), lambda j, i, k: (i, j)), out_shape=_S((M, N), out_dtype),
                  scratch_shapes=[pltpu.VMEM((tm, tn), F32)], name=name)(*args)


def matmul_tn(a, b, name, b_col0=0, n_cols=None, out_dtype=F32):
    T, M = a.shape
    N = b.shape[1] if n_cols is None else n_cols
    tm, tn, tt = _pick(M, 512), _pick(N, 1024), min(T, 512)
    nt = T // tt
    jb = b_col0 // tn

    def body(a_ref, b_ref, o_ref, acc):
        t = pl.program_id(2)
        p = _dotb(a_ref[...].astype(BF), b_ref[...].astype(BF), TN)

        @pl.when(t == 0)
        def _():
            acc[...] = p

        @pl.when(t > 0)
        def _():
            acc[...] += p

        @pl.when(t == nt - 1)
        def _():
            o_ref[...] = acc[...].astype(out_dtype)

    return _pcall(body, grid=(M // tm, N // tn, nt),
                  in_specs=[pl.BlockSpec((tt, tm), lambda i, j, t: (t, i)), pl.BlockSpec((tt, tn), lambda i, j, t: (t, jb + j))],
                  out_specs=pl.BlockSpec((tm, tn), lambda i, j, t: (i, j)), out_shape=_S((M, N), out_dtype),
                  scratch_shapes=[pltpu.VMEM((tm, tn), F32)], name=name)(a, b)


def _pieces(v):
    if v.ndim == 3:
        return [v[i] for i in range(v.shape[0])]
    n = v.shape[1] // LANES
    if n <= 1:
        return [v]
    return [v[:, i * LANES:(i + 1) * LANES] for i in range(n)]


def _join(ps, like_ndim):
    if like_ndim == 3:
        return jnp.stack(ps, axis=0)
    return ps[0] if len(ps) == 1 else jnp.concatenate(ps, axis=1)


def scan_fwd(name, f, xs, ps, ys, state_shape, nc, nh=1, rider=None):
    nx, npar, ny = len(xs), len(ps), len(ys)

    def body(*refs):
        x_refs, p_refs = refs[:nx], refs[nx:nx + npar]
        y_refs = refs[nx + npar:nx + npar + ny]
        st_out, st = refs[nx + npar + ny], refs[nx + npar + ny + 1]
        c, h = pl.program_id(0), pl.program_id(1)

        @pl.when(c == 0)
        def _():
            st[h] = jnp.zeros(state_shape, F32)

        S = st[h]
        st_out[...] = S
        yv, Sn = f([_pieces(r[...]) for r in x_refs], [_pieces(r[...]) for r in p_refs], _pieces(S), h)
        for r, v in zip(y_refs, yv):
            r[...] = _join(v, 2)
        st[h] = _join(Sn, 3)

    in_specs = [pl.BlockSpec(bs, im) for (_, bs, im) in xs]
    in_specs += [pl.BlockSpec(p.shape, (lambda c, h, n=p.ndim: (0,) * n)) for p in ps]
    out_specs = [pl.BlockSpec(bs, im) for (_, bs, im) in ys]
    out_specs.append(pl.BlockSpec((None, None) + tuple(state_shape), lambda c, h: (c, h, 0, 0, 0)))
    out_shape = [_S(s) for (s, _, _) in ys] + [_S((nc, nh) + tuple(state_shape))]
    return hosted_call(body, rider, grid=(nc, nh), in_specs=in_specs, out_specs=out_specs, out_shape=out_shape,
                       scratch_shapes=[pltpu.VMEM((nh,) + tuple(state_shape), F32)], name=name, args=[*[x[0] for x in xs], *ps])


def scan_bwd(name, f, xs, ps, ys, dys, states, state_shape, nc, nh=1, shared=(), rider=None):
    nx, npar, ny = len(xs), len(ps), len(ys)

    def body(*refs):
        x_refs, p_refs = refs[:nx], refs[nx:nx + npar]
        s_ref = refs[nx + npar]
        dy_refs = refs[nx + npar + 1:nx + npar + 1 + ny]
        o = nx + npar + 1 + ny
        dx_refs, dp_refs, dst = refs[o:o + nx], refs[o + nx:o + nx + npar], refs[o + nx + npar]
        c, h = pl.program_id(0), pl.program_id(1)

        @pl.when(c == 0)
        def _():
            dst[h] = jnp.zeros(state_shape, F32)

        @pl.when((c == 0) & (h == 0))
        def _():
            for r in dp_refs:
                r[...] = jnp.zeros(r.shape, F32)

        xv = [_pieces(r[...]) for r in x_refs]
        pv = [_pieces(r[...]) for r in p_refs]
        _, vjp = jax.vjp(lambda a, b, s: f(a, b, s, h), xv, pv, _pieces(s_ref[...]))
        dxv, dpv, dS = vjp(([_pieces(r[...]) for r in dy_refs], _pieces(dst[h])))
        for i, (r, v) in enumerate(zip(dx_refs, dxv)):
            if i in shared and nh > 1:
                @pl.when(h == 0)
                def _(r=r, v=v):
                    r[...] = _join(v, 2)

                @pl.when(h > 0)
                def _(r=r, v=v):
                    r[...] += _join(v, 2)
            else:
                r[...] = _join(v, 2)
        for r, v in zip(dp_refs, dpv):
            r[...] += _join(v, len(r.shape))
        dst[h] = _join(dS, 3)

    def rev(im):
        return lambda c, h: im(nc - 1 - c, h)

    in_specs = [pl.BlockSpec(bs, rev(im)) for (_, bs, im) in xs]
    in_specs += [pl.BlockSpec(p.shape, (lambda c, h, n=p.ndim: (0,) * n)) for p in ps]
    in_specs.append(pl.BlockSpec((None, None) + tuple(state_shape), lambda c, h: (nc - 1 - c, h, 0, 0, 0)))
    in_specs += [pl.BlockSpec(bs, rev(im)) for (_, bs, im) in ys]
    out_specs = [pl.BlockSpec(bs, rev(im)) for (_, bs, im) in xs]
    out_specs += [pl.BlockSpec(p.shape, (lambda c, h, n=p.ndim: (0,) * n)) for p in ps]
    out_shape = [_S(x[0].shape) for x in xs] + [_S(p.shape) for p in ps]
    return hosted_call(body, rider, grid=(nc, nh), in_specs=in_specs, out_specs=out_specs, out_shape=out_shape,
                       scratch_shapes=[pltpu.VMEM((nh,) + tuple(state_shape), F32)], name=name,
                       args=[*[x[0] for x in xs], *ps, states, *dys])


def _lane():
    return _iota((1, LANES), 1)


def _col(v, idx):
    return jnp.sum(v * (_lane() == idx).astype(F32), axis=1, keepdims=True)


def _last_row(v):
    r = v.shape[0]
    return jnp.sum(v * (_iota((r, 1), 0) == r - 1).astype(F32), axis=0, keepdims=True)


def _tril(n, strict=False):
    r, c = _iota((n, n), 0), _iota((n, n), 1)
    return (r > c) if strict else (r >= c)


def ssd_chunk(xs, ps, S, h):
    zp, xbc, (sm,) = xs
    (bias,), (alog,), (dsk,), nw = ps
    Q = sm.shape[0]
    lane = _lane()
    a128 = jnp.where(lane < 8, -jnp.exp(alog), 0.0)
    dtl = jax.nn.softplus(sm + bias)
    tri = _tril(Q)
    trif = tri.astype(F32)
    cum = mm(trif, dtl * a128, "nn", "xa")
    sel8 = (_iota((8, LANES), 0) == _iota((8, LANES), 1)).astype(F32)
    cum_t = mm(sel8, cum, "nt", "xa")
    m0 = (lane < 64).astype(F32)
    rows0 = (_iota((LANES, 1), 0) < 64).astype(F32)
    yz, Sn, ssq, cb = [], [], 0.0, None
    for pr in range(4):
        g = pr // 2
        Bg, Cg = xbc[4 + g], xbc[6 + g]
        if pr % 2 == 0:
            cb = mm(Cg, Bg, "nt", "1")
        xp, Hp = xbc[pr], S[pr]
        ypair, stpair, rowdec = 0.0, 0.0, 0.0
        for e in range(2):
            hh = 2 * pr + e
            me = m0 if e == 0 else 1.0 - m0
            re = rows0 if e == 0 else 1.0 - rows0
            col = _col(cum, hh)
            row = jnp.sum(cum_t * (_iota((8, 1), 0) == hh).astype(F32), axis=0, keepdims=True)
            xh = xp * me
            xdt = xh * _col(dtl, hh)
            seg = jnp.exp(jnp.where(tri, col - row, -jnp.inf))
            last = _last_row(col)
            ypair = ypair + mm(cb * seg, xdt, "nn", "1") + mm(Cg * jnp.exp(col), Hp, "nt", "1") * me + _col(dsk, hh) * xh
            stpair = stpair + mm(xdt, Bg * jnp.exp(last - col), "tn", "1")
            rowdec = rowdec + jnp.exp(last) * re
        Sn.append(Hp * rowdec + stpair)
        v = ypair * _silu(zp[pr])
        ssq = ssq + jnp.sum(v * v, axis=1, keepdims=True)
        yz.append(v)
    scale = lax.rsqrt(ssq / BRANCH_W + NORM_EPS)
    return [[yz[i] * scale * nw[i] for i in range(4)]], Sn


def dn_chunk(xs, ps, S, h):
    act, gate, (sm,) = xs
    (alog,), (dtb,), (nw,) = ps
    H = range(4)
    C = sm.shape[0]
    lane = _lane()
    G = jnp.where((lane >= A0) & (lane < A0 + 4), -jnp.exp(alog) * jax.nn.softplus(sm + dtb), 0.0)
    tri, strict = _tril(C), _tril(C, True)
    gcs = mm(tri.astype(F32), G, "nn", "xa")
    sig = jax.nn.sigmoid(sm)
    qn = [act[h] * lax.rsqrt(jnp.sum(act[h] * act[h], axis=1, keepdims=True) + NORM_EPS) * (LANES ** -0.5) for h in H]
    kn = [act[4 + h] * lax.rsqrt(jnp.sum(act[4 + h] * act[4 + h], axis=1, keepdims=True) + NORM_EPS) for h in H]
    beta = [_col(sig, BETA0 + h) for h in H]
    gcol = [_col(gcs, A0 + h) for h in H]
    selr = [((_iota((8, LANES), 0) == 0) & (_iota((8, LANES), 1) == A0 + h)).astype(F32) for h in H]
    grow = [jnp.sum(mm(selr[h], gcs, "nt", "xa"), axis=0, keepdims=True) for h in H]
    gamma = [jnp.exp(jnp.where(tri, gcol[h] - grow[h], -jnp.inf)) for h in H]
    kb = [kn[h] * beta[h] for h in H]
    pk = [-(mm(kb[h], kn[h], "nt", "1") * jnp.where(strict, gamma[h], 0.0)) for h in H]
    eye = (_iota((C, C), 0) == _iota((C, C), 1)).astype(F32)
    minv = [eye + pk[h] for h in H]
    for _ in range(5):
        pk = [mm(pk[h], pk[h], "nn", "3") for h in H]
        minv = [minv[h] + mm(minv[h], pk[h], "nn", "3") for h in H]
    eg = [jnp.exp(gcol[h]) for h in H]
    w = [mm(minv[h], kb[h] * eg[h], "nn", "3") for h in H]
    u = [mm(minv[h], act[8 + h] * beta[h], "nn", "3") for h in H]
    glast = [_last_row(gcol[h]) for h in H]
    vnew = [u[h] - mm(w[h], S[h], "nn", "1") for h in H]
    qk = [mm(qn[h], kn[h], "nt", "1") * gamma[h] for h in H]
    o = [mm(qn[h] * eg[h], S[h], "nn", "1") + mm(qk[h], vnew[h], "nn", "1") for h in H]
    Sn = [S[h] * jnp.exp(glast[h]) + mm(kn[h] * jnp.exp(glast[h] - gcol[h]), vnew[h], "tn", "1") for h in H]
    on = [o[h] * lax.rsqrt(jnp.mean(o[h] * o[h], axis=1, keepdims=True) + NORM_EPS) * nw for h in H]
    return [[on[h] * _silu(gate[h]) for h in H]], Sn


def sg_chunk(xs, ps, S, h):
    (uv,) = xs
    lng, lnb, W, (bt,) = ps
    u = [jax.nn.gelu(p) for p in uv[:4]]
    v = [jax.nn.gelu(p) for p in uv[4:]]
    mu = sum(jnp.sum(p, axis=1, keepdims=True) for p in v) / BRANCH_W
    vc = [p - mu for p in v]
    var = sum(jnp.sum(p * p, axis=1, keepdims=True) for p in vc) / BRANCH_W
    inv = lax.rsqrt(var + LN_EPS)
    trif = _tril(W[0].shape[0]).astype(F32)
    out = []
    for g in range(4):
        vn = vc[g] * inv * lng[g] + lnb[g]
        out.append(u[g] * (mm(W[g] * trif, vn, "nn", "1") + _col(bt, g)))
    return [out], S


def foxc_chunk(xs, ps, S, h):
    (sm,), ((fb,),), (carry,) = xs[0], ps, S
    lane = _lane()
    ls = jnp.where((lane >= FF0) & (lane < FF0 + 8), jax.nn.log_sigmoid(sm + fb), 0.0)
    c = mm(_tril(sm.shape[0]).astype(F32), ls, "nn", "xa") + carry
    return [[c]], [_last_row(c)]


def _conv_tiles(T, C):
    return min(T, 512), _pick(C, 512)


def conv_fwd(x, w, b, name):
    T, C = x.shape
    tm, cb = _conv_tiles(T, C)

    def body(xp_ref, x_ref, w_ref, b_ref, o_ref):
        i = pl.program_id(1)
        tail = xp_ref[tm - 8:tm, :] * (i > 0).astype(F32)
        e = jnp.concatenate([tail, x_ref[...]], axis=0)
        pre = b_ref[...] + sum(w_ref[k:k + 1, :] * e[5 + k:5 + k + tm, :] for k in range(4))
        o_ref[...] = _silu(pre)

    return _pcall(body, grid=(C // cb, T // tm),
                  in_specs=[pl.BlockSpec((tm, cb), lambda j, i: (jnp.maximum(i - 1, 0), j)), pl.BlockSpec((tm, cb), lambda j, i: (i, j)),
                            pl.BlockSpec((4, cb), lambda j, i: (0, j)), pl.BlockSpec((1, cb), lambda j, i: (0, j))],
                  out_specs=pl.BlockSpec((tm, cb), lambda j, i: (i, j)), out_shape=_S((T, C)), name=name)(x, x, w, b)


def conv_bwd(x, w, b, dact, name):
    T, C = x.shape
    tm, cb = _conv_tiles(T, C)
    nt = T // tm

    def body(xp_ref, x_ref, xn_ref, w_ref, b_ref, d_ref, dn_ref, dx_ref, dw_ref, db_ref):
        i = pl.program_id(1)
        has_prev, has_next = (i > 0).astype(F32), (i < nt - 1).astype(F32)
        e = jnp.concatenate([xp_ref[tm - 8:tm, :] * has_prev, x_ref[...], xn_ref[0:8, :] * has_next], axis=0)
        pre = b_ref[...] + sum(w_ref[k:k + 1, :] * e[5 + k:5 + k + tm + 8, :] for k in range(4))
        de = jnp.concatenate([d_ref[...], dn_ref[0:8, :] * has_next], axis=0)
        sg = jax.nn.sigmoid(pre)
        dpre = de * (sg * (1.0 + pre * (1.0 - sg)))
        dx_ref[...] = sum(w_ref[k:k + 1, :] * dpre[3 - k:3 - k + tm, :] for k in range(4))
        dcur = dpre[0:tm, :]
        dw = jnp.concatenate([jnp.sum(dcur * e[5 + k:5 + k + tm, :], axis=0, keepdims=True) for k in range(4)], axis=0)
        db = jnp.sum(dcur, axis=0, keepdims=True)

        @pl.when(i == 0)
        def _():
            dw_ref[...] = dw
            db_ref[...] = db

        @pl.when(i > 0)
        def _():
            dw_ref[...] += dw
            db_ref[...] += db

    blk = lambda f: pl.BlockSpec((tm, cb), f)
    return _pcall(body, grid=(C // cb, nt),
                  in_specs=[blk(lambda j, i: (jnp.maximum(i - 1, 0), j)), blk(lambda j, i: (i, j)), blk(lambda j, i: (jnp.minimum(i + 1, nt - 1), j)),
                            pl.BlockSpec((4, cb), lambda j, i: (0, j)), pl.BlockSpec((1, cb), lambda j, i: (0, j)),
                            blk(lambda j, i: (i, j)), blk(lambda j, i: (jnp.minimum(i + 1, nt - 1), j))],
                  out_specs=[blk(lambda j, i: (i, j)), pl.BlockSpec((4, cb), lambda j, i: (0, j)), pl.BlockSpec((1, cb), lambda j, i: (0, j))],
                  out_shape=[_S((T, C)), _S((4, C)), _S((1, C))], name=name)(x, x, x, w, b, dact, dact)


FOX_SCALE = 64 ** -0.5
LOG2E = 1.4426950408889634


def _spare(e, i):
    return (_lane() == 64 * (1 - e) + i).astype(F32)


def _lanes_of(e):
    lane = _lane()
    return ((lane < 64) if e == 0 else (lane >= 64)).astype(F32)


def _col3(col, e, first):
    c1 = col.astype(BF).astype(F32)
    c2 = (col - c1).astype(BF).astype(F32)
    c3 = (col - c1 - c2).astype(BF).astype(F32)
    return c1 * _spare(e, first) + c2 * _spare(e, first + 1) + c3 * _spare(e, first + 2)


def _ones3(e, first):
    return _spare(e, first) + _spare(e, first + 1) + _spare(e, first + 2)


def _causal_bias(n):
    return jnp.where(_iota((n, n), 0) >= _iota((n, n), 1), 0.0, -jnp.inf).astype(F32)


def _c_col(cc, hh):
    return jnp.sum(cc * (_lane() == FF0 + hh).astype(F32), axis=1, keepdims=True) * LOG2E


def _pair_spec(tq, row_of):
    return pl.BlockSpec((None, 2, tq, LANES), lambda hp, a, b: (hp, 0, row_of(a, b), 0))


def fox_prep(qkv, ccol, name):
    T = qkv.shape[0]
    tq = min(T, 512)

    def body(q_ref, k_ref, v_ref, cc_ref, qa_ref, ka_ref, va_ref):
        hp = pl.program_id(0)
        q, k, v, cc = q_ref[...], k_ref[...], v_ref[...], cc_ref[...]
        for e in range(2):
            me = _lanes_of(e)
            c2 = _c_col(cc, 2 * hp + e)
            qa_ref[e] = (q * me * (FOX_SCALE * LOG2E) + _col3(c2, e, 0) + _ones3(e, 3)).astype(BF)
            ka_ref[e] = (k * me + _ones3(e, 0) + _col3(-c2, e, 3) + _ones3(e, 6)).astype(BF)
            va_ref[e] = (v * me + (1.0 - me)).astype(BF)

    blk = lambda off: pl.BlockSpec((tq, LANES), lambda hp, i: (i, off + hp))
    out = pl.BlockSpec((None, 2, tq, LANES), lambda hp, i: (hp, 0, i, 0))
    return _pcall(body, grid=(4, T // tq), in_specs=[blk(0), blk(4), blk(8), pl.BlockSpec((tq, LANES), lambda hp, i: (i, 0))],
                  out_specs=[out] * 3, out_shape=[_S((4, 2, T, LANES), BF)] * 3, name=name)(qkv, qkv, qkv, ccol)


def fox_fwd(qa, ka, va, name, rider=None):
    T = qa.shape[2]
    tq = min(T, 512)
    nq = T // tq

    def body(qa_ref, ka_ref, va_ref, o_ref, lse_ref, m_s, acc, causal):
        i, j = pl.program_id(1), pl.program_id(2)

        @pl.when((pl.program_id(0) == 0) & (i == 0) & (j == 0))
        def _():
            causal[...] = _causal_bias(tq)

        @pl.when(j == 0)
        def _():
            m_s[...] = jnp.full(m_s.shape, -jnp.inf, F32)
            acc[...] = jnp.zeros(acc.shape, F32)

        def step(diagonal):
            for e in range(2):
                s = _dotb(qa_ref[e], ka_ref[e], NT)
                if diagonal:
                    s = s + causal[...]
                m_old = m_s[e]
                m_new = jnp.maximum(m_old, jnp.max(s, axis=1, keepdims=True))
                p = jnp.exp2(s - m_new)
                m_s[e] = m_new
                acc[e] = acc[e] * jnp.exp2(m_old - m_new) + _dotb(p.astype(BF), va_ref[e], NN)

        @pl.when(j < i)
        def _():
            step(False)

        @pl.when(j == i)
        def _():
            step(True)
            lane = _lane()
            o, lse = 0.0, 0.0
            for e in range(2):
                l = jnp.sum(acc[e] * _spare(e, 0), axis=1, keepdims=True)
                o = o + acc[e] * _lanes_of(e) / l
                lse = lse + (m_s[e] + jnp.log2(l)) * (lane == e).astype(F32)
            o_ref[...] = o
            lse_ref[...] = lse

    kv = _pair_spec(tq, lambda i, j: jnp.minimum(j, i))
    return hosted_call(body, rider, grid=(4, nq, nq), in_specs=[_pair_spec(tq, lambda i, j: i), kv, kv],
                       out_specs=[pl.BlockSpec((tq, LANES), lambda hp, i, j: (i, hp)), pl.BlockSpec((None, tq, LANES), lambda hp, i, j: (hp, i, 0))],
                       out_shape=[_S((T, BRANCH_W)), _S((4, T, LANES))],
                       scratch_shapes=[pltpu.VMEM((2, tq, 1), F32), pltpu.VMEM((2, tq, LANES), F32), pltpu.VMEM((tq, tq), F32)],
                       name=name, args=[qa, ka, va])


def fox_prep_bwd(qkv, qa, o, lse, do, name):
    T = qkv.shape[0]
    tq = min(T, 512)

    def body(q_ref, k_ref, qa_ref, o_ref, lse_ref, do_ref, qb_ref, doa_ref, qd_ref, kd_ref):
        q, k, dov = q_ref[...], k_ref[...], do_ref[...]
        dd = dov * o_ref[...]
        lane = _lane()
        for e in range(2):
            me = _lanes_of(e)
            lse_e = jnp.sum(lse_ref[...] * (lane == e).astype(F32), axis=1, keepdims=True)
            qb_ref[e] = (qa_ref[e].astype(F32) + _col3(-lse_e, e, 6)).astype(BF)
            doa_ref[e] = (dov * me + _col3(-jnp.sum(dd * me, axis=1, keepdims=True), e, 0)).astype(BF)
            qd_ref[e] = (q * me * FOX_SCALE + _spare(e, 0)).astype(BF)
            kd_ref[e] = (k * me * FOX_SCALE + _spare(e, 0)).astype(BF)

    blk = lambda off: pl.BlockSpec((tq, LANES), lambda hp, i: (i, off + hp))
    pair = pl.BlockSpec((None, 2, tq, LANES), lambda hp, i: (hp, 0, i, 0))
    return _pcall(body, grid=(4, T // tq),
                  in_specs=[blk(0), blk(4), pair, blk(0), pl.BlockSpec((None, tq, LANES), lambda hp, i: (hp, i, 0)), blk(0)],
                  out_specs=[pair] * 4, out_shape=[_S((4, 2, T, LANES), BF)] * 4, name=name)(qkv, qkv, qa, o, lse, do)


def fox_bwd(qb, ka, va, doa, qd, kd, name, rider=None):
    T = qb.shape[2]
    tq = min(T, 512)
    nq = T // tq

    def body(qb_ref, ka_ref, va_ref, doa_ref, qd_ref, kd_ref, dq_ref, dk_ref, dv_ref, dcc_ref, dq_s, dk_s, dv_s, causal):
        hp, j, ii = pl.program_id(0), pl.program_id(1), pl.program_id(2)
        i = jnp.maximum(ii, j)

        @pl.when((hp == 0) & (j == 0) & (ii == 0))
        def _():
            dcc_ref[...] = jnp.zeros(dcc_ref.shape, F32)
            causal[...] = _causal_bias(tq)

        @pl.when((j == 0) & (ii == 0))
        def _():
            dq_s[...] = jnp.zeros(dq_s.shape, F32)

        @pl.when(ii == 0)
        def _():
            dk_s[...] = jnp.zeros(dk_s.shape, F32)
            dv_s[...] = jnp.zeros(dv_s.shape, F32)

        def step(diagonal):
            rows = pl.ds(pl.multiple_of(i * tq, tq), tq)
            for e in range(2):
                s = _dotb(qb_ref[e], ka_ref[e], NT)
                if diagonal:
                    s = s + causal[...]
                p = jnp.exp2(s)
                ds = (p * _dotb(doa_ref[e], va_ref[e], NT)).astype(BF)
                dv_s[e] += _dotb(p.astype(BF), doa_ref[e], TN)
                dq_s[e, rows, :] += _dotb(ds, kd_ref[e], NN)
                dk_s[e] += _dotb(ds, qd_ref[e], TN)

        @pl.when(ii > j)
        def _():
            step(False)

        @pl.when(ii == j)
        def _():
            step(True)

        def fold(acc, sign):
            grad, dc = 0.0, 0.0
            for e in range(2):
                a = acc[e]
                grad = grad + a * _lanes_of(e)
                dc = dc + sign * jnp.sum(a * _spare(e, 0), axis=1, keepdims=True) * (_lane() == FF0 + 2 * hp + e).astype(F32)
            return grad, dc

        @pl.when(ii == nq - 1)
        def _():
            grad, dc = fold(dk_s, -1.0)
            dk_ref[...] = grad
            dv_ref[...] = dv_s[0] * _lanes_of(0) + dv_s[1] * _lanes_of(1)
            dcc_ref[pl.ds(pl.multiple_of(j * tq, tq), tq), :] += dc

        @pl.when((j == nq - 1) & (ii == nq - 1))
        def _():
            grad, dc = fold(dq_s, 1.0)
            dq_ref[...] = grad
            dcc_ref[...] += dc

    irow, jrow = _pair_spec(tq, lambda j, ii: jnp.maximum(ii, j)), _pair_spec(tq, lambda j, ii: j)
    jout = pl.BlockSpec((tq, LANES), lambda hp, j, ii: (j, hp))
    return hosted_call(body, rider, grid=(4, nq, nq), in_specs=[irow, jrow, jrow, irow, irow, jrow],
                       out_specs=[pl.BlockSpec((T, LANES), lambda hp, j, ii: (0, hp)), jout, jout, pl.BlockSpec((T, LANES), lambda hp, j, ii: (0, 0))],
                       out_shape=[_S((T, BRANCH_W)), _S((T, BRANCH_W)), _S((T, BRANCH_W)), _S((T, LANES))],
                       scratch_shapes=[pltpu.VMEM((2, T, LANES), F32), pltpu.VMEM((2, tq, LANES), F32), pltpu.VMEM((2, tq, LANES), F32),
                                       pltpu.VMEM((tq, tq), F32)],
                       name=name, args=[qb, ka, va, doa, qd, kd])


def _acc_out(ref, val, first):
    @pl.when(first)
    def _():
        ref[...] = val

    @pl.when(jnp.logical_not(first))
    def _():
        ref[...] += val


def _row(tm, c):
    return pl.BlockSpec((tm, c), lambda i: (i, 0))


def _full(shape):
    return pl.BlockSpec(shape, lambda *_: (0,) * len(shape))


def ln_fwd(x, g, b, name):
    T, C = x.shape
    tm = min(T, 512)

    def body(x_ref, g_ref, b_ref, o_ref):
        o_ref[...] = _ln(x_ref[...], g_ref[...], b_ref[...])

    return _pcall(body, grid=(T // tm,), in_specs=[_row(tm, C), _full((1, C)), _full((1, C))], out_specs=_row(tm, C),
                  out_shape=_S((T, C)), name=name)(x, g, b)


def ln_bwd(x, g, b, dy, name):
    T, C = x.shape
    tm = min(T, 512)

    def body(x_ref, g_ref, b_ref, dy_ref, dx_ref, dg_ref, db_ref):
        _, vjp = jax.vjp(_ln, x_ref[...], g_ref[...], b_ref[...])
        dx, dg, db = vjp(dy_ref[...])
        dx_ref[...] = dx
        first = pl.program_id(0) == 0
        _acc_out(dg_ref, dg, first)
        _acc_out(db_ref, db, first)

    return _pcall(body, grid=(T // tm,), in_specs=[_row(tm, C), _full((1, C)), _full((1, C)), _row(tm, C)],
                  out_specs=[_row(tm, C), _full((1, C)), _full((1, C))], out_shape=[_S((T, C)), _S((1, C)), _S((1, C))], name=name)(x, g, b, dy)


def loss_head(h, target, name):
    T, C = h.shape
    tm = min(T, 512)

    def body(h_ref, t_ref, d_ref, l_ref):
        e = h_ref[...] - t_ref[...]
        d_ref[...] = e * (1.0 / C)
        part = jnp.sum(jnp.sum(e * e, axis=1, keepdims=True), axis=0, keepdims=True) * (0.5 / C)
        _acc_out(l_ref, part, pl.program_id(0) == 0)

    return _pcall(body, grid=(T // tm,), in_specs=[_row(tm, C), _row(tm, C)], out_specs=[_row(tm, C), _full((1, 1))],
                  out_shape=[_S((T, C)), _S((1, 1))], name=name)(h, target)


def add3(a, b, c, name):
    T, C = a.shape
    tm = min(T, 512)

    def body(a_ref, b_ref, c_ref, o_ref):
        o_ref[...] = a_ref[...] + b_ref[...] + c_ref[...]

    return _pcall(body, grid=(T // tm,), in_specs=[_row(tm, C)] * 3, out_specs=_row(tm, C), out_shape=_S((T, C)), name=name)(a, b, c)


def _wb_spec(l):
    return pl.BlockSpec((None, 4, BRANCH_W, D_MODEL), lambda *_: (l, 0, 0, 0))


def merge_fwd(ys, gl, gb, wb, l, name):
    T = gl.shape[0]
    tm = min(T, 256)

    def body(y0, y1, y2, y3, gl_ref, gb_ref, wb_ref, o_ref):
        acc = 0.0
        for i, y in enumerate((y0, y1, y2, y3)):
            z = _dotb(y[...].astype(BF), wb_ref[i], NN)
            g = jax.nn.sigmoid(gl_ref[:, i * D_MODEL:(i + 1) * D_MODEL] + gb_ref[i:i + 1, :])
            acc = acc + g * z
        o_ref[...] = acc

    return _pcall(body, grid=(T // tm,), in_specs=[_row(tm, BRANCH_W)] * 4 + [_row(tm, 4 * D_MODEL), _full((4, D_MODEL)), _wb_spec(l)],
                  out_specs=_row(tm, D_MODEL), out_shape=_S((T, D_MODEL)), name=name)(*ys, gl, gb, wb)


def merge_bwd(ys, gl, gb, wb, l, dm, name):
    T = gl.shape[0]
    tm = min(T, 256)

    def body(y0, y1, y2, y3, gl_ref, gb_ref, wb_ref, dm_ref, d0, d1, d2, d3, dgl_ref, dz_ref, dgb_ref):
        dmv = dm_ref[...]
        first = pl.program_id(0) == 0
        for i, (y, d) in enumerate(zip((y0, y1, y2, y3), (d0, d1, d2, d3))):
            cols = slice(i * D_MODEL, (i + 1) * D_MODEL)
            z = _dotb(y[...].astype(BF), wb_ref[i], NN)
            g = jax.nn.sigmoid(gl_ref[:, cols] + gb_ref[i:i + 1, :])
            dgl = dmv * z * (g * (1.0 - g))
            dz = (g * dmv).astype(BF)
            dgl_ref[:, cols] = dgl
            dz_ref[:, cols] = dz
            d[...] = _dotb(dz, wb_ref[i], NT)
            _acc_out(dgb_ref.at[i:i + 1, :], jnp.sum(dgl, axis=0, keepdims=True), first)

    return _pcall(body, grid=(T // tm,),
                  in_specs=[_row(tm, BRANCH_W)] * 4 + [_row(tm, 4 * D_MODEL), _full((4, D_MODEL)), _wb_spec(l), _row(tm, D_MODEL)],
                  out_specs=[_row(tm, BRANCH_W)] * 4 + [_row(tm, 4 * D_MODEL), _row(tm, 4 * D_MODEL), _full((4, D_MODEL))],
                  out_shape=[_S((T, BRANCH_W))] * 4 + [_S((T, 4 * D_MODEL)), _S((T, 4 * D_MODEL), BF), _S((4, D_MODEL))], name=name)(
                      *ys, gl, gb, wb, dm)


def _wout_spec(l):
    return pl.BlockSpec((None, D_MODEL, D_MODEL), lambda *_: (l, 0, 0))


def out_fwd(merged, h, wout, l, g, b, name):
    T = h.shape[0]
    tm = min(T, 512)

    def body(m_ref, h_ref, w_ref, g_ref, b_ref, u_ref, o_ref):
        u = ALPHA * h_ref[...] + _dotb(m_ref[...].astype(BF), w_ref[...], NN)
        u_ref[...] = u
        o_ref[...] = _ln(u, g_ref[...], b_ref[...])

    C = D_MODEL
    return _pcall(body, grid=(T // tm,), in_specs=[_row(tm, C), _row(tm, C), _wout_spec(l), _full((1, C)), _full((1, C))],
                  out_specs=[_row(tm, C), _row(tm, C)], out_shape=[_S((T, C)), _S((T, C))], name=name)(merged, h, wout, g, b)


def out_bwd(u, dy, g, b, wout, l, name):
    T, C = u.shape
    tm = min(T, 512)

    def body(u_ref, dy_ref, g_ref, b_ref, w_ref, du_ref, dm_ref, dg_ref, db_ref):
        _, vjp = jax.vjp(_ln, u_ref[...], g_ref[...], b_ref[...])
        du, dg, db = vjp(dy_ref[...])
        du_ref[...] = du
        dm_ref[...] = _dotb(du.astype(BF), w_ref[...], NT)
        first = pl.program_id(0) == 0
        _acc_out(dg_ref, dg, first)
        _acc_out(db_ref, db, first)

    return _pcall(body, grid=(T // tm,), in_specs=[_row(tm, C), _row(tm, C), _full((1, C)), _full((1, C)), _wout_spec(l)],
                  out_specs=[_row(tm, C), _row(tm, C), _full((1, C)), _full((1, C))],
                  out_shape=[_S((T, C)), _S((T, C)), _S((1, C)), _S((1, C))], name=name)(u, dy, g, b, wout)


def ff_fwd(h, wup, wdown, l, g, b, name):
    T, C = h.shape
    F = wup.shape[2]
    tm, tf = min(T, 512), 1024
    nf = F // tf

    def body(h_ref, wu_ref, wd_ref, g_ref, b_ref, u_ref, o_ref, acc):
        f = pl.program_id(1)
        a = _dotb(h_ref[...].astype(BF), wu_ref[...], NN)
        r = jnp.square(jnp.maximum(a, 0.0))
        p = _dotb(r.astype(BF), wd_ref[...], NN)
        _acc_out(acc, p, f == 0)

        @pl.when(f == nf - 1)
        def _():
            u = ALPHA * h_ref[...] + acc[...]
            u_ref[...] = u
            o_ref[...] = _ln(u, g_ref[...], b_ref[...])

    row = pl.BlockSpec((tm, C), lambda i, f: (i, 0))
    return _pcall(body, grid=(T // tm, nf),
                  in_specs=[row, pl.BlockSpec((None, C, tf), lambda i, f: (l, 0, f)), pl.BlockSpec((None, tf, C), lambda i, f: (l, f, 0)),
                            _full((1, C)), _full((1, C))],
                  out_specs=[row, row], out_shape=[_S((T, C)), _S((T, C))], scratch_shapes=[pltpu.VMEM((tm, C), F32)], name=name)(h, wup, wdown, g, b)


def ff_bwd(u, dy, h, g, b, wup, wdown, l, name):
    T, C = h.shape
    F = wup.shape[2]
    tm, tf = min(T, 512), 1024
    nf = F // tf

    def body(u_ref, dy_ref, h_ref, g_ref, b_ref, wu_ref, wd_ref, du_ref, dh_ref, da_ref, r_ref, dg_ref, db_ref, du_s, acc):
        i, f = pl.program_id(0), pl.program_id(1)

        @pl.when(f == 0)
        def _():
            _, vjp = jax.vjp(_ln, u_ref[...], g_ref[...], b_ref[...])
            du, dg, db = vjp(dy_ref[...])
            du_s[...] = du
            du_ref[...] = du
            _acc_out(dg_ref, dg, i == 0)
            _acc_out(db_ref, db, i == 0)

        a = _dotb(h_ref[...].astype(BF), wu_ref[...], NN)
        ap = jnp.maximum(a, 0.0)
        dr = _dotb(du_s[...].astype(BF), wd_ref[...], NT)
        da = (dr * (2.0 * ap)).astype(BF)
        da_ref[...] = da
        r_ref[...] = jnp.square(ap).astype(BF)
        _acc_out(acc, _dotb(da, wu_ref[...], NT), f == 0)

        @pl.when(f == nf - 1)
        def _():
            dh_ref[...] = ALPHA * du_s[...] + acc[...]

    row = pl.BlockSpec((tm, C), lambda i, f: (i, 0))
    colf = pl.BlockSpec((tm, tf), lambda i, f: (i, f))
    return _pcall(body, grid=(T // tm, nf),
                  in_specs=[row, row, row, _full((1, C)), _full((1, C)), pl.BlockSpec((None, C, tf), lambda i, f: (l, 0, f)),
                            pl.BlockSpec((None, tf, C), lambda i, f: (l, f, 0))],
                  out_specs=[row, row, colf, colf, _full((1, C)), _full((1, C))],
                  out_shape=[_S((T, C)), _S((T, C)), _S((T, F), BF), _S((T, F), BF), _S((1, C)), _S((1, C))],
                  scratch_shapes=[pltpu.VMEM((tm, C), F32), pltpu.VMEM((tm, C), F32)], name=name)(u, dy, h, g, b, wup, wdown)


MESH_ID = pl.DeviceIdType.MESH
_ANY = pl.BlockSpec(memory_space=pl.ANY)


def _window(ref, ax, idx, n):
    if n < 0:
        return ref
    sel = idx if n == 0 else pl.ds(pl.multiple_of(idx * n, n), n)
    return ref.at[(slice(None),) * ax + (sel,)]


Rider = collections.namedtuple("Rider", "operands out_shape scratch start wait")


def hosted_call(body, rider, *, grid, in_specs, out_specs, out_shape, scratch_shapes, name, args):
    n_in, n_out, n_scr = len(in_specs), len(out_specs), len(scratch_shapes)
    if rider is None:
        return _pcall(body, grid=grid, in_specs=in_specs, out_specs=out_specs, out_shape=out_shape, scratch_shapes=scratch_shapes, name=name)(*args), []
    ri, ro = len(rider.operands), len(rider.out_shape)

    def wrapped(*refs):
        ins, r_in = refs[:n_in], refs[n_in:n_in + ri]
        o0 = n_in + ri
        outs, r_out = refs[o0:o0 + n_out], refs[o0 + n_out:o0 + n_out + ro]
        s0 = o0 + n_out + ro
        scr, r_scr = refs[s0:s0 + n_scr], refs[s0 + n_scr:]
        ids = [pl.program_id(i) for i in range(len(grid))]
        first = functools.reduce(jnp.logical_and, [i == 0 for i in ids])
        last = functools.reduce(jnp.logical_and, [i == g - 1 for i, g in zip(ids, grid)])

        @pl.when(first)
        def _():
            rider.start(r_in, r_out, r_scr)

        body(*ins, *outs, *scr)

        @pl.when(last)
        def _():
            rider.wait(r_in, r_out, r_scr)

    res = _pcall(wrapped, grid=grid, in_specs=list(in_specs) + [_ANY] * ri, out_specs=list(out_specs) + [_ANY] * ro,
                 out_shape=list(out_shape) + list(rider.out_shape), scratch_shapes=list(scratch_shapes) + list(rider.scratch),
                 name=name)(*args, *rider.operands)
    return res[:n_out], res[n_out:]


def comm_call(rider, name):
    ri = len(rider.operands)

    def body(*refs):
        r_in, r_out, r_scr = refs[:ri], refs[ri:ri + len(rider.out_shape)], refs[ri + len(rider.out_shape):]
        rider.start(r_in, r_out, r_scr)
        rider.wait(r_in, r_out, r_scr)

    return _pcall(body, in_specs=[_ANY] * ri, out_specs=[_ANY] * len(rider.out_shape), out_shape=list(rider.out_shape),
                  scratch_shapes=list(rider.scratch), name=name)(*rider.operands)


def gather_rider(shards, axes):
    K = len(shards)
    widths = [s.shape[a] for s, a in zip(shards, axes)]
    out_shape = [_S(s.shape[:a] + (N_DEV * s.shape[a],) + s.shape[a + 1:], s.dtype) for s, a in zip(shards, axes)]

    def plan(x_refs, o_refs, sems):
        send_sems, recv_sems, local_sems = sems
        mx, my, mc = lax.axis_index("x"), lax.axis_index("y"), lax.axis_index("c")
        me, sibling = (mx, my, mc), (mx, my, 1 - mc)
        chips = [(1 - mx, my), (mx, 1 - my), (1 - mx, 1 - my)]

        def win(k, px, py, pc):
            return _window(o_refs[k], axes[k], 4 * px + 2 * py + pc, widths[k])

        def copy(k, slot, block, to, src=None):
            return pltpu.make_async_remote_copy(src_ref=win(k, *block) if src is None else src, dst_ref=win(k, *block),
                                                send_sem=send_sems.at[7 * k + slot], recv_sem=recv_sems.at[7 * k + slot],
                                                device_id=to, device_id_type=MESH_ID)

        mine = [pltpu.make_async_copy(x_refs[k], win(k, *me), local_sems.at[k]) for k in range(K)]
        first = []
        for k in range(K):
            first.append(copy(k, 0, me, sibling, src=x_refs[k]))
            first += [copy(k, 1 + j, me, (*chip, mc), src=x_refs[k]) for j, chip in enumerate(chips)]
        return me, sibling, chips, copy, mine, first

    def start(x_refs, o_refs, sems):
        _, _, _, _, mine, first = plan(x_refs, o_refs, sems)
        for cp in mine + first:
            cp.start()

    def wait(x_refs, o_refs, sems):
        me, sibling, chips, copy, mine, first = plan(x_refs, o_refs, sems)
        mc = me[2]
        passed = []
        for j, chip in enumerate(chips):
            for k in range(K):
                copy(k, 1 + j, (*chip, mc), me).wait_recv()
                passed.append(copy(k, 4 + j, (*chip, mc), sibling))
                passed[-1].start()
        for k in range(K):
            copy(k, 0, sibling, me).wait_recv()
        for j, chip in enumerate(chips):
            for k in range(K):
                copy(k, 4 + j, (*chip, 1 - mc), me).wait_recv()
        for cp in first + passed:
            cp.wait_send()
        for cp in mine:
            cp.wait()

    scratch = [pltpu.SemaphoreType.DMA((7 * K,)), pltpu.SemaphoreType.DMA((7 * K,)), pltpu.SemaphoreType.DMA((K,))]
    return Rider(list(shards), out_shape, scratch, start, wait)


def exchange_rider(items):
    ns = len(items)
    out_shape = [_S((N_DEV,) + tuple(it[3]), it[0].dtype) for it in items]

    def plan(src_refs, o_refs, sems):
        send_sems, recv_sems, local_sems = sems
        mx, my, mc = lax.axis_index("x"), lax.axis_index("y"), lax.axis_index("c")
        me = 4 * mx + 2 * my + mc
        remote, own = [], []
        for s, (_, ax, n, _) in enumerate(items):
            own.append(pltpu.make_async_copy(_window(src_refs[s], ax, me, n), o_refs[s].at[me], local_sems.at[s]))
            for k in range(1, N_DEV):
                px = 1 - mx if k & 4 else mx
                py = 1 - my if k & 2 else my
                pc = 1 - mc if k & 1 else mc
                remote.append(pltpu.make_async_remote_copy(
                    src_ref=_window(src_refs[s], ax, 4 * px + 2 * py + pc, n), dst_ref=o_refs[s].at[me],
                    send_sem=send_sems.at[7 * s + k - 1], recv_sem=recv_sems.at[7 * s + k - 1],
                    device_id=(px, py, pc), device_id_type=MESH_ID))
        return remote, own

    def start(src_refs, o_refs, sems):
        remote, own = plan(src_refs, o_refs, sems)
        for cp in own + remote:
            cp.start()

    def wait(src_refs, o_refs, sems):
        remote, own = plan(src_refs, o_refs, sems)
        for cp in remote + own:
            cp.wait()

    scratch = [pltpu.SemaphoreType.DMA((7 * ns,)), pltpu.SemaphoreType.DMA((7 * ns,)), pltpu.SemaphoreType.DMA((ns,))]
    return Rider([it[0] for it in items], out_shape, scratch, start, wait)


def reduce_adamw(rcvs, w, m, v, name):
    L = len(rcvs)
    _, A, B, C = rcvs[0].shape
    tb = B
    while tb > 8 and tb * C > (1 << 17):
        tb //= 2

    def body(*refs):
        r_refs, (w_ref, m_ref, v_ref, g_ref, d_ref, mo_ref, vo_ref) = refs[:L], refs[L:]
        for k in range(L):
            @pl.when(pl.program_id(0) == k)
            def _(k=k):
                g = r_refs[k][0].astype(F32)
                for d in range(1, N_DEV):
                    g = g + r_refs[k][d].astype(F32)
                mn = ADAM_B1 * m_ref[...] + (1.0 - ADAM_B1) * g
                vn = ADAM_B2 * v_ref[...] + (1.0 - ADAM_B2) * jnp.square(g)
                m_hat = mn / (1.0 - ADAM_B1 ** ADAM_STEP)
                v_hat = vn / (1.0 - ADAM_B2 ** ADAM_STEP)
                g_ref[...] = g
                d_ref[...] = -ADAM_LR * (m_hat / (jnp.sqrt(v_hat) + ADAM_EPS) + ADAM_WD * w_ref[...])
                mo_ref[...] = mn
                vo_ref[...] = vn

    def rspec(k):
        return pl.BlockSpec((N_DEV, None, tb, C), lambda l, a, i: (0, jnp.where(l == k, a, 0), jnp.where(l == k, i, 0), 0))

    blk = pl.BlockSpec((None, tb, C), lambda l, a, i: (l * A + a, i, 0))
    return _pcall(body, grid=(L, A, B // tb), in_specs=[rspec(k) for k in range(L)] + [blk, blk, blk],
                  out_specs=[blk] * 4, out_shape=[_S((L * A, B, C))] * 4, name=name)(*rcvs, w, m, v)


def _w_in_pieces(g0, g1):
    per = D_IN // N_DEV
    return [(d, max(g0, d * per) - d * per, min(g1, (d + 1) * per) - d * per) for d in range(N_DEV) if max(g0, d * per) < min(g1, (d + 1) * per)]


def repack_w_in(w8, name):
    _, L, R, per = w8.shape
    tr = 256

    def cols(x_ref, g0, g1):
        return [x_ref[d, :, a:b] for d, a, b in _w_in_pieces(g0, g1)]

    def body(x_ref, *o_refs):
        for (name_, i), o_ref in zip(SEGS, o_refs):
            o_ref[...] = jnp.concatenate(cols(x_ref, _OFF[i], _OFF[i + 1]), axis=1)
        parts, at = [], 0
        for i, lane0 in SMALL_SRC:
            assert lane0 == at
            parts += cols(x_ref, _OFF[i], _OFF[i + 1])
            at += IN_SIZES[i]
        parts.append(jnp.zeros((tr, LANES - at), w8.dtype))
        o_refs[-1][...] = jnp.concatenate(parts, axis=1)

    widths = [IN_SIZES[i] for _, i in SEGS] + [LANES]
    outs = _pcall(body, grid=(L, R // tr), in_specs=[pl.BlockSpec((N_DEV, None, tr, per), lambda l, r: (0, l, r, 0))],
                  out_specs=[pl.BlockSpec((None, tr, w), lambda l, r: (l, r, 0)) for w in widths],
                  out_shape=[_S((L, R, w), w8.dtype) for w in widths], name=name)(w8)
    return dict(zip(SEG_NAMES, outs))


def repack_dw_in(dseg, name):
    R = dseg["z"].shape[0]
    per = D_IN // N_DEV
    tr = 128
    src = {i: (k, 0) for k, (_, i) in enumerate(SEGS)}
    src.update({i: (len(SEGS), lane0) for i, lane0 in SMALL_SRC})

    def body(*refs):
        s_refs, o_ref = refs[:-1], refs[-1]
        for d in range(N_DEV):
            parts = []
            for i in range(len(IN_SIZES)):
                g0, g1 = max(_OFF[i], d * per), min(_OFF[i + 1], (d + 1) * per)
                if g0 < g1:
                    k, c0 = src[i]
                    parts.append(s_refs[k][:, c0 + g0 - _OFF[i]:c0 + g1 - _OFF[i]])
            o_ref[d] = jnp.concatenate(parts, axis=1)

    arrs = [dseg[n] for n in SEG_NAMES]
    return _pcall(body, grid=(R // tr,), in_specs=[pl.BlockSpec((tr, a.shape[1]), lambda r: (r, 0)) for a in arrs],
                  out_specs=pl.BlockSpec((N_DEV, tr, per), lambda r: (0, r, 0)), out_shape=_S((N_DEV, R, per), arrs[0].dtype), name=name)(*arrs)


WEIGHTS = ("ln_in_g", "ln_in_b", "w_in", "ssd_conv_w", "ssd_conv_b", "ssd_dt_bias", "ssd_a_log", "ssd_d", "ssd_norm_w", "dn_conv_w",
           "dn_a_log", "dn_dt_bias", "dn_norm_w", "sg_ln_g", "sg_ln_b", "sg_w", "sg_b", "fox_f_bias", "gate_b", "w_branch", "w_out",
           "ln1_g", "ln1_b", "w_up", "w_down", "ln2_g", "ln2_b")
SHARDED = {"w_in": 2, "ssd_conv_w": 2, "dn_conv_w": 2, "gate_b": 2, "w_branch": 3, "w_out": 1, "w_up": 2, "w_down": 1}
SLABBED = ("w_in", "dn_conv_w")
MATMUL_WEIGHTS = ("w_in", "w_branch", "w_out", "w_up", "w_down")
REPLICATED = tuple(n for n in WEIGHTS if n not in SHARDED)
SEG_NAMES = tuple(n for n, _ in SEGS) + ("small",)
PACK_COLS = 1024


def _lanes(vec, off):
    return jnp.pad(vec, (off, LANES - off - vec.shape[0]))[None]


def _pack_small(parts):
    flat = jnp.concatenate([q.reshape(-1) for q in parts])
    rows = -(-flat.shape[0] // (PACK_COLS * 64)) * 64
    return jnp.pad(flat, (0, rows * PACK_COLS - flat.shape[0])).reshape(1, rows, PACK_COLS)


EARLY = ("w_branch", "w_out", "w_up", "w_down", "gate_b")
LATE = ("w_in", "ssd_conv_w", "dn_conv_w")


def _gather_rider(p, l, names):
    shards, axes = [], []
    for n in names:
        s = p[n][l:l + 1]
        s = s.astype(BF) if n in MATMUL_WEIGHTS else s
        shards.append(s[None] if n in SLABBED else s)
        axes.append(0 if n in SLABBED else SHARDED[n])
    return gather_rider(shards, axes)


def _exchange_items(g, p, names):
    items = []
    for n in names:
        local = p[n].shape[1:]
        items.append((g[n], 0, 0, local) if n in SLABBED else (g[n], SHARDED[n] - 1, local[SHARDED[n] - 1], local))
    return items


def _use_gathered(w, names, arrays, l):
    for n, arr in zip(names, arrays):
        if n == "w_in":
            w[n] = repack_w_in(arr, f"w_in_repack_{l}")
        elif n == "ssd_conv_w":
            w["ssd_cw"] = arr[0]
        elif n == "dn_conv_w":
            w["dn_cw"] = jnp.moveaxis(arr[:, 0], 0, 1).reshape(4, 3 * BRANCH_W)
        elif n == "gate_b":
            w[n] = arr[0]
        else:
            w[n] = arr


def _layer_weights(p, l):
    w = {}
    w["ssd_cb"] = p["ssd_conv_b"][l][None]
    w["dn_cb"] = jnp.zeros((1, 3 * BRANCH_W), F32)
    w["ssd_ps"] = [_lanes(p["ssd_dt_bias"][l], DT0), _lanes(p["ssd_a_log"][l], DT0), _lanes(p["ssd_d"][l], DT0), p["ssd_norm_w"][l][None]]
    w["dn_ps"] = [_lanes(p["dn_a_log"][l], A0), _lanes(p["dn_dt_bias"][l], A0), p["dn_norm_w"][l][None]]
    w["sg_ps"] = [p["sg_ln_g"][l][None], p["sg_ln_b"][l][None], p["sg_w"][l], jnp.pad(p["sg_b"][l].T, ((0, 0), (0, LANES - 4)))]
    w["fox_ps"] = [_lanes(p["fox_f_bias"][l], FF0)]
    for n in ("ln1_g", "ln1_b", "ln2_g", "ln2_b"):
        w[n] = p[n][l][None]
    return w


def _scan_specs(T, a):
    c0 = lambda c, h: (c, 0)
    ssd = dict(f=ssd_chunk, xs=[(a["z"], (128, 512), c0), (a["xbc_act"], (128, 1024), c0), (a["small"], (128, LANES), c0)],
               ys=[((T, BRANCH_W), (128, BRANCH_W), c0)], state=(4, LANES, LANES), nc=T // 128, nh=1, shared=())
    dn = dict(f=dn_chunk, xs=[(a["dn_act"], (64, 3 * BRANCH_W), c0), (a["dngate"], (64, BRANCH_W), c0), (a["small"], (64, LANES), c0)],
              ys=[((T, BRANCH_W), (64, BRANCH_W), c0)], state=(4, LANES, LANES), nc=T // 64, nh=1, shared=())
    sg = dict(f=sg_chunk, xs=[(a["sguv"], (128, 1024), c0)], ys=[((T, BRANCH_W), (128, BRANCH_W), c0)], state=(1, 8, LANES), nc=T // 128, nh=1, shared=())
    fc = dict(f=foxc_chunk, xs=[(a["small"], (128, LANES), c0)],
              ys=[((T, LANES), (128, LANES), c0)], state=(1, 1, LANES), nc=T // 128, nh=1, shared=())
    return ssd, dn, sg, fc


def _layer_fwd(h, w, l, dn_rider=None, fox_rider=None):
    T = h.shape[0]
    a = {"h": h}
    for n in SEG_NAMES:
        a[n] = matmul_w(h, w["w_in"][n], 0, "nn", f"proj_{n}_{l}")
    a["xbc_act"] = conv_fwd(a["xbc"], w["ssd_cw"], w["ssd_cb"], f"ssd_conv_{l}")
    a["dn_act"] = conv_fwd(a["dnqkv"], w["dn_cw"], w["dn_cb"], f"dn_conv_{l}")
    ssd, dn, sg, fc = _scan_specs(T, a)
    (a["ya"], a["ssd_st"]), _ = scan_fwd(f"ssd_fwd_{l}", ssd["f"], ssd["xs"], w["ssd_ps"], ssd["ys"], ssd["state"], ssd["nc"], ssd["nh"])
    (a["yb"], a["dn_st"]), got = scan_fwd(f"dn_fwd_{l}", dn["f"], dn["xs"], w["dn_ps"], dn["ys"], dn["state"], dn["nc"], dn["nh"], rider=dn_rider)
    _use_gathered(w, EARLY, got, l)
    (a["yc"], a["sg_st"]), _ = scan_fwd(f"sg_fwd_{l}", sg["f"], sg["xs"], w["sg_ps"], sg["ys"], sg["state"], sg["nc"], sg["nh"])
    (a["ccol"], a["fc_st"]), _ = scan_fwd(f"foxc_fwd_{l}", fc["f"], fc["xs"], w["fox_ps"], fc["ys"], fc["state"], fc["nc"], fc["nh"])
    a["fox_qa"], a["fox_ka"], a["fox_va"] = fox_prep(a["foxqkv"], a["ccol"], f"fox_prep_{l}")
    (a["yd"], a["lse"]), carried = fox_fwd(a["fox_qa"], a["fox_ka"], a["fox_va"], f"fox_fwd_{l}", rider=fox_rider)
    a["merged"] = merge_fwd([a["ya"], a["yb"], a["yc"], a["yd"]], a["gates"], w["gate_b"], w["w_branch"], 0, f"merge_fwd_{l}")
    a["u1"], a["h1"] = out_fwd(a["merged"], h, w["w_out"], 0, w["ln1_g"], w["ln1_b"], f"out_fwd_{l}")
    a["u2"], a["h2"] = ff_fwd(a["h1"], w["w_up"], w["w_down"], 0, w["ln2_g"], w["ln2_b"], f"ff_fwd_{l}")
    return a, carried


def _layer_bwd(dh2, a, w, l, p, late_above):
    T = dh2.shape[0]
    g = {}
    du2, dh1, da, r, dg2, db2 = ff_bwd(a["u2"], dh2, a["h1"], w["ln2_g"], w["ln2_b"], w["w_up"], w["w_down"], 0, f"ff_bwd_{l}")
    g["ln2_g"], g["ln2_b"] = dg2[0], db2[0]
    g["w_up"] = matmul_w(transpose_bf16(a["h1"], f"h1_t_{l}"), da[None], 0, "nn", f"dwup_{l}", out_dtype=BF)
    g["w_down"] = matmul_tn(r, du2, f"dwdown_{l}", out_dtype=BF)
    du1, dmerged, dg1, db1 = out_bwd(a["u1"], dh1, w["ln1_g"], w["ln1_b"], w["w_out"], 0, f"out_bwd_{l}")
    g["ln1_g"], g["ln1_b"] = dg1[0], db1[0]
    g["w_out"] = matmul_w(transpose_bf16(a["merged"], f"merged_t_{l}"), du1[None], 0, "nn", f"dwout_{l}", out_dtype=BF)
    ys = [a["ya"], a["yb"], a["yc"], a["yd"]]
    dya, dyb, dyc, dyd, dgl, dz, dgb = merge_bwd(ys, a["gates"], w["gate_b"], w["w_branch"], 0, dmerged, f"merge_bwd_{l}")
    g["gate_b"] = dgb
    g["w_branch"] = jnp.stack([matmul_tn(ys[i], dz, f"dwb{i}_{l}", b_col0=i * D_MODEL, n_cols=D_MODEL, out_dtype=BF) for i in range(4)])
    early = exchange_rider(_exchange_items(g, p, EARLY))
    dn_rider = early if late_above is None else exchange_rider(late_above)
    fox_rider = None if late_above is None else early
    ssd, dn, sg, fc = _scan_specs(T, a)
    (dz_ssd, dxbc_act, dsm_ssd, d_dtb, d_alog, d_dsk, d_nw), _ = scan_bwd(f"ssd_bwd_{l}", ssd["f"], ssd["xs"], w["ssd_ps"], ssd["ys"], [dya], a["ssd_st"],
                                                                           ssd["state"], ssd["nc"], ssd["nh"])
    g["ssd_dt_bias"], g["ssd_a_log"], g["ssd_d"], g["ssd_norm_w"] = d_dtb[0, DT0:DT0 + 8], d_alog[0, DT0:DT0 + 8], d_dsk[0, DT0:DT0 + 8], d_nw[0]
    dxbc, g["ssd_conv_w"], dcb = conv_bwd(a["xbc"], w["ssd_cw"], w["ssd_cb"], dxbc_act, f"ssd_conv_bwd_{l}")
    g["ssd_conv_b"] = dcb[0]
    (ddn_act, ddngate, dsm_dn, d_alog, d_dtb, d_nw), got_dn = scan_bwd(f"dn_bwd_{l}", dn["f"], dn["xs"], w["dn_ps"], dn["ys"], [dyb], a["dn_st"],
                                                                        dn["state"], dn["nc"], dn["nh"], rider=dn_rider)
    g["dn_a_log"], g["dn_dt_bias"], g["dn_norm_w"] = d_alog[0, A0:A0 + 4], d_dtb[0, A0:A0 + 4], d_nw[0]
    ddnqkv, g["dn_conv_w"], _ = conv_bwd(a["dnqkv"], w["dn_cw"], w["dn_cb"], ddn_act, f"dn_conv_bwd_{l}")
    (dsguv, d_lng, d_lnb, d_w, d_bt), _ = scan_bwd(f"sg_bwd_{l}", sg["f"], sg["xs"], w["sg_ps"], sg["ys"], [dyc], a["sg_st"], sg["state"], sg["nc"], sg["nh"])
    g["sg_ln_g"], g["sg_ln_b"], g["sg_w"], g["sg_b"] = d_lng[0], d_lnb[0], d_w, d_bt[:, :4].T
    qb, doa, qd, kd = fox_prep_bwd(a["foxqkv"], a["fox_qa"], a["yd"], a["lse"], dyd, f"fox_prep_bwd_{l}")
    (dfq, dfk, dfv, dccol), got_fox = fox_bwd(qb, a["fox_ka"], a["fox_va"], doa, qd, kd, f"fox_bwd_{l}", rider=fox_rider)
    (dsm_fox, d_fb), _ = scan_bwd(f"foxc_bwd_{l}", fc["f"], fc["xs"], w["fox_ps"], fc["ys"], [dccol], a["fc_st"], fc["state"], fc["nc"], fc["nh"])
    g["fox_f_bias"] = d_fb[0, FF0:FF0 + 8]
    dseg = {"z": dz_ssd, "xbc": dxbc, "dnqkv": ddnqkv, "dngate": ddngate, "sguv": dsguv,
            "foxqkv": jnp.concatenate([dfq, dfk, dfv], axis=1), "gates": dgl, "small": add3(dsm_ssd, dsm_dn, dsm_fox, f"dsmall_{l}")}
    dh, scale, dwin = du1, ALPHA, {}
    h_t = transpose_bf16(a["h"], f"h_t_{l}")
    for n in SEG_NAMES:
        dh = matmul_w(dseg[n], w["w_in"][n], 0, "nt", f"dh_{n}_{l}", add=dh, add_scale=scale)
        scale = 1.0
        dwin[n] = matmul_w(h_t, dseg[n][None], 0, "nn", f"dwin_{n}_{l}", out_dtype=BF)
    g["w_in"] = repack_dw_in(dwin, f"dw_in_repack_{l}")
    g["dn_conv_w"] = jnp.moveaxis(g["dn_conv_w"].reshape(4, N_DEV, 3 * BRANCH_W // N_DEV), 1, 0)
    got = {(EARLY, l): got_dn} if late_above is None else {(LATE, l + 1): got_dn, (EARLY, l): got_fox}
    return dh, g, got


def kernel(x, ln_in_g, ln_in_b, w_in, ssd_conv_w, ssd_conv_b, ssd_dt_bias, ssd_a_log, ssd_d, ssd_norm_w, dn_conv_w, dn_a_log, dn_dt_bias, dn_norm_w, sg_ln_g, sg_ln_b, sg_w, sg_b, fox_f_bias, gate_b, w_branch, w_out, ln1_g, ln1_b, w_up, w_down, ln2_g, ln2_b, loss_target, m_ln_in_g, m_ln_in_b, m_w_in, m_ssd_conv_w, m_ssd_conv_b, m_ssd_dt_bias, m_ssd_a_log, m_ssd_d, m_ssd_norm_w, m_dn_conv_w, m_dn_a_log, m_dn_dt_bias, m_dn_norm_w, m_sg_ln_g, m_sg_ln_b, m_sg_w, m_sg_b, m_fox_f_bias, m_gate_b, m_w_branch, m_w_out, m_ln1_g, m_ln1_b, m_w_up, m_w_down, m_ln2_g, m_ln2_b, v_ln_in_g, v_ln_in_b, v_w_in, v_ssd_conv_w, v_ssd_conv_b, v_ssd_dt_bias, v_ssd_a_log, v_ssd_d, v_ssd_norm_w, v_dn_conv_w, v_dn_a_log, v_dn_dt_bias, v_dn_norm_w, v_sg_ln_g, v_sg_ln_b, v_sg_w, v_sg_b, v_fox_f_bias, v_gate_b, v_w_branch, v_w_out, v_ln1_g, v_ln1_b, v_w_up, v_w_down, v_ln2_g, v_ln2_b):
    args = dict(locals())
    p = {n: args[n] for n in WEIGHTS}
    xt, target = x[0], loss_target[0]
    ws, acts = [_layer_weights(p, l) for l in range(DEPTH)], []
    _use_gathered(ws[0], LATE, comm_call(_gather_rider(p, 0, LATE), "weights_all_gather_0"), 0)
    h = ln_fwd(xt, ln_in_g[None], ln_in_b[None], "ln_in_fwd")
    for l in range(DEPTH):
        a, gathered = _layer_fwd(h, ws[l], l, dn_rider=_gather_rider(p, 0, EARLY) if l == 0 else None,
                                 fox_rider=_gather_rider(p, l + 1, LATE + EARLY) if l + 1 < DEPTH else None)
        if l + 1 < DEPTH:
            _use_gathered(ws[l + 1], LATE + EARLY, gathered, l + 1)
        acts.append(a)
        h = a["h2"]
    dh, loss = loss_head(h, target, "loss_head")
    loss = lax.psum(loss[0, 0], ("x", "y", "c"))

    layer_grads, got, late = [None] * DEPTH, {}, None
    for l in reversed(range(DEPTH)):
        dh, layer_grads[l], got_l = _layer_bwd(dh, acts[l], ws[l], l, p, late)
        got.update(got_l)
        late = _exchange_items(layer_grads[l], p, LATE)
    grad_x, dg_in, db_in = ln_bwd(xt, ln_in_g[None], ln_in_b[None], dh, "ln_in_bwd")
    small = {n: jnp.stack([layer_grads[l][n] for l in range(DEPTH)]) for n in REPLICATED if n not in ("ln_in_g", "ln_in_b")}
    small["ln_in_g"], small["ln_in_b"] = dg_in[0], db_in[0]
    pack = _pack_small([small[n] for n in REPLICATED])
    last = comm_call(exchange_rider(late + [(pack[0], 0, -1, pack.shape[1:])]), "grads_exchange_last")
    got[(LATE, 0)] = last[:-1]
    rcv = {(n, l): arr for (names, l), arrs in got.items() for n, arr in zip(names, arrs)}

    res = [{}, {}, {}, {}]
    for n in SHARDED:
        shp = p[n].shape
        lead = math.prod(shp[1:-2])
        to3 = lambda t: t.reshape((-1,) + shp[-2:])
        outs = reduce_adamw([rcv[(n, l)].reshape((N_DEV, lead) + shp[-2:]) for l in range(DEPTH)],
                            to3(p[n]), to3(args["m_" + n]), to3(args["v_" + n]), f"adamw_{n}")
        for k in range(4):
            res[k][n] = outs[k].reshape(shp)
    outs = reduce_adamw([last[-1][:, None]], _pack_small([p[n] for n in REPLICATED]), _pack_small([args["m_" + n] for n in REPLICATED]),
                        _pack_small([args["v_" + n] for n in REPLICATED]), "adamw_replicated")
    off = 0
    for n in REPLICATED:
        shp = p[n].shape
        cnt = math.prod(shp)
        for k in range(4):
            res[k][n] = outs[k].reshape(-1)[off:off + cnt].reshape(shp)
        off += cnt
    return (loss, grad_x[None], *[res[0][n] for n in WEIGHTS], *[res[1][n] for n in WEIGHTS],
            *[res[2][n] for n in WEIGHTS], *[res[3][n] for n in WEIGHTS])
```

```python
import collections
import functools
import math

import jax
import jax.numpy as jnp
from jax import lax
from jax.experimental import pallas as pl
from jax.experimental.pallas import tpu as pltpu

F32 = jnp.float32
BF = jnp.bfloat16

D_MODEL = 1024
DEPTH = 2
BRANCH_W = 512
D_FF = 4096
LN_EPS = 1e-5
NORM_EPS = 1e-6
ALPHA = (2 * DEPTH) ** 0.25
N_DEV = 8
LANES = 128
ADAM_LR, ADAM_B1, ADAM_B2, ADAM_EPS, ADAM_WD, ADAM_STEP = 0.001, 0.9, 0.999, 1e-08, 0.01, 10

DT0, BETA0, A0, FF0 = 0, 8, 12, 16
IN_SIZES = (512, 1024, 8, 1536, 4, 4, 512, 1024, 1536, 8, 4096)
_OFF = [0]
for _s in IN_SIZES:
    _OFF.append(_OFF[-1] + _s)
D_IN = _OFF[-1]
SEGS = (("z", 0), ("xbc", 1), ("dnqkv", 3), ("dngate", 6), ("sguv", 7), ("foxqkv", 8), ("gates", 10))
SMALL_SRC = ((2, DT0), (4, BETA0), (5, A0), (9, FF0))

NN = ((1,), (0,))
NT = ((1,), (1,))
TN = ((0,), (0,))
_DIMS = {"nn": NN, "nt": NT, "tn": TN}


def _pcall(body, **kw):
    return pl.pallas_call(body, **kw)


def _S(shape, dtype=F32):
    return jax.ShapeDtypeStruct(tuple(shape), dtype)


def _iota(shape, dim):
    return lax.broadcasted_iota(jnp.int32, shape, dim)


def _dotb(a, b, dims):
    return lax.dot_general(a, b, (dims, ((), ())), preferred_element_type=F32)


def _split2(a):
    ah = a.astype(BF)
    return ah, (a - ah.astype(F32)).astype(BF)


def _split3(a):
    a1 = a.astype(BF)
    r = a - a1.astype(F32)
    a2 = r.astype(BF)
    a3 = (r - a2.astype(F32)).astype(BF)
    return a1, a2, a3


def _mm_raw(a, b, form, mode):
    d = _DIMS[form]
    if mode == "1":
        return _dotb(a.astype(BF), b.astype(BF), d)
    if mode == "3":
        ah, al = _split2(a)
        bh, bl = _split2(b)
        return _dotb(ah, bh, d) + (_dotb(ah, bl, d) + _dotb(al, bh, d))
    if mode == "xa":
        ab = a.astype(BF)
        b1, b2, b3 = _split3(b)
        return _dotb(ab, b1, d) + (_dotb(ab, b2, d) + _dotb(ab, b3, d))
    bb = b.astype(BF)
    a1, a2, a3 = _split3(a)
    return _dotb(a1, bb, d) + (_dotb(a2, bb, d) + _dotb(a3, bb, d))


@functools.partial(jax.custom_vjp, nondiff_argnums=(2, 3))
def mm(a, b, form, mode):
    return _mm_raw(a, b, form, mode)


def _mm_fwd(a, b, form, mode):
    return _mm_raw(a, b, form, mode), (a, b)


_XA_DB = {"nn": "xa", "nt": "xb", "tn": "xa"}
_XB_DA = {"nn": "xb", "nt": "xb", "tn": "xa"}


def _mm_bwd(form, mode, res, g):
    a, b = res
    ma = _XB_DA[form] if mode == "xb" else mode
    mb = _XA_DB[form] if mode == "xa" else mode
    da = db = None
    if mode != "xa":
        da = {"nn": lambda: mm(g, b, "nt", ma), "nt": lambda: mm(g, b, "nn", ma), "tn": lambda: mm(b, g, "nt", ma)}[form]()
    if mode != "xb":
        db = {"nn": lambda: mm(a, g, "tn", mb), "nt": lambda: mm(g, a, "tn", mb), "tn": lambda: mm(a, g, "nn", mb)}[form]()
    if da is None:
        da = jnp.zeros_like(a)
    if db is None:
        db = jnp.zeros_like(b)
    return da, db


mm.defvjp(_mm_fwd, _mm_bwd)


def _silu(x):
    return x * jax.nn.sigmoid(x)


def _ln(x, g, b):
    mu = jnp.mean(x, -1, keepdims=True)
    xc = x - mu
    var = jnp.mean(xc * xc, -1, keepdims=True)
    return xc * lax.rsqrt(var + LN_EPS) * g + b


def _pick(n, cap):
    if n <= cap:
        return n
    best = LANES
    for t in range(LANES, cap + 1, LANES):
        if n % t == 0:
            best = t
    return best


def transpose_bf16(a, name):
    T, C = a.shape
    tt = min(T, 512)

    def body(a_ref, o_ref):
        o_ref[...] = a_ref[...].T.astype(BF)

    return _pcall(body, grid=(T // tt,), in_specs=[pl.BlockSpec((tt, C), lambda t: (t, 0))], out_specs=pl.BlockSpec((C, tt), lambda t: (0, t)),
                  out_shape=_S((C, T), BF), name=name)(a)


def matmul_w(a, w, l, form, name, add=None, add_scale=1.0, out_dtype=F32):
    M, K = a.shape
    N = w.shape[2] if form == "nn" else w.shape[1]
    tm, tn, tk = min(M, 512), _pick(N, 1024), _pick(K, 1024)
    nk = K // tk

    def body(*refs):
        if add is None:
            a_ref, b_ref, o_ref, acc = refs
        else:
            a_ref, b_ref, d_ref, o_ref, acc = refs
        k = pl.program_id(2)
        p = _dotb(a_ref[...].astype(BF), b_ref[...].astype(BF), _DIMS[form])

        @pl.when(k == 0)
        def _():
            acc[...] = p

        @pl.when(k > 0)
        def _():
            acc[...] += p

        @pl.when(k == nk - 1)
        def _():
            r = acc[...]
            if add is not None:
                r = r + add_scale * d_ref[...]
            o_ref[...] = r.astype(out_dtype)

    if form == "nn":
        wspec = pl.BlockSpec((None, tk, tn), lambda j, i, k: (l, k, j))
    else:
        wspec = pl.BlockSpec((None, tn, tk), lambda j, i, k: (l, j, k))
    in_specs = [pl.BlockSpec((tm, tk), lambda j, i, k: (i, k)), wspec]
    args = [a, w]
    if add is not None:
        in_specs.append(pl.BlockSpec((tm, tn), lambda j, i, k: (i, j)))
        args.append(add)
    return _pcall(body, grid=(N // tn, M // tm, nk), in_specs=in_specs,
                  out_specs=pl.BlockSpec((tm, tn), lambda j, i, k: (i, j)), out_shape=_S((M, N), out_dtype),
                  scratch_shapes=[pltpu.VMEM((tm, tn), F32)], name=name)(*args)


def matmul_tn(a, b, name, b_col0=0, n_cols=None, out_dtype=F32):
    T, M = a.shape
    N = b.shape[1] if n_cols is None else n_cols
    tm, tn, tt = _pick(M, 512), _pick(N, 1024), min(T, 512)
    nt = T // tt
    jb = b_col0 // tn

    def body(a_ref, b_ref, o_ref, acc):
        t = pl.program_id(2)
        p = _dotb(a_ref[...].astype(BF), b_ref[...].astype(BF), TN)

        @pl.when(t == 0)
        def _():
            acc[...] = p

        @pl.when(t > 0)
        def _():
            acc[...] += p

        @pl.when(t == nt - 1)
        def _():
            o_ref[...] = acc[...].astype(out_dtype)

    return _pcall(body, grid=(M // tm, N // tn, nt),
                  in_specs=[pl.BlockSpec((tt, tm), lambda i, j, t: (t, i)), pl.BlockSpec((tt, tn), lambda i, j, t: (t, jb + j))],
                  out_specs=pl.BlockSpec((tm, tn), lambda i, j, t: (i, j)), out_shape=_S((M, N), out_dtype),
                  scratch_shapes=[pltpu.VMEM((tm, tn), F32)], name=name)(a, b)


def _pieces(v):
    if v.ndim == 3:
        return [v[i] for i in range(v.shape[0])]
    n = v.shape[1] // LANES
    if n <= 1:
        return [v]
    return [v[:, i * LANES:(i + 1) * LANES] for i in range(n)]


def _join(ps, like_ndim):
    if like_ndim == 3:
        return jnp.stack(ps, axis=0)
    return ps[0] if len(ps) == 1 else jnp.concatenate(ps, axis=1)


def scan_fwd(name, f, xs, ps, ys, state_shape, nc, nh=1, rider=None):
    nx, npar, ny = len(xs), len(ps), len(ys)

    def body(*refs):
        x_refs, p_refs = refs[:nx], refs[nx:nx + npar]
        y_refs = refs[nx + npar:nx + npar + ny]
        st_out, st = refs[nx + npar + ny], refs[nx + npar + ny + 1]
        c, h = pl.program_id(0), pl.program_id(1)

        @pl.when(c == 0)
        def _():
            st[h] = jnp.zeros(state_shape, F32)

        S = st[h]
        st_out[...] = S
        yv, Sn = f([_pieces(r[...]) for r in x_refs], [_pieces(r[...]) for r in p_refs], _pieces(S), h)
        for r, v in zip(y_refs, yv):
            r[...] = _join(v, 2)
        st[h] = _join(Sn, 3)

    in_specs = [pl.BlockSpec(bs, im) for (_, bs, im) in xs]
    in_specs += [pl.BlockSpec(p.shape, (lambda c, h, n=p.ndim: (0,) * n)) for p in ps]
    out_specs = [pl.BlockSpec(bs, im) for (_, bs, im) in ys]
    out_specs.append(pl.BlockSpec((None, None) + tuple(state_shape), lambda c, h: (c, h, 0, 0, 0)))
    out_shape = [_S(s) for (s, _, _) in ys] + [_S((nc, nh) + tuple(state_shape))]
    return hosted_call(body, rider, grid=(nc, nh), in_specs=in_specs, out_specs=out_specs, out_shape=out_shape,
                       scratch_shapes=[pltpu.VMEM((nh,) + tuple(state_shape), F32)], name=name, args=[*[x[0] for x in xs], *ps])


def scan_bwd(name, f, xs, ps, ys, dys, states, state_shape, nc, nh=1, shared=(), rider=None):
    nx, npar, ny = len(xs), len(ps), len(ys)

    def body(*refs):
        x_refs, p_refs = refs[:nx], refs[nx:nx + npar]
        s_ref = refs[nx + npar]
        dy_refs = refs[nx + npar + 1:nx + npar + 1 + ny]
        o = nx + npar + 1 + ny
        dx_refs, dp_refs, dst = refs[o:o + nx], refs[o + nx:o + nx + npar], refs[o + nx + npar]
        c, h = pl.program_id(0), pl.program_id(1)

        @pl.when(c == 0)
        def _():
            dst[h] = jnp.zeros(state_shape, F32)

        @pl.when((c == 0) & (h == 0))
        def _():
            for r in dp_refs:
                r[...] = jnp.zeros(r.shape, F32)

        xv = [_pieces(r[...]) for r in x_refs]
        pv = [_pieces(r[...]) for r in p_refs]
        _, vjp = jax.vjp(lambda a, b, s: f(a, b, s, h), xv, pv, _pieces(s_ref[...]))
        dxv, dpv, dS = vjp(([_pieces(r[...]) for r in dy_refs], _pieces(dst[h])))
        for i, (r, v) in enumerate(zip(dx_refs, dxv)):
            if i in shared and nh > 1:
                @pl.when(h == 0)
                def _(r=r, v=v):
                    r[...] = _join(v, 2)

                @pl.when(h > 0)
                def _(r=r, v=v):
                    r[...] += _join(v, 2)
            else:
                r[...] = _join(v, 2)
        for r, v in zip(dp_refs, dpv):
            r[...] += _join(v, len(r.shape))
        dst[h] = _join(dS, 3)

    def rev(im):
        return lambda c, h: im(nc - 1 - c, h)

    in_specs = [pl.BlockSpec(bs, rev(im)) for (_, bs, im) in xs]
    in_specs += [pl.BlockSpec(p.shape, (lambda c, h, n=p.ndim: (0,) * n)) for p in ps]
    in_specs.append(pl.BlockSpec((None, None) + tuple(state_shape), lambda c, h: (nc - 1 - c, h, 0, 0, 0)))
    in_specs += [pl.BlockSpec(bs, rev(im)) for (_, bs, im) in ys]
    out_specs = [pl.BlockSpec(bs, rev(im)) for (_, bs, im) in xs]
    out_specs += [pl.BlockSpec(p.shape, (lambda c, h, n=p.ndim: (0,) * n)) for p in ps]
    out_shape = [_S(x[0].shape) for x in xs] + [_S(p.shape) for p in ps]
    return hosted_call(body, rider, grid=(nc, nh), in_specs=in_specs, out_specs=out_specs, out_shape=out_shape,
                       scratch_shapes=[pltpu.VMEM((nh,) + tuple(state_shape), F32)], name=name,
                       args=[*[x[0] for x in xs], *ps, states, *dys])


def _lane():
    return _iota((1, LANES), 1)


def _col(v, idx):
    return jnp.sum(v * (_lane() == idx).astype(F32), axis=1, keepdims=True)


def _last_row(v):
    r = v.shape[0]
    return jnp.sum(v * (_iota((r, 1), 0) == r - 1).astype(F32), axis=0, keepdims=True)


def _tril(n, strict=False):
    r, c = _iota((n, n), 0), _iota((n, n), 1)
    return (r > c) if strict else (r >= c)


def ssd_chunk(xs, ps, S, h):
    zp, xbc, (sm,) = xs
    (bias,), (alog,), (dsk,), nw = ps
    Q = sm.shape[0]
    H = range(8)
    lane = _lane()
    a128 = jnp.where(lane < 8, -jnp.exp(alog), 0.0)
    dtl = jax.nn.softplus(sm + bias)
    tri = _tril(Q)
    cum = mm(tri.astype(F32), dtl * a128, "nn", "xa")
    sel8 = (_iota((8, LANES), 0) == _iota((8, LANES), 1)).astype(F32)
    cum_t = mm(sel8, cum, "nt", "xa")
    m0 = (lane < 64).astype(F32)
    rows0 = (_iota((LANES, 1), 0) < 64).astype(F32)
    me = [m0 if hh % 2 == 0 else 1.0 - m0 for hh in H]
    re = [rows0 if hh % 2 == 0 else 1.0 - rows0 for hh in H]
    Bm, Cm = [xbc[4 + hh // 4] for hh in H], [xbc[6 + hh // 4] for hh in H]
    cb = [mm(xbc[6 + g], xbc[4 + g], "nt", "1") for g in range(2)]
    col = [_col(cum, hh) for hh in H]
    row = [jnp.sum(cum_t * (_iota((8, 1), 0) == hh).astype(F32), axis=0, keepdims=True) for hh in H]
    xh = [xbc[hh // 2] * me[hh] for hh in H]
    xdt = [xh[hh] * _col(dtl, hh) for hh in H]
    seg = [jnp.exp(jnp.where(tri, col[hh] - row[hh], -jnp.inf)) for hh in H]
    last = [_last_row(col[hh]) for hh in H]
    y_diag = [mm(cb[hh // 4] * seg[hh], xdt[hh], "nn", "1") for hh in H]
    y_off = [mm(Cm[hh] * jnp.exp(col[hh]), S[hh // 2], "nt", "1") * me[hh] for hh in H]
    st = [mm(xdt[hh], Bm[hh] * jnp.exp(last[hh] - col[hh]), "tn", "1") for hh in H]
    y = [y_diag[hh] + y_off[hh] + _col(dsk, hh) * xh[hh] for hh in H]
    Sn = [S[pr] * (jnp.exp(last[2 * pr]) * re[0] + jnp.exp(last[2 * pr + 1]) * re[1]) + st[2 * pr] + st[2 * pr + 1] for pr in range(4)]
    yz = [(y[2 * pr] + y[2 * pr + 1]) * _silu(zp[pr]) for pr in range(4)]
    ssq = sum(jnp.sum(v * v, axis=1, keepdims=True) for v in yz)
    scale = lax.rsqrt(ssq / BRANCH_W + NORM_EPS)
    return [[yz[i] * scale * nw[i] for i in range(4)]], Sn


def dn_chunk(xs, ps, S, h):
    act, gate, (sm,) = xs
    (alog,), (dtb,), (nw,) = ps
    H = range(4)
    C = sm.shape[0]
    lane = _lane()
    G = jnp.where((lane >= A0) & (lane < A0 + 4), -jnp.exp(alog) * jax.nn.softplus(sm + dtb), 0.0)
    tri, strict = _tril(C), _tril(C, True)
    gcs = mm(tri.astype(F32), G, "nn", "xa")
    sig = jax.nn.sigmoid(sm)
    qn = [act[h] * lax.rsqrt(jnp.sum(act[h] * act[h], axis=1, keepdims=True) + NORM_EPS) * (LANES ** -0.5) for h in H]
    kn = [act[4 + h] * lax.rsqrt(jnp.sum(act[4 + h] * act[4 + h], axis=1, keepdims=True) + NORM_EPS) for h in H]
    beta = [_col(sig, BETA0 + h) for h in H]
    gcol = [_col(gcs, A0 + h) for h in H]
    selr = [((_iota((8, LANES), 0) == 0) & (_iota((8, LANES), 1) == A0 + h)).astype(F32) for h in H]
    grow = [jnp.sum(mm(selr[h], gcs, "nt", "xa"), axis=0, keepdims=True) for h in H]
    gamma = [jnp.exp(jnp.where(tri, gcol[h] - grow[h], -jnp.inf)) for h in H]
    kb = [kn[h] * beta[h] for h in H]
    pk = [-(mm(kb[h], kn[h], "nt", "1") * jnp.where(strict, gamma[h], 0.0)) for h in H]
    eye = (_iota((C, C), 0) == _iota((C, C), 1)).astype(F32)
    minv = [eye + pk[h] for h in H]
    for _ in range(5):
        pk = [mm(pk[h], pk[h], "nn", "3") for h in H]
        minv = [minv[h] + mm(minv[h], pk[h], "nn", "3") for h in H]
    eg = [jnp.exp(gcol[h]) for h in H]
    w = [mm(minv[h], kb[h] * eg[h], "nn", "3") for h in H]
    u = [mm(minv[h], act[8 + h] * beta[h], "nn", "3") for h in H]
    glast = [_last_row(gcol[h]) for h in H]
    vnew = [u[h] - mm(w[h], S[h], "nn", "1") for h in H]
    qk = [mm(qn[h], kn[h], "nt", "1") * gamma[h] for h in H]
    o = [mm(qn[h] * eg[h], S[h], "nn", "1") + mm(qk[h], vnew[h], "nn", "1") for h in H]
    Sn = [S[h] * jnp.exp(glast[h]) + mm(kn[h] * jnp.exp(glast[h] - gcol[h]), vnew[h], "tn", "1") for h in H]
    on = [o[h] * lax.rsqrt(jnp.mean(o[h] * o[h], axis=1, keepdims=True) + NORM_EPS) * nw for h in H]
    return [[on[h] * _silu(gate[h]) for h in H]], Sn


def sg_chunk(xs, ps, S, h):
    (uv,) = xs
    lng, lnb, W, (bt,) = ps
    u = [jax.nn.gelu(p) for p in uv[:4]]
    v = [jax.nn.gelu(p) for p in uv[4:]]
    mu = sum(jnp.sum(p, axis=1, keepdims=True) for p in v) / BRANCH_W
    vc = [p - mu for p in v]
    var = sum(jnp.sum(p * p, axis=1, keepdims=True) for p in vc) / BRANCH_W
    inv = lax.rsqrt(var + LN_EPS)
    trif = _tril(W[0].shape[0]).astype(F32)
    out = []
    for g in range(4):
        vn = vc[g] * inv * lng[g] + lnb[g]
        out.append(u[g] * (mm(W[g] * trif, vn, "nn", "1") + _col(bt, g)))
    return [out], S


def foxc_chunk(xs, ps, S, h):
    (sm,), ((fb,),), (carry,) = xs[0], ps, S
    lane = _lane()
    ls = jnp.where((lane >= FF0) & (lane < FF0 + 8), jax.nn.log_sigmoid(sm + fb), 0.0)
    c = mm(_tril(sm.shape[0]).astype(F32), ls, "nn", "xa") + carry
    return [[c]], [_last_row(c)]


HALO = 8


def _conv_tiles(T, C):
    return min(T, 512), _pick(C, 512)


def conv_fwd(x, w, b, name):
    T, C = x.shape
    tm, cb = _conv_tiles(T, C)

    def body(xp_ref, x_ref, w_ref, b_ref, o_ref):
        i = pl.program_id(1)
        e = jnp.concatenate([xp_ref[...] * (i > 0).astype(F32), x_ref[...]], axis=0)
        pre = b_ref[...] + sum(w_ref[k:k + 1, :] * e[5 + k:5 + k + tm, :] for k in range(4))
        o_ref[...] = _silu(pre)

    hb = tm // HALO
    return _pcall(body, grid=(C // cb, T // tm),
                  in_specs=[pl.BlockSpec((HALO, cb), lambda j, i: (jnp.maximum(i * hb - 1, 0), j)), pl.BlockSpec((tm, cb), lambda j, i: (i, j)),
                            pl.BlockSpec((4, cb), lambda j, i: (0, j)), pl.BlockSpec((1, cb), lambda j, i: (0, j))],
                  out_specs=pl.BlockSpec((tm, cb), lambda j, i: (i, j)), out_shape=_S((T, C)), name=name)(x, x, w, b)


def conv_bwd(x, w, b, dact, name):
    T, C = x.shape
    tm, cb = _conv_tiles(T, C)
    nt = T // tm

    def body(xp_ref, x_ref, xn_ref, w_ref, b_ref, d_ref, dn_ref, dx_ref, dw_ref, db_ref):
        i = pl.program_id(1)
        has_prev, has_next = (i > 0).astype(F32), (i < nt - 1).astype(F32)
        e = jnp.concatenate([xp_ref[...] * has_prev, x_ref[...], xn_ref[...] * has_next], axis=0)
        pre = b_ref[...] + sum(w_ref[k:k + 1, :] * e[5 + k:5 + k + tm + 8, :] for k in range(4))
        de = jnp.concatenate([d_ref[...], dn_ref[...] * has_next], axis=0)
        sg = jax.nn.sigmoid(pre)
        dpre = de * (sg * (1.0 + pre * (1.0 - sg)))
        dx_ref[...] = sum(w_ref[k:k + 1, :] * dpre[3 - k:3 - k + tm, :] for k in range(4))
        dcur = dpre[0:tm, :]
        dw = jnp.concatenate([jnp.sum(dcur * e[5 + k:5 + k + tm, :], axis=0, keepdims=True) for k in range(4)], axis=0)
        db = jnp.sum(dcur, axis=0, keepdims=True)

        @pl.when(i == 0)
        def _():
            dw_ref[...] = dw
            db_ref[...] = db

        @pl.when(i > 0)
        def _():
            dw_ref[...] += dw
            db_ref[...] += db

    blk = lambda f: pl.BlockSpec((tm, cb), f)
    hb = tm // HALO
    before = pl.BlockSpec((HALO, cb), lambda j, i: (jnp.maximum(i * hb - 1, 0), j))
    after = pl.BlockSpec((HALO, cb), lambda j, i: (jnp.minimum((i + 1) * hb, nt * hb - 1), j))
    return _pcall(body, grid=(C // cb, nt),
                  in_specs=[before, blk(lambda j, i: (i, j)), after,
                            pl.BlockSpec((4, cb), lambda j, i: (0, j)), pl.BlockSpec((1, cb), lambda j, i: (0, j)),
                            blk(lambda j, i: (i, j)), after],
                  out_specs=[blk(lambda j, i: (i, j)), pl.BlockSpec((4, cb), lambda j, i: (0, j)), pl.BlockSpec((1, cb), lambda j, i: (0, j))],
                  out_shape=[_S((T, C)), _S((4, C)), _S((1, C))], name=name)(x, x, x, w, b, dact, dact)


FOX_SCALE = 64 ** -0.5
LOG2E = 1.4426950408889634


def _spare(e, i):
    return (_lane() == 64 * (1 - e) + i).astype(F32)


def _lanes_of(e):
    lane = _lane()
    return ((lane < 64) if e == 0 else (lane >= 64)).astype(F32)


def _col3(col, e, first):
    c1 = col.astype(BF).astype(F32)
    c2 = (col - c1).astype(BF).astype(F32)
    c3 = (col - c1 - c2).astype(BF).astype(F32)
    return c1 * _spare(e, first) + c2 * _spare(e, first + 1) + c3 * _spare(e, first + 2)


def _ones3(e, first):
    return _spare(e, first) + _spare(e, first + 1) + _spare(e, first + 2)


def _causal_bias(n):
    return jnp.where(_iota((n, n), 0) >= _iota((n, n), 1), 0.0, -jnp.inf).astype(F32)


def _c_col(cc, hh):
    return jnp.sum(cc * (_lane() == FF0 + hh).astype(F32), axis=1, keepdims=True) * LOG2E


def _pair_spec(tq, row_of):
    return pl.BlockSpec((None, 2, tq, LANES), lambda hp, a, b: (hp, 0, row_of(a, b), 0))


def fox_prep(qkv, ccol, name):
    T = qkv.shape[0]
    tq = min(T, 512)

    def body(q_ref, k_ref, v_ref, cc_ref, qa_ref, ka_ref, va_ref):
        hp = pl.program_id(0)
        q, k, v, cc = q_ref[...], k_ref[...], v_ref[...], cc_ref[...]
        for e in range(2):
            me = _lanes_of(e)
            c2 = _c_col(cc, 2 * hp + e)
            qa_ref[e] = (q * me * (FOX_SCALE * LOG2E) + _col3(c2, e, 0) + _ones3(e, 3)).astype(BF)
            ka_ref[e] = (k * me + _ones3(e, 0) + _col3(-c2, e, 3) + _ones3(e, 6)).astype(BF)
            va_ref[e] = (v * me + (1.0 - me)).astype(BF)

    blk = lambda off: pl.BlockSpec((tq, LANES), lambda hp, i: (i, off + hp))
    out = pl.BlockSpec((None, 2, tq, LANES), lambda hp, i: (hp, 0, i, 0))
    return _pcall(body, grid=(4, T // tq), in_specs=[blk(0), blk(4), blk(8), pl.BlockSpec((tq, LANES), lambda hp, i: (i, 0))],
                  out_specs=[out] * 3, out_shape=[_S((4, 2, T, LANES), BF)] * 3, name=name)(qkv, qkv, qkv, ccol)


def fox_fwd(qa, ka, va, name, rider=None):
    T = qa.shape[2]
    tq = min(T, 512)
    nq = T // tq

    def body(qa_ref, ka_ref, va_ref, o_ref, lse_ref, m_s, acc, causal):
        i, j = pl.program_id(1), pl.program_id(2)

        @pl.when((pl.program_id(0) == 0) & (i == 0) & (j == 0))
        def _():
            causal[...] = _causal_bias(tq)

        @pl.when(j == 0)
        def _():
            m_s[...] = jnp.full(m_s.shape, -jnp.inf, F32)
            acc[...] = jnp.zeros(acc.shape, F32)

        def step(diagonal):
            for e in range(2):
                s = _dotb(qa_ref[e], ka_ref[e], NT)
                if diagonal:
                    s = s + causal[...]
                m_old = m_s[e]
                m_new = jnp.maximum(m_old, jnp.max(s, axis=1, keepdims=True))
                p = jnp.exp2(s - m_new)
                m_s[e] = m_new
                acc[e] = acc[e] * jnp.exp2(m_old - m_new) + _dotb(p.astype(BF), va_ref[e], NN)

        @pl.when(j < i)
        def _():
            step(False)

        @pl.when(j == i)
        def _():
            step(True)
            lane = _lane()
            o, lse = 0.0, 0.0
            for e in range(2):
                l = jnp.sum(acc[e] * _spare(e, 0), axis=1, keepdims=True)
                o = o + acc[e] * _lanes_of(e) / l
                lse = lse + (m_s[e] + jnp.log2(l)) * (lane == e).astype(F32)
            o_ref[...] = o
            lse_ref[...] = lse

    kv = _pair_spec(tq, lambda i, j: jnp.minimum(j, i))
    return hosted_call(body, rider, grid=(4, nq, nq), in_specs=[_pair_spec(tq, lambda i, j: i), kv, kv],
                       out_specs=[pl.BlockSpec((tq, LANES), lambda hp, i, j: (i, hp)), pl.BlockSpec((None, tq, LANES), lambda hp, i, j: (hp, i, 0))],
                       out_shape=[_S((T, BRANCH_W)), _S((4, T, LANES))],
                       scratch_shapes=[pltpu.VMEM((2, tq, 1), F32), pltpu.VMEM((2, tq, LANES), F32), pltpu.VMEM((tq, tq), F32)],
                       name=name, args=[qa, ka, va])


def fox_prep_bwd(qkv, qa, o, lse, do, name):
    T = qkv.shape[0]
    tq = min(T, 512)

    def body(q_ref, k_ref, qa_ref, o_ref, lse_ref, do_ref, qb_ref, doa_ref, qd_ref, kd_ref):
        q, k, dov = q_ref[...], k_ref[...], do_ref[...]
        dd = dov * o_ref[...]
        lane = _lane()
        for e in range(2):
            me = _lanes_of(e)
            lse_e = jnp.sum(lse_ref[...] * (lane == e).astype(F32), axis=1, keepdims=True)
            qb_ref[e] = (qa_ref[e].astype(F32) + _col3(-lse_e, e, 6)).astype(BF)
            doa_ref[e] = (dov * me + _col3(-jnp.sum(dd * me, axis=1, keepdims=True), e, 0)).astype(BF)
            qd_ref[e] = (q * me * FOX_SCALE + _spare(e, 0)).astype(BF)
            kd_ref[e] = (k * me * FOX_SCALE + _spare(e, 0)).astype(BF)

    blk = lambda off: pl.BlockSpec((tq, LANES), lambda hp, i: (i, off + hp))
    pair = pl.BlockSpec((None, 2, tq, LANES), lambda hp, i: (hp, 0, i, 0))
    return _pcall(body, grid=(4, T // tq),
                  in_specs=[blk(0), blk(4), pair, blk(0), pl.BlockSpec((None, tq, LANES), lambda hp, i: (hp, i, 0)), blk(0)],
                  out_specs=[pair] * 4, out_shape=[_S((4, 2, T, LANES), BF)] * 4, name=name)(qkv, qkv, qa, o, lse, do)


def fox_bwd(qb, ka, va, doa, qd, kd, name, rider=None):
    T = qb.shape[2]
    tq = min(T, 512)
    nq = T // tq

    def body(qb_ref, ka_ref, va_ref, doa_ref, qd_ref, kd_ref, dq_ref, dk_ref, dv_ref, dcc_ref, dq_s, dk_s, dv_s, causal):
        hp, j, ii = pl.program_id(0), pl.program_id(1), pl.program_id(2)
        i = jnp.maximum(ii, j)

        @pl.when((hp == 0) & (j == 0) & (ii == 0))
        def _():
            dcc_ref[...] = jnp.zeros(dcc_ref.shape, F32)
            causal[...] = _causal_bias(tq)

        @pl.when((j == 0) & (ii == 0))
        def _():
            dq_s[...] = jnp.zeros(dq_s.shape, F32)

        @pl.when(ii == 0)
        def _():
            dk_s[...] = jnp.zeros(dk_s.shape, F32)
            dv_s[...] = jnp.zeros(dv_s.shape, F32)

        def step(diagonal):
            rows = pl.ds(pl.multiple_of(i * tq, tq), tq)
            for e in range(2):
                s = _dotb(qb_ref[e], ka_ref[e], NT)
                if diagonal:
                    s = s + causal[...]
                p = jnp.exp2(s)
                ds = (p * _dotb(doa_ref[e], va_ref[e], NT)).astype(BF)
                dv_s[e] += _dotb(p.astype(BF), doa_ref[e], TN)
                dq_s[e, rows, :] += _dotb(ds, kd_ref[e], NN)
                dk_s[e] += _dotb(ds, qd_ref[e], TN)

        @pl.when(ii > j)
        def _():
            step(False)

        @pl.when(ii == j)
        def _():
            step(True)

        def fold(acc, sign):
            grad, dc = 0.0, 0.0
            for e in range(2):
                a = acc[e]
                grad = grad + a * _lanes_of(e)
                dc = dc + sign * jnp.sum(a * _spare(e, 0), axis=1, keepdims=True) * (_lane() == FF0 + 2 * hp + e).astype(F32)
            return grad, dc

        @pl.when(ii == nq - 1)
        def _():
            grad, dc = fold(dk_s, -1.0)
            dk_ref[...] = grad
            dv_ref[...] = dv_s[0] * _lanes_of(0) + dv_s[1] * _lanes_of(1)
            dcc_ref[pl.ds(pl.multiple_of(j * tq, tq), tq), :] += dc

        @pl.when((j == nq - 1) & (ii == nq - 1))
        def _():
            grad, dc = fold(dq_s, 1.0)
            dq_ref[...] = grad
            dcc_ref[...] += dc

    irow, jrow = _pair_spec(tq, lambda j, ii: jnp.maximum(ii, j)), _pair_spec(tq, lambda j, ii: j)
    jout = pl.BlockSpec((tq, LANES), lambda hp, j, ii: (j, hp))
    return hosted_call(body, rider, grid=(4, nq, nq), in_specs=[irow, jrow, jrow, irow, irow, jrow],
                       out_specs=[pl.BlockSpec((T, LANES), lambda hp, j, ii: (0, hp)), jout, jout, pl.BlockSpec((T, LANES), lambda hp, j, ii: (0, 0))],
                       out_shape=[_S((T, BRANCH_W)), _S((T, BRANCH_W)), _S((T, BRANCH_W)), _S((T, LANES))],
                       scratch_shapes=[pltpu.VMEM((2, T, LANES), F32), pltpu.VMEM((2, tq, LANES), F32), pltpu.VMEM((2, tq, LANES), F32),
                                       pltpu.VMEM((tq, tq), F32)],
                       name=name, args=[qb, ka, va, doa, qd, kd])


def _acc_out(ref, val, first):
    @pl.when(first)
    def _():
        ref[...] = val

    @pl.when(jnp.logical_not(first))
    def _():
        ref[...] += val


def _row(tm, c):
    return pl.BlockSpec((tm, c), lambda i: (i, 0))


def _full(shape):
    return pl.BlockSpec(shape, lambda *_: (0,) * len(shape))


def ln_fwd(x, g, b, name):
    T, C = x.shape
    tm = min(T, 512)

    def body(x_ref, g_ref, b_ref, o_ref):
        o_ref[...] = _ln(x_ref[...], g_ref[...], b_ref[...])

    return _pcall(body, grid=(T // tm,), in_specs=[_row(tm, C), _full((1, C)), _full((1, C))], out_specs=_row(tm, C),
                  out_shape=_S((T, C)), name=name)(x, g, b)


def ln_bwd(x, g, b, dy, name):
    T, C = x.shape
    tm = min(T, 512)

    def body(x_ref, g_ref, b_ref, dy_ref, dx_ref, dg_ref, db_ref):
        _, vjp = jax.vjp(_ln, x_ref[...], g_ref[...], b_ref[...])
        dx, dg, db = vjp(dy_ref[...])
        dx_ref[...] = dx
        first = pl.program_id(0) == 0
        _acc_out(dg_ref, dg, first)
        _acc_out(db_ref, db, first)

    return _pcall(body, grid=(T // tm,), in_specs=[_row(tm, C), _full((1, C)), _full((1, C)), _row(tm, C)],
                  out_specs=[_row(tm, C), _full((1, C)), _full((1, C))], out_shape=[_S((T, C)), _S((1, C)), _S((1, C))], name=name)(x, g, b, dy)


def loss_head(h, target, name):
    T, C = h.shape
    tm = min(T, 512)

    def body(h_ref, t_ref, d_ref, l_ref):
        e = h_ref[...] - t_ref[...]
        d_ref[...] = e * (1.0 / C)
        part = jnp.sum(jnp.sum(e * e, axis=1, keepdims=True), axis=0, keepdims=True) * (0.5 / C)
        _acc_out(l_ref, part, pl.program_id(0) == 0)

    return _pcall(body, grid=(T // tm,), in_specs=[_row(tm, C), _row(tm, C)], out_specs=[_row(tm, C), _full((1, 1))],
                  out_shape=[_S((T, C)), _S((1, 1))], name=name)(h, target)


def add3(a, b, c, name):
    T, C = a.shape
    tm = min(T, 512)

    def body(a_ref, b_ref, c_ref, o_ref):
        o_ref[...] = a_ref[...] + b_ref[...] + c_ref[...]

    return _pcall(body, grid=(T // tm,), in_specs=[_row(tm, C)] * 3, out_specs=_row(tm, C), out_shape=_S((T, C)), name=name)(a, b, c)


def _wb_spec(l):
    return pl.BlockSpec((None, 4, BRANCH_W, D_MODEL), lambda *_: (l, 0, 0, 0))


def merge_fwd(ys, gl, gb, wb, l, name):
    T = gl.shape[0]
    tm = min(T, 256)

    def body(y0, y1, y2, y3, gl_ref, gb_ref, wb_ref, o_ref):
        acc = 0.0
        for i, y in enumerate((y0, y1, y2, y3)):
            z = _dotb(y[...].astype(BF), wb_ref[i], NN)
            g = jax.nn.sigmoid(gl_ref[:, i * D_MODEL:(i + 1) * D_MODEL] + gb_ref[i:i + 1, :])
            acc = acc + g * z
        o_ref[...] = acc

    return _pcall(body, grid=(T // tm,), in_specs=[_row(tm, BRANCH_W)] * 4 + [_row(tm, 4 * D_MODEL), _full((4, D_MODEL)), _wb_spec(l)],
                  out_specs=_row(tm, D_MODEL), out_shape=_S((T, D_MODEL)), name=name)(*ys, gl, gb, wb)


def merge_bwd(ys, gl, gb, wb, l, dm, name):
    T = gl.shape[0]
    tm = min(T, 256)

    def body(y0, y1, y2, y3, gl_ref, gb_ref, wb_ref, dm_ref, d0, d1, d2, d3, dgl_ref, dz_ref, dgb_ref):
        dmv = dm_ref[...]
        first = pl.program_id(0) == 0
        for i, (y, d) in enumerate(zip((y0, y1, y2, y3), (d0, d1, d2, d3))):
            cols = slice(i * D_MODEL, (i + 1) * D_MODEL)
            z = _dotb(y[...].astype(BF), wb_ref[i], NN)
            g = jax.nn.sigmoid(gl_ref[:, cols] + gb_ref[i:i + 1, :])
            dgl = dmv * z * (g * (1.0 - g))
            dz = (g * dmv).astype(BF)
            dgl_ref[:, cols] = dgl
            dz_ref[:, cols] = dz
            d[...] = _dotb(dz, wb_ref[i], NT)
            _acc_out(dgb_ref.at[i:i + 1, :], jnp.sum(dgl, axis=0, keepdims=True), first)

    return _pcall(body, grid=(T // tm,),
                  in_specs=[_row(tm, BRANCH_W)] * 4 + [_row(tm, 4 * D_MODEL), _full((4, D_MODEL)), _wb_spec(l), _row(tm, D_MODEL)],
                  out_specs=[_row(tm, BRANCH_W)] * 4 + [_row(tm, 4 * D_MODEL), _row(tm, 4 * D_MODEL), _full((4, D_MODEL))],
                  out_shape=[_S((T, BRANCH_W))] * 4 + [_S((T, 4 * D_MODEL)), _S((T, 4 * D_MODEL), BF), _S((4, D_MODEL))], name=name)(
                      *ys, gl, gb, wb, dm)


def _wout_spec(l):
    return pl.BlockSpec((None, D_MODEL, D_MODEL), lambda *_: (l, 0, 0))


def out_fwd(merged, h, wout, l, g, b, name):
    T = h.shape[0]
    tm = min(T, 512)

    def body(m_ref, h_ref, w_ref, g_ref, b_ref, u_ref, o_ref):
        u = ALPHA * h_ref[...] + _dotb(m_ref[...].astype(BF), w_ref[...], NN)
        u_ref[...] = u
        o_ref[...] = _ln(u, g_ref[...], b_ref[...])

    C = D_MODEL
    return _pcall(body, grid=(T // tm,), in_specs=[_row(tm, C), _row(tm, C), _wout_spec(l), _full((1, C)), _full((1, C))],
                  out_specs=[_row(tm, C), _row(tm, C)], out_shape=[_S((T, C)), _S((T, C))], name=name)(merged, h, wout, g, b)


def out_bwd(u, dy, g, b, wout, l, name):
    T, C = u.shape
    tm = min(T, 512)

    def body(u_ref, dy_ref, g_ref, b_ref, w_ref, du_ref, dm_ref, dg_ref, db_ref):
        _, vjp = jax.vjp(_ln, u_ref[...], g_ref[...], b_ref[...])
        du, dg, db = vjp(dy_ref[...])
        du_ref[...] = du
        dm_ref[...] = _dotb(du.astype(BF), w_ref[...], NT)
        first = pl.program_id(0) == 0
        _acc_out(dg_ref, dg, first)
        _acc_out(db_ref, db, first)

    return _pcall(body, grid=(T // tm,), in_specs=[_row(tm, C), _row(tm, C), _full((1, C)), _full((1, C)), _wout_spec(l)],
                  out_specs=[_row(tm, C), _row(tm, C), _full((1, C)), _full((1, C))],
                  out_shape=[_S((T, C)), _S((T, C)), _S((1, C)), _S((1, C))], name=name)(u, dy, g, b, wout)


def ff_fwd(h, wup, wdown, l, g, b, name):
    T, C = h.shape
    F = wup.shape[2]
    tm, tf = min(T, 512), 1024
    nf = F // tf

    def body(h_ref, wu_ref, wd_ref, g_ref, b_ref, u_ref, o_ref, acc):
        f = pl.program_id(1)
        a = _dotb(h_ref[...].astype(BF), wu_ref[...], NN)
        r = jnp.square(jnp.maximum(a, 0.0))
        p = _dotb(r.astype(BF), wd_ref[...], NN)
        _acc_out(acc, p, f == 0)

        @pl.when(f == nf - 1)
        def _():
            u = ALPHA * h_ref[...] + acc[...]
            u_ref[...] = u
            o_ref[...] = _ln(u, g_ref[...], b_ref[...])

    row = pl.BlockSpec((tm, C), lambda i, f: (i, 0))
    return _pcall(body, grid=(T // tm, nf),
                  in_specs=[row, pl.BlockSpec((None, C, tf), lambda i, f: (l, 0, f)), pl.BlockSpec((None, tf, C), lambda i, f: (l, f, 0)),
                            _full((1, C)), _full((1, C))],
                  out_specs=[row, row], out_shape=[_S((T, C)), _S((T, C))], scratch_shapes=[pltpu.VMEM((tm, C), F32)], name=name)(h, wup, wdown, g, b)


def ff_bwd(u, dy, h, g, b, wup, wdown, l, name):
    T, C = h.shape
    F = wup.shape[2]
    tm, tf = min(T, 512), 1024
    nf = F // tf

    def body(u_ref, dy_ref, h_ref, g_ref, b_ref, wu_ref, wd_ref, du_ref, dh_ref, da_ref, r_ref, dg_ref, db_ref, du_s, acc):
        i, f = pl.program_id(0), pl.program_id(1)

        @pl.when(f == 0)
        def _():
            _, vjp = jax.vjp(_ln, u_ref[...], g_ref[...], b_ref[...])
            du, dg, db = vjp(dy_ref[...])
            du_s[...] = du
            du_ref[...] = du
            _acc_out(dg_ref, dg, i == 0)
            _acc_out(db_ref, db, i == 0)

        a = _dotb(h_ref[...].astype(BF), wu_ref[...], NN)
        ap = jnp.maximum(a, 0.0)
        dr = _dotb(du_s[...].astype(BF), wd_ref[...], NT)
        da = (dr * (2.0 * ap)).astype(BF)
        da_ref[...] = da
        r_ref[...] = jnp.square(ap).T.astype(BF)
        _acc_out(acc, _dotb(da, wu_ref[...], NT), f == 0)

        @pl.when(f == nf - 1)
        def _():
            dh_ref[...] = ALPHA * du_s[...] + acc[...]

    row = pl.BlockSpec((tm, C), lambda i, f: (i, 0))
    colf = pl.BlockSpec((tm, tf), lambda i, f: (i, f))
    return _pcall(body, grid=(T // tm, nf),
                  in_specs=[row, row, row, _full((1, C)), _full((1, C)), pl.BlockSpec((None, C, tf), lambda i, f: (l, 0, f)),
                            pl.BlockSpec((None, tf, C), lambda i, f: (l, f, 0))],
                  out_specs=[row, row, colf, pl.BlockSpec((tf, tm), lambda i, f: (f, i)), _full((1, C)), _full((1, C))],
                  out_shape=[_S((T, C)), _S((T, C)), _S((T, F), BF), _S((F, T), BF), _S((1, C)), _S((1, C))],
                  scratch_shapes=[pltpu.VMEM((tm, C), F32), pltpu.VMEM((tm, C), F32)], name=name)(u, dy, h, g, b, wup, wdown)


MESH_ID = pl.DeviceIdType.MESH
_ANY = pl.BlockSpec(memory_space=pl.ANY)


def _window(ref, ax, idx, n):
    if n < 0:
        return ref
    sel = idx if n == 0 else pl.ds(pl.multiple_of(idx * n, n), n)
    return ref.at[(slice(None),) * ax + (sel,)]


Rider = collections.namedtuple("Rider", "operands out_shape scratch start wait")


def hosted_call(body, rider, *, grid, in_specs, out_specs, out_shape, scratch_shapes, name, args):
    n_in, n_out, n_scr = len(in_specs), len(out_specs), len(scratch_shapes)
    if rider is None:
        return _pcall(body, grid=grid, in_specs=in_specs, out_specs=out_specs, out_shape=out_shape, scratch_shapes=scratch_shapes, name=name)(*args), []
    ri, ro = len(rider.operands), len(rider.out_shape)

    def wrapped(*refs):
        ins, r_in = refs[:n_in], refs[n_in:n_in + ri]
        o0 = n_in + ri
        outs, r_out = refs[o0:o0 + n_out], refs[o0 + n_out:o0 + n_out + ro]
        s0 = o0 + n_out + ro
        scr, r_scr = refs[s0:s0 + n_scr], refs[s0 + n_scr:]
        ids = [pl.program_id(i) for i in range(len(grid))]
        first = functools.reduce(jnp.logical_and, [i == 0 for i in ids])
        last = functools.reduce(jnp.logical_and, [i == g - 1 for i, g in zip(ids, grid)])

        @pl.when(first)
        def _():
            rider.start(r_in, r_out, r_scr)

        body(*ins, *outs, *scr)

        @pl.when(last)
        def _():
            rider.wait(r_in, r_out, r_scr)

    res = _pcall(wrapped, grid=grid, in_specs=list(in_specs) + [_ANY] * ri, out_specs=list(out_specs) + [_ANY] * ro,
                 out_shape=list(out_shape) + list(rider.out_shape), scratch_shapes=list(scratch_shapes) + list(rider.scratch),
                 name=name)(*args, *rider.operands)
    return res[:n_out], res[n_out:]


def comm_call(rider, name):
    ri = len(rider.operands)

    def body(*refs):
        r_in, r_out, r_scr = refs[:ri], refs[ri:ri + len(rider.out_shape)], refs[ri + len(rider.out_shape):]
        rider.start(r_in, r_out, r_scr)
        rider.wait(r_in, r_out, r_scr)

    return _pcall(body, in_specs=[_ANY] * ri, out_specs=[_ANY] * len(rider.out_shape), out_shape=list(rider.out_shape),
                  scratch_shapes=list(rider.scratch), name=name)(*rider.operands)


def gather_rider(shards, axes):
    K = len(shards)
    widths = [s.shape[a] for s, a in zip(shards, axes)]
    out_shape = [_S(s.shape[:a] + (N_DEV * s.shape[a],) + s.shape[a + 1:], s.dtype) for s, a in zip(shards, axes)]

    def plan(x_refs, o_refs, sems):
        send_sems, recv_sems, local_sems = sems
        mx, my, mc = lax.axis_index("x"), lax.axis_index("y"), lax.axis_index("c")
        me, sibling = (mx, my, mc), (mx, my, 1 - mc)
        chips = [(1 - mx, my), (mx, 1 - my), (1 - mx, 1 - my)]

        def win(k, px, py, pc):
            return _window(o_refs[k], axes[k], 4 * px + 2 * py + pc, widths[k])

        def copy(k, slot, block, to, src=None):
            return pltpu.make_async_remote_copy(src_ref=win(k, *block) if src is None else src, dst_ref=win(k, *block),
                                                send_sem=send_sems.at[7 * k + slot], recv_sem=recv_sems.at[7 * k + slot],
                                                device_id=to, device_id_type=MESH_ID)

        mine = [pltpu.make_async_copy(x_refs[k], win(k, *me), local_sems.at[k]) for k in range(K)]
        first = []
        for k in range(K):
            first.append(copy(k, 0, me, sibling, src=x_refs[k]))
            first += [copy(k, 1 + j, me, (*chip, mc), src=x_refs[k]) for j, chip in enumerate(chips)]
        return me, sibling, chips, copy, mine, first

    def start(x_refs, o_refs, sems):
        _, _, _, _, mine, first = plan(x_refs, o_refs, sems)
        for cp in mine + first:
            cp.start()

    def wait(x_refs, o_refs, sems):
        me, sibling, chips, copy, mine, first = plan(x_refs, o_refs, sems)
        mc = me[2]
        passed = []
        for j, chip in enumerate(chips):
            for k in range(K):
                copy(k, 1 + j, (*chip, mc), me).wait_recv()
                passed.append(copy(k, 4 + j, (*chip, mc), sibling))
                passed[-1].start()
        for k in range(K):
            copy(k, 0, sibling, me).wait_recv()
        for j, chip in enumerate(chips):
            for k in range(K):
                copy(k, 4 + j, (*chip, 1 - mc), me).wait_recv()
        for cp in first + passed:
            cp.wait_send()
        for cp in mine:
            cp.wait()

    scratch = [pltpu.SemaphoreType.DMA((7 * K,)), pltpu.SemaphoreType.DMA((7 * K,)), pltpu.SemaphoreType.DMA((K,))]
    return Rider(list(shards), out_shape, scratch, start, wait)


def exchange_rider(items):
    ns = len(items)
    out_shape = [_S((N_DEV,) + tuple(it[3]), it[0].dtype) for it in items]

    def plan(src_refs, o_refs, sems):
        send_sems, recv_sems, local_sems = sems
        mx, my, mc = lax.axis_index("x"), lax.axis_index("y"), lax.axis_index("c")
        me = 4 * mx + 2 * my + mc
        remote, own = [], []
        for s, (_, ax, n, _) in enumerate(items):
            own.append(pltpu.make_async_copy(_window(src_refs[s], ax, me, n), o_refs[s].at[me], local_sems.at[s]))
            for k in range(1, N_DEV):
                px = 1 - mx if k & 4 else mx
                py = 1 - my if k & 2 else my
                pc = 1 - mc if k & 1 else mc
                remote.append(pltpu.make_async_remote_copy(
                    src_ref=_window(src_refs[s], ax, 4 * px + 2 * py + pc, n), dst_ref=o_refs[s].at[me],
                    send_sem=send_sems.at[7 * s + k - 1], recv_sem=recv_sems.at[7 * s + k - 1],
                    device_id=(px, py, pc), device_id_type=MESH_ID))
        return remote, own

    def start(src_refs, o_refs, sems):
        remote, own = plan(src_refs, o_refs, sems)
        for cp in own + remote:
            cp.start()

    def wait(src_refs, o_refs, sems):
        remote, own = plan(src_refs, o_refs, sems)
        for cp in remote + own:
            cp.wait()

    scratch = [pltpu.SemaphoreType.DMA((7 * ns,)), pltpu.SemaphoreType.DMA((7 * ns,)), pltpu.SemaphoreType.DMA((ns,))]
    return Rider([it[0] for it in items], out_shape, scratch, start, wait)


def reduce_adamw(rcvs, w, m, v, name):
    L = len(rcvs)
    _, A, B, C = rcvs[0].shape
    tb = B
    while tb > 8 and tb * C > (1 << 17):
        tb //= 2

    def body(*refs):
        r_refs, (w_ref, m_ref, v_ref, g_ref, d_ref, mo_ref, vo_ref) = refs[:L], refs[L:]
        for k in range(L):
            @pl.when(pl.program_id(0) == k)
            def _(k=k):
                g = r_refs[k][0].astype(F32)
                for d in range(1, N_DEV):
                    g = g + r_refs[k][d].astype(F32)
                mn = ADAM_B1 * m_ref[...] + (1.0 - ADAM_B1) * g
                vn = ADAM_B2 * v_ref[...] + (1.0 - ADAM_B2) * jnp.square(g)
                m_hat = mn / (1.0 - ADAM_B1 ** ADAM_STEP)
                v_hat = vn / (1.0 - ADAM_B2 ** ADAM_STEP)
                g_ref[...] = g
                d_ref[...] = -ADAM_LR * (m_hat / (jnp.sqrt(v_hat) + ADAM_EPS) + ADAM_WD * w_ref[...])
                mo_ref[...] = mn
                vo_ref[...] = vn

    def rspec(k):
        return pl.BlockSpec((N_DEV, None, tb, C), lambda l, a, i: (0, jnp.where(l == k, a, 0), jnp.where(l == k, i, 0), 0))

    blk = pl.BlockSpec((None, tb, C), lambda l, a, i: (l * A + a, i, 0))
    return _pcall(body, grid=(L, A, B // tb), in_specs=[rspec(k) for k in range(L)] + [blk, blk, blk],
                  out_specs=[blk] * 4, out_shape=[_S((L * A, B, C))] * 4, name=name)(*rcvs, w, m, v)


def _w_in_pieces(g0, g1):
    per = D_IN // N_DEV
    return [(d, max(g0, d * per) - d * per, min(g1, (d + 1) * per) - d * per) for d in range(N_DEV) if max(g0, d * per) < min(g1, (d + 1) * per)]


def repack_w_in(w8, name):
    _, L, R, per = w8.shape
    tr = 256

    def cols(x_ref, g0, g1):
        return [x_ref[d, :, a:b] for d, a, b in _w_in_pieces(g0, g1)]

    def body(x_ref, *o_refs):
        for (name_, i), o_ref in zip(SEGS, o_refs):
            o_ref[...] = jnp.concatenate(cols(x_ref, _OFF[i], _OFF[i + 1]), axis=1)
        parts, at = [], 0
        for i, lane0 in SMALL_SRC:
            assert lane0 == at
            parts += cols(x_ref, _OFF[i], _OFF[i + 1])
            at += IN_SIZES[i]
        parts.append(jnp.zeros((tr, LANES - at), w8.dtype))
        o_refs[-1][...] = jnp.concatenate(parts, axis=1)

    widths = [IN_SIZES[i] for _, i in SEGS] + [LANES]
    outs = _pcall(body, grid=(L, R // tr), in_specs=[pl.BlockSpec((N_DEV, None, tr, per), lambda l, r: (0, l, r, 0))],
                  out_specs=[pl.BlockSpec((None, tr, w), lambda l, r: (l, r, 0)) for w in widths],
                  out_shape=[_S((L, R, w), w8.dtype) for w in widths], name=name)(w8)
    return dict(zip(SEG_NAMES, outs))


def repack_dw_in(dseg, name):
    R = dseg["z"].shape[0]
    per = D_IN // N_DEV
    tr = 128
    src = {i: (k, 0) for k, (_, i) in enumerate(SEGS)}
    src.update({i: (len(SEGS), lane0) for i, lane0 in SMALL_SRC})

    def body(*refs):
        s_refs, o_ref = refs[:-1], refs[-1]
        for d in range(N_DEV):
            parts = []
            for i in range(len(IN_SIZES)):
                g0, g1 = max(_OFF[i], d * per), min(_OFF[i + 1], (d + 1) * per)
                if g0 < g1:
                    k, c0 = src[i]
                    parts.append(s_refs[k][:, c0 + g0 - _OFF[i]:c0 + g1 - _OFF[i]])
            o_ref[d] = jnp.concatenate(parts, axis=1)

    arrs = [dseg[n] for n in SEG_NAMES]
    return _pcall(body, grid=(R // tr,), in_specs=[pl.BlockSpec((tr, a.shape[1]), lambda r: (r, 0)) for a in arrs],
                  out_specs=pl.BlockSpec((N_DEV, tr, per), lambda r: (0, r, 0)), out_shape=_S((N_DEV, R, per), arrs[0].dtype), name=name)(*arrs)


WEIGHTS = ("ln_in_g", "ln_in_b", "w_in", "ssd_conv_w", "ssd_conv_b", "ssd_dt_bias", "ssd_a_log", "ssd_d", "ssd_norm_w", "dn_conv_w",
           "dn_a_log", "dn_dt_bias", "dn_norm_w", "sg_ln_g", "sg_ln_b", "sg_w", "sg_b", "fox_f_bias", "gate_b", "w_branch", "w_out",
           "ln1_g", "ln1_b", "w_up", "w_down", "ln2_g", "ln2_b")
SHARDED = {"w_in": 2, "ssd_conv_w": 2, "dn_conv_w": 2, "gate_b": 2, "w_branch": 3, "w_out": 1, "w_up": 2, "w_down": 1}
SLABBED = ("w_in", "dn_conv_w")
MATMUL_WEIGHTS = ("w_in", "w_branch", "w_out", "w_up", "w_down")
REPLICATED = tuple(n for n in WEIGHTS if n not in SHARDED)
SEG_NAMES = tuple(n for n, _ in SEGS) + ("small",)
PACK_COLS = 1024


def _lanes(vec, off):
    return jnp.pad(vec, (off, LANES - off - vec.shape[0]))[None]


def _pack_small(parts):
    flat = jnp.concatenate([q.reshape(-1) for q in parts])
    rows = -(-flat.shape[0] // (PACK_COLS * 64)) * 64
    return jnp.pad(flat, (0, rows * PACK_COLS - flat.shape[0])).reshape(1, rows, PACK_COLS)


EARLY = ("w_branch", "w_out", "w_up", "w_down", "gate_b")
LATE = ("w_in", "ssd_conv_w", "dn_conv_w")


def _gather_rider(p, l, names):
    shards, axes = [], []
    for n in names:
        s = p[n][l:l + 1]
        s = s.astype(BF) if n in MATMUL_WEIGHTS else s
        shards.append(s[None] if n in SLABBED else s)
        axes.append(0 if n in SLABBED else SHARDED[n])
    return gather_rider(shards, axes)


def _exchange_items(g, p, names):
    items = []
    for n in names:
        local = p[n].shape[1:]
        items.append((g[n], 0, 0, local) if n in SLABBED else (g[n], SHARDED[n] - 1, local[SHARDED[n] - 1], local))
    return items


def _use_gathered(w, names, arrays, l):
    for n, arr in zip(names, arrays):
        if n == "w_in":
            w[n] = repack_w_in(arr, f"w_in_repack_{l}")
        elif n == "ssd_conv_w":
            w["ssd_cw"] = arr[0]
        elif n == "dn_conv_w":
            w["dn_cw"] = jnp.moveaxis(arr[:, 0], 0, 1).reshape(4, 3 * BRANCH_W)
        elif n == "gate_b":
            w[n] = arr[0]
        else:
            w[n] = arr


def _layer_weights(p, l):
    w = {}
    w["ssd_cb"] = p["ssd_conv_b"][l][None]
    w["dn_cb"] = jnp.zeros((1, 3 * BRANCH_W), F32)
    w["ssd_ps"] = [_lanes(p["ssd_dt_bias"][l], DT0), _lanes(p["ssd_a_log"][l], DT0), _lanes(p["ssd_d"][l], DT0), p["ssd_norm_w"][l][None]]
    w["dn_ps"] = [_lanes(p["dn_a_log"][l], A0), _lanes(p["dn_dt_bias"][l], A0), p["dn_norm_w"][l][None]]
    w["sg_ps"] = [p["sg_ln_g"][l][None], p["sg_ln_b"][l][None], p["sg_w"][l], jnp.pad(p["sg_b"][l].T, ((0, 0), (0, LANES - 4)))]
    w["fox_ps"] = [_lanes(p["fox_f_bias"][l], FF0)]
    for n in ("ln1_g", "ln1_b", "ln2_g", "ln2_b"):
        w[n] = p[n][l][None]
    return w


def _scan_specs(T, a):
    c0 = lambda c, h: (c, 0)
    ssd = dict(f=ssd_chunk, xs=[(a["z"], (128, 512), c0), (a["xbc_act"], (128, 1024), c0), (a["small"], (128, LANES), c0)],
               ys=[((T, BRANCH_W), (128, BRANCH_W), c0)], state=(4, LANES, LANES), nc=T // 128, nh=1, shared=())
    dn = dict(f=dn_chunk, xs=[(a["dn_act"], (64, 3 * BRANCH_W), c0), (a["dngate"], (64, BRANCH_W), c0), (a["small"], (64, LANES), c0)],
              ys=[((T, BRANCH_W), (64, BRANCH_W), c0)], state=(4, LANES, LANES), nc=T // 64, nh=1, shared=())
    sg = dict(f=sg_chunk, xs=[(a["sguv"], (128, 1024), c0)], ys=[((T, BRANCH_W), (128, BRANCH_W), c0)], state=(1, 8, LANES), nc=T // 128, nh=1, shared=())
    fc = dict(f=foxc_chunk, xs=[(a["small"], (128, LANES), c0)],
              ys=[((T, LANES), (128, LANES), c0)], state=(1, 1, LANES), nc=T // 128, nh=1, shared=())
    return ssd, dn, sg, fc


def _layer_fwd(h, w, l, dn_rider=None, fox_rider=None):
    T = h.shape[0]
    a = {"h": h}
    for n in SEG_NAMES:
        a[n] = matmul_w(h, w["w_in"][n], 0, "nn", f"proj_{n}_{l}")
    a["xbc_act"] = conv_fwd(a["xbc"], w["ssd_cw"], w["ssd_cb"], f"ssd_conv_{l}")
    a["dn_act"] = conv_fwd(a["dnqkv"], w["dn_cw"], w["dn_cb"], f"dn_conv_{l}")
    ssd, dn, sg, fc = _scan_specs(T, a)
    (a["ya"], a["ssd_st"]), _ = scan_fwd(f"ssd_fwd_{l}", ssd["f"], ssd["xs"], w["ssd_ps"], ssd["ys"], ssd["state"], ssd["nc"], ssd["nh"])
    (a["yb"], a["dn_st"]), got = scan_fwd(f"dn_fwd_{l}", dn["f"], dn["xs"], w["dn_ps"], dn["ys"], dn["state"], dn["nc"], dn["nh"], rider=dn_rider)
    _use_gathered(w, EARLY, got, l)
    (a["yc"], a["sg_st"]), _ = scan_fwd(f"sg_fwd_{l}", sg["f"], sg["xs"], w["sg_ps"], sg["ys"], sg["state"], sg["nc"], sg["nh"])
    (a["ccol"], a["fc_st"]), _ = scan_fwd(f"foxc_fwd_{l}", fc["f"], fc["xs"], w["fox_ps"], fc["ys"], fc["state"], fc["nc"], fc["nh"])
    a["fox_qa"], a["fox_ka"], a["fox_va"] = fox_prep(a["foxqkv"], a["ccol"], f"fox_prep_{l}")
    (a["yd"], a["lse"]), carried = fox_fwd(a["fox_qa"], a["fox_ka"], a["fox_va"], f"fox_fwd_{l}", rider=fox_rider)
    a["merged"] = merge_fwd([a["ya"], a["yb"], a["yc"], a["yd"]], a["gates"], w["gate_b"], w["w_branch"], 0, f"merge_fwd_{l}")
    a["u1"], a["h1"] = out_fwd(a["merged"], h, w["w_out"], 0, w["ln1_g"], w["ln1_b"], f"out_fwd_{l}")
    a["u2"], a["h2"] = ff_fwd(a["h1"], w["w_up"], w["w_down"], 0, w["ln2_g"], w["ln2_b"], f"ff_fwd_{l}")
    return a, carried


def _layer_bwd(dh2, a, w, l, p, late_above):
    T = dh2.shape[0]
    g = {}
    du2, dh1, da, r, dg2, db2 = ff_bwd(a["u2"], dh2, a["h1"], w["ln2_g"], w["ln2_b"], w["w_up"], w["w_down"], 0, f"ff_bwd_{l}")
    g["ln2_g"], g["ln2_b"] = dg2[0], db2[0]
    g["w_up"] = matmul_w(transpose_bf16(a["h1"], f"h1_t_{l}"), da[None], 0, "nn", f"dwup_{l}", out_dtype=BF)
    g["w_down"] = matmul_w(r, du2[None], 0, "nn", f"dwdown_{l}", out_dtype=BF)
    du1, dmerged, dg1, db1 = out_bwd(a["u1"], dh1, w["ln1_g"], w["ln1_b"], w["w_out"], 0, f"out_bwd_{l}")
    g["ln1_g"], g["ln1_b"] = dg1[0], db1[0]
    g["w_out"] = matmul_w(transpose_bf16(a["merged"], f"merged_t_{l}"), du1[None], 0, "nn", f"dwout_{l}", out_dtype=BF)
    ys = [a["ya"], a["yb"], a["yc"], a["yd"]]
    dya, dyb, dyc, dyd, dgl, dz, dgb = merge_bwd(ys, a["gates"], w["gate_b"], w["w_branch"], 0, dmerged, f"merge_bwd_{l}")
    g["gate_b"] = dgb
    g["w_branch"] = jnp.stack([matmul_tn(ys[i], dz, f"dwb{i}_{l}", b_col0=i * D_MODEL, n_cols=D_MODEL, out_dtype=BF) for i in range(4)])
    early = exchange_rider(_exchange_items(g, p, EARLY))
    dn_rider = early if late_above is None else exchange_rider(late_above)
    fox_rider = None if late_above is None else early
    ssd, dn, sg, fc = _scan_specs(T, a)
    (dz_ssd, dxbc_act, dsm_ssd, d_dtb, d_alog, d_dsk, d_nw), _ = scan_bwd(f"ssd_bwd_{l}", ssd["f"], ssd["xs"], w["ssd_ps"], ssd["ys"], [dya], a["ssd_st"],
                                                                           ssd["state"], ssd["nc"], ssd["nh"])
    g["ssd_dt_bias"], g["ssd_a_log"], g["ssd_d"], g["ssd_norm_w"] = d_dtb[0, DT0:DT0 + 8], d_alog[0, DT0:DT0 + 8], d_dsk[0, DT0:DT0 + 8], d_nw[0]
    dxbc, g["ssd_conv_w"], dcb = conv_bwd(a["xbc"], w["ssd_cw"], w["ssd_cb"], dxbc_act, f"ssd_conv_bwd_{l}")
    g["ssd_conv_b"] = dcb[0]
    (ddn_act, ddngate, dsm_dn, d_alog, d_dtb, d_nw), got_dn = scan_bwd(f"dn_bwd_{l}", dn["f"], dn["xs"], w["dn_ps"], dn["ys"], [dyb], a["dn_st"],
                                                                        dn["state"], dn["nc"], dn["nh"], rider=dn_rider)
    g["dn_a_log"], g["dn_dt_bias"], g["dn_norm_w"] = d_alog[0, A0:A0 + 4], d_dtb[0, A0:A0 + 4], d_nw[0]
    ddnqkv, g["dn_conv_w"], _ = conv_bwd(a["dnqkv"], w["dn_cw"], w["dn_cb"], ddn_act, f"dn_conv_bwd_{l}")
    (dsguv, d_lng, d_lnb, d_w, d_bt), _ = scan_bwd(f"sg_bwd_{l}", sg["f"], sg["xs"], w["sg_ps"], sg["ys"], [dyc], a["sg_st"], sg["state"], sg["nc"], sg["nh"])
    g["sg_ln_g"], g["sg_ln_b"], g["sg_w"], g["sg_b"] = d_lng[0], d_lnb[0], d_w, d_bt[:, :4].T
    qb, doa, qd, kd = fox_prep_bwd(a["foxqkv"], a["fox_qa"], a["yd"], a["lse"], dyd, f"fox_prep_bwd_{l}")
    (dfq, dfk, dfv, dccol), got_fox = fox_bwd(qb, a["fox_ka"], a["fox_va"], doa, qd, kd, f"fox_bwd_{l}", rider=fox_rider)
    (dsm_fox, d_fb), _ = scan_bwd(f"foxc_bwd_{l}", fc["f"], fc["xs"], w["fox_ps"], fc["ys"], [dccol], a["fc_st"], fc["state"], fc["nc"], fc["nh"])
    g["fox_f_bias"] = d_fb[0, FF0:FF0 + 8]
    dseg = {"z": dz_ssd, "xbc": dxbc, "dnqkv": ddnqkv, "dngate": ddngate, "sguv": dsguv,
            "foxqkv": jnp.concatenate([dfq, dfk, dfv], axis=1), "gates": dgl, "small": add3(dsm_ssd, dsm_dn, dsm_fox, f"dsmall_{l}")}
    dh, scale, dwin = du1, ALPHA, {}
    h_t = transpose_bf16(a["h"], f"h_t_{l}")
    for n in SEG_NAMES:
        dh = matmul_w(dseg[n], w["w_in"][n], 0, "nt", f"dh_{n}_{l}", add=dh, add_scale=scale)
        scale = 1.0
        dwin[n] = matmul_w(h_t, dseg[n][None], 0, "nn", f"dwin_{n}_{l}", out_dtype=BF)
    g["w_in"] = repack_dw_in(dwin, f"dw_in_repack_{l}")
    g["dn_conv_w"] = jnp.moveaxis(g["dn_conv_w"].reshape(4, N_DEV, 3 * BRANCH_W // N_DEV), 1, 0)
    got = {(EARLY, l): got_dn} if late_above is None else {(LATE, l + 1): got_dn, (EARLY, l): got_fox}
    return dh, g, got


def kernel(x, ln_in_g, ln_in_b, w_in, ssd_conv_w, ssd_conv_b, ssd_dt_bias, ssd_a_log, ssd_d, ssd_norm_w, dn_conv_w, dn_a_log, dn_dt_bias, dn_norm_w, sg_ln_g, sg_ln_b, sg_w, sg_b, fox_f_bias, gate_b, w_branch, w_out, ln1_g, ln1_b, w_up, w_down, ln2_g, ln2_b, loss_target, m_ln_in_g, m_ln_in_b, m_w_in, m_ssd_conv_w, m_ssd_conv_b, m_ssd_dt_bias, m_ssd_a_log, m_ssd_d, m_ssd_norm_w, m_dn_conv_w, m_dn_a_log, m_dn_dt_bias, m_dn_norm_w, m_sg_ln_g, m_sg_ln_b, m_sg_w, m_sg_b, m_fox_f_bias, m_gate_b, m_w_branch, m_w_out, m_ln1_g, m_ln1_b, m_w_up, m_w_down, m_ln2_g, m_ln2_b, v_ln_in_g, v_ln_in_b, v_w_in, v_ssd_conv_w, v_ssd_conv_b, v_ssd_dt_bias, v_ssd_a_log, v_ssd_d, v_ssd_norm_w, v_dn_conv_w, v_dn_a_log, v_dn_dt_bias, v_dn_norm_w, v_sg_ln_g, v_sg_ln_b, v_sg_w, v_sg_b, v_fox_f_bias, v_gate_b, v_w_branch, v_w_out, v_ln1_g, v_ln1_b, v_w_up, v_w_down, v_ln2_g, v_ln2_b):
    args = dict(locals())
    p = {n: args[n] for n in WEIGHTS}
    xt, target = x[0], loss_target[0]
    ws, acts = [_layer_weights(p, l) for l in range(DEPTH)], []
    _use_gathered(ws[0], LATE, comm_call(_gather_rider(p, 0, LATE), "weights_all_gather_0"), 0)
    h = ln_fwd(xt, ln_in_g[None], ln_in_b[None], "ln_in_fwd")
    for l in range(DEPTH):
        a, gathered = _layer_fwd(h, ws[l], l, dn_rider=_gather_rider(p, 0, EARLY) if l == 0 else None,
                                 fox_rider=_gather_rider(p, l + 1, LATE + EARLY) if l + 1 < DEPTH else None)
        if l + 1 < DEPTH:
            _use_gathered(ws[l + 1], LATE + EARLY, gathered, l + 1)
        acts.append(a)
        h = a["h2"]
    dh, loss = loss_head(h, target, "loss_head")
    loss = lax.psum(loss[0, 0], ("x", "y", "c"))

    layer_grads, got, late = [None] * DEPTH, {}, None
    for l in reversed(range(DEPTH)):
        dh, layer_grads[l], got_l = _layer_bwd(dh, acts[l], ws[l], l, p, late)
        got.update(got_l)
        late = _exchange_items(layer_grads[l], p, LATE)
    grad_x, dg_in, db_in = ln_bwd(xt, ln_in_g[None], ln_in_b[None], dh, "ln_in_bwd")
    small = {n: jnp.stack([layer_grads[l][n] for l in range(DEPTH)]) for n in REPLICATED if n not in ("ln_in_g", "ln_in_b")}
    small["ln_in_g"], small["ln_in_b"] = dg_in[0], db_in[0]
    pack = _pack_small([small[n] for n in REPLICATED])
    last = comm_call(exchange_rider(late + [(pack[0], 0, -1, pack.shape[1:])]), "grads_exchange_last")
    got[(LATE, 0)] = last[:-1]
    rcv = {(n, l): arr for (names, l), arrs in got.items() for n, arr in zip(names, arrs)}

    res = [{}, {}, {}, {}]
    for n in SHARDED:
        shp = p[n].shape
        lead = math.prod(shp[1:-2])
        to3 = lambda t: t.reshape((-1,) + shp[-2:])
        outs = reduce_adamw([rcv[(n, l)].reshape((N_DEV, lead) + shp[-2:]) for l in range(DEPTH)],
                            to3(p[n]), to3(args["m_" + n]), to3(args["v_" + n]), f"adamw_{n}")
        for k in range(4):
            res[k][n] = outs[k].reshape(shp)
    outs = reduce_adamw([last[-1][:, None]], _pack_small([p[n] for n in REPLICATED]), _pack_small([args["m_" + n] for n in REPLICATED]),
                        _pack_small([args["v_" + n] for n in REPLICATED]), "adamw_replicated")
    off = 0
    for n in REPLICATED:
        shp = p[n].shape
        cnt = math.prod(shp)
        for k in range(4):
            res[k][n] = outs[k].reshape(-1)[off:off + cnt].reshape(shp)
        off += cnt
    return (loss, grad_x[None], *[res[0][n] for n in WEIGHTS], *[res[1][n] for n in WEIGHTS],
            *[res[2][n] for n in WEIGHTS], *[res[3][n] for n in WEIGHTS])
```

```python
import collections
import functools
import math

import jax
import jax.numpy as jnp
from jax import lax
from jax.experimental import pallas as pl
from jax.experimental.pallas import tpu as pltpu

F32 = jnp.float32
BF = jnp.bfloat16

D_MODEL = 1024
DEPTH = 2
BRANCH_W = 512
D_FF = 4096
LN_EPS = 1e-5
NORM_EPS = 1e-6
ALPHA = (2 * DEPTH) ** 0.25
N_DEV = 8
LANES = 128
ADAM_LR, ADAM_B1, ADAM_B2, ADAM_EPS, ADAM_WD, ADAM_STEP = 0.001, 0.9, 0.999, 1e-08, 0.01, 10

DT0, BETA0, A0, FF0 = 0, 8, 12, 16
IN_SIZES = (512, 1024, 8, 1536, 4, 4, 512, 1024, 1536, 8, 4096)
_OFF = [0]
for _s in IN_SIZES:
    _OFF.append(_OFF[-1] + _s)
D_IN = _OFF[-1]
SEGS = (("z", 0), ("xbc", 1), ("dnqkv", 3), ("dngate", 6), ("sguv", 7), ("foxqkv", 8), ("gates", 10))
SMALL_SRC = ((2, DT0), (4, BETA0), (5, A0), (9, FF0))

NN = ((1,), (0,))
NT = ((1,), (1,))
TN = ((0,), (0,))
_DIMS = {"nn": NN, "nt": NT, "tn": TN}


def _pcall(body, **kw):
    return pl.pallas_call(body, **kw)


def _S(shape, dtype=F32):
    return jax.ShapeDtypeStruct(tuple(shape), dtype)


def _iota(shape, dim):
    return lax.broadcasted_iota(jnp.int32, shape, dim)


def _dotb(a, b, dims):
    return lax.dot_general(a, b, (dims, ((), ())), preferred_element_type=F32)


def _split2(a):
    ah = a.astype(BF)
    return ah, (a - ah.astype(F32)).astype(BF)


def _split3(a):
    a1 = a.astype(BF)
    r = a - a1.astype(F32)
    a2 = r.astype(BF)
    a3 = (r - a2.astype(F32)).astype(BF)
    return a1, a2, a3


def _mm_raw(a, b, form, mode):
    d = _DIMS[form]
    if mode == "1":
        return _dotb(a.astype(BF), b.astype(BF), d)
    if mode == "3":
        ah, al = _split2(a)
        bh, bl = _split2(b)
        return _dotb(ah, bh, d) + (_dotb(ah, bl, d) + _dotb(al, bh, d))
    if mode == "xa":
        ab = a.astype(BF)
        b1, b2, b3 = _split3(b)
        return _dotb(ab, b1, d) + (_dotb(ab, b2, d) + _dotb(ab, b3, d))
    bb = b.astype(BF)
    a1, a2, a3 = _split3(a)
    return _dotb(a1, bb, d) + (_dotb(a2, bb, d) + _dotb(a3, bb, d))


@functools.partial(jax.custom_vjp, nondiff_argnums=(2, 3))
def mm(a, b, form, mode):
    return _mm_raw(a, b, form, mode)


def _mm_fwd(a, b, form, mode):
    return _mm_raw(a, b, form, mode), (a, b)


_XA_DB = {"nn": "xa", "nt": "xb", "tn": "xa"}
_XB_DA = {"nn": "xb", "nt": "xb", "tn": "xa"}


def _mm_bwd(form, mode, res, g):
    a, b = res
    ma = _XB_DA[form] if mode == "xb" else mode
    mb = _XA_DB[form] if mode == "xa" else mode
    da = db = None
    if mode != "xa":
        da = {"nn": lambda: mm(g, b, "nt", ma), "nt": lambda: mm(g, b, "nn", ma), "tn": lambda: mm(b, g, "nt", ma)}[form]()
    if mode != "xb":
        db = {"nn": lambda: mm(a, g, "tn", mb), "nt": lambda: mm(g, a, "tn", mb), "tn": lambda: mm(a, g, "nn", mb)}[form]()
    if da is None:
        da = jnp.zeros_like(a)
    if db is None:
        db = jnp.zeros_like(b)
    return da, db


mm.defvjp(_mm_fwd, _mm_bwd)


def _silu(x):
    return x * jax.nn.sigmoid(x)


def _ln(x, g, b):
    mu = jnp.mean(x, -1, keepdims=True)
    xc = x - mu
    var = jnp.mean(xc * xc, -1, keepdims=True)
    return xc * lax.rsqrt(var + LN_EPS) * g + b


def _pick(n, cap):
    if n <= cap:
        return n
    best = LANES
    for t in range(LANES, cap + 1, LANES):
        if n % t == 0:
            best = t
    return best


def transpose_bf16(a, name):
    T, C = a.shape
    tt = min(T, 512)

    def body(a_ref, o_ref):
        o_ref[...] = a_ref[...].T.astype(BF)

    return _pcall(body, grid=(T // tt,), in_specs=[pl.BlockSpec((tt, C), lambda t: (t, 0))], out_specs=pl.BlockSpec((C, tt), lambda t: (0, t)),
                  out_shape=_S((C, T), BF), name=name)(a)


def matmul_w(a, w, l, form, name, add=None, add_scale=1.0, out_dtype=F32):
    M, K = a.shape
    N = w.shape[2] if form == "nn" else w.shape[1]
    tm, tn, tk = min(M, 512), _pick(N, 1024), _pick(K, 1024)
    nk = K // tk

    def body(*refs):
        if add is None:
            a_ref, b_ref, o_ref, acc = refs
        else:
            a_ref, b_ref, d_ref, o_ref, acc = refs
        k = pl.program_id(2)
        p = _dotb(a_ref[...].astype(BF), b_ref[...].astype(BF), _DIMS[form])

        @pl.when(k == 0)
        def _():
            acc[...] = p

        @pl.when(k > 0)
        def _():
            acc[...] += p

        @pl.when(k == nk - 1)
        def _():
            r = acc[...]
            if add is not None:
                r = r + add_scale * d_ref[...]
            o_ref[...] = r.astype(out_dtype)

    if form == "nn":
        wspec = pl.BlockSpec((None, tk, tn), lambda j, i, k: (l, k, j))
    else:
        wspec = pl.BlockSpec((None, tn, tk), lambda j, i, k: (l, j, k))
    in_specs = [pl.BlockSpec((tm, tk), lambda j, i, k: (i, k)), wspec]
    args = [a, w]
    if add is not None:
        in_specs.append(pl.BlockSpec((tm, tn), lambda j, i, k: (i, j)))
        args.append(add)
    return _pcall(body, grid=(N // tn, M // tm, nk), in_specs=in_specs,
                  out_specs=pl.BlockSpec((tm, tn), lambda j, i, k: (i, j)), out_shape=_S((M, N), out_dtype),
                  scratch_shapes=[pltpu.VMEM((tm, tn), F32)], name=name)(*args)


def matmul_tn(a, b, name, b_col0=0, n_cols=None, out_dtype=F32):
    T, M = a.shape
    N = b.shape[1] if n_cols is None else n_cols
    tm, tn, tt = _pick(M, 512), _pick(N, 1024), min(T, 512)
    nt = T // tt
    jb = b_col0 // tn

    def body(a_ref, b_ref, o_ref, acc):
        t = pl.program_id(2)
        p = _dotb(a_ref[...].astype(BF), b_ref[...].astype(BF), TN)

        @pl.when(t == 0)
        def _():
            acc[...] = p

        @pl.when(t > 0)
        def _():
            acc[...] += p

        @pl.when(t == nt - 1)
        def _():
            o_ref[...] = acc[...].astype(out_dtype)

    return _pcall(body, grid=(M // tm, N // tn, nt),
                  in_specs=[pl.BlockSpec((tt, tm), lambda i, j, t: (t, i)), pl.BlockSpec((tt, tn), lambda i, j, t: (t, jb + j))],
                  out_specs=pl.BlockSpec((tm, tn), lambda i, j, t: (i, j)), out_shape=_S((M, N), out_dtype),
                  scratch_shapes=[pltpu.VMEM((tm, tn), F32)], name=name)(a, b)


def _pieces(v):
    if v.ndim == 3:
        return [v[i] for i in range(v.shape[0])]
    n = v.shape[1] // LANES
    if n <= 1:
        return [v]
    return [v[:, i * LANES:(i + 1) * LANES] for i in range(n)]


def _join(ps, like_ndim):
    if like_ndim == 3:
        return jnp.stack(ps, axis=0)
    return ps[0] if len(ps) == 1 else jnp.concatenate(ps, axis=1)


def scan_fwd(name, f, xs, ps, ys, state_shape, nc, nh=1, rider=None):
    nx, npar, ny = len(xs), len(ps), len(ys)

    def body(*refs):
        x_refs, p_refs = refs[:nx], refs[nx:nx + npar]
        y_refs = refs[nx + npar:nx + npar + ny]
        st_out, st = refs[nx + npar + ny], refs[nx + npar + ny + 1]
        c, h = pl.program_id(0), pl.program_id(1)

        @pl.when(c == 0)
        def _():
            st[h] = jnp.zeros(state_shape, F32)

        S = st[h]
        st_out[...] = S
        yv, Sn = f([_pieces(r[...]) for r in x_refs], [_pieces(r[...]) for r in p_refs], _pieces(S), h)
        for r, v in zip(y_refs, yv):
            r[...] = _join(v, 2)
        st[h] = _join(Sn, 3)

    in_specs = [pl.BlockSpec(bs, im) for (_, bs, im) in xs]
    in_specs += [pl.BlockSpec(p.shape, (lambda c, h, n=p.ndim: (0,) * n)) for p in ps]
    out_specs = [pl.BlockSpec(bs, im) for (_, bs, im) in ys]
    out_specs.append(pl.BlockSpec((None, None) + tuple(state_shape), lambda c, h: (c, h, 0, 0, 0)))
    out_shape = [_S(s) for (s, _, _) in ys] + [_S((nc, nh) + tuple(state_shape))]
    return hosted_call(body, rider, grid=(nc, nh), in_specs=in_specs, out_specs=out_specs, out_shape=out_shape,
                       scratch_shapes=[pltpu.VMEM((nh,) + tuple(state_shape), F32)], name=name, args=[*[x[0] for x in xs], *ps])


def scan_bwd(name, f, xs, ps, ys, dys, states, state_shape, nc, nh=1, shared=(), rider=None):
    nx, npar, ny = len(xs), len(ps), len(ys)

    def body(*refs):
        x_refs, p_refs = refs[:nx], refs[nx:nx + npar]
        s_ref = refs[nx + npar]
        dy_refs = refs[nx + npar + 1:nx + npar + 1 + ny]
        o = nx + npar + 1 + ny
        dx_refs, dp_refs, dst = refs[o:o + nx], refs[o + nx:o + nx + npar], refs[o + nx + npar]
        c, h = pl.program_id(0), pl.program_id(1)

        @pl.when(c == 0)
        def _():
            dst[h] = jnp.zeros(state_shape, F32)

        @pl.when((c == 0) & (h == 0))
        def _():
            for r in dp_refs:
                r[...] = jnp.zeros(r.shape, F32)

        xv = [_pieces(r[...]) for r in x_refs]
        pv = [_pieces(r[...]) for r in p_refs]
        _, vjp = jax.vjp(lambda a, b, s: f(a, b, s, h), xv, pv, _pieces(s_ref[...]))
        dxv, dpv, dS = vjp(([_pieces(r[...]) for r in dy_refs], _pieces(dst[h])))
        for i, (r, v) in enumerate(zip(dx_refs, dxv)):
            if i in shared and nh > 1:
                @pl.when(h == 0)
                def _(r=r, v=v):
                    r[...] = _join(v, 2)

                @pl.when(h > 0)
                def _(r=r, v=v):
                    r[...] += _join(v, 2)
            else:
                r[...] = _join(v, 2)
        for r, v in zip(dp_refs, dpv):
            r[...] += _join(v, len(r.shape))
        dst[h] = _join(dS, 3)

    def rev(im):
        return lambda c, h: im(nc - 1 - c, h)

    in_specs = [pl.BlockSpec(bs, rev(im)) for (_, bs, im) in xs]
    in_specs += [pl.BlockSpec(p.shape, (lambda c, h, n=p.ndim: (0,) * n)) for p in ps]
    in_specs.append(pl.BlockSpec((None, None) + tuple(state_shape), lambda c, h: (nc - 1 - c, h, 0, 0, 0)))
    in_specs += [pl.BlockSpec(bs, rev(im)) for (_, bs, im) in ys]
    out_specs = [pl.BlockSpec(bs, rev(im)) for (_, bs, im) in xs]
    out_specs += [pl.BlockSpec(p.shape, (lambda c, h, n=p.ndim: (0,) * n)) for p in ps]
    out_shape = [_S(x[0].shape) for x in xs] + [_S(p.shape) for p in ps]
    return hosted_call(body, rider, grid=(nc, nh), in_specs=in_specs, out_specs=out_specs, out_shape=out_shape,
                       scratch_shapes=[pltpu.VMEM((nh,) + tuple(state_shape), F32)], name=name,
                       args=[*[x[0] for x in xs], *ps, states, *dys])


def _lane():
    return _iota((1, LANES), 1)


def _col(v, idx):
    return jnp.sum(v * (_lane() == idx).astype(F32), axis=1, keepdims=True)


def _last_row(v):
    r = v.shape[0]
    return jnp.sum(v * (_iota((r, 1), 0) == r - 1).astype(F32), axis=0, keepdims=True)


def _tril(n, strict=False):
    r, c = _iota((n, n), 0), _iota((n, n), 1)
    return (r > c) if strict else (r >= c)


def ssd_chunk(xs, ps, S, h):
    zp, xbc, (sm,) = xs
    (bias,), (alog,), (dsk,), nw = ps
    Q = sm.shape[0]
    H = range(8)
    lane = _lane()
    a128 = jnp.where(lane < 8, -jnp.exp(alog), 0.0)
    dtl = jax.nn.softplus(sm + bias)
    tri = _tril(Q)
    cum = mm(tri.astype(F32), dtl * a128, "nn", "xa")
    sel8 = (_iota((8, LANES), 0) == _iota((8, LANES), 1)).astype(F32)
    cum_t = mm(sel8, cum, "nt", "xa")
    m0 = (lane < 64).astype(F32)
    rows0 = (_iota((LANES, 1), 0) < 64).astype(F32)
    me = [m0 if hh % 2 == 0 else 1.0 - m0 for hh in H]
    re = [rows0 if hh % 2 == 0 else 1.0 - rows0 for hh in H]
    Bm, Cm = [xbc[4 + hh // 4] for hh in H], [xbc[6 + hh // 4] for hh in H]
    cb = [mm(xbc[6 + g], xbc[4 + g], "nt", "1") for g in range(2)]
    col = [_col(cum, hh) for hh in H]
    row = [jnp.sum(cum_t * (_iota((8, 1), 0) == hh).astype(F32), axis=0, keepdims=True) for hh in H]
    xh = [xbc[hh // 2] * me[hh] for hh in H]
    xdt = [xh[hh] * _col(dtl, hh) for hh in H]
    seg = [jnp.exp(jnp.where(tri, col[hh] - row[hh], -jnp.inf)) for hh in H]
    last = [_last_row(col[hh]) for hh in H]
    y_diag = [mm(cb[hh // 4] * seg[hh], xdt[hh], "nn", "1") for hh in H]
    y_off = [mm(Cm[hh] * jnp.exp(col[hh]), S[hh // 2], "nt", "1") * me[hh] for hh in H]
    st = [mm(xdt[hh], Bm[hh] * jnp.exp(last[hh] - col[hh]), "tn", "1") for hh in H]
    y = [y_diag[hh] + y_off[hh] + _col(dsk, hh) * xh[hh] for hh in H]
    Sn = [S[pr] * (jnp.exp(last[2 * pr]) * re[0] + jnp.exp(last[2 * pr + 1]) * re[1]) + st[2 * pr] + st[2 * pr + 1] for pr in range(4)]
    yz = [(y[2 * pr] + y[2 * pr + 1]) * _silu(zp[pr]) for pr in range(4)]
    ssq = sum(jnp.sum(v * v, axis=1, keepdims=True) for v in yz)
    scale = lax.rsqrt(ssq / BRANCH_W + NORM_EPS)
    return [[yz[i] * scale * nw[i] for i in range(4)]], Sn


def dn_chunk(xs, ps, S, h):
    act, gate, (sm,) = xs
    (alog,), (dtb,), (nw,) = ps
    H = range(4)
    C = sm.shape[0]
    lane = _lane()
    G = jnp.where((lane >= A0) & (lane < A0 + 4), -jnp.exp(alog) * jax.nn.softplus(sm + dtb), 0.0)
    tri, strict = _tril(C), _tril(C, True)
    gcs = mm(tri.astype(F32), G, "nn", "xa")
    sig = jax.nn.sigmoid(sm)
    qn = [act[h] * lax.rsqrt(jnp.sum(act[h] * act[h], axis=1, keepdims=True) + NORM_EPS) * (LANES ** -0.5) for h in H]
    kn = [act[4 + h] * lax.rsqrt(jnp.sum(act[4 + h] * act[4 + h], axis=1, keepdims=True) + NORM_EPS) for h in H]
    beta = [_col(sig, BETA0 + h) for h in H]
    gcol = [_col(gcs, A0 + h) for h in H]
    selr = [((_iota((8, LANES), 0) == 0) & (_iota((8, LANES), 1) == A0 + h)).astype(F32) for h in H]
    grow = [jnp.sum(mm(selr[h], gcs, "nt", "xa"), axis=0, keepdims=True) for h in H]
    gamma = [jnp.exp(jnp.where(tri, gcol[h] - grow[h], -jnp.inf)) for h in H]
    kb = [kn[h] * beta[h] for h in H]
    pk = [-(mm(kb[h], kn[h], "nt", "1") * jnp.where(strict, gamma[h], 0.0)) for h in H]
    eye = (_iota((C, C), 0) == _iota((C, C), 1)).astype(F32)
    minv = [eye + pk[h] for h in H]
    for _ in range(5):
        pk = [mm(pk[h], pk[h], "nn", "3") for h in H]
        minv = [minv[h] + mm(minv[h], pk[h], "nn", "3") for h in H]
    eg = [jnp.exp(gcol[h]) for h in H]
    w = [mm(minv[h], kb[h] * eg[h], "nn", "3") for h in H]
    u = [mm(minv[h], act[8 + h] * beta[h], "nn", "3") for h in H]
    glast = [_last_row(gcol[h]) for h in H]
    vnew = [u[h] - mm(w[h], S[h], "nn", "1") for h in H]
    qk = [mm(qn[h], kn[h], "nt", "1") * gamma[h] for h in H]
    o = [mm(qn[h] * eg[h], S[h], "nn", "1") + mm(qk[h], vnew[h], "nn", "1") for h in H]
    Sn = [S[h] * jnp.exp(glast[h]) + mm(kn[h] * jnp.exp(glast[h] - gcol[h]), vnew[h], "tn", "1") for h in H]
    on = [o[h] * lax.rsqrt(jnp.mean(o[h] * o[h], axis=1, keepdims=True) + NORM_EPS) * nw for h in H]
    return [[on[h] * _silu(gate[h]) for h in H]], Sn


def sg_chunk(xs, ps, S, h):
    (uv,) = xs
    lng, lnb, W, (bt,) = ps
    u = [jax.nn.gelu(p) for p in uv[:4]]
    v = [jax.nn.gelu(p) for p in uv[4:]]
    mu = sum(jnp.sum(p, axis=1, keepdims=True) for p in v) / BRANCH_W
    vc = [p - mu for p in v]
    var = sum(jnp.sum(p * p, axis=1, keepdims=True) for p in vc) / BRANCH_W
    inv = lax.rsqrt(var + LN_EPS)
    trif = _tril(W[0].shape[0]).astype(F32)
    out = []
    for g in range(4):
        vn = vc[g] * inv * lng[g] + lnb[g]
        out.append(u[g] * (mm(W[g] * trif, vn, "nn", "1") + _col(bt, g)))
    return [out], S


def foxc_chunk(xs, ps, S, h):
    (sm,), ((fb,),), (carry,) = xs[0], ps, S
    lane = _lane()
    ls = jnp.where((lane >= FF0) & (lane < FF0 + 8), jax.nn.log_sigmoid(sm + fb), 0.0)
    c = mm(_tril(sm.shape[0]).astype(F32), ls, "nn", "xa") + carry
    return [[c]], [_last_row(c)]


HALO = 8


def _conv_tiles(T, C):
    return min(T, 512), _pick(C, 512)


def conv_fwd(x, w, b, name):
    T, C = x.shape
    tm, cb = _conv_tiles(T, C)

    def body(xp_ref, x_ref, w_ref, b_ref, o_ref):
        i = pl.program_id(1)
        e = jnp.concatenate([xp_ref[...] * (i > 0).astype(F32), x_ref[...]], axis=0)
        pre = b_ref[...] + sum(w_ref[k:k + 1, :] * e[5 + k:5 + k + tm, :] for k in range(4))
        o_ref[...] = _silu(pre)

    hb = tm // HALO
    return _pcall(body, grid=(C // cb, T // tm),
                  in_specs=[pl.BlockSpec((HALO, cb), lambda j, i: (jnp.maximum(i * hb - 1, 0), j)), pl.BlockSpec((tm, cb), lambda j, i: (i, j)),
                            pl.BlockSpec((4, cb), lambda j, i: (0, j)), pl.BlockSpec((1, cb), lambda j, i: (0, j))],
                  out_specs=pl.BlockSpec((tm, cb), lambda j, i: (i, j)), out_shape=_S((T, C)), name=name)(x, x, w, b)


def conv_bwd(x, w, b, dact, name):
    T, C = x.shape
    tm, cb = _conv_tiles(T, C)
    nt = T // tm

    def body(xp_ref, x_ref, xn_ref, w_ref, b_ref, d_ref, dn_ref, dx_ref, dw_ref, db_ref):
        i = pl.program_id(1)
        has_prev, has_next = (i > 0).astype(F32), (i < nt - 1).astype(F32)
        e = jnp.concatenate([xp_ref[...] * has_prev, x_ref[...], xn_ref[...] * has_next], axis=0)
        pre = b_ref[...] + sum(w_ref[k:k + 1, :] * e[5 + k:5 + k + tm + 8, :] for k in range(4))
        de = jnp.concatenate([d_ref[...], dn_ref[...] * has_next], axis=0)
        sg = jax.nn.sigmoid(pre)
        dpre = de * (sg * (1.0 + pre * (1.0 - sg)))
        dx_ref[...] = sum(w_ref[k:k + 1, :] * dpre[3 - k:3 - k + tm, :] for k in range(4))
        dcur = dpre[0:tm, :]
        dw = jnp.concatenate([jnp.sum(dcur * e[5 + k:5 + k + tm, :], axis=0, keepdims=True) for k in range(4)], axis=0)
        db = jnp.sum(dcur, axis=0, keepdims=True)

        @pl.when(i == 0)
        def _():
            dw_ref[...] = dw
            db_ref[...] = db

        @pl.when(i > 0)
        def _():
            dw_ref[...] += dw
            db_ref[...] += db

    blk = lambda f: pl.BlockSpec((tm, cb), f)
    hb = tm // HALO
    before = pl.BlockSpec((HALO, cb), lambda j, i: (jnp.maximum(i * hb - 1, 0), j))
    after = pl.BlockSpec((HALO, cb), lambda j, i: (jnp.minimum((i + 1) * hb, nt * hb - 1), j))
    return _pcall(body, grid=(C // cb, nt),
                  in_specs=[before, blk(lambda j, i: (i, j)), after,
                            pl.BlockSpec((4, cb), lambda j, i: (0, j)), pl.BlockSpec((1, cb), lambda j, i: (0, j)),
                            blk(lambda j, i: (i, j)), after],
                  out_specs=[blk(lambda j, i: (i, j)), pl.BlockSpec((4, cb), lambda j, i: (0, j)), pl.BlockSpec((1, cb), lambda j, i: (0, j))],
                  out_shape=[_S((T, C)), _S((4, C)), _S((1, C))], name=name)(x, x, x, w, b, dact, dact)


FOX_SCALE = 64 ** -0.5
LOG2E = 1.4426950408889634


def _spare(e, i):
    return (_lane() == 64 * (1 - e) + i).astype(F32)


def _lanes_of(e):
    lane = _lane()
    return ((lane < 64) if e == 0 else (lane >= 64)).astype(F32)


def _col3(col, e, first):
    c1 = col.astype(BF).astype(F32)
    c2 = (col - c1).astype(BF).astype(F32)
    c3 = (col - c1 - c2).astype(BF).astype(F32)
    return c1 * _spare(e, first) + c2 * _spare(e, first + 1) + c3 * _spare(e, first + 2)


def _ones3(e, first):
    return _spare(e, first) + _spare(e, first + 1) + _spare(e, first + 2)


def _causal_bias(n):
    return jnp.where(_iota((n, n), 0) >= _iota((n, n), 1), 0.0, -jnp.inf).astype(F32)


def _c_col(cc, hh):
    return jnp.sum(cc * (_lane() == FF0 + hh).astype(F32), axis=1, keepdims=True) * LOG2E


def _pair_spec(tq, row_of):
    return pl.BlockSpec((None, 2, tq, LANES), lambda hp, a, b: (hp, 0, row_of(a, b), 0))


def fox_prep(qkv, ccol, name):
    T = qkv.shape[0]
    tq = min(T, 512)

    def body(q_ref, k_ref, v_ref, cc_ref, qa_ref, ka_ref, va_ref):
        hp = pl.program_id(0)
        q, k, v, cc = q_ref[...], k_ref[...], v_ref[...], cc_ref[...]
        for e in range(2):
            me = _lanes_of(e)
            c2 = _c_col(cc, 2 * hp + e)
            qa_ref[e] = (q * me * (FOX_SCALE * LOG2E) + _col3(c2, e, 0) + _ones3(e, 3)).astype(BF)
            ka_ref[e] = (k * me + _ones3(e, 0) + _col3(-c2, e, 3) + _ones3(e, 6)).astype(BF)
            va_ref[e] = (v * me + (1.0 - me)).astype(BF)

    blk = lambda off: pl.BlockSpec((tq, LANES), lambda hp, i: (i, off + hp))
    out = pl.BlockSpec((None, 2, tq, LANES), lambda hp, i: (hp, 0, i, 0))
    return _pcall(body, grid=(4, T // tq), in_specs=[blk(0), blk(4), blk(8), pl.BlockSpec((tq, LANES), lambda hp, i: (i, 0))],
                  out_specs=[out] * 3, out_shape=[_S((4, 2, T, LANES), BF)] * 3, name=name)(qkv, qkv, qkv, ccol)


def fox_fwd(qa, ka, va, name, rider=None):
    T = qa.shape[2]
    tq = min(T, 512)
    nq = T // tq

    def body(qa_ref, ka_ref, va_ref, o_ref, lse_ref, m_s, acc, causal):
        i, j = pl.program_id(1), pl.program_id(2)

        @pl.when((pl.program_id(0) == 0) & (i == 0) & (j == 0))
        def _():
            causal[...] = _causal_bias(tq)

        @pl.when(j == 0)
        def _():
            m_s[...] = jnp.full(m_s.shape, -jnp.inf, F32)
            acc[...] = jnp.zeros(acc.shape, F32)

        def step(diagonal):
            for e in range(2):
                s = _dotb(qa_ref[e], ka_ref[e], NT)
                if diagonal:
                    s = s + causal[...]
                m_old = m_s[e]
                m_new = jnp.maximum(m_old, jnp.max(s, axis=1, keepdims=True))
                p = jnp.exp2(s - m_new)
                m_s[e] = m_new
                acc[e] = acc[e] * jnp.exp2(m_old - m_new) + _dotb(p.astype(BF), va_ref[e], NN)

        @pl.when(j < i)
        def _():
            step(False)

        @pl.when(j == i)
        def _():
            step(True)
            lane = _lane()
            o, lse = 0.0, 0.0
            for e in range(2):
                l = jnp.sum(acc[e] * _spare(e, 0), axis=1, keepdims=True)
                o = o + acc[e] * _lanes_of(e) / l
                lse = lse + (m_s[e] + jnp.log2(l)) * (lane == e).astype(F32)
            o_ref[...] = o
            lse_ref[...] = lse

    kv = _pair_spec(tq, lambda i, j: jnp.minimum(j, i))
    return hosted_call(body, rider, grid=(4, nq, nq), in_specs=[_pair_spec(tq, lambda i, j: i), kv, kv],
                       out_specs=[pl.BlockSpec((tq, LANES), lambda hp, i, j: (i, hp)), pl.BlockSpec((None, tq, LANES), lambda hp, i, j: (hp, i, 0))],
                       out_shape=[_S((T, BRANCH_W)), _S((4, T, LANES))],
                       scratch_shapes=[pltpu.VMEM((2, tq, 1), F32), pltpu.VMEM((2, tq, LANES), F32), pltpu.VMEM((tq, tq), F32)],
                       name=name, args=[qa, ka, va])


def fox_prep_bwd(qkv, qa, o, lse, do, name):
    T = qkv.shape[0]
    tq = min(T, 512)

    def body(q_ref, k_ref, qa_ref, o_ref, lse_ref, do_ref, qb_ref, doa_ref, qd_ref, kd_ref):
        q, k, dov = q_ref[...], k_ref[...], do_ref[...]
        dd = dov * o_ref[...]
        lane = _lane()
        for e in range(2):
            me = _lanes_of(e)
            lse_e = jnp.sum(lse_ref[...] * (lane == e).astype(F32), axis=1, keepdims=True)
            qb_ref[e] = (qa_ref[e].astype(F32) + _col3(-lse_e, e, 6)).astype(BF)
            doa_ref[e] = (dov * me + _col3(-jnp.sum(dd * me, axis=1, keepdims=True), e, 0)).astype(BF)
            qd_ref[e] = (q * me * FOX_SCALE + _spare(e, 0)).astype(BF)
            kd_ref[e] = (k * me * FOX_SCALE + _spare(e, 0)).astype(BF)

    blk = lambda off: pl.BlockSpec((tq, LANES), lambda hp, i: (i, off + hp))
    pair = pl.BlockSpec((None, 2, tq, LANES), lambda hp, i: (hp, 0, i, 0))
    return _pcall(body, grid=(4, T // tq),
                  in_specs=[blk(0), blk(4), pair, blk(0), pl.BlockSpec((None, tq, LANES), lambda hp, i: (hp, i, 0)), blk(0)],
                  out_specs=[pair] * 4, out_shape=[_S((4, 2, T, LANES), BF)] * 4, name=name)(qkv, qkv, qa, o, lse, do)


def fox_bwd(qb, ka, va, doa, qd, kd, name, rider=None):
    T = qb.shape[2]
    tq = min(T, 512)
    nq = T // tq

    def body(qb_ref, ka_ref, va_ref, doa_ref, qd_ref, kd_ref, dq_ref, dk_ref, dv_ref, dcc_ref, dq_s, dk_s, dv_s, causal):
        hp, j, ii = pl.program_id(0), pl.program_id(1), pl.program_id(2)
        i = jnp.maximum(ii, j)

        @pl.when((hp == 0) & (j == 0) & (ii == 0))
        def _():
            dcc_ref[...] = jnp.zeros(dcc_ref.shape, F32)
            causal[...] = _causal_bias(tq)

        @pl.when((j == 0) & (ii == 0))
        def _():
            dq_s[...] = jnp.zeros(dq_s.shape, F32)

        @pl.when(ii == 0)
        def _():
            dk_s[...] = jnp.zeros(dk_s.shape, F32)
            dv_s[...] = jnp.zeros(dv_s.shape, F32)

        def step(diagonal):
            rows = pl.ds(pl.multiple_of(i * tq, tq), tq)
            for e in range(2):
                s = _dotb(qb_ref[e], ka_ref[e], NT)
                if diagonal:
                    s = s + causal[...]
                p = jnp.exp2(s)
                ds = (p * _dotb(doa_ref[e], va_ref[e], NT)).astype(BF)
                dv_s[e] += _dotb(p.astype(BF), doa_ref[e], TN)
                dq_s[e, rows, :] += _dotb(ds, kd_ref[e], NN)
                dk_s[e] += _dotb(ds, qd_ref[e], TN)

        @pl.when(ii > j)
        def _():
            step(False)

        @pl.when(ii == j)
        def _():
            step(True)

        def fold(acc, sign):
            grad, dc = 0.0, 0.0
            for e in range(2):
                a = acc[e]
                grad = grad + a * _lanes_of(e)
                dc = dc + sign * jnp.sum(a * _spare(e, 0), axis=1, keepdims=True) * (_lane() == FF0 + 2 * hp + e).astype(F32)
            return grad, dc

        @pl.when(ii == nq - 1)
        def _():
            grad, dc = fold(dk_s, -1.0)
            dk_ref[...] = grad
            dv_ref[...] = dv_s[0] * _lanes_of(0) + dv_s[1] * _lanes_of(1)
            dcc_ref[pl.ds(pl.multiple_of(j * tq, tq), tq), :] += dc

        @pl.when((j == nq - 1) & (ii == nq - 1))
        def _():
            grad, dc = fold(dq_s, 1.0)
            dq_ref[...] = grad
            dcc_ref[...] += dc

    irow, jrow = _pair_spec(tq, lambda j, ii: jnp.maximum(ii, j)), _pair_spec(tq, lambda j, ii: j)
    jout = pl.BlockSpec((tq, LANES), lambda hp, j, ii: (j, hp))
    return hosted_call(body, rider, grid=(4, nq, nq), in_specs=[irow, jrow, jrow, irow, irow, jrow],
                       out_specs=[pl.BlockSpec((T, LANES), lambda hp, j, ii: (0, hp)), jout, jout, pl.BlockSpec((T, LANES), lambda hp, j, ii: (0, 0))],
                       out_shape=[_S((T, BRANCH_W)), _S((T, BRANCH_W)), _S((T, BRANCH_W)), _S((T, LANES))],
                       scratch_shapes=[pltpu.VMEM((2, T, LANES), F32), pltpu.VMEM((2, tq, LANES), F32), pltpu.VMEM((2, tq, LANES), F32),
                                       pltpu.VMEM((tq, tq), F32)],
                       name=name, args=[qb, ka, va, doa, qd, kd])


def _acc_out(ref, val, first):
    @pl.when(first)
    def _():
        ref[...] = val

    @pl.when(jnp.logical_not(first))
    def _():
        ref[...] += val


def _row(tm, c):
    return pl.BlockSpec((tm, c), lambda i: (i, 0))


def _full(shape):
    return pl.BlockSpec(shape, lambda *_: (0,) * len(shape))


def ln_fwd(x, g, b, name):
    T, C = x.shape
    tm = min(T, 512)

    def body(x_ref, g_ref, b_ref, o_ref):
        o_ref[...] = _ln(x_ref[...], g_ref[...], b_ref[...])

    return _pcall(body, grid=(T // tm,), in_specs=[_row(tm, C), _full((1, C)), _full((1, C))], out_specs=_row(tm, C),
                  out_shape=_S((T, C)), name=name)(x, g, b)


def ln_bwd(x, g, b, dy, name):
    T, C = x.shape
    tm = min(T, 512)

    def body(x_ref, g_ref, b_ref, dy_ref, dx_ref, dg_ref, db_ref):
        _, vjp = jax.vjp(_ln, x_ref[...], g_ref[...], b_ref[...])
        dx, dg, db = vjp(dy_ref[...])
        dx_ref[...] = dx
        first = pl.program_id(0) == 0
        _acc_out(dg_ref, dg, first)
        _acc_out(db_ref, db, first)

    return _pcall(body, grid=(T // tm,), in_specs=[_row(tm, C), _full((1, C)), _full((1, C)), _row(tm, C)],
                  out_specs=[_row(tm, C), _full((1, C)), _full((1, C))], out_shape=[_S((T, C)), _S((1, C)), _S((1, C))], name=name)(x, g, b, dy)


def loss_head(h, target, name):
    T, C = h.shape
    tm = min(T, 512)

    def body(h_ref, t_ref, d_ref, l_ref):
        e = h_ref[...] - t_ref[...]
        d_ref[...] = e * (1.0 / C)
        part = jnp.sum(jnp.sum(e * e, axis=1, keepdims=True), axis=0, keepdims=True) * (0.5 / C)
        _acc_out(l_ref, part, pl.program_id(0) == 0)

    return _pcall(body, grid=(T // tm,), in_specs=[_row(tm, C), _row(tm, C)], out_specs=[_row(tm, C), _full((1, 1))],
                  out_shape=[_S((T, C)), _S((1, 1))], name=name)(h, target)


def add3(a, b, c, name):
    T, C = a.shape
    tm = min(T, 512)

    def body(a_ref, b_ref, c_ref, o_ref):
        o_ref[...] = a_ref[...] + b_ref[...] + c_ref[...]

    return _pcall(body, grid=(T // tm,), in_specs=[_row(tm, C)] * 3, out_specs=_row(tm, C), out_shape=_S((T, C)), name=name)(a, b, c)


def _wb_spec(l):
    return pl.BlockSpec((None, 4, BRANCH_W, D_MODEL), lambda *_: (l, 0, 0, 0))


def merge_fwd(ys, gl, gb, wb, l, name):
    T = gl.shape[0]
    tm = min(T, 256)

    def body(y0, y1, y2, y3, gl_ref, gb_ref, wb_ref, o_ref):
        acc = 0.0
        for i, y in enumerate((y0, y1, y2, y3)):
            z = _dotb(y[...].astype(BF), wb_ref[i], NN)
            g = jax.nn.sigmoid(gl_ref[:, i * D_MODEL:(i + 1) * D_MODEL] + gb_ref[i:i + 1, :])
            acc = acc + g * z
        o_ref[...] = acc

    return _pcall(body, grid=(T // tm,), in_specs=[_row(tm, BRANCH_W)] * 4 + [_row(tm, 4 * D_MODEL), _full((4, D_MODEL)), _wb_spec(l)],
                  out_specs=_row(tm, D_MODEL), out_shape=_S((T, D_MODEL)), name=name)(*ys, gl, gb, wb)


def merge_bwd(ys, gl, gb, wb, l, dm, name):
    T = gl.shape[0]
    tm = min(T, 256)

    def body(y0, y1, y2, y3, gl_ref, gb_ref, wb_ref, dm_ref, d0, d1, d2, d3, dgl_ref, dz_ref, dgb_ref):
        dmv = dm_ref[...]
        first = pl.program_id(0) == 0
        for i, (y, d) in enumerate(zip((y0, y1, y2, y3), (d0, d1, d2, d3))):
            cols = slice(i * D_MODEL, (i + 1) * D_MODEL)
            z = _dotb(y[...].astype(BF), wb_ref[i], NN)
            g = jax.nn.sigmoid(gl_ref[:, cols] + gb_ref[i:i + 1, :])
            dgl = dmv * z * (g * (1.0 - g))
            dz = (g * dmv).astype(BF)
            dgl_ref[:, cols] = dgl
            dz_ref[:, cols] = dz
            d[...] = _dotb(dz, wb_ref[i], NT)
            _acc_out(dgb_ref.at[i:i + 1, :], jnp.sum(dgl, axis=0, keepdims=True), first)

    return _pcall(body, grid=(T // tm,),
                  in_specs=[_row(tm, BRANCH_W)] * 4 + [_row(tm, 4 * D_MODEL), _full((4, D_MODEL)), _wb_spec(l), _row(tm, D_MODEL)],
                  out_specs=[_row(tm, BRANCH_W)] * 4 + [_row(tm, 4 * D_MODEL), _row(tm, 4 * D_MODEL), _full((4, D_MODEL))],
                  out_shape=[_S((T, BRANCH_W))] * 4 + [_S((T, 4 * D_MODEL)), _S((T, 4 * D_MODEL), BF), _S((4, D_MODEL))], name=name)(
                      *ys, gl, gb, wb, dm)


def _wout_spec(l):
    return pl.BlockSpec((None, D_MODEL, D_MODEL), lambda *_: (l, 0, 0))


def out_fwd(merged, h, wout, l, g, b, name):
    T = h.shape[0]
    tm = min(T, 512)

    def body(m_ref, h_ref, w_ref, g_ref, b_ref, u_ref, o_ref):
        u = ALPHA * h_ref[...] + _dotb(m_ref[...].astype(BF), w_ref[...], NN)
        u_ref[...] = u
        o_ref[...] = _ln(u, g_ref[...], b_ref[...])

    C = D_MODEL
    return _pcall(body, grid=(T // tm,), in_specs=[_row(tm, C), _row(tm, C), _wout_spec(l), _full((1, C)), _full((1, C))],
                  out_specs=[_row(tm, C), _row(tm, C)], out_shape=[_S((T, C)), _S((T, C))], name=name)(merged, h, wout, g, b)


def out_bwd(u, dy, g, b, wout, l, name):
    T, C = u.shape
    tm = min(T, 512)

    def body(u_ref, dy_ref, g_ref, b_ref, w_ref, du_ref, dm_ref, dg_ref, db_ref):
        _, vjp = jax.vjp(_ln, u_ref[...], g_ref[...], b_ref[...])
        du, dg, db = vjp(dy_ref[...])
        du_ref[...] = du
        dm_ref[...] = _dotb(du.astype(BF), w_ref[...], NT)
        first = pl.program_id(0) == 0
        _acc_out(dg_ref, dg, first)
        _acc_out(db_ref, db, first)

    return _pcall(body, grid=(T // tm,), in_specs=[_row(tm, C), _row(tm, C), _full((1, C)), _full((1, C)), _wout_spec(l)],
                  out_specs=[_row(tm, C), _row(tm, C), _full((1, C)), _full((1, C))],
                  out_shape=[_S((T, C)), _S((T, C)), _S((1, C)), _S((1, C))], name=name)(u, dy, g, b, wout)


def ff_fwd(h, wup, wdown, l, g, b, name):
    T, C = h.shape
    F = wup.shape[2]
    tm, tf = min(T, 512), 1024
    nf = F // tf

    def body(h_ref, wu_ref, wd_ref, g_ref, b_ref, u_ref, o_ref, acc):
        f = pl.program_id(1)
        a = _dotb(h_ref[...].astype(BF), wu_ref[...], NN)
        r = jnp.square(jnp.maximum(a, 0.0))
        p = _dotb(r.astype(BF), wd_ref[...], NN)
        _acc_out(acc, p, f == 0)

        @pl.when(f == nf - 1)
        def _():
            u = ALPHA * h_ref[...] + acc[...]
            u_ref[...] = u
            o_ref[...] = _ln(u, g_ref[...], b_ref[...])

    row = pl.BlockSpec((tm, C), lambda i, f: (i, 0))
    return _pcall(body, grid=(T // tm, nf),
                  in_specs=[row, pl.BlockSpec((None, C, tf), lambda i, f: (l, 0, f)), pl.BlockSpec((None, tf, C), lambda i, f: (l, f, 0)),
                            _full((1, C)), _full((1, C))],
                  out_specs=[row, row], out_shape=[_S((T, C)), _S((T, C))], scratch_shapes=[pltpu.VMEM((tm, C), F32)], name=name)(h, wup, wdown, g, b)


def ff_bwd(u, dy, h, g, b, wup, wdown, l, name):
    T, C = h.shape
    F = wup.shape[2]
    tm, tf = min(T, 512), 1024
    nf = F // tf

    def body(u_ref, dy_ref, h_ref, g_ref, b_ref, wu_ref, wd_ref, du_ref, dh_ref, da_ref, r_ref, dg_ref, db_ref, du_s, acc):
        i, f = pl.program_id(0), pl.program_id(1)

        @pl.when(f == 0)
        def _():
            _, vjp = jax.vjp(_ln, u_ref[...], g_ref[...], b_ref[...])
            du, dg, db = vjp(dy_ref[...])
            du_s[...] = du
            du_ref[...] = du
            _acc_out(dg_ref, dg, i == 0)
            _acc_out(db_ref, db, i == 0)

        a = _dotb(h_ref[...].astype(BF), wu_ref[...], NN)
        ap = jnp.maximum(a, 0.0)
        dr = _dotb(du_s[...].astype(BF), wd_ref[...], NT)
        da = (dr * (2.0 * ap)).astype(BF)
        da_ref[...] = da
        r_ref[...] = jnp.square(ap).T.astype(BF)
        _acc_out(acc, _dotb(da, wu_ref[...], NT), f == 0)

        @pl.when(f == nf - 1)
        def _():
            dh_ref[...] = ALPHA * du_s[...] + acc[...]

    row = pl.BlockSpec((tm, C), lambda i, f: (i, 0))
    colf = pl.BlockSpec((tm, tf), lambda i, f: (i, f))
    return _pcall(body, grid=(T // tm, nf),
                  in_specs=[row, row, row, _full((1, C)), _full((1, C)), pl.BlockSpec((None, C, tf), lambda i, f: (l, 0, f)),
                            pl.BlockSpec((None, tf, C), lambda i, f: (l, f, 0))],
                  out_specs=[row, row, colf, pl.BlockSpec((tf, tm), lambda i, f: (f, i)), _full((1, C)), _full((1, C))],
                  out_shape=[_S((T, C)), _S((T, C)), _S((T, F), BF), _S((F, T), BF), _S((1, C)), _S((1, C))],
                  scratch_shapes=[pltpu.VMEM((tm, C), F32), pltpu.VMEM((tm, C), F32)], name=name)(u, dy, h, g, b, wup, wdown)


MESH_ID = pl.DeviceIdType.MESH
_ANY = pl.BlockSpec(memory_space=pl.ANY)


def _window(ref, ax, idx, n):
    if n < 0:
        return ref
    sel = idx if n == 0 else pl.ds(pl.multiple_of(idx * n, n), n)
    return ref.at[(slice(None),) * ax + (sel,)]


Rider = collections.namedtuple("Rider", "operands out_shape scratch start wait")


def hosted_call(body, rider, *, grid, in_specs, out_specs, out_shape, scratch_shapes, name, args):
    n_in, n_out, n_scr = len(in_specs), len(out_specs), len(scratch_shapes)
    if rider is None:
        return _pcall(body, grid=grid, in_specs=in_specs, out_specs=out_specs, out_shape=out_shape, scratch_shapes=scratch_shapes, name=name)(*args), []
    ri, ro = len(rider.operands), len(rider.out_shape)

    def wrapped(*refs):
        ins, r_in = refs[:n_in], refs[n_in:n_in + ri]
        o0 = n_in + ri
        outs, r_out = refs[o0:o0 + n_out], refs[o0 + n_out:o0 + n_out + ro]
        s0 = o0 + n_out + ro
        scr, r_scr = refs[s0:s0 + n_scr], refs[s0 + n_scr:]
        ids = [pl.program_id(i) for i in range(len(grid))]
        first = functools.reduce(jnp.logical_and, [i == 0 for i in ids])
        last = functools.reduce(jnp.logical_and, [i == g - 1 for i, g in zip(ids, grid)])

        @pl.when(first)
        def _():
            rider.start(r_in, r_out, r_scr)

        body(*ins, *outs, *scr)

        @pl.when(last)
        def _():
            rider.wait(r_in, r_out, r_scr)

    res = _pcall(wrapped, grid=grid, in_specs=list(in_specs) + [_ANY] * ri, out_specs=list(out_specs) + [_ANY] * ro,
                 out_shape=list(out_shape) + list(rider.out_shape), scratch_shapes=list(scratch_shapes) + list(rider.scratch),
                 name=name)(*args, *rider.operands)
    return res[:n_out], res[n_out:]


def comm_call(rider, name):
    ri = len(rider.operands)

    def body(*refs):
        r_in, r_out, r_scr = refs[:ri], refs[ri:ri + len(rider.out_shape)], refs[ri + len(rider.out_shape):]
        rider.start(r_in, r_out, r_scr)
        rider.wait(r_in, r_out, r_scr)

    return _pcall(body, in_specs=[_ANY] * ri, out_specs=[_ANY] * len(rider.out_shape), out_shape=list(rider.out_shape),
                  scratch_shapes=list(rider.scratch), name=name)(*rider.operands)


def gather_rider(shards, axes):
    K = len(shards)
    widths = [s.shape[a] for s, a in zip(shards, axes)]
    out_shape = [_S(s.shape[:a] + (N_DEV * s.shape[a],) + s.shape[a + 1:], s.dtype) for s, a in zip(shards, axes)]

    def plan(x_refs, o_refs, sems):
        send_sems, recv_sems, local_sems = sems
        mx, my, mc = lax.axis_index("x"), lax.axis_index("y"), lax.axis_index("c")
        me, sibling = (mx, my, mc), (mx, my, 1 - mc)
        chips = [(1 - mx, my), (mx, 1 - my), (1 - mx, 1 - my)]

        def win(k, px, py, pc):
            return _window(o_refs[k], axes[k], 4 * px + 2 * py + pc, widths[k])

        def copy(k, slot, block, to, src=None):
            return pltpu.make_async_remote_copy(src_ref=win(k, *block) if src is None else src, dst_ref=win(k, *block),
                                                send_sem=send_sems.at[7 * k + slot], recv_sem=recv_sems.at[7 * k + slot],
                                                device_id=to, device_id_type=MESH_ID)

        mine = [pltpu.make_async_copy(x_refs[k], win(k, *me), local_sems.at[k]) for k in range(K)]
        first = []
        for k in range(K):
            first.append(copy(k, 0, me, sibling, src=x_refs[k]))
            first += [copy(k, 1 + j, me, (*chip, mc), src=x_refs[k]) for j, chip in enumerate(chips)]
        return me, sibling, chips, copy, mine, first

    def start(x_refs, o_refs, sems):
        _, _, _, _, mine, first = plan(x_refs, o_refs, sems)
        for cp in mine + first:
            cp.start()

    def wait(x_refs, o_refs, sems):
        me, sibling, chips, copy, mine, first = plan(x_refs, o_refs, sems)
        mc = me[2]
        passed = []
        for j, chip in enumerate(chips):
            for k in range(K):
                copy(k, 1 + j, (*chip, mc), me).wait_recv()
                passed.append(copy(k, 4 + j, (*chip, mc), sibling))
                passed[-1].start()
        for k in range(K):
            copy(k, 0, sibling, me).wait_recv()
        for j, chip in enumerate(chips):
            for k in range(K):
                copy(k, 4 + j, (*chip, 1 - mc), me).wait_recv()
        for cp in first + passed:
            cp.wait_send()
        for cp in mine:
            cp.wait()

    scratch = [pltpu.SemaphoreType.DMA((7 * K,)), pltpu.SemaphoreType.DMA((7 * K,)), pltpu.SemaphoreType.DMA((K,))]
    return Rider(list(shards), out_shape, scratch, start, wait)


def exchange_rider(items):
    ns = len(items)
    out_shape = [_S((N_DEV,) + tuple(it[3]), it[0].dtype) for it in items]

    def plan(src_refs, o_refs, sems):
        send_sems, recv_sems, local_sems = sems
        mx, my, mc = lax.axis_index("x"), lax.axis_index("y"), lax.axis_index("c")
        me = 4 * mx + 2 * my + mc
        remote, own = [], []
        for s, (_, ax, n, _) in enumerate(items):
            own.append(pltpu.make_async_copy(_window(src_refs[s], ax, me, n), o_refs[s].at[me], local_sems.at[s]))
            for k in range(1, N_DEV):
                px = 1 - mx if k & 4 else mx
                py = 1 - my if k & 2 else my
                pc = 1 - mc if k & 1 else mc
                remote.append(pltpu.make_async_remote_copy(
                    src_ref=_window(src_refs[s], ax, 4 * px + 2 * py + pc, n), dst_ref=o_refs[s].at[me],
                    send_sem=send_sems.at[7 * s + k - 1], recv_sem=recv_sems.at[7 * s + k - 1],
                    device_id=(px, py, pc), device_id_type=MESH_ID))
        return remote, own

    def start(src_refs, o_refs, sems):
        remote, own = plan(src_refs, o_refs, sems)
        for cp in own + remote:
            cp.start()

    def wait(src_refs, o_refs, sems):
        remote, own = plan(src_refs, o_refs, sems)
        for cp in remote + own:
            cp.wait()

    scratch = [pltpu.SemaphoreType.DMA((7 * ns,)), pltpu.SemaphoreType.DMA((7 * ns,)), pltpu.SemaphoreType.DMA((ns,))]
    return Rider([it[0] for it in items], out_shape, scratch, start, wait)


def reduce_adamw(rcvs, w, m, v, name):
    L = len(rcvs)
    _, A, B, C = rcvs[0].shape
    tb = B
    while tb > 8 and tb * C > (1 << 17):
        tb //= 2

    def body(*refs):
        r_refs, (w_ref, m_ref, v_ref, g_ref, d_ref, mo_ref, vo_ref) = refs[:L], refs[L:]
        for k in range(L):
            @pl.when(pl.program_id(0) == k)
            def _(k=k):
                g = r_refs[k][0].astype(F32)
                for d in range(1, N_DEV):
                    g = g + r_refs[k][d].astype(F32)
                mn = ADAM_B1 * m_ref[...] + (1.0 - ADAM_B1) * g
                vn = ADAM_B2 * v_ref[...] + (1.0 - ADAM_B2) * jnp.square(g)
                m_hat = mn / (1.0 - ADAM_B1 ** ADAM_STEP)
                v_hat = vn / (1.0 - ADAM_B2 ** ADAM_STEP)
                g_ref[...] = g
                d_ref[...] = -ADAM_LR * (m_hat / (jnp.sqrt(v_hat) + ADAM_EPS) + ADAM_WD * w_ref[...])
                mo_ref[...] = mn
                vo_ref[...] = vn

    def rspec(k):
        return pl.BlockSpec((N_DEV, None, tb, C), lambda l, a, i: (0, jnp.where(l == k, a, 0), jnp.where(l == k, i, 0), 0))

    blk = pl.BlockSpec((None, tb, C), lambda l, a, i: (l * A + a, i, 0))
    return _pcall(body, grid=(L, A, B // tb), in_specs=[rspec(k) for k in range(L)] + [blk, blk, blk],
                  out_specs=[blk] * 4, out_shape=[_S((L * A, B, C))] * 4, name=name)(*rcvs, w, m, v)


def _w_in_pieces(g0, g1):
    per = D_IN // N_DEV
    return [(d, max(g0, d * per) - d * per, min(g1, (d + 1) * per) - d * per) for d in range(N_DEV) if max(g0, d * per) < min(g1, (d + 1) * per)]


def repack_w_in(w8, name):
    _, L, R, per = w8.shape
    tr = 256

    def cols(x_ref, g0, g1):
        return [x_ref[d, :, a:b] for d, a, b in _w_in_pieces(g0, g1)]

    def body(x_ref, *o_refs):
        for (name_, i), o_ref in zip(SEGS, o_refs):
            o_ref[...] = jnp.concatenate(cols(x_ref, _OFF[i], _OFF[i + 1]), axis=1)
        parts, at = [], 0
        for i, lane0 in SMALL_SRC:
            assert lane0 == at
            parts += cols(x_ref, _OFF[i], _OFF[i + 1])
            at += IN_SIZES[i]
        parts.append(jnp.zeros((tr, LANES - at), w8.dtype))
        o_refs[-1][...] = jnp.concatenate(parts, axis=1)

    widths = [IN_SIZES[i] for _, i in SEGS] + [LANES]
    outs = _pcall(body, grid=(L, R // tr), in_specs=[pl.BlockSpec((N_DEV, None, tr, per), lambda l, r: (0, l, r, 0))],
                  out_specs=[pl.BlockSpec((None, tr, w), lambda l, r: (l, r, 0)) for w in widths],
                  out_shape=[_S((L, R, w), w8.dtype) for w in widths], name=name)(w8)
    return dict(zip(SEG_NAMES, outs))


def repack_dw_in(dseg, name):
    R = dseg["z"].shape[0]
    per = D_IN // N_DEV
    tr = 128
    src = {i: (k, 0) for k, (_, i) in enumerate(SEGS)}
    src.update({i: (len(SEGS), lane0) for i, lane0 in SMALL_SRC})

    def body(*refs):
        s_refs, o_ref = refs[:-1], refs[-1]
        for d in range(N_DEV):
            parts = []
            for i in range(len(IN_SIZES)):
                g0, g1 = max(_OFF[i], d * per), min(_OFF[i + 1], (d + 1) * per)
                if g0 < g1:
                    k, c0 = src[i]
                    parts.append(s_refs[k][:, c0 + g0 - _OFF[i]:c0 + g1 - _OFF[i]])
            o_ref[d] = jnp.concatenate(parts, axis=1)

    arrs = [dseg[n] for n in SEG_NAMES]
    return _pcall(body, grid=(R // tr,), in_specs=[pl.BlockSpec((tr, a.shape[1]), lambda r: (r, 0)) for a in arrs],
                  out_specs=pl.BlockSpec((N_DEV, tr, per), lambda r: (0, r, 0)), out_shape=_S((N_DEV, R, per), arrs[0].dtype), name=name)(*arrs)


WEIGHTS = ("ln_in_g", "ln_in_b", "w_in", "ssd_conv_w", "ssd_conv_b", "ssd_dt_bias", "ssd_a_log", "ssd_d", "ssd_norm_w", "dn_conv_w",
           "dn_a_log", "dn_dt_bias", "dn_norm_w", "sg_ln_g", "sg_ln_b", "sg_w", "sg_b", "fox_f_bias", "gate_b", "w_branch", "w_out",
           "ln1_g", "ln1_b", "w_up", "w_down", "ln2_g", "ln2_b")
SHARDED = {"w_in": 2, "ssd_conv_w": 2, "dn_conv_w": 2, "gate_b": 2, "w_branch": 3, "w_out": 1, "w_up": 2, "w_down": 1}
SLABBED = ("w_in", "dn_conv_w")
MATMUL_WEIGHTS = ("w_in", "w_branch", "w_out", "w_up", "w_down")
REPLICATED_ENTRY = ("ln_in_g", "ln_in_b")
REPLICATED_LAYER = tuple(n for n in WEIGHTS if n not in SHARDED and n not in REPLICATED_ENTRY)
SEG_NAMES = tuple(n for n, _ in SEGS) + ("small",)
PACK_COLS = 1024


def _lanes(vec, off):
    return jnp.pad(vec, (off, LANES - off - vec.shape[0]))[None]


def _pack_small(parts, row_mult):
    flat = jnp.concatenate([q.reshape(-1) for q in parts])
    rows = -(-flat.shape[0] // (PACK_COLS * row_mult)) * row_mult
    return jnp.pad(flat, (0, rows * PACK_COLS - flat.shape[0])).reshape(1, rows, PACK_COLS)


EARLY = ("w_branch", "w_out", "w_up", "w_down", "gate_b")
LATE = ("w_in", "ssd_conv_w", "dn_conv_w")


def _gather_rider(p, l, names):
    shards, axes = [], []
    for n in names:
        s = p[n][l:l + 1]
        s = s.astype(BF) if n in MATMUL_WEIGHTS else s
        shards.append(s[None] if n in SLABBED else s)
        axes.append(0 if n in SLABBED else SHARDED[n])
    return gather_rider(shards, axes)


def _exchange_items(g, p, names):
    items = []
    for n in names:
        local = p[n].shape[1:]
        items.append((g[n], 0, 0, local) if n in SLABBED else (g[n], SHARDED[n] - 1, local[SHARDED[n] - 1], local))
    return items


def _use_gathered(w, names, arrays, l):
    for n, arr in zip(names, arrays):
        if n == "w_in":
            w[n] = repack_w_in(arr, f"w_in_repack_{l}")
        elif n == "ssd_conv_w":
            w["ssd_cw"] = arr[0]
        elif n == "dn_conv_w":
            w["dn_cw"] = jnp.moveaxis(arr[:, 0], 0, 1).reshape(4, 3 * BRANCH_W)
        elif n == "gate_b":
            w[n] = arr[0]
        else:
            w[n] = arr


def _layer_weights(p, l):
    w = {}
    w["ssd_cb"] = p["ssd_conv_b"][l][None]
    w["dn_cb"] = jnp.zeros((1, 3 * BRANCH_W), F32)
    w["ssd_ps"] = [_lanes(p["ssd_dt_bias"][l], DT0), _lanes(p["ssd_a_log"][l], DT0), _lanes(p["ssd_d"][l], DT0), p["ssd_norm_w"][l][None]]
    w["dn_ps"] = [_lanes(p["dn_a_log"][l], A0), _lanes(p["dn_dt_bias"][l], A0), p["dn_norm_w"][l][None]]
    w["sg_ps"] = [p["sg_ln_g"][l][None], p["sg_ln_b"][l][None], p["sg_w"][l], jnp.pad(p["sg_b"][l].T, ((0, 0), (0, LANES - 4)))]
    w["fox_ps"] = [_lanes(p["fox_f_bias"][l], FF0)]
    for n in ("ln1_g", "ln1_b", "ln2_g", "ln2_b"):
        w[n] = p[n][l][None]
    return w


def _scan_specs(T, a):
    c0 = lambda c, h: (c, 0)
    ssd = dict(f=ssd_chunk, xs=[(a["z"], (128, 512), c0), (a["xbc_act"], (128, 1024), c0), (a["small"], (128, LANES), c0)],
               ys=[((T, BRANCH_W), (128, BRANCH_W), c0)], state=(4, LANES, LANES), nc=T // 128, nh=1, shared=())
    dn = dict(f=dn_chunk, xs=[(a["dn_act"], (64, 3 * BRANCH_W), c0), (a["dngate"], (64, BRANCH_W), c0), (a["small"], (64, LANES), c0)],
              ys=[((T, BRANCH_W), (64, BRANCH_W), c0)], state=(4, LANES, LANES), nc=T // 64, nh=1, shared=())
    sg = dict(f=sg_chunk, xs=[(a["sguv"], (128, 1024), c0)], ys=[((T, BRANCH_W), (128, BRANCH_W), c0)], state=(1, 8, LANES), nc=T // 128, nh=1, shared=())
    fc = dict(f=foxc_chunk, xs=[(a["small"], (128, LANES), c0)],
              ys=[((T, LANES), (128, LANES), c0)], state=(1, 1, LANES), nc=T // 128, nh=1, shared=())
    return ssd, dn, sg, fc


def _layer_fwd(h, w, l, dn_rider=None, fox_rider=None):
    T = h.shape[0]
    a = {"h": h}
    for n in SEG_NAMES:
        a[n] = matmul_w(h, w["w_in"][n], 0, "nn", f"proj_{n}_{l}")
    a["xbc_act"] = conv_fwd(a["xbc"], w["ssd_cw"], w["ssd_cb"], f"ssd_conv_{l}")
    a["dn_act"] = conv_fwd(a["dnqkv"], w["dn_cw"], w["dn_cb"], f"dn_conv_{l}")
    ssd, dn, sg, fc = _scan_specs(T, a)
    (a["ya"], a["ssd_st"]), _ = scan_fwd(f"ssd_fwd_{l}", ssd["f"], ssd["xs"], w["ssd_ps"], ssd["ys"], ssd["state"], ssd["nc"], ssd["nh"])
    (a["yb"], a["dn_st"]), got = scan_fwd(f"dn_fwd_{l}", dn["f"], dn["xs"], w["dn_ps"], dn["ys"], dn["state"], dn["nc"], dn["nh"], rider=dn_rider)
    _use_gathered(w, EARLY, got, l)
    (a["yc"], a["sg_st"]), _ = scan_fwd(f"sg_fwd_{l}", sg["f"], sg["xs"], w["sg_ps"], sg["ys"], sg["state"], sg["nc"], sg["nh"])
    (a["ccol"], a["fc_st"]), _ = scan_fwd(f"foxc_fwd_{l}", fc["f"], fc["xs"], w["fox_ps"], fc["ys"], fc["state"], fc["nc"], fc["nh"])
    a["fox_qa"], a["fox_ka"], a["fox_va"] = fox_prep(a["foxqkv"], a["ccol"], f"fox_prep_{l}")
    (a["yd"], a["lse"]), carried = fox_fwd(a["fox_qa"], a["fox_ka"], a["fox_va"], f"fox_fwd_{l}", rider=fox_rider)
    a["merged"] = merge_fwd([a["ya"], a["yb"], a["yc"], a["yd"]], a["gates"], w["gate_b"], w["w_branch"], 0, f"merge_fwd_{l}")
    a["u1"], a["h1"] = out_fwd(a["merged"], h, w["w_out"], 0, w["ln1_g"], w["ln1_b"], f"out_fwd_{l}")
    a["u2"], a["h2"] = ff_fwd(a["h1"], w["w_up"], w["w_down"], 0, w["ln2_g"], w["ln2_b"], f"ff_fwd_{l}")
    return a, carried


def _layer_bwd(dh2, a, w, l, p, late_above):
    T = dh2.shape[0]
    g = {}
    du2, dh1, da, r, dg2, db2 = ff_bwd(a["u2"], dh2, a["h1"], w["ln2_g"], w["ln2_b"], w["w_up"], w["w_down"], 0, f"ff_bwd_{l}")
    g["ln2_g"], g["ln2_b"] = dg2[0], db2[0]
    g["w_up"] = matmul_w(transpose_bf16(a["h1"], f"h1_t_{l}"), da[None], 0, "nn", f"dwup_{l}", out_dtype=BF)
    g["w_down"] = matmul_w(r, du2[None], 0, "nn", f"dwdown_{l}", out_dtype=BF)
    du1, dmerged, dg1, db1 = out_bwd(a["u1"], dh1, w["ln1_g"], w["ln1_b"], w["w_out"], 0, f"out_bwd_{l}")
    g["ln1_g"], g["ln1_b"] = dg1[0], db1[0]
    g["w_out"] = matmul_w(transpose_bf16(a["merged"], f"merged_t_{l}"), du1[None], 0, "nn", f"dwout_{l}", out_dtype=BF)
    ys = [a["ya"], a["yb"], a["yc"], a["yd"]]
    dya, dyb, dyc, dyd, dgl, dz, dgb = merge_bwd(ys, a["gates"], w["gate_b"], w["w_branch"], 0, dmerged, f"merge_bwd_{l}")
    g["gate_b"] = dgb
    g["w_branch"] = jnp.stack([matmul_tn(ys[i], dz, f"dwb{i}_{l}", b_col0=i * D_MODEL, n_cols=D_MODEL, out_dtype=BF) for i in range(4)])
    early = exchange_rider(_exchange_items(g, p, EARLY))
    dn_rider = early if late_above is None else exchange_rider(late_above)
    fox_rider = None if late_above is None else early
    ssd, dn, sg, fc = _scan_specs(T, a)
    (dz_ssd, dxbc_act, dsm_ssd, d_dtb, d_alog, d_dsk, d_nw), _ = scan_bwd(f"ssd_bwd_{l}", ssd["f"], ssd["xs"], w["ssd_ps"], ssd["ys"], [dya], a["ssd_st"],
                                                                           ssd["state"], ssd["nc"], ssd["nh"])
    g["ssd_dt_bias"], g["ssd_a_log"], g["ssd_d"], g["ssd_norm_w"] = d_dtb[0, DT0:DT0 + 8], d_alog[0, DT0:DT0 + 8], d_dsk[0, DT0:DT0 + 8], d_nw[0]
    dxbc, g["ssd_conv_w"], dcb = conv_bwd(a["xbc"], w["ssd_cw"], w["ssd_cb"], dxbc_act, f"ssd_conv_bwd_{l}")
    g["ssd_conv_b"] = dcb[0]
    (ddn_act, ddngate, dsm_dn, d_alog, d_dtb, d_nw), got_dn = scan_bwd(f"dn_bwd_{l}", dn["f"], dn["xs"], w["dn_ps"], dn["ys"], [dyb], a["dn_st"],
                                                                        dn["state"], dn["nc"], dn["nh"], rider=dn_rider)
    g["dn_a_log"], g["dn_dt_bias"], g["dn_norm_w"] = d_alog[0, A0:A0 + 4], d_dtb[0, A0:A0 + 4], d_nw[0]
    ddnqkv, g["dn_conv_w"], _ = conv_bwd(a["dnqkv"], w["dn_cw"], w["dn_cb"], ddn_act, f"dn_conv_bwd_{l}")
    (dsguv, d_lng, d_lnb, d_w, d_bt), _ = scan_bwd(f"sg_bwd_{l}", sg["f"], sg["xs"], w["sg_ps"], sg["ys"], [dyc], a["sg_st"], sg["state"], sg["nc"], sg["nh"])
    g["sg_ln_g"], g["sg_ln_b"], g["sg_w"], g["sg_b"] = d_lng[0], d_lnb[0], d_w, d_bt[:, :4].T
    qb, doa, qd, kd = fox_prep_bwd(a["foxqkv"], a["fox_qa"], a["yd"], a["lse"], dyd, f"fox_prep_bwd_{l}")
    (dfq, dfk, dfv, dccol), got_fox = fox_bwd(qb, a["fox_ka"], a["fox_va"], doa, qd, kd, f"fox_bwd_{l}", rider=fox_rider)
    (dsm_fox, d_fb), _ = scan_bwd(f"foxc_bwd_{l}", fc["f"], fc["xs"], w["fox_ps"], fc["ys"], [dccol], a["fc_st"], fc["state"], fc["nc"], fc["nh"])
    g["fox_f_bias"] = d_fb[0, FF0:FF0 + 8]
    dseg = {"z": dz_ssd, "xbc": dxbc, "dnqkv": ddnqkv, "dngate": ddngate, "sguv": dsguv,
            "foxqkv": jnp.concatenate([dfq, dfk, dfv], axis=1), "gates": dgl, "small": add3(dsm_ssd, dsm_dn, dsm_fox, f"dsmall_{l}")}
    h_t = transpose_bf16(a["h"], f"h_t_{l}")
    dwin = {n: matmul_w(h_t, dseg[n][None], 0, "nn", f"dwin_{n}_{l}", out_dtype=BF) for n in SEG_NAMES}
    g["w_in"] = repack_dw_in(dwin, f"dw_in_repack_{l}")
    g["dn_conv_w"] = jnp.moveaxis(g["dn_conv_w"].reshape(4, N_DEV, 3 * BRANCH_W // N_DEV), 1, 0)
    got = {(EARLY, l): got_dn} if late_above is None else {(LATE, l + 1): got_dn, (EARLY, l): got_fox}
    return dseg, du1, g, got


def dh_all(dseg, w_in, add, name, rider=None):
    T = add.shape[0]
    tm = min(T, 256)
    ns = len(SEG_NAMES)

    def body(*refs):
        d_refs, w_refs, add_ref, o_ref = refs[:ns], refs[ns:2 * ns], refs[2 * ns], refs[2 * ns + 1]
        acc = ALPHA * add_ref[...]
        for d_ref, w_ref in zip(d_refs, w_refs):
            acc = acc + _dotb(d_ref[...].astype(BF), w_ref[...], NT)
        o_ref[...] = acc

    in_specs = [_row(tm, dseg[n].shape[1]) for n in SEG_NAMES]
    in_specs += [pl.BlockSpec((None,) + w_in[n].shape[1:], lambda i: (0, 0, 0), pipeline_mode=pl.Buffered(1)) for n in SEG_NAMES]
    in_specs.append(_row(tm, D_MODEL))
    return hosted_call(body, rider, grid=(T // tm,), in_specs=in_specs, out_specs=[_row(tm, D_MODEL)], out_shape=[_S((T, D_MODEL))],
                       scratch_shapes=[], name=name, args=[*[dseg[n] for n in SEG_NAMES], *[w_in[n] for n in SEG_NAMES], add])


def kernel(x, ln_in_g, ln_in_b, w_in, ssd_conv_w, ssd_conv_b, ssd_dt_bias, ssd_a_log, ssd_d, ssd_norm_w, dn_conv_w, dn_a_log, dn_dt_bias, dn_norm_w, sg_ln_g, sg_ln_b, sg_w, sg_b, fox_f_bias, gate_b, w_branch, w_out, ln1_g, ln1_b, w_up, w_down, ln2_g, ln2_b, loss_target, m_ln_in_g, m_ln_in_b, m_w_in, m_ssd_conv_w, m_ssd_conv_b, m_ssd_dt_bias, m_ssd_a_log, m_ssd_d, m_ssd_norm_w, m_dn_conv_w, m_dn_a_log, m_dn_dt_bias, m_dn_norm_w, m_sg_ln_g, m_sg_ln_b, m_sg_w, m_sg_b, m_fox_f_bias, m_gate_b, m_w_branch, m_w_out, m_ln1_g, m_ln1_b, m_w_up, m_w_down, m_ln2_g, m_ln2_b, v_ln_in_g, v_ln_in_b, v_w_in, v_ssd_conv_w, v_ssd_conv_b, v_ssd_dt_bias, v_ssd_a_log, v_ssd_d, v_ssd_norm_w, v_dn_conv_w, v_dn_a_log, v_dn_dt_bias, v_dn_norm_w, v_sg_ln_g, v_sg_ln_b, v_sg_w, v_sg_b, v_fox_f_bias, v_gate_b, v_w_branch, v_w_out, v_ln1_g, v_ln1_b, v_w_up, v_w_down, v_ln2_g, v_ln2_b):
    args = dict(locals())
    p = {n: args[n] for n in WEIGHTS}
    xt, target = x[0], loss_target[0]
    ws, acts = [_layer_weights(p, l) for l in range(DEPTH)], []
    _use_gathered(ws[0], LATE, comm_call(_gather_rider(p, 0, LATE), "weights_all_gather_0"), 0)
    h = ln_fwd(xt, ln_in_g[None], ln_in_b[None], "ln_in_fwd")
    for l in range(DEPTH):
        a, gathered = _layer_fwd(h, ws[l], l, dn_rider=_gather_rider(p, 0, EARLY) if l == 0 else None,
                                 fox_rider=_gather_rider(p, l + 1, LATE + EARLY) if l + 1 < DEPTH else None)
        if l + 1 < DEPTH:
            _use_gathered(ws[l + 1], LATE + EARLY, gathered, l + 1)
        acts.append(a)
        h = a["h2"]
    dh, loss = loss_head(h, target, "loss_head")
    loss = lax.psum(loss[0, 0], ("x", "y", "c"))

    layer_grads, got, late = [None] * DEPTH, {}, None
    for l in reversed(range(DEPTH)):
        dseg, du1, layer_grads[l], got_l = _layer_bwd(dh, acts[l], ws[l], l, p, late)
        got.update(got_l)
        late = _exchange_items(layer_grads[l], p, LATE)
        rider = None
        if l == 0:
            pack = _pack_small([jnp.stack([layer_grads[k][n] for k in range(DEPTH)]) for n in REPLICATED_LAYER], 64)
            rider = exchange_rider(late + [(pack[0], 0, -1, pack.shape[1:])])
        (dh,), carried = dh_all(dseg, ws[l]["w_in"], du1, f"dh_{l}", rider=rider)
    got[(LATE, 0)], got_layer_pack = carried[:-1], carried[-1]
    grad_x, dg_in, db_in = ln_bwd(xt, ln_in_g[None], ln_in_b[None], dh, "ln_in_bwd")
    pack = _pack_small([dg_in[0], db_in[0]], 8)
    got_entry_pack = comm_call(exchange_rider([(pack[0], 0, -1, pack.shape[1:])]), "grads_exchange_entry_norm")[0]
    rcv = {(n, l): arr for (names, l), arrs in got.items() for n, arr in zip(names, arrs)}

    res = [{}, {}, {}, {}]
    for n in SHARDED:
        shp = p[n].shape
        lead = math.prod(shp[1:-2])
        to3 = lambda t: t.reshape((-1,) + shp[-2:])
        outs = reduce_adamw([rcv[(n, l)].reshape((N_DEV, lead) + shp[-2:]) for l in range(DEPTH)],
                            to3(p[n]), to3(args["m_" + n]), to3(args["v_" + n]), f"adamw_{n}")
        for k in range(4):
            res[k][n] = outs[k].reshape(shp)
    for names, got_pack, rows, name in ((REPLICATED_LAYER, got_layer_pack, 64, "adamw_replicated"), (REPLICATED_ENTRY, got_entry_pack, 8, "adamw_entry_norm")):
        outs = reduce_adamw([got_pack[:, None]], _pack_small([p[n] for n in names], rows), _pack_small([args["m_" + n] for n in names], rows),
                            _pack_small([args["v_" + n] for n in names], rows), name)
        off = 0
        for n in names:
            shp = p[n].shape
            cnt = math.prod(shp)
            for k in range(4):
                res[k][n] = outs[k].reshape(-1)[off:off + cnt].reshape(shp)
            off += cnt
    return (loss, grad_x[None], *[res[0][n] for n in WEIGHTS], *[res[1][n] for n in WEIGHTS],
            *[res[2][n] for n in WEIGHTS], *[res[3][n] for n in WEIGHTS])
```

```python
import collections
import functools
import math

import jax
import jax.numpy as jnp
from jax import lax
from jax.experimental import pallas as pl
from jax.experimental.pallas import tpu as pltpu

F32 = jnp.float32
BF = jnp.bfloat16

D_MODEL = 1024
DEPTH = 2
BRANCH_W = 512
D_FF = 4096
LN_EPS = 1e-5
NORM_EPS = 1e-6
ALPHA = (2 * DEPTH) ** 0.25
N_DEV = 8
LANES = 128
ADAM_LR, ADAM_B1, ADAM_B2, ADAM_EPS, ADAM_WD, ADAM_STEP = 0.001, 0.9, 0.999, 1e-08, 0.01, 10

DT0, BETA0, A0, FF0 = 0, 8, 12, 16
IN_SIZES = (512, 1024, 8, 1536, 4, 4, 512, 1024, 1536, 8, 4096)
_OFF = [0]
for _s in IN_SIZES:
    _OFF.append(_OFF[-1] + _s)
D_IN = _OFF[-1]
SEGS = (("z", 0), ("xbc", 1), ("dnqkv", 3), ("dngate", 6), ("sguv", 7), ("foxqkv", 8), ("gates", 10))
SMALL_SRC = ((2, DT0), (4, BETA0), (5, A0), (9, FF0))

NN = ((1,), (0,))
NT = ((1,), (1,))
TN = ((0,), (0,))
_DIMS = {"nn": NN, "nt": NT, "tn": TN}


def _pcall(body, **kw):
    return pl.pallas_call(body, **kw)


def _S(shape, dtype=F32):
    return jax.ShapeDtypeStruct(tuple(shape), dtype)


def _iota(shape, dim):
    return lax.broadcasted_iota(jnp.int32, shape, dim)


def _dotb(a, b, dims):
    return lax.dot_general(a, b, (dims, ((), ())), preferred_element_type=F32)


def _split2(a):
    ah = a.astype(BF)
    return ah, (a - ah.astype(F32)).astype(BF)


def _split3(a):
    a1 = a.astype(BF)
    r = a - a1.astype(F32)
    a2 = r.astype(BF)
    a3 = (r - a2.astype(F32)).astype(BF)
    return a1, a2, a3


def _mm_raw(a, b, form, mode):
    d = _DIMS[form]
    if mode == "1":
        return _dotb(a.astype(BF), b.astype(BF), d)
    if mode == "3":
        ah, al = _split2(a)
        bh, bl = _split2(b)
        return _dotb(ah, bh, d) + (_dotb(ah, bl, d) + _dotb(al, bh, d))
    if mode == "xa":
        ab = a.astype(BF)
        b1, b2, b3 = _split3(b)
        return _dotb(ab, b1, d) + (_dotb(ab, b2, d) + _dotb(ab, b3, d))
    bb = b.astype(BF)
    a1, a2, a3 = _split3(a)
    return _dotb(a1, bb, d) + (_dotb(a2, bb, d) + _dotb(a3, bb, d))


@functools.partial(jax.custom_vjp, nondiff_argnums=(2, 3))
def mm(a, b, form, mode):
    return _mm_raw(a, b, form, mode)


def _mm_fwd(a, b, form, mode):
    return _mm_raw(a, b, form, mode), (a, b)


_XA_DB = {"nn": "xa", "nt": "xb", "tn": "xa"}
_XB_DA = {"nn": "xb", "nt": "xb", "tn": "xa"}


def _mm_bwd(form, mode, res, g):
    a, b = res
    ma = _XB_DA[form] if mode == "xb" else mode
    mb = _XA_DB[form] if mode == "xa" else mode
    da = db = None
    if mode != "xa":
        da = {"nn": lambda: mm(g, b, "nt", ma), "nt": lambda: mm(g, b, "nn", ma), "tn": lambda: mm(b, g, "nt", ma)}[form]()
    if mode != "xb":
        db = {"nn": lambda: mm(a, g, "tn", mb), "nt": lambda: mm(g, a, "tn", mb), "tn": lambda: mm(a, g, "nn", mb)}[form]()
    if da is None:
        da = jnp.zeros_like(a)
    if db is None:
        db = jnp.zeros_like(b)
    return da, db


mm.defvjp(_mm_fwd, _mm_bwd)


def _silu(x):
    return x * jax.nn.sigmoid(x)


def _ln(x, g, b):
    mu = jnp.mean(x, -1, keepdims=True)
    xc = x - mu
    var = jnp.mean(xc * xc, -1, keepdims=True)
    return xc * lax.rsqrt(var + LN_EPS) * g + b


def _pick(n, cap):
    if n <= cap:
        return n
    best = LANES
    for t in range(LANES, cap + 1, LANES):
        if n % t == 0:
            best = t
    return best


def transpose_bf16(a, name):
    T, C = a.shape
    tt = min(T, 512)

    def body(a_ref, o_ref):
        o_ref[...] = a_ref[...].T.astype(BF)

    return _pcall(body, grid=(T // tt,), in_specs=[pl.BlockSpec((tt, C), lambda t: (t, 0))], out_specs=pl.BlockSpec((C, tt), lambda t: (0, t)),
                  out_shape=_S((C, T), BF), name=name)(a)


def matmul_w(a, w, l, form, name, add=None, add_scale=1.0, out_dtype=F32):
    M, K = a.shape
    N = w.shape[2] if form == "nn" else w.shape[1]
    tm, tn, tk = min(M, 512), _pick(N, 1024), _pick(K, 1024)
    nk = K // tk

    def body(*refs):
        if add is None:
            a_ref, b_ref, o_ref, acc = refs
        else:
            a_ref, b_ref, d_ref, o_ref, acc = refs
        k = pl.program_id(2)
        p = _dotb(a_ref[...].astype(BF), b_ref[...].astype(BF), _DIMS[form])

        @pl.when(k == 0)
        def _():
            acc[...] = p

        @pl.when(k > 0)
        def _():
            acc[...] += p

        @pl.when(k == nk - 1)
        def _():
            r = acc[...]
            if add is not None:
                r = r + add_scale * d_ref[...]
            o_ref[...] = r.astype(out_dtype)

    if form == "nn":
        wspec = pl.BlockSpec((None, tk, tn), lambda j, i, k: (l, k, j))
    else:
        wspec = pl.BlockSpec((None, tn, tk), lambda j, i, k: (l, j, k))
    in_specs = [pl.BlockSpec((tm, tk), lambda j, i, k: (i, k)), wspec]
    args = [a, w]
    if add is not None:
        in_specs.append(pl.BlockSpec((tm, tn), lambda j, i, k: (i, j)))
        args.append(add)
    return _pcall(body, grid=(N // tn, M // tm, nk), in_specs=in_specs,
                  out_specs=pl.BlockSpec((tm, tn), lambda j, i, k: (i, j)), out_shape=_S((M, N), out_dtype),
                  scratch_shapes=[pltpu.VMEM((tm, tn), F32)], name=name)(*args)


def matmul_tn(a, b, name, b_col0=0, n_cols=None, out_dtype=F32):
    T, M = a.shape
    N = b.shape[1] if n_cols is None else n_cols
    tm, tn, tt = _pick(M, 512), _pick(N, 1024), min(T, 512)
    nt = T // tt
    jb = b_col0 // tn

    def body(a_ref, b_ref, o_ref, acc):
        t = pl.program_id(2)
        p = _dotb(a_ref[...].astype(BF), b_ref[...].astype(BF), TN)

        @pl.when(t == 0)
        def _():
            acc[...] = p

        @pl.when(t > 0)
        def _():
            acc[...] += p

        @pl.when(t == nt - 1)
        def _():
            o_ref[...] = acc[...].astype(out_dtype)

    return _pcall(body, grid=(M // tm, N // tn, nt),
                  in_specs=[pl.BlockSpec((tt, tm), lambda i, j, t: (t, i)), pl.BlockSpec((tt, tn), lambda i, j, t: (t, jb + j))],
                  out_specs=pl.BlockSpec((tm, tn), lambda i, j, t: (i, j)), out_shape=_S((M, N), out_dtype),
                  scratch_shapes=[pltpu.VMEM((tm, tn), F32)], name=name)(a, b)


def _pieces(v):
    if v.ndim == 3:
        return [v[i] for i in range(v.shape[0])]
    n = v.shape[1] // LANES
    if n <= 1:
        return [v]
    return [v[:, i * LANES:(i + 1) * LANES] for i in range(n)]


def _join(ps, like_ndim):
    if like_ndim == 3:
        return jnp.stack(ps, axis=0)
    return ps[0] if len(ps) == 1 else jnp.concatenate(ps, axis=1)


def scan_fwd(name, f, xs, ps, ys, state_shape, nc, nh=1, rider=None):
    nx, npar, ny = len(xs), len(ps), len(ys)

    def body(*refs):
        x_refs, p_refs = refs[:nx], refs[nx:nx + npar]
        y_refs = refs[nx + npar:nx + npar + ny]
        st_out, st = refs[nx + npar + ny], refs[nx + npar + ny + 1]
        c, h = pl.program_id(0), pl.program_id(1)

        @pl.when(c == 0)
        def _():
            st[h] = jnp.zeros(state_shape, F32)

        S = st[h]
        st_out[...] = S
        yv, Sn = f([_pieces(r[...]) for r in x_refs], [_pieces(r[...]) for r in p_refs], _pieces(S), h)
        for r, v in zip(y_refs, yv):
            r[...] = _join(v, 2)
        st[h] = _join(Sn, 3)

    in_specs = [pl.BlockSpec(bs, im) for (_, bs, im) in xs]
    in_specs += [pl.BlockSpec(p.shape, (lambda c, h, n=p.ndim: (0,) * n)) for p in ps]
    out_specs = [pl.BlockSpec(bs, im) for (_, bs, im) in ys]
    out_specs.append(pl.BlockSpec((None, None) + tuple(state_shape), lambda c, h: (c, h, 0, 0, 0)))
    out_shape = [_S(s) for (s, _, _) in ys] + [_S((nc, nh) + tuple(state_shape))]
    return hosted_call(body, rider, grid=(nc, nh), in_specs=in_specs, out_specs=out_specs, out_shape=out_shape,
                       scratch_shapes=[pltpu.VMEM((nh,) + tuple(state_shape), F32)], name=name, args=[*[x[0] for x in xs], *ps])


def scan_bwd(name, f, xs, ps, ys, dys, states, state_shape, nc, nh=1, shared=(), rider=None):
    nx, npar, ny = len(xs), len(ps), len(ys)

    def body(*refs):
        x_refs, p_refs = refs[:nx], refs[nx:nx + npar]
        s_ref = refs[nx + npar]
        dy_refs = refs[nx + npar + 1:nx + npar + 1 + ny]
        o = nx + npar + 1 + ny
        dx_refs, dp_refs, dst = refs[o:o + nx], refs[o + nx:o + nx + npar], refs[o + nx + npar]
        c, h = pl.program_id(0), pl.program_id(1)

        @pl.when(c == 0)
        def _():
            dst[h] = jnp.zeros(state_shape, F32)

        @pl.when((c == 0) & (h == 0))
        def _():
            for r in dp_refs:
                r[...] = jnp.zeros(r.shape, F32)

        xv = [_pieces(r[...]) for r in x_refs]
        pv = [_pieces(r[...]) for r in p_refs]
        _, vjp = jax.vjp(lambda a, b, s: f(a, b, s, h), xv, pv, _pieces(s_ref[...]))
        dxv, dpv, dS = vjp(([_pieces(r[...]) for r in dy_refs], _pieces(dst[h])))
        for i, (r, v) in enumerate(zip(dx_refs, dxv)):
            if i in shared and nh > 1:
                @pl.when(h == 0)
                def _(r=r, v=v):
                    r[...] = _join(v, 2)

                @pl.when(h > 0)
                def _(r=r, v=v):
                    r[...] += _join(v, 2)
            else:
                r[...] = _join(v, 2)
        for r, v in zip(dp_refs, dpv):
            r[...] += _join(v, len(r.shape))
        dst[h] = _join(dS, 3)

    def rev(im):
        return lambda c, h: im(nc - 1 - c, h)

    in_specs = [pl.BlockSpec(bs, rev(im)) for (_, bs, im) in xs]
    in_specs += [pl.BlockSpec(p.shape, (lambda c, h, n=p.ndim: (0,) * n)) for p in ps]
    in_specs.append(pl.BlockSpec((None, None) + tuple(state_shape), lambda c, h: (nc - 1 - c, h, 0, 0, 0)))
    in_specs += [pl.BlockSpec(bs, rev(im)) for (_, bs, im) in ys]
    out_specs = [pl.BlockSpec(bs, rev(im)) for (_, bs, im) in xs]
    out_specs += [pl.BlockSpec(p.shape, (lambda c, h, n=p.ndim: (0,) * n)) for p in ps]
    out_shape = [_S(x[0].shape) for x in xs] + [_S(p.shape) for p in ps]
    return hosted_call(body, rider, grid=(nc, nh), in_specs=in_specs, out_specs=out_specs, out_shape=out_shape,
                       scratch_shapes=[pltpu.VMEM((nh,) + tuple(state_shape), F32)], name=name,
                       args=[*[x[0] for x in xs], *ps, states, *dys])


def _lane():
    return _iota((1, LANES), 1)


def _col(v, idx):
    return jnp.sum(v * (_lane() == idx).astype(F32), axis=1, keepdims=True)


def _last_row(v):
    r = v.shape[0]
    return jnp.sum(v * (_iota((r, 1), 0) == r - 1).astype(F32), axis=0, keepdims=True)


def _tril(n, strict=False):
    r, c = _iota((n, n), 0), _iota((n, n), 1)
    return (r > c) if strict else (r >= c)


def ssd_chunk(xs, ps, S, h):
    zp, xbc, (sm,) = xs
    (bias,), (alog,), (dsk,), nw = ps
    Q = sm.shape[0]
    H = range(8)
    lane = _lane()
    a128 = jnp.where(lane < 8, -jnp.exp(alog), 0.0)
    dtl = jax.nn.softplus(sm + bias)
    tri = _tril(Q)
    cum = mm(tri.astype(F32), dtl * a128, "nn", "xa")
    sel8 = (_iota((8, LANES), 0) == _iota((8, LANES), 1)).astype(F32)
    cum_t = mm(sel8, cum, "nt", "xa")
    m0 = (lane < 64).astype(F32)
    rows0 = (_iota((LANES, 1), 0) < 64).astype(F32)
    me = [m0 if hh % 2 == 0 else 1.0 - m0 for hh in H]
    re = [rows0 if hh % 2 == 0 else 1.0 - rows0 for hh in H]
    Bm, Cm = [xbc[4 + hh // 4] for hh in H], [xbc[6 + hh // 4] for hh in H]
    cb = [mm(xbc[6 + g], xbc[4 + g], "nt", "1") for g in range(2)]
    col = [_col(cum, hh) for hh in H]
    row = [jnp.sum(cum_t * (_iota((8, 1), 0) == hh).astype(F32), axis=0, keepdims=True) for hh in H]
    xh = [xbc[hh // 2] * me[hh] for hh in H]
    xdt = [xh[hh] * _col(dtl, hh) for hh in H]
    seg = [jnp.exp(jnp.where(tri, col[hh] - row[hh], -jnp.inf)) for hh in H]
    last = [_last_row(col[hh]) for hh in H]
    y_diag = [mm(cb[hh // 4] * seg[hh], xdt[hh], "nn", "1") for hh in H]
    y_off = [mm(Cm[hh] * jnp.exp(col[hh]), S[hh // 2], "nt", "1") * me[hh] for hh in H]
    st = [mm(xdt[hh], Bm[hh] * jnp.exp(last[hh] - col[hh]), "tn", "1") for hh in H]
    y = [y_diag[hh] + y_off[hh] + _col(dsk, hh) * xh[hh] for hh in H]
    Sn = [S[pr] * (jnp.exp(last[2 * pr]) * re[0] + jnp.exp(last[2 * pr + 1]) * re[1]) + st[2 * pr] + st[2 * pr + 1] for pr in range(4)]
    yz = [(y[2 * pr] + y[2 * pr + 1]) * _silu(zp[pr]) for pr in range(4)]
    ssq = sum(jnp.sum(v * v, axis=1, keepdims=True) for v in yz)
    scale = lax.rsqrt(ssq / BRANCH_W + NORM_EPS)
    return [[yz[i] * scale * nw[i] for i in range(4)]], Sn


def dn_chunk(xs, ps, S, h):
    act, gate, (sm,) = xs
    (alog,), (dtb,), (nw,) = ps
    H = range(4)
    C = sm.shape[0]
    lane = _lane()
    G = jnp.where((lane >= A0) & (lane < A0 + 4), -jnp.exp(alog) * jax.nn.softplus(sm + dtb), 0.0)
    tri, strict = _tril(C), _tril(C, True)
    gcs = mm(tri.astype(F32), G, "nn", "xa")
    sig = jax.nn.sigmoid(sm)
    qn = [act[h] * lax.rsqrt(jnp.sum(act[h] * act[h], axis=1, keepdims=True) + NORM_EPS) * (LANES ** -0.5) for h in H]
    kn = [act[4 + h] * lax.rsqrt(jnp.sum(act[4 + h] * act[4 + h], axis=1, keepdims=True) + NORM_EPS) for h in H]
    beta = [_col(sig, BETA0 + h) for h in H]
    gcol = [_col(gcs, A0 + h) for h in H]
    selr = [((_iota((8, LANES), 0) == 0) & (_iota((8, LANES), 1) == A0 + h)).astype(F32) for h in H]
    grow = [jnp.sum(mm(selr[h], gcs, "nt", "xa"), axis=0, keepdims=True) for h in H]
    gamma = [jnp.exp(jnp.where(tri, gcol[h] - grow[h], -jnp.inf)) for h in H]
    kb = [kn[h] * beta[h] for h in H]
    pk = [-(mm(kb[h], kn[h], "nt", "1") * jnp.where(strict, gamma[h], 0.0)) for h in H]
    eye = (_iota((C, C), 0) == _iota((C, C), 1)).astype(F32)
    minv = [eye + pk[h] for h in H]
    for _ in range(5):
        pk = [mm(pk[h], pk[h], "nn", "3") for h in H]
        minv = [minv[h] + mm(minv[h], pk[h], "nn", "3") for h in H]
    eg = [jnp.exp(gcol[h]) for h in H]
    w = [mm(minv[h], kb[h] * eg[h], "nn", "3") for h in H]
    u = [mm(minv[h], act[8 + h] * beta[h], "nn", "3") for h in H]
    glast = [_last_row(gcol[h]) for h in H]
    vnew = [u[h] - mm(w[h], S[h], "nn", "1") for h in H]
    qk = [mm(qn[h], kn[h], "nt", "1") * gamma[h] for h in H]
    o = [mm(qn[h] * eg[h], S[h], "nn", "1") + mm(qk[h], vnew[h], "nn", "1") for h in H]
    Sn = [S[h] * jnp.exp(glast[h]) + mm(kn[h] * jnp.exp(glast[h] - gcol[h]), vnew[h], "tn", "1") for h in H]
    on = [o[h] * lax.rsqrt(jnp.mean(o[h] * o[h], axis=1, keepdims=True) + NORM_EPS) * nw for h in H]
    return [[on[h] * _silu(gate[h]) for h in H]], Sn


def sg_chunk(xs, ps, S, h):
    (uv,) = xs
    lng, lnb, W, (bt,) = ps
    u = [jax.nn.gelu(p) for p in uv[:4]]
    v = [jax.nn.gelu(p) for p in uv[4:]]
    mu = sum(jnp.sum(p, axis=1, keepdims=True) for p in v) / BRANCH_W
    vc = [p - mu for p in v]
    var = sum(jnp.sum(p * p, axis=1, keepdims=True) for p in vc) / BRANCH_W
    inv = lax.rsqrt(var + LN_EPS)
    trif = _tril(W[0].shape[0]).astype(F32)
    out = []
    for g in range(4):
        vn = vc[g] * inv * lng[g] + lnb[g]
        out.append(u[g] * (mm(W[g] * trif, vn, "nn", "1") + _col(bt, g)))
    return [out], S


def foxc_chunk(xs, ps, S, h):
    (sm,), ((fb,),), (carry,) = xs[0], ps, S
    lane = _lane()
    ls = jnp.where((lane >= FF0) & (lane < FF0 + 8), jax.nn.log_sigmoid(sm + fb), 0.0)
    c = mm(_tril(sm.shape[0]).astype(F32), ls, "nn", "xa") + carry
    return [[c]], [_last_row(c)]


HALO = 8


def _conv_tiles(T, C):
    return min(T, 512), _pick(C, 512)


def conv_fwd(x, w, b, name):
    T, C = x.shape
    tm, cb = _conv_tiles(T, C)

    def body(xp_ref, x_ref, w_ref, b_ref, o_ref):
        i = pl.program_id(1)
        e = jnp.concatenate([xp_ref[...] * (i > 0).astype(F32), x_ref[...]], axis=0)
        pre = b_ref[...] + sum(w_ref[k:k + 1, :] * e[5 + k:5 + k + tm, :] for k in range(4))
        o_ref[...] = _silu(pre)

    hb = tm // HALO
    return _pcall(body, grid=(C // cb, T // tm),
                  in_specs=[pl.BlockSpec((HALO, cb), lambda j, i: (jnp.maximum(i * hb - 1, 0), j)), pl.BlockSpec((tm, cb), lambda j, i: (i, j)),
                            pl.BlockSpec((4, cb), lambda j, i: (0, j)), pl.BlockSpec((1, cb), lambda j, i: (0, j))],
                  out_specs=pl.BlockSpec((tm, cb), lambda j, i: (i, j)), out_shape=_S((T, C)), name=name)(x, x, w, b)


def conv_bwd(x, w, b, dact, name):
    T, C = x.shape
    tm, cb = _conv_tiles(T, C)
    nt = T // tm

    def body(xp_ref, x_ref, xn_ref, w_ref, b_ref, d_ref, dn_ref, dx_ref, dw_ref, db_ref):
        i = pl.program_id(1)
        has_prev, has_next = (i > 0).astype(F32), (i < nt - 1).astype(F32)
        e = jnp.concatenate([xp_ref[...] * has_prev, x_ref[...], xn_ref[...] * has_next], axis=0)
        pre = b_ref[...] + sum(w_ref[k:k + 1, :] * e[5 + k:5 + k + tm + 8, :] for k in range(4))
        de = jnp.concatenate([d_ref[...], dn_ref[...] * has_next], axis=0)
        sg = jax.nn.sigmoid(pre)
        dpre = de * (sg * (1.0 + pre * (1.0 - sg)))
        dx_ref[...] = sum(w_ref[k:k + 1, :] * dpre[3 - k:3 - k + tm, :] for k in range(4))
        dcur = dpre[0:tm, :]
        dw = jnp.concatenate([jnp.sum(dcur * e[5 + k:5 + k + tm, :], axis=0, keepdims=True) for k in range(4)], axis=0)
        db = jnp.sum(dcur, axis=0, keepdims=True)

        @pl.when(i == 0)
        def _():
            dw_ref[...] = dw
            db_ref[...] = db

        @pl.when(i > 0)
        def _():
            dw_ref[...] += dw
            db_ref[...] += db

    blk = lambda f: pl.BlockSpec((tm, cb), f)
    hb = tm // HALO
    before = pl.BlockSpec((HALO, cb), lambda j, i: (jnp.maximum(i * hb - 1, 0), j))
    after = pl.BlockSpec((HALO, cb), lambda j, i: (jnp.minimum((i + 1) * hb, nt * hb - 1), j))
    return _pcall(body, grid=(C // cb, nt),
                  in_specs=[before, blk(lambda j, i: (i, j)), after,
                            pl.BlockSpec((4, cb), lambda j, i: (0, j)), pl.BlockSpec((1, cb), lambda j, i: (0, j)),
                            blk(lambda j, i: (i, j)), after],
                  out_specs=[blk(lambda j, i: (i, j)), pl.BlockSpec((4, cb), lambda j, i: (0, j)), pl.BlockSpec((1, cb), lambda j, i: (0, j))],
                  out_shape=[_S((T, C)), _S((4, C)), _S((1, C))], name=name)(x, x, x, w, b, dact, dact)


FOX_SCALE = 64 ** -0.5
LOG2E = 1.4426950408889634


def _spare(e, i):
    return (_lane() == 64 * (1 - e) + i).astype(F32)


def _lanes_of(e):
    lane = _lane()
    return ((lane < 64) if e == 0 else (lane >= 64)).astype(F32)


def _col3(col, e, first):
    c1 = col.astype(BF).astype(F32)
    c2 = (col - c1).astype(BF).astype(F32)
    c3 = (col - c1 - c2).astype(BF).astype(F32)
    return c1 * _spare(e, first) + c2 * _spare(e, first + 1) + c3 * _spare(e, first + 2)


def _ones3(e, first):
    return _spare(e, first) + _spare(e, first + 1) + _spare(e, first + 2)


def _causal_bias(n):
    return jnp.where(_iota((n, n), 0) >= _iota((n, n), 1), 0.0, -jnp.inf).astype(F32)


def _c_col(cc, hh):
    return jnp.sum(cc * (_lane() == FF0 + hh).astype(F32), axis=1, keepdims=True) * LOG2E


def _pair_spec(tq, row_of):
    return pl.BlockSpec((None, 2, tq, LANES), lambda hp, a, b: (hp, 0, row_of(a, b), 0))


def fox_prep(qkv, ccol, name):
    T = qkv.shape[0]
    tq = min(T, 512)

    def body(q_ref, k_ref, v_ref, cc_ref, qa_ref, ka_ref, va_ref):
        hp = pl.program_id(0)
        q, k, v, cc = q_ref[...], k_ref[...], v_ref[...], cc_ref[...]
        for e in range(2):
            me = _lanes_of(e)
            c2 = _c_col(cc, 2 * hp + e)
            qa_ref[e] = (q * me * (FOX_SCALE * LOG2E) + _col3(c2, e, 0) + _ones3(e, 3)).astype(BF)
            ka_ref[e] = (k * me + _ones3(e, 0) + _col3(-c2, e, 3) + _ones3(e, 6)).astype(BF)
            va_ref[e] = (v * me + (1.0 - me)).astype(BF)

    blk = lambda off: pl.BlockSpec((tq, LANES), lambda hp, i: (i, off + hp))
    out = pl.BlockSpec((None, 2, tq, LANES), lambda hp, i: (hp, 0, i, 0))
    return _pcall(body, grid=(4, T // tq), in_specs=[blk(0), blk(4), blk(8), pl.BlockSpec((tq, LANES), lambda hp, i: (i, 0))],
                  out_specs=[out] * 3, out_shape=[_S((4, 2, T, LANES), BF)] * 3, name=name)(qkv, qkv, qkv, ccol)


def fox_fwd(qa, ka, va, name, rider=None):
    T = qa.shape[2]
    tq = min(T, 512)
    nq = T // tq

    def body(qa_ref, ka_ref, va_ref, o_ref, lse_ref, m_s, acc, causal):
        i, j = pl.program_id(1), pl.program_id(2)

        @pl.when((pl.program_id(0) == 0) & (i == 0) & (j == 0))
        def _():
            causal[...] = _causal_bias(tq)

        @pl.when(j == 0)
        def _():
            m_s[...] = jnp.full(m_s.shape, -jnp.inf, F32)
            acc[...] = jnp.zeros(acc.shape, F32)

        def step(diagonal):
            for e in range(2):
                s = _dotb(qa_ref[e], ka_ref[e], NT)
                if diagonal:
                    s = s + causal[...]
                m_old = m_s[e]
                m_new = jnp.maximum(m_old, jnp.max(s, axis=1, keepdims=True))
                p = jnp.exp2(s - m_new)
                m_s[e] = m_new
                acc[e] = acc[e] * jnp.exp2(m_old - m_new) + _dotb(p.astype(BF), va_ref[e], NN)

        @pl.when(j < i)
        def _():
            step(False)

        @pl.when(j == i)
        def _():
            step(True)
            lane = _lane()
            o, lse = 0.0, 0.0
            for e in range(2):
                l = jnp.sum(acc[e] * _spare(e, 0), axis=1, keepdims=True)
                o = o + acc[e] * _lanes_of(e) / l
                lse = lse + (m_s[e] + jnp.log2(l)) * (lane == e).astype(F32)
            o_ref[...] = o
            lse_ref[...] = lse

    kv = _pair_spec(tq, lambda i, j: jnp.minimum(j, i))
    return hosted_call(body, rider, grid=(4, nq, nq), in_specs=[_pair_spec(tq, lambda i, j: i), kv, kv],
                       out_specs=[pl.BlockSpec((tq, LANES), lambda hp, i, j: (i, hp)), pl.BlockSpec((None, tq, LANES), lambda hp, i, j: (hp, i, 0))],
                       out_shape=[_S((T, BRANCH_W)), _S((4, T, LANES))],
                       scratch_shapes=[pltpu.VMEM((2, tq, 1), F32), pltpu.VMEM((2, tq, LANES), F32), pltpu.VMEM((tq, tq), F32)],
                       name=name, args=[qa, ka, va])


def fox_prep_bwd(qkv, qa, o, lse, do, name):
    T = qkv.shape[0]
    tq = min(T, 512)

    def body(q_ref, k_ref, qa_ref, o_ref, lse_ref, do_ref, qb_ref, doa_ref, qd_ref, kd_ref):
        q, k, dov = q_ref[...], k_ref[...], do_ref[...]
        dd = dov * o_ref[...]
        lane = _lane()
        for e in range(2):
            me = _lanes_of(e)
            lse_e = jnp.sum(lse_ref[...] * (lane == e).astype(F32), axis=1, keepdims=True)
            qb_ref[e] = (qa_ref[e].astype(F32) + _col3(-lse_e, e, 6)).astype(BF)
            doa_ref[e] = (dov * me + _col3(-jnp.sum(dd * me, axis=1, keepdims=True), e, 0)).astype(BF)
            qd_ref[e] = (q * me * FOX_SCALE + _spare(e, 0)).astype(BF)
            kd_ref[e] = (k * me * FOX_SCALE + _spare(e, 0)).astype(BF)

    blk = lambda off: pl.BlockSpec((tq, LANES), lambda hp, i: (i, off + hp))
    pair = pl.BlockSpec((None, 2, tq, LANES), lambda hp, i: (hp, 0, i, 0))
    return _pcall(body, grid=(4, T // tq),
                  in_specs=[blk(0), blk(4), pair, blk(0), pl.BlockSpec((None, tq, LANES), lambda hp, i: (hp, i, 0)), blk(0)],
                  out_specs=[pair] * 4, out_shape=[_S((4, 2, T, LANES), BF)] * 4, name=name)(qkv, qkv, qa, o, lse, do)


def fox_bwd(qb, ka, va, doa, qd, kd, name, rider=None):
    T = qb.shape[2]
    tq = min(T, 512)
    nq = T // tq

    def body(qb_ref, ka_ref, va_ref, doa_ref, qd_ref, kd_ref, dq_ref, dk_ref, dv_ref, dcc_ref, dq_s, dk_s, dv_s, causal):
        hp, j, ii = pl.program_id(0), pl.program_id(1), pl.program_id(2)
        i = jnp.maximum(ii, j)

        @pl.when((hp == 0) & (j == 0) & (ii == 0))
        def _():
            dcc_ref[...] = jnp.zeros(dcc_ref.shape, F32)
            causal[...] = _causal_bias(tq)

        @pl.when((j == 0) & (ii == 0))
        def _():
            dq_s[...] = jnp.zeros(dq_s.shape, F32)

        @pl.when(ii == 0)
        def _():
            dk_s[...] = jnp.zeros(dk_s.shape, F32)
            dv_s[...] = jnp.zeros(dv_s.shape, F32)

        def step(diagonal):
            rows = pl.ds(pl.multiple_of(i * tq, tq), tq)
            for e in range(2):
                s = _dotb(qb_ref[e], ka_ref[e], NT)
                if diagonal:
                    s = s + causal[...]
                p = jnp.exp2(s)
                ds = (p * _dotb(doa_ref[e], va_ref[e], NT)).astype(BF)
                dv_s[e] += _dotb(p.astype(BF), doa_ref[e], TN)
                dq_s[e, rows, :] += _dotb(ds, kd_ref[e], NN)
                dk_s[e] += _dotb(ds, qd_ref[e], TN)

        @pl.when(ii > j)
        def _():
            step(False)

        @pl.when(ii == j)
        def _():
            step(True)

        def fold(acc, sign):
            grad, dc = 0.0, 0.0
            for e in range(2):
                a = acc[e]
                grad = grad + a * _lanes_of(e)
                dc = dc + sign * jnp.sum(a * _spare(e, 0), axis=1, keepdims=True) * (_lane() == FF0 + 2 * hp + e).astype(F32)
            return grad, dc

        @pl.when(ii == nq - 1)
        def _():
            grad, dc = fold(dk_s, -1.0)
            dk_ref[...] = grad
            dv_ref[...] = dv_s[0] * _lanes_of(0) + dv_s[1] * _lanes_of(1)
            dcc_ref[pl.ds(pl.multiple_of(j * tq, tq), tq), :] += dc

        @pl.when((j == nq - 1) & (ii == nq - 1))
        def _():
            grad, dc = fold(dq_s, 1.0)
            dq_ref[...] = grad
            dcc_ref[...] += dc

    irow, jrow = _pair_spec(tq, lambda j, ii: jnp.maximum(ii, j)), _pair_spec(tq, lambda j, ii: j)
    jout = pl.BlockSpec((tq, LANES), lambda hp, j, ii: (j, hp))
    return hosted_call(body, rider, grid=(4, nq, nq), in_specs=[irow, jrow, jrow, irow, irow, jrow],
                       out_specs=[pl.BlockSpec((T, LANES), lambda hp, j, ii: (0, hp)), jout, jout, pl.BlockSpec((T, LANES), lambda hp, j, ii: (0, 0))],
                       out_shape=[_S((T, BRANCH_W)), _S((T, BRANCH_W)), _S((T, BRANCH_W)), _S((T, LANES))],
                       scratch_shapes=[pltpu.VMEM((2, T, LANES), F32), pltpu.VMEM((2, tq, LANES), F32), pltpu.VMEM((2, tq, LANES), F32),
                                       pltpu.VMEM((tq, tq), F32)],
                       name=name, args=[qb, ka, va, doa, qd, kd])


def _acc_out(ref, val, first):
    @pl.when(first)
    def _():
        ref[...] = val

    @pl.when(jnp.logical_not(first))
    def _():
        ref[...] += val


def _row(tm, c):
    return pl.BlockSpec((tm, c), lambda i: (i, 0))


def _full(shape):
    return pl.BlockSpec(shape, lambda *_: (0,) * len(shape))


def ln_fwd(x, g, b, name):
    T, C = x.shape
    tm = min(T, 512)

    def body(x_ref, g_ref, b_ref, o_ref):
        o_ref[...] = _ln(x_ref[...], g_ref[...], b_ref[...])

    return _pcall(body, grid=(T // tm,), in_specs=[_row(tm, C), _full((1, C)), _full((1, C))], out_specs=_row(tm, C),
                  out_shape=_S((T, C)), name=name)(x, g, b)


def ln_bwd(x, g, b, dy, name):
    T, C = x.shape
    tm = min(T, 512)

    def body(x_ref, g_ref, b_ref, dy_ref, dx_ref, dg_ref, db_ref):
        _, vjp = jax.vjp(_ln, x_ref[...], g_ref[...], b_ref[...])
        dx, dg, db = vjp(dy_ref[...])
        dx_ref[...] = dx
        first = pl.program_id(0) == 0
        _acc_out(dg_ref, dg, first)
        _acc_out(db_ref, db, first)

    return _pcall(body, grid=(T // tm,), in_specs=[_row(tm, C), _full((1, C)), _full((1, C)), _row(tm, C)],
                  out_specs=[_row(tm, C), _full((1, C)), _full((1, C))], out_shape=[_S((T, C)), _S((1, C)), _S((1, C))], name=name)(x, g, b, dy)


def loss_head(h, target, name):
    T, C = h.shape
    tm = min(T, 512)

    def body(h_ref, t_ref, d_ref, l_ref):
        e = h_ref[...] - t_ref[...]
        d_ref[...] = e * (1.0 / C)
        part = jnp.sum(jnp.sum(e * e, axis=1, keepdims=True), axis=0, keepdims=True) * (0.5 / C)
        _acc_out(l_ref, part, pl.program_id(0) == 0)

    return _pcall(body, grid=(T // tm,), in_specs=[_row(tm, C), _row(tm, C)], out_specs=[_row(tm, C), _full((1, 1))],
                  out_shape=[_S((T, C)), _S((1, 1))], name=name)(h, target)


def add3(a, b, c, name):
    T, C = a.shape
    tm = min(T, 512)

    def body(a_ref, b_ref, c_ref, o_ref):
        o_ref[...] = a_ref[...] + b_ref[...] + c_ref[...]

    return _pcall(body, grid=(T // tm,), in_specs=[_row(tm, C)] * 3, out_specs=_row(tm, C), out_shape=_S((T, C)), name=name)(a, b, c)


def _wb_spec(l):
    return pl.BlockSpec((None, 4, BRANCH_W, D_MODEL), lambda *_: (l, 0, 0, 0))


def merge_fwd(ys, gl, gb, wb, l, name):
    T = gl.shape[0]
    tm = min(T, 256)

    def body(y0, y1, y2, y3, gl_ref, gb_ref, wb_ref, o_ref):
        acc = 0.0
        for i, y in enumerate((y0, y1, y2, y3)):
            z = _dotb(y[...].astype(BF), wb_ref[i], NN)
            g = jax.nn.sigmoid(gl_ref[:, i * D_MODEL:(i + 1) * D_MODEL] + gb_ref[i:i + 1, :])
            acc = acc + g * z
        o_ref[...] = acc

    return _pcall(body, grid=(T // tm,), in_specs=[_row(tm, BRANCH_W)] * 4 + [_row(tm, 4 * D_MODEL), _full((4, D_MODEL)), _wb_spec(l)],
                  out_specs=_row(tm, D_MODEL), out_shape=_S((T, D_MODEL)), name=name)(*ys, gl, gb, wb)


def merge_bwd(ys, gl, gb, wb, l, dm, name):
    T = gl.shape[0]
    tm = min(T, 256)

    def body(y0, y1, y2, y3, gl_ref, gb_ref, wb_ref, dm_ref, d0, d1, d2, d3, dgl_ref, dz_ref, dgb_ref):
        dmv = dm_ref[...]
        first = pl.program_id(0) == 0
        for i, (y, d) in enumerate(zip((y0, y1, y2, y3), (d0, d1, d2, d3))):
            cols = slice(i * D_MODEL, (i + 1) * D_MODEL)
            z = _dotb(y[...].astype(BF), wb_ref[i], NN)
            g = jax.nn.sigmoid(gl_ref[:, cols] + gb_ref[i:i + 1, :])
            dgl = dmv * z * (g * (1.0 - g))
            dz = (g * dmv).astype(BF)
            dgl_ref[:, cols] = dgl
            dz_ref[:, cols] = dz
            d[...] = _dotb(dz, wb_ref[i], NT)
            _acc_out(dgb_ref.at[i:i + 1, :], jnp.sum(dgl, axis=0, keepdims=True), first)

    return _pcall(body, grid=(T // tm,),
                  in_specs=[_row(tm, BRANCH_W)] * 4 + [_row(tm, 4 * D_MODEL), _full((4, D_MODEL)), _wb_spec(l), _row(tm, D_MODEL)],
                  out_specs=[_row(tm, BRANCH_W)] * 4 + [_row(tm, 4 * D_MODEL), _row(tm, 4 * D_MODEL), _full((4, D_MODEL))],
                  out_shape=[_S((T, BRANCH_W))] * 4 + [_S((T, 4 * D_MODEL)), _S((T, 4 * D_MODEL), BF), _S((4, D_MODEL))], name=name)(
                      *ys, gl, gb, wb, dm)


def _wout_spec(l):
    return pl.BlockSpec((None, D_MODEL, D_MODEL), lambda *_: (l, 0, 0))


def out_fwd(merged, h, wout, l, g, b, name):
    T = h.shape[0]
    tm = min(T, 512)

    def body(m_ref, h_ref, w_ref, g_ref, b_ref, u_ref, o_ref):
        u = ALPHA * h_ref[...] + _dotb(m_ref[...].astype(BF), w_ref[...], NN)
        u_ref[...] = u
        o_ref[...] = _ln(u, g_ref[...], b_ref[...])

    C = D_MODEL
    return _pcall(body, grid=(T // tm,), in_specs=[_row(tm, C), _row(tm, C), _wout_spec(l), _full((1, C)), _full((1, C))],
                  out_specs=[_row(tm, C), _row(tm, C)], out_shape=[_S((T, C)), _S((T, C))], name=name)(merged, h, wout, g, b)


def out_bwd(u, dy, g, b, wout, l, name):
    T, C = u.shape
    tm = min(T, 512)

    def body(u_ref, dy_ref, g_ref, b_ref, w_ref, du_ref, dm_ref, dg_ref, db_ref):
        _, vjp = jax.vjp(_ln, u_ref[...], g_ref[...], b_ref[...])
        du, dg, db = vjp(dy_ref[...])
        du_ref[...] = du
        dm_ref[...] = _dotb(du.astype(BF), w_ref[...], NT)
        first = pl.program_id(0) == 0
        _acc_out(dg_ref, dg, first)
        _acc_out(db_ref, db, first)

    return _pcall(body, grid=(T // tm,), in_specs=[_row(tm, C), _row(tm, C), _full((1, C)), _full((1, C)), _wout_spec(l)],
                  out_specs=[_row(tm, C), _row(tm, C), _full((1, C)), _full((1, C))],
                  out_shape=[_S((T, C)), _S((T, C)), _S((1, C)), _S((1, C))], name=name)(u, dy, g, b, wout)


def ff_fwd(h, wup, wdown, l, g, b, name):
    T, C = h.shape
    F = wup.shape[2]
    tm, tf = min(T, 512), 1024
    nf = F // tf

    def body(h_ref, wu_ref, wd_ref, g_ref, b_ref, u_ref, o_ref, acc):
        f = pl.program_id(1)
        a = _dotb(h_ref[...].astype(BF), wu_ref[...], NN)
        r = jnp.square(jnp.maximum(a, 0.0))
        p = _dotb(r.astype(BF), wd_ref[...], NN)
        _acc_out(acc, p, f == 0)

        @pl.when(f == nf - 1)
        def _():
            u = ALPHA * h_ref[...] + acc[...]
            u_ref[...] = u
            o_ref[...] = _ln(u, g_ref[...], b_ref[...])

    row = pl.BlockSpec((tm, C), lambda i, f: (i, 0))
    return _pcall(body, grid=(T // tm, nf),
                  in_specs=[row, pl.BlockSpec((None, C, tf), lambda i, f: (l, 0, f)), pl.BlockSpec((None, tf, C), lambda i, f: (l, f, 0)),
                            _full((1, C)), _full((1, C))],
                  out_specs=[row, row], out_shape=[_S((T, C)), _S((T, C))], scratch_shapes=[pltpu.VMEM((tm, C), F32)], name=name)(h, wup, wdown, g, b)


def ff_bwd(u, dy, h, g, b, wup, wdown, l, name):
    T, C = h.shape
    F = wup.shape[2]
    tm, tf = min(T, 512), 1024
    nf = F // tf

    def body(u_ref, dy_ref, h_ref, g_ref, b_ref, wu_ref, wd_ref, du_ref, dh_ref, da_ref, r_ref, dg_ref, db_ref, du_s, acc):
        i, f = pl.program_id(0), pl.program_id(1)

        @pl.when(f == 0)
        def _():
            _, vjp = jax.vjp(_ln, u_ref[...], g_ref[...], b_ref[...])
            du, dg, db = vjp(dy_ref[...])
            du_s[...] = du
            du_ref[...] = du
            _acc_out(dg_ref, dg, i == 0)
            _acc_out(db_ref, db, i == 0)

        a = _dotb(h_ref[...].astype(BF), wu_ref[...], NN)
        ap = jnp.maximum(a, 0.0)
        dr = _dotb(du_s[...].astype(BF), wd_ref[...], NT)
        da = (dr * (2.0 * ap)).astype(BF)
        da_ref[...] = da
        r_ref[...] = jnp.square(ap).T.astype(BF)
        _acc_out(acc, _dotb(da, wu_ref[...], NT), f == 0)

        @pl.when(f == nf - 1)
        def _():
            dh_ref[...] = ALPHA * du_s[...] + acc[...]

    row = pl.BlockSpec((tm, C), lambda i, f: (i, 0))
    colf = pl.BlockSpec((tm, tf), lambda i, f: (i, f))
    return _pcall(body, grid=(T // tm, nf),
                  in_specs=[row, row, row, _full((1, C)), _full((1, C)), pl.BlockSpec((None, C, tf), lambda i, f: (l, 0, f)),
                            pl.BlockSpec((None, tf, C), lambda i, f: (l, f, 0))],
                  out_specs=[row, row, colf, pl.BlockSpec((tf, tm), lambda i, f: (f, i)), _full((1, C)), _full((1, C))],
                  out_shape=[_S((T, C)), _S((T, C)), _S((T, F), BF), _S((F, T), BF), _S((1, C)), _S((1, C))],
                  scratch_shapes=[pltpu.VMEM((tm, C), F32), pltpu.VMEM((tm, C), F32)], name=name)(u, dy, h, g, b, wup, wdown)


MESH_ID = pl.DeviceIdType.MESH
_ANY = pl.BlockSpec(memory_space=pl.ANY)


def _window(ref, ax, idx, n):
    if n < 0:
        return ref
    sel = idx if n == 0 else pl.ds(pl.multiple_of(idx * n, n), n)
    return ref.at[(slice(None),) * ax + (sel,)]


Rider = collections.namedtuple("Rider", "operands out_shape scratch start wait")


def hosted_call(body, rider, *, grid, in_specs, out_specs, out_shape, scratch_shapes, name, args):
    n_in, n_out, n_scr = len(in_specs), len(out_specs), len(scratch_shapes)
    if rider is None:
        return _pcall(body, grid=grid, in_specs=in_specs, out_specs=out_specs, out_shape=out_shape, scratch_shapes=scratch_shapes, name=name)(*args), []
    ri, ro = len(rider.operands), len(rider.out_shape)

    def wrapped(*refs):
        ins, r_in = refs[:n_in], refs[n_in:n_in + ri]
        o0 = n_in + ri
        outs, r_out = refs[o0:o0 + n_out], refs[o0 + n_out:o0 + n_out + ro]
        s0 = o0 + n_out + ro
        scr, r_scr = refs[s0:s0 + n_scr], refs[s0 + n_scr:]
        ids = [pl.program_id(i) for i in range(len(grid))]
        first = functools.reduce(jnp.logical_and, [i == 0 for i in ids])
        last = functools.reduce(jnp.logical_and, [i == g - 1 for i, g in zip(ids, grid)])

        @pl.when(first)
        def _():
            rider.start(r_in, r_out, r_scr)

        body(*ins, *outs, *scr)

        @pl.when(last)
        def _():
            rider.wait(r_in, r_out, r_scr)

    res = _pcall(wrapped, grid=grid, in_specs=list(in_specs) + [_ANY] * ri, out_specs=list(out_specs) + [_ANY] * ro,
                 out_shape=list(out_shape) + list(rider.out_shape), scratch_shapes=list(scratch_shapes) + list(rider.scratch),
                 name=name)(*args, *rider.operands)
    return res[:n_out], res[n_out:]


def comm_call(rider, name):
    ri = len(rider.operands)

    def body(*refs):
        r_in, r_out, r_scr = refs[:ri], refs[ri:ri + len(rider.out_shape)], refs[ri + len(rider.out_shape):]
        rider.start(r_in, r_out, r_scr)
        rider.wait(r_in, r_out, r_scr)

    return _pcall(body, in_specs=[_ANY] * ri, out_specs=[_ANY] * len(rider.out_shape), out_shape=list(rider.out_shape),
                  scratch_shapes=list(rider.scratch), name=name)(*rider.operands)


def gather_rider(shards, axes):
    K = len(shards)
    widths = [s.shape[a] for s, a in zip(shards, axes)]
    out_shape = [_S(s.shape[:a] + (N_DEV * s.shape[a],) + s.shape[a + 1:], s.dtype) for s, a in zip(shards, axes)]

    def plan(x_refs, o_refs, sems):
        send_sems, recv_sems, local_sems = sems
        mx, my, mc = lax.axis_index("x"), lax.axis_index("y"), lax.axis_index("c")
        me, sibling = (mx, my, mc), (mx, my, 1 - mc)
        chips = [(1 - mx, my), (mx, 1 - my), (1 - mx, 1 - my)]

        def win(k, px, py, pc):
            return _window(o_refs[k], axes[k], 4 * px + 2 * py + pc, widths[k])

        def copy(k, slot, block, to, src=None):
            return pltpu.make_async_remote_copy(src_ref=win(k, *block) if src is None else src, dst_ref=win(k, *block),
                                                send_sem=send_sems.at[7 * k + slot], recv_sem=recv_sems.at[7 * k + slot],
                                                device_id=to, device_id_type=MESH_ID)

        mine = [pltpu.make_async_copy(x_refs[k], win(k, *me), local_sems.at[k]) for k in range(K)]
        first = []
        for k in range(K):
            first.append(copy(k, 0, me, sibling, src=x_refs[k]))
            first += [copy(k, 1 + j, me, (*chip, mc), src=x_refs[k]) for j, chip in enumerate(chips)]
        return me, sibling, chips, copy, mine, first

    def start(x_refs, o_refs, sems):
        _, _, _, _, mine, first = plan(x_refs, o_refs, sems)
        for cp in mine + first:
            cp.start()

    def wait(x_refs, o_refs, sems):
        me, sibling, chips, copy, mine, first = plan(x_refs, o_refs, sems)
        mc = me[2]
        passed = []
        for j, chip in enumerate(chips):
            for k in range(K):
                copy(k, 1 + j, (*chip, mc), me).wait_recv()
                passed.append(copy(k, 4 + j, (*chip, mc), sibling))
                passed[-1].start()
        for k in range(K):
            copy(k, 0, sibling, me).wait_recv()
        for j, chip in enumerate(chips):
            for k in range(K):
                copy(k, 4 + j, (*chip, 1 - mc), me).wait_recv()
        for cp in first + passed:
            cp.wait_send()
        for cp in mine:
            cp.wait()

    scratch = [pltpu.SemaphoreType.DMA((7 * K,)), pltpu.SemaphoreType.DMA((7 * K,)), pltpu.SemaphoreType.DMA((K,))]
    return Rider(list(shards), out_shape, scratch, start, wait)


def exchange_rider(items):
    ns = len(items)
    out_shape = [_S((N_DEV,) + tuple(it[3]), it[0].dtype) for it in items]

    def plan(src_refs, o_refs, sems):
        send_sems, recv_sems, local_sems = sems
        mx, my, mc = lax.axis_index("x"), lax.axis_index("y"), lax.axis_index("c")
        me = 4 * mx + 2 * my + mc
        remote, own = [], []
        for s, (_, ax, n, _) in enumerate(items):
            own.append(pltpu.make_async_copy(_window(src_refs[s], ax, me, n), o_refs[s].at[me], local_sems.at[s]))
            for k in range(1, N_DEV):
                px = 1 - mx if k & 4 else mx
                py = 1 - my if k & 2 else my
                pc = 1 - mc if k & 1 else mc
                remote.append(pltpu.make_async_remote_copy(
                    src_ref=_window(src_refs[s], ax, 4 * px + 2 * py + pc, n), dst_ref=o_refs[s].at[me],
                    send_sem=send_sems.at[7 * s + k - 1], recv_sem=recv_sems.at[7 * s + k - 1],
                    device_id=(px, py, pc), device_id_type=MESH_ID))
        return remote, own

    def start(src_refs, o_refs, sems):
        remote, own = plan(src_refs, o_refs, sems)
        for cp in own + remote:
            cp.start()

    def wait(src_refs, o_refs, sems):
        remote, own = plan(src_refs, o_refs, sems)
        for cp in remote + own:
            cp.wait()

    scratch = [pltpu.SemaphoreType.DMA((7 * ns,)), pltpu.SemaphoreType.DMA((7 * ns,)), pltpu.SemaphoreType.DMA((ns,))]
    return Rider([it[0] for it in items], out_shape, scratch, start, wait)


def reduce_adamw(rcvs, w, m, v, name):
    L = len(rcvs)
    _, A, B, C = rcvs[0].shape
    tb = B
    while tb > 8 and tb * C > (1 << 17):
        tb //= 2

    def body(*refs):
        r_refs, (w_ref, m_ref, v_ref, g_ref, d_ref, mo_ref, vo_ref) = refs[:L], refs[L:]
        for k in range(L):
            @pl.when(pl.program_id(0) == k)
            def _(k=k):
                g = r_refs[k][0].astype(F32)
                for d in range(1, N_DEV):
                    g = g + r_refs[k][d].astype(F32)
                mn = ADAM_B1 * m_ref[...] + (1.0 - ADAM_B1) * g
                vn = ADAM_B2 * v_ref[...] + (1.0 - ADAM_B2) * jnp.square(g)
                m_hat = mn / (1.0 - ADAM_B1 ** ADAM_STEP)
                v_hat = vn / (1.0 - ADAM_B2 ** ADAM_STEP)
                g_ref[...] = g
                d_ref[...] = -ADAM_LR * (m_hat / (jnp.sqrt(v_hat) + ADAM_EPS) + ADAM_WD * w_ref[...])
                mo_ref[...] = mn
                vo_ref[...] = vn

    def rspec(k):
        return pl.BlockSpec((N_DEV, None, tb, C), lambda l, a, i: (0, jnp.where(l == k, a, 0), jnp.where(l == k, i, 0), 0))

    blk = pl.BlockSpec((None, tb, C), lambda l, a, i: (l * A + a, i, 0))
    return _pcall(body, grid=(L, A, B // tb), in_specs=[rspec(k) for k in range(L)] + [blk, blk, blk],
                  out_specs=[blk] * 4, out_shape=[_S((L * A, B, C))] * 4, name=name)(*rcvs, w, m, v)


def _w_in_pieces(g0, g1):
    per = D_IN // N_DEV
    return [(d, max(g0, d * per) - d * per, min(g1, (d + 1) * per) - d * per) for d in range(N_DEV) if max(g0, d * per) < min(g1, (d + 1) * per)]


def repack_w_in(w8, name):
    _, L, R, per = w8.shape
    tr = 256

    def cols(x_ref, g0, g1):
        return [x_ref[d, :, a:b] for d, a, b in _w_in_pieces(g0, g1)]

    def body(x_ref, *o_refs):
        for (name_, i), o_ref in zip(SEGS, o_refs):
            o_ref[...] = jnp.concatenate(cols(x_ref, _OFF[i], _OFF[i + 1]), axis=1)
        parts, at = [], 0
        for i, lane0 in SMALL_SRC:
            assert lane0 == at
            parts += cols(x_ref, _OFF[i], _OFF[i + 1])
            at += IN_SIZES[i]
        parts.append(jnp.zeros((tr, LANES - at), w8.dtype))
        o_refs[-1][...] = jnp.concatenate(parts, axis=1)

    widths = [IN_SIZES[i] for _, i in SEGS] + [LANES]
    outs = _pcall(body, grid=(L, R // tr), in_specs=[pl.BlockSpec((N_DEV, None, tr, per), lambda l, r: (0, l, r, 0))],
                  out_specs=[pl.BlockSpec((None, tr, w), lambda l, r: (l, r, 0)) for w in widths],
                  out_shape=[_S((L, R, w), w8.dtype) for w in widths], name=name)(w8)
    return dict(zip(SEG_NAMES, outs))


def repack_dw_in(dseg, name):
    R = dseg["z"].shape[0]
    per = D_IN // N_DEV
    tr = 128
    src = {i: (k, 0) for k, (_, i) in enumerate(SEGS)}
    src.update({i: (len(SEGS), lane0) for i, lane0 in SMALL_SRC})

    def body(*refs):
        s_refs, o_ref = refs[:-1], refs[-1]
        for d in range(N_DEV):
            parts = []
            for i in range(len(IN_SIZES)):
                g0, g1 = max(_OFF[i], d * per), min(_OFF[i + 1], (d + 1) * per)
                if g0 < g1:
                    k, c0 = src[i]
                    parts.append(s_refs[k][:, c0 + g0 - _OFF[i]:c0 + g1 - _OFF[i]])
            o_ref[d] = jnp.concatenate(parts, axis=1)

    arrs = [dseg[n] for n in SEG_NAMES]
    return _pcall(body, grid=(R // tr,), in_specs=[pl.BlockSpec((tr, a.shape[1]), lambda r: (r, 0)) for a in arrs],
                  out_specs=pl.BlockSpec((N_DEV, tr, per), lambda r: (0, r, 0)), out_shape=_S((N_DEV, R, per), arrs[0].dtype), name=name)(*arrs)


WEIGHTS = ("ln_in_g", "ln_in_b", "w_in", "ssd_conv_w", "ssd_conv_b", "ssd_dt_bias", "ssd_a_log", "ssd_d", "ssd_norm_w", "dn_conv_w",
           "dn_a_log", "dn_dt_bias", "dn_norm_w", "sg_ln_g", "sg_ln_b", "sg_w", "sg_b", "fox_f_bias", "gate_b", "w_branch", "w_out",
           "ln1_g", "ln1_b", "w_up", "w_down", "ln2_g", "ln2_b")
SHARDED = {"w_in": 2, "ssd_conv_w": 2, "dn_conv_w": 2, "gate_b": 2, "w_branch": 3, "w_out": 1, "w_up": 2, "w_down": 1}
SLABBED = ("w_in", "dn_conv_w")
MATMUL_WEIGHTS = ("w_in", "w_branch", "w_out", "w_up", "w_down")
REPLICATED_ENTRY = ("ln_in_g", "ln_in_b")
REPLICATED_LAYER = tuple(n for n in WEIGHTS if n not in SHARDED and n not in REPLICATED_ENTRY)
SEG_NAMES = tuple(n for n, _ in SEGS) + ("small",)
PACK_COLS = 1024


def _lanes(vec, off):
    return jnp.pad(vec, (off, LANES - off - vec.shape[0]))[None]


def _pack_small(parts, row_mult):
    flat = jnp.concatenate([q.reshape(-1) for q in parts])
    rows = -(-flat.shape[0] // (PACK_COLS * row_mult)) * row_mult
    return jnp.pad(flat, (0, rows * PACK_COLS - flat.shape[0])).reshape(1, rows, PACK_COLS)


EARLY = ("w_branch", "w_out", "w_up", "w_down", "gate_b")
LATE = ("w_in", "ssd_conv_w", "dn_conv_w")


def _gather_rider(p, l, names):
    shards, axes = [], []
    for n in names:
        s = p[n][l:l + 1]
        s = s.astype(BF) if n in MATMUL_WEIGHTS else s
        shards.append(s[None] if n in SLABBED else s)
        axes.append(0 if n in SLABBED else SHARDED[n])
    return gather_rider(shards, axes)


def _exchange_items(g, p, names):
    items = []
    for n in names:
        local = p[n].shape[1:]
        items.append((g[n], 0, 0, local) if n in SLABBED else (g[n], SHARDED[n] - 1, local[SHARDED[n] - 1], local))
    return items


def _use_gathered(w, names, arrays, l):
    for n, arr in zip(names, arrays):
        if n == "w_in":
            w[n] = repack_w_in(arr, f"w_in_repack_{l}")
        elif n == "ssd_conv_w":
            w["ssd_cw"] = arr[0]
        elif n == "dn_conv_w":
            w["dn_cw"] = jnp.moveaxis(arr[:, 0], 0, 1).reshape(4, 3 * BRANCH_W)
        elif n == "gate_b":
            w[n] = arr[0]
        else:
            w[n] = arr


def _layer_weights(p, l):
    w = {}
    w["ssd_cb"] = p["ssd_conv_b"][l][None]
    w["dn_cb"] = jnp.zeros((1, 3 * BRANCH_W), F32)
    w["ssd_ps"] = [_lanes(p["ssd_dt_bias"][l], DT0), _lanes(p["ssd_a_log"][l], DT0), _lanes(p["ssd_d"][l], DT0), p["ssd_norm_w"][l][None]]
    w["dn_ps"] = [_lanes(p["dn_a_log"][l], A0), _lanes(p["dn_dt_bias"][l], A0), p["dn_norm_w"][l][None]]
    w["sg_ps"] = [p["sg_ln_g"][l][None], p["sg_ln_b"][l][None], p["sg_w"][l], jnp.pad(p["sg_b"][l].T, ((0, 0), (0, LANES - 4)))]
    w["fox_ps"] = [_lanes(p["fox_f_bias"][l], FF0)]
    for n in ("ln1_g", "ln1_b", "ln2_g", "ln2_b"):
        w[n] = p[n][l][None]
    return w


def _scan_specs(T, a):
    c0 = lambda c, h: (c, 0)
    ssd = dict(f=ssd_chunk, xs=[(a["z"], (128, 512), c0), (a["xbc_act"], (128, 1024), c0), (a["small"], (128, LANES), c0)],
               ys=[((T, BRANCH_W), (128, BRANCH_W), c0)], state=(4, LANES, LANES), nc=T // 128, nh=1, shared=())
    dn = dict(f=dn_chunk, xs=[(a["dn_act"], (64, 3 * BRANCH_W), c0), (a["dngate"], (64, BRANCH_W), c0), (a["small"], (64, LANES), c0)],
              ys=[((T, BRANCH_W), (64, BRANCH_W), c0)], state=(4, LANES, LANES), nc=T // 64, nh=1, shared=())
    sg = dict(f=sg_chunk, xs=[(a["sguv"], (128, 1024), c0)], ys=[((T, BRANCH_W), (128, BRANCH_W), c0)], state=(1, 8, LANES), nc=T // 128, nh=1, shared=())
    fc = dict(f=foxc_chunk, xs=[(a["small"], (128, LANES), c0)],
              ys=[((T, LANES), (128, LANES), c0)], state=(1, 1, LANES), nc=T // 128, nh=1, shared=())
    return ssd, dn, sg, fc


def _layer_fwd(h, w, l, dn_rider=None, fox_rider=None):
    T = h.shape[0]
    a = {"h": h, **proj_all(h, w["w_in"], f"proj_{l}")}
    a["xbc_act"] = conv_fwd(a["xbc"], w["ssd_cw"], w["ssd_cb"], f"ssd_conv_{l}")
    a["dn_act"] = conv_fwd(a["dnqkv"], w["dn_cw"], w["dn_cb"], f"dn_conv_{l}")
    ssd, dn, sg, fc = _scan_specs(T, a)
    (a["ya"], a["ssd_st"]), _ = scan_fwd(f"ssd_fwd_{l}", ssd["f"], ssd["xs"], w["ssd_ps"], ssd["ys"], ssd["state"], ssd["nc"], ssd["nh"])
    (a["yb"], a["dn_st"]), got = scan_fwd(f"dn_fwd_{l}", dn["f"], dn["xs"], w["dn_ps"], dn["ys"], dn["state"], dn["nc"], dn["nh"], rider=dn_rider)
    _use_gathered(w, EARLY, got, l)
    (a["yc"], a["sg_st"]), _ = scan_fwd(f"sg_fwd_{l}", sg["f"], sg["xs"], w["sg_ps"], sg["ys"], sg["state"], sg["nc"], sg["nh"])
    (a["ccol"], a["fc_st"]), _ = scan_fwd(f"foxc_fwd_{l}", fc["f"], fc["xs"], w["fox_ps"], fc["ys"], fc["state"], fc["nc"], fc["nh"])
    a["fox_qa"], a["fox_ka"], a["fox_va"] = fox_prep(a["foxqkv"], a["ccol"], f"fox_prep_{l}")
    (a["yd"], a["lse"]), carried = fox_fwd(a["fox_qa"], a["fox_ka"], a["fox_va"], f"fox_fwd_{l}", rider=fox_rider)
    a["merged"] = merge_fwd([a["ya"], a["yb"], a["yc"], a["yd"]], a["gates"], w["gate_b"], w["w_branch"], 0, f"merge_fwd_{l}")
    a["u1"], a["h1"] = out_fwd(a["merged"], h, w["w_out"], 0, w["ln1_g"], w["ln1_b"], f"out_fwd_{l}")
    a["u2"], a["h2"] = ff_fwd(a["h1"], w["w_up"], w["w_down"], 0, w["ln2_g"], w["ln2_b"], f"ff_fwd_{l}")
    return a, carried


def _layer_bwd(dh2, a, w, l, p, late_above):
    T = dh2.shape[0]
    g = {}
    du2, dh1, da, r, dg2, db2 = ff_bwd(a["u2"], dh2, a["h1"], w["ln2_g"], w["ln2_b"], w["w_up"], w["w_down"], 0, f"ff_bwd_{l}")
    g["ln2_g"], g["ln2_b"] = dg2[0], db2[0]
    g["w_up"] = matmul_w(transpose_bf16(a["h1"], f"h1_t_{l}"), da[None], 0, "nn", f"dwup_{l}", out_dtype=BF)
    g["w_down"] = matmul_w(r, du2[None], 0, "nn", f"dwdown_{l}", out_dtype=BF)
    du1, dmerged, dg1, db1 = out_bwd(a["u1"], dh1, w["ln1_g"], w["ln1_b"], w["w_out"], 0, f"out_bwd_{l}")
    g["ln1_g"], g["ln1_b"] = dg1[0], db1[0]
    g["w_out"] = matmul_w(transpose_bf16(a["merged"], f"merged_t_{l}"), du1[None], 0, "nn", f"dwout_{l}", out_dtype=BF)
    ys = [a["ya"], a["yb"], a["yc"], a["yd"]]
    dya, dyb, dyc, dyd, dgl, dz, dgb = merge_bwd(ys, a["gates"], w["gate_b"], w["w_branch"], 0, dmerged, f"merge_bwd_{l}")
    g["gate_b"] = dgb
    g["w_branch"] = jnp.stack([matmul_tn(ys[i], dz, f"dwb{i}_{l}", b_col0=i * D_MODEL, n_cols=D_MODEL, out_dtype=BF) for i in range(4)])
    early = exchange_rider(_exchange_items(g, p, EARLY))
    dn_rider = early if late_above is None else exchange_rider(late_above)
    fox_rider = None if late_above is None else early
    ssd, dn, sg, fc = _scan_specs(T, a)
    (dz_ssd, dxbc_act, dsm_ssd, d_dtb, d_alog, d_dsk, d_nw), _ = scan_bwd(f"ssd_bwd_{l}", ssd["f"], ssd["xs"], w["ssd_ps"], ssd["ys"], [dya], a["ssd_st"],
                                                                           ssd["state"], ssd["nc"], ssd["nh"])
    g["ssd_dt_bias"], g["ssd_a_log"], g["ssd_d"], g["ssd_norm_w"] = d_dtb[0, DT0:DT0 + 8], d_alog[0, DT0:DT0 + 8], d_dsk[0, DT0:DT0 + 8], d_nw[0]
    dxbc, g["ssd_conv_w"], dcb = conv_bwd(a["xbc"], w["ssd_cw"], w["ssd_cb"], dxbc_act, f"ssd_conv_bwd_{l}")
    g["ssd_conv_b"] = dcb[0]
    (ddn_act, ddngate, dsm_dn, d_alog, d_dtb, d_nw), got_dn = scan_bwd(f"dn_bwd_{l}", dn["f"], dn["xs"], w["dn_ps"], dn["ys"], [dyb], a["dn_st"],
                                                                        dn["state"], dn["nc"], dn["nh"], rider=dn_rider)
    g["dn_a_log"], g["dn_dt_bias"], g["dn_norm_w"] = d_alog[0, A0:A0 + 4], d_dtb[0, A0:A0 + 4], d_nw[0]
    ddnqkv, g["dn_conv_w"], _ = conv_bwd(a["dnqkv"], w["dn_cw"], w["dn_cb"], ddn_act, f"dn_conv_bwd_{l}")
    (dsguv, d_lng, d_lnb, d_w, d_bt), _ = scan_bwd(f"sg_bwd_{l}", sg["f"], sg["xs"], w["sg_ps"], sg["ys"], [dyc], a["sg_st"], sg["state"], sg["nc"], sg["nh"])
    g["sg_ln_g"], g["sg_ln_b"], g["sg_w"], g["sg_b"] = d_lng[0], d_lnb[0], d_w, d_bt[:, :4].T
    qb, doa, qd, kd = fox_prep_bwd(a["foxqkv"], a["fox_qa"], a["yd"], a["lse"], dyd, f"fox_prep_bwd_{l}")
    (dfq, dfk, dfv, dccol), got_fox = fox_bwd(qb, a["fox_ka"], a["fox_va"], doa, qd, kd, f"fox_bwd_{l}", rider=fox_rider)
    (dsm_fox, d_fb), _ = scan_bwd(f"foxc_bwd_{l}", fc["f"], fc["xs"], w["fox_ps"], fc["ys"], [dccol], a["fc_st"], fc["state"], fc["nc"], fc["nh"])
    g["fox_f_bias"] = d_fb[0, FF0:FF0 + 8]
    dseg = {"z": dz_ssd, "xbc": dxbc, "dnqkv": ddnqkv, "dngate": ddngate, "sguv": dsguv,
            "foxqkv": jnp.concatenate([dfq, dfk, dfv], axis=1), "gates": dgl, "small": add3(dsm_ssd, dsm_dn, dsm_fox, f"dsmall_{l}")}
    h_t = transpose_bf16(a["h"], f"h_t_{l}")
    dwin = {n: matmul_w(h_t, dseg[n][None], 0, "nn", f"dwin_{n}_{l}", out_dtype=BF) for n in SEG_NAMES}
    g["w_in"] = repack_dw_in(dwin, f"dw_in_repack_{l}")
    g["dn_conv_w"] = jnp.moveaxis(g["dn_conv_w"].reshape(4, N_DEV, 3 * BRANCH_W // N_DEV), 1, 0)
    got = {(EARLY, l): got_dn} if late_above is None else {(LATE, l + 1): got_dn, (EARLY, l): got_fox}
    return dseg, du1, g, got


def proj_all(h, w_in, name):
    T = h.shape[0]
    tm = min(T, 256)
    ns = len(SEG_NAMES)

    def body(*refs):
        h_ref, w_refs, o_refs = refs[0], refs[1:1 + ns], refs[1 + ns:]
        hb = h_ref[...].astype(BF)
        for w_ref, o_ref in zip(w_refs, o_refs):
            o_ref[...] = _dotb(hb, w_ref[...], NN)

    widths = [w_in[n].shape[2] for n in SEG_NAMES]
    in_specs = [_row(tm, D_MODEL)]
    in_specs += [pl.BlockSpec((None,) + w_in[n].shape[1:], lambda i: (0, 0, 0), pipeline_mode=pl.Buffered(1)) for n in SEG_NAMES]
    outs = _pcall(body, grid=(T // tm,), in_specs=in_specs, out_specs=[_row(tm, wd) for wd in widths],
                  out_shape=[_S((T, wd)) for wd in widths], name=name)(h, *[w_in[n] for n in SEG_NAMES])
    return dict(zip(SEG_NAMES, outs))


def dh_all(dseg, w_in, add, name, rider=None):
    T = add.shape[0]
    tm = min(T, 256)
    ns = len(SEG_NAMES)

    def body(*refs):
        d_refs, w_refs, add_ref, o_ref = refs[:ns], refs[ns:2 * ns], refs[2 * ns], refs[2 * ns + 1]
        acc = ALPHA * add_ref[...]
        for d_ref, w_ref in zip(d_refs, w_refs):
            acc = acc + _dotb(d_ref[...].astype(BF), w_ref[...], NT)
        o_ref[...] = acc

    in_specs = [_row(tm, dseg[n].shape[1]) for n in SEG_NAMES]
    in_specs += [pl.BlockSpec((None,) + w_in[n].shape[1:], lambda i: (0, 0, 0), pipeline_mode=pl.Buffered(1)) for n in SEG_NAMES]
    in_specs.append(_row(tm, D_MODEL))
    return hosted_call(body, rider, grid=(T // tm,), in_specs=in_specs, out_specs=[_row(tm, D_MODEL)], out_shape=[_S((T, D_MODEL))],
                       scratch_shapes=[], name=name, args=[*[dseg[n] for n in SEG_NAMES], *[w_in[n] for n in SEG_NAMES], add])


def kernel(x, ln_in_g, ln_in_b, w_in, ssd_conv_w, ssd_conv_b, ssd_dt_bias, ssd_a_log, ssd_d, ssd_norm_w, dn_conv_w, dn_a_log, dn_dt_bias, dn_norm_w, sg_ln_g, sg_ln_b, sg_w, sg_b, fox_f_bias, gate_b, w_branch, w_out, ln1_g, ln1_b, w_up, w_down, ln2_g, ln2_b, loss_target, m_ln_in_g, m_ln_in_b, m_w_in, m_ssd_conv_w, m_ssd_conv_b, m_ssd_dt_bias, m_ssd_a_log, m_ssd_d, m_ssd_norm_w, m_dn_conv_w, m_dn_a_log, m_dn_dt_bias, m_dn_norm_w, m_sg_ln_g, m_sg_ln_b, m_sg_w, m_sg_b, m_fox_f_bias, m_gate_b, m_w_branch, m_w_out, m_ln1_g, m_ln1_b, m_w_up, m_w_down, m_ln2_g, m_ln2_b, v_ln_in_g, v_ln_in_b, v_w_in, v_ssd_conv_w, v_ssd_conv_b, v_ssd_dt_bias, v_ssd_a_log, v_ssd_d, v_ssd_norm_w, v_dn_conv_w, v_dn_a_log, v_dn_dt_bias, v_dn_norm_w, v_sg_ln_g, v_sg_ln_b, v_sg_w, v_sg_b, v_fox_f_bias, v_gate_b, v_w_branch, v_w_out, v_ln1_g, v_ln1_b, v_w_up, v_w_down, v_ln2_g, v_ln2_b):
    args = dict(locals())
    p = {n: args[n] for n in WEIGHTS}
    xt, target = x[0], loss_target[0]
    ws, acts = [_layer_weights(p, l) for l in range(DEPTH)], []
    _use_gathered(ws[0], LATE, comm_call(_gather_rider(p, 0, LATE), "weights_all_gather_0"), 0)
    h = ln_fwd(xt, ln_in_g[None], ln_in_b[None], "ln_in_fwd")
    for l in range(DEPTH):
        a, gathered = _layer_fwd(h, ws[l], l, dn_rider=_gather_rider(p, 0, EARLY) if l == 0 else None,
                                 fox_rider=_gather_rider(p, l + 1, LATE + EARLY) if l + 1 < DEPTH else None)
        if l + 1 < DEPTH:
            _use_gathered(ws[l + 1], LATE + EARLY, gathered, l + 1)
        acts.append(a)
        h = a["h2"]
    dh, loss = loss_head(h, target, "loss_head")
    loss = lax.psum(loss[0, 0], ("x", "y", "c"))

    layer_grads, got, late = [None] * DEPTH, {}, None
    for l in reversed(range(DEPTH)):
        dseg, du1, layer_grads[l], got_l = _layer_bwd(dh, acts[l], ws[l], l, p, late)
        got.update(got_l)
        late = _exchange_items(layer_grads[l], p, LATE)
        rider = None
        if l == 0:
            pack = _pack_small([jnp.stack([layer_grads[k][n] for k in range(DEPTH)]) for n in REPLICATED_LAYER], 64)
            rider = exchange_rider(late + [(pack[0], 0, -1, pack.shape[1:])])
        (dh,), carried = dh_all(dseg, ws[l]["w_in"], du1, f"dh_{l}", rider=rider)
    got[(LATE, 0)], got_layer_pack = carried[:-1], carried[-1]
    grad_x, dg_in, db_in = ln_bwd(xt, ln_in_g[None], ln_in_b[None], dh, "ln_in_bwd")
    pack = _pack_small([dg_in[0], db_in[0]], 8)
    got_entry_pack = comm_call(exchange_rider([(pack[0], 0, -1, pack.shape[1:])]), "grads_exchange_entry_norm")[0]
    rcv = {(n, l): arr for (names, l), arrs in got.items() for n, arr in zip(names, arrs)}

    res = [{}, {}, {}, {}]
    for n in SHARDED:
        shp = p[n].shape
        lead = math.prod(shp[1:-2])
        to3 = lambda t: t.reshape((-1,) + shp[-2:])
        outs = reduce_adamw([rcv[(n, l)].reshape((N_DEV, lead) + shp[-2:]) for l in range(DEPTH)],
                            to3(p[n]), to3(args["m_" + n]), to3(args["v_" + n]), f"adamw_{n}")
        for k in range(4):
            res[k][n] = outs[k].reshape(shp)
    for names, got_pack, rows, name in ((REPLICATED_LAYER, got_layer_pack, 64, "adamw_replicated"), (REPLICATED_ENTRY, got_entry_pack, 8, "adamw_entry_norm")):
        outs = reduce_adamw([got_pack[:, None]], _pack_small([p[n] for n in names], rows), _pack_small([args["m_" + n] for n in names], rows),
                            _pack_small([args["v_" + n] for n in names], rows), name)
        off = 0
        for n in names:
            shp = p[n].shape
            cnt = math.prod(shp)
            for k in range(4):
                res[k][n] = outs[k].reshape(-1)[off:off + cnt].reshape(shp)
            off += cnt
    return (loss, grad_x[None], *[res[0][n] for n in WEIGHTS], *[res[1][n] for n in WEIGHTS],
            *[res[2][n] for n in WEIGHTS], *[res[3][n] for n in WEIGHTS])
```

```python
import collections
import functools
import math

import jax
import jax.numpy as jnp
from jax import lax
from jax.experimental import pallas as pl
from jax.experimental.pallas import tpu as pltpu

F32 = jnp.float32
BF = jnp.bfloat16

D_MODEL = 1024
DEPTH = 2
BRANCH_W = 512
D_FF = 4096
LN_EPS = 1e-5
NORM_EPS = 1e-6
ALPHA = (2 * DEPTH) ** 0.25
N_DEV = 8
LANES = 128
ADAM_LR, ADAM_B1, ADAM_B2, ADAM_EPS, ADAM_WD, ADAM_STEP = 0.001, 0.9, 0.999, 1e-08, 0.01, 10

DT0, BETA0, A0, FF0 = 0, 8, 12, 16
IN_SIZES = (512, 1024, 8, 1536, 4, 4, 512, 1024, 1536, 8, 4096)
_OFF = [0]
for _s in IN_SIZES:
    _OFF.append(_OFF[-1] + _s)
D_IN = _OFF[-1]
SEGS = (("z", 0), ("xbc", 1), ("dnqkv", 3), ("dngate", 6), ("sguv", 7), ("foxqkv", 8), ("gates", 10))
SMALL_SRC = ((2, DT0), (4, BETA0), (5, A0), (9, FF0))

NN = ((1,), (0,))
NT = ((1,), (1,))
TN = ((0,), (0,))
_DIMS = {"nn": NN, "nt": NT, "tn": TN}


def _pcall(body, **kw):
    return pl.pallas_call(body, **kw)


def _S(shape, dtype=F32):
    return jax.ShapeDtypeStruct(tuple(shape), dtype)


def _iota(shape, dim):
    return lax.broadcasted_iota(jnp.int32, shape, dim)


def _dotb(a, b, dims):
    return lax.dot_general(a, b, (dims, ((), ())), preferred_element_type=F32)


def _split2(a):
    ah = a.astype(BF)
    return ah, (a - ah.astype(F32)).astype(BF)


def _split3(a):
    a1 = a.astype(BF)
    r = a - a1.astype(F32)
    a2 = r.astype(BF)
    a3 = (r - a2.astype(F32)).astype(BF)
    return a1, a2, a3


def _mm_raw(a, b, form, mode):
    d = _DIMS[form]
    if mode == "1":
        return _dotb(a.astype(BF), b.astype(BF), d)
    if mode == "3":
        ah, al = _split2(a)
        bh, bl = _split2(b)
        return _dotb(ah, bh, d) + (_dotb(ah, bl, d) + _dotb(al, bh, d))
    if mode == "xa":
        ab = a.astype(BF)
        b1, b2, b3 = _split3(b)
        return _dotb(ab, b1, d) + (_dotb(ab, b2, d) + _dotb(ab, b3, d))
    bb = b.astype(BF)
    a1, a2, a3 = _split3(a)
    return _dotb(a1, bb, d) + (_dotb(a2, bb, d) + _dotb(a3, bb, d))


@functools.partial(jax.custom_vjp, nondiff_argnums=(2, 3))
def mm(a, b, form, mode):
    return _mm_raw(a, b, form, mode)


def _mm_fwd(a, b, form, mode):
    return _mm_raw(a, b, form, mode), (a, b)


_XA_DB = {"nn": "xa", "nt": "xb", "tn": "xa"}
_XB_DA = {"nn": "xb", "nt": "xb", "tn": "xa"}


def _mm_bwd(form, mode, res, g):
    a, b = res
    ma = _XB_DA[form] if mode == "xb" else mode
    mb = _XA_DB[form] if mode == "xa" else mode
    da = db = None
    if mode != "xa":
        da = {"nn": lambda: mm(g, b, "nt", ma), "nt": lambda: mm(g, b, "nn", ma), "tn": lambda: mm(b, g, "nt", ma)}[form]()
    if mode != "xb":
        db = {"nn": lambda: mm(a, g, "tn", mb), "nt": lambda: mm(g, a, "tn", mb), "tn": lambda: mm(a, g, "nn", mb)}[form]()
    if da is None:
        da = jnp.zeros_like(a)
    if db is None:
        db = jnp.zeros_like(b)
    return da, db


mm.defvjp(_mm_fwd, _mm_bwd)


def _silu(x):
    return x * jax.nn.sigmoid(x)


def _ln(x, g, b):
    mu = jnp.mean(x, -1, keepdims=True)
    xc = x - mu
    var = jnp.mean(xc * xc, -1, keepdims=True)
    return xc * lax.rsqrt(var + LN_EPS) * g + b


def _pick(n, cap):
    if n <= cap:
        return n
    best = LANES
    for t in range(LANES, cap + 1, LANES):
        if n % t == 0:
            best = t
    return best


def transpose_bf16(a, name):
    T, C = a.shape
    tt = min(T, 512)

    def body(a_ref, o_ref):
        o_ref[...] = a_ref[...].T.astype(BF)

    return _pcall(body, grid=(T // tt,), in_specs=[pl.BlockSpec((tt, C), lambda t: (t, 0))], out_specs=pl.BlockSpec((C, tt), lambda t: (0, t)),
                  out_shape=_S((C, T), BF), name=name)(a)


def matmul_w(a, w, l, form, name, add=None, add_scale=1.0, out_dtype=F32):
    M, K = a.shape
    N = w.shape[2] if form == "nn" else w.shape[1]
    tm, tn, tk = min(M, 512), _pick(N, 1024), _pick(K, 1024)
    nk = K // tk

    def body(*refs):
        if add is None:
            a_ref, b_ref, o_ref, acc = refs
        else:
            a_ref, b_ref, d_ref, o_ref, acc = refs
        k = pl.program_id(2)
        p = _dotb(a_ref[...].astype(BF), b_ref[...].astype(BF), _DIMS[form])

        @pl.when(k == 0)
        def _():
            acc[...] = p

        @pl.when(k > 0)
        def _():
            acc[...] += p

        @pl.when(k == nk - 1)
        def _():
            r = acc[...]
            if add is not None:
                r = r + add_scale * d_ref[...]
            o_ref[...] = r.astype(out_dtype)

    if form == "nn":
        wspec = pl.BlockSpec((None, tk, tn), lambda j, i, k: (l, k, j))
    else:
        wspec = pl.BlockSpec((None, tn, tk), lambda j, i, k: (l, j, k))
    in_specs = [pl.BlockSpec((tm, tk), lambda j, i, k: (i, k)), wspec]
    args = [a, w]
    if add is not None:
        in_specs.append(pl.BlockSpec((tm, tn), lambda j, i, k: (i, j)))
        args.append(add)
    return _pcall(body, grid=(N // tn, M // tm, nk), in_specs=in_specs,
                  out_specs=pl.BlockSpec((tm, tn), lambda j, i, k: (i, j)), out_shape=_S((M, N), out_dtype),
                  scratch_shapes=[pltpu.VMEM((tm, tn), F32)], name=name)(*args)


def matmul_tn(a, b, name, b_col0=0, n_cols=None, out_dtype=F32):
    T, M = a.shape
    N = b.shape[1] if n_cols is None else n_cols
    tm, tn, tt = _pick(M, 512), _pick(N, 1024), min(T, 512)
    nt = T // tt
    jb = b_col0 // tn

    def body(a_ref, b_ref, o_ref, acc):
        t = pl.program_id(2)
        p = _dotb(a_ref[...].astype(BF), b_ref[...].astype(BF), TN)

        @pl.when(t == 0)
        def _():
            acc[...] = p

        @pl.when(t > 0)
        def _():
            acc[...] += p

        @pl.when(t == nt - 1)
        def _():
            o_ref[...] = acc[...].astype(out_dtype)

    return _pcall(body, grid=(M // tm, N // tn, nt),
                  in_specs=[pl.BlockSpec((tt, tm), lambda i, j, t: (t, i)), pl.BlockSpec((tt, tn), lambda i, j, t: (t, jb + j))],
                  out_specs=pl.BlockSpec((tm, tn), lambda i, j, t: (i, j)), out_shape=_S((M, N), out_dtype),
                  scratch_shapes=[pltpu.VMEM((tm, tn), F32)], name=name)(a, b)


def _pieces(v):
    if v.ndim == 3:
        return [v[i] for i in range(v.shape[0])]
    n = v.shape[1] // LANES
    if n <= 1:
        return [v]
    return [v[:, i * LANES:(i + 1) * LANES] for i in range(n)]


def _join(ps, like_ndim):
    if like_ndim == 3:
        return jnp.stack(ps, axis=0)
    return ps[0] if len(ps) == 1 else jnp.concatenate(ps, axis=1)


def scan_fwd(name, f, xs, ps, ys, state_shape, nc, nh=1, rider=None):
    nx, npar, ny = len(xs), len(ps), len(ys)

    def body(*refs):
        x_refs, p_refs = refs[:nx], refs[nx:nx + npar]
        y_refs = refs[nx + npar:nx + npar + ny]
        st_out, st = refs[nx + npar + ny], refs[nx + npar + ny + 1]
        c, h = pl.program_id(0), pl.program_id(1)

        @pl.when(c == 0)
        def _():
            st[h] = jnp.zeros(state_shape, F32)

        S = st[h]
        st_out[...] = S
        yv, Sn = f([_pieces(r[...]) for r in x_refs], [_pieces(r[...]) for r in p_refs], _pieces(S), h)
        for r, v in zip(y_refs, yv):
            r[...] = _join(v, 2)
        st[h] = _join(Sn, 3)

    in_specs = [pl.BlockSpec(bs, im) for (_, bs, im) in xs]
    in_specs += [pl.BlockSpec(p.shape, (lambda c, h, n=p.ndim: (0,) * n)) for p in ps]
    out_specs = [pl.BlockSpec(bs, im) for (_, bs, im) in ys]
    out_specs.append(pl.BlockSpec((None, None) + tuple(state_shape), lambda c, h: (c, h, 0, 0, 0)))
    out_shape = [_S(s) for (s, _, _) in ys] + [_S((nc, nh) + tuple(state_shape))]
    return hosted_call(body, rider, grid=(nc, nh), in_specs=in_specs, out_specs=out_specs, out_shape=out_shape,
                       scratch_shapes=[pltpu.VMEM((nh,) + tuple(state_shape), F32)], name=name, args=[*[x[0] for x in xs], *ps])


def scan_bwd(name, f, xs, ps, ys, dys, states, state_shape, nc, nh=1, shared=(), rider=None):
    nx, npar, ny = len(xs), len(ps), len(ys)

    def body(*refs):
        x_refs, p_refs = refs[:nx], refs[nx:nx + npar]
        s_ref = refs[nx + npar]
        dy_refs = refs[nx + npar + 1:nx + npar + 1 + ny]
        o = nx + npar + 1 + ny
        dx_refs, dp_refs, dst = refs[o:o + nx], refs[o + nx:o + nx + npar], refs[o + nx + npar]
        c, h = pl.program_id(0), pl.program_id(1)

        @pl.when(c == 0)
        def _():
            dst[h] = jnp.zeros(state_shape, F32)

        @pl.when((c == 0) & (h == 0))
        def _():
            for r in dp_refs:
                r[...] = jnp.zeros(r.shape, F32)

        xv = [_pieces(r[...]) for r in x_refs]
        pv = [_pieces(r[...]) for r in p_refs]
        _, vjp = jax.vjp(lambda a, b, s: f(a, b, s, h), xv, pv, _pieces(s_ref[...]))
        dxv, dpv, dS = vjp(([_pieces(r[...]) for r in dy_refs], _pieces(dst[h])))
        for i, (r, v) in enumerate(zip(dx_refs, dxv)):
            if i in shared and nh > 1:
                @pl.when(h == 0)
                def _(r=r, v=v):
                    r[...] = _join(v, 2)

                @pl.when(h > 0)
                def _(r=r, v=v):
                    r[...] += _join(v, 2)
            else:
                r[...] = _join(v, 2)
        for r, v in zip(dp_refs, dpv):
            r[...] += _join(v, len(r.shape))
        dst[h] = _join(dS, 3)

    def rev(im):
        return lambda c, h: im(nc - 1 - c, h)

    in_specs = [pl.BlockSpec(bs, rev(im)) for (_, bs, im) in xs]
    in_specs += [pl.BlockSpec(p.shape, (lambda c, h, n=p.ndim: (0,) * n)) for p in ps]
    in_specs.append(pl.BlockSpec((None, None) + tuple(state_shape), lambda c, h: (nc - 1 - c, h, 0, 0, 0)))
    in_specs += [pl.BlockSpec(bs, rev(im)) for (_, bs, im) in ys]
    out_specs = [pl.BlockSpec(bs, rev(im)) for (_, bs, im) in xs]
    out_specs += [pl.BlockSpec(p.shape, (lambda c, h, n=p.ndim: (0,) * n)) for p in ps]
    out_shape = [_S(x[0].shape) for x in xs] + [_S(p.shape) for p in ps]
    return hosted_call(body, rider, grid=(nc, nh), in_specs=in_specs, out_specs=out_specs, out_shape=out_shape,
                       scratch_shapes=[pltpu.VMEM((nh,) + tuple(state_shape), F32)], name=name,
                       args=[*[x[0] for x in xs], *ps, states, *dys])


def _lane():
    return _iota((1, LANES), 1)


def _col(v, idx):
    return jnp.sum(v * (_lane() == idx).astype(F32), axis=1, keepdims=True)


def _last_row(v):
    r = v.shape[0]
    return jnp.sum(v * (_iota((r, 1), 0) == r - 1).astype(F32), axis=0, keepdims=True)


def _tril(n, strict=False):
    r, c = _iota((n, n), 0), _iota((n, n), 1)
    return (r > c) if strict else (r >= c)


def ssd_chunk(xs, ps, S, h):
    zp, xbc, (sm,) = xs
    (bias,), (alog,), (dsk,), nw = ps
    Q = sm.shape[0]
    H = range(8)
    lane = _lane()
    a128 = jnp.where(lane < 8, -jnp.exp(alog), 0.0)
    dtl = jax.nn.softplus(sm + bias)
    tri = _tril(Q)
    cum = mm(tri.astype(F32), dtl * a128, "nn", "xa")
    sel8 = (_iota((8, LANES), 0) == _iota((8, LANES), 1)).astype(F32)
    cum_t = mm(sel8, cum, "nt", "xa")
    m0 = (lane < 64).astype(F32)
    rows0 = (_iota((LANES, 1), 0) < 64).astype(F32)
    me = [m0 if hh % 2 == 0 else 1.0 - m0 for hh in H]
    re = [rows0 if hh % 2 == 0 else 1.0 - rows0 for hh in H]
    Bm, Cm = [xbc[4 + hh // 4] for hh in H], [xbc[6 + hh // 4] for hh in H]
    cb = [mm(xbc[6 + g], xbc[4 + g], "nt", "1") for g in range(2)]
    col = [_col(cum, hh) for hh in H]
    row = [jnp.sum(cum_t * (_iota((8, 1), 0) == hh).astype(F32), axis=0, keepdims=True) for hh in H]
    xh = [xbc[hh // 2] * me[hh] for hh in H]
    xdt = [xh[hh] * _col(dtl, hh) for hh in H]
    seg = [jnp.exp(jnp.where(tri, col[hh] - row[hh], -jnp.inf)) for hh in H]
    last = [_last_row(col[hh]) for hh in H]
    y_diag = [mm(cb[hh // 4] * seg[hh], xdt[hh], "nn", "1") for hh in H]
    y_off = [mm(Cm[hh] * jnp.exp(col[hh]), S[hh // 2], "nt", "1") * me[hh] for hh in H]
    st = [mm(xdt[hh], Bm[hh] * jnp.exp(last[hh] - col[hh]), "tn", "1") for hh in H]
    y = [y_diag[hh] + y_off[hh] + _col(dsk, hh) * xh[hh] for hh in H]
    Sn = [S[pr] * (jnp.exp(last[2 * pr]) * re[0] + jnp.exp(last[2 * pr + 1]) * re[1]) + st[2 * pr] + st[2 * pr + 1] for pr in range(4)]
    yz = [(y[2 * pr] + y[2 * pr + 1]) * _silu(zp[pr]) for pr in range(4)]
    ssq = sum(jnp.sum(v * v, axis=1, keepdims=True) for v in yz)
    scale = lax.rsqrt(ssq / BRANCH_W + NORM_EPS)
    return [[yz[i] * scale * nw[i] for i in range(4)]], Sn


def dn_chunk(xs, ps, S, h):
    act, gate, (sm,) = xs
    (alog,), (dtb,), (nw,) = ps
    H = range(4)
    C = sm.shape[0]
    lane = _lane()
    G = jnp.where((lane >= A0) & (lane < A0 + 4), -jnp.exp(alog) * jax.nn.softplus(sm + dtb), 0.0)
    tri, strict = _tril(C), _tril(C, True)
    gcs = mm(tri.astype(F32), G, "nn", "xa")
    sig = jax.nn.sigmoid(sm)
    qn = [act[h] * lax.rsqrt(jnp.sum(act[h] * act[h], axis=1, keepdims=True) + NORM_EPS) * (LANES ** -0.5) for h in H]
    kn = [act[4 + h] * lax.rsqrt(jnp.sum(act[4 + h] * act[4 + h], axis=1, keepdims=True) + NORM_EPS) for h in H]
    beta = [_col(sig, BETA0 + h) for h in H]
    gcol = [_col(gcs, A0 + h) for h in H]
    selr = [((_iota((8, LANES), 0) == 0) & (_iota((8, LANES), 1) == A0 + h)).astype(F32) for h in H]
    grow = [jnp.sum(mm(selr[h], gcs, "nt", "xa"), axis=0, keepdims=True) for h in H]
    gamma = [jnp.exp(jnp.where(tri, gcol[h] - grow[h], -jnp.inf)) for h in H]
    kb = [kn[h] * beta[h] for h in H]
    pk = [-(mm(kb[h], kn[h], "nt", "1") * jnp.where(strict, gamma[h], 0.0)) for h in H]
    eye = (_iota((C, C), 0) == _iota((C, C), 1)).astype(F32)
    minv = [eye + pk[h] for h in H]
    for _ in range(5):
        pk = [mm(pk[h], pk[h], "nn", "3") for h in H]
        minv = [minv[h] + mm(minv[h], pk[h], "nn", "3") for h in H]
    eg = [jnp.exp(gcol[h]) for h in H]
    w = [mm(minv[h], kb[h] * eg[h], "nn", "3") for h in H]
    u = [mm(minv[h], act[8 + h] * beta[h], "nn", "3") for h in H]
    glast = [_last_row(gcol[h]) for h in H]
    vnew = [u[h] - mm(w[h], S[h], "nn", "1") for h in H]
    qk = [mm(qn[h], kn[h], "nt", "1") * gamma[h] for h in H]
    o = [mm(qn[h] * eg[h], S[h], "nn", "1") + mm(qk[h], vnew[h], "nn", "1") for h in H]
    Sn = [S[h] * jnp.exp(glast[h]) + mm(kn[h] * jnp.exp(glast[h] - gcol[h]), vnew[h], "tn", "1") for h in H]
    on = [o[h] * lax.rsqrt(jnp.mean(o[h] * o[h], axis=1, keepdims=True) + NORM_EPS) * nw for h in H]
    return [[on[h] * _silu(gate[h]) for h in H]], Sn


def sg_chunk(xs, ps, S, h):
    (uv,) = xs
    lng, lnb, W, (bt,) = ps
    u = [jax.nn.gelu(p) for p in uv[:4]]
    v = [jax.nn.gelu(p) for p in uv[4:]]
    mu = sum(jnp.sum(p, axis=1, keepdims=True) for p in v) / BRANCH_W
    vc = [p - mu for p in v]
    var = sum(jnp.sum(p * p, axis=1, keepdims=True) for p in vc) / BRANCH_W
    inv = lax.rsqrt(var + LN_EPS)
    trif = _tril(W[0].shape[0]).astype(F32)
    out = []
    for g in range(4):
        vn = vc[g] * inv * lng[g] + lnb[g]
        out.append(u[g] * (mm(W[g] * trif, vn, "nn", "1") + _col(bt, g)))
    return [out], S


def foxc_chunk(xs, ps, S, h):
    (sm,), ((fb,),), (carry,) = xs[0], ps, S
    lane = _lane()
    ls = jnp.where((lane >= FF0) & (lane < FF0 + 8), jax.nn.log_sigmoid(sm + fb), 0.0)
    c = mm(_tril(sm.shape[0]).astype(F32), ls, "nn", "xa") + carry
    return [[c]], [_last_row(c)]


HALO = 8


def _conv_tiles(T, C):
    return min(T, 512), _pick(C, 512)


def conv_fwd(x, w, b, name):
    T, C = x.shape
    tm, cb = _conv_tiles(T, C)

    def body(xp_ref, x_ref, w_ref, b_ref, o_ref):
        i = pl.program_id(1)
        e = jnp.concatenate([xp_ref[...] * (i > 0).astype(F32), x_ref[...]], axis=0)
        pre = b_ref[...] + sum(w_ref[k:k + 1, :] * e[5 + k:5 + k + tm, :] for k in range(4))
        o_ref[...] = _silu(pre)

    hb = tm // HALO
    return _pcall(body, grid=(C // cb, T // tm),
                  in_specs=[pl.BlockSpec((HALO, cb), lambda j, i: (jnp.maximum(i * hb - 1, 0), j)), pl.BlockSpec((tm, cb), lambda j, i: (i, j)),
                            pl.BlockSpec((4, cb), lambda j, i: (0, j)), pl.BlockSpec((1, cb), lambda j, i: (0, j))],
                  out_specs=pl.BlockSpec((tm, cb), lambda j, i: (i, j)), out_shape=_S((T, C)), name=name)(x, x, w, b)


def conv_bwd(x, w, b, dact, name):
    T, C = x.shape
    tm, cb = _conv_tiles(T, C)
    nt = T // tm

    def body(xp_ref, x_ref, xn_ref, w_ref, b_ref, d_ref, dn_ref, dx_ref, dw_ref, db_ref):
        i = pl.program_id(1)
        has_prev, has_next = (i > 0).astype(F32), (i < nt - 1).astype(F32)
        e = jnp.concatenate([xp_ref[...] * has_prev, x_ref[...], xn_ref[...] * has_next], axis=0)
        pre = b_ref[...] + sum(w_ref[k:k + 1, :] * e[5 + k:5 + k + tm + 8, :] for k in range(4))
        de = jnp.concatenate([d_ref[...], dn_ref[...] * has_next], axis=0)
        sg = jax.nn.sigmoid(pre)
        dpre = de * (sg * (1.0 + pre * (1.0 - sg)))
        dx_ref[...] = sum(w_ref[k:k + 1, :] * dpre[3 - k:3 - k + tm, :] for k in range(4))
        dcur = dpre[0:tm, :]
        dw = jnp.concatenate([jnp.sum(dcur * e[5 + k:5 + k + tm, :], axis=0, keepdims=True) for k in range(4)], axis=0)
        db = jnp.sum(dcur, axis=0, keepdims=True)

        @pl.when(i == 0)
        def _():
            dw_ref[...] = dw
            db_ref[...] = db

        @pl.when(i > 0)
        def _():
            dw_ref[...] += dw
            db_ref[...] += db

    blk = lambda f: pl.BlockSpec((tm, cb), f)
    hb = tm // HALO
    before = pl.BlockSpec((HALO, cb), lambda j, i: (jnp.maximum(i * hb - 1, 0), j))
    after = pl.BlockSpec((HALO, cb), lambda j, i: (jnp.minimum((i + 1) * hb, nt * hb - 1), j))
    return _pcall(body, grid=(C // cb, nt),
                  in_specs=[before, blk(lambda j, i: (i, j)), after,
                            pl.BlockSpec((4, cb), lambda j, i: (0, j)), pl.BlockSpec((1, cb), lambda j, i: (0, j)),
                            blk(lambda j, i: (i, j)), after],
                  out_specs=[blk(lambda j, i: (i, j)), pl.BlockSpec((4, cb), lambda j, i: (0, j)), pl.BlockSpec((1, cb), lambda j, i: (0, j))],
                  out_shape=[_S((T, C)), _S((4, C)), _S((1, C))], name=name)(x, x, x, w, b, dact, dact)


FOX_SCALE = 64 ** -0.5
LOG2E = 1.4426950408889634


def _spare(e, i):
    return (_lane() == 64 * (1 - e) + i).astype(F32)


def _lanes_of(e):
    lane = _lane()
    return ((lane < 64) if e == 0 else (lane >= 64)).astype(F32)


def _col3(col, e, first):
    c1 = col.astype(BF).astype(F32)
    c2 = (col - c1).astype(BF).astype(F32)
    c3 = (col - c1 - c2).astype(BF).astype(F32)
    return c1 * _spare(e, first) + c2 * _spare(e, first + 1) + c3 * _spare(e, first + 2)


def _ones3(e, first):
    return _spare(e, first) + _spare(e, first + 1) + _spare(e, first + 2)


def _causal_bias(n):
    return jnp.where(_iota((n, n), 0) >= _iota((n, n), 1), 0.0, -jnp.inf).astype(F32)


def _c_col(cc, hh):
    return jnp.sum(cc * (_lane() == FF0 + hh).astype(F32), axis=1, keepdims=True) * LOG2E


def _pair_spec(tq, row_of):
    return pl.BlockSpec((None, 2, tq, LANES), lambda hp, a, b: (hp, 0, row_of(a, b), 0))


def fox_prep(qkv, ccol, name):
    T = qkv.shape[0]
    tq = min(T, 512)

    def body(q_ref, k_ref, v_ref, cc_ref, qa_ref, ka_ref, va_ref):
        hp = pl.program_id(0)
        q, k, v, cc = q_ref[...], k_ref[...], v_ref[...], cc_ref[...]
        for e in range(2):
            me = _lanes_of(e)
            c2 = _c_col(cc, 2 * hp + e)
            qa_ref[e] = (q * me * (FOX_SCALE * LOG2E) + _col3(c2, e, 0) + _ones3(e, 3)).astype(BF)
            ka_ref[e] = (k * me + _ones3(e, 0) + _col3(-c2, e, 3) + _ones3(e, 6)).astype(BF)
            va_ref[e] = (v * me + (1.0 - me)).astype(BF)

    blk = lambda off: pl.BlockSpec((tq, LANES), lambda hp, i: (i, off + hp))
    out = pl.BlockSpec((None, 2, tq, LANES), lambda hp, i: (hp, 0, i, 0))
    return _pcall(body, grid=(4, T // tq), in_specs=[blk(0), blk(4), blk(8), pl.BlockSpec((tq, LANES), lambda hp, i: (i, 0))],
                  out_specs=[out] * 3, out_shape=[_S((4, 2, T, LANES), BF)] * 3, name=name)(qkv, qkv, qkv, ccol)


def fox_fwd(qa, ka, va, name, rider=None):
    T = qa.shape[2]
    tq = min(T, 512)
    nq = T // tq

    def body(qa_ref, ka_ref, va_ref, o_ref, lse_ref, m_s, acc, causal):
        i, j = pl.program_id(1), pl.program_id(2)

        @pl.when((pl.program_id(0) == 0) & (i == 0) & (j == 0))
        def _():
            causal[...] = _causal_bias(tq)

        @pl.when(j == 0)
        def _():
            m_s[...] = jnp.full(m_s.shape, -jnp.inf, F32)
            acc[...] = jnp.zeros(acc.shape, F32)

        def step(diagonal):
            for e in range(2):
                s = _dotb(qa_ref[e], ka_ref[e], NT)
                if diagonal:
                    s = s + causal[...]
                m_old = m_s[e]
                m_new = jnp.maximum(m_old, jnp.max(s, axis=1, keepdims=True))
                p = jnp.exp2(s - m_new)
                m_s[e] = m_new
                acc[e] = acc[e] * jnp.exp2(m_old - m_new) + _dotb(p.astype(BF), va_ref[e], NN)

        @pl.when(j < i)
        def _():
            step(False)

        @pl.when(j == i)
        def _():
            step(True)
            lane = _lane()
            o, lse = 0.0, 0.0
            for e in range(2):
                l = jnp.sum(acc[e] * _spare(e, 0), axis=1, keepdims=True)
                o = o + acc[e] * _lanes_of(e) / l
                lse = lse + (m_s[e] + jnp.log2(l)) * (lane == e).astype(F32)
            o_ref[...] = o
            lse_ref[...] = lse

    kv = _pair_spec(tq, lambda i, j: jnp.minimum(j, i))
    return hosted_call(body, rider, grid=(4, nq, nq), in_specs=[_pair_spec(tq, lambda i, j: i), kv, kv],
                       out_specs=[pl.BlockSpec((tq, LANES), lambda hp, i, j: (i, hp)), pl.BlockSpec((None, tq, LANES), lambda hp, i, j: (hp, i, 0))],
                       out_shape=[_S((T, BRANCH_W)), _S((4, T, LANES))],
                       scratch_shapes=[pltpu.VMEM((2, tq, 1), F32), pltpu.VMEM((2, tq, LANES), F32), pltpu.VMEM((tq, tq), F32)],
                       name=name, args=[qa, ka, va])


def fox_prep_bwd(qkv, qa, o, lse, do, name):
    T = qkv.shape[0]
    tq = min(T, 512)

    def body(q_ref, k_ref, qa_ref, o_ref, lse_ref, do_ref, qb_ref, doa_ref, qd_ref, kd_ref):
        q, k, dov = q_ref[...], k_ref[...], do_ref[...]
        dd = dov * o_ref[...]
        lane = _lane()
        for e in range(2):
            me = _lanes_of(e)
            lse_e = jnp.sum(lse_ref[...] * (lane == e).astype(F32), axis=1, keepdims=True)
            qb_ref[e] = (qa_ref[e].astype(F32) + _col3(-lse_e, e, 6)).astype(BF)
            doa_ref[e] = (dov * me + _col3(-jnp.sum(dd * me, axis=1, keepdims=True), e, 0)).astype(BF)
            qd_ref[e] = (q * me * FOX_SCALE + _spare(e, 0)).astype(BF)
            kd_ref[e] = (k * me * FOX_SCALE + _spare(e, 0)).astype(BF)

    blk = lambda off: pl.BlockSpec((tq, LANES), lambda hp, i: (i, off + hp))
    pair = pl.BlockSpec((None, 2, tq, LANES), lambda hp, i: (hp, 0, i, 0))
    return _pcall(body, grid=(4, T // tq),
                  in_specs=[blk(0), blk(4), pair, blk(0), pl.BlockSpec((None, tq, LANES), lambda hp, i: (hp, i, 0)), blk(0)],
                  out_specs=[pair] * 4, out_shape=[_S((4, 2, T, LANES), BF)] * 4, name=name)(qkv, qkv, qa, o, lse, do)


def fox_bwd(qb, ka, va, doa, qd, kd, name, rider=None):
    T = qb.shape[2]
    tq = min(T, 512)
    nq = T // tq

    def body(qb_ref, ka_ref, va_ref, doa_ref, qd_ref, kd_ref, dq_ref, dk_ref, dv_ref, dcc_ref, dq_s, dk_s, dv_s, causal):
        hp, j, ii = pl.program_id(0), pl.program_id(1), pl.program_id(2)
        i = jnp.maximum(ii, j)

        @pl.when((hp == 0) & (j == 0) & (ii == 0))
        def _():
            dcc_ref[...] = jnp.zeros(dcc_ref.shape, F32)
            causal[...] = _causal_bias(tq)

        @pl.when((j == 0) & (ii == 0))
        def _():
            dq_s[...] = jnp.zeros(dq_s.shape, F32)

        @pl.when(ii == 0)
        def _():
            dk_s[...] = jnp.zeros(dk_s.shape, F32)
            dv_s[...] = jnp.zeros(dv_s.shape, F32)

        def step(diagonal):
            rows = pl.ds(pl.multiple_of(i * tq, tq), tq)
            for e in range(2):
                s = _dotb(qb_ref[e], ka_ref[e], NT)
                if diagonal:
                    s = s + causal[...]
                p = jnp.exp2(s)
                ds = (p * _dotb(doa_ref[e], va_ref[e], NT)).astype(BF)
                dv_s[e] += _dotb(p.astype(BF), doa_ref[e], TN)
                dq_s[e, rows, :] += _dotb(ds, kd_ref[e], NN)
                dk_s[e] += _dotb(ds, qd_ref[e], TN)

        @pl.when(ii > j)
        def _():
            step(False)

        @pl.when(ii == j)
        def _():
            step(True)

        def fold(acc, sign):
            grad, dc = 0.0, 0.0
            for e in range(2):
                a = acc[e]
                grad = grad + a * _lanes_of(e)
                dc = dc + sign * jnp.sum(a * _spare(e, 0), axis=1, keepdims=True) * (_lane() == FF0 + 2 * hp + e).astype(F32)
            return grad, dc

        @pl.when(ii == nq - 1)
        def _():
            grad, dc = fold(dk_s, -1.0)
            dk_ref[...] = grad
            dv_ref[...] = dv_s[0] * _lanes_of(0) + dv_s[1] * _lanes_of(1)
            dcc_ref[pl.ds(pl.multiple_of(j * tq, tq), tq), :] += dc

        @pl.when((j == nq - 1) & (ii == nq - 1))
        def _():
            grad, dc = fold(dq_s, 1.0)
            dq_ref[...] = grad
            dcc_ref[...] += dc

    irow, jrow = _pair_spec(tq, lambda j, ii: jnp.maximum(ii, j)), _pair_spec(tq, lambda j, ii: j)
    jout = pl.BlockSpec((tq, LANES), lambda hp, j, ii: (j, hp))
    return hosted_call(body, rider, grid=(4, nq, nq), in_specs=[irow, jrow, jrow, irow, irow, jrow],
                       out_specs=[pl.BlockSpec((T, LANES), lambda hp, j, ii: (0, hp)), jout, jout, pl.BlockSpec((T, LANES), lambda hp, j, ii: (0, 0))],
                       out_shape=[_S((T, BRANCH_W)), _S((T, BRANCH_W)), _S((T, BRANCH_W)), _S((T, LANES))],
                       scratch_shapes=[pltpu.VMEM((2, T, LANES), F32), pltpu.VMEM((2, tq, LANES), F32), pltpu.VMEM((2, tq, LANES), F32),
                                       pltpu.VMEM((tq, tq), F32)],
                       name=name, args=[qb, ka, va, doa, qd, kd])


def _acc_out(ref, val, first):
    @pl.when(first)
    def _():
        ref[...] = val

    @pl.when(jnp.logical_not(first))
    def _():
        ref[...] += val


def _row(tm, c):
    return pl.BlockSpec((tm, c), lambda i: (i, 0))


def _full(shape):
    return pl.BlockSpec(shape, lambda *_: (0,) * len(shape))


def ln_fwd(x, g, b, name):
    T, C = x.shape
    tm = min(T, 512)

    def body(x_ref, g_ref, b_ref, o_ref):
        o_ref[...] = _ln(x_ref[...], g_ref[...], b_ref[...])

    return _pcall(body, grid=(T // tm,), in_specs=[_row(tm, C), _full((1, C)), _full((1, C))], out_specs=_row(tm, C),
                  out_shape=_S((T, C)), name=name)(x, g, b)


def ln_bwd(x, g, b, dy, name):
    T, C = x.shape
    tm = min(T, 512)

    def body(x_ref, g_ref, b_ref, dy_ref, dx_ref, dg_ref, db_ref):
        _, vjp = jax.vjp(_ln, x_ref[...], g_ref[...], b_ref[...])
        dx, dg, db = vjp(dy_ref[...])
        dx_ref[...] = dx
        first = pl.program_id(0) == 0
        _acc_out(dg_ref, dg, first)
        _acc_out(db_ref, db, first)

    return _pcall(body, grid=(T // tm,), in_specs=[_row(tm, C), _full((1, C)), _full((1, C)), _row(tm, C)],
                  out_specs=[_row(tm, C), _full((1, C)), _full((1, C))], out_shape=[_S((T, C)), _S((1, C)), _S((1, C))], name=name)(x, g, b, dy)


def loss_head(h, target, name):
    T, C = h.shape
    tm = min(T, 512)

    def body(h_ref, t_ref, d_ref, l_ref):
        e = h_ref[...] - t_ref[...]
        d_ref[...] = e * (1.0 / C)
        part = jnp.sum(jnp.sum(e * e, axis=1, keepdims=True), axis=0, keepdims=True) * (0.5 / C)
        _acc_out(l_ref, part, pl.program_id(0) == 0)

    return _pcall(body, grid=(T // tm,), in_specs=[_row(tm, C), _row(tm, C)], out_specs=[_row(tm, C), _full((1, 1))],
                  out_shape=[_S((T, C)), _S((1, 1))], name=name)(h, target)


def add3(a, b, c, name):
    T, C = a.shape
    tm = min(T, 512)

    def body(a_ref, b_ref, c_ref, o_ref):
        o_ref[...] = a_ref[...] + b_ref[...] + c_ref[...]

    return _pcall(body, grid=(T // tm,), in_specs=[_row(tm, C)] * 3, out_specs=_row(tm, C), out_shape=_S((T, C)), name=name)(a, b, c)


def _wb_spec(l):
    return pl.BlockSpec((None, 4, BRANCH_W, D_MODEL), lambda *_: (l, 0, 0, 0))


def merge_fwd(ys, gl, gb, wb, l, name):
    T = gl.shape[0]
    tm = min(T, 256)

    def body(y0, y1, y2, y3, gl_ref, gb_ref, wb_ref, o_ref):
        acc = 0.0
        for i, y in enumerate((y0, y1, y2, y3)):
            z = _dotb(y[...].astype(BF), wb_ref[i], NN)
            g = jax.nn.sigmoid(gl_ref[:, i * D_MODEL:(i + 1) * D_MODEL] + gb_ref[i:i + 1, :])
            acc = acc + g * z
        o_ref[...] = acc

    return _pcall(body, grid=(T // tm,), in_specs=[_row(tm, BRANCH_W)] * 4 + [_row(tm, 4 * D_MODEL), _full((4, D_MODEL)), _wb_spec(l)],
                  out_specs=_row(tm, D_MODEL), out_shape=_S((T, D_MODEL)), name=name)(*ys, gl, gb, wb)


def merge_bwd(ys, gl, gb, wb, l, dm, name):
    T = gl.shape[0]
    tm = min(T, 256)

    def body(y0, y1, y2, y3, gl_ref, gb_ref, wb_ref, dm_ref, d0, d1, d2, d3, dgl_ref, dz_ref, dgb_ref):
        dmv = dm_ref[...]
        first = pl.program_id(0) == 0
        for i, (y, d) in enumerate(zip((y0, y1, y2, y3), (d0, d1, d2, d3))):
            cols = slice(i * D_MODEL, (i + 1) * D_MODEL)
            z = _dotb(y[...].astype(BF), wb_ref[i], NN)
            g = jax.nn.sigmoid(gl_ref[:, cols] + gb_ref[i:i + 1, :])
            dgl = dmv * z * (g * (1.0 - g))
            dz = (g * dmv).astype(BF)
            dgl_ref[:, cols] = dgl
            dz_ref[:, cols] = dz
            d[...] = _dotb(dz, wb_ref[i], NT)
            _acc_out(dgb_ref.at[i:i + 1, :], jnp.sum(dgl, axis=0, keepdims=True), first)

    return _pcall(body, grid=(T // tm,),
                  in_specs=[_row(tm, BRANCH_W)] * 4 + [_row(tm, 4 * D_MODEL), _full((4, D_MODEL)), _wb_spec(l), _row(tm, D_MODEL)],
                  out_specs=[_row(tm, BRANCH_W)] * 4 + [_row(tm, 4 * D_MODEL), _row(tm, 4 * D_MODEL), _full((4, D_MODEL))],
                  out_shape=[_S((T, BRANCH_W))] * 4 + [_S((T, 4 * D_MODEL)), _S((T, 4 * D_MODEL), BF), _S((4, D_MODEL))], name=name)(
                      *ys, gl, gb, wb, dm)


def _wout_spec(l):
    return pl.BlockSpec((None, D_MODEL, D_MODEL), lambda *_: (l, 0, 0))


def out_fwd(merged, h, wout, l, g, b, name):
    T = h.shape[0]
    tm = min(T, 512)

    def body(m_ref, h_ref, w_ref, g_ref, b_ref, u_ref, o_ref):
        u = ALPHA * h_ref[...] + _dotb(m_ref[...].astype(BF), w_ref[...], NN)
        u_ref[...] = u
        o_ref[...] = _ln(u, g_ref[...], b_ref[...])

    C = D_MODEL
    return _pcall(body, grid=(T // tm,), in_specs=[_row(tm, C), _row(tm, C), _wout_spec(l), _full((1, C)), _full((1, C))],
                  out_specs=[_row(tm, C), _row(tm, C)], out_shape=[_S((T, C)), _S((T, C))], name=name)(merged, h, wout, g, b)


def out_bwd(u, dy, g, b, wout, l, name):
    T, C = u.shape
    tm = min(T, 512)

    def body(u_ref, dy_ref, g_ref, b_ref, w_ref, du_ref, dm_ref, dg_ref, db_ref):
        _, vjp = jax.vjp(_ln, u_ref[...], g_ref[...], b_ref[...])
        du, dg, db = vjp(dy_ref[...])
        du_ref[...] = du
        dm_ref[...] = _dotb(du.astype(BF), w_ref[...], NT)
        first = pl.program_id(0) == 0
        _acc_out(dg_ref, dg, first)
        _acc_out(db_ref, db, first)

    return _pcall(body, grid=(T // tm,), in_specs=[_row(tm, C), _row(tm, C), _full((1, C)), _full((1, C)), _wout_spec(l)],
                  out_specs=[_row(tm, C), _row(tm, C), _full((1, C)), _full((1, C))],
                  out_shape=[_S((T, C)), _S((T, C)), _S((1, C)), _S((1, C))], name=name)(u, dy, g, b, wout)


def ff_fwd(h, wup, wdown, l, g, b, name):
    T, C = h.shape
    F = wup.shape[2]
    tm, tf = min(T, 512), 1024
    nf = F // tf

    def body(h_ref, wu_ref, wd_ref, g_ref, b_ref, u_ref, o_ref, acc):
        f = pl.program_id(1)
        a = _dotb(h_ref[...].astype(BF), wu_ref[...], NN)
        r = jnp.square(jnp.maximum(a, 0.0))
        p = _dotb(r.astype(BF), wd_ref[...], NN)
        _acc_out(acc, p, f == 0)

        @pl.when(f == nf - 1)
        def _():
            u = ALPHA * h_ref[...] + acc[...]
            u_ref[...] = u
            o_ref[...] = _ln(u, g_ref[...], b_ref[...])

    row = pl.BlockSpec((tm, C), lambda i, f: (i, 0))
    return _pcall(body, grid=(T // tm, nf),
                  in_specs=[row, pl.BlockSpec((None, C, tf), lambda i, f: (l, 0, f)), pl.BlockSpec((None, tf, C), lambda i, f: (l, f, 0)),
                            _full((1, C)), _full((1, C))],
                  out_specs=[row, row], out_shape=[_S((T, C)), _S((T, C))], scratch_shapes=[pltpu.VMEM((tm, C), F32)], name=name)(h, wup, wdown, g, b)


def ff_bwd(u, dy, h, g, b, wup, wdown, l, name):
    T, C = h.shape
    F = wup.shape[2]
    tm, tf = min(T, 512), 1024
    nf = F // tf

    def body(u_ref, dy_ref, h_ref, g_ref, b_ref, wu_ref, wd_ref, du_ref, dh_ref, da_ref, r_ref, dg_ref, db_ref, du_s, acc):
        i, f = pl.program_id(0), pl.program_id(1)

        @pl.when(f == 0)
        def _():
            _, vjp = jax.vjp(_ln, u_ref[...], g_ref[...], b_ref[...])
            du, dg, db = vjp(dy_ref[...])
            du_s[...] = du
            du_ref[...] = du
            _acc_out(dg_ref, dg, i == 0)
            _acc_out(db_ref, db, i == 0)

        a = _dotb(h_ref[...].astype(BF), wu_ref[...], NN)
        ap = jnp.maximum(a, 0.0)
        dr = _dotb(du_s[...].astype(BF), wd_ref[...], NT)
        da = (dr * (2.0 * ap)).astype(BF)
        da_ref[...] = da
        r_ref[...] = jnp.square(ap).T.astype(BF)
        _acc_out(acc, _dotb(da, wu_ref[...], NT), f == 0)

        @pl.when(f == nf - 1)
        def _():
            dh_ref[...] = ALPHA * du_s[...] + acc[...]

    row = pl.BlockSpec((tm, C), lambda i, f: (i, 0))
    colf = pl.BlockSpec((tm, tf), lambda i, f: (i, f))
    return _pcall(body, grid=(T // tm, nf),
                  in_specs=[row, row, row, _full((1, C)), _full((1, C)), pl.BlockSpec((None, C, tf), lambda i, f: (l, 0, f)),
                            pl.BlockSpec((None, tf, C), lambda i, f: (l, f, 0))],
                  out_specs=[row, row, colf, pl.BlockSpec((tf, tm), lambda i, f: (f, i)), _full((1, C)), _full((1, C))],
                  out_shape=[_S((T, C)), _S((T, C)), _S((T, F), BF), _S((F, T), BF), _S((1, C)), _S((1, C))],
                  scratch_shapes=[pltpu.VMEM((tm, C), F32), pltpu.VMEM((tm, C), F32)], name=name)(u, dy, h, g, b, wup, wdown)


MESH_ID = pl.DeviceIdType.MESH
_ANY = pl.BlockSpec(memory_space=pl.ANY)


def _window(ref, ax, idx, n):
    if n < 0:
        return ref
    sel = idx if n == 0 else pl.ds(pl.multiple_of(idx * n, n), n)
    return ref.at[(slice(None),) * ax + (sel,)]


Rider = collections.namedtuple("Rider", "operands out_shape scratch start wait")


def hosted_call(body, rider, *, grid, in_specs, out_specs, out_shape, scratch_shapes, name, args):
    n_in, n_out, n_scr = len(in_specs), len(out_specs), len(scratch_shapes)
    if rider is None:
        return _pcall(body, grid=grid, in_specs=in_specs, out_specs=out_specs, out_shape=out_shape, scratch_shapes=scratch_shapes, name=name)(*args), []
    ri, ro = len(rider.operands), len(rider.out_shape)

    def wrapped(*refs):
        ins, r_in = refs[:n_in], refs[n_in:n_in + ri]
        o0 = n_in + ri
        outs, r_out = refs[o0:o0 + n_out], refs[o0 + n_out:o0 + n_out + ro]
        s0 = o0 + n_out + ro
        scr, r_scr = refs[s0:s0 + n_scr], refs[s0 + n_scr:]
        ids = [pl.program_id(i) for i in range(len(grid))]
        first = functools.reduce(jnp.logical_and, [i == 0 for i in ids])
        last = functools.reduce(jnp.logical_and, [i == g - 1 for i, g in zip(ids, grid)])

        @pl.when(first)
        def _():
            rider.start(r_in, r_out, r_scr)

        body(*ins, *outs, *scr)

        @pl.when(last)
        def _():
            rider.wait(r_in, r_out, r_scr)

    res = _pcall(wrapped, grid=grid, in_specs=list(in_specs) + [_ANY] * ri, out_specs=list(out_specs) + [_ANY] * ro,
                 out_shape=list(out_shape) + list(rider.out_shape), scratch_shapes=list(scratch_shapes) + list(rider.scratch),
                 name=name)(*args, *rider.operands)
    return res[:n_out], res[n_out:]


def comm_call(rider, name):
    ri = len(rider.operands)

    def body(*refs):
        r_in, r_out, r_scr = refs[:ri], refs[ri:ri + len(rider.out_shape)], refs[ri + len(rider.out_shape):]
        rider.start(r_in, r_out, r_scr)
        rider.wait(r_in, r_out, r_scr)

    return _pcall(body, in_specs=[_ANY] * ri, out_specs=[_ANY] * len(rider.out_shape), out_shape=list(rider.out_shape),
                  scratch_shapes=list(rider.scratch), name=name)(*rider.operands)


def gather_rider(shards, axes):
    K = len(shards)
    widths = [s.shape[a] for s, a in zip(shards, axes)]
    out_shape = [_S(s.shape[:a] + (N_DEV * s.shape[a],) + s.shape[a + 1:], s.dtype) for s, a in zip(shards, axes)]

    def plan(x_refs, o_refs, sems):
        send_sems, recv_sems, local_sems = sems
        mx, my, mc = lax.axis_index("x"), lax.axis_index("y"), lax.axis_index("c")
        me, sibling = (mx, my, mc), (mx, my, 1 - mc)
        chips = [(1 - mx, my), (mx, 1 - my), (1 - mx, 1 - my)]

        def win(k, px, py, pc):
            return _window(o_refs[k], axes[k], 4 * px + 2 * py + pc, widths[k])

        def copy(k, slot, block, to, src=None):
            return pltpu.make_async_remote_copy(src_ref=win(k, *block) if src is None else src, dst_ref=win(k, *block),
                                                send_sem=send_sems.at[7 * k + slot], recv_sem=recv_sems.at[7 * k + slot],
                                                device_id=to, device_id_type=MESH_ID)

        mine = [pltpu.make_async_copy(x_refs[k], win(k, *me), local_sems.at[k]) for k in range(K)]
        first = []
        for k in range(K):
            first.append(copy(k, 0, me, sibling, src=x_refs[k]))
            first += [copy(k, 1 + j, me, (*chip, mc), src=x_refs[k]) for j, chip in enumerate(chips)]
        return me, sibling, chips, copy, mine, first

    def start(x_refs, o_refs, sems):
        _, _, _, _, mine, first = plan(x_refs, o_refs, sems)
        for cp in mine + first:
            cp.start()

    def wait(x_refs, o_refs, sems):
        me, sibling, chips, copy, mine, first = plan(x_refs, o_refs, sems)
        mc = me[2]
        passed = []
        for j, chip in enumerate(chips):
            for k in range(K):
                copy(k, 1 + j, (*chip, mc), me).wait_recv()
                passed.append(copy(k, 4 + j, (*chip, mc), sibling))
                passed[-1].start()
        for k in range(K):
            copy(k, 0, sibling, me).wait_recv()
        for j, chip in enumerate(chips):
            for k in range(K):
                copy(k, 4 + j, (*chip, 1 - mc), me).wait_recv()
        for cp in first + passed:
            cp.wait_send()
        for cp in mine:
            cp.wait()

    scratch = [pltpu.SemaphoreType.DMA((7 * K,)), pltpu.SemaphoreType.DMA((7 * K,)), pltpu.SemaphoreType.DMA((K,))]
    return Rider(list(shards), out_shape, scratch, start, wait)


def exchange_rider(items):
    ns = len(items)
    out_shape = [_S((N_DEV,) + tuple(it[3]), it[0].dtype) for it in items]

    def plan(src_refs, o_refs, sems):
        send_sems, recv_sems, local_sems = sems
        mx, my, mc = lax.axis_index("x"), lax.axis_index("y"), lax.axis_index("c")
        me = 4 * mx + 2 * my + mc
        remote, own = [], []
        for s, (_, ax, n, _) in enumerate(items):
            own.append(pltpu.make_async_copy(_window(src_refs[s], ax, me, n), o_refs[s].at[me], local_sems.at[s]))
            for k in range(1, N_DEV):
                px = 1 - mx if k & 4 else mx
                py = 1 - my if k & 2 else my
                pc = 1 - mc if k & 1 else mc
                remote.append(pltpu.make_async_remote_copy(
                    src_ref=_window(src_refs[s], ax, 4 * px + 2 * py + pc, n), dst_ref=o_refs[s].at[me],
                    send_sem=send_sems.at[7 * s + k - 1], recv_sem=recv_sems.at[7 * s + k - 1],
                    device_id=(px, py, pc), device_id_type=MESH_ID))
        return remote, own

    def start(src_refs, o_refs, sems):
        remote, own = plan(src_refs, o_refs, sems)
        for cp in own + remote:
            cp.start()

    def wait(src_refs, o_refs, sems):
        remote, own = plan(src_refs, o_refs, sems)
        for cp in remote + own:
            cp.wait()

    scratch = [pltpu.SemaphoreType.DMA((7 * ns,)), pltpu.SemaphoreType.DMA((7 * ns,)), pltpu.SemaphoreType.DMA((ns,))]
    return Rider([it[0] for it in items], out_shape, scratch, start, wait)


def reduce_adamw(rcvs, w, m, v, name):
    L = len(rcvs)
    _, A, B, C = rcvs[0].shape
    tb = B
    while tb > 8 and tb * C > (1 << 17):
        tb //= 2

    def body(*refs):
        r_refs, (w_ref, m_ref, v_ref, g_ref, d_ref, mo_ref, vo_ref) = refs[:L], refs[L:]
        for k in range(L):
            @pl.when(pl.program_id(0) == k)
            def _(k=k):
                g = r_refs[k][0].astype(F32)
                for d in range(1, N_DEV):
                    g = g + r_refs[k][d].astype(F32)
                mn = ADAM_B1 * m_ref[...] + (1.0 - ADAM_B1) * g
                vn = ADAM_B2 * v_ref[...] + (1.0 - ADAM_B2) * jnp.square(g)
                m_hat = mn / (1.0 - ADAM_B1 ** ADAM_STEP)
                v_hat = vn / (1.0 - ADAM_B2 ** ADAM_STEP)
                g_ref[...] = g
                d_ref[...] = -ADAM_LR * (m_hat / (jnp.sqrt(v_hat) + ADAM_EPS) + ADAM_WD * w_ref[...])
                mo_ref[...] = mn
                vo_ref[...] = vn

    def rspec(k):
        return pl.BlockSpec((N_DEV, None, tb, C), lambda l, a, i: (0, jnp.where(l == k, a, 0), jnp.where(l == k, i, 0), 0))

    blk = pl.BlockSpec((None, tb, C), lambda l, a, i: (l * A + a, i, 0))
    return _pcall(body, grid=(L, A, B // tb), in_specs=[rspec(k) for k in range(L)] + [blk, blk, blk],
                  out_specs=[blk] * 4, out_shape=[_S((L * A, B, C))] * 4, name=name)(*rcvs, w, m, v)


def _w_in_pieces(g0, g1):
    per = D_IN // N_DEV
    return [(d, max(g0, d * per) - d * per, min(g1, (d + 1) * per) - d * per) for d in range(N_DEV) if max(g0, d * per) < min(g1, (d + 1) * per)]


def repack_w_in(w8, name):
    _, L, R, per = w8.shape
    tr = 256

    def cols(x_ref, g0, g1):
        return [x_ref[d, :, a:b] for d, a, b in _w_in_pieces(g0, g1)]

    def body(x_ref, *o_refs):
        for (name_, i), o_ref in zip(SEGS, o_refs):
            o_ref[...] = jnp.concatenate(cols(x_ref, _OFF[i], _OFF[i + 1]), axis=1)
        parts, at = [], 0
        for i, lane0 in SMALL_SRC:
            assert lane0 == at
            parts += cols(x_ref, _OFF[i], _OFF[i + 1])
            at += IN_SIZES[i]
        parts.append(jnp.zeros((tr, LANES - at), w8.dtype))
        o_refs[-1][...] = jnp.concatenate(parts, axis=1)

    widths = [IN_SIZES[i] for _, i in SEGS] + [LANES]
    outs = _pcall(body, grid=(L, R // tr), in_specs=[pl.BlockSpec((N_DEV, None, tr, per), lambda l, r: (0, l, r, 0))],
                  out_specs=[pl.BlockSpec((None, tr, w), lambda l, r: (l, r, 0)) for w in widths],
                  out_shape=[_S((L, R, w), w8.dtype) for w in widths], name=name)(w8)
    return dict(zip(SEG_NAMES, outs))


def repack_dw_in(dseg, name):
    R = dseg["z"].shape[0]
    per = D_IN // N_DEV
    tr = 128
    src = {i: (k, 0) for k, (_, i) in enumerate(SEGS)}
    src.update({i: (len(SEGS), lane0) for i, lane0 in SMALL_SRC})

    def body(*refs):
        s_refs, o_ref = refs[:-1], refs[-1]
        for d in range(N_DEV):
            parts = []
            for i in range(len(IN_SIZES)):
                g0, g1 = max(_OFF[i], d * per), min(_OFF[i + 1], (d + 1) * per)
                if g0 < g1:
                    k, c0 = src[i]
                    parts.append(s_refs[k][:, c0 + g0 - _OFF[i]:c0 + g1 - _OFF[i]])
            o_ref[d] = jnp.concatenate(parts, axis=1)

    arrs = [dseg[n] for n in SEG_NAMES]
    return _pcall(body, grid=(R // tr,), in_specs=[pl.BlockSpec((tr, a.shape[1]), lambda r: (r, 0)) for a in arrs],
                  out_specs=pl.BlockSpec((N_DEV, tr, per), lambda r: (0, r, 0)), out_shape=_S((N_DEV, R, per), arrs[0].dtype), name=name)(*arrs)


WEIGHTS = ("ln_in_g", "ln_in_b", "w_in", "ssd_conv_w", "ssd_conv_b", "ssd_dt_bias", "ssd_a_log", "ssd_d", "ssd_norm_w", "dn_conv_w",
           "dn_a_log", "dn_dt_bias", "dn_norm_w", "sg_ln_g", "sg_ln_b", "sg_w", "sg_b", "fox_f_bias", "gate_b", "w_branch", "w_out",
           "ln1_g", "ln1_b", "w_up", "w_down", "ln2_g", "ln2_b")
SHARDED = {"w_in": 2, "ssd_conv_w": 2, "dn_conv_w": 2, "gate_b": 2, "w_branch": 3, "w_out": 1, "w_up": 2, "w_down": 1}
SLABBED = ("w_in", "dn_conv_w")
MATMUL_WEIGHTS = ("w_in", "w_branch", "w_out", "w_up", "w_down")
REPLICATED_ENTRY = ("ln_in_g", "ln_in_b")
REPLICATED_LAYER = tuple(n for n in WEIGHTS if n not in SHARDED and n not in REPLICATED_ENTRY and n != "sg_w")
SEG_NAMES = tuple(n for n, _ in SEGS) + ("small",)
PACK_COLS = 1024


def _lanes(vec, off):
    return jnp.pad(vec, (off, LANES - off - vec.shape[0]))[None]


def _pack_small(parts, row_mult):
    flat = jnp.concatenate([q.reshape(-1) for q in parts])
    rows = -(-flat.shape[0] // (PACK_COLS * row_mult)) * row_mult
    return jnp.pad(flat, (0, rows * PACK_COLS - flat.shape[0])).reshape(1, rows, PACK_COLS)


EARLY = ("w_branch", "w_out", "w_up", "w_down", "gate_b")
LATE = ("w_in", "ssd_conv_w", "dn_conv_w")
WHOLE = ("sg_w",)


def _gather_rider(p, l, names):
    shards, axes = [], []
    for n in names:
        s = p[n][l:l + 1]
        s = s.astype(BF) if n in MATMUL_WEIGHTS else s
        shards.append(s[None] if n in SLABBED else s)
        axes.append(0 if n in SLABBED else SHARDED[n])
    return gather_rider(shards, axes)


def _exchange_items(g, p, names):
    items = []
    for n in names:
        local = p[n].shape[1:]
        if n in WHOLE:
            items.append((g[n], 0, -1, local))
        elif n in SLABBED:
            items.append((g[n], 0, 0, local))
        else:
            items.append((g[n], SHARDED[n] - 1, local[SHARDED[n] - 1], local))
    return items


def _use_gathered(w, names, arrays, l):
    for n, arr in zip(names, arrays):
        if n == "w_in":
            w[n] = repack_w_in(arr, f"w_in_repack_{l}")
        elif n == "ssd_conv_w":
            w["ssd_cw"] = arr[0]
        elif n == "dn_conv_w":
            w["dn_cw"] = jnp.moveaxis(arr[:, 0], 0, 1).reshape(4, 3 * BRANCH_W)
        elif n == "gate_b":
            w[n] = arr[0]
        else:
            w[n] = arr


def _layer_weights(p, l):
    w = {}
    w["ssd_cb"] = p["ssd_conv_b"][l][None]
    w["dn_cb"] = jnp.zeros((1, 3 * BRANCH_W), F32)
    w["ssd_ps"] = [_lanes(p["ssd_dt_bias"][l], DT0), _lanes(p["ssd_a_log"][l], DT0), _lanes(p["ssd_d"][l], DT0), p["ssd_norm_w"][l][None]]
    w["dn_ps"] = [_lanes(p["dn_a_log"][l], A0), _lanes(p["dn_dt_bias"][l], A0), p["dn_norm_w"][l][None]]
    w["sg_ps"] = [p["sg_ln_g"][l][None], p["sg_ln_b"][l][None], p["sg_w"][l], jnp.pad(p["sg_b"][l].T, ((0, 0), (0, LANES - 4)))]
    w["fox_ps"] = [_lanes(p["fox_f_bias"][l], FF0)]
    for n in ("ln1_g", "ln1_b", "ln2_g", "ln2_b"):
        w[n] = p[n][l][None]
    return w


def _scan_specs(T, a):
    c0 = lambda c, h: (c, 0)
    ssd = dict(f=ssd_chunk, xs=[(a["z"], (128, 512), c0), (a["xbc_act"], (128, 1024), c0), (a["small"], (128, LANES), c0)],
               ys=[((T, BRANCH_W), (128, BRANCH_W), c0)], state=(4, LANES, LANES), nc=T // 128, nh=1, shared=())
    dn = dict(f=dn_chunk, xs=[(a["dn_act"], (64, 3 * BRANCH_W), c0), (a["dngate"], (64, BRANCH_W), c0), (a["small"], (64, LANES), c0)],
              ys=[((T, BRANCH_W), (64, BRANCH_W), c0)], state=(4, LANES, LANES), nc=T // 64, nh=1, shared=())
    sg = dict(f=sg_chunk, xs=[(a["sguv"], (128, 1024), c0)], ys=[((T, BRANCH_W), (128, BRANCH_W), c0)], state=(1, 8, LANES), nc=T // 128, nh=1, shared=())
    fc = dict(f=foxc_chunk, xs=[(a["small"], (128, LANES), c0)],
              ys=[((T, LANES), (128, LANES), c0)], state=(1, 1, LANES), nc=T // 128, nh=1, shared=())
    return ssd, dn, sg, fc


def _layer_fwd(h, w, l, dn_rider=None, fox_rider=None):
    T = h.shape[0]
    a = {"h": h, **proj_all(h, w["w_in"], f"proj_{l}")}
    a["xbc_act"] = conv_fwd(a["xbc"], w["ssd_cw"], w["ssd_cb"], f"ssd_conv_{l}")
    a["dn_act"] = conv_fwd(a["dnqkv"], w["dn_cw"], w["dn_cb"], f"dn_conv_{l}")
    ssd, dn, sg, fc = _scan_specs(T, a)
    (a["ya"], a["ssd_st"]), _ = scan_fwd(f"ssd_fwd_{l}", ssd["f"], ssd["xs"], w["ssd_ps"], ssd["ys"], ssd["state"], ssd["nc"], ssd["nh"])
    (a["yb"], a["dn_st"]), got = scan_fwd(f"dn_fwd_{l}", dn["f"], dn["xs"], w["dn_ps"], dn["ys"], dn["state"], dn["nc"], dn["nh"], rider=dn_rider)
    _use_gathered(w, EARLY, got, l)
    (a["yc"], a["sg_st"]), _ = scan_fwd(f"sg_fwd_{l}", sg["f"], sg["xs"], w["sg_ps"], sg["ys"], sg["state"], sg["nc"], sg["nh"])
    (a["ccol"], a["fc_st"]), _ = scan_fwd(f"foxc_fwd_{l}", fc["f"], fc["xs"], w["fox_ps"], fc["ys"], fc["state"], fc["nc"], fc["nh"])
    a["fox_qa"], a["fox_ka"], a["fox_va"] = fox_prep(a["foxqkv"], a["ccol"], f"fox_prep_{l}")
    (a["yd"], a["lse"]), carried = fox_fwd(a["fox_qa"], a["fox_ka"], a["fox_va"], f"fox_fwd_{l}", rider=fox_rider)
    a["merged"] = merge_fwd([a["ya"], a["yb"], a["yc"], a["yd"]], a["gates"], w["gate_b"], w["w_branch"], 0, f"merge_fwd_{l}")
    a["u1"], a["h1"] = out_fwd(a["merged"], h, w["w_out"], 0, w["ln1_g"], w["ln1_b"], f"out_fwd_{l}")
    a["u2"], a["h2"] = ff_fwd(a["h1"], w["w_up"], w["w_down"], 0, w["ln2_g"], w["ln2_b"], f"ff_fwd_{l}")
    return a, carried


def _layer_bwd(dh2, a, w, l, p, late_above):
    T = dh2.shape[0]
    g = {}
    du2, dh1, da, r, dg2, db2 = ff_bwd(a["u2"], dh2, a["h1"], w["ln2_g"], w["ln2_b"], w["w_up"], w["w_down"], 0, f"ff_bwd_{l}")
    g["ln2_g"], g["ln2_b"] = dg2[0], db2[0]
    g["w_up"] = matmul_w(transpose_bf16(a["h1"], f"h1_t_{l}"), da[None], 0, "nn", f"dwup_{l}", out_dtype=BF)
    g["w_down"] = matmul_w(r, du2[None], 0, "nn", f"dwdown_{l}", out_dtype=BF)
    du1, dmerged, dg1, db1 = out_bwd(a["u1"], dh1, w["ln1_g"], w["ln1_b"], w["w_out"], 0, f"out_bwd_{l}")
    g["ln1_g"], g["ln1_b"] = dg1[0], db1[0]
    g["w_out"] = matmul_w(transpose_bf16(a["merged"], f"merged_t_{l}"), du1[None], 0, "nn", f"dwout_{l}", out_dtype=BF)
    ys = [a["ya"], a["yb"], a["yc"], a["yd"]]
    dya, dyb, dyc, dyd, dgl, dz, dgb = merge_bwd(ys, a["gates"], w["gate_b"], w["w_branch"], 0, dmerged, f"merge_bwd_{l}")
    g["gate_b"] = dgb
    g["w_branch"] = jnp.stack([matmul_tn(ys[i], dz, f"dwb{i}_{l}", b_col0=i * D_MODEL, n_cols=D_MODEL, out_dtype=BF) for i in range(4)])
    early = exchange_rider(_exchange_items(g, p, EARLY))
    dn_rider = early if late_above is None else exchange_rider(late_above)
    fox_rider = None if late_above is None else early
    ssd, dn, sg, fc = _scan_specs(T, a)
    (dz_ssd, dxbc_act, dsm_ssd, d_dtb, d_alog, d_dsk, d_nw), _ = scan_bwd(f"ssd_bwd_{l}", ssd["f"], ssd["xs"], w["ssd_ps"], ssd["ys"], [dya], a["ssd_st"],
                                                                           ssd["state"], ssd["nc"], ssd["nh"])
    g["ssd_dt_bias"], g["ssd_a_log"], g["ssd_d"], g["ssd_norm_w"] = d_dtb[0, DT0:DT0 + 8], d_alog[0, DT0:DT0 + 8], d_dsk[0, DT0:DT0 + 8], d_nw[0]
    dxbc, g["ssd_conv_w"], dcb = conv_bwd(a["xbc"], w["ssd_cw"], w["ssd_cb"], dxbc_act, f"ssd_conv_bwd_{l}")
    g["ssd_conv_b"] = dcb[0]
    (ddn_act, ddngate, dsm_dn, d_alog, d_dtb, d_nw), got_dn = scan_bwd(f"dn_bwd_{l}", dn["f"], dn["xs"], w["dn_ps"], dn["ys"], [dyb], a["dn_st"],
                                                                        dn["state"], dn["nc"], dn["nh"], rider=dn_rider)
    g["dn_a_log"], g["dn_dt_bias"], g["dn_norm_w"] = d_alog[0, A0:A0 + 4], d_dtb[0, A0:A0 + 4], d_nw[0]
    ddnqkv, g["dn_conv_w"], _ = conv_bwd(a["dnqkv"], w["dn_cw"], w["dn_cb"], ddn_act, f"dn_conv_bwd_{l}")
    (dsguv, d_lng, d_lnb, d_w, d_bt), _ = scan_bwd(f"sg_bwd_{l}", sg["f"], sg["xs"], w["sg_ps"], sg["ys"], [dyc], a["sg_st"], sg["state"], sg["nc"], sg["nh"])
    g["sg_ln_g"], g["sg_ln_b"], g["sg_w"], g["sg_b"] = d_lng[0], d_lnb[0], d_w, d_bt[:, :4].T
    qb, doa, qd, kd = fox_prep_bwd(a["foxqkv"], a["fox_qa"], a["yd"], a["lse"], dyd, f"fox_prep_bwd_{l}")
    (dfq, dfk, dfv, dccol), got_fox = fox_bwd(qb, a["fox_ka"], a["fox_va"], doa, qd, kd, f"fox_bwd_{l}", rider=fox_rider)
    (dsm_fox, d_fb), _ = scan_bwd(f"foxc_bwd_{l}", fc["f"], fc["xs"], w["fox_ps"], fc["ys"], [dccol], a["fc_st"], fc["state"], fc["nc"], fc["nh"])
    g["fox_f_bias"] = d_fb[0, FF0:FF0 + 8]
    dseg = {"z": dz_ssd, "xbc": dxbc, "dnqkv": ddnqkv, "dngate": ddngate, "sguv": dsguv,
            "foxqkv": jnp.concatenate([dfq, dfk, dfv], axis=1), "gates": dgl, "small": add3(dsm_ssd, dsm_dn, dsm_fox, f"dsmall_{l}")}
    h_t = transpose_bf16(a["h"], f"h_t_{l}")
    dwin = {n: matmul_w(h_t, dseg[n][None], 0, "nn", f"dwin_{n}_{l}", out_dtype=BF) for n in SEG_NAMES}
    g["w_in"] = repack_dw_in(dwin, f"dw_in_repack_{l}")
    g["dn_conv_w"] = jnp.moveaxis(g["dn_conv_w"].reshape(4, N_DEV, 3 * BRANCH_W // N_DEV), 1, 0)
    got = {(EARLY, l): got_dn} if late_above is None else {(LATE + WHOLE, l + 1): got_dn, (EARLY, l): got_fox}
    return dseg, du1, g, got


def proj_all(h, w_in, name):
    T = h.shape[0]
    tm = min(T, 256)
    ns = len(SEG_NAMES)

    def body(*refs):
        h_ref, w_refs, o_refs = refs[0], refs[1:1 + ns], refs[1 + ns:]
        hb = h_ref[...].astype(BF)
        for w_ref, o_ref in zip(w_refs, o_refs):
            o_ref[...] = _dotb(hb, w_ref[...], NN)

    widths = [w_in[n].shape[2] for n in SEG_NAMES]
    in_specs = [_row(tm, D_MODEL)]
    in_specs += [pl.BlockSpec((None,) + w_in[n].shape[1:], lambda i: (0, 0, 0), pipeline_mode=pl.Buffered(1)) for n in SEG_NAMES]
    outs = _pcall(body, grid=(T // tm,), in_specs=in_specs, out_specs=[_row(tm, wd) for wd in widths],
                  out_shape=[_S((T, wd)) for wd in widths], name=name)(h, *[w_in[n] for n in SEG_NAMES])
    return dict(zip(SEG_NAMES, outs))


def dh_all(dseg, w_in, add, name, rider=None):
    T = add.shape[0]
    tm = min(T, 256)
    ns = len(SEG_NAMES)

    def body(*refs):
        d_refs, w_refs, add_ref, o_ref = refs[:ns], refs[ns:2 * ns], refs[2 * ns], refs[2 * ns + 1]
        acc = ALPHA * add_ref[...]
        for d_ref, w_ref in zip(d_refs, w_refs):
            acc = acc + _dotb(d_ref[...].astype(BF), w_ref[...], NT)
        o_ref[...] = acc

    in_specs = [_row(tm, dseg[n].shape[1]) for n in SEG_NAMES]
    in_specs += [pl.BlockSpec((None,) + w_in[n].shape[1:], lambda i: (0, 0, 0), pipeline_mode=pl.Buffered(1)) for n in SEG_NAMES]
    in_specs.append(_row(tm, D_MODEL))
    return hosted_call(body, rider, grid=(T // tm,), in_specs=in_specs, out_specs=[_row(tm, D_MODEL)], out_shape=[_S((T, D_MODEL))],
                       scratch_shapes=[], name=name, args=[*[dseg[n] for n in SEG_NAMES], *[w_in[n] for n in SEG_NAMES], add])


def kernel(x, ln_in_g, ln_in_b, w_in, ssd_conv_w, ssd_conv_b, ssd_dt_bias, ssd_a_log, ssd_d, ssd_norm_w, dn_conv_w, dn_a_log, dn_dt_bias, dn_norm_w, sg_ln_g, sg_ln_b, sg_w, sg_b, fox_f_bias, gate_b, w_branch, w_out, ln1_g, ln1_b, w_up, w_down, ln2_g, ln2_b, loss_target, m_ln_in_g, m_ln_in_b, m_w_in, m_ssd_conv_w, m_ssd_conv_b, m_ssd_dt_bias, m_ssd_a_log, m_ssd_d, m_ssd_norm_w, m_dn_conv_w, m_dn_a_log, m_dn_dt_bias, m_dn_norm_w, m_sg_ln_g, m_sg_ln_b, m_sg_w, m_sg_b, m_fox_f_bias, m_gate_b, m_w_branch, m_w_out, m_ln1_g, m_ln1_b, m_w_up, m_w_down, m_ln2_g, m_ln2_b, v_ln_in_g, v_ln_in_b, v_w_in, v_ssd_conv_w, v_ssd_conv_b, v_ssd_dt_bias, v_ssd_a_log, v_ssd_d, v_ssd_norm_w, v_dn_conv_w, v_dn_a_log, v_dn_dt_bias, v_dn_norm_w, v_sg_ln_g, v_sg_ln_b, v_sg_w, v_sg_b, v_fox_f_bias, v_gate_b, v_w_branch, v_w_out, v_ln1_g, v_ln1_b, v_w_up, v_w_down, v_ln2_g, v_ln2_b):
    args = dict(locals())
    p = {n: args[n] for n in WEIGHTS}
    xt, target = x[0], loss_target[0]
    ws, acts = [_layer_weights(p, l) for l in range(DEPTH)], []
    _use_gathered(ws[0], LATE, comm_call(_gather_rider(p, 0, LATE), "weights_all_gather_0"), 0)
    h = ln_fwd(xt, ln_in_g[None], ln_in_b[None], "ln_in_fwd")
    for l in range(DEPTH):
        a, gathered = _layer_fwd(h, ws[l], l, dn_rider=_gather_rider(p, 0, EARLY) if l == 0 else None,
                                 fox_rider=_gather_rider(p, l + 1, LATE + EARLY) if l + 1 < DEPTH else None)
        if l + 1 < DEPTH:
            _use_gathered(ws[l + 1], LATE + EARLY, gathered, l + 1)
        acts.append(a)
        h = a["h2"]
    dh, loss = loss_head(h, target, "loss_head")
    loss = lax.psum(loss[0, 0], ("x", "y", "c"))

    layer_grads, got, late = [None] * DEPTH, {}, None
    for l in reversed(range(DEPTH)):
        dseg, du1, layer_grads[l], got_l = _layer_bwd(dh, acts[l], ws[l], l, p, late)
        got.update(got_l)
        late = _exchange_items(layer_grads[l], p, LATE + WHOLE)
        rider = None
        if l == 0:
            pack = _pack_small([jnp.stack([layer_grads[k][n] for k in range(DEPTH)]) for n in REPLICATED_LAYER], 8)
            rider = exchange_rider(late + [(pack[0], 0, -1, pack.shape[1:])])
        (dh,), carried = dh_all(dseg, ws[l]["w_in"], du1, f"dh_{l}", rider=rider)
    got[(LATE + WHOLE, 0)], got_layer_pack = carried[:-1], carried[-1]
    grad_x, dg_in, db_in = ln_bwd(xt, ln_in_g[None], ln_in_b[None], dh, "ln_in_bwd")
    pack = _pack_small([dg_in[0], db_in[0]], 8)
    got_entry_pack = comm_call(exchange_rider([(pack[0], 0, -1, pack.shape[1:])]), "grads_exchange_entry_norm")[0]
    rcv = {(n, l): arr for (names, l), arrs in got.items() for n, arr in zip(names, arrs)}

    res = [{}, {}, {}, {}]
    for n in tuple(SHARDED) + WHOLE:
        shp = p[n].shape
        lead = math.prod(shp[1:-2])
        to3 = lambda t: t.reshape((-1,) + shp[-2:])
        outs = reduce_adamw([rcv[(n, l)].reshape((N_DEV, lead) + shp[-2:]) for l in range(DEPTH)],
                            to3(p[n]), to3(args["m_" + n]), to3(args["v_" + n]), f"adamw_{n}")
        for k in range(4):
            res[k][n] = outs[k].reshape(shp)
    for names, got_pack, rows, name in ((REPLICATED_LAYER, got_layer_pack, 8, "adamw_replicated"), (REPLICATED_ENTRY, got_entry_pack, 8, "adamw_entry_norm")):
        outs = reduce_adamw([got_pack[:, None]], _pack_small([p[n] for n in names], rows), _pack_small([args["m_" + n] for n in names], rows),
                            _pack_small([args["v_" + n] for n in names], rows), name)
        off = 0
        for n in names:
            shp = p[n].shape
            cnt = math.prod(shp)
            for k in range(4):
                res[k][n] = outs[k].reshape(-1)[off:off + cnt].reshape(shp)
            off += cnt
    return (loss, grad_x[None], *[res[0][n] for n in WEIGHTS], *[res[1][n] for n in WEIGHTS],
            *[res[2][n] for n in WEIGHTS], *[res[3][n] for n in WEIGHTS])
```

```python
import collections
import functools
import math

import jax
import jax.numpy as jnp
from jax import lax
from jax.experimental import pallas as pl
from jax.experimental.pallas import tpu as pltpu

F32 = jnp.float32
BF = jnp.bfloat16

D_MODEL = 1024
DEPTH = 2
BRANCH_W = 512
D_FF = 4096
LN_EPS = 1e-5
NORM_EPS = 1e-6
ALPHA = (2 * DEPTH) ** 0.25
N_DEV = 8
LANES = 128
ADAM_LR, ADAM_B1, ADAM_B2, ADAM_EPS, ADAM_WD, ADAM_STEP = 0.001, 0.9, 0.999, 1e-08, 0.01, 10

DT0, BETA0, A0, FF0 = 0, 8, 12, 16
IN_SIZES = (512, 1024, 8, 1536, 4, 4, 512, 1024, 1536, 8, 4096)
_OFF = [0]
for _s in IN_SIZES:
    _OFF.append(_OFF[-1] + _s)
D_IN = _OFF[-1]
SEGS = (("z", 0), ("xbc", 1), ("dnqkv", 3), ("dngate", 6), ("sguv", 7), ("foxqkv", 8), ("gates", 10))
SMALL_SRC = ((2, DT0), (4, BETA0), (5, A0), (9, FF0))

NN = ((1,), (0,))
NT = ((1,), (1,))
TN = ((0,), (0,))
_DIMS = {"nn": NN, "nt": NT, "tn": TN}


def _pcall(body, **kw):
    return pl.pallas_call(body, **kw)


def _S(shape, dtype=F32):
    return jax.ShapeDtypeStruct(tuple(shape), dtype)


def _iota(shape, dim):
    return lax.broadcasted_iota(jnp.int32, shape, dim)


def _dotb(a, b, dims):
    return lax.dot_general(a, b, (dims, ((), ())), preferred_element_type=F32)


def _split2(a):
    ah = a.astype(BF)
    return ah, (a - ah.astype(F32)).astype(BF)


def _split3(a):
    a1 = a.astype(BF)
    r = a - a1.astype(F32)
    a2 = r.astype(BF)
    a3 = (r - a2.astype(F32)).astype(BF)
    return a1, a2, a3


def _mm_raw(a, b, form, mode):
    d = _DIMS[form]
    if mode == "1":
        return _dotb(a.astype(BF), b.astype(BF), d)
    if mode == "3":
        ah, al = _split2(a)
        bh, bl = _split2(b)
        return _dotb(ah, bh, d) + (_dotb(ah, bl, d) + _dotb(al, bh, d))
    if mode == "xa":
        ab = a.astype(BF)
        b1, b2, b3 = _split3(b)
        return _dotb(ab, b1, d) + (_dotb(ab, b2, d) + _dotb(ab, b3, d))
    bb = b.astype(BF)
    a1, a2, a3 = _split3(a)
    return _dotb(a1, bb, d) + (_dotb(a2, bb, d) + _dotb(a3, bb, d))


@functools.partial(jax.custom_vjp, nondiff_argnums=(2, 3))
def mm(a, b, form, mode):
    return _mm_raw(a, b, form, mode)


def _mm_fwd(a, b, form, mode):
    return _mm_raw(a, b, form, mode), (a, b)


_XA_DB = {"nn": "xa", "nt": "xb", "tn": "xa"}
_XB_DA = {"nn": "xb", "nt": "xb", "tn": "xa"}


def _mm_bwd(form, mode, res, g):
    a, b = res
    ma = _XB_DA[form] if mode == "xb" else mode
    mb = _XA_DB[form] if mode == "xa" else mode
    da = db = None
    if mode != "xa":
        da = {"nn": lambda: mm(g, b, "nt", ma), "nt": lambda: mm(g, b, "nn", ma), "tn": lambda: mm(b, g, "nt", ma)}[form]()
    if mode != "xb":
        db = {"nn": lambda: mm(a, g, "tn", mb), "nt": lambda: mm(g, a, "tn", mb), "tn": lambda: mm(a, g, "nn", mb)}[form]()
    if da is None:
        da = jnp.zeros_like(a)
    if db is None:
        db = jnp.zeros_like(b)
    return da, db


mm.defvjp(_mm_fwd, _mm_bwd)


def _silu(x):
    return x * jax.nn.sigmoid(x)


def _ln(x, g, b):
    mu = jnp.mean(x, -1, keepdims=True)
    xc = x - mu
    var = jnp.mean(xc * xc, -1, keepdims=True)
    return xc * lax.rsqrt(var + LN_EPS) * g + b


def _pick(n, cap):
    if n <= cap:
        return n
    best = LANES
    for t in range(LANES, cap + 1, LANES):
        if n % t == 0:
            best = t
    return best


def transpose_bf16(a, name):
    T, C = a.shape
    tt = min(T, 512)

    def body(a_ref, o_ref):
        o_ref[...] = a_ref[...].T.astype(BF)

    return _pcall(body, grid=(T // tt,), in_specs=[pl.BlockSpec((tt, C), lambda t: (t, 0))], out_specs=pl.BlockSpec((C, tt), lambda t: (0, t)),
                  out_shape=_S((C, T), BF), name=name)(a)


def matmul_w(a, w, l, form, name, add=None, add_scale=1.0, out_dtype=F32):
    M, K = a.shape
    N = w.shape[2] if form == "nn" else w.shape[1]
    tm, tn, tk = min(M, 512), _pick(N, 1024), _pick(K, 1024)
    nk = K // tk

    def body(*refs):
        if add is None:
            a_ref, b_ref, o_ref, acc = refs
        else:
            a_ref, b_ref, d_ref, o_ref, acc = refs
        k = pl.program_id(2)
        p = _dotb(a_ref[...].astype(BF), b_ref[...].astype(BF), _DIMS[form])

        @pl.when(k == 0)
        def _():
            acc[...] = p

        @pl.when(k > 0)
        def _():
            acc[...] += p

        @pl.when(k == nk - 1)
        def _():
            r = acc[...]
            if add is not None:
                r = r + add_scale * d_ref[...]
            o_ref[...] = r.astype(out_dtype)

    if form == "nn":
        wspec = pl.BlockSpec((None, tk, tn), lambda j, i, k: (l, k, j))
    else:
        wspec = pl.BlockSpec((None, tn, tk), lambda j, i, k: (l, j, k))
    in_specs = [pl.BlockSpec((tm, tk), lambda j, i, k: (i, k)), wspec]
    args = [a, w]
    if add is not None:
        in_specs.append(pl.BlockSpec((tm, tn), lambda j, i, k: (i, j)))
        args.append(add)
    return _pcall(body, grid=(N // tn, M // tm, nk), in_specs=in_specs,
                  out_specs=pl.BlockSpec((tm, tn), lambda j, i, k: (i, j)), out_shape=_S((M, N), out_dtype),
                  scratch_shapes=[pltpu.VMEM((tm, tn), F32)], name=name)(*args)


def matmul_tn(a, b, name, b_col0=0, n_cols=None, out_dtype=F32):
    T, M = a.shape
    N = b.shape[1] if n_cols is None else n_cols
    tm, tn, tt = _pick(M, 512), _pick(N, 1024), min(T, 512)
    nt = T // tt
    jb = b_col0 // tn

    def body(a_ref, b_ref, o_ref, acc):
        t = pl.program_id(2)
        p = _dotb(a_ref[...].astype(BF), b_ref[...].astype(BF), TN)

        @pl.when(t == 0)
        def _():
            acc[...] = p

        @pl.when(t > 0)
        def _():
            acc[...] += p

        @pl.when(t == nt - 1)
        def _():
            o_ref[...] = acc[...].astype(out_dtype)

    return _pcall(body, grid=(M // tm, N // tn, nt),
                  in_specs=[pl.BlockSpec((tt, tm), lambda i, j, t: (t, i)), pl.BlockSpec((tt, tn), lambda i, j, t: (t, jb + j))],
                  out_specs=pl.BlockSpec((tm, tn), lambda i, j, t: (i, j)), out_shape=_S((M, N), out_dtype),
                  scratch_shapes=[pltpu.VMEM((tm, tn), F32)], name=name)(a, b)


def _pieces(v):
    if v.ndim == 3:
        return [v[i] for i in range(v.shape[0])]
    n = v.shape[1] // LANES
    if n <= 1:
        return [v]
    return [v[:, i * LANES:(i + 1) * LANES] for i in range(n)]


def _join(ps, like_ndim):
    if like_ndim == 3:
        return jnp.stack(ps, axis=0)
    return ps[0] if len(ps) == 1 else jnp.concatenate(ps, axis=1)


def scan_fwd(name, f, xs, ps, ys, state_shape, nc, nh=1, rider=None):
    nx, npar, ny = len(xs), len(ps), len(ys)

    def body(*refs):
        x_refs, p_refs = refs[:nx], refs[nx:nx + npar]
        y_refs = refs[nx + npar:nx + npar + ny]
        st_out, st = refs[nx + npar + ny], refs[nx + npar + ny + 1]
        c, h = pl.program_id(0), pl.program_id(1)

        @pl.when(c == 0)
        def _():
            st[h] = jnp.zeros(state_shape, F32)

        S = st[h]
        st_out[...] = S
        yv, Sn = f([_pieces(r[...]) for r in x_refs], [_pieces(r[...]) for r in p_refs], _pieces(S), h)
        for r, v in zip(y_refs, yv):
            r[...] = _join(v, 2)
        st[h] = _join(Sn, 3)

    in_specs = [pl.BlockSpec(bs, im) for (_, bs, im) in xs]
    in_specs += [pl.BlockSpec(p.shape, (lambda c, h, n=p.ndim: (0,) * n)) for p in ps]
    out_specs = [pl.BlockSpec(bs, im) for (_, bs, im) in ys]
    out_specs.append(pl.BlockSpec((None, None) + tuple(state_shape), lambda c, h: (c, h, 0, 0, 0)))
    out_shape = [_S(s) for (s, _, _) in ys] + [_S((nc, nh) + tuple(state_shape))]
    return hosted_call(body, rider, grid=(nc, nh), in_specs=in_specs, out_specs=out_specs, out_shape=out_shape,
                       scratch_shapes=[pltpu.VMEM((nh,) + tuple(state_shape), F32)], name=name, args=[*[x[0] for x in xs], *ps])


def scan_bwd(name, f, xs, ps, ys, dys, states, state_shape, nc, nh=1, shared=(), rider=None):
    nx, npar, ny = len(xs), len(ps), len(ys)

    def body(*refs):
        x_refs, p_refs = refs[:nx], refs[nx:nx + npar]
        s_ref = refs[nx + npar]
        dy_refs = refs[nx + npar + 1:nx + npar + 1 + ny]
        o = nx + npar + 1 + ny
        dx_refs, dp_refs, dst = refs[o:o + nx], refs[o + nx:o + nx + npar], refs[o + nx + npar]
        c, h = pl.program_id(0), pl.program_id(1)

        @pl.when(c == 0)
        def _():
            dst[h] = jnp.zeros(state_shape, F32)

        @pl.when((c == 0) & (h == 0))
        def _():
            for r in dp_refs:
                r[...] = jnp.zeros(r.shape, F32)

        xv = [_pieces(r[...]) for r in x_refs]
        pv = [_pieces(r[...]) for r in p_refs]
        _, vjp = jax.vjp(lambda a, b, s: f(a, b, s, h), xv, pv, _pieces(s_ref[...]))
        dxv, dpv, dS = vjp(([_pieces(r[...]) for r in dy_refs], _pieces(dst[h])))
        for i, (r, v) in enumerate(zip(dx_refs, dxv)):
            if i in shared and nh > 1:
                @pl.when(h == 0)
                def _(r=r, v=v):
                    r[...] = _join(v, 2)

                @pl.when(h > 0)
                def _(r=r, v=v):
                    r[...] += _join(v, 2)
            else:
                r[...] = _join(v, 2)
        for r, v in zip(dp_refs, dpv):
            r[...] += _join(v, len(r.shape))
        dst[h] = _join(dS, 3)

    def rev(im):
        return lambda c, h: im(nc - 1 - c, h)

    in_specs = [pl.BlockSpec(bs, rev(im)) for (_, bs, im) in xs]
    in_specs += [pl.BlockSpec(p.shape, (lambda c, h, n=p.ndim: (0,) * n)) for p in ps]
    in_specs.append(pl.BlockSpec((None, None) + tuple(state_shape), lambda c, h: (nc - 1 - c, h, 0, 0, 0)))
    in_specs += [pl.BlockSpec(bs, rev(im)) for (_, bs, im) in ys]
    out_specs = [pl.BlockSpec(bs, rev(im)) for (_, bs, im) in xs]
    out_specs += [pl.BlockSpec(p.shape, (lambda c, h, n=p.ndim: (0,) * n)) for p in ps]
    out_shape = [_S(x[0].shape) for x in xs] + [_S(p.shape) for p in ps]
    return hosted_call(body, rider, grid=(nc, nh), in_specs=in_specs, out_specs=out_specs, out_shape=out_shape,
                       scratch_shapes=[pltpu.VMEM((nh,) + tuple(state_shape), F32)], name=name,
                       args=[*[x[0] for x in xs], *ps, states, *dys])


def _lane():
    return _iota((1, LANES), 1)


def _col(v, idx):
    return jnp.sum(v * (_lane() == idx).astype(F32), axis=1, keepdims=True)


def _last_row(v):
    r = v.shape[0]
    return jnp.sum(v * (_iota((r, 1), 0) == r - 1).astype(F32), axis=0, keepdims=True)


def _tril(n, strict=False):
    r, c = _iota((n, n), 0), _iota((n, n), 1)
    return (r > c) if strict else (r >= c)


def ssd_chunk(xs, ps, S, h):
    zp, xbc, (sm,) = xs
    (bias,), (alog,), (dsk,), nw = ps
    Q = sm.shape[0]
    H = range(8)
    lane = _lane()
    a128 = jnp.where(lane < 8, -jnp.exp(alog), 0.0)
    dtl = jax.nn.softplus(sm + bias)
    tri = _tril(Q)
    cum = mm(tri.astype(F32), dtl * a128, "nn", "xa")
    sel8 = (_iota((8, LANES), 0) == _iota((8, LANES), 1)).astype(F32)
    cum_t = mm(sel8, cum, "nt", "xa")
    m0 = (lane < 64).astype(F32)
    rows0 = (_iota((LANES, 1), 0) < 64).astype(F32)
    me = [m0 if hh % 2 == 0 else 1.0 - m0 for hh in H]
    re = [rows0 if hh % 2 == 0 else 1.0 - rows0 for hh in H]
    Bm, Cm = [xbc[4 + hh // 4] for hh in H], [xbc[6 + hh // 4] for hh in H]
    cb = [mm(xbc[6 + g], xbc[4 + g], "nt", "1") for g in range(2)]
    col = [_col(cum, hh) for hh in H]
    row = [jnp.sum(cum_t * (_iota((8, 1), 0) == hh).astype(F32), axis=0, keepdims=True) for hh in H]
    xh = [xbc[hh // 2] * me[hh] for hh in H]
    xdt = [xh[hh] * _col(dtl, hh) for hh in H]
    seg = [jnp.exp(jnp.where(tri, col[hh] - row[hh], -jnp.inf)) for hh in H]
    last = [_last_row(col[hh]) for hh in H]
    y_diag = [mm(cb[hh // 4] * seg[hh], xdt[hh], "nn", "1") for hh in H]
    y_off = [mm(Cm[hh] * jnp.exp(col[hh]), S[hh // 2], "nt", "1") * me[hh] for hh in H]
    st = [mm(xdt[hh], Bm[hh] * jnp.exp(last[hh] - col[hh]), "tn", "1") for hh in H]
    y = [y_diag[hh] + y_off[hh] + _col(dsk, hh) * xh[hh] for hh in H]
    Sn = [S[pr] * (jnp.exp(last[2 * pr]) * re[0] + jnp.exp(last[2 * pr + 1]) * re[1]) + st[2 * pr] + st[2 * pr + 1] for pr in range(4)]
    yz = [(y[2 * pr] + y[2 * pr + 1]) * _silu(zp[pr]) for pr in range(4)]
    ssq = sum(jnp.sum(v * v, axis=1, keepdims=True) for v in yz)
    scale = lax.rsqrt(ssq / BRANCH_W + NORM_EPS)
    return [[yz[i] * scale * nw[i] for i in range(4)]], Sn


@jax.custom_vjp
def _halves(x):
    r = x.shape[0] // 2
    return x[:r], x[r:]


_halves.defvjp(lambda x: (_halves(x), None), lambda _, g: (jnp.concatenate(g, axis=0),))
DN_CHUNK = 64


def dn_chunk(xs, ps, S, h):
    act, gate, (sm,) = xs
    (alog,), (dtb,), (nw,) = ps
    C = DN_CHUNK
    I = range(8)
    lane = _lane()
    tri, strict = _tril(C), _tril(C, True)
    sm2 = _halves(sm)
    G = [jnp.where((lane >= A0) & (lane < A0 + 4), -jnp.exp(alog) * jax.nn.softplus(s + dtb), 0.0) for s in sm2]
    gcs = [mm(tri.astype(F32), g, "nn", "xa") for g in G]
    sig = [jax.nn.sigmoid(s) for s in sm2]
    parts = [_halves(x) for x in act]
    q, k, v = ([parts[o + i % 4][i // 4] for i in I] for o in (0, 4, 8))
    gt = [_halves(x) for x in gate]
    qn = [q[i] * lax.rsqrt(jnp.sum(q[i] * q[i], axis=1, keepdims=True) + NORM_EPS) * (LANES ** -0.5) for i in I]
    kn = [k[i] * lax.rsqrt(jnp.sum(k[i] * k[i], axis=1, keepdims=True) + NORM_EPS) for i in I]
    beta = [_col(sig[i // 4], BETA0 + i % 4) for i in I]
    gcol = [_col(gcs[i // 4], A0 + i % 4) for i in I]
    selr = [((_iota((8, LANES), 0) == 0) & (_iota((8, LANES), 1) == A0 + h)).astype(F32) for h in range(4)]
    grow = [jnp.sum(mm(selr[i % 4], gcs[i // 4], "nt", "xa"), axis=0, keepdims=True) for i in I]
    gamma = [jnp.exp(jnp.where(tri, gcol[i] - grow[i], -jnp.inf)) for i in I]
    kb = [kn[i] * beta[i] for i in I]
    pk = [-(mm(kb[i], kn[i], "nt", "1") * jnp.where(strict, gamma[i], 0.0)) for i in I]
    eye = (_iota((C, C), 0) == _iota((C, C), 1)).astype(F32)
    minv = [eye + pk[i] for i in I]
    for _ in range(5):
        pk = [mm(pk[i], pk[i], "nn", "3") for i in I]
        minv = [minv[i] + mm(minv[i], pk[i], "nn", "3") for i in I]
    eg = [jnp.exp(gcol[i]) for i in I]
    w = [mm(minv[i], kb[i] * eg[i], "nn", "3") for i in I]
    u = [mm(minv[i], v[i] * beta[i], "nn", "3") for i in I]
    glast = [_last_row(gcol[i]) for i in I]
    qg = [qn[i] * eg[i] for i in I]
    qk = [mm(qn[i], kn[i], "nt", "1") * gamma[i] for i in I]
    kdec = [kn[i] * jnp.exp(glast[i] - gcol[i]) for i in I]
    y = []
    for c in range(2):
        J = range(4 * c, 4 * c + 4)
        vnew = [u[i] - mm(w[i], S[i % 4], "nn", "1") for i in J]
        o = [mm(qg[i], S[i % 4], "nn", "1") + mm(qk[i], vn, "nn", "1") for i, vn in zip(J, vnew)]
        S = [S[i % 4] * jnp.exp(glast[i]) + mm(kdec[i], vn, "tn", "1") for i, vn in zip(J, vnew)]
        on = [x * lax.rsqrt(jnp.mean(x * x, axis=1, keepdims=True) + NORM_EPS) * nw for x in o]
        y.append([on[h] * _silu(gt[h][c]) for h in range(4)])
    return [[jnp.concatenate([y[0][h], y[1][h]], axis=0) for h in range(4)]], S


def sg_chunk(xs, ps, S, h):
    (uv,) = xs
    lng, lnb, W, (bt,) = ps
    u = [jax.nn.gelu(p) for p in uv[:4]]
    v = [jax.nn.gelu(p) for p in uv[4:]]
    mu = sum(jnp.sum(p, axis=1, keepdims=True) for p in v) / BRANCH_W
    vc = [p - mu for p in v]
    var = sum(jnp.sum(p * p, axis=1, keepdims=True) for p in vc) / BRANCH_W
    inv = lax.rsqrt(var + LN_EPS)
    trif = _tril(W[0].shape[0]).astype(F32)
    out = []
    for g in range(4):
        vn = vc[g] * inv * lng[g] + lnb[g]
        out.append(u[g] * (mm(W[g] * trif, vn, "nn", "1") + _col(bt, g)))
    return [out], S


def foxc_chunk(xs, ps, S, h):
    (sm,), ((fb,),), (carry,) = xs[0], ps, S
    lane = _lane()
    ls = jnp.where((lane >= FF0) & (lane < FF0 + 8), jax.nn.log_sigmoid(sm + fb), 0.0)
    c = mm(_tril(sm.shape[0]).astype(F32), ls, "nn", "xa") + carry
    return [[c]], [_last_row(c)]


HALO = 8


def _conv_tiles(T, C):
    return min(T, 512), _pick(C, 512)


def conv_fwd(x, w, b, name):
    T, C = x.shape
    tm, cb = _conv_tiles(T, C)

    def body(xp_ref, x_ref, w_ref, b_ref, o_ref):
        i = pl.program_id(1)
        e = jnp.concatenate([xp_ref[...] * (i > 0).astype(F32), x_ref[...]], axis=0)
        pre = b_ref[...] + sum(w_ref[k:k + 1, :] * e[5 + k:5 + k + tm, :] for k in range(4))
        o_ref[...] = _silu(pre)

    hb = tm // HALO
    return _pcall(body, grid=(C // cb, T // tm),
                  in_specs=[pl.BlockSpec((HALO, cb), lambda j, i: (jnp.maximum(i * hb - 1, 0), j)), pl.BlockSpec((tm, cb), lambda j, i: (i, j)),
                            pl.BlockSpec((4, cb), lambda j, i: (0, j)), pl.BlockSpec((1, cb), lambda j, i: (0, j))],
                  out_specs=pl.BlockSpec((tm, cb), lambda j, i: (i, j)), out_shape=_S((T, C)), name=name)(x, x, w, b)


def conv_bwd(x, w, b, dact, name):
    T, C = x.shape
    tm, cb = _conv_tiles(T, C)
    nt = T // tm

    def body(xp_ref, x_ref, xn_ref, w_ref, b_ref, d_ref, dn_ref, dx_ref, dw_ref, db_ref):
        i = pl.program_id(1)
        has_prev, has_next = (i > 0).astype(F32), (i < nt - 1).astype(F32)
        e = jnp.concatenate([xp_ref[...] * has_prev, x_ref[...], xn_ref[...] * has_next], axis=0)
        pre = b_ref[...] + sum(w_ref[k:k + 1, :] * e[5 + k:5 + k + tm + 8, :] for k in range(4))
        de = jnp.concatenate([d_ref[...], dn_ref[...] * has_next], axis=0)
        sg = jax.nn.sigmoid(pre)
        dpre = de * (sg * (1.0 + pre * (1.0 - sg)))
        dx_ref[...] = sum(w_ref[k:k + 1, :] * dpre[3 - k:3 - k + tm, :] for k in range(4))
        dcur = dpre[0:tm, :]
        dw = jnp.concatenate([jnp.sum(dcur * e[5 + k:5 + k + tm, :], axis=0, keepdims=True) for k in range(4)], axis=0)
        db = jnp.sum(dcur, axis=0, keepdims=True)

        @pl.when(i == 0)
        def _():
            dw_ref[...] = dw
            db_ref[...] = db

        @pl.when(i > 0)
        def _():
            dw_ref[...] += dw
            db_ref[...] += db

    blk = lambda f: pl.BlockSpec((tm, cb), f)
    hb = tm // HALO
    before = pl.BlockSpec((HALO, cb), lambda j, i: (jnp.maximum(i * hb - 1, 0), j))
    after = pl.BlockSpec((HALO, cb), lambda j, i: (jnp.minimum((i + 1) * hb, nt * hb - 1), j))
    return _pcall(body, grid=(C // cb, nt),
                  in_specs=[before, blk(lambda j, i: (i, j)), after,
                            pl.BlockSpec((4, cb), lambda j, i: (0, j)), pl.BlockSpec((1, cb), lambda j, i: (0, j)),
                            blk(lambda j, i: (i, j)), after],
                  out_specs=[blk(lambda j, i: (i, j)), pl.BlockSpec((4, cb), lambda j, i: (0, j)), pl.BlockSpec((1, cb), lambda j, i: (0, j))],
                  out_shape=[_S((T, C)), _S((4, C)), _S((1, C))], name=name)(x, x, x, w, b, dact, dact)


FOX_SCALE = 64 ** -0.5
LOG2E = 1.4426950408889634


def _spare(e, i):
    return (_lane() == 64 * (1 - e) + i).astype(F32)


def _lanes_of(e):
    lane = _lane()
    return ((lane < 64) if e == 0 else (lane >= 64)).astype(F32)


def _col3(col, e, first):
    c1 = col.astype(BF).astype(F32)
    c2 = (col - c1).astype(BF).astype(F32)
    c3 = (col - c1 - c2).astype(BF).astype(F32)
    return c1 * _spare(e, first) + c2 * _spare(e, first + 1) + c3 * _spare(e, first + 2)


def _ones3(e, first):
    return _spare(e, first) + _spare(e, first + 1) + _spare(e, first + 2)


def _causal_bias(n):
    return jnp.where(_iota((n, n), 0) >= _iota((n, n), 1), 0.0, -jnp.inf).astype(F32)


def _c_col(cc, hh):
    return jnp.sum(cc * (_lane() == FF0 + hh).astype(F32), axis=1, keepdims=True) * LOG2E


def _pair_spec(tq, row_of):
    return pl.BlockSpec((None, 2, tq, LANES), lambda hp, a, b: (hp, 0, row_of(a, b), 0))


def fox_prep(qkv, ccol, name):
    T = qkv.shape[0]
    tq = min(T, 512)

    def body(q_ref, k_ref, v_ref, cc_ref, qa_ref, ka_ref, va_ref):
        hp = pl.program_id(0)
        q, k, v, cc = q_ref[...], k_ref[...], v_ref[...], cc_ref[...]
        for e in range(2):
            me = _lanes_of(e)
            c2 = _c_col(cc, 2 * hp + e)
            qa_ref[e] = (q * me * (FOX_SCALE * LOG2E) + _col3(c2, e, 0) + _ones3(e, 3)).astype(BF)
            ka_ref[e] = (k * me + _ones3(e, 0) + _col3(-c2, e, 3) + _ones3(e, 6)).astype(BF)
            va_ref[e] = (v * me + (1.0 - me)).astype(BF)

    blk = lambda off: pl.BlockSpec((tq, LANES), lambda hp, i: (i, off + hp))
    out = pl.BlockSpec((None, 2, tq, LANES), lambda hp, i: (hp, 0, i, 0))
    return _pcall(body, grid=(4, T // tq), in_specs=[blk(0), blk(4), blk(8), pl.BlockSpec((tq, LANES), lambda hp, i: (i, 0))],
                  out_specs=[out] * 3, out_shape=[_S((4, 2, T, LANES), BF)] * 3, name=name)(qkv, qkv, qkv, ccol)


def fox_fwd(qa, ka, va, name, rider=None):
    T = qa.shape[2]
    tq = min(T, 512)
    nq = T // tq

    def body(qa_ref, ka_ref, va_ref, o_ref, lse_ref, m_s, acc, causal):
        i, j = pl.program_id(1), pl.program_id(2)

        @pl.when((pl.program_id(0) == 0) & (i == 0) & (j == 0))
        def _():
            causal[...] = _causal_bias(tq)

        @pl.when(j == 0)
        def _():
            m_s[...] = jnp.full(m_s.shape, -jnp.inf, F32)
            acc[...] = jnp.zeros(acc.shape, F32)

        def step(diagonal):
            for e in range(2):
                s = _dotb(qa_ref[e], ka_ref[e], NT)
                if diagonal:
                    s = s + causal[...]
                m_old = m_s[e]
                m_new = jnp.maximum(m_old, jnp.max(s, axis=1, keepdims=True))
                p = jnp.exp2(s - m_new)
                m_s[e] = m_new
                acc[e] = acc[e] * jnp.exp2(m_old - m_new) + _dotb(p.astype(BF), va_ref[e], NN)

        @pl.when(j < i)
        def _():
            step(False)

        @pl.when(j == i)
        def _():
            step(True)
            lane = _lane()
            o, lse = 0.0, 0.0
            for e in range(2):
                l = jnp.sum(acc[e] * _spare(e, 0), axis=1, keepdims=True)
                o = o + acc[e] * _lanes_of(e) / l
                lse = lse + (m_s[e] + jnp.log2(l)) * (lane == e).astype(F32)
            o_ref[...] = o
            lse_ref[...] = lse

    kv = _pair_spec(tq, lambda i, j: jnp.minimum(j, i))
    return hosted_call(body, rider, grid=(4, nq, nq), in_specs=[_pair_spec(tq, lambda i, j: i), kv, kv],
                       out_specs=[pl.BlockSpec((tq, LANES), lambda hp, i, j: (i, hp)), pl.BlockSpec((None, tq, LANES), lambda hp, i, j: (hp, i, 0))],
                       out_shape=[_S((T, BRANCH_W)), _S((4, T, LANES))],
                       scratch_shapes=[pltpu.VMEM((2, tq, 1), F32), pltpu.VMEM((2, tq, LANES), F32), pltpu.VMEM((tq, tq), F32)],
                       name=name, args=[qa, ka, va])


def fox_prep_bwd(qkv, qa, o, lse, do, name):
    T = qkv.shape[0]
    tq = min(T, 512)

    def body(q_ref, k_ref, qa_ref, o_ref, lse_ref, do_ref, qb_ref, doa_ref, qd_ref, kd_ref):
        q, k, dov = q_ref[...], k_ref[...], do_ref[...]
        dd = dov * o_ref[...]
        lane = _lane()
        for e in range(2):
            me = _lanes_of(e)
            lse_e = jnp.sum(lse_ref[...] * (lane == e).astype(F32), axis=1, keepdims=True)
            qb_ref[e] = (qa_ref[e].astype(F32) + _col3(-lse_e, e, 6)).astype(BF)
            doa_ref[e] = (dov * me + _col3(-jnp.sum(dd * me, axis=1, keepdims=True), e, 0)).astype(BF)
            qd_ref[e] = (q * me * FOX_SCALE + _spare(e, 0)).astype(BF)
            kd_ref[e] = (k * me * FOX_SCALE + _spare(e, 0)).astype(BF)

    blk = lambda off: pl.BlockSpec((tq, LANES), lambda hp, i: (i, off + hp))
    pair = pl.BlockSpec((None, 2, tq, LANES), lambda hp, i: (hp, 0, i, 0))
    return _pcall(body, grid=(4, T // tq),
                  in_specs=[blk(0), blk(4), pair, blk(0), pl.BlockSpec((None, tq, LANES), lambda hp, i: (hp, i, 0)), blk(0)],
                  out_specs=[pair] * 4, out_shape=[_S((4, 2, T, LANES), BF)] * 4, name=name)(qkv, qkv, qa, o, lse, do)


def fox_bwd(qb, ka, va, doa, qd, kd, name, rider=None):
    T = qb.shape[2]
    tq = min(T, 512)
    nq = T // tq

    def body(qb_ref, ka_ref, va_ref, doa_ref, qd_ref, kd_ref, dq_ref, dk_ref, dv_ref, dcc_ref, dq_s, dk_s, dv_s, causal):
        hp, j, ii = pl.program_id(0), pl.program_id(1), pl.program_id(2)
        i = jnp.maximum(ii, j)

        @pl.when((hp == 0) & (j == 0) & (ii == 0))
        def _():
            dcc_ref[...] = jnp.zeros(dcc_ref.shape, F32)
            causal[...] = _causal_bias(tq)

        @pl.when((j == 0) & (ii == 0))
        def _():
            dq_s[...] = jnp.zeros(dq_s.shape, F32)

        @pl.when(ii == 0)
        def _():
            dk_s[...] = jnp.zeros(dk_s.shape, F32)
            dv_s[...] = jnp.zeros(dv_s.shape, F32)

        def step(diagonal):
            rows = pl.ds(pl.multiple_of(i * tq, tq), tq)
            for e in range(2):
                s = _dotb(qb_ref[e], ka_ref[e], NT)
                if diagonal:
                    s = s + causal[...]
                p = jnp.exp2(s)
                ds = (p * _dotb(doa_ref[e], va_ref[e], NT)).astype(BF)
                dv_s[e] += _dotb(p.astype(BF), doa_ref[e], TN)
                dq_s[e, rows, :] += _dotb(ds, kd_ref[e], NN)
                dk_s[e] += _dotb(ds, qd_ref[e], TN)

        @pl.when(ii > j)
        def _():
            step(False)

        @pl.when(ii == j)
        def _():
            step(True)

        def fold(acc, sign):
            grad, dc = 0.0, 0.0
            for e in range(2):
                a = acc[e]
                grad = grad + a * _lanes_of(e)
                dc = dc + sign * jnp.sum(a * _spare(e, 0), axis=1, keepdims=True) * (_lane() == FF0 + 2 * hp + e).astype(F32)
            return grad, dc

        @pl.when(ii == nq - 1)
        def _():
            grad, dc = fold(dk_s, -1.0)
            dk_ref[...] = grad
            dv_ref[...] = dv_s[0] * _lanes_of(0) + dv_s[1] * _lanes_of(1)
            dcc_ref[pl.ds(pl.multiple_of(j * tq, tq), tq), :] += dc

        @pl.when((j == nq - 1) & (ii == nq - 1))
        def _():
            grad, dc = fold(dq_s, 1.0)
            dq_ref[...] = grad
            dcc_ref[...] += dc

    irow, jrow = _pair_spec(tq, lambda j, ii: jnp.maximum(ii, j)), _pair_spec(tq, lambda j, ii: j)
    jout = pl.BlockSpec((tq, LANES), lambda hp, j, ii: (j, hp))
    return hosted_call(body, rider, grid=(4, nq, nq), in_specs=[irow, jrow, jrow, irow, irow, jrow],
                       out_specs=[pl.BlockSpec((T, LANES), lambda hp, j, ii: (0, hp)), jout, jout, pl.BlockSpec((T, LANES), lambda hp, j, ii: (0, 0))],
                       out_shape=[_S((T, BRANCH_W)), _S((T, BRANCH_W)), _S((T, BRANCH_W)), _S((T, LANES))],
                       scratch_shapes=[pltpu.VMEM((2, T, LANES), F32), pltpu.VMEM((2, tq, LANES), F32), pltpu.VMEM((2, tq, LANES), F32),
                                       pltpu.VMEM((tq, tq), F32)],
                       name=name, args=[qb, ka, va, doa, qd, kd])


def _acc_out(ref, val, first):
    @pl.when(first)
    def _():
        ref[...] = val

    @pl.when(jnp.logical_not(first))
    def _():
        ref[...] += val


def _row(tm, c):
    return pl.BlockSpec((tm, c), lambda i: (i, 0))


def _full(shape):
    return pl.BlockSpec(shape, lambda *_: (0,) * len(shape))


def ln_fwd(x, g, b, name):
    T, C = x.shape
    tm = min(T, 512)

    def body(x_ref, g_ref, b_ref, o_ref):
        o_ref[...] = _ln(x_ref[...], g_ref[...], b_ref[...])

    return _pcall(body, grid=(T // tm,), in_specs=[_row(tm, C), _full((1, C)), _full((1, C))], out_specs=_row(tm, C),
                  out_shape=_S((T, C)), name=name)(x, g, b)


def ln_bwd(x, g, b, dy, name):
    T, C = x.shape
    tm = min(T, 512)

    def body(x_ref, g_ref, b_ref, dy_ref, dx_ref, dg_ref, db_ref):
        _, vjp = jax.vjp(_ln, x_ref[...], g_ref[...], b_ref[...])
        dx, dg, db = vjp(dy_ref[...])
        dx_ref[...] = dx
        first = pl.program_id(0) == 0
        _acc_out(dg_ref, dg, first)
        _acc_out(db_ref, db, first)

    return _pcall(body, grid=(T // tm,), in_specs=[_row(tm, C), _full((1, C)), _full((1, C)), _row(tm, C)],
                  out_specs=[_row(tm, C), _full((1, C)), _full((1, C))], out_shape=[_S((T, C)), _S((1, C)), _S((1, C))], name=name)(x, g, b, dy)


def loss_head(h, target, name):
    T, C = h.shape
    tm = min(T, 512)

    def body(h_ref, t_ref, d_ref, l_ref):
        e = h_ref[...] - t_ref[...]
        d_ref[...] = e * (1.0 / C)
        part = jnp.sum(jnp.sum(e * e, axis=1, keepdims=True), axis=0, keepdims=True) * (0.5 / C)
        _acc_out(l_ref, part, pl.program_id(0) == 0)

    return _pcall(body, grid=(T // tm,), in_specs=[_row(tm, C), _row(tm, C)], out_specs=[_row(tm, C), _full((1, 1))],
                  out_shape=[_S((T, C)), _S((1, 1))], name=name)(h, target)


def add3(a, b, c, name):
    T, C = a.shape
    tm = min(T, 512)

    def body(a_ref, b_ref, c_ref, o_ref):
        o_ref[...] = a_ref[...] + b_ref[...] + c_ref[...]

    return _pcall(body, grid=(T // tm,), in_specs=[_row(tm, C)] * 3, out_specs=_row(tm, C), out_shape=_S((T, C)), name=name)(a, b, c)


def _wb_spec(l):
    return pl.BlockSpec((None, 4, BRANCH_W, D_MODEL), lambda *_: (l, 0, 0, 0))


def merge_fwd(ys, gl, gb, wb, l, name):
    T = gl.shape[0]
    tm = min(T, 256)

    def body(y0, y1, y2, y3, gl_ref, gb_ref, wb_ref, o_ref):
        acc = 0.0
        for i, y in enumerate((y0, y1, y2, y3)):
            z = _dotb(y[...].astype(BF), wb_ref[i], NN)
            g = jax.nn.sigmoid(gl_ref[:, i * D_MODEL:(i + 1) * D_MODEL] + gb_ref[i:i + 1, :])
            acc = acc + g * z
        o_ref[...] = acc

    return _pcall(body, grid=(T // tm,), in_specs=[_row(tm, BRANCH_W)] * 4 + [_row(tm, 4 * D_MODEL), _full((4, D_MODEL)), _wb_spec(l)],
                  out_specs=_row(tm, D_MODEL), out_shape=_S((T, D_MODEL)), name=name)(*ys, gl, gb, wb)


def merge_bwd(ys, gl, gb, wb, l, dm, name):
    T = gl.shape[0]
    tm = min(T, 256)

    def body(y0, y1, y2, y3, gl_ref, gb_ref, wb_ref, dm_ref, d0, d1, d2, d3, dgl_ref, dz_ref, dgb_ref):
        dmv = dm_ref[...]
        first = pl.program_id(0) == 0
        for i, (y, d) in enumerate(zip((y0, y1, y2, y3), (d0, d1, d2, d3))):
            cols = slice(i * D_MODEL, (i + 1) * D_MODEL)
            z = _dotb(y[...].astype(BF), wb_ref[i], NN)
            g = jax.nn.sigmoid(gl_ref[:, cols] + gb_ref[i:i + 1, :])
            dgl = dmv * z * (g * (1.0 - g))
            dz = (g * dmv).astype(BF)
            dgl_ref[:, cols] = dgl
            dz_ref[:, cols] = dz
            d[...] = _dotb(dz, wb_ref[i], NT)
            _acc_out(dgb_ref.at[i:i + 1, :], jnp.sum(dgl, axis=0, keepdims=True), first)

    return _pcall(body, grid=(T // tm,),
                  in_specs=[_row(tm, BRANCH_W)] * 4 + [_row(tm, 4 * D_MODEL), _full((4, D_MODEL)), _wb_spec(l), _row(tm, D_MODEL)],
                  out_specs=[_row(tm, BRANCH_W)] * 4 + [_row(tm, 4 * D_MODEL), _row(tm, 4 * D_MODEL), _full((4, D_MODEL))],
                  out_shape=[_S((T, BRANCH_W))] * 4 + [_S((T, 4 * D_MODEL)), _S((T, 4 * D_MODEL), BF), _S((4, D_MODEL))], name=name)(
                      *ys, gl, gb, wb, dm)


def _wout_spec(l):
    return pl.BlockSpec((None, D_MODEL, D_MODEL), lambda *_: (l, 0, 0))


def out_fwd(merged, h, wout, l, g, b, name):
    T = h.shape[0]
    tm = min(T, 512)

    def body(m_ref, h_ref, w_ref, g_ref, b_ref, u_ref, o_ref):
        u = ALPHA * h_ref[...] + _dotb(m_ref[...].astype(BF), w_ref[...], NN)
        u_ref[...] = u
        o_ref[...] = _ln(u, g_ref[...], b_ref[...])

    C = D_MODEL
    return _pcall(body, grid=(T // tm,), in_specs=[_row(tm, C), _row(tm, C), _wout_spec(l), _full((1, C)), _full((1, C))],
                  out_specs=[_row(tm, C), _row(tm, C)], out_shape=[_S((T, C)), _S((T, C))], name=name)(merged, h, wout, g, b)


def out_bwd(u, dy, g, b, wout, l, name):
    T, C = u.shape
    tm = min(T, 512)

    def body(u_ref, dy_ref, g_ref, b_ref, w_ref, du_ref, dm_ref, dg_ref, db_ref):
        _, vjp = jax.vjp(_ln, u_ref[...], g_ref[...], b_ref[...])
        du, dg, db = vjp(dy_ref[...])
        du_ref[...] = du
        dm_ref[...] = _dotb(du.astype(BF), w_ref[...], NT)
        first = pl.program_id(0) == 0
        _acc_out(dg_ref, dg, first)
        _acc_out(db_ref, db, first)

    return _pcall(body, grid=(T // tm,), in_specs=[_row(tm, C), _row(tm, C), _full((1, C)), _full((1, C)), _wout_spec(l)],
                  out_specs=[_row(tm, C), _row(tm, C), _full((1, C)), _full((1, C))],
                  out_shape=[_S((T, C)), _S((T, C)), _S((1, C)), _S((1, C))], name=name)(u, dy, g, b, wout)


def ff_fwd(h, wup, wdown, l, g, b, name):
    T, C = h.shape
    F = wup.shape[2]
    tm, tf = min(T, 512), 1024
    nf = F // tf

    def body(h_ref, wu_ref, wd_ref, g_ref, b_ref, u_ref, o_ref, acc):
        f = pl.program_id(1)
        a = _dotb(h_ref[...].astype(BF), wu_ref[...], NN)
        r = jnp.square(jnp.maximum(a, 0.0))
        p = _dotb(r.astype(BF), wd_ref[...], NN)
        _acc_out(acc, p, f == 0)

        @pl.when(f == nf - 1)
        def _():
            u = ALPHA * h_ref[...] + acc[...]
            u_ref[...] = u
            o_ref[...] = _ln(u, g_ref[...], b_ref[...])

    row = pl.BlockSpec((tm, C), lambda i, f: (i, 0))
    return _pcall(body, grid=(T // tm, nf),
                  in_specs=[row, pl.BlockSpec((None, C, tf), lambda i, f: (l, 0, f)), pl.BlockSpec((None, tf, C), lambda i, f: (l, f, 0)),
                            _full((1, C)), _full((1, C))],
                  out_specs=[row, row], out_shape=[_S((T, C)), _S((T, C))], scratch_shapes=[pltpu.VMEM((tm, C), F32)], name=name)(h, wup, wdown, g, b)


def ff_bwd(u, dy, h, g, b, wup, wdown, l, name):
    T, C = h.shape
    F = wup.shape[2]
    tm, tf = min(T, 512), 1024
    nf = F // tf

    def body(u_ref, dy_ref, h_ref, g_ref, b_ref, wu_ref, wd_ref, du_ref, dh_ref, da_ref, r_ref, dg_ref, db_ref, du_s, acc):
        i, f = pl.program_id(0), pl.program_id(1)

        @pl.when(f == 0)
        def _():
            _, vjp = jax.vjp(_ln, u_ref[...], g_ref[...], b_ref[...])
            du, dg, db = vjp(dy_ref[...])
            du_s[...] = du
            du_ref[...] = du
            _acc_out(dg_ref, dg, i == 0)
            _acc_out(db_ref, db, i == 0)

        a = _dotb(h_ref[...].astype(BF), wu_ref[...], NN)
        ap = jnp.maximum(a, 0.0)
        dr = _dotb(du_s[...].astype(BF), wd_ref[...], NT)
        da = (dr * (2.0 * ap)).astype(BF)
        da_ref[...] = da
        r_ref[...] = jnp.square(ap).T.astype(BF)
        _acc_out(acc, _dotb(da, wu_ref[...], NT), f == 0)

        @pl.when(f == nf - 1)
        def _():
            dh_ref[...] = ALPHA * du_s[...] + acc[...]

    row = pl.BlockSpec((tm, C), lambda i, f: (i, 0))
    colf = pl.BlockSpec((tm, tf), lambda i, f: (i, f))
    return _pcall(body, grid=(T // tm, nf),
                  in_specs=[row, row, row, _full((1, C)), _full((1, C)), pl.BlockSpec((None, C, tf), lambda i, f: (l, 0, f)),
                            pl.BlockSpec((None, tf, C), lambda i, f: (l, f, 0))],
                  out_specs=[row, row, colf, pl.BlockSpec((tf, tm), lambda i, f: (f, i)), _full((1, C)), _full((1, C))],
                  out_shape=[_S((T, C)), _S((T, C)), _S((T, F), BF), _S((F, T), BF), _S((1, C)), _S((1, C))],
                  scratch_shapes=[pltpu.VMEM((tm, C), F32), pltpu.VMEM((tm, C), F32)], name=name)(u, dy, h, g, b, wup, wdown)


MESH_ID = pl.DeviceIdType.MESH
_ANY = pl.BlockSpec(memory_space=pl.ANY)


def _window(ref, ax, idx, n):
    if n < 0:
        return ref
    sel = idx if n == 0 else pl.ds(pl.multiple_of(idx * n, n), n)
    return ref.at[(slice(None),) * ax + (sel,)]


Rider = collections.namedtuple("Rider", "operands out_shape scratch start wait")


def hosted_call(body, rider, *, grid, in_specs, out_specs, out_shape, scratch_shapes, name, args):
    n_in, n_out, n_scr = len(in_specs), len(out_specs), len(scratch_shapes)
    if rider is None:
        return _pcall(body, grid=grid, in_specs=in_specs, out_specs=out_specs, out_shape=out_shape, scratch_shapes=scratch_shapes, name=name)(*args), []
    ri, ro = len(rider.operands), len(rider.out_shape)

    def wrapped(*refs):
        ins, r_in = refs[:n_in], refs[n_in:n_in + ri]
        o0 = n_in + ri
        outs, r_out = refs[o0:o0 + n_out], refs[o0 + n_out:o0 + n_out + ro]
        s0 = o0 + n_out + ro
        scr, r_scr = refs[s0:s0 + n_scr], refs[s0 + n_scr:]
        ids = [pl.program_id(i) for i in range(len(grid))]
        first = functools.reduce(jnp.logical_and, [i == 0 for i in ids])
        last = functools.reduce(jnp.logical_and, [i == g - 1 for i, g in zip(ids, grid)])

        @pl.when(first)
        def _():
            rider.start(r_in, r_out, r_scr)

        body(*ins, *outs, *scr)

        @pl.when(last)
        def _():
            rider.wait(r_in, r_out, r_scr)

    res = _pcall(wrapped, grid=grid, in_specs=list(in_specs) + [_ANY] * ri, out_specs=list(out_specs) + [_ANY] * ro,
                 out_shape=list(out_shape) + list(rider.out_shape), scratch_shapes=list(scratch_shapes) + list(rider.scratch),
                 name=name)(*args, *rider.operands)
    return res[:n_out], res[n_out:]


def comm_call(rider, name):
    ri = len(rider.operands)

    def body(*refs):
        r_in, r_out, r_scr = refs[:ri], refs[ri:ri + len(rider.out_shape)], refs[ri + len(rider.out_shape):]
        rider.start(r_in, r_out, r_scr)
        rider.wait(r_in, r_out, r_scr)

    return _pcall(body, in_specs=[_ANY] * ri, out_specs=[_ANY] * len(rider.out_shape), out_shape=list(rider.out_shape),
                  scratch_shapes=list(rider.scratch), name=name)(*rider.operands)


def gather_rider(shards, axes):
    K = len(shards)
    widths = [s.shape[a] for s, a in zip(shards, axes)]
    out_shape = [_S(s.shape[:a] + (N_DEV * s.shape[a],) + s.shape[a + 1:], s.dtype) for s, a in zip(shards, axes)]

    def plan(x_refs, o_refs, sems):
        send_sems, recv_sems, local_sems = sems
        mx, my, mc = lax.axis_index("x"), lax.axis_index("y"), lax.axis_index("c")
        me, sibling = (mx, my, mc), (mx, my, 1 - mc)
        chips = [(1 - mx, my), (mx, 1 - my), (1 - mx, 1 - my)]

        def win(k, px, py, pc):
            return _window(o_refs[k], axes[k], 4 * px + 2 * py + pc, widths[k])

        def copy(k, slot, block, to, src=None):
            return pltpu.make_async_remote_copy(src_ref=win(k, *block) if src is None else src, dst_ref=win(k, *block),
                                                send_sem=send_sems.at[7 * k + slot], recv_sem=recv_sems.at[7 * k + slot],
                                                device_id=to, device_id_type=MESH_ID)

        mine = [pltpu.make_async_copy(x_refs[k], win(k, *me), local_sems.at[k]) for k in range(K)]
        first = []
        for k in range(K):
            first.append(copy(k, 0, me, sibling, src=x_refs[k]))
            first += [copy(k, 1 + j, me, (*chip, mc), src=x_refs[k]) for j, chip in enumerate(chips)]
        return me, sibling, chips, copy, mine, first

    def start(x_refs, o_refs, sems):
        _, _, _, _, mine, first = plan(x_refs, o_refs, sems)
        for cp in mine + first:
            cp.start()

    def wait(x_refs, o_refs, sems):
        me, sibling, chips, copy, mine, first = plan(x_refs, o_refs, sems)
        mc = me[2]
        passed = []
        for j, chip in enumerate(chips):
            for k in range(K):
                copy(k, 1 + j, (*chip, mc), me).wait_recv()
                passed.append(copy(k, 4 + j, (*chip, mc), sibling))
                passed[-1].start()
        for k in range(K):
            copy(k, 0, sibling, me).wait_recv()
        for j, chip in enumerate(chips):
            for k in range(K):
                copy(k, 4 + j, (*chip, 1 - mc), me).wait_recv()
        for cp in first + passed:
            cp.wait_send()
        for cp in mine:
            cp.wait()

    scratch = [pltpu.SemaphoreType.DMA((7 * K,)), pltpu.SemaphoreType.DMA((7 * K,)), pltpu.SemaphoreType.DMA((K,))]
    return Rider(list(shards), out_shape, scratch, start, wait)


def exchange_rider(items):
    ns = len(items)
    out_shape = [_S((N_DEV,) + tuple(it[3]), it[0].dtype) for it in items]

    def plan(src_refs, o_refs, sems):
        send_sems, recv_sems, local_sems = sems
        mx, my, mc = lax.axis_index("x"), lax.axis_index("y"), lax.axis_index("c")
        me = 4 * mx + 2 * my + mc
        remote, own = [], []
        for s, (_, ax, n, _) in enumerate(items):
            own.append(pltpu.make_async_copy(_window(src_refs[s], ax, me, n), o_refs[s].at[me], local_sems.at[s]))
            for k in range(1, N_DEV):
                px = 1 - mx if k & 4 else mx
                py = 1 - my if k & 2 else my
                pc = 1 - mc if k & 1 else mc
                remote.append(pltpu.make_async_remote_copy(
                    src_ref=_window(src_refs[s], ax, 4 * px + 2 * py + pc, n), dst_ref=o_refs[s].at[me],
                    send_sem=send_sems.at[7 * s + k - 1], recv_sem=recv_sems.at[7 * s + k - 1],
                    device_id=(px, py, pc), device_id_type=MESH_ID))
        return remote, own

    def start(src_refs, o_refs, sems):
        remote, own = plan(src_refs, o_refs, sems)
        for cp in own + remote:
            cp.start()

    def wait(src_refs, o_refs, sems):
        remote, own = plan(src_refs, o_refs, sems)
        for cp in remote + own:
            cp.wait()

    scratch = [pltpu.SemaphoreType.DMA((7 * ns,)), pltpu.SemaphoreType.DMA((7 * ns,)), pltpu.SemaphoreType.DMA((ns,))]
    return Rider([it[0] for it in items], out_shape, scratch, start, wait)


def reduce_adamw(rcvs, w, m, v, name):
    L = len(rcvs)
    _, A, B, C = rcvs[0].shape
    tb = B
    while tb > 8 and tb * C > (1 << 17):
        tb //= 2

    def body(*refs):
        r_refs, (w_ref, m_ref, v_ref, g_ref, d_ref, mo_ref, vo_ref) = refs[:L], refs[L:]
        for k in range(L):
            @pl.when(pl.program_id(0) == k)
            def _(k=k):
                g = r_refs[k][0].astype(F32)
                for d in range(1, N_DEV):
                    g = g + r_refs[k][d].astype(F32)
                mn = ADAM_B1 * m_ref[...] + (1.0 - ADAM_B1) * g
                vn = ADAM_B2 * v_ref[...] + (1.0 - ADAM_B2) * jnp.square(g)
                m_hat = mn / (1.0 - ADAM_B1 ** ADAM_STEP)
                v_hat = vn / (1.0 - ADAM_B2 ** ADAM_STEP)
                g_ref[...] = g
                d_ref[...] = -ADAM_LR * (m_hat / (jnp.sqrt(v_hat) + ADAM_EPS) + ADAM_WD * w_ref[...])
                mo_ref[...] = mn
                vo_ref[...] = vn

    def rspec(k):
        return pl.BlockSpec((N_DEV, None, tb, C), lambda l, a, i: (0, jnp.where(l == k, a, 0), jnp.where(l == k, i, 0), 0))

    blk = pl.BlockSpec((None, tb, C), lambda l, a, i: (l * A + a, i, 0))
    return _pcall(body, grid=(L, A, B // tb), in_specs=[rspec(k) for k in range(L)] + [blk, blk, blk],
                  out_specs=[blk] * 4, out_shape=[_S((L * A, B, C))] * 4, name=name)(*rcvs, w, m, v)


def _w_in_pieces(g0, g1):
    per = D_IN // N_DEV
    return [(d, max(g0, d * per) - d * per, min(g1, (d + 1) * per) - d * per) for d in range(N_DEV) if max(g0, d * per) < min(g1, (d + 1) * per)]


def repack_w_in(w8, name):
    _, L, R, per = w8.shape
    tr = 256

    def cols(x_ref, g0, g1):
        return [x_ref[d, :, a:b] for d, a, b in _w_in_pieces(g0, g1)]

    def body(x_ref, *o_refs):
        for (name_, i), o_ref in zip(SEGS, o_refs):
            o_ref[...] = jnp.concatenate(cols(x_ref, _OFF[i], _OFF[i + 1]), axis=1)
        parts, at = [], 0
        for i, lane0 in SMALL_SRC:
            assert lane0 == at
            parts += cols(x_ref, _OFF[i], _OFF[i + 1])
            at += IN_SIZES[i]
        parts.append(jnp.zeros((tr, LANES - at), w8.dtype))
        o_refs[-1][...] = jnp.concatenate(parts, axis=1)

    widths = [IN_SIZES[i] for _, i in SEGS] + [LANES]
    outs = _pcall(body, grid=(L, R // tr), in_specs=[pl.BlockSpec((N_DEV, None, tr, per), lambda l, r: (0, l, r, 0))],
                  out_specs=[pl.BlockSpec((None, tr, w), lambda l, r: (l, r, 0)) for w in widths],
                  out_shape=[_S((L, R, w), w8.dtype) for w in widths], name=name)(w8)
    return dict(zip(SEG_NAMES, outs))


def repack_dw_in(dseg, name):
    R = dseg["z"].shape[0]
    per = D_IN // N_DEV
    tr = 128
    src = {i: (k, 0) for k, (_, i) in enumerate(SEGS)}
    src.update({i: (len(SEGS), lane0) for i, lane0 in SMALL_SRC})

    def body(*refs):
        s_refs, o_ref = refs[:-1], refs[-1]
        for d in range(N_DEV):
            parts = []
            for i in range(len(IN_SIZES)):
                g0, g1 = max(_OFF[i], d * per), min(_OFF[i + 1], (d + 1) * per)
                if g0 < g1:
                    k, c0 = src[i]
                    parts.append(s_refs[k][:, c0 + g0 - _OFF[i]:c0 + g1 - _OFF[i]])
            o_ref[d] = jnp.concatenate(parts, axis=1)

    arrs = [dseg[n] for n in SEG_NAMES]
    return _pcall(body, grid=(R // tr,), in_specs=[pl.BlockSpec((tr, a.shape[1]), lambda r: (r, 0)) for a in arrs],
                  out_specs=pl.BlockSpec((N_DEV, tr, per), lambda r: (0, r, 0)), out_shape=_S((N_DEV, R, per), arrs[0].dtype), name=name)(*arrs)


WEIGHTS = ("ln_in_g", "ln_in_b", "w_in", "ssd_conv_w", "ssd_conv_b", "ssd_dt_bias", "ssd_a_log", "ssd_d", "ssd_norm_w", "dn_conv_w",
           "dn_a_log", "dn_dt_bias", "dn_norm_w", "sg_ln_g", "sg_ln_b", "sg_w", "sg_b", "fox_f_bias", "gate_b", "w_branch", "w_out",
           "ln1_g", "ln1_b", "w_up", "w_down", "ln2_g", "ln2_b")
SHARDED = {"w_in": 2, "ssd_conv_w": 2, "dn_conv_w": 2, "gate_b": 2, "w_branch": 3, "w_out": 1, "w_up": 2, "w_down": 1}
SLABBED = ("w_in", "dn_conv_w")
MATMUL_WEIGHTS = ("w_in", "w_branch", "w_out", "w_up", "w_down")
REPLICATED_ENTRY = ("ln_in_g", "ln_in_b")
REPLICATED_LAYER = tuple(n for n in WEIGHTS if n not in SHARDED and n not in REPLICATED_ENTRY and n != "sg_w")
SEG_NAMES = tuple(n for n, _ in SEGS) + ("small",)
PACK_COLS = 1024


def _lanes(vec, off):
    return jnp.pad(vec, (off, LANES - off - vec.shape[0]))[None]


def _pack_small(parts, row_mult):
    flat = jnp.concatenate([q.reshape(-1) for q in parts])
    rows = -(-flat.shape[0] // (PACK_COLS * row_mult)) * row_mult
    return jnp.pad(flat, (0, rows * PACK_COLS - flat.shape[0])).reshape(1, rows, PACK_COLS)


EARLY = ("w_branch", "w_out", "w_up", "w_down", "gate_b")
LATE = ("w_in", "ssd_conv_w", "dn_conv_w")
WHOLE = ("sg_w",)


def _gather_rider(p, l, names):
    shards, axes = [], []
    for n in names:
        s = p[n][l:l + 1]
        s = s.astype(BF) if n in MATMUL_WEIGHTS else s
        shards.append(s[None] if n in SLABBED else s)
        axes.append(0 if n in SLABBED else SHARDED[n])
    return gather_rider(shards, axes)


def _exchange_items(g, p, names):
    items = []
    for n in names:
        local = p[n].shape[1:]
        if n in WHOLE:
            items.append((g[n], 0, -1, local))
        elif n in SLABBED:
            items.append((g[n], 0, 0, local))
        else:
            items.append((g[n], SHARDED[n] - 1, local[SHARDED[n] - 1], local))
    return items


def _use_gathered(w, names, arrays, l):
    for n, arr in zip(names, arrays):
        if n == "w_in":
            w[n] = repack_w_in(arr, f"w_in_repack_{l}")
        elif n == "ssd_conv_w":
            w["ssd_cw"] = arr[0]
        elif n == "dn_conv_w":
            w["dn_cw"] = jnp.moveaxis(arr[:, 0], 0, 1).reshape(4, 3 * BRANCH_W)
        elif n == "gate_b":
            w[n] = arr[0]
        else:
            w[n] = arr


def _layer_weights(p, l):
    w = {}
    w["ssd_cb"] = p["ssd_conv_b"][l][None]
    w["dn_cb"] = jnp.zeros((1, 3 * BRANCH_W), F32)
    w["ssd_ps"] = [_lanes(p["ssd_dt_bias"][l], DT0), _lanes(p["ssd_a_log"][l], DT0), _lanes(p["ssd_d"][l], DT0), p["ssd_norm_w"][l][None]]
    w["dn_ps"] = [_lanes(p["dn_a_log"][l], A0), _lanes(p["dn_dt_bias"][l], A0), p["dn_norm_w"][l][None]]
    w["sg_ps"] = [p["sg_ln_g"][l][None], p["sg_ln_b"][l][None], p["sg_w"][l], jnp.pad(p["sg_b"][l].T, ((0, 0), (0, LANES - 4)))]
    w["fox_ps"] = [_lanes(p["fox_f_bias"][l], FF0)]
    for n in ("ln1_g", "ln1_b", "ln2_g", "ln2_b"):
        w[n] = p[n][l][None]
    return w


def _scan_specs(T, a):
    c0 = lambda c, h: (c, 0)
    ssd = dict(f=ssd_chunk, xs=[(a["z"], (128, 512), c0), (a["xbc_act"], (128, 1024), c0), (a["small"], (128, LANES), c0)],
               ys=[((T, BRANCH_W), (128, BRANCH_W), c0)], state=(4, LANES, LANES), nc=T // 128, nh=1, shared=())
    dn = dict(f=dn_chunk, xs=[(a["dn_act"], (128, 3 * BRANCH_W), c0), (a["dngate"], (128, BRANCH_W), c0), (a["small"], (128, LANES), c0)],
              ys=[((T, BRANCH_W), (128, BRANCH_W), c0)], state=(4, LANES, LANES), nc=T // 128, nh=1, shared=())
    sg = dict(f=sg_chunk, xs=[(a["sguv"], (128, 1024), c0)], ys=[((T, BRANCH_W), (128, BRANCH_W), c0)], state=(1, 8, LANES), nc=T // 128, nh=1, shared=())
    fc = dict(f=foxc_chunk, xs=[(a["small"], (128, LANES), c0)],
              ys=[((T, LANES), (128, LANES), c0)], state=(1, 1, LANES), nc=T // 128, nh=1, shared=())
    return ssd, dn, sg, fc


def _layer_fwd(h, w, l, dn_rider=None, fox_rider=None):
    T = h.shape[0]
    a = {"h": h, **proj_all(h, w["w_in"], f"proj_{l}")}
    a["xbc_act"] = conv_fwd(a["xbc"], w["ssd_cw"], w["ssd_cb"], f"ssd_conv_{l}")
    a["dn_act"] = conv_fwd(a["dnqkv"], w["dn_cw"], w["dn_cb"], f"dn_conv_{l}")
    ssd, dn, sg, fc = _scan_specs(T, a)
    (a["ya"], a["ssd_st"]), _ = scan_fwd(f"ssd_fwd_{l}", ssd["f"], ssd["xs"], w["ssd_ps"], ssd["ys"], ssd["state"], ssd["nc"], ssd["nh"])
    (a["yb"], a["dn_st"]), got = scan_fwd(f"dn_fwd_{l}", dn["f"], dn["xs"], w["dn_ps"], dn["ys"], dn["state"], dn["nc"], dn["nh"], rider=dn_rider)
    _use_gathered(w, EARLY, got, l)
    (a["yc"], a["sg_st"]), _ = scan_fwd(f"sg_fwd_{l}", sg["f"], sg["xs"], w["sg_ps"], sg["ys"], sg["state"], sg["nc"], sg["nh"])
    (a["ccol"], a["fc_st"]), _ = scan_fwd(f"foxc_fwd_{l}", fc["f"], fc["xs"], w["fox_ps"], fc["ys"], fc["state"], fc["nc"], fc["nh"])
    a["fox_qa"], a["fox_ka"], a["fox_va"] = fox_prep(a["foxqkv"], a["ccol"], f"fox_prep_{l}")
    (a["yd"], a["lse"]), carried = fox_fwd(a["fox_qa"], a["fox_ka"], a["fox_va"], f"fox_fwd_{l}", rider=fox_rider)
    a["merged"] = merge_fwd([a["ya"], a["yb"], a["yc"], a["yd"]], a["gates"], w["gate_b"], w["w_branch"], 0, f"merge_fwd_{l}")
    a["u1"], a["h1"] = out_fwd(a["merged"], h, w["w_out"], 0, w["ln1_g"], w["ln1_b"], f"out_fwd_{l}")
    a["u2"], a["h2"] = ff_fwd(a["h1"], w["w_up"], w["w_down"], 0, w["ln2_g"], w["ln2_b"], f"ff_fwd_{l}")
    return a, carried


def _layer_bwd(dh2, a, w, l, p, late_above):
    T = dh2.shape[0]
    g = {}
    du2, dh1, da, r, dg2, db2 = ff_bwd(a["u2"], dh2, a["h1"], w["ln2_g"], w["ln2_b"], w["w_up"], w["w_down"], 0, f"ff_bwd_{l}")
    g["ln2_g"], g["ln2_b"] = dg2[0], db2[0]
    g["w_up"] = matmul_w(transpose_bf16(a["h1"], f"h1_t_{l}"), da[None], 0, "nn", f"dwup_{l}", out_dtype=BF)
    g["w_down"] = matmul_w(r, du2[None], 0, "nn", f"dwdown_{l}", out_dtype=BF)
    du1, dmerged, dg1, db1 = out_bwd(a["u1"], dh1, w["ln1_g"], w["ln1_b"], w["w_out"], 0, f"out_bwd_{l}")
    g["ln1_g"], g["ln1_b"] = dg1[0], db1[0]
    g["w_out"] = matmul_w(transpose_bf16(a["merged"], f"merged_t_{l}"), du1[None], 0, "nn", f"dwout_{l}", out_dtype=BF)
    ys = [a["ya"], a["yb"], a["yc"], a["yd"]]
    dya, dyb, dyc, dyd, dgl, dz, dgb = merge_bwd(ys, a["gates"], w["gate_b"], w["w_branch"], 0, dmerged, f"merge_bwd_{l}")
    g["gate_b"] = dgb
    g["w_branch"] = jnp.stack([matmul_tn(ys[i], dz, f"dwb{i}_{l}", b_col0=i * D_MODEL, n_cols=D_MODEL, out_dtype=BF) for i in range(4)])
    early = exchange_rider(_exchange_items(g, p, EARLY))
    dn_rider = early if late_above is None else exchange_rider(late_above)
    fox_rider = None if late_above is None else early
    ssd, dn, sg, fc = _scan_specs(T, a)
    (dz_ssd, dxbc_act, dsm_ssd, d_dtb, d_alog, d_dsk, d_nw), _ = scan_bwd(f"ssd_bwd_{l}", ssd["f"], ssd["xs"], w["ssd_ps"], ssd["ys"], [dya], a["ssd_st"],
                                                                           ssd["state"], ssd["nc"], ssd["nh"])
    g["ssd_dt_bias"], g["ssd_a_log"], g["ssd_d"], g["ssd_norm_w"] = d_dtb[0, DT0:DT0 + 8], d_alog[0, DT0:DT0 + 8], d_dsk[0, DT0:DT0 + 8], d_nw[0]
    dxbc, g["ssd_conv_w"], dcb = conv_bwd(a["xbc"], w["ssd_cw"], w["ssd_cb"], dxbc_act, f"ssd_conv_bwd_{l}")
    g["ssd_conv_b"] = dcb[0]
    (ddn_act, ddngate, dsm_dn, d_alog, d_dtb, d_nw), got_dn = scan_bwd(f"dn_bwd_{l}", dn["f"], dn["xs"], w["dn_ps"], dn["ys"], [dyb], a["dn_st"],
                                                                        dn["state"], dn["nc"], dn["nh"], rider=dn_rider)
    g["dn_a_log"], g["dn_dt_bias"], g["dn_norm_w"] = d_alog[0, A0:A0 + 4], d_dtb[0, A0:A0 + 4], d_nw[0]
    ddnqkv, g["dn_conv_w"], _ = conv_bwd(a["dnqkv"], w["dn_cw"], w["dn_cb"], ddn_act, f"dn_conv_bwd_{l}")
    (dsguv, d_lng, d_lnb, d_w, d_bt), _ = scan_bwd(f"sg_bwd_{l}", sg["f"], sg["xs"], w["sg_ps"], sg["ys"], [dyc], a["sg_st"], sg["state"], sg["nc"], sg["nh"])
    g["sg_ln_g"], g["sg_ln_b"], g["sg_w"], g["sg_b"] = d_lng[0], d_lnb[0], d_w, d_bt[:, :4].T
    qb, doa, qd, kd = fox_prep_bwd(a["foxqkv"], a["fox_qa"], a["yd"], a["lse"], dyd, f"fox_prep_bwd_{l}")
    (dfq, dfk, dfv, dccol), got_fox = fox_bwd(qb, a["fox_ka"], a["fox_va"], doa, qd, kd, f"fox_bwd_{l}", rider=fox_rider)
    (dsm_fox, d_fb), _ = scan_bwd(f"foxc_bwd_{l}", fc["f"], fc["xs"], w["fox_ps"], fc["ys"], [dccol], a["fc_st"], fc["state"], fc["nc"], fc["nh"])
    g["fox_f_bias"] = d_fb[0, FF0:FF0 + 8]
    dseg = {"z": dz_ssd, "xbc": dxbc, "dnqkv": ddnqkv, "dngate": ddngate, "sguv": dsguv,
            "foxqkv": jnp.concatenate([dfq, dfk, dfv], axis=1), "gates": dgl, "small": add3(dsm_ssd, dsm_dn, dsm_fox, f"dsmall_{l}")}
    h_t = transpose_bf16(a["h"], f"h_t_{l}")
    dwin = {n: matmul_w(h_t, dseg[n][None], 0, "nn", f"dwin_{n}_{l}", out_dtype=BF) for n in SEG_NAMES}
    g["w_in"] = repack_dw_in(dwin, f"dw_in_repack_{l}")
    g["dn_conv_w"] = jnp.moveaxis(g["dn_conv_w"].reshape(4, N_DEV, 3 * BRANCH_W // N_DEV), 1, 0)
    got = {(EARLY, l): got_dn} if late_above is None else {(LATE + WHOLE, l + 1): got_dn, (EARLY, l): got_fox}
    return dseg, du1, g, got


def proj_all(h, w_in, name):
    T = h.shape[0]
    tm = min(T, 256)
    ns = len(SEG_NAMES)

    def body(*refs):
        h_ref, w_refs, o_refs = refs[0], refs[1:1 + ns], refs[1 + ns:]
        hb = h_ref[...].astype(BF)
        for w_ref, o_ref in zip(w_refs, o_refs):
            o_ref[...] = _dotb(hb, w_ref[...], NN)

    widths = [w_in[n].shape[2] for n in SEG_NAMES]
    in_specs = [_row(tm, D_MODEL)]
    in_specs += [pl.BlockSpec((None,) + w_in[n].shape[1:], lambda i: (0, 0, 0), pipeline_mode=pl.Buffered(1)) for n in SEG_NAMES]
    outs = _pcall(body, grid=(T // tm,), in_specs=in_specs, out_specs=[_row(tm, wd) for wd in widths],
                  out_shape=[_S((T, wd)) for wd in widths], name=name)(h, *[w_in[n] for n in SEG_NAMES])
    return dict(zip(SEG_NAMES, outs))


def dh_all(dseg, w_in, add, name, rider=None):
    T = add.shape[0]
    tm = min(T, 256)
    ns = len(SEG_NAMES)

    def body(*refs):
        d_refs, w_refs, add_ref, o_ref = refs[:ns], refs[ns:2 * ns], refs[2 * ns], refs[2 * ns + 1]
        acc = ALPHA * add_ref[...]
        for d_ref, w_ref in zip(d_refs, w_refs):
            acc = acc + _dotb(d_ref[...].astype(BF), w_ref[...], NT)
        o_ref[...] = acc

    in_specs = [_row(tm, dseg[n].shape[1]) for n in SEG_NAMES]
    in_specs += [pl.BlockSpec((None,) + w_in[n].shape[1:], lambda i: (0, 0, 0), pipeline_mode=pl.Buffered(1)) for n in SEG_NAMES]
    in_specs.append(_row(tm, D_MODEL))
    return hosted_call(body, rider, grid=(T // tm,), in_specs=in_specs, out_specs=[_row(tm, D_MODEL)], out_shape=[_S((T, D_MODEL))],
                       scratch_shapes=[], name=name, args=[*[dseg[n] for n in SEG_NAMES], *[w_in[n] for n in SEG_NAMES], add])


def kernel(x, ln_in_g, ln_in_b, w_in, ssd_conv_w, ssd_conv_b, ssd_dt_bias, ssd_a_log, ssd_d, ssd_norm_w, dn_conv_w, dn_a_log, dn_dt_bias, dn_norm_w, sg_ln_g, sg_ln_b, sg_w, sg_b, fox_f_bias, gate_b, w_branch, w_out, ln1_g, ln1_b, w_up, w_down, ln2_g, ln2_b, loss_target, m_ln_in_g, m_ln_in_b, m_w_in, m_ssd_conv_w, m_ssd_conv_b, m_ssd_dt_bias, m_ssd_a_log, m_ssd_d, m_ssd_norm_w, m_dn_conv_w, m_dn_a_log, m_dn_dt_bias, m_dn_norm_w, m_sg_ln_g, m_sg_ln_b, m_sg_w, m_sg_b, m_fox_f_bias, m_gate_b, m_w_branch, m_w_out, m_ln1_g, m_ln1_b, m_w_up, m_w_down, m_ln2_g, m_ln2_b, v_ln_in_g, v_ln_in_b, v_w_in, v_ssd_conv_w, v_ssd_conv_b, v_ssd_dt_bias, v_ssd_a_log, v_ssd_d, v_ssd_norm_w, v_dn_conv_w, v_dn_a_log, v_dn_dt_bias, v_dn_norm_w, v_sg_ln_g, v_sg_ln_b, v_sg_w, v_sg_b, v_fox_f_bias, v_gate_b, v_w_branch, v_w_out, v_ln1_g, v_ln1_b, v_w_up, v_w_down, v_ln2_g, v_ln2_b):
    args = dict(locals())
    p = {n: args[n] for n in WEIGHTS}
    xt, target = x[0], loss_target[0]
    ws, acts = [_layer_weights(p, l) for l in range(DEPTH)], []
    _use_gathered(ws[0], LATE, comm_call(_gather_rider(p, 0, LATE), "weights_all_gather_0"), 0)
    h = ln_fwd(xt, ln_in_g[None], ln_in_b[None], "ln_in_fwd")
    for l in range(DEPTH):
        a, gathered = _layer_fwd(h, ws[l], l, dn_rider=_gather_rider(p, 0, EARLY) if l == 0 else None,
                                 fox_rider=_gather_rider(p, l + 1, LATE + EARLY) if l + 1 < DEPTH else None)
        if l + 1 < DEPTH:
            _use_gathered(ws[l + 1], LATE + EARLY, gathered, l + 1)
        acts.append(a)
        h = a["h2"]
    dh, loss = loss_head(h, target, "loss_head")
    loss = lax.psum(loss[0, 0], ("x", "y", "c"))

    layer_grads, got, late = [None] * DEPTH, {}, None
    for l in reversed(range(DEPTH)):
        dseg, du1, layer_grads[l], got_l = _layer_bwd(dh, acts[l], ws[l], l, p, late)
        got.update(got_l)
        late = _exchange_items(layer_grads[l], p, LATE + WHOLE)
        rider = None
        if l == 0:
            pack = _pack_small([jnp.stack([layer_grads[k][n] for k in range(DEPTH)]) for n in REPLICATED_LAYER], 8)
            rider = exchange_rider(late + [(pack[0], 0, -1, pack.shape[1:])])
        (dh,), carried = dh_all(dseg, ws[l]["w_in"], du1, f"dh_{l}", rider=rider)
    got[(LATE + WHOLE, 0)], got_layer_pack = carried[:-1], carried[-1]
    grad_x, dg_in, db_in = ln_bwd(xt, ln_in_g[None], ln_in_b[None], dh, "ln_in_bwd")
    pack = _pack_small([dg_in[0], db_in[0]], 8)
    got_entry_pack = comm_call(exchange_rider([(pack[0], 0, -1, pack.shape[1:])]), "grads_exchange_entry_norm")[0]
    rcv = {(n, l): arr for (names, l), arrs in got.items() for n, arr in zip(names, arrs)}

    res = [{}, {}, {}, {}]
    for n in tuple(SHARDED) + WHOLE:
        shp = p[n].shape
        lead = math.prod(shp[1:-2])
        to3 = lambda t: t.reshape((-1,) + shp[-2:])
        outs = reduce_adamw([rcv[(n, l)].reshape((N_DEV, lead) + shp[-2:]) for l in range(DEPTH)],
                            to3(p[n]), to3(args["m_" + n]), to3(args["v_" + n]), f"adamw_{n}")
        for k in range(4):
            res[k][n] = outs[k].reshape(shp)
    for names, got_pack, rows, name in ((REPLICATED_LAYER, got_layer_pack, 8, "adamw_replicated"), (REPLICATED_ENTRY, got_entry_pack, 8, "adamw_entry_norm")):
        outs = reduce_adamw([got_pack[:, None]], _pack_small([p[n] for n in names], rows), _pack_small([args["m_" + n] for n in names], rows),
                            _pack_small([args["v_" + n] for n in names], rows), name)
        off = 0
        for n in names:
            shp = p[n].shape
            cnt = math.prod(shp)
            for k in range(4):
                res[k][n] = outs[k].reshape(-1)[off:off + cnt].reshape(shp)
            off += cnt
    return (loss, grad_x[None], *[res[0][n] for n in WEIGHTS], *[res[1][n] for n in WEIGHTS],
            *[res[2][n] for n in WEIGHTS], *[res[3][n] for n in WEIGHTS])
```

```python
import collections
import functools
import math

import jax
import jax.numpy as jnp
from jax import lax
from jax.experimental import pallas as pl
from jax.experimental.pallas import tpu as pltpu

F32 = jnp.float32
BF = jnp.bfloat16

D_MODEL = 1024
DEPTH = 2
BRANCH_W = 512
D_FF = 4096
LN_EPS = 1e-5
NORM_EPS = 1e-6
ALPHA = (2 * DEPTH) ** 0.25
N_DEV = 8
LANES = 128
ADAM_LR, ADAM_B1, ADAM_B2, ADAM_EPS, ADAM_WD, ADAM_STEP = 0.001, 0.9, 0.999, 1e-08, 0.01, 10

DT0, BETA0, A0, FF0 = 0, 8, 12, 16
IN_SIZES = (512, 1024, 8, 1536, 4, 4, 512, 1024, 1536, 8, 4096)
_OFF = [0]
for _s in IN_SIZES:
    _OFF.append(_OFF[-1] + _s)
D_IN = _OFF[-1]
SEGS = (("z", 0), ("xbc", 1), ("dnqkv", 3), ("dngate", 6), ("sguv", 7), ("foxqkv", 8), ("gates", 10))
SMALL_SRC = ((2, DT0), (4, BETA0), (5, A0), (9, FF0))

NN = ((1,), (0,))
NT = ((1,), (1,))
TN = ((0,), (0,))
_DIMS = {"nn": NN, "nt": NT, "tn": TN}


def _pcall(body, **kw):
    return pl.pallas_call(body, **kw)


def _S(shape, dtype=F32):
    return jax.ShapeDtypeStruct(tuple(shape), dtype)


def _iota(shape, dim):
    return lax.broadcasted_iota(jnp.int32, shape, dim)


def _dotb(a, b, dims):
    return lax.dot_general(a, b, (dims, ((), ())), preferred_element_type=F32)


def _split2(a):
    ah = a.astype(BF)
    return ah, (a - ah.astype(F32)).astype(BF)


def _split3(a):
    a1 = a.astype(BF)
    r = a - a1.astype(F32)
    a2 = r.astype(BF)
    a3 = (r - a2.astype(F32)).astype(BF)
    return a1, a2, a3


def _mm_raw(a, b, form, mode):
    d = _DIMS[form]
    if mode == "1":
        return _dotb(a.astype(BF), b.astype(BF), d)
    if mode == "3":
        ah, al = _split2(a)
        bh, bl = _split2(b)
        return _dotb(ah, bh, d) + (_dotb(ah, bl, d) + _dotb(al, bh, d))
    if mode == "xa":
        ab = a.astype(BF)
        b1, b2, b3 = _split3(b)
        return _dotb(ab, b1, d) + (_dotb(ab, b2, d) + _dotb(ab, b3, d))
    bb = b.astype(BF)
    a1, a2, a3 = _split3(a)
    return _dotb(a1, bb, d) + (_dotb(a2, bb, d) + _dotb(a3, bb, d))


@functools.partial(jax.custom_vjp, nondiff_argnums=(2, 3))
def mm(a, b, form, mode):
    return _mm_raw(a, b, form, mode)


def _mm_fwd(a, b, form, mode):
    return _mm_raw(a, b, form, mode), (a, b)


_XA_DB = {"nn": "xa", "nt": "xb", "tn": "xa"}
_XB_DA = {"nn": "xb", "nt": "xb", "tn": "xa"}


def _mm_bwd(form, mode, res, g):
    a, b = res
    ma = _XB_DA[form] if mode == "xb" else mode
    mb = _XA_DB[form] if mode == "xa" else mode
    da = db = None
    if mode != "xa":
        da = {"nn": lambda: mm(g, b, "nt", ma), "nt": lambda: mm(g, b, "nn", ma), "tn": lambda: mm(b, g, "nt", ma)}[form]()
    if mode != "xb":
        db = {"nn": lambda: mm(a, g, "tn", mb), "nt": lambda: mm(g, a, "tn", mb), "tn": lambda: mm(a, g, "nn", mb)}[form]()
    if da is None:
        da = jnp.zeros_like(a)
    if db is None:
        db = jnp.zeros_like(b)
    return da, db


mm.defvjp(_mm_fwd, _mm_bwd)


def _silu(x):
    return x * jax.nn.sigmoid(x)


def _ln(x, g, b):
    mu = jnp.mean(x, -1, keepdims=True)
    xc = x - mu
    var = jnp.mean(xc * xc, -1, keepdims=True)
    return xc * lax.rsqrt(var + LN_EPS) * g + b


def _pick(n, cap):
    if n <= cap:
        return n
    best = LANES
    for t in range(LANES, cap + 1, LANES):
        if n % t == 0:
            best = t
    return best


def transpose_bf16(a, name):
    T, C = a.shape
    tt = min(T, 512)

    def body(a_ref, o_ref):
        o_ref[...] = a_ref[...].T.astype(BF)

    return _pcall(body, grid=(T // tt,), in_specs=[pl.BlockSpec((tt, C), lambda t: (t, 0))], out_specs=pl.BlockSpec((C, tt), lambda t: (0, t)),
                  out_shape=_S((C, T), BF), name=name)(a)


def matmul_dw(a_t, b, name):
    M, K = a_t.shape
    N = b.shape[1]
    tm, tn, tk = min(M, 1024), _pick(N, 1024), _pick(K, 1024)
    nk = K // tk

    def body(a_ref, b_ref, o_ref, acc):
        k = pl.program_id(2)
        p = _dotb(a_ref[...], b_ref[...].astype(BF), NN)

        @pl.when(k == 0)
        def _():
            acc[...] = p

        @pl.when(k > 0)
        def _():
            acc[...] += p

        @pl.when(k == nk - 1)
        def _():
            o_ref[...] = acc[...].astype(BF)

    return _pcall(body, grid=(N // tn, M // tm, nk),
                  in_specs=[pl.BlockSpec((tm, tk), lambda j, i, k: (i, k)), pl.BlockSpec((tk, tn), lambda j, i, k: (k, j))],
                  out_specs=pl.BlockSpec((tm, tn), lambda j, i, k: (i, j)), out_shape=_S((M, N), BF),
                  scratch_shapes=[pltpu.VMEM((tm, tn), F32)], name=name)(a_t, b)


def matmul_tn(a, b, name, b_col0=0, n_cols=None, out_dtype=F32):
    T, M = a.shape
    N = b.shape[1] if n_cols is None else n_cols
    tm, tn, tt = _pick(M, 512), _pick(N, 1024), min(T, 512)
    nt = T // tt
    jb = b_col0 // tn

    def body(a_ref, b_ref, o_ref, acc):
        t = pl.program_id(2)
        p = _dotb(a_ref[...].astype(BF), b_ref[...].astype(BF), TN)

        @pl.when(t == 0)
        def _():
            acc[...] = p

        @pl.when(t > 0)
        def _():
            acc[...] += p

        @pl.when(t == nt - 1)
        def _():
            o_ref[...] = acc[...].astype(out_dtype)

    return _pcall(body, grid=(M // tm, N // tn, nt),
                  in_specs=[pl.BlockSpec((tt, tm), lambda i, j, t: (t, i)), pl.BlockSpec((tt, tn), lambda i, j, t: (t, jb + j))],
                  out_specs=pl.BlockSpec((tm, tn), lambda i, j, t: (i, j)), out_shape=_S((M, N), out_dtype),
                  scratch_shapes=[pltpu.VMEM((tm, tn), F32)], name=name)(a, b)


def _pieces(v):
    if v.ndim == 3:
        return [v[i] for i in range(v.shape[0])]
    n = v.shape[1] // LANES
    if n <= 1:
        return [v]
    return [v[:, i * LANES:(i + 1) * LANES] for i in range(n)]


def _join(ps, like_ndim):
    if like_ndim == 3:
        return jnp.stack(ps, axis=0)
    return ps[0] if len(ps) == 1 else jnp.concatenate(ps, axis=1)


def scan_fwd(name, f, xs, ps, ys, state_shape, nc, nh=1, rider=None):
    nx, npar, ny = len(xs), len(ps), len(ys)

    def body(*refs):
        x_refs, p_refs = refs[:nx], refs[nx:nx + npar]
        y_refs = refs[nx + npar:nx + npar + ny]
        st_out, st = refs[nx + npar + ny], refs[nx + npar + ny + 1]
        c, h = pl.program_id(0), pl.program_id(1)

        @pl.when(c == 0)
        def _():
            st[h] = jnp.zeros(state_shape, F32)

        S = st[h]
        st_out[...] = S
        yv, Sn = f([_pieces(r[...]) for r in x_refs], [_pieces(r[...]) for r in p_refs], _pieces(S), h)
        for r, v in zip(y_refs, yv):
            r[...] = _join(v, 2)
        st[h] = _join(Sn, 3)

    in_specs = [pl.BlockSpec(bs, im) for (_, bs, im) in xs]
    in_specs += [pl.BlockSpec(p.shape, (lambda c, h, n=p.ndim: (0,) * n)) for p in ps]
    out_specs = [pl.BlockSpec(bs, im) for (_, bs, im) in ys]
    out_specs.append(pl.BlockSpec((None, None) + tuple(state_shape), lambda c, h: (c, h, 0, 0, 0)))
    out_shape = [_S(s) for (s, _, _) in ys] + [_S((nc, nh) + tuple(state_shape))]
    return hosted_call(body, rider, grid=(nc, nh), in_specs=in_specs, out_specs=out_specs, out_shape=out_shape,
                       scratch_shapes=[pltpu.VMEM((nh,) + tuple(state_shape), F32)], name=name, args=[*[x[0] for x in xs], *ps])


def scan_bwd(name, f, xs, ps, ys, dys, states, state_shape, nc, nh=1, shared=(), rider=None):
    nx, npar, ny = len(xs), len(ps), len(ys)

    def body(*refs):
        x_refs, p_refs = refs[:nx], refs[nx:nx + npar]
        s_ref = refs[nx + npar]
        dy_refs = refs[nx + npar + 1:nx + npar + 1 + ny]
        o = nx + npar + 1 + ny
        dx_refs, dp_refs, dst = refs[o:o + nx], refs[o + nx:o + nx + npar], refs[o + nx + npar]
        c, h = pl.program_id(0), pl.program_id(1)

        @pl.when(c == 0)
        def _():
            dst[h] = jnp.zeros(state_shape, F32)

        @pl.when((c == 0) & (h == 0))
        def _():
            for r in dp_refs:
                r[...] = jnp.zeros(r.shape, F32)

        xv = [_pieces(r[...]) for r in x_refs]
        pv = [_pieces(r[...]) for r in p_refs]
        _, vjp = jax.vjp(lambda a, b, s: f(a, b, s, h), xv, pv, _pieces(s_ref[...]))
        dxv, dpv, dS = vjp(([_pieces(r[...]) for r in dy_refs], _pieces(dst[h])))
        for i, (r, v) in enumerate(zip(dx_refs, dxv)):
            if i in shared and nh > 1:
                @pl.when(h == 0)
                def _(r=r, v=v):
                    r[...] = _join(v, 2)

                @pl.when(h > 0)
                def _(r=r, v=v):
                    r[...] += _join(v, 2)
            else:
                r[...] = _join(v, 2)
        for r, v in zip(dp_refs, dpv):
            r[...] += _join(v, len(r.shape))
        dst[h] = _join(dS, 3)

    def rev(im):
        return lambda c, h: im(nc - 1 - c, h)

    in_specs = [pl.BlockSpec(bs, rev(im)) for (_, bs, im) in xs]
    in_specs += [pl.BlockSpec(p.shape, (lambda c, h, n=p.ndim: (0,) * n)) for p in ps]
    in_specs.append(pl.BlockSpec((None, None) + tuple(state_shape), lambda c, h: (nc - 1 - c, h, 0, 0, 0)))
    in_specs += [pl.BlockSpec(bs, rev(im)) for (_, bs, im) in ys]
    out_specs = [pl.BlockSpec(bs, rev(im)) for (_, bs, im) in xs]
    out_specs += [pl.BlockSpec(p.shape, (lambda c, h, n=p.ndim: (0,) * n)) for p in ps]
    out_shape = [_S(x[0].shape) for x in xs] + [_S(p.shape) for p in ps]
    return hosted_call(body, rider, grid=(nc, nh), in_specs=in_specs, out_specs=out_specs, out_shape=out_shape,
                       scratch_shapes=[pltpu.VMEM((nh,) + tuple(state_shape), F32)], name=name,
                       args=[*[x[0] for x in xs], *ps, states, *dys])


def _lane():
    return _iota((1, LANES), 1)


def _col(v, idx):
    return jnp.sum(v * (_lane() == idx).astype(F32), axis=1, keepdims=True)


def _last_row(v):
    r = v.shape[0]
    return jnp.sum(v * (_iota((r, 1), 0) == r - 1).astype(F32), axis=0, keepdims=True)


def _tril(n, strict=False):
    r, c = _iota((n, n), 0), _iota((n, n), 1)
    return (r > c) if strict else (r >= c)


def ssd_chunk(xs, ps, S, h):
    zp, xbc, (sm,) = xs
    (bias,), (alog,), (dsk,), nw = ps
    Q = sm.shape[0]
    H = range(8)
    lane = _lane()
    a128 = jnp.where(lane < 8, -jnp.exp(alog), 0.0)
    dtl = jax.nn.softplus(sm + bias)
    tri = _tril(Q)
    cum = mm(tri.astype(F32), dtl * a128, "nn", "xa")
    sel8 = (_iota((8, LANES), 0) == _iota((8, LANES), 1)).astype(F32)
    cum_t = mm(sel8, cum, "nt", "xa")
    m0 = (lane < 64).astype(F32)
    rows0 = (_iota((LANES, 1), 0) < 64).astype(F32)
    me = [m0 if hh % 2 == 0 else 1.0 - m0 for hh in H]
    re = [rows0 if hh % 2 == 0 else 1.0 - rows0 for hh in H]
    Bm, Cm = [xbc[4 + hh // 4] for hh in H], [xbc[6 + hh // 4] for hh in H]
    cb = [mm(xbc[6 + g], xbc[4 + g], "nt", "1") for g in range(2)]
    col = [_col(cum, hh) for hh in H]
    row = [jnp.sum(cum_t * (_iota((8, 1), 0) == hh).astype(F32), axis=0, keepdims=True) for hh in H]
    xh = [xbc[hh // 2] * me[hh] for hh in H]
    xdt = [xh[hh] * _col(dtl, hh) for hh in H]
    seg = [jnp.exp(jnp.where(tri, col[hh] - row[hh], -jnp.inf)) for hh in H]
    last = [_last_row(col[hh]) for hh in H]
    y_diag = [mm(cb[hh // 4] * seg[hh], xdt[hh], "nn", "1") for hh in H]
    y_off = [mm(Cm[hh] * jnp.exp(col[hh]), S[hh // 2], "nt", "1") * me[hh] for hh in H]
    st = [mm(xdt[hh], Bm[hh] * jnp.exp(last[hh] - col[hh]), "tn", "1") for hh in H]
    y = [y_diag[hh] + y_off[hh] + _col(dsk, hh) * xh[hh] for hh in H]
    Sn = [S[pr] * (jnp.exp(last[2 * pr]) * re[0] + jnp.exp(last[2 * pr + 1]) * re[1]) + st[2 * pr] + st[2 * pr + 1] for pr in range(4)]
    yz = [(y[2 * pr] + y[2 * pr + 1]) * _silu(zp[pr]) for pr in range(4)]
    ssq = sum(jnp.sum(v * v, axis=1, keepdims=True) for v in yz)
    scale = lax.rsqrt(ssq / BRANCH_W + NORM_EPS)
    return [[yz[i] * scale * nw[i] for i in range(4)]], Sn


@jax.custom_vjp
def _halves(x):
    r = x.shape[0] // 2
    return x[:r], x[r:]


_halves.defvjp(lambda x: (_halves(x), None), lambda _, g: (jnp.concatenate(g, axis=0),))
DN_CHUNK = 64


def dn_chunk(xs, ps, S, h):
    act, gate, (sm,) = xs
    (alog,), (dtb,), (nw,) = ps
    C = DN_CHUNK
    I = range(8)
    lane = _lane()
    tri, strict = _tril(C), _tril(C, True)
    sm2 = _halves(sm)
    G = [jnp.where((lane >= A0) & (lane < A0 + 4), -jnp.exp(alog) * jax.nn.softplus(s + dtb), 0.0) for s in sm2]
    gcs = [mm(tri.astype(F32), g, "nn", "xa") for g in G]
    sig = [jax.nn.sigmoid(s) for s in sm2]
    parts = [_halves(x) for x in act]
    q, k, v = ([parts[o + i % 4][i // 4] for i in I] for o in (0, 4, 8))
    gt = [_halves(x) for x in gate]
    qn = [q[i] * lax.rsqrt(jnp.sum(q[i] * q[i], axis=1, keepdims=True) + NORM_EPS) * (LANES ** -0.5) for i in I]
    kn = [k[i] * lax.rsqrt(jnp.sum(k[i] * k[i], axis=1, keepdims=True) + NORM_EPS) for i in I]
    beta = [_col(sig[i // 4], BETA0 + i % 4) for i in I]
    gcol = [_col(gcs[i // 4], A0 + i % 4) for i in I]
    selr = [((_iota((8, LANES), 0) == 0) & (_iota((8, LANES), 1) == A0 + h)).astype(F32) for h in range(4)]
    grow = [jnp.sum(mm(selr[i % 4], gcs[i // 4], "nt", "xa"), axis=0, keepdims=True) for i in I]
    gamma = [jnp.exp(jnp.where(tri, gcol[i] - grow[i], -jnp.inf)) for i in I]
    kb = [kn[i] * beta[i] for i in I]
    pk = [-(mm(kb[i], kn[i], "nt", "1") * jnp.where(strict, gamma[i], 0.0)) for i in I]
    eye = (_iota((C, C), 0) == _iota((C, C), 1)).astype(F32)
    minv = [eye + pk[i] for i in I]
    for _ in range(5):
        pk = [mm(pk[i], pk[i], "nn", "3") for i in I]
        minv = [minv[i] + mm(minv[i], pk[i], "nn", "3") for i in I]
    eg = [jnp.exp(gcol[i]) for i in I]
    w = [mm(minv[i], kb[i] * eg[i], "nn", "3") for i in I]
    u = [mm(minv[i], v[i] * beta[i], "nn", "3") for i in I]
    glast = [_last_row(gcol[i]) for i in I]
    qg = [qn[i] * eg[i] for i in I]
    qk = [mm(qn[i], kn[i], "nt", "1") * gamma[i] for i in I]
    kdec = [kn[i] * jnp.exp(glast[i] - gcol[i]) for i in I]
    y = []
    for c in range(2):
        J = range(4 * c, 4 * c + 4)
        vnew = [u[i] - mm(w[i], S[i % 4], "nn", "1") for i in J]
        o = [mm(qg[i], S[i % 4], "nn", "1") + mm(qk[i], vn, "nn", "1") for i, vn in zip(J, vnew)]
        S = [S[i % 4] * jnp.exp(glast[i]) + mm(kdec[i], vn, "tn", "1") for i, vn in zip(J, vnew)]
        on = [x * lax.rsqrt(jnp.mean(x * x, axis=1, keepdims=True) + NORM_EPS) * nw for x in o]
        y.append([on[h] * _silu(gt[h][c]) for h in range(4)])
    return [[jnp.concatenate([y[0][h], y[1][h]], axis=0) for h in range(4)]], S


def sg_chunk(xs, ps, S, h):
    (uv,) = xs
    lng, lnb, W, (bt,) = ps
    u = [jax.nn.gelu(p) for p in uv[:4]]
    v = [jax.nn.gelu(p) for p in uv[4:]]
    mu = sum(jnp.sum(p, axis=1, keepdims=True) for p in v) / BRANCH_W
    vc = [p - mu for p in v]
    var = sum(jnp.sum(p * p, axis=1, keepdims=True) for p in vc) / BRANCH_W
    inv = lax.rsqrt(var + LN_EPS)
    trif = _tril(W[0].shape[0]).astype(F32)
    out = []
    for g in range(4):
        vn = vc[g] * inv * lng[g] + lnb[g]
        out.append(u[g] * (mm(W[g] * trif, vn, "nn", "1") + _col(bt, g)))
    return [out], S


def foxc_chunk(xs, ps, S, h):
    (sm,), ((fb,),), (carry,) = xs[0], ps, S
    lane = _lane()
    ls = jnp.where((lane >= FF0) & (lane < FF0 + 8), jax.nn.log_sigmoid(sm + fb), 0.0)
    c = mm(_tril(sm.shape[0]).astype(F32), ls, "nn", "xa") + carry
    return [[c]], [_last_row(c)]


HALO = 8


def _conv_tiles(T, C):
    return min(T, 512), _pick(C, 512)


def conv_fwd(x, w, b, name):
    T, C = x.shape
    tm, cb = _conv_tiles(T, C)

    def body(xp_ref, x_ref, w_ref, b_ref, o_ref):
        i = pl.program_id(1)
        e = jnp.concatenate([xp_ref[...] * (i > 0).astype(F32), x_ref[...]], axis=0)
        pre = b_ref[...] + sum(w_ref[k:k + 1, :] * e[5 + k:5 + k + tm, :] for k in range(4))
        o_ref[...] = _silu(pre)

    hb = tm // HALO
    return _pcall(body, grid=(C // cb, T // tm),
                  in_specs=[pl.BlockSpec((HALO, cb), lambda j, i: (jnp.maximum(i * hb - 1, 0), j)), pl.BlockSpec((tm, cb), lambda j, i: (i, j)),
                            pl.BlockSpec((4, cb), lambda j, i: (0, j)), pl.BlockSpec((1, cb), lambda j, i: (0, j))],
                  out_specs=pl.BlockSpec((tm, cb), lambda j, i: (i, j)), out_shape=_S((T, C)), name=name)(x, x, w, b)


def conv_bwd(x, w, b, dact, name):
    T, C = x.shape
    tm, cb = _conv_tiles(T, C)
    nt = T // tm

    def body(xp_ref, x_ref, xn_ref, w_ref, b_ref, d_ref, dn_ref, dx_ref, dw_ref, db_ref):
        i = pl.program_id(1)
        has_prev, has_next = (i > 0).astype(F32), (i < nt - 1).astype(F32)
        e = jnp.concatenate([xp_ref[...] * has_prev, x_ref[...], xn_ref[...] * has_next], axis=0)
        pre = b_ref[...] + sum(w_ref[k:k + 1, :] * e[5 + k:5 + k + tm + 8, :] for k in range(4))
        de = jnp.concatenate([d_ref[...], dn_ref[...] * has_next], axis=0)
        sg = jax.nn.sigmoid(pre)
        dpre = de * (sg * (1.0 + pre * (1.0 - sg)))
        dx_ref[...] = sum(w_ref[k:k + 1, :] * dpre[3 - k:3 - k + tm, :] for k in range(4))
        dcur = dpre[0:tm, :]
        dw = jnp.concatenate([jnp.sum(dcur * e[5 + k:5 + k + tm, :], axis=0, keepdims=True) for k in range(4)], axis=0)
        db = jnp.sum(dcur, axis=0, keepdims=True)

        @pl.when(i == 0)
        def _():
            dw_ref[...] = dw
            db_ref[...] = db

        @pl.when(i > 0)
        def _():
            dw_ref[...] += dw
            db_ref[...] += db

    blk = lambda f: pl.BlockSpec((tm, cb), f)
    hb = tm // HALO
    before = pl.BlockSpec((HALO, cb), lambda j, i: (jnp.maximum(i * hb - 1, 0), j))
    after = pl.BlockSpec((HALO, cb), lambda j, i: (jnp.minimum((i + 1) * hb, nt * hb - 1), j))
    return _pcall(body, grid=(C // cb, nt),
                  in_specs=[before, blk(lambda j, i: (i, j)), after,
                            pl.BlockSpec((4, cb), lambda j, i: (0, j)), pl.BlockSpec((1, cb), lambda j, i: (0, j)),
                            blk(lambda j, i: (i, j)), after],
                  out_specs=[blk(lambda j, i: (i, j)), pl.BlockSpec((4, cb), lambda j, i: (0, j)), pl.BlockSpec((1, cb), lambda j, i: (0, j))],
                  out_shape=[_S((T, C)), _S((4, C)), _S((1, C))], name=name)(x, x, x, w, b, dact, dact)


FOX_SCALE = 64 ** -0.5
LOG2E = 1.4426950408889634


def _spare(e, i):
    return (_lane() == 64 * (1 - e) + i).astype(F32)


def _lanes_of(e):
    lane = _lane()
    return ((lane < 64) if e == 0 else (lane >= 64)).astype(F32)


def _col3(col, e, first):
    c1 = col.astype(BF).astype(F32)
    c2 = (col - c1).astype(BF).astype(F32)
    c3 = (col - c1 - c2).astype(BF).astype(F32)
    return c1 * _spare(e, first) + c2 * _spare(e, first + 1) + c3 * _spare(e, first + 2)


def _ones3(e, first):
    return _spare(e, first) + _spare(e, first + 1) + _spare(e, first + 2)


def _causal_bias(n):
    return jnp.where(_iota((n, n), 0) >= _iota((n, n), 1), 0.0, -jnp.inf).astype(F32)


def _c_col(cc, hh):
    return jnp.sum(cc * (_lane() == FF0 + hh).astype(F32), axis=1, keepdims=True) * LOG2E


def _pair_spec(tq, row_of):
    return pl.BlockSpec((None, 2, tq, LANES), lambda hp, a, b: (hp, 0, row_of(a, b), 0))


def fox_prep(qkv, ccol, name):
    T = qkv.shape[0]
    tq = min(T, 512)

    def body(q_ref, k_ref, v_ref, cc_ref, qa_ref, ka_ref, va_ref):
        hp = pl.program_id(0)
        q, k, v, cc = q_ref[...], k_ref[...], v_ref[...], cc_ref[...]
        for e in range(2):
            me = _lanes_of(e)
            c2 = _c_col(cc, 2 * hp + e)
            qa_ref[e] = (q * me * (FOX_SCALE * LOG2E) + _col3(c2, e, 0) + _ones3(e, 3)).astype(BF)
            ka_ref[e] = (k * me + _ones3(e, 0) + _col3(-c2, e, 3) + _ones3(e, 6)).astype(BF)
            va_ref[e] = (v * me + (1.0 - me)).astype(BF)

    blk = lambda off: pl.BlockSpec((tq, LANES), lambda hp, i: (i, off + hp))
    out = pl.BlockSpec((None, 2, tq, LANES), lambda hp, i: (hp, 0, i, 0))
    return _pcall(body, grid=(4, T // tq), in_specs=[blk(0), blk(4), blk(8), pl.BlockSpec((tq, LANES), lambda hp, i: (i, 0))],
                  out_specs=[out] * 3, out_shape=[_S((4, 2, T, LANES), BF)] * 3, name=name)(qkv, qkv, qkv, ccol)


def fox_fwd(qa, ka, va, name, rider=None):
    T = qa.shape[2]
    tq = min(T, 512)
    nq = T // tq

    def body(qa_ref, ka_ref, va_ref, o_ref, lse_ref, m_s, acc, causal):
        i, j = pl.program_id(1), pl.program_id(2)

        @pl.when((pl.program_id(0) == 0) & (i == 0) & (j == 0))
        def _():
            causal[...] = _causal_bias(tq)

        @pl.when(j == 0)
        def _():
            m_s[...] = jnp.full(m_s.shape, -jnp.inf, F32)
            acc[...] = jnp.zeros(acc.shape, F32)

        def step(diagonal):
            for e in range(2):
                s = _dotb(qa_ref[e], ka_ref[e], NT)
                if diagonal:
                    s = s + causal[...]
                m_old = m_s[e]
                m_new = jnp.maximum(m_old, jnp.max(s, axis=1, keepdims=True))
                p = jnp.exp2(s - m_new)
                m_s[e] = m_new
                acc[e] = acc[e] * jnp.exp2(m_old - m_new) + _dotb(p.astype(BF), va_ref[e], NN)

        @pl.when(j < i)
        def _():
            step(False)

        @pl.when(j == i)
        def _():
            step(True)
            lane = _lane()
            o, lse = 0.0, 0.0
            for e in range(2):
                l = jnp.sum(acc[e] * _spare(e, 0), axis=1, keepdims=True)
                o = o + acc[e] * _lanes_of(e) / l
                lse = lse + (m_s[e] + jnp.log2(l)) * (lane == e).astype(F32)
            o_ref[...] = o
            lse_ref[...] = lse

    kv = _pair_spec(tq, lambda i, j: jnp.minimum(j, i))
    return hosted_call(body, rider, grid=(4, nq, nq), in_specs=[_pair_spec(tq, lambda i, j: i), kv, kv],
                       out_specs=[pl.BlockSpec((tq, LANES), lambda hp, i, j: (i, hp)), pl.BlockSpec((None, tq, LANES), lambda hp, i, j: (hp, i, 0))],
                       out_shape=[_S((T, BRANCH_W)), _S((4, T, LANES))],
                       scratch_shapes=[pltpu.VMEM((2, tq, 1), F32), pltpu.VMEM((2, tq, LANES), F32), pltpu.VMEM((tq, tq), F32)],
                       name=name, args=[qa, ka, va])


def fox_prep_bwd(qkv, qa, o, lse, do, name):
    T = qkv.shape[0]
    tq = min(T, 512)

    def body(q_ref, k_ref, qa_ref, o_ref, lse_ref, do_ref, qb_ref, doa_ref, qd_ref, kd_ref):
        q, k, dov = q_ref[...], k_ref[...], do_ref[...]
        dd = dov * o_ref[...]
        lane = _lane()
        for e in range(2):
            me = _lanes_of(e)
            lse_e = jnp.sum(lse_ref[...] * (lane == e).astype(F32), axis=1, keepdims=True)
            qb_ref[e] = (qa_ref[e].astype(F32) + _col3(-lse_e, e, 6)).astype(BF)
            doa_ref[e] = (dov * me + _col3(-jnp.sum(dd * me, axis=1, keepdims=True), e, 0)).astype(BF)
            qd_ref[e] = (q * me * FOX_SCALE + _spare(e, 0)).astype(BF)
            kd_ref[e] = (k * me * FOX_SCALE + _spare(e, 0)).astype(BF)

    blk = lambda off: pl.BlockSpec((tq, LANES), lambda hp, i: (i, off + hp))
    pair = pl.BlockSpec((None, 2, tq, LANES), lambda hp, i: (hp, 0, i, 0))
    return _pcall(body, grid=(4, T // tq),
                  in_specs=[blk(0), blk(4), pair, blk(0), pl.BlockSpec((None, tq, LANES), lambda hp, i: (hp, i, 0)), blk(0)],
                  out_specs=[pair] * 4, out_shape=[_S((4, 2, T, LANES), BF)] * 4, name=name)(qkv, qkv, qa, o, lse, do)


def fox_bwd(qb, ka, va, doa, qd, kd, name, rider=None):
    T = qb.shape[2]
    tq = min(T, 512)
    nq = T // tq

    def body(qb_ref, ka_ref, va_ref, doa_ref, qd_ref, kd_ref, dq_ref, dk_ref, dv_ref, dcc_ref, dq_s, dk_s, dv_s, causal):
        hp, j, ii = pl.program_id(0), pl.program_id(1), pl.program_id(2)
        i = jnp.maximum(ii, j)

        @pl.when((hp == 0) & (j == 0) & (ii == 0))
        def _():
            dcc_ref[...] = jnp.zeros(dcc_ref.shape, F32)
            causal[...] = _causal_bias(tq)

        @pl.when((j == 0) & (ii == 0))
        def _():
            dq_s[...] = jnp.zeros(dq_s.shape, F32)

        @pl.when(ii == 0)
        def _():
            dk_s[...] = jnp.zeros(dk_s.shape, F32)
            dv_s[...] = jnp.zeros(dv_s.shape, F32)

        def step(diagonal):
            rows = pl.ds(pl.multiple_of(i * tq, tq), tq)
            for e in range(2):
                s = _dotb(qb_ref[e], ka_ref[e], NT)
                if diagonal:
                    s = s + causal[...]
                p = jnp.exp2(s)
                ds = (p * _dotb(doa_ref[e], va_ref[e], NT)).astype(BF)
                dv_s[e] += _dotb(p.astype(BF), doa_ref[e], TN)
                dq_s[e, rows, :] += _dotb(ds, kd_ref[e], NN)
                dk_s[e] += _dotb(ds, qd_ref[e], TN)

        @pl.when(ii > j)
        def _():
            step(False)

        @pl.when(ii == j)
        def _():
            step(True)

        def fold(acc, sign):
            grad, dc = 0.0, 0.0
            for e in range(2):
                a = acc[e]
                grad = grad + a * _lanes_of(e)
                dc = dc + sign * jnp.sum(a * _spare(e, 0), axis=1, keepdims=True) * (_lane() == FF0 + 2 * hp + e).astype(F32)
            return grad, dc

        @pl.when(ii == nq - 1)
        def _():
            grad, dc = fold(dk_s, -1.0)
            dk_ref[...] = grad
            dv_ref[...] = dv_s[0] * _lanes_of(0) + dv_s[1] * _lanes_of(1)
            dcc_ref[pl.ds(pl.multiple_of(j * tq, tq), tq), :] += dc

        @pl.when((j == nq - 1) & (ii == nq - 1))
        def _():
            grad, dc = fold(dq_s, 1.0)
            dq_ref[...] = grad
            dcc_ref[...] += dc

    irow, jrow = _pair_spec(tq, lambda j, ii: jnp.maximum(ii, j)), _pair_spec(tq, lambda j, ii: j)
    jout = pl.BlockSpec((tq, LANES), lambda hp, j, ii: (j, hp))
    return hosted_call(body, rider, grid=(4, nq, nq), in_specs=[irow, jrow, jrow, irow, irow, jrow],
                       out_specs=[pl.BlockSpec((T, LANES), lambda hp, j, ii: (0, hp)), jout, jout, pl.BlockSpec((T, LANES), lambda hp, j, ii: (0, 0))],
                       out_shape=[_S((T, BRANCH_W)), _S((T, BRANCH_W)), _S((T, BRANCH_W)), _S((T, LANES))],
                       scratch_shapes=[pltpu.VMEM((2, T, LANES), F32), pltpu.VMEM((2, tq, LANES), F32), pltpu.VMEM((2, tq, LANES), F32),
                                       pltpu.VMEM((tq, tq), F32)],
                       name=name, args=[qb, ka, va, doa, qd, kd])


def _acc_out(ref, val, first):
    @pl.when(first)
    def _():
        ref[...] = val

    @pl.when(jnp.logical_not(first))
    def _():
        ref[...] += val


def _row(tm, c):
    return pl.BlockSpec((tm, c), lambda i: (i, 0))


def _full(shape):
    return pl.BlockSpec(shape, lambda *_: (0,) * len(shape))


def ln_fwd(x, g, b, name, rider=None):
    T, C = x.shape
    tm = min(T, 512)

    def body(x_ref, g_ref, b_ref, o_ref):
        o_ref[...] = _ln(x_ref[...], g_ref[...], b_ref[...])

    return hosted_call(body, rider, grid=(T // tm,), in_specs=[_row(tm, C), _full((1, C)), _full((1, C))], out_specs=[_row(tm, C)],
                       out_shape=[_S((T, C))], scratch_shapes=[], name=name, args=[x, g, b])


def ln_bwd(x, g, b, dy, name):
    T, C = x.shape
    tm = min(T, 512)

    def body(x_ref, g_ref, b_ref, dy_ref, dx_ref, dg_ref, db_ref):
        _, vjp = jax.vjp(_ln, x_ref[...], g_ref[...], b_ref[...])
        dx, dg, db = vjp(dy_ref[...])
        dx_ref[...] = dx
        first = pl.program_id(0) == 0
        _acc_out(dg_ref, dg, first)
        _acc_out(db_ref, db, first)

    return _pcall(body, grid=(T // tm,), in_specs=[_row(tm, C), _full((1, C)), _full((1, C)), _row(tm, C)],
                  out_specs=[_row(tm, C), _full((1, C)), _full((1, C))], out_shape=[_S((T, C)), _S((1, C)), _S((1, C))], name=name)(x, g, b, dy)


def loss_head(h, target, name):
    T, C = h.shape
    tm = min(T, 512)

    def body(h_ref, t_ref, d_ref, l_ref):
        e = h_ref[...] - t_ref[...]
        d_ref[...] = e * (1.0 / C)
        part = jnp.sum(jnp.sum(e * e, axis=1, keepdims=True), axis=0, keepdims=True) * (0.5 / C)
        _acc_out(l_ref, part, pl.program_id(0) == 0)

    return _pcall(body, grid=(T // tm,), in_specs=[_row(tm, C), _row(tm, C)], out_specs=[_row(tm, C), _full((1, 1))],
                  out_shape=[_S((T, C)), _S((1, 1))], name=name)(h, target)


def add3(a, b, c, name):
    T, C = a.shape
    tm = min(T, 512)

    def body(a_ref, b_ref, c_ref, o_ref):
        o_ref[...] = a_ref[...] + b_ref[...] + c_ref[...]

    return _pcall(body, grid=(T // tm,), in_specs=[_row(tm, C)] * 3, out_specs=_row(tm, C), out_shape=_S((T, C)), name=name)(a, b, c)


def _wb_spec(l):
    return pl.BlockSpec((None, 4, BRANCH_W, D_MODEL), lambda *_: (l, 0, 0, 0))


def merge_fwd(ys, gl, gb, wb, l, name):
    T = gl.shape[0]
    tm = min(T, 256)

    def body(y0, y1, y2, y3, gl_ref, gb_ref, wb_ref, o_ref):
        acc = 0.0
        for i, y in enumerate((y0, y1, y2, y3)):
            z = _dotb(y[...].astype(BF), wb_ref[i], NN)
            g = jax.nn.sigmoid(gl_ref[:, i * D_MODEL:(i + 1) * D_MODEL] + gb_ref[i:i + 1, :])
            acc = acc + g * z
        o_ref[...] = acc

    return _pcall(body, grid=(T // tm,), in_specs=[_row(tm, BRANCH_W)] * 4 + [_row(tm, 4 * D_MODEL), _full((4, D_MODEL)), _wb_spec(l)],
                  out_specs=_row(tm, D_MODEL), out_shape=_S((T, D_MODEL)), name=name)(*ys, gl, gb, wb)


def merge_bwd(ys, gl, gb, wb, l, dm, name):
    T = gl.shape[0]
    tm = min(T, 256)

    def body(y0, y1, y2, y3, gl_ref, gb_ref, wb_ref, dm_ref, d0, d1, d2, d3, dgl_ref, dz_ref, dgb_ref):
        dmv = dm_ref[...]
        first = pl.program_id(0) == 0
        for i, (y, d) in enumerate(zip((y0, y1, y2, y3), (d0, d1, d2, d3))):
            cols = slice(i * D_MODEL, (i + 1) * D_MODEL)
            z = _dotb(y[...].astype(BF), wb_ref[i], NN)
            g = jax.nn.sigmoid(gl_ref[:, cols] + gb_ref[i:i + 1, :])
            dgl = dmv * z * (g * (1.0 - g))
            dz = (g * dmv).astype(BF)
            dgl_ref[:, cols] = dgl
            dz_ref[:, cols] = dz
            d[...] = _dotb(dz, wb_ref[i], NT)
            _acc_out(dgb_ref.at[i:i + 1, :], jnp.sum(dgl, axis=0, keepdims=True), first)

    return _pcall(body, grid=(T // tm,),
                  in_specs=[_row(tm, BRANCH_W)] * 4 + [_row(tm, 4 * D_MODEL), _full((4, D_MODEL)), _wb_spec(l), _row(tm, D_MODEL)],
                  out_specs=[_row(tm, BRANCH_W)] * 4 + [_row(tm, 4 * D_MODEL), _row(tm, 4 * D_MODEL), _full((4, D_MODEL))],
                  out_shape=[_S((T, BRANCH_W))] * 4 + [_S((T, 4 * D_MODEL)), _S((T, 4 * D_MODEL), BF), _S((4, D_MODEL))], name=name)(
                      *ys, gl, gb, wb, dm)


def _wout_spec(l):
    return pl.BlockSpec((None, D_MODEL, D_MODEL), lambda *_: (l, 0, 0))


def out_fwd(merged, h, wout, l, g, b, name):
    T = h.shape[0]
    tm = min(T, 512)

    def body(m_ref, h_ref, w_ref, g_ref, b_ref, u_ref, o_ref):
        u = ALPHA * h_ref[...] + _dotb(m_ref[...].astype(BF), w_ref[...], NN)
        u_ref[...] = u
        o_ref[...] = _ln(u, g_ref[...], b_ref[...])

    C = D_MODEL
    return _pcall(body, grid=(T // tm,), in_specs=[_row(tm, C), _row(tm, C), _wout_spec(l), _full((1, C)), _full((1, C))],
                  out_specs=[_row(tm, C), _row(tm, C)], out_shape=[_S((T, C)), _S((T, C))], name=name)(merged, h, wout, g, b)


def out_bwd(u, dy, g, b, wout, l, name):
    T, C = u.shape
    tm = min(T, 512)

    def body(u_ref, dy_ref, g_ref, b_ref, w_ref, du_ref, dm_ref, dg_ref, db_ref):
        _, vjp = jax.vjp(_ln, u_ref[...], g_ref[...], b_ref[...])
        du, dg, db = vjp(dy_ref[...])
        du_ref[...] = du
        dm_ref[...] = _dotb(du.astype(BF), w_ref[...], NT)
        first = pl.program_id(0) == 0
        _acc_out(dg_ref, dg, first)
        _acc_out(db_ref, db, first)

    return _pcall(body, grid=(T // tm,), in_specs=[_row(tm, C), _row(tm, C), _full((1, C)), _full((1, C)), _wout_spec(l)],
                  out_specs=[_row(tm, C), _row(tm, C), _full((1, C)), _full((1, C))],
                  out_shape=[_S((T, C)), _S((T, C)), _S((1, C)), _S((1, C))], name=name)(u, dy, g, b, wout)


def ff_fwd(h, wup, wdown, l, g, b, name):
    T, C = h.shape
    F = wup.shape[2]
    tm, tf = min(T, 512), 1024
    nf = F // tf

    def body(h_ref, wu_ref, wd_ref, g_ref, b_ref, u_ref, o_ref, acc):
        f = pl.program_id(1)
        a = _dotb(h_ref[...].astype(BF), wu_ref[...], NN)
        r = jnp.square(jnp.maximum(a, 0.0))
        p = _dotb(r.astype(BF), wd_ref[...], NN)
        _acc_out(acc, p, f == 0)

        @pl.when(f == nf - 1)
        def _():
            u = ALPHA * h_ref[...] + acc[...]
            u_ref[...] = u
            o_ref[...] = _ln(u, g_ref[...], b_ref[...])

    row = pl.BlockSpec((tm, C), lambda i, f: (i, 0))
    return _pcall(body, grid=(T // tm, nf),
                  in_specs=[row, pl.BlockSpec((None, C, tf), lambda i, f: (l, 0, f)), pl.BlockSpec((None, tf, C), lambda i, f: (l, f, 0)),
                            _full((1, C)), _full((1, C))],
                  out_specs=[row, row], out_shape=[_S((T, C)), _S((T, C))], scratch_shapes=[pltpu.VMEM((tm, C), F32)], name=name)(h, wup, wdown, g, b)


def ff_bwd(u, dy, h, g, b, wup, wdown, l, name):
    T, C = h.shape
    F = wup.shape[2]
    tm, tf = min(T, 512), 1024
    nf = F // tf

    def body(u_ref, dy_ref, h_ref, g_ref, b_ref, wu_ref, wd_ref, du_ref, dh_ref, da_ref, r_ref, dg_ref, db_ref, du_s, acc):
        i, f = pl.program_id(0), pl.program_id(1)

        @pl.when(f == 0)
        def _():
            _, vjp = jax.vjp(_ln, u_ref[...], g_ref[...], b_ref[...])
            du, dg, db = vjp(dy_ref[...])
            du_s[...] = du
            du_ref[...] = du
            _acc_out(dg_ref, dg, i == 0)
            _acc_out(db_ref, db, i == 0)

        a = _dotb(h_ref[...].astype(BF), wu_ref[...], NN)
        ap = jnp.maximum(a, 0.0)
        dr = _dotb(du_s[...].astype(BF), wd_ref[...], NT)
        da = (dr * (2.0 * ap)).astype(BF)
        da_ref[...] = da
        r_ref[...] = jnp.square(ap).T.astype(BF)
        _acc_out(acc, _dotb(da, wu_ref[...], NT), f == 0)

        @pl.when(f == nf - 1)
        def _():
            dh_ref[...] = ALPHA * du_s[...] + acc[...]

    row = pl.BlockSpec((tm, C), lambda i, f: (i, 0))
    colf = pl.BlockSpec((tm, tf), lambda i, f: (i, f))
    return _pcall(body, grid=(T // tm, nf),
                  in_specs=[row, row, row, _full((1, C)), _full((1, C)), pl.BlockSpec((None, C, tf), lambda i, f: (l, 0, f)),
                            pl.BlockSpec((None, tf, C), lambda i, f: (l, f, 0))],
                  out_specs=[row, row, colf, pl.BlockSpec((tf, tm), lambda i, f: (f, i)), _full((1, C)), _full((1, C))],
                  out_shape=[_S((T, C)), _S((T, C)), _S((T, F), BF), _S((F, T), BF), _S((1, C)), _S((1, C))],
                  scratch_shapes=[pltpu.VMEM((tm, C), F32), pltpu.VMEM((tm, C), F32)], name=name)(u, dy, h, g, b, wup, wdown)


MESH_ID = pl.DeviceIdType.MESH
_ANY = pl.BlockSpec(memory_space=pl.ANY)


def _window(ref, ax, idx, n):
    if n < 0:
        return ref
    sel = idx if n == 0 else pl.ds(pl.multiple_of(idx * n, n), n)
    return ref.at[(slice(None),) * ax + (sel,)]


Rider = collections.namedtuple("Rider", "operands out_shape scratch start wait")


def hosted_call(body, rider, *, grid, in_specs, out_specs, out_shape, scratch_shapes, name, args):
    n_in, n_out, n_scr = len(in_specs), len(out_specs), len(scratch_shapes)
    if rider is None:
        return _pcall(body, grid=grid, in_specs=in_specs, out_specs=out_specs, out_shape=out_shape, scratch_shapes=scratch_shapes, name=name)(*args), []
    ri, ro = len(rider.operands), len(rider.out_shape)

    def wrapped(*refs):
        ins, r_in = refs[:n_in], refs[n_in:n_in + ri]
        o0 = n_in + ri
        outs, r_out = refs[o0:o0 + n_out], refs[o0 + n_out:o0 + n_out + ro]
        s0 = o0 + n_out + ro
        scr, r_scr = refs[s0:s0 + n_scr], refs[s0 + n_scr:]
        ids = [pl.program_id(i) for i in range(len(grid))]
        first = functools.reduce(jnp.logical_and, [i == 0 for i in ids])
        last = functools.reduce(jnp.logical_and, [i == g - 1 for i, g in zip(ids, grid)])

        @pl.when(first)
        def _():
            rider.start(r_in, r_out, r_scr)

        body(*ins, *outs, *scr)

        @pl.when(last)
        def _():
            rider.wait(r_in, r_out, r_scr)

    res = _pcall(wrapped, grid=grid, in_specs=list(in_specs) + [_ANY] * ri, out_specs=list(out_specs) + [_ANY] * ro,
                 out_shape=list(out_shape) + list(rider.out_shape), scratch_shapes=list(scratch_shapes) + list(rider.scratch),
                 name=name)(*args, *rider.operands)
    return res[:n_out], res[n_out:]


def comm_call(rider, name):
    ri = len(rider.operands)

    def body(*refs):
        r_in, r_out, r_scr = refs[:ri], refs[ri:ri + len(rider.out_shape)], refs[ri + len(rider.out_shape):]
        rider.start(r_in, r_out, r_scr)
        rider.wait(r_in, r_out, r_scr)

    return _pcall(body, in_specs=[_ANY] * ri, out_specs=[_ANY] * len(rider.out_shape), out_shape=list(rider.out_shape),
                  scratch_shapes=list(rider.scratch), name=name)(*rider.operands)


def gather_rider(shards, axes):
    K = len(shards)
    widths = [s.shape[a] for s, a in zip(shards, axes)]
    out_shape = [_S(s.shape[:a] + (N_DEV * s.shape[a],) + s.shape[a + 1:], s.dtype) for s, a in zip(shards, axes)]

    def plan(x_refs, o_refs, sems):
        send_sems, recv_sems, local_sems = sems
        mx, my, mc = lax.axis_index("x"), lax.axis_index("y"), lax.axis_index("c")
        me, sibling = (mx, my, mc), (mx, my, 1 - mc)
        chips = [(1 - mx, my), (mx, 1 - my), (1 - mx, 1 - my)]

        def win(k, px, py, pc):
            return _window(o_refs[k], axes[k], 4 * px + 2 * py + pc, widths[k])

        def copy(k, slot, block, to, src=None):
            return pltpu.make_async_remote_copy(src_ref=win(k, *block) if src is None else src, dst_ref=win(k, *block),
                                                send_sem=send_sems.at[7 * k + slot], recv_sem=recv_sems.at[7 * k + slot],
                                                device_id=to, device_id_type=MESH_ID)

        mine = [pltpu.make_async_copy(x_refs[k], win(k, *me), local_sems.at[k]) for k in range(K)]
        first = []
        for k in range(K):
            first.append(copy(k, 0, me, sibling, src=x_refs[k]))
            first += [copy(k, 1 + j, me, (*chip, mc), src=x_refs[k]) for j, chip in enumerate(chips)]
        return me, sibling, chips, copy, mine, first

    def start(x_refs, o_refs, sems):
        _, _, _, _, mine, first = plan(x_refs, o_refs, sems)
        for cp in mine + first:
            cp.start()

    def wait(x_refs, o_refs, sems):
        me, sibling, chips, copy, mine, first = plan(x_refs, o_refs, sems)
        mc = me[2]
        passed = []
        for j, chip in enumerate(chips):
            for k in range(K):
                copy(k, 1 + j, (*chip, mc), me).wait_recv()
                passed.append(copy(k, 4 + j, (*chip, mc), sibling))
                passed[-1].start()
        for k in range(K):
            copy(k, 0, sibling, me).wait_recv()
        for j, chip in enumerate(chips):
            for k in range(K):
                copy(k, 4 + j, (*chip, 1 - mc), me).wait_recv()
        for cp in first + passed:
            cp.wait_send()
        for cp in mine:
            cp.wait()

    scratch = [pltpu.SemaphoreType.DMA((7 * K,)), pltpu.SemaphoreType.DMA((7 * K,)), pltpu.SemaphoreType.DMA((K,))]
    return Rider(list(shards), out_shape, scratch, start, wait)


def exchange_rider(items):
    ns = len(items)
    out_shape = [_S((N_DEV,) + tuple(it[3]), it[0].dtype) for it in items]

    def plan(src_refs, o_refs, sems):
        send_sems, recv_sems, local_sems = sems
        mx, my, mc = lax.axis_index("x"), lax.axis_index("y"), lax.axis_index("c")
        me = 4 * mx + 2 * my + mc
        remote, own = [], []
        for s, (_, ax, n, _) in enumerate(items):
            own.append(pltpu.make_async_copy(_window(src_refs[s], ax, me, n), o_refs[s].at[me], local_sems.at[s]))
            for k in range(1, N_DEV):
                px = 1 - mx if k & 4 else mx
                py = 1 - my if k & 2 else my
                pc = 1 - mc if k & 1 else mc
                remote.append(pltpu.make_async_remote_copy(
                    src_ref=_window(src_refs[s], ax, 4 * px + 2 * py + pc, n), dst_ref=o_refs[s].at[me],
                    send_sem=send_sems.at[7 * s + k - 1], recv_sem=recv_sems.at[7 * s + k - 1],
                    device_id=(px, py, pc), device_id_type=MESH_ID))
        return remote, own

    def start(src_refs, o_refs, sems):
        remote, own = plan(src_refs, o_refs, sems)
        for cp in own + remote:
            cp.start()

    def wait(src_refs, o_refs, sems):
        remote, own = plan(src_refs, o_refs, sems)
        for cp in remote + own:
            cp.wait()

    scratch = [pltpu.SemaphoreType.DMA((7 * ns,)), pltpu.SemaphoreType.DMA((7 * ns,)), pltpu.SemaphoreType.DMA((ns,))]
    return Rider([it[0] for it in items], out_shape, scratch, start, wait)


def reduce_adamw(rcvs, w, m, v, name):
    L = len(rcvs)
    _, A, B, C = rcvs[0].shape
    tb = B
    while tb > 8 and tb * C > (1 << 17):
        tb //= 2

    def body(*refs):
        r_refs, (w_ref, m_ref, v_ref, g_ref, d_ref, mo_ref, vo_ref) = refs[:L], refs[L:]
        for k in range(L):
            @pl.when(pl.program_id(0) == k)
            def _(k=k):
                g = r_refs[k][0].astype(F32)
                for d in range(1, N_DEV):
                    g = g + r_refs[k][d].astype(F32)
                mn = ADAM_B1 * m_ref[...] + (1.0 - ADAM_B1) * g
                vn = ADAM_B2 * v_ref[...] + (1.0 - ADAM_B2) * jnp.square(g)
                m_hat = mn / (1.0 - ADAM_B1 ** ADAM_STEP)
                v_hat = vn / (1.0 - ADAM_B2 ** ADAM_STEP)
                g_ref[...] = g
                d_ref[...] = -ADAM_LR * (m_hat / (jnp.sqrt(v_hat) + ADAM_EPS) + ADAM_WD * w_ref[...])
                mo_ref[...] = mn
                vo_ref[...] = vn

    def rspec(k):
        return pl.BlockSpec((N_DEV, None, tb, C), lambda l, a, i: (0, jnp.where(l == k, a, 0), jnp.where(l == k, i, 0), 0))

    blk = pl.BlockSpec((None, tb, C), lambda l, a, i: (l * A + a, i, 0))
    return _pcall(body, grid=(L, A, B // tb), in_specs=[rspec(k) for k in range(L)] + [blk, blk, blk],
                  out_specs=[blk] * 4, out_shape=[_S((L * A, B, C))] * 4, name=name)(*rcvs, w, m, v)


def _w_in_pieces(g0, g1):
    per = D_IN // N_DEV
    return [(d, max(g0, d * per) - d * per, min(g1, (d + 1) * per) - d * per) for d in range(N_DEV) if max(g0, d * per) < min(g1, (d + 1) * per)]


def repack_w_in(w8, name):
    _, L, R, per = w8.shape
    tr = 256

    def cols(x_ref, g0, g1):
        return [x_ref[d, :, a:b] for d, a, b in _w_in_pieces(g0, g1)]

    def body(x_ref, *o_refs):
        for (name_, i), o_ref in zip(SEGS, o_refs):
            o_ref[...] = jnp.concatenate(cols(x_ref, _OFF[i], _OFF[i + 1]), axis=1)
        parts, at = [], 0
        for i, lane0 in SMALL_SRC:
            assert lane0 == at
            parts += cols(x_ref, _OFF[i], _OFF[i + 1])
            at += IN_SIZES[i]
        parts.append(jnp.zeros((tr, LANES - at), w8.dtype))
        o_refs[-1][...] = jnp.concatenate(parts, axis=1)

    widths = [IN_SIZES[i] for _, i in SEGS] + [LANES]
    outs = _pcall(body, grid=(L, R // tr), in_specs=[pl.BlockSpec((N_DEV, None, tr, per), lambda l, r: (0, l, r, 0))],
                  out_specs=[pl.BlockSpec((None, tr, w), lambda l, r: (l, r, 0)) for w in widths],
                  out_shape=[_S((L, R, w), w8.dtype) for w in widths], name=name)(w8)
    return dict(zip(SEG_NAMES, outs))


def repack_dw_in(dseg, name):
    R = dseg["z"].shape[0]
    per = D_IN // N_DEV
    tr = 128
    src = {i: (k, 0) for k, (_, i) in enumerate(SEGS)}
    src.update({i: (len(SEGS), lane0) for i, lane0 in SMALL_SRC})

    def body(*refs):
        s_refs, o_ref = refs[:-1], refs[-1]
        for d in range(N_DEV):
            parts = []
            for i in range(len(IN_SIZES)):
                g0, g1 = max(_OFF[i], d * per), min(_OFF[i + 1], (d + 1) * per)
                if g0 < g1:
                    k, c0 = src[i]
                    parts.append(s_refs[k][:, c0 + g0 - _OFF[i]:c0 + g1 - _OFF[i]])
            o_ref[d] = jnp.concatenate(parts, axis=1)

    arrs = [dseg[n] for n in SEG_NAMES]
    return _pcall(body, grid=(R // tr,), in_specs=[pl.BlockSpec((tr, a.shape[1]), lambda r: (r, 0)) for a in arrs],
                  out_specs=pl.BlockSpec((N_DEV, tr, per), lambda r: (0, r, 0)), out_shape=_S((N_DEV, R, per), arrs[0].dtype), name=name)(*arrs)


WEIGHTS = ("ln_in_g", "ln_in_b", "w_in", "ssd_conv_w", "ssd_conv_b", "ssd_dt_bias", "ssd_a_log", "ssd_d", "ssd_norm_w", "dn_conv_w",
           "dn_a_log", "dn_dt_bias", "dn_norm_w", "sg_ln_g", "sg_ln_b", "sg_w", "sg_b", "fox_f_bias", "gate_b", "w_branch", "w_out",
           "ln1_g", "ln1_b", "w_up", "w_down", "ln2_g", "ln2_b")
SHARDED = {"w_in": 2, "ssd_conv_w": 2, "dn_conv_w": 2, "gate_b": 2, "w_branch": 3, "w_out": 1, "w_up": 2, "w_down": 1}
SLABBED = ("w_in", "dn_conv_w")
MATMUL_WEIGHTS = ("w_in", "w_branch", "w_out", "w_up", "w_down")
REPLICATED_ENTRY = ("ln_in_g", "ln_in_b")
REPLICATED_LAYER = tuple(n for n in WEIGHTS if n not in SHARDED and n not in REPLICATED_ENTRY and n != "sg_w")
SEG_NAMES = tuple(n for n, _ in SEGS) + ("small",)
PACK_COLS = 1024


def _lanes(vec, off):
    return jnp.pad(vec, (off, LANES - off - vec.shape[0]))[None]


def _pack_small(parts, row_mult):
    flat = jnp.concatenate([q.reshape(-1) for q in parts])
    rows = -(-flat.shape[0] // (PACK_COLS * row_mult)) * row_mult
    return jnp.pad(flat, (0, rows * PACK_COLS - flat.shape[0])).reshape(1, rows, PACK_COLS)


EARLY = ("w_branch", "w_out", "w_up", "w_down", "gate_b")
LATE = ("w_in", "ssd_conv_w", "dn_conv_w")
WHOLE = ("sg_w",)


def _gather_rider(p, l, names):
    shards, axes = [], []
    for n in names:
        s = p[n][l:l + 1]
        s = s.astype(BF) if n in MATMUL_WEIGHTS else s
        shards.append(s[None] if n in SLABBED else s)
        axes.append(0 if n in SLABBED else SHARDED[n])
    return gather_rider(shards, axes)


def _exchange_items(g, p, names):
    items = []
    for n in names:
        local = p[n].shape[1:]
        if n in WHOLE:
            items.append((g[n], 0, -1, local))
        elif n in SLABBED:
            items.append((g[n], 0, 0, local))
        else:
            items.append((g[n], SHARDED[n] - 1, local[SHARDED[n] - 1], local))
    return items


def _use_gathered(w, names, arrays, l):
    for n, arr in zip(names, arrays):
        if n == "w_in":
            w[n] = repack_w_in(arr, f"w_in_repack_{l}")
        elif n == "ssd_conv_w":
            w["ssd_cw"] = arr[0]
        elif n == "dn_conv_w":
            w["dn_cw"] = jnp.moveaxis(arr[:, 0], 0, 1).reshape(4, 3 * BRANCH_W)
        elif n == "gate_b":
            w[n] = arr[0]
        else:
            w[n] = arr


def _layer_weights(p, l):
    w = {}
    w["ssd_cb"] = p["ssd_conv_b"][l][None]
    w["dn_cb"] = jnp.zeros((1, 3 * BRANCH_W), F32)
    w["ssd_ps"] = [_lanes(p["ssd_dt_bias"][l], DT0), _lanes(p["ssd_a_log"][l], DT0), _lanes(p["ssd_d"][l], DT0), p["ssd_norm_w"][l][None]]
    w["dn_ps"] = [_lanes(p["dn_a_log"][l], A0), _lanes(p["dn_dt_bias"][l], A0), p["dn_norm_w"][l][None]]
    w["sg_ps"] = [p["sg_ln_g"][l][None], p["sg_ln_b"][l][None], p["sg_w"][l], jnp.pad(p["sg_b"][l].T, ((0, 0), (0, LANES - 4)))]
    w["fox_ps"] = [_lanes(p["fox_f_bias"][l], FF0)]
    for n in ("ln1_g", "ln1_b", "ln2_g", "ln2_b"):
        w[n] = p[n][l][None]
    return w


def _scan_specs(T, a):
    c0 = lambda c, h: (c, 0)
    ssd = dict(f=ssd_chunk, xs=[(a["z"], (128, 512), c0), (a["xbc_act"], (128, 1024), c0), (a["small"], (128, LANES), c0)],
               ys=[((T, BRANCH_W), (128, BRANCH_W), c0)], state=(4, LANES, LANES), nc=T // 128, nh=1, shared=())
    dn = dict(f=dn_chunk, xs=[(a["dn_act"], (128, 3 * BRANCH_W), c0), (a["dngate"], (128, BRANCH_W), c0), (a["small"], (128, LANES), c0)],
              ys=[((T, BRANCH_W), (128, BRANCH_W), c0)], state=(4, LANES, LANES), nc=T // 128, nh=1, shared=())
    sg = dict(f=sg_chunk, xs=[(a["sguv"], (128, 1024), c0)], ys=[((T, BRANCH_W), (128, BRANCH_W), c0)], state=(1, 8, LANES), nc=T // 128, nh=1, shared=())
    fc = dict(f=foxc_chunk, xs=[(a["small"], (128, LANES), c0)],
              ys=[((T, LANES), (128, LANES), c0)], state=(1, 1, LANES), nc=T // 128, nh=1, shared=())
    return ssd, dn, sg, fc


def _layer_fwd(h, w, l, dn_rider=None, fox_rider=None):
    T = h.shape[0]
    a = {"h": h, **proj_all(h, w["w_in"], f"proj_{l}")}
    a["xbc_act"] = conv_fwd(a["xbc"], w["ssd_cw"], w["ssd_cb"], f"ssd_conv_{l}")
    a["dn_act"] = conv_fwd(a["dnqkv"], w["dn_cw"], w["dn_cb"], f"dn_conv_{l}")
    ssd, dn, sg, fc = _scan_specs(T, a)
    (a["ya"], a["ssd_st"]), _ = scan_fwd(f"ssd_fwd_{l}", ssd["f"], ssd["xs"], w["ssd_ps"], ssd["ys"], ssd["state"], ssd["nc"], ssd["nh"])
    (a["yb"], a["dn_st"]), got = scan_fwd(f"dn_fwd_{l}", dn["f"], dn["xs"], w["dn_ps"], dn["ys"], dn["state"], dn["nc"], dn["nh"], rider=dn_rider)
    _use_gathered(w, EARLY, got, l)
    (a["yc"], a["sg_st"]), _ = scan_fwd(f"sg_fwd_{l}", sg["f"], sg["xs"], w["sg_ps"], sg["ys"], sg["state"], sg["nc"], sg["nh"])
    (a["ccol"], a["fc_st"]), _ = scan_fwd(f"foxc_fwd_{l}", fc["f"], fc["xs"], w["fox_ps"], fc["ys"], fc["state"], fc["nc"], fc["nh"])
    a["fox_qa"], a["fox_ka"], a["fox_va"] = fox_prep(a["foxqkv"], a["ccol"], f"fox_prep_{l}")
    (a["yd"], a["lse"]), carried = fox_fwd(a["fox_qa"], a["fox_ka"], a["fox_va"], f"fox_fwd_{l}", rider=fox_rider)
    a["merged"] = merge_fwd([a["ya"], a["yb"], a["yc"], a["yd"]], a["gates"], w["gate_b"], w["w_branch"], 0, f"merge_fwd_{l}")
    a["u1"], a["h1"] = out_fwd(a["merged"], h, w["w_out"], 0, w["ln1_g"], w["ln1_b"], f"out_fwd_{l}")
    a["u2"], a["h2"] = ff_fwd(a["h1"], w["w_up"], w["w_down"], 0, w["ln2_g"], w["ln2_b"], f"ff_fwd_{l}")
    return a, carried


def _layer_bwd(dh2, a, w, l, p, late_above):
    T = dh2.shape[0]
    g = {}
    du2, dh1, da, r, dg2, db2 = ff_bwd(a["u2"], dh2, a["h1"], w["ln2_g"], w["ln2_b"], w["w_up"], w["w_down"], 0, f"ff_bwd_{l}")
    g["ln2_g"], g["ln2_b"] = dg2[0], db2[0]
    g["w_up"] = matmul_dw(transpose_bf16(a["h1"], f"h1_t_{l}"), da, f"dwup_{l}")
    g["w_down"] = matmul_dw(r, du2, f"dwdown_{l}")
    du1, dmerged, dg1, db1 = out_bwd(a["u1"], dh1, w["ln1_g"], w["ln1_b"], w["w_out"], 0, f"out_bwd_{l}")
    g["ln1_g"], g["ln1_b"] = dg1[0], db1[0]
    g["w_out"] = matmul_dw(transpose_bf16(a["merged"], f"merged_t_{l}"), du1, f"dwout_{l}")
    ys = [a["ya"], a["yb"], a["yc"], a["yd"]]
    dya, dyb, dyc, dyd, dgl, dz, dgb = merge_bwd(ys, a["gates"], w["gate_b"], w["w_branch"], 0, dmerged, f"merge_bwd_{l}")
    g["gate_b"] = dgb
    g["w_branch"] = jnp.stack([matmul_tn(ys[i], dz, f"dwb{i}_{l}", b_col0=i * D_MODEL, n_cols=D_MODEL, out_dtype=BF) for i in range(4)])
    early = exchange_rider(_exchange_items(g, p, EARLY))
    dn_rider = early if late_above is None else exchange_rider(late_above)
    fox_rider = None if late_above is None else early
    ssd, dn, sg, fc = _scan_specs(T, a)
    (dz_ssd, dxbc_act, dsm_ssd, d_dtb, d_alog, d_dsk, d_nw), _ = scan_bwd(f"ssd_bwd_{l}", ssd["f"], ssd["xs"], w["ssd_ps"], ssd["ys"], [dya], a["ssd_st"],
                                                                           ssd["state"], ssd["nc"], ssd["nh"])
    g["ssd_dt_bias"], g["ssd_a_log"], g["ssd_d"], g["ssd_norm_w"] = d_dtb[0, DT0:DT0 + 8], d_alog[0, DT0:DT0 + 8], d_dsk[0, DT0:DT0 + 8], d_nw[0]
    dxbc, g["ssd_conv_w"], dcb = conv_bwd(a["xbc"], w["ssd_cw"], w["ssd_cb"], dxbc_act, f"ssd_conv_bwd_{l}")
    g["ssd_conv_b"] = dcb[0]
    (ddn_act, ddngate, dsm_dn, d_alog, d_dtb, d_nw), got_dn = scan_bwd(f"dn_bwd_{l}", dn["f"], dn["xs"], w["dn_ps"], dn["ys"], [dyb], a["dn_st"],
                                                                        dn["state"], dn["nc"], dn["nh"], rider=dn_rider)
    g["dn_a_log"], g["dn_dt_bias"], g["dn_norm_w"] = d_alog[0, A0:A0 + 4], d_dtb[0, A0:A0 + 4], d_nw[0]
    ddnqkv, g["dn_conv_w"], _ = conv_bwd(a["dnqkv"], w["dn_cw"], w["dn_cb"], ddn_act, f"dn_conv_bwd_{l}")
    (dsguv, d_lng, d_lnb, d_w, d_bt), _ = scan_bwd(f"sg_bwd_{l}", sg["f"], sg["xs"], w["sg_ps"], sg["ys"], [dyc], a["sg_st"], sg["state"], sg["nc"], sg["nh"])
    g["sg_ln_g"], g["sg_ln_b"], g["sg_w"], g["sg_b"] = d_lng[0], d_lnb[0], d_w, d_bt[:, :4].T
    qb, doa, qd, kd = fox_prep_bwd(a["foxqkv"], a["fox_qa"], a["yd"], a["lse"], dyd, f"fox_prep_bwd_{l}")
    (dfq, dfk, dfv, dccol), got_fox = fox_bwd(qb, a["fox_ka"], a["fox_va"], doa, qd, kd, f"fox_bwd_{l}", rider=fox_rider)
    (dsm_fox, d_fb), _ = scan_bwd(f"foxc_bwd_{l}", fc["f"], fc["xs"], w["fox_ps"], fc["ys"], [dccol], a["fc_st"], fc["state"], fc["nc"], fc["nh"])
    g["fox_f_bias"] = d_fb[0, FF0:FF0 + 8]
    dseg = {"z": dz_ssd, "xbc": dxbc, "dnqkv": ddnqkv, "dngate": ddngate, "sguv": dsguv,
            "foxqkv": jnp.concatenate([dfq, dfk, dfv], axis=1), "gates": dgl, "small": add3(dsm_ssd, dsm_dn, dsm_fox, f"dsmall_{l}")}
    h_t = transpose_bf16(a["h"], f"h_t_{l}")
    dwin = {n: matmul_dw(h_t, dseg[n], f"dwin_{n}_{l}") for n in SEG_NAMES}
    g["w_in"] = repack_dw_in(dwin, f"dw_in_repack_{l}")
    g["dn_conv_w"] = jnp.moveaxis(g["dn_conv_w"].reshape(4, N_DEV, 3 * BRANCH_W // N_DEV), 1, 0)
    got = {(EARLY, l): got_dn} if late_above is None else {(LATE + WHOLE, l + 1): got_dn, (EARLY, l): got_fox}
    return dseg, du1, g, got


def proj_all(h, w_in, name):
    T = h.shape[0]
    tm = min(T, 256)
    ns = len(SEG_NAMES)

    def body(*refs):
        h_ref, w_refs, o_refs = refs[0], refs[1:1 + ns], refs[1 + ns:]
        hb = h_ref[...].astype(BF)
        for w_ref, o_ref in zip(w_refs, o_refs):
            o_ref[...] = _dotb(hb, w_ref[...], NN)

    widths = [w_in[n].shape[2] for n in SEG_NAMES]
    in_specs = [_row(tm, D_MODEL)]
    in_specs += [pl.BlockSpec((None,) + w_in[n].shape[1:], lambda i: (0, 0, 0), pipeline_mode=pl.Buffered(1)) for n in SEG_NAMES]
    outs = _pcall(body, grid=(T // tm,), in_specs=in_specs, out_specs=[_row(tm, wd) for wd in widths],
                  out_shape=[_S((T, wd)) for wd in widths], name=name)(h, *[w_in[n] for n in SEG_NAMES])
    return dict(zip(SEG_NAMES, outs))


def dh_all(dseg, w_in, add, name, rider=None):
    T = add.shape[0]
    tm = min(T, 256)
    ns = len(SEG_NAMES)

    def body(*refs):
        d_refs, w_refs, add_ref, o_ref = refs[:ns], refs[ns:2 * ns], refs[2 * ns], refs[2 * ns + 1]
        acc = ALPHA * add_ref[...]
        for d_ref, w_ref in zip(d_refs, w_refs):
            acc = acc + _dotb(d_ref[...].astype(BF), w_ref[...], NT)
        o_ref[...] = acc

    in_specs = [_row(tm, dseg[n].shape[1]) for n in SEG_NAMES]
    in_specs += [pl.BlockSpec((None,) + w_in[n].shape[1:], lambda i: (0, 0, 0), pipeline_mode=pl.Buffered(1)) for n in SEG_NAMES]
    in_specs.append(_row(tm, D_MODEL))
    return hosted_call(body, rider, grid=(T // tm,), in_specs=in_specs, out_specs=[_row(tm, D_MODEL)], out_shape=[_S((T, D_MODEL))],
                       scratch_shapes=[], name=name, args=[*[dseg[n] for n in SEG_NAMES], *[w_in[n] for n in SEG_NAMES], add])


def kernel(x, ln_in_g, ln_in_b, w_in, ssd_conv_w, ssd_conv_b, ssd_dt_bias, ssd_a_log, ssd_d, ssd_norm_w, dn_conv_w, dn_a_log, dn_dt_bias, dn_norm_w, sg_ln_g, sg_ln_b, sg_w, sg_b, fox_f_bias, gate_b, w_branch, w_out, ln1_g, ln1_b, w_up, w_down, ln2_g, ln2_b, loss_target, m_ln_in_g, m_ln_in_b, m_w_in, m_ssd_conv_w, m_ssd_conv_b, m_ssd_dt_bias, m_ssd_a_log, m_ssd_d, m_ssd_norm_w, m_dn_conv_w, m_dn_a_log, m_dn_dt_bias, m_dn_norm_w, m_sg_ln_g, m_sg_ln_b, m_sg_w, m_sg_b, m_fox_f_bias, m_gate_b, m_w_branch, m_w_out, m_ln1_g, m_ln1_b, m_w_up, m_w_down, m_ln2_g, m_ln2_b, v_ln_in_g, v_ln_in_b, v_w_in, v_ssd_conv_w, v_ssd_conv_b, v_ssd_dt_bias, v_ssd_a_log, v_ssd_d, v_ssd_norm_w, v_dn_conv_w, v_dn_a_log, v_dn_dt_bias, v_dn_norm_w, v_sg_ln_g, v_sg_ln_b, v_sg_w, v_sg_b, v_fox_f_bias, v_gate_b, v_w_branch, v_w_out, v_ln1_g, v_ln1_b, v_w_up, v_w_down, v_ln2_g, v_ln2_b):
    args = dict(locals())
    p = {n: args[n] for n in WEIGHTS}
    xt, target = x[0], loss_target[0]
    ws, acts = [_layer_weights(p, l) for l in range(DEPTH)], []
    (h,), gathered = ln_fwd(xt, ln_in_g[None], ln_in_b[None], "ln_in_fwd", rider=_gather_rider(p, 0, LATE))
    _use_gathered(ws[0], LATE, gathered, 0)
    for l in range(DEPTH):
        a, gathered = _layer_fwd(h, ws[l], l, dn_rider=_gather_rider(p, 0, EARLY) if l == 0 else None,
                                 fox_rider=_gather_rider(p, l + 1, LATE + EARLY) if l + 1 < DEPTH else None)
        if l + 1 < DEPTH:
            _use_gathered(ws[l + 1], LATE + EARLY, gathered, l + 1)
        acts.append(a)
        h = a["h2"]
    dh, loss = loss_head(h, target, "loss_head")
    loss = lax.psum(loss[0, 0], ("x", "y", "c"))

    layer_grads, got, late = [None] * DEPTH, {}, None
    for l in reversed(range(DEPTH)):
        dseg, du1, layer_grads[l], got_l = _layer_bwd(dh, acts[l], ws[l], l, p, late)
        got.update(got_l)
        late = _exchange_items(layer_grads[l], p, LATE + WHOLE)
        rider = None
        if l == 0:
            pack = _pack_small([jnp.stack([layer_grads[k][n] for k in range(DEPTH)]) for n in REPLICATED_LAYER], 8)
            rider = exchange_rider(late + [(pack[0], 0, -1, pack.shape[1:])])
        (dh,), carried = dh_all(dseg, ws[l]["w_in"], du1, f"dh_{l}", rider=rider)
    got[(LATE + WHOLE, 0)], got_layer_pack = carried[:-1], carried[-1]
    grad_x, dg_in, db_in = ln_bwd(xt, ln_in_g[None], ln_in_b[None], dh, "ln_in_bwd")
    pack = _pack_small([dg_in[0], db_in[0]], 8)
    got_entry_pack = comm_call(exchange_rider([(pack[0], 0, -1, pack.shape[1:])]), "grads_exchange_entry_norm")[0]
    rcv = {(n, l): arr for (names, l), arrs in got.items() for n, arr in zip(names, arrs)}

    res = [{}, {}, {}, {}]
    for n in tuple(SHARDED) + WHOLE:
        shp = p[n].shape
        lead = math.prod(shp[1:-2])
        to3 = lambda t: t.reshape((-1,) + shp[-2:])
        outs = reduce_adamw([rcv[(n, l)].reshape((N_DEV, lead) + shp[-2:]) for l in range(DEPTH)],
                            to3(p[n]), to3(args["m_" + n]), to3(args["v_" + n]), f"adamw_{n}")
        for k in range(4):
            res[k][n] = outs[k].reshape(shp)
    for names, got_pack, rows, name in ((REPLICATED_LAYER, got_layer_pack, 8, "adamw_replicated"), (REPLICATED_ENTRY, got_entry_pack, 8, "adamw_entry_norm")):
        outs = reduce_adamw([got_pack[:, None]], _pack_small([p[n] for n in names], rows), _pack_small([args["m_" + n] for n in names], rows),
                            _pack_small([args["v_" + n] for n in names], rows), name)
        off = 0
        for n in names:
            shp = p[n].shape
            cnt = math.prod(shp)
            for k in range(4):
                res[k][n] = outs[k].reshape(-1)[off:off + cnt].reshape(shp)
            off += cnt
    return (loss, grad_x[None], *[res[0][n] for n in WEIGHTS], *[res[1][n] for n in WEIGHTS],
            *[res[2][n] for n in WEIGHTS], *[res[3][n] for n in WEIGHTS])
```

```python
import collections
import functools
import math

import jax
import jax.numpy as jnp
from jax import lax
from jax.experimental import pallas as pl
from jax.experimental.pallas import tpu as pltpu

F32 = jnp.float32
BF = jnp.bfloat16

D_MODEL = 1024
DEPTH = 2
BRANCH_W = 512
D_FF = 4096
LN_EPS = 1e-5
NORM_EPS = 1e-6
ALPHA = (2 * DEPTH) ** 0.25
N_DEV = 8
LANES = 128
ADAM_LR, ADAM_B1, ADAM_B2, ADAM_EPS, ADAM_WD, ADAM_STEP = 0.001, 0.9, 0.999, 1e-08, 0.01, 10

DT0, BETA0, A0, FF0 = 0, 8, 12, 16
IN_SIZES = (512, 1024, 8, 1536, 4, 4, 512, 1024, 1536, 8, 4096)
_OFF = [0]
for _s in IN_SIZES:
    _OFF.append(_OFF[-1] + _s)
D_IN = _OFF[-1]
SEGS = (("z", 0), ("xbc", 1), ("dnqkv", 3), ("dngate", 6), ("sguv", 7), ("foxqkv", 8), ("gates", 10))
SMALL_SRC = ((2, DT0), (4, BETA0), (5, A0), (9, FF0))

NN = ((1,), (0,))
NT = ((1,), (1,))
TN = ((0,), (0,))
_DIMS = {"nn": NN, "nt": NT, "tn": TN}


def _pcall(body, **kw):
    return pl.pallas_call(body, **kw)


def _S(shape, dtype=F32):
    return jax.ShapeDtypeStruct(tuple(shape), dtype)


def _iota(shape, dim):
    return lax.broadcasted_iota(jnp.int32, shape, dim)


def _dotb(a, b, dims):
    return lax.dot_general(a, b, (dims, ((), ())), preferred_element_type=F32)


def _split2(a):
    ah = a.astype(BF)
    return ah, (a - ah.astype(F32)).astype(BF)


def _split3(a):
    a1 = a.astype(BF)
    r = a - a1.astype(F32)
    a2 = r.astype(BF)
    a3 = (r - a2.astype(F32)).astype(BF)
    return a1, a2, a3


def _mm_raw(a, b, form, mode):
    d = _DIMS[form]
    if mode == "1":
        return _dotb(a.astype(BF), b.astype(BF), d)
    if mode == "3":
        ah, al = _split2(a)
        bh, bl = _split2(b)
        return _dotb(ah, bh, d) + (_dotb(ah, bl, d) + _dotb(al, bh, d))
    if mode == "xa":
        ab = a.astype(BF)
        b1, b2, b3 = _split3(b)
        return _dotb(ab, b1, d) + (_dotb(ab, b2, d) + _dotb(ab, b3, d))
    bb = b.astype(BF)
    a1, a2, a3 = _split3(a)
    return _dotb(a1, bb, d) + (_dotb(a2, bb, d) + _dotb(a3, bb, d))


@functools.partial(jax.custom_vjp, nondiff_argnums=(2, 3))
def mm(a, b, form, mode):
    return _mm_raw(a, b, form, mode)


def _mm_fwd(a, b, form, mode):
    return _mm_raw(a, b, form, mode), (a, b)


_XA_DB = {"nn": "xa", "nt": "xb", "tn": "xa"}
_XB_DA = {"nn": "xb", "nt": "xb", "tn": "xa"}


def _mm_bwd(form, mode, res, g):
    a, b = res
    ma = _XB_DA[form] if mode == "xb" else mode
    mb = _XA_DB[form] if mode == "xa" else mode
    da = db = None
    if mode != "xa":
        da = {"nn": lambda: mm(g, b, "nt", ma), "nt": lambda: mm(g, b, "nn", ma), "tn": lambda: mm(b, g, "nt", ma)}[form]()
    if mode != "xb":
        db = {"nn": lambda: mm(a, g, "tn", mb), "nt": lambda: mm(g, a, "tn", mb), "tn": lambda: mm(a, g, "nn", mb)}[form]()
    if da is None:
        da = jnp.zeros_like(a)
    if db is None:
        db = jnp.zeros_like(b)
    return da, db


mm.defvjp(_mm_fwd, _mm_bwd)


def _silu(x):
    return x * jax.nn.sigmoid(x)


def _ln(x, g, b):
    mu = jnp.mean(x, -1, keepdims=True)
    xc = x - mu
    var = jnp.mean(xc * xc, -1, keepdims=True)
    return xc * lax.rsqrt(var + LN_EPS) * g + b


def _pick(n, cap):
    if n <= cap:
        return n
    best = LANES
    for t in range(LANES, cap + 1, LANES):
        if n % t == 0:
            best = t
    return best


def transpose_bf16(a, name):
    T, C = a.shape
    tt = min(T, 512)

    def body(a_ref, o_ref):
        o_ref[...] = a_ref[...].T.astype(BF)

    return _pcall(body, grid=(T // tt,), in_specs=[pl.BlockSpec((tt, C), lambda t: (t, 0))], out_specs=pl.BlockSpec((C, tt), lambda t: (0, t)),
                  out_shape=_S((C, T), BF), name=name)(a)


def matmul_dw(a_t, b, name):
    M, K = a_t.shape
    N = b.shape[1]
    tm, tn, tk = min(M, 1024), _pick(N, 1024), _pick(K, 1024)
    nk = K // tk

    def body(a_ref, b_ref, o_ref, acc):
        k = pl.program_id(2)
        p = _dotb(a_ref[...], b_ref[...].astype(BF), NN)

        @pl.when(k == 0)
        def _():
            acc[...] = p

        @pl.when(k > 0)
        def _():
            acc[...] += p

        @pl.when(k == nk - 1)
        def _():
            o_ref[...] = acc[...].astype(BF)

    return _pcall(body, grid=(N // tn, M // tm, nk),
                  in_specs=[pl.BlockSpec((tm, tk), lambda j, i, k: (i, k)), pl.BlockSpec((tk, tn), lambda j, i, k: (k, j))],
                  out_specs=pl.BlockSpec((tm, tn), lambda j, i, k: (i, j)), out_shape=_S((M, N), BF),
                  scratch_shapes=[pltpu.VMEM((tm, tn), F32)], name=name)(a_t, b)


def matmul_tn(a, b, name, b_col0=0, n_cols=None, out_dtype=F32):
    T, M = a.shape
    N = b.shape[1] if n_cols is None else n_cols
    tm, tn, tt = _pick(M, 512), _pick(N, 1024), min(T, 512)
    nt = T // tt
    jb = b_col0 // tn

    def body(a_ref, b_ref, o_ref, acc):
        t = pl.program_id(2)
        p = _dotb(a_ref[...].astype(BF), b_ref[...].astype(BF), TN)

        @pl.when(t == 0)
        def _():
            acc[...] = p

        @pl.when(t > 0)
        def _():
            acc[...] += p

        @pl.when(t == nt - 1)
        def _():
            o_ref[...] = acc[...].astype(out_dtype)

    return _pcall(body, grid=(M // tm, N // tn, nt),
                  in_specs=[pl.BlockSpec((tt, tm), lambda i, j, t: (t, i)), pl.BlockSpec((tt, tn), lambda i, j, t: (t, jb + j))],
                  out_specs=pl.BlockSpec((tm, tn), lambda i, j, t: (i, j)), out_shape=_S((M, N), out_dtype),
                  scratch_shapes=[pltpu.VMEM((tm, tn), F32)], name=name)(a, b)


def _pieces(v):
    if v.ndim == 3:
        return [v[i] for i in range(v.shape[0])]
    n = v.shape[1] // LANES
    if n <= 1:
        return [v]
    return [v[:, i * LANES:(i + 1) * LANES] for i in range(n)]


def _join(ps, like_ndim):
    if like_ndim == 3:
        return jnp.stack(ps, axis=0)
    return ps[0] if len(ps) == 1 else jnp.concatenate(ps, axis=1)


def scan_fwd(name, f, xs, ps, ys, state_shape, nc, nh=1, rider=None):
    nx, npar, ny = len(xs), len(ps), len(ys)

    def body(*refs):
        x_refs, p_refs = refs[:nx], refs[nx:nx + npar]
        y_refs = refs[nx + npar:nx + npar + ny]
        st_out, st = refs[nx + npar + ny], refs[nx + npar + ny + 1]
        c, h = pl.program_id(0), pl.program_id(1)

        @pl.when(c == 0)
        def _():
            st[h] = jnp.zeros(state_shape, F32)

        S = st[h]
        st_out[...] = S
        yv, Sn = f([_pieces(r[...]) for r in x_refs], [_pieces(r[...]) for r in p_refs], _pieces(S), h)
        for r, v in zip(y_refs, yv):
            r[...] = _join(v, 2)
        st[h] = _join(Sn, 3)

    in_specs = [pl.BlockSpec(bs, im) for (_, bs, im) in xs]
    in_specs += [pl.BlockSpec(p.shape, (lambda c, h, n=p.ndim: (0,) * n)) for p in ps]
    out_specs = [pl.BlockSpec(bs, im) for (_, bs, im) in ys]
    out_specs.append(pl.BlockSpec((None, None) + tuple(state_shape), lambda c, h: (c, h, 0, 0, 0)))
    out_shape = [_S(s) for (s, _, _) in ys] + [_S((nc, nh) + tuple(state_shape))]
    return hosted_call(body, rider, grid=(nc, nh), in_specs=in_specs, out_specs=out_specs, out_shape=out_shape,
                       scratch_shapes=[pltpu.VMEM((nh,) + tuple(state_shape), F32)], name=name, args=[*[x[0] for x in xs], *ps])


def scan_bwd(name, f, xs, ps, ys, dys, states, state_shape, nc, nh=1, shared=(), rider=None):
    nx, npar, ny = len(xs), len(ps), len(ys)

    def body(*refs):
        x_refs, p_refs = refs[:nx], refs[nx:nx + npar]
        s_ref = refs[nx + npar]
        dy_refs = refs[nx + npar + 1:nx + npar + 1 + ny]
        o = nx + npar + 1 + ny
        dx_refs, dp_refs, dst = refs[o:o + nx], refs[o + nx:o + nx + npar], refs[o + nx + npar]
        c, h = pl.program_id(0), pl.program_id(1)

        @pl.when(c == 0)
        def _():
            dst[h] = jnp.zeros(state_shape, F32)

        @pl.when((c == 0) & (h == 0))
        def _():
            for r in dp_refs:
                r[...] = jnp.zeros(r.shape, F32)

        xv = [_pieces(r[...]) for r in x_refs]
        pv = [_pieces(r[...]) for r in p_refs]
        _, vjp = jax.vjp(lambda a, b, s: f(a, b, s, h), xv, pv, _pieces(s_ref[...]))
        dxv, dpv, dS = vjp(([_pieces(r[...]) for r in dy_refs], _pieces(dst[h])))
        for i, (r, v) in enumerate(zip(dx_refs, dxv)):
            if i in shared and nh > 1:
                @pl.when(h == 0)
                def _(r=r, v=v):
                    r[...] = _join(v, 2)

                @pl.when(h > 0)
                def _(r=r, v=v):
                    r[...] += _join(v, 2)
            else:
                r[...] = _join(v, 2)
        for r, v in zip(dp_refs, dpv):
            r[...] += _join(v, len(r.shape))
        dst[h] = _join(dS, 3)

    def rev(im):
        return lambda c, h: im(nc - 1 - c, h)

    in_specs = [pl.BlockSpec(bs, rev(im)) for (_, bs, im) in xs]
    in_specs += [pl.BlockSpec(p.shape, (lambda c, h, n=p.ndim: (0,) * n)) for p in ps]
    in_specs.append(pl.BlockSpec((None, None) + tuple(state_shape), lambda c, h: (nc - 1 - c, h, 0, 0, 0)))
    in_specs += [pl.BlockSpec(bs, rev(im)) for (_, bs, im) in ys]
    out_specs = [pl.BlockSpec(bs, rev(im)) for (_, bs, im) in xs]
    out_specs += [pl.BlockSpec(p.shape, (lambda c, h, n=p.ndim: (0,) * n)) for p in ps]
    out_shape = [_S(x[0].shape) for x in xs] + [_S(p.shape) for p in ps]
    return hosted_call(body, rider, grid=(nc, nh), in_specs=in_specs, out_specs=out_specs, out_shape=out_shape,
                       scratch_shapes=[pltpu.VMEM((nh,) + tuple(state_shape), F32)], name=name,
                       args=[*[x[0] for x in xs], *ps, states, *dys])


def _lane():
    return _iota((1, LANES), 1)


def _col(v, idx):
    return jnp.sum(v * (_lane() == idx).astype(F32), axis=1, keepdims=True)


def _last_row(v):
    r = v.shape[0]
    return jnp.sum(v * (_iota((r, 1), 0) == r - 1).astype(F32), axis=0, keepdims=True)


def _tril(n, strict=False):
    r, c = _iota((n, n), 0), _iota((n, n), 1)
    return (r > c) if strict else (r >= c)


def ssd_chunk(xs, ps, S, h):
    zp, xbc, (sm,) = xs
    (bias,), (alog,), (dsk,), nw = ps
    Q = sm.shape[0]
    H = range(8)
    lane = _lane()
    a128 = jnp.where(lane < 8, -jnp.exp(alog), 0.0)
    dtl = jax.nn.softplus(sm + bias)
    tri = _tril(Q)
    cum = mm(tri.astype(F32), dtl * a128, "nn", "xa")
    sel8 = (_iota((8, LANES), 0) == _iota((8, LANES), 1)).astype(F32)
    cum_t = mm(sel8, cum, "nt", "xa")
    m0 = (lane < 64).astype(F32)
    rows0 = (_iota((LANES, 1), 0) < 64).astype(F32)
    me = [m0 if hh % 2 == 0 else 1.0 - m0 for hh in H]
    re = [rows0 if hh % 2 == 0 else 1.0 - rows0 for hh in H]
    Bm, Cm = [xbc[4 + hh // 4] for hh in H], [xbc[6 + hh // 4] for hh in H]
    cb = [mm(xbc[6 + g], xbc[4 + g], "nt", "1") for g in range(2)]
    col = [_col(cum, hh) for hh in H]
    row = [jnp.sum(cum_t * (_iota((8, 1), 0) == hh).astype(F32), axis=0, keepdims=True) for hh in H]
    xh = [xbc[hh // 2] * me[hh] for hh in H]
    xdt = [xh[hh] * _col(dtl, hh) for hh in H]
    seg = [jnp.exp(jnp.where(tri, col[hh] - row[hh], -jnp.inf)) for hh in H]
    last = [_last_row(col[hh]) for hh in H]
    y_diag = [mm(cb[hh // 4] * seg[hh], xdt[hh], "nn", "1") for hh in H]
    y_off = [mm(Cm[hh] * jnp.exp(col[hh]), S[hh // 2], "nt", "1") * me[hh] for hh in H]
    st = [mm(xdt[hh], Bm[hh] * jnp.exp(last[hh] - col[hh]), "tn", "1") for hh in H]
    y = [y_diag[hh] + y_off[hh] + _col(dsk, hh) * xh[hh] for hh in H]
    Sn = [S[pr] * (jnp.exp(last[2 * pr]) * re[0] + jnp.exp(last[2 * pr + 1]) * re[1]) + st[2 * pr] + st[2 * pr + 1] for pr in range(4)]
    yz = [(y[2 * pr] + y[2 * pr + 1]) * _silu(zp[pr]) for pr in range(4)]
    ssq = sum(jnp.sum(v * v, axis=1, keepdims=True) for v in yz)
    scale = lax.rsqrt(ssq / BRANCH_W + NORM_EPS)
    return [[yz[i] * scale * nw[i] for i in range(4)]], Sn


@jax.custom_vjp
def _halves(x):
    r = x.shape[0] // 2
    return x[:r], x[r:]


_halves.defvjp(lambda x: (_halves(x), None), lambda _, g: (jnp.concatenate(g, axis=0),))
DN_CHUNK = 64
DN_ROWS = 256


def _chunks(x):
    if x.shape[0] == DN_CHUNK:
        return [x]
    a, b = _halves(x)
    return _chunks(a) + _chunks(b)


def dn_chunk(xs, ps, S, h):
    act, gate, (sm,) = xs
    (alog,), (dtb,), (nw,) = ps
    C = DN_CHUNK
    n = sm.shape[0] // C
    I = range(4 * n)
    lane = _lane()
    tri, strict = _tril(C), _tril(C, True)
    sm2 = _chunks(sm)
    G = [jnp.where((lane >= A0) & (lane < A0 + 4), -jnp.exp(alog) * jax.nn.softplus(s + dtb), 0.0) for s in sm2]
    gcs = [mm(tri.astype(F32), g, "nn", "xa") for g in G]
    sig = [jax.nn.sigmoid(s) for s in sm2]
    parts = [_chunks(x) for x in act]
    q, k, v = ([parts[o + i % 4][i // 4] for i in I] for o in (0, 4, 8))
    gt = [_chunks(x) for x in gate]
    qn = [q[i] * lax.rsqrt(jnp.sum(q[i] * q[i], axis=1, keepdims=True) + NORM_EPS) * (LANES ** -0.5) for i in I]
    kn = [k[i] * lax.rsqrt(jnp.sum(k[i] * k[i], axis=1, keepdims=True) + NORM_EPS) for i in I]
    beta = [_col(sig[i // 4], BETA0 + i % 4) for i in I]
    gcol = [_col(gcs[i // 4], A0 + i % 4) for i in I]
    selr = [((_iota((8, LANES), 0) == 0) & (_iota((8, LANES), 1) == A0 + h)).astype(F32) for h in range(4)]
    grow = [jnp.sum(mm(selr[i % 4], gcs[i // 4], "nt", "xa"), axis=0, keepdims=True) for i in I]
    gamma = [jnp.exp(jnp.where(tri, gcol[i] - grow[i], -jnp.inf)) for i in I]
    kb = [kn[i] * beta[i] for i in I]
    pk = [-(mm(kb[i], kn[i], "nt", "1") * jnp.where(strict, gamma[i], 0.0)) for i in I]
    eye = (_iota((C, C), 0) == _iota((C, C), 1)).astype(F32)
    minv = [eye + pk[i] for i in I]
    for _ in range(5):
        pk = [mm(pk[i], pk[i], "nn", "3") for i in I]
        minv = [minv[i] + mm(minv[i], pk[i], "nn", "3") for i in I]
    eg = [jnp.exp(gcol[i]) for i in I]
    w = [mm(minv[i], kb[i] * eg[i], "nn", "3") for i in I]
    u = [mm(minv[i], v[i] * beta[i], "nn", "3") for i in I]
    glast = [_last_row(gcol[i]) for i in I]
    qg = [qn[i] * eg[i] for i in I]
    qk = [mm(qn[i], kn[i], "nt", "1") * gamma[i] for i in I]
    kdec = [kn[i] * jnp.exp(glast[i] - gcol[i]) for i in I]
    y = []
    for c in range(n):
        J = range(4 * c, 4 * c + 4)
        vnew = [u[i] - mm(w[i], S[i % 4], "nn", "1") for i in J]
        o = [mm(qg[i], S[i % 4], "nn", "1") + mm(qk[i], vn, "nn", "1") for i, vn in zip(J, vnew)]
        S = [S[i % 4] * jnp.exp(glast[i]) + mm(kdec[i], vn, "tn", "1") for i, vn in zip(J, vnew)]
        on = [x * lax.rsqrt(jnp.mean(x * x, axis=1, keepdims=True) + NORM_EPS) * nw for x in o]
        y.append([on[h] * _silu(gt[h][c]) for h in range(4)])
    return [[jnp.concatenate([y[c][h] for c in range(n)], axis=0) for h in range(4)]], S


def sg_chunk(xs, ps, S, h):
    (uv,) = xs
    lng, lnb, W, (bt,) = ps
    u = [jax.nn.gelu(p) for p in uv[:4]]
    v = [jax.nn.gelu(p) for p in uv[4:]]
    mu = sum(jnp.sum(p, axis=1, keepdims=True) for p in v) / BRANCH_W
    vc = [p - mu for p in v]
    var = sum(jnp.sum(p * p, axis=1, keepdims=True) for p in vc) / BRANCH_W
    inv = lax.rsqrt(var + LN_EPS)
    trif = _tril(W[0].shape[0]).astype(F32)
    out = []
    for g in range(4):
        vn = vc[g] * inv * lng[g] + lnb[g]
        out.append(u[g] * (mm(W[g] * trif, vn, "nn", "1") + _col(bt, g)))
    return [out], S


def foxc_chunk(xs, ps, S, h):
    (sm,), ((fb,),), (carry,) = xs[0], ps, S
    lane = _lane()
    ls = jnp.where((lane >= FF0) & (lane < FF0 + 8), jax.nn.log_sigmoid(sm + fb), 0.0)
    c = mm(_tril(sm.shape[0]).astype(F32), ls, "nn", "xa") + carry
    return [[c]], [_last_row(c)]


HALO = 8


def _conv_tiles(T, C):
    return min(T, 512), _pick(C, 512)


def conv_fwd(x, w, b, name):
    T, C = x.shape
    tm, cb = _conv_tiles(T, C)

    def body(xp_ref, x_ref, w_ref, b_ref, o_ref):
        i = pl.program_id(1)
        e = jnp.concatenate([xp_ref[...] * (i > 0).astype(F32), x_ref[...]], axis=0)
        pre = b_ref[...] + sum(w_ref[k:k + 1, :] * e[5 + k:5 + k + tm, :] for k in range(4))
        o_ref[...] = _silu(pre)

    hb = tm // HALO
    return _pcall(body, grid=(C // cb, T // tm),
                  in_specs=[pl.BlockSpec((HALO, cb), lambda j, i: (jnp.maximum(i * hb - 1, 0), j)), pl.BlockSpec((tm, cb), lambda j, i: (i, j)),
                            pl.BlockSpec((4, cb), lambda j, i: (0, j)), pl.BlockSpec((1, cb), lambda j, i: (0, j))],
                  out_specs=pl.BlockSpec((tm, cb), lambda j, i: (i, j)), out_shape=_S((T, C)), name=name)(x, x, w, b)


def conv_bwd(x, w, b, dact, name):
    T, C = x.shape
    tm, cb = _conv_tiles(T, C)
    nt = T // tm

    def body(xp_ref, x_ref, xn_ref, w_ref, b_ref, d_ref, dn_ref, dx_ref, dw_ref, db_ref):
        i = pl.program_id(1)
        has_prev, has_next = (i > 0).astype(F32), (i < nt - 1).astype(F32)
        e = jnp.concatenate([xp_ref[...] * has_prev, x_ref[...], xn_ref[...] * has_next], axis=0)
        pre = b_ref[...] + sum(w_ref[k:k + 1, :] * e[5 + k:5 + k + tm + 8, :] for k in range(4))
        de = jnp.concatenate([d_ref[...], dn_ref[...] * has_next], axis=0)
        sg = jax.nn.sigmoid(pre)
        dpre = de * (sg * (1.0 + pre * (1.0 - sg)))
        dx_ref[...] = sum(w_ref[k:k + 1, :] * dpre[3 - k:3 - k + tm, :] for k in range(4))
        dcur = dpre[0:tm, :]
        dw = jnp.concatenate([jnp.sum(dcur * e[5 + k:5 + k + tm, :], axis=0, keepdims=True) for k in range(4)], axis=0)
        db = jnp.sum(dcur, axis=0, keepdims=True)

        @pl.when(i == 0)
        def _():
            dw_ref[...] = dw
            db_ref[...] = db

        @pl.when(i > 0)
        def _():
            dw_ref[...] += dw
            db_ref[...] += db

    blk = lambda f: pl.BlockSpec((tm, cb), f)
    hb = tm // HALO
    before = pl.BlockSpec((HALO, cb), lambda j, i: (jnp.maximum(i * hb - 1, 0), j))
    after = pl.BlockSpec((HALO, cb), lambda j, i: (jnp.minimum((i + 1) * hb, nt * hb - 1), j))
    return _pcall(body, grid=(C // cb, nt),
                  in_specs=[before, blk(lambda j, i: (i, j)), after,
                            pl.BlockSpec((4, cb), lambda j, i: (0, j)), pl.BlockSpec((1, cb), lambda j, i: (0, j)),
                            blk(lambda j, i: (i, j)), after],
                  out_specs=[blk(lambda j, i: (i, j)), pl.BlockSpec((4, cb), lambda j, i: (0, j)), pl.BlockSpec((1, cb), lambda j, i: (0, j))],
                  out_shape=[_S((T, C)), _S((4, C)), _S((1, C))], name=name)(x, x, x, w, b, dact, dact)


FOX_SCALE = 64 ** -0.5
LOG2E = 1.4426950408889634


def _spare(e, i):
    return (_lane() == 64 * (1 - e) + i).astype(F32)


def _lanes_of(e):
    lane = _lane()
    return ((lane < 64) if e == 0 else (lane >= 64)).astype(F32)


def _col3(col, e, first):
    c1 = col.astype(BF).astype(F32)
    c2 = (col - c1).astype(BF).astype(F32)
    c3 = (col - c1 - c2).astype(BF).astype(F32)
    return c1 * _spare(e, first) + c2 * _spare(e, first + 1) + c3 * _spare(e, first + 2)


def _ones3(e, first):
    return _spare(e, first) + _spare(e, first + 1) + _spare(e, first + 2)


def _causal_bias(n):
    return jnp.where(_iota((n, n), 0) >= _iota((n, n), 1), 0.0, -jnp.inf).astype(F32)


def _c_col(cc, hh):
    return jnp.sum(cc * (_lane() == FF0 + hh).astype(F32), axis=1, keepdims=True) * LOG2E


def _pair_spec(tq, row_of):
    return pl.BlockSpec((None, 2, tq, LANES), lambda hp, a, b: (hp, 0, row_of(a, b), 0))


def fox_prep(qkv, ccol, name):
    T = qkv.shape[0]
    tq = min(T, 512)

    def body(q_ref, k_ref, v_ref, cc_ref, qa_ref, ka_ref, va_ref):
        hp = pl.program_id(0)
        q, k, v, cc = q_ref[...], k_ref[...], v_ref[...], cc_ref[...]
        for e in range(2):
            me = _lanes_of(e)
            c2 = _c_col(cc, 2 * hp + e)
            qa_ref[e] = (q * me * (FOX_SCALE * LOG2E) + _col3(c2, e, 0) + _ones3(e, 3)).astype(BF)
            ka_ref[e] = (k * me + _ones3(e, 0) + _col3(-c2, e, 3) + _ones3(e, 6)).astype(BF)
            va_ref[e] = (v * me + (1.0 - me)).astype(BF)

    blk = lambda off: pl.BlockSpec((tq, LANES), lambda hp, i: (i, off + hp))
    out = pl.BlockSpec((None, 2, tq, LANES), lambda hp, i: (hp, 0, i, 0))
    return _pcall(body, grid=(4, T // tq), in_specs=[blk(0), blk(4), blk(8), pl.BlockSpec((tq, LANES), lambda hp, i: (i, 0))],
                  out_specs=[out] * 3, out_shape=[_S((4, 2, T, LANES), BF)] * 3, name=name)(qkv, qkv, qkv, ccol)


def fox_fwd(qa, ka, va, name, rider=None):
    T = qa.shape[2]
    tq = min(T, 512)
    nq = T // tq

    def body(qa_ref, ka_ref, va_ref, o_ref, lse_ref, m_s, acc, causal):
        i, j = pl.program_id(1), pl.program_id(2)

        @pl.when((pl.program_id(0) == 0) & (i == 0) & (j == 0))
        def _():
            causal[...] = _causal_bias(tq)

        @pl.when(j == 0)
        def _():
            m_s[...] = jnp.full(m_s.shape, -jnp.inf, F32)
            acc[...] = jnp.zeros(acc.shape, F32)

        def step(diagonal):
            for e in range(2):
                s = _dotb(qa_ref[e], ka_ref[e], NT)
                if diagonal:
                    s = s + causal[...]
                m_old = m_s[e]
                m_new = jnp.maximum(m_old, jnp.max(s, axis=1, keepdims=True))
                p = jnp.exp2(s - m_new)
                m_s[e] = m_new
                acc[e] = acc[e] * jnp.exp2(m_old - m_new) + _dotb(p.astype(BF), va_ref[e], NN)

        @pl.when(j < i)
        def _():
            step(False)

        @pl.when(j == i)
        def _():
            step(True)
            lane = _lane()
            o, lse = 0.0, 0.0
            for e in range(2):
                l = jnp.sum(acc[e] * _spare(e, 0), axis=1, keepdims=True)
                o = o + acc[e] * _lanes_of(e) / l
                lse = lse + (m_s[e] + jnp.log2(l)) * (lane == e).astype(F32)
            o_ref[...] = o
            lse_ref[...] = lse

    kv = _pair_spec(tq, lambda i, j: jnp.minimum(j, i))
    return hosted_call(body, rider, grid=(4, nq, nq), in_specs=[_pair_spec(tq, lambda i, j: i), kv, kv],
                       out_specs=[pl.BlockSpec((tq, LANES), lambda hp, i, j: (i, hp)), pl.BlockSpec((None, tq, LANES), lambda hp, i, j: (hp, i, 0))],
                       out_shape=[_S((T, BRANCH_W)), _S((4, T, LANES))],
                       scratch_shapes=[pltpu.VMEM((2, tq, 1), F32), pltpu.VMEM((2, tq, LANES), F32), pltpu.VMEM((tq, tq), F32)],
                       name=name, args=[qa, ka, va])


def fox_prep_bwd(qkv, qa, o, lse, do, name):
    T = qkv.shape[0]
    tq = min(T, 512)

    def body(q_ref, k_ref, qa_ref, o_ref, lse_ref, do_ref, qb_ref, doa_ref, qd_ref, kd_ref):
        q, k, dov = q_ref[...], k_ref[...], do_ref[...]
        dd = dov * o_ref[...]
        lane = _lane()
        for e in range(2):
            me = _lanes_of(e)
            lse_e = jnp.sum(lse_ref[...] * (lane == e).astype(F32), axis=1, keepdims=True)
            qb_ref[e] = (qa_ref[e].astype(F32) + _col3(-lse_e, e, 6)).astype(BF)
            doa_ref[e] = (dov * me + _col3(-jnp.sum(dd * me, axis=1, keepdims=True), e, 0)).astype(BF)
            qd_ref[e] = (q * me * FOX_SCALE + _spare(e, 0)).astype(BF)
            kd_ref[e] = (k * me * FOX_SCALE + _spare(e, 0)).astype(BF)

    blk = lambda off: pl.BlockSpec((tq, LANES), lambda hp, i: (i, off + hp))
    pair = pl.BlockSpec((None, 2, tq, LANES), lambda hp, i: (hp, 0, i, 0))
    return _pcall(body, grid=(4, T // tq),
                  in_specs=[blk(0), blk(4), pair, blk(0), pl.BlockSpec((None, tq, LANES), lambda hp, i: (hp, i, 0)), blk(0)],
                  out_specs=[pair] * 4, out_shape=[_S((4, 2, T, LANES), BF)] * 4, name=name)(qkv, qkv, qa, o, lse, do)


def fox_bwd(qb, ka, va, doa, qd, kd, name, rider=None):
    T = qb.shape[2]
    tq = min(T, 512)
    nq = T // tq

    def body(qb_ref, ka_ref, va_ref, doa_ref, qd_ref, kd_ref, dq_ref, dk_ref, dv_ref, dcc_ref, dq_s, dk_s, dv_s, causal):
        hp, j, ii = pl.program_id(0), pl.program_id(1), pl.program_id(2)
        i = jnp.maximum(ii, j)

        @pl.when((hp == 0) & (j == 0) & (ii == 0))
        def _():
            dcc_ref[...] = jnp.zeros(dcc_ref.shape, F32)
            causal[...] = _causal_bias(tq)

        @pl.when((j == 0) & (ii == 0))
        def _():
            dq_s[...] = jnp.zeros(dq_s.shape, F32)

        @pl.when(ii == 0)
        def _():
            dk_s[...] = jnp.zeros(dk_s.shape, F32)
            dv_s[...] = jnp.zeros(dv_s.shape, F32)

        def step(diagonal):
            rows = pl.ds(pl.multiple_of(i * tq, tq), tq)
            for e in range(2):
                s = _dotb(qb_ref[e], ka_ref[e], NT)
                if diagonal:
                    s = s + causal[...]
                p = jnp.exp2(s)
                ds = (p * _dotb(doa_ref[e], va_ref[e], NT)).astype(BF)
                dv_s[e] += _dotb(p.astype(BF), doa_ref[e], TN)
                dq_s[e, rows, :] += _dotb(ds, kd_ref[e], NN)
                dk_s[e] += _dotb(ds, qd_ref[e], TN)

        @pl.when(ii > j)
        def _():
            step(False)

        @pl.when(ii == j)
        def _():
            step(True)

        def fold(acc, sign):
            grad, dc = 0.0, 0.0
            for e in range(2):
                a = acc[e]
                grad = grad + a * _lanes_of(e)
                dc = dc + sign * jnp.sum(a * _spare(e, 0), axis=1, keepdims=True) * (_lane() == FF0 + 2 * hp + e).astype(F32)
            return grad, dc

        @pl.when(ii == nq - 1)
        def _():
            grad, dc = fold(dk_s, -1.0)
            dk_ref[...] = grad
            dv_ref[...] = dv_s[0] * _lanes_of(0) + dv_s[1] * _lanes_of(1)
            dcc_ref[pl.ds(pl.multiple_of(j * tq, tq), tq), :] += dc

        @pl.when((j == nq - 1) & (ii == nq - 1))
        def _():
            grad, dc = fold(dq_s, 1.0)
            dq_ref[...] = grad
            dcc_ref[...] += dc

    irow, jrow = _pair_spec(tq, lambda j, ii: jnp.maximum(ii, j)), _pair_spec(tq, lambda j, ii: j)
    jout = pl.BlockSpec((tq, LANES), lambda hp, j, ii: (j, hp))
    return hosted_call(body, rider, grid=(4, nq, nq), in_specs=[irow, jrow, jrow, irow, irow, jrow],
                       out_specs=[pl.BlockSpec((T, LANES), lambda hp, j, ii: (0, hp)), jout, jout, pl.BlockSpec((T, LANES), lambda hp, j, ii: (0, 0))],
                       out_shape=[_S((T, BRANCH_W)), _S((T, BRANCH_W)), _S((T, BRANCH_W)), _S((T, LANES))],
                       scratch_shapes=[pltpu.VMEM((2, T, LANES), F32), pltpu.VMEM((2, tq, LANES), F32), pltpu.VMEM((2, tq, LANES), F32),
                                       pltpu.VMEM((tq, tq), F32)],
                       name=name, args=[qb, ka, va, doa, qd, kd])


def _acc_out(ref, val, first):
    @pl.when(first)
    def _():
        ref[...] = val

    @pl.when(jnp.logical_not(first))
    def _():
        ref[...] += val


def _row(tm, c):
    return pl.BlockSpec((tm, c), lambda i: (i, 0))


def _full(shape):
    return pl.BlockSpec(shape, lambda *_: (0,) * len(shape))


def ln_fwd(x, g, b, name, rider=None):
    T, C = x.shape
    tm = min(T, 512)

    def body(x_ref, g_ref, b_ref, o_ref):
        o_ref[...] = _ln(x_ref[...], g_ref[...], b_ref[...])

    return hosted_call(body, rider, grid=(T // tm,), in_specs=[_row(tm, C), _full((1, C)), _full((1, C))], out_specs=[_row(tm, C)],
                       out_shape=[_S((T, C))], scratch_shapes=[], name=name, args=[x, g, b])


def ln_bwd(x, g, b, dy, name):
    T, C = x.shape
    tm = min(T, 512)

    def body(x_ref, g_ref, b_ref, dy_ref, dx_ref, dg_ref, db_ref):
        _, vjp = jax.vjp(_ln, x_ref[...], g_ref[...], b_ref[...])
        dx, dg, db = vjp(dy_ref[...])
        dx_ref[...] = dx
        first = pl.program_id(0) == 0
        _acc_out(dg_ref, dg, first)
        _acc_out(db_ref, db, first)

    return _pcall(body, grid=(T // tm,), in_specs=[_row(tm, C), _full((1, C)), _full((1, C)), _row(tm, C)],
                  out_specs=[_row(tm, C), _full((1, C)), _full((1, C))], out_shape=[_S((T, C)), _S((1, C)), _S((1, C))], name=name)(x, g, b, dy)


def loss_head(h, target, name):
    T, C = h.shape
    tm = min(T, 512)

    def body(h_ref, t_ref, d_ref, l_ref):
        e = h_ref[...] - t_ref[...]
        d_ref[...] = e * (1.0 / C)
        part = jnp.sum(jnp.sum(e * e, axis=1, keepdims=True), axis=0, keepdims=True) * (0.5 / C)
        _acc_out(l_ref, part, pl.program_id(0) == 0)

    return _pcall(body, grid=(T // tm,), in_specs=[_row(tm, C), _row(tm, C)], out_specs=[_row(tm, C), _full((1, 1))],
                  out_shape=[_S((T, C)), _S((1, 1))], name=name)(h, target)


def add3(a, b, c, name):
    T, C = a.shape
    tm = min(T, 512)

    def body(a_ref, b_ref, c_ref, o_ref):
        o_ref[...] = a_ref[...] + b_ref[...] + c_ref[...]

    return _pcall(body, grid=(T // tm,), in_specs=[_row(tm, C)] * 3, out_specs=_row(tm, C), out_shape=_S((T, C)), name=name)(a, b, c)


def _wb_spec(l):
    return pl.BlockSpec((None, 4, BRANCH_W, D_MODEL), lambda *_: (l, 0, 0, 0))


def merge_fwd(ys, gl, gb, wb, l, name):
    T = gl.shape[0]
    tm = min(T, 256)

    def body(y0, y1, y2, y3, gl_ref, gb_ref, wb_ref, o_ref):
        acc = 0.0
        for i, y in enumerate((y0, y1, y2, y3)):
            z = _dotb(y[...].astype(BF), wb_ref[i], NN)
            g = jax.nn.sigmoid(gl_ref[:, i * D_MODEL:(i + 1) * D_MODEL] + gb_ref[i:i + 1, :])
            acc = acc + g * z
        o_ref[...] = acc

    return _pcall(body, grid=(T // tm,), in_specs=[_row(tm, BRANCH_W)] * 4 + [_row(tm, 4 * D_MODEL), _full((4, D_MODEL)), _wb_spec(l)],
                  out_specs=_row(tm, D_MODEL), out_shape=_S((T, D_MODEL)), name=name)(*ys, gl, gb, wb)


def merge_bwd(ys, gl, gb, wb, l, dm, name):
    T = gl.shape[0]
    tm = min(T, 256)

    def body(y0, y1, y2, y3, gl_ref, gb_ref, wb_ref, dm_ref, d0, d1, d2, d3, dgl_ref, dz_ref, dgb_ref):
        dmv = dm_ref[...]
        first = pl.program_id(0) == 0
        for i, (y, d) in enumerate(zip((y0, y1, y2, y3), (d0, d1, d2, d3))):
            cols = slice(i * D_MODEL, (i + 1) * D_MODEL)
            z = _dotb(y[...].astype(BF), wb_ref[i], NN)
            g = jax.nn.sigmoid(gl_ref[:, cols] + gb_ref[i:i + 1, :])
            dgl = dmv * z * (g * (1.0 - g))
            dz = (g * dmv).astype(BF)
            dgl_ref[:, cols] = dgl
            dz_ref[:, cols] = dz
            d[...] = _dotb(dz, wb_ref[i], NT)
            _acc_out(dgb_ref.at[i:i + 1, :], jnp.sum(dgl, axis=0, keepdims=True), first)

    return _pcall(body, grid=(T // tm,),
                  in_specs=[_row(tm, BRANCH_W)] * 4 + [_row(tm, 4 * D_MODEL), _full((4, D_MODEL)), _wb_spec(l), _row(tm, D_MODEL)],
                  out_specs=[_row(tm, BRANCH_W)] * 4 + [_row(tm, 4 * D_MODEL), _row(tm, 4 * D_MODEL), _full((4, D_MODEL))],
                  out_shape=[_S((T, BRANCH_W))] * 4 + [_S((T, 4 * D_MODEL)), _S((T, 4 * D_MODEL), BF), _S((4, D_MODEL))], name=name)(
                      *ys, gl, gb, wb, dm)


def _wout_spec(l):
    return pl.BlockSpec((None, D_MODEL, D_MODEL), lambda *_: (l, 0, 0))


def out_fwd(merged, h, wout, l, g, b, name):
    T = h.shape[0]
    tm = min(T, 512)

    def body(m_ref, h_ref, w_ref, g_ref, b_ref, u_ref, o_ref):
        u = ALPHA * h_ref[...] + _dotb(m_ref[...].astype(BF), w_ref[...], NN)
        u_ref[...] = u
        o_ref[...] = _ln(u, g_ref[...], b_ref[...])

    C = D_MODEL
    return _pcall(body, grid=(T // tm,), in_specs=[_row(tm, C), _row(tm, C), _wout_spec(l), _full((1, C)), _full((1, C))],
                  out_specs=[_row(tm, C), _row(tm, C)], out_shape=[_S((T, C)), _S((T, C))], name=name)(merged, h, wout, g, b)


def out_bwd(u, dy, g, b, wout, l, name):
    T, C = u.shape
    tm = min(T, 512)

    def body(u_ref, dy_ref, g_ref, b_ref, w_ref, du_ref, dm_ref, dg_ref, db_ref):
        _, vjp = jax.vjp(_ln, u_ref[...], g_ref[...], b_ref[...])
        du, dg, db = vjp(dy_ref[...])
        du_ref[...] = du
        dm_ref[...] = _dotb(du.astype(BF), w_ref[...], NT)
        first = pl.program_id(0) == 0
        _acc_out(dg_ref, dg, first)
        _acc_out(db_ref, db, first)

    return _pcall(body, grid=(T // tm,), in_specs=[_row(tm, C), _row(tm, C), _full((1, C)), _full((1, C)), _wout_spec(l)],
                  out_specs=[_row(tm, C), _row(tm, C), _full((1, C)), _full((1, C))],
                  out_shape=[_S((T, C)), _S((T, C)), _S((1, C)), _S((1, C))], name=name)(u, dy, g, b, wout)


def ff_fwd(h, wup, wdown, l, g, b, name):
    T, C = h.shape
    F = wup.shape[2]
    tm, tf = min(T, 512), 1024
    nf = F // tf

    def body(h_ref, wu_ref, wd_ref, g_ref, b_ref, u_ref, o_ref, acc):
        f = pl.program_id(1)
        a = _dotb(h_ref[...].astype(BF), wu_ref[...], NN)
        r = jnp.square(jnp.maximum(a, 0.0))
        p = _dotb(r.astype(BF), wd_ref[...], NN)
        _acc_out(acc, p, f == 0)

        @pl.when(f == nf - 1)
        def _():
            u = ALPHA * h_ref[...] + acc[...]
            u_ref[...] = u
            o_ref[...] = _ln(u, g_ref[...], b_ref[...])

    row = pl.BlockSpec((tm, C), lambda i, f: (i, 0))
    return _pcall(body, grid=(T // tm, nf),
                  in_specs=[row, pl.BlockSpec((None, C, tf), lambda i, f: (l, 0, f)), pl.BlockSpec((None, tf, C), lambda i, f: (l, f, 0)),
                            _full((1, C)), _full((1, C))],
                  out_specs=[row, row], out_shape=[_S((T, C)), _S((T, C))], scratch_shapes=[pltpu.VMEM((tm, C), F32)], name=name)(h, wup, wdown, g, b)


def ff_bwd(u, dy, h, g, b, wup, wdown, l, name):
    T, C = h.shape
    F = wup.shape[2]
    tm, tf = min(T, 512), 1024
    nf = F // tf

    def body(u_ref, dy_ref, h_ref, g_ref, b_ref, wu_ref, wd_ref, du_ref, dh_ref, da_ref, r_ref, dg_ref, db_ref, du_s, acc):
        i, f = pl.program_id(0), pl.program_id(1)

        @pl.when(f == 0)
        def _():
            _, vjp = jax.vjp(_ln, u_ref[...], g_ref[...], b_ref[...])
            du, dg, db = vjp(dy_ref[...])
            du_s[...] = du
            du_ref[...] = du
            _acc_out(dg_ref, dg, i == 0)
            _acc_out(db_ref, db, i == 0)

        a = _dotb(h_ref[...].astype(BF), wu_ref[...], NN)
        ap = jnp.maximum(a, 0.0)
        dr = _dotb(du_s[...].astype(BF), wd_ref[...], NT)
        da = (dr * (2.0 * ap)).astype(BF)
        da_ref[...] = da
        r_ref[...] = jnp.square(ap).T.astype(BF)
        _acc_out(acc, _dotb(da, wu_ref[...], NT), f == 0)

        @pl.when(f == nf - 1)
        def _():
            dh_ref[...] = ALPHA * du_s[...] + acc[...]

    row = pl.BlockSpec((tm, C), lambda i, f: (i, 0))
    colf = pl.BlockSpec((tm, tf), lambda i, f: (i, f))
    return _pcall(body, grid=(T // tm, nf),
                  in_specs=[row, row, row, _full((1, C)), _full((1, C)), pl.BlockSpec((None, C, tf), lambda i, f: (l, 0, f)),
                            pl.BlockSpec((None, tf, C), lambda i, f: (l, f, 0))],
                  out_specs=[row, row, colf, pl.BlockSpec((tf, tm), lambda i, f: (f, i)), _full((1, C)), _full((1, C))],
                  out_shape=[_S((T, C)), _S((T, C)), _S((T, F), BF), _S((F, T), BF), _S((1, C)), _S((1, C))],
                  scratch_shapes=[pltpu.VMEM((tm, C), F32), pltpu.VMEM((tm, C), F32)], name=name)(u, dy, h, g, b, wup, wdown)


MESH_ID = pl.DeviceIdType.MESH
_ANY = pl.BlockSpec(memory_space=pl.ANY)


def _window(ref, ax, idx, n):
    if n < 0:
        return ref
    sel = idx if n == 0 else pl.ds(pl.multiple_of(idx * n, n), n)
    return ref.at[(slice(None),) * ax + (sel,)]


Rider = collections.namedtuple("Rider", "operands out_shape scratch start wait")


def hosted_call(body, rider, *, grid, in_specs, out_specs, out_shape, scratch_shapes, name, args):
    n_in, n_out, n_scr = len(in_specs), len(out_specs), len(scratch_shapes)
    if rider is None:
        return _pcall(body, grid=grid, in_specs=in_specs, out_specs=out_specs, out_shape=out_shape, scratch_shapes=scratch_shapes, name=name)(*args), []
    ri, ro = len(rider.operands), len(rider.out_shape)

    def wrapped(*refs):
        ins, r_in = refs[:n_in], refs[n_in:n_in + ri]
        o0 = n_in + ri
        outs, r_out = refs[o0:o0 + n_out], refs[o0 + n_out:o0 + n_out + ro]
        s0 = o0 + n_out + ro
        scr, r_scr = refs[s0:s0 + n_scr], refs[s0 + n_scr:]
        ids = [pl.program_id(i) for i in range(len(grid))]
        first = functools.reduce(jnp.logical_and, [i == 0 for i in ids])
        last = functools.reduce(jnp.logical_and, [i == g - 1 for i, g in zip(ids, grid)])

        @pl.when(first)
        def _():
            rider.start(r_in, r_out, r_scr)

        body(*ins, *outs, *scr)

        @pl.when(last)
        def _():
            rider.wait(r_in, r_out, r_scr)

    res = _pcall(wrapped, grid=grid, in_specs=list(in_specs) + [_ANY] * ri, out_specs=list(out_specs) + [_ANY] * ro,
                 out_shape=list(out_shape) + list(rider.out_shape), scratch_shapes=list(scratch_shapes) + list(rider.scratch),
                 name=name)(*args, *rider.operands)
    return res[:n_out], res[n_out:]


def comm_call(rider, name):
    ri = len(rider.operands)

    def body(*refs):
        r_in, r_out, r_scr = refs[:ri], refs[ri:ri + len(rider.out_shape)], refs[ri + len(rider.out_shape):]
        rider.start(r_in, r_out, r_scr)
        rider.wait(r_in, r_out, r_scr)

    return _pcall(body, in_specs=[_ANY] * ri, out_specs=[_ANY] * len(rider.out_shape), out_shape=list(rider.out_shape),
                  scratch_shapes=list(rider.scratch), name=name)(*rider.operands)


def gather_rider(shards, axes):
    K = len(shards)
    widths = [s.shape[a] for s, a in zip(shards, axes)]
    out_shape = [_S(s.shape[:a] + (N_DEV * s.shape[a],) + s.shape[a + 1:], s.dtype) for s, a in zip(shards, axes)]

    def plan(x_refs, o_refs, sems):
        send_sems, recv_sems, local_sems = sems
        mx, my, mc = lax.axis_index("x"), lax.axis_index("y"), lax.axis_index("c")
        me, sibling = (mx, my, mc), (mx, my, 1 - mc)
        chips = [(1 - mx, my), (mx, 1 - my), (1 - mx, 1 - my)]

        def win(k, px, py, pc):
            return _window(o_refs[k], axes[k], 4 * px + 2 * py + pc, widths[k])

        def copy(k, slot, block, to, src=None):
            return pltpu.make_async_remote_copy(src_ref=win(k, *block) if src is None else src, dst_ref=win(k, *block),
                                                send_sem=send_sems.at[7 * k + slot], recv_sem=recv_sems.at[7 * k + slot],
                                                device_id=to, device_id_type=MESH_ID)

        mine = [pltpu.make_async_copy(x_refs[k], win(k, *me), local_sems.at[k]) for k in range(K)]
        first = []
        for k in range(K):
            first.append(copy(k, 0, me, sibling, src=x_refs[k]))
            first += [copy(k, 1 + j, me, (*chip, mc), src=x_refs[k]) for j, chip in enumerate(chips)]
        return me, sibling, chips, copy, mine, first

    def start(x_refs, o_refs, sems):
        _, _, _, _, mine, first = plan(x_refs, o_refs, sems)
        for cp in mine + first:
            cp.start()

    def wait(x_refs, o_refs, sems):
        me, sibling, chips, copy, mine, first = plan(x_refs, o_refs, sems)
        mc = me[2]
        passed = []
        for j, chip in enumerate(chips):
            for k in range(K):
                copy(k, 1 + j, (*chip, mc), me).wait_recv()
                passed.append(copy(k, 4 + j, (*chip, mc), sibling))
                passed[-1].start()
        for k in range(K):
            copy(k, 0, sibling, me).wait_recv()
        for j, chip in enumerate(chips):
            for k in range(K):
                copy(k, 4 + j, (*chip, 1 - mc), me).wait_recv()
        for cp in first + passed:
            cp.wait_send()
        for cp in mine:
            cp.wait()

    scratch = [pltpu.SemaphoreType.DMA((7 * K,)), pltpu.SemaphoreType.DMA((7 * K,)), pltpu.SemaphoreType.DMA((K,))]
    return Rider(list(shards), out_shape, scratch, start, wait)


def exchange_rider(items):
    ns = len(items)
    out_shape = [_S((N_DEV,) + tuple(it[3]), it[0].dtype) for it in items]

    def plan(src_refs, o_refs, sems):
        send_sems, recv_sems, local_sems = sems
        mx, my, mc = lax.axis_index("x"), lax.axis_index("y"), lax.axis_index("c")
        me = 4 * mx + 2 * my + mc
        remote, own = [], []
        for s, (_, ax, n, _) in enumerate(items):
            own.append(pltpu.make_async_copy(_window(src_refs[s], ax, me, n), o_refs[s].at[me], local_sems.at[s]))
            for k in range(1, N_DEV):
                px = 1 - mx if k & 4 else mx
                py = 1 - my if k & 2 else my
                pc = 1 - mc if k & 1 else mc
                remote.append(pltpu.make_async_remote_copy(
                    src_ref=_window(src_refs[s], ax, 4 * px + 2 * py + pc, n), dst_ref=o_refs[s].at[me],
                    send_sem=send_sems.at[7 * s + k - 1], recv_sem=recv_sems.at[7 * s + k - 1],
                    device_id=(px, py, pc), device_id_type=MESH_ID))
        return remote, own

    def start(src_refs, o_refs, sems):
        remote, own = plan(src_refs, o_refs, sems)
        for cp in own + remote:
            cp.start()

    def wait(src_refs, o_refs, sems):
        remote, own = plan(src_refs, o_refs, sems)
        for cp in remote + own:
            cp.wait()

    scratch = [pltpu.SemaphoreType.DMA((7 * ns,)), pltpu.SemaphoreType.DMA((7 * ns,)), pltpu.SemaphoreType.DMA((ns,))]
    return Rider([it[0] for it in items], out_shape, scratch, start, wait)


def reduce_adamw(rcvs, w, m, v, name):
    L = len(rcvs)
    _, A, B, C = rcvs[0].shape
    tb = B
    while tb > 8 and tb * C > (1 << 17):
        tb //= 2

    def body(*refs):
        r_refs, (w_ref, m_ref, v_ref, g_ref, d_ref, mo_ref, vo_ref) = refs[:L], refs[L:]
        for k in range(L):
            @pl.when(pl.program_id(0) == k)
            def _(k=k):
                g = r_refs[k][0].astype(F32)
                for d in range(1, N_DEV):
                    g = g + r_refs[k][d].astype(F32)
                mn = ADAM_B1 * m_ref[...] + (1.0 - ADAM_B1) * g
                vn = ADAM_B2 * v_ref[...] + (1.0 - ADAM_B2) * jnp.square(g)
                m_hat = mn / (1.0 - ADAM_B1 ** ADAM_STEP)
                v_hat = vn / (1.0 - ADAM_B2 ** ADAM_STEP)
                g_ref[...] = g
                d_ref[...] = -ADAM_LR * (m_hat / (jnp.sqrt(v_hat) + ADAM_EPS) + ADAM_WD * w_ref[...])
                mo_ref[...] = mn
                vo_ref[...] = vn

    def rspec(k):
        return pl.BlockSpec((N_DEV, None, tb, C), lambda l, a, i: (0, jnp.where(l == k, a, 0), jnp.where(l == k, i, 0), 0))

    blk = pl.BlockSpec((None, tb, C), lambda l, a, i: (l * A + a, i, 0))
    return _pcall(body, grid=(L, A, B // tb), in_specs=[rspec(k) for k in range(L)] + [blk, blk, blk],
                  out_specs=[blk] * 4, out_shape=[_S((L * A, B, C))] * 4, name=name)(*rcvs, w, m, v)


def _w_in_pieces(g0, g1):
    per = D_IN // N_DEV
    return [(d, max(g0, d * per) - d * per, min(g1, (d + 1) * per) - d * per) for d in range(N_DEV) if max(g0, d * per) < min(g1, (d + 1) * per)]


def repack_w_in(w8, name):
    _, L, R, per = w8.shape
    tr = 256

    def cols(x_ref, g0, g1):
        return [x_ref[d, :, a:b] for d, a, b in _w_in_pieces(g0, g1)]

    def body(x_ref, *o_refs):
        for (name_, i), o_ref in zip(SEGS, o_refs):
            o_ref[...] = jnp.concatenate(cols(x_ref, _OFF[i], _OFF[i + 1]), axis=1)
        parts, at = [], 0
        for i, lane0 in SMALL_SRC:
            assert lane0 == at
            parts += cols(x_ref, _OFF[i], _OFF[i + 1])
            at += IN_SIZES[i]
        parts.append(jnp.zeros((tr, LANES - at), w8.dtype))
        o_refs[-1][...] = jnp.concatenate(parts, axis=1)

    widths = [IN_SIZES[i] for _, i in SEGS] + [LANES]
    outs = _pcall(body, grid=(L, R // tr), in_specs=[pl.BlockSpec((N_DEV, None, tr, per), lambda l, r: (0, l, r, 0))],
                  out_specs=[pl.BlockSpec((None, tr, w), lambda l, r: (l, r, 0)) for w in widths],
                  out_shape=[_S((L, R, w), w8.dtype) for w in widths], name=name)(w8)
    return dict(zip(SEG_NAMES, outs))


def repack_dw_in(dseg, name):
    R = dseg["z"].shape[0]
    per = D_IN // N_DEV
    tr = 128
    src = {i: (k, 0) for k, (_, i) in enumerate(SEGS)}
    src.update({i: (len(SEGS), lane0) for i, lane0 in SMALL_SRC})

    def body(*refs):
        s_refs, o_ref = refs[:-1], refs[-1]
        for d in range(N_DEV):
            parts = []
            for i in range(len(IN_SIZES)):
                g0, g1 = max(_OFF[i], d * per), min(_OFF[i + 1], (d + 1) * per)
                if g0 < g1:
                    k, c0 = src[i]
                    parts.append(s_refs[k][:, c0 + g0 - _OFF[i]:c0 + g1 - _OFF[i]])
            o_ref[d] = jnp.concatenate(parts, axis=1)

    arrs = [dseg[n] for n in SEG_NAMES]
    return _pcall(body, grid=(R // tr,), in_specs=[pl.BlockSpec((tr, a.shape[1]), lambda r: (r, 0)) for a in arrs],
                  out_specs=pl.BlockSpec((N_DEV, tr, per), lambda r: (0, r, 0)), out_shape=_S((N_DEV, R, per), arrs[0].dtype), name=name)(*arrs)


WEIGHTS = ("ln_in_g", "ln_in_b", "w_in", "ssd_conv_w", "ssd_conv_b", "ssd_dt_bias", "ssd_a_log", "ssd_d", "ssd_norm_w", "dn_conv_w",
           "dn_a_log", "dn_dt_bias", "dn_norm_w", "sg_ln_g", "sg_ln_b", "sg_w", "sg_b", "fox_f_bias", "gate_b", "w_branch", "w_out",
           "ln1_g", "ln1_b", "w_up", "w_down", "ln2_g", "ln2_b")
SHARDED = {"w_in": 2, "ssd_conv_w": 2, "dn_conv_w": 2, "gate_b": 2, "w_branch": 3, "w_out": 1, "w_up": 2, "w_down": 1}
SLABBED = ("w_in", "dn_conv_w")
MATMUL_WEIGHTS = ("w_in", "w_branch", "w_out", "w_up", "w_down")
REPLICATED_ENTRY = ("ln_in_g", "ln_in_b")
REPLICATED_LAYER = tuple(n for n in WEIGHTS if n not in SHARDED and n not in REPLICATED_ENTRY and n != "sg_w")
SEG_NAMES = tuple(n for n, _ in SEGS) + ("small",)
PACK_COLS = 1024


def _lanes(vec, off):
    return jnp.pad(vec, (off, LANES - off - vec.shape[0]))[None]


def _pack_small(parts, row_mult):
    flat = jnp.concatenate([q.reshape(-1) for q in parts])
    rows = -(-flat.shape[0] // (PACK_COLS * row_mult)) * row_mult
    return jnp.pad(flat, (0, rows * PACK_COLS - flat.shape[0])).reshape(1, rows, PACK_COLS)


EARLY = ("w_branch", "w_out", "w_up", "w_down", "gate_b")
LATE = ("w_in", "ssd_conv_w", "dn_conv_w")
WHOLE = ("sg_w",)


def _gather_rider(p, l, names):
    shards, axes = [], []
    for n in names:
        s = p[n][l:l + 1]
        s = s.astype(BF) if n in MATMUL_WEIGHTS else s
        shards.append(s[None] if n in SLABBED else s)
        axes.append(0 if n in SLABBED else SHARDED[n])
    return gather_rider(shards, axes)


def _exchange_items(g, p, names):
    items = []
    for n in names:
        local = p[n].shape[1:]
        if n in WHOLE:
            items.append((g[n], 0, -1, local))
        elif n in SLABBED:
            items.append((g[n], 0, 0, local))
        else:
            items.append((g[n], SHARDED[n] - 1, local[SHARDED[n] - 1], local))
    return items


def _use_gathered(w, names, arrays, l):
    for n, arr in zip(names, arrays):
        if n == "w_in":
            w[n] = repack_w_in(arr, f"w_in_repack_{l}")
        elif n == "ssd_conv_w":
            w["ssd_cw"] = arr[0]
        elif n == "dn_conv_w":
            w["dn_cw"] = jnp.moveaxis(arr[:, 0], 0, 1).reshape(4, 3 * BRANCH_W)
        elif n == "gate_b":
            w[n] = arr[0]
        else:
            w[n] = arr


def _layer_weights(p, l):
    w = {}
    w["ssd_cb"] = p["ssd_conv_b"][l][None]
    w["dn_cb"] = jnp.zeros((1, 3 * BRANCH_W), F32)
    w["ssd_ps"] = [_lanes(p["ssd_dt_bias"][l], DT0), _lanes(p["ssd_a_log"][l], DT0), _lanes(p["ssd_d"][l], DT0), p["ssd_norm_w"][l][None]]
    w["dn_ps"] = [_lanes(p["dn_a_log"][l], A0), _lanes(p["dn_dt_bias"][l], A0), p["dn_norm_w"][l][None]]
    w["sg_ps"] = [p["sg_ln_g"][l][None], p["sg_ln_b"][l][None], p["sg_w"][l], jnp.pad(p["sg_b"][l].T, ((0, 0), (0, LANES - 4)))]
    w["fox_ps"] = [_lanes(p["fox_f_bias"][l], FF0)]
    for n in ("ln1_g", "ln1_b", "ln2_g", "ln2_b"):
        w[n] = p[n][l][None]
    return w


def _scan_specs(T, a):
    c0 = lambda c, h: (c, 0)
    ssd = dict(f=ssd_chunk, xs=[(a["z"], (128, 512), c0), (a["xbc_act"], (128, 1024), c0), (a["small"], (128, LANES), c0)],
               ys=[((T, BRANCH_W), (128, BRANCH_W), c0)], state=(4, LANES, LANES), nc=T // 128, nh=1, shared=())
    dr = min(T, DN_ROWS)
    dn = dict(f=dn_chunk, xs=[(a["dn_act"], (dr, 3 * BRANCH_W), c0), (a["dngate"], (dr, BRANCH_W), c0), (a["small"], (dr, LANES), c0)],
              ys=[((T, BRANCH_W), (dr, BRANCH_W), c0)], state=(4, LANES, LANES), nc=T // dr, nh=1, shared=())
    sg = dict(f=sg_chunk, xs=[(a["sguv"], (128, 1024), c0)], ys=[((T, BRANCH_W), (128, BRANCH_W), c0)], state=(1, 8, LANES), nc=T // 128, nh=1, shared=())
    fc = dict(f=foxc_chunk, xs=[(a["small"], (128, LANES), c0)],
              ys=[((T, LANES), (128, LANES), c0)], state=(1, 1, LANES), nc=T // 128, nh=1, shared=())
    return ssd, dn, sg, fc


def _layer_fwd(h, w, l, dn_rider=None, fox_rider=None):
    T = h.shape[0]
    a = {"h": h, **proj_all(h, w["w_in"], f"proj_{l}")}
    a["xbc_act"] = conv_fwd(a["xbc"], w["ssd_cw"], w["ssd_cb"], f"ssd_conv_{l}")
    a["dn_act"] = conv_fwd(a["dnqkv"], w["dn_cw"], w["dn_cb"], f"dn_conv_{l}")
    ssd, dn, sg, fc = _scan_specs(T, a)
    (a["ya"], a["ssd_st"]), _ = scan_fwd(f"ssd_fwd_{l}", ssd["f"], ssd["xs"], w["ssd_ps"], ssd["ys"], ssd["state"], ssd["nc"], ssd["nh"])
    (a["yb"], a["dn_st"]), got = scan_fwd(f"dn_fwd_{l}", dn["f"], dn["xs"], w["dn_ps"], dn["ys"], dn["state"], dn["nc"], dn["nh"], rider=dn_rider)
    _use_gathered(w, EARLY, got, l)
    (a["yc"], a["sg_st"]), _ = scan_fwd(f"sg_fwd_{l}", sg["f"], sg["xs"], w["sg_ps"], sg["ys"], sg["state"], sg["nc"], sg["nh"])
    (a["ccol"], a["fc_st"]), _ = scan_fwd(f"foxc_fwd_{l}", fc["f"], fc["xs"], w["fox_ps"], fc["ys"], fc["state"], fc["nc"], fc["nh"])
    a["fox_qa"], a["fox_ka"], a["fox_va"] = fox_prep(a["foxqkv"], a["ccol"], f"fox_prep_{l}")
    (a["yd"], a["lse"]), carried = fox_fwd(a["fox_qa"], a["fox_ka"], a["fox_va"], f"fox_fwd_{l}", rider=fox_rider)
    a["merged"] = merge_fwd([a["ya"], a["yb"], a["yc"], a["yd"]], a["gates"], w["gate_b"], w["w_branch"], 0, f"merge_fwd_{l}")
    a["u1"], a["h1"] = out_fwd(a["merged"], h, w["w_out"], 0, w["ln1_g"], w["ln1_b"], f"out_fwd_{l}")
    a["u2"], a["h2"] = ff_fwd(a["h1"], w["w_up"], w["w_down"], 0, w["ln2_g"], w["ln2_b"], f"ff_fwd_{l}")
    return a, carried


def _layer_bwd(dh2, a, w, l, p, late_above):
    T = dh2.shape[0]
    g = {}
    du2, dh1, da, r, dg2, db2 = ff_bwd(a["u2"], dh2, a["h1"], w["ln2_g"], w["ln2_b"], w["w_up"], w["w_down"], 0, f"ff_bwd_{l}")
    g["ln2_g"], g["ln2_b"] = dg2[0], db2[0]
    g["w_up"] = matmul_dw(transpose_bf16(a["h1"], f"h1_t_{l}"), da, f"dwup_{l}")
    g["w_down"] = matmul_dw(r, du2, f"dwdown_{l}")
    du1, dmerged, dg1, db1 = out_bwd(a["u1"], dh1, w["ln1_g"], w["ln1_b"], w["w_out"], 0, f"out_bwd_{l}")
    g["ln1_g"], g["ln1_b"] = dg1[0], db1[0]
    g["w_out"] = matmul_dw(transpose_bf16(a["merged"], f"merged_t_{l}"), du1, f"dwout_{l}")
    ys = [a["ya"], a["yb"], a["yc"], a["yd"]]
    dya, dyb, dyc, dyd, dgl, dz, dgb = merge_bwd(ys, a["gates"], w["gate_b"], w["w_branch"], 0, dmerged, f"merge_bwd_{l}")
    g["gate_b"] = dgb
    g["w_branch"] = jnp.stack([matmul_tn(ys[i], dz, f"dwb{i}_{l}", b_col0=i * D_MODEL, n_cols=D_MODEL, out_dtype=BF) for i in range(4)])
    early = exchange_rider(_exchange_items(g, p, EARLY))
    dn_rider = early if late_above is None else exchange_rider(late_above)
    fox_rider = None if late_above is None else early
    ssd, dn, sg, fc = _scan_specs(T, a)
    (dz_ssd, dxbc_act, dsm_ssd, d_dtb, d_alog, d_dsk, d_nw), _ = scan_bwd(f"ssd_bwd_{l}", ssd["f"], ssd["xs"], w["ssd_ps"], ssd["ys"], [dya], a["ssd_st"],
                                                                           ssd["state"], ssd["nc"], ssd["nh"])
    g["ssd_dt_bias"], g["ssd_a_log"], g["ssd_d"], g["ssd_norm_w"] = d_dtb[0, DT0:DT0 + 8], d_alog[0, DT0:DT0 + 8], d_dsk[0, DT0:DT0 + 8], d_nw[0]
    dxbc, g["ssd_conv_w"], dcb = conv_bwd(a["xbc"], w["ssd_cw"], w["ssd_cb"], dxbc_act, f"ssd_conv_bwd_{l}")
    g["ssd_conv_b"] = dcb[0]
    (ddn_act, ddngate, dsm_dn, d_alog, d_dtb, d_nw), got_dn = scan_bwd(f"dn_bwd_{l}", dn["f"], dn["xs"], w["dn_ps"], dn["ys"], [dyb], a["dn_st"],
                                                                        dn["state"], dn["nc"], dn["nh"], rider=dn_rider)
    g["dn_a_log"], g["dn_dt_bias"], g["dn_norm_w"] = d_alog[0, A0:A0 + 4], d_dtb[0, A0:A0 + 4], d_nw[0]
    ddnqkv, g["dn_conv_w"], _ = conv_bwd(a["dnqkv"], w["dn_cw"], w["dn_cb"], ddn_act, f"dn_conv_bwd_{l}")
    (dsguv, d_lng, d_lnb, d_w, d_bt), _ = scan_bwd(f"sg_bwd_{l}", sg["f"], sg["xs"], w["sg_ps"], sg["ys"], [dyc], a["sg_st"], sg["state"], sg["nc"], sg["nh"])
    g["sg_ln_g"], g["sg_ln_b"], g["sg_w"], g["sg_b"] = d_lng[0], d_lnb[0], d_w, d_bt[:, :4].T
    qb, doa, qd, kd = fox_prep_bwd(a["foxqkv"], a["fox_qa"], a["yd"], a["lse"], dyd, f"fox_prep_bwd_{l}")
    (dfq, dfk, dfv, dccol), got_fox = fox_bwd(qb, a["fox_ka"], a["fox_va"], doa, qd, kd, f"fox_bwd_{l}", rider=fox_rider)
    (dsm_fox, d_fb), _ = scan_bwd(f"foxc_bwd_{l}", fc["f"], fc["xs"], w["fox_ps"], fc["ys"], [dccol], a["fc_st"], fc["state"], fc["nc"], fc["nh"])
    g["fox_f_bias"] = d_fb[0, FF0:FF0 + 8]
    dseg = {"z": dz_ssd, "xbc": dxbc, "dnqkv": ddnqkv, "dngate": ddngate, "sguv": dsguv,
            "foxqkv": jnp.concatenate([dfq, dfk, dfv], axis=1), "gates": dgl, "small": add3(dsm_ssd, dsm_dn, dsm_fox, f"dsmall_{l}")}
    h_t = transpose_bf16(a["h"], f"h_t_{l}")
    dwin = {n: matmul_dw(h_t, dseg[n], f"dwin_{n}_{l}") for n in SEG_NAMES}
    g["w_in"] = repack_dw_in(dwin, f"dw_in_repack_{l}")
    g["dn_conv_w"] = jnp.moveaxis(g["dn_conv_w"].reshape(4, N_DEV, 3 * BRANCH_W // N_DEV), 1, 0)
    got = {(EARLY, l): got_dn} if late_above is None else {(LATE + WHOLE, l + 1): got_dn, (EARLY, l): got_fox}
    return dseg, du1, g, got


def proj_all(h, w_in, name):
    T = h.shape[0]
    tm = min(T, 256)
    ns = len(SEG_NAMES)

    def body(*refs):
        h_ref, w_refs, o_refs = refs[0], refs[1:1 + ns], refs[1 + ns:]
        hb = h_ref[...].astype(BF)
        for w_ref, o_ref in zip(w_refs, o_refs):
            o_ref[...] = _dotb(hb, w_ref[...], NN)

    widths = [w_in[n].shape[2] for n in SEG_NAMES]
    in_specs = [_row(tm, D_MODEL)]
    in_specs += [pl.BlockSpec((None,) + w_in[n].shape[1:], lambda i: (0, 0, 0), pipeline_mode=pl.Buffered(1)) for n in SEG_NAMES]
    outs = _pcall(body, grid=(T // tm,), in_specs=in_specs, out_specs=[_row(tm, wd) for wd in widths],
                  out_shape=[_S((T, wd)) for wd in widths], name=name)(h, *[w_in[n] for n in SEG_NAMES])
    return dict(zip(SEG_NAMES, outs))


def dh_all(dseg, w_in, add, name, rider=None):
    T = add.shape[0]
    tm = min(T, 256)
    ns = len(SEG_NAMES)

    def body(*refs):
        d_refs, w_refs, add_ref, o_ref = refs[:ns], refs[ns:2 * ns], refs[2 * ns], refs[2 * ns + 1]
        acc = ALPHA * add_ref[...]
        for d_ref, w_ref in zip(d_refs, w_refs):
            acc = acc + _dotb(d_ref[...].astype(BF), w_ref[...], NT)
        o_ref[...] = acc

    in_specs = [_row(tm, dseg[n].shape[1]) for n in SEG_NAMES]
    in_specs += [pl.BlockSpec((None,) + w_in[n].shape[1:], lambda i: (0, 0, 0), pipeline_mode=pl.Buffered(1)) for n in SEG_NAMES]
    in_specs.append(_row(tm, D_MODEL))
    return hosted_call(body, rider, grid=(T // tm,), in_specs=in_specs, out_specs=[_row(tm, D_MODEL)], out_shape=[_S((T, D_MODEL))],
                       scratch_shapes=[], name=name, args=[*[dseg[n] for n in SEG_NAMES], *[w_in[n] for n in SEG_NAMES], add])


def kernel(x, ln_in_g, ln_in_b, w_in, ssd_conv_w, ssd_conv_b, ssd_dt_bias, ssd_a_log, ssd_d, ssd_norm_w, dn_conv_w, dn_a_log, dn_dt_bias, dn_norm_w, sg_ln_g, sg_ln_b, sg_w, sg_b, fox_f_bias, gate_b, w_branch, w_out, ln1_g, ln1_b, w_up, w_down, ln2_g, ln2_b, loss_target, m_ln_in_g, m_ln_in_b, m_w_in, m_ssd_conv_w, m_ssd_conv_b, m_ssd_dt_bias, m_ssd_a_log, m_ssd_d, m_ssd_norm_w, m_dn_conv_w, m_dn_a_log, m_dn_dt_bias, m_dn_norm_w, m_sg_ln_g, m_sg_ln_b, m_sg_w, m_sg_b, m_fox_f_bias, m_gate_b, m_w_branch, m_w_out, m_ln1_g, m_ln1_b, m_w_up, m_w_down, m_ln2_g, m_ln2_b, v_ln_in_g, v_ln_in_b, v_w_in, v_ssd_conv_w, v_ssd_conv_b, v_ssd_dt_bias, v_ssd_a_log, v_ssd_d, v_ssd_norm_w, v_dn_conv_w, v_dn_a_log, v_dn_dt_bias, v_dn_norm_w, v_sg_ln_g, v_sg_ln_b, v_sg_w, v_sg_b, v_fox_f_bias, v_gate_b, v_w_branch, v_w_out, v_ln1_g, v_ln1_b, v_w_up, v_w_down, v_ln2_g, v_ln2_b):
    args = dict(locals())
    p = {n: args[n] for n in WEIGHTS}
    xt, target = x[0], loss_target[0]
    ws, acts = [_layer_weights(p, l) for l in range(DEPTH)], []
    (h,), gathered = ln_fwd(xt, ln_in_g[None], ln_in_b[None], "ln_in_fwd", rider=_gather_rider(p, 0, LATE))
    _use_gathered(ws[0], LATE, gathered, 0)
    for l in range(DEPTH):
        a, gathered = _layer_fwd(h, ws[l], l, dn_rider=_gather_rider(p, 0, EARLY) if l == 0 else None,
                                 fox_rider=_gather_rider(p, l + 1, LATE + EARLY) if l + 1 < DEPTH else None)
        if l + 1 < DEPTH:
            _use_gathered(ws[l + 1], LATE + EARLY, gathered, l + 1)
        acts.append(a)
        h = a["h2"]
    dh, loss = loss_head(h, target, "loss_head")
    loss = lax.psum(loss[0, 0], ("x", "y", "c"))

    layer_grads, got, late = [None] * DEPTH, {}, None
    for l in reversed(range(DEPTH)):
        dseg, du1, layer_grads[l], got_l = _layer_bwd(dh, acts[l], ws[l], l, p, late)
        got.update(got_l)
        late = _exchange_items(layer_grads[l], p, LATE + WHOLE)
        rider = None
        if l == 0:
            pack = _pack_small([jnp.stack([layer_grads[k][n] for k in range(DEPTH)]) for n in REPLICATED_LAYER], 8)
            rider = exchange_rider(late + [(pack[0], 0, -1, pack.shape[1:])])
        (dh,), carried = dh_all(dseg, ws[l]["w_in"], du1, f"dh_{l}", rider=rider)
    got[(LATE + WHOLE, 0)], got_layer_pack = carried[:-1], carried[-1]
    grad_x, dg_in, db_in = ln_bwd(xt, ln_in_g[None], ln_in_b[None], dh, "ln_in_bwd")
    pack = _pack_small([dg_in[0], db_in[0]], 8)
    got_entry_pack = comm_call(exchange_rider([(pack[0], 0, -1, pack.shape[1:])]), "grads_exchange_entry_norm")[0]
    rcv = {(n, l): arr for (names, l), arrs in got.items() for n, arr in zip(names, arrs)}

    res = [{}, {}, {}, {}]
    for n in tuple(SHARDED) + WHOLE:
        shp = p[n].shape
        lead = math.prod(shp[1:-2])
        to3 = lambda t: t.reshape((-1,) + shp[-2:])
        outs = reduce_adamw([rcv[(n, l)].reshape((N_DEV, lead) + shp[-2:]) for l in range(DEPTH)],
                            to3(p[n]), to3(args["m_" + n]), to3(args["v_" + n]), f"adamw_{n}")
        for k in range(4):
            res[k][n] = outs[k].reshape(shp)
    for names, got_pack, rows, name in ((REPLICATED_LAYER, got_layer_pack, 8, "adamw_replicated"), (REPLICATED_ENTRY, got_entry_pack, 8, "adamw_entry_norm")):
        outs = reduce_adamw([got_pack[:, None]], _pack_small([p[n] for n in names], rows), _pack_small([args["m_" + n] for n in names], rows),
                            _pack_small([args["v_" + n] for n in names], rows), name)
        off = 0
        for n in names:
            shp = p[n].shape
            cnt = math.prod(shp)
            for k in range(4):
                res[k][n] = outs[k].reshape(-1)[off:off + cnt].reshape(shp)
            off += cnt
    return (loss, grad_x[None], *[res[0][n] for n in WEIGHTS], *[res[1][n] for n in WEIGHTS],
            *[res[2][n] for n in WEIGHTS], *[res[3][n] for n in WEIGHTS])
```

```python
import collections
import functools
import math

import jax
import jax.numpy as jnp
from jax import lax
from jax.experimental import pallas as pl
from jax.experimental.pallas import tpu as pltpu

F32 = jnp.float32
BF = jnp.bfloat16

D_MODEL = 1024
DEPTH = 2
BRANCH_W = 512
D_FF = 4096
LN_EPS = 1e-5
NORM_EPS = 1e-6
ALPHA = (2 * DEPTH) ** 0.25
N_DEV = 8
LANES = 128
ADAM_LR, ADAM_B1, ADAM_B2, ADAM_EPS, ADAM_WD, ADAM_STEP = 0.001, 0.9, 0.999, 1e-08, 0.01, 10

DT0, BETA0, A0, FF0 = 0, 8, 12, 16
IN_SIZES = (512, 1024, 8, 1536, 4, 4, 512, 1024, 1536, 8, 4096)
_OFF = [0]
for _s in IN_SIZES:
    _OFF.append(_OFF[-1] + _s)
D_IN = _OFF[-1]
SEGS = (("z", 0), ("xbc", 1), ("dnqkv", 3), ("dngate", 6), ("sguv", 7), ("foxqkv", 8), ("gates", 10))
SMALL_SRC = ((2, DT0), (4, BETA0), (5, A0), (9, FF0))

NN = ((1,), (0,))
NT = ((1,), (1,))
TN = ((0,), (0,))
_DIMS = {"nn": NN, "nt": NT, "tn": TN}


def _pcall(body, **kw):
    return pl.pallas_call(body, **kw)


def _S(shape, dtype=F32):
    return jax.ShapeDtypeStruct(tuple(shape), dtype)


def _iota(shape, dim):
    return lax.broadcasted_iota(jnp.int32, shape, dim)


def _dotb(a, b, dims):
    return lax.dot_general(a, b, (dims, ((), ())), preferred_element_type=F32)


def _split2(a):
    ah = a.astype(BF)
    return ah, (a - ah.astype(F32)).astype(BF)


def _split3(a):
    a1 = a.astype(BF)
    r = a - a1.astype(F32)
    a2 = r.astype(BF)
    a3 = (r - a2.astype(F32)).astype(BF)
    return a1, a2, a3


def _mm_raw(a, b, form, mode):
    d = _DIMS[form]
    if mode == "1":
        return _dotb(a.astype(BF), b.astype(BF), d)
    if mode == "3":
        ah, al = _split2(a)
        bh, bl = _split2(b)
        return _dotb(ah, bh, d) + (_dotb(ah, bl, d) + _dotb(al, bh, d))
    if mode == "xa":
        ab = a.astype(BF)
        b1, b2, b3 = _split3(b)
        return _dotb(ab, b1, d) + (_dotb(ab, b2, d) + _dotb(ab, b3, d))
    bb = b.astype(BF)
    a1, a2, a3 = _split3(a)
    return _dotb(a1, bb, d) + (_dotb(a2, bb, d) + _dotb(a3, bb, d))


@functools.partial(jax.custom_vjp, nondiff_argnums=(2, 3))
def mm(a, b, form, mode):
    return _mm_raw(a, b, form, mode)


def _mm_fwd(a, b, form, mode):
    return _mm_raw(a, b, form, mode), (a, b)


_XA_DB = {"nn": "xa", "nt": "xb", "tn": "xa"}
_XB_DA = {"nn": "xb", "nt": "xb", "tn": "xa"}


def _mm_bwd(form, mode, res, g):
    a, b = res
    ma = _XB_DA[form] if mode == "xb" else mode
    mb = _XA_DB[form] if mode == "xa" else mode
    da = db = None
    if mode != "xa":
        da = {"nn": lambda: mm(g, b, "nt", ma), "nt": lambda: mm(g, b, "nn", ma), "tn": lambda: mm(b, g, "nt", ma)}[form]()
    if mode != "xb":
        db = {"nn": lambda: mm(a, g, "tn", mb), "nt": lambda: mm(g, a, "tn", mb), "tn": lambda: mm(a, g, "nn", mb)}[form]()
    if da is None:
        da = jnp.zeros_like(a)
    if db is None:
        db = jnp.zeros_like(b)
    return da, db


mm.defvjp(_mm_fwd, _mm_bwd)


def _silu(x):
    return x * jax.nn.sigmoid(x)


def _ln(x, g, b):
    mu = jnp.mean(x, -1, keepdims=True)
    xc = x - mu
    var = jnp.mean(xc * xc, -1, keepdims=True)
    return xc * lax.rsqrt(var + LN_EPS) * g + b


def _pick(n, cap):
    if n <= cap:
        return n
    best = LANES
    for t in range(LANES, cap + 1, LANES):
        if n % t == 0:
            best = t
    return best


def transpose_bf16(a, name):
    T, C = a.shape
    tt = min(T, 512)

    def body(a_ref, o_ref):
        o_ref[...] = a_ref[...].T.astype(BF)

    return _pcall(body, grid=(T // tt,), in_specs=[pl.BlockSpec((tt, C), lambda t: (t, 0))], out_specs=pl.BlockSpec((C, tt), lambda t: (0, t)),
                  out_shape=_S((C, T), BF), name=name)(a)


def matmul_dw(a_t, b, name):
    M, K = a_t.shape
    N = b.shape[1]
    tm, tn, tk = min(M, 1024), _pick(N, 1024), _pick(K, 1024)
    nk = K // tk

    def body(a_ref, b_ref, o_ref, acc):
        k = pl.program_id(2)
        p = _dotb(a_ref[...], b_ref[...].astype(BF), NN)

        @pl.when(k == 0)
        def _():
            acc[...] = p

        @pl.when(k > 0)
        def _():
            acc[...] += p

        @pl.when(k == nk - 1)
        def _():
            o_ref[...] = acc[...].astype(BF)

    return _pcall(body, grid=(N // tn, M // tm, nk),
                  in_specs=[pl.BlockSpec((tm, tk), lambda j, i, k: (i, k)), pl.BlockSpec((tk, tn), lambda j, i, k: (k, j))],
                  out_specs=pl.BlockSpec((tm, tn), lambda j, i, k: (i, j)), out_shape=_S((M, N), BF),
                  scratch_shapes=[pltpu.VMEM((tm, tn), F32)], name=name)(a_t, b)


def matmul_tn(a, b, name, b_col0=0, n_cols=None, out_dtype=F32):
    T, M = a.shape
    N = b.shape[1] if n_cols is None else n_cols
    tm, tn, tt = _pick(M, 512), _pick(N, 1024), min(T, 512)
    nt = T // tt
    jb = b_col0 // tn

    def body(a_ref, b_ref, o_ref, acc):
        t = pl.program_id(2)
        p = _dotb(a_ref[...].astype(BF), b_ref[...].astype(BF), TN)

        @pl.when(t == 0)
        def _():
            acc[...] = p

        @pl.when(t > 0)
        def _():
            acc[...] += p

        @pl.when(t == nt - 1)
        def _():
            o_ref[...] = acc[...].astype(out_dtype)

    return _pcall(body, grid=(M // tm, N // tn, nt),
                  in_specs=[pl.BlockSpec((tt, tm), lambda i, j, t: (t, i)), pl.BlockSpec((tt, tn), lambda i, j, t: (t, jb + j))],
                  out_specs=pl.BlockSpec((tm, tn), lambda i, j, t: (i, j)), out_shape=_S((M, N), out_dtype),
                  scratch_shapes=[pltpu.VMEM((tm, tn), F32)], name=name)(a, b)


def _pieces(v):
    if v.ndim == 3:
        return [v[i] for i in range(v.shape[0])]
    n = v.shape[1] // LANES
    if n <= 1:
        return [v]
    return [v[:, i * LANES:(i + 1) * LANES] for i in range(n)]


def _join(ps, like_ndim):
    if like_ndim == 3:
        return jnp.stack(ps, axis=0)
    return ps[0] if len(ps) == 1 else jnp.concatenate(ps, axis=1)


def scan_fwd(name, f, xs, ps, ys, state_shape, nc, nh=1, rider=None):
    nx, npar, ny = len(xs), len(ps), len(ys)

    def body(*refs):
        x_refs, p_refs = refs[:nx], refs[nx:nx + npar]
        y_refs = refs[nx + npar:nx + npar + ny]
        st_out, st = refs[nx + npar + ny], refs[nx + npar + ny + 1]
        c, h = pl.program_id(0), pl.program_id(1)

        @pl.when(c == 0)
        def _():
            st[h] = jnp.zeros(state_shape, F32)

        S = st[h]
        st_out[...] = S
        yv, Sn = f([_pieces(r[...]) for r in x_refs], [_pieces(r[...]) for r in p_refs], _pieces(S), h)
        for r, v in zip(y_refs, yv):
            r[...] = _join(v, 2)
        st[h] = _join(Sn, 3)

    in_specs = [pl.BlockSpec(bs, im) for (_, bs, im) in xs]
    in_specs += [pl.BlockSpec(p.shape, (lambda c, h, n=p.ndim: (0,) * n)) for p in ps]
    out_specs = [pl.BlockSpec(bs, im) for (_, bs, im) in ys]
    out_specs.append(pl.BlockSpec((None, None) + tuple(state_shape), lambda c, h: (c, h, 0, 0, 0)))
    out_shape = [_S(s) for (s, _, _) in ys] + [_S((nc, nh) + tuple(state_shape))]
    return hosted_call(body, rider, grid=(nc, nh), in_specs=in_specs, out_specs=out_specs, out_shape=out_shape,
                       scratch_shapes=[pltpu.VMEM((nh,) + tuple(state_shape), F32)], name=name, args=[*[x[0] for x in xs], *ps])


def scan_bwd(name, f, xs, ps, ys, dys, states, state_shape, nc, nh=1, shared=(), rider=None):
    nx, npar, ny = len(xs), len(ps), len(ys)

    def body(*refs):
        x_refs, p_refs = refs[:nx], refs[nx:nx + npar]
        s_ref = refs[nx + npar]
        dy_refs = refs[nx + npar + 1:nx + npar + 1 + ny]
        o = nx + npar + 1 + ny
        dx_refs, dp_refs, dst = refs[o:o + nx], refs[o + nx:o + nx + npar], refs[o + nx + npar]
        c, h = pl.program_id(0), pl.program_id(1)

        @pl.when(c == 0)
        def _():
            dst[h] = jnp.zeros(state_shape, F32)

        @pl.when((c == 0) & (h == 0))
        def _():
            for r in dp_refs:
                r[...] = jnp.zeros(r.shape, F32)

        xv = [_pieces(r[...]) for r in x_refs]
        pv = [_pieces(r[...]) for r in p_refs]
        _, vjp = jax.vjp(lambda a, b, s: f(a, b, s, h), xv, pv, _pieces(s_ref[...]))
        dxv, dpv, dS = vjp(([_pieces(r[...]) for r in dy_refs], _pieces(dst[h])))
        for i, (r, v) in enumerate(zip(dx_refs, dxv)):
            if i in shared and nh > 1:
                @pl.when(h == 0)
                def _(r=r, v=v):
                    r[...] = _join(v, 2)

                @pl.when(h > 0)
                def _(r=r, v=v):
                    r[...] += _join(v, 2)
            else:
                r[...] = _join(v, 2)
        for r, v in zip(dp_refs, dpv):
            r[...] += _join(v, len(r.shape))
        dst[h] = _join(dS, 3)

    def rev(im):
        return lambda c, h: im(nc - 1 - c, h)

    in_specs = [pl.BlockSpec(bs, rev(im)) for (_, bs, im) in xs]
    in_specs += [pl.BlockSpec(p.shape, (lambda c, h, n=p.ndim: (0,) * n)) for p in ps]
    in_specs.append(pl.BlockSpec((None, None) + tuple(state_shape), lambda c, h: (nc - 1 - c, h, 0, 0, 0)))
    in_specs += [pl.BlockSpec(bs, rev(im)) for (_, bs, im) in ys]
    out_specs = [pl.BlockSpec(bs, rev(im)) for (_, bs, im) in xs]
    out_specs += [pl.BlockSpec(p.shape, (lambda c, h, n=p.ndim: (0,) * n)) for p in ps]
    out_shape = [_S(x[0].shape) for x in xs] + [_S(p.shape) for p in ps]
    return hosted_call(body, rider, grid=(nc, nh), in_specs=in_specs, out_specs=out_specs, out_shape=out_shape,
                       scratch_shapes=[pltpu.VMEM((nh,) + tuple(state_shape), F32)], name=name,
                       args=[*[x[0] for x in xs], *ps, states, *dys])


def _lane():
    return _iota((1, LANES), 1)


def _col(v, idx):
    return jnp.sum(v * (_lane() == idx).astype(F32), axis=1, keepdims=True)


def _last_row(v):
    r = v.shape[0]
    return jnp.sum(v * (_iota((r, 1), 0) == r - 1).astype(F32), axis=0, keepdims=True)


def _tril(n, strict=False):
    r, c = _iota((n, n), 0), _iota((n, n), 1)
    return (r > c) if strict else (r >= c)


def ssd_chunk(xs, ps, S, h):
    zp, xbc, (sm,) = xs
    (bias,), (alog,), (dsk,), nw = ps
    Q = sm.shape[0]
    H = range(8)
    lane = _lane()
    a128 = jnp.where(lane < 8, -jnp.exp(alog), 0.0)
    dtl = jax.nn.softplus(sm + bias)
    tri = _tril(Q)
    cum = mm(tri.astype(F32), dtl * a128, "nn", "xa")
    sel8 = (_iota((8, LANES), 0) == _iota((8, LANES), 1)).astype(F32)
    cum_t = mm(sel8, cum, "nt", "xa")
    m0 = (lane < 64).astype(F32)
    rows0 = (_iota((LANES, 1), 0) < 64).astype(F32)
    me = [m0 if hh % 2 == 0 else 1.0 - m0 for hh in H]
    re = [rows0 if hh % 2 == 0 else 1.0 - rows0 for hh in H]
    Bm, Cm = [xbc[4 + hh // 4] for hh in H], [xbc[6 + hh // 4] for hh in H]
    cb = [mm(xbc[6 + g], xbc[4 + g], "nt", "1") for g in range(2)]
    col = [_col(cum, hh) for hh in H]
    row = [jnp.sum(cum_t * (_iota((8, 1), 0) == hh).astype(F32), axis=0, keepdims=True) for hh in H]
    xh = [xbc[hh // 2] * me[hh] for hh in H]
    xdt = [xh[hh] * _col(dtl, hh) for hh in H]
    seg = [jnp.exp(jnp.where(tri, col[hh] - row[hh], -jnp.inf)) for hh in H]
    last = [_last_row(col[hh]) for hh in H]
    y_diag = [mm(cb[hh // 4] * seg[hh], xdt[hh], "nn", "1") for hh in H]
    y_off = [mm(Cm[hh] * jnp.exp(col[hh]), S[hh // 2], "nt", "1") * me[hh] for hh in H]
    st = [mm(xdt[hh], Bm[hh] * jnp.exp(last[hh] - col[hh]), "tn", "1") for hh in H]
    y = [y_diag[hh] + y_off[hh] + _col(dsk, hh) * xh[hh] for hh in H]
    Sn = [S[pr] * (jnp.exp(last[2 * pr]) * re[0] + jnp.exp(last[2 * pr + 1]) * re[1]) + st[2 * pr] + st[2 * pr + 1] for pr in range(4)]
    yz = [(y[2 * pr] + y[2 * pr + 1]) * _silu(zp[pr]) for pr in range(4)]
    ssq = sum(jnp.sum(v * v, axis=1, keepdims=True) for v in yz)
    scale = lax.rsqrt(ssq / BRANCH_W + NORM_EPS)
    return [[yz[i] * scale * nw[i] for i in range(4)]], Sn


@jax.custom_vjp
def _halves(x):
    r = x.shape[0] // 2
    return x[:r], x[r:]


_halves.defvjp(lambda x: (_halves(x), None), lambda _, g: (jnp.concatenate(g, axis=0),))
DN_CHUNK = 64
DN_ROWS = 256


def _chunks(x):
    if x.shape[0] == DN_CHUNK:
        return [x]
    a, b = _halves(x)
    return _chunks(a) + _chunks(b)


def dn_chunk(xs, ps, S, h):
    act, gate, (sm,) = xs
    (alog,), (dtb,), (nw,) = ps
    C = DN_CHUNK
    n = sm.shape[0] // C
    I = range(4 * n)
    lane = _lane()
    tri, strict = _tril(C), _tril(C, True)
    sm2 = _chunks(sm)
    G = [jnp.where((lane >= A0) & (lane < A0 + 4), -jnp.exp(alog) * jax.nn.softplus(s + dtb), 0.0) for s in sm2]
    gcs = [mm(tri.astype(F32), g, "nn", "xa") for g in G]
    sig = [jax.nn.sigmoid(s) for s in sm2]
    parts = [_chunks(x) for x in act]
    q, k, v = ([parts[o + i % 4][i // 4] for i in I] for o in (0, 4, 8))
    gt = [_chunks(x) for x in gate]
    qn = [q[i] * lax.rsqrt(jnp.sum(q[i] * q[i], axis=1, keepdims=True) + NORM_EPS) * (LANES ** -0.5) for i in I]
    kn = [k[i] * lax.rsqrt(jnp.sum(k[i] * k[i], axis=1, keepdims=True) + NORM_EPS) for i in I]
    beta = [_col(sig[i // 4], BETA0 + i % 4) for i in I]
    gcol = [_col(gcs[i // 4], A0 + i % 4) for i in I]
    selr = [((_iota((8, LANES), 0) == 0) & (_iota((8, LANES), 1) == A0 + h)).astype(F32) for h in range(4)]
    grow = [jnp.sum(mm(selr[i % 4], gcs[i // 4], "nt", "xa"), axis=0, keepdims=True) for i in I]
    gamma = [jnp.exp(jnp.where(tri, gcol[i] - grow[i], -jnp.inf)) for i in I]
    kb = [kn[i] * beta[i] for i in I]
    pk = [-(mm(kb[i], kn[i], "nt", "1") * jnp.where(strict, gamma[i], 0.0)) for i in I]
    eye = (_iota((C, C), 0) == _iota((C, C), 1)).astype(F32)
    minv = [eye + pk[i] for i in I]
    for _ in range(5):
        pk = [mm(pk[i], pk[i], "nn", "3") for i in I]
        minv = [minv[i] + mm(minv[i], pk[i], "nn", "3") for i in I]
    eg = [jnp.exp(gcol[i]) for i in I]
    w = [mm(minv[i], kb[i] * eg[i], "nn", "3") for i in I]
    u = [mm(minv[i], v[i] * beta[i], "nn", "3") for i in I]
    glast = [_last_row(gcol[i]) for i in I]
    qg = [qn[i] * eg[i] for i in I]
    qk = [mm(qn[i], kn[i], "nt", "1") * gamma[i] for i in I]
    kdec = [kn[i] * jnp.exp(glast[i] - gcol[i]) for i in I]
    y = []
    for c in range(n):
        J = range(4 * c, 4 * c + 4)
        vnew = [u[i] - mm(w[i], S[i % 4], "nn", "1") for i in J]
        o = [mm(qg[i], S[i % 4], "nn", "1") + mm(qk[i], vn, "nn", "1") for i, vn in zip(J, vnew)]
        S = [S[i % 4] * jnp.exp(glast[i]) + mm(kdec[i], vn, "tn", "1") for i, vn in zip(J, vnew)]
        on = [x * lax.rsqrt(jnp.mean(x * x, axis=1, keepdims=True) + NORM_EPS) * nw for x in o]
        y.append([on[h] * _silu(gt[h][c]) for h in range(4)])
    return [[jnp.concatenate([y[c][h] for c in range(n)], axis=0) for h in range(4)]], S


def sg_chunk(xs, ps, S, h):
    (uv,) = xs
    lng, lnb, W, (bt,) = ps
    u = [jax.nn.gelu(p) for p in uv[:4]]
    v = [jax.nn.gelu(p) for p in uv[4:]]
    mu = sum(jnp.sum(p, axis=1, keepdims=True) for p in v) / BRANCH_W
    vc = [p - mu for p in v]
    var = sum(jnp.sum(p * p, axis=1, keepdims=True) for p in vc) / BRANCH_W
    inv = lax.rsqrt(var + LN_EPS)
    trif = _tril(W[0].shape[0]).astype(F32)
    out = []
    for g in range(4):
        vn = vc[g] * inv * lng[g] + lnb[g]
        out.append(u[g] * (mm(W[g] * trif, vn, "nn", "1") + _col(bt, g)))
    return [out], S


def foxc_chunk(xs, ps, S, h):
    (sm,), ((fb,),), (carry,) = xs[0], ps, S
    lane = _lane()
    ls = jnp.where((lane >= FF0) & (lane < FF0 + 8), jax.nn.log_sigmoid(sm + fb), 0.0)
    c = mm(_tril(sm.shape[0]).astype(F32), ls, "nn", "xa") + carry
    return [[c]], [_last_row(c)]


HALO = 8


def _conv_tiles(T, C):
    return min(T, 512), _pick(C, 512)


def conv_fwd(x, w, b, name):
    T, C = x.shape
    tm, cb = _conv_tiles(T, C)

    def body(xp_ref, x_ref, w_ref, b_ref, o_ref):
        i = pl.program_id(1)
        e = jnp.concatenate([xp_ref[...] * (i > 0).astype(F32), x_ref[...]], axis=0)
        pre = b_ref[...] + sum(w_ref[k:k + 1, :] * e[5 + k:5 + k + tm, :] for k in range(4))
        o_ref[...] = _silu(pre)

    hb = tm // HALO
    return _pcall(body, grid=(C // cb, T // tm),
                  in_specs=[pl.BlockSpec((HALO, cb), lambda j, i: (jnp.maximum(i * hb - 1, 0), j)), pl.BlockSpec((tm, cb), lambda j, i: (i, j)),
                            pl.BlockSpec((4, cb), lambda j, i: (0, j)), pl.BlockSpec((1, cb), lambda j, i: (0, j))],
                  out_specs=pl.BlockSpec((tm, cb), lambda j, i: (i, j)), out_shape=_S((T, C)), name=name)(x, x, w, b)


def conv_bwd(x, w, b, dact, name):
    T, C = x.shape
    tm, cb = _conv_tiles(T, C)
    nt = T // tm

    def body(xp_ref, x_ref, xn_ref, w_ref, b_ref, d_ref, dn_ref, dx_ref, dw_ref, db_ref):
        i = pl.program_id(1)
        has_prev, has_next = (i > 0).astype(F32), (i < nt - 1).astype(F32)
        e = jnp.concatenate([xp_ref[...] * has_prev, x_ref[...], xn_ref[...] * has_next], axis=0)
        pre = b_ref[...] + sum(w_ref[k:k + 1, :] * e[5 + k:5 + k + tm + 8, :] for k in range(4))
        de = jnp.concatenate([d_ref[...], dn_ref[...] * has_next], axis=0)
        sg = jax.nn.sigmoid(pre)
        dpre = de * (sg * (1.0 + pre * (1.0 - sg)))
        dx_ref[...] = sum(w_ref[k:k + 1, :] * dpre[3 - k:3 - k + tm, :] for k in range(4))
        dcur = dpre[0:tm, :]
        dw = jnp.concatenate([jnp.sum(dcur * e[5 + k:5 + k + tm, :], axis=0, keepdims=True) for k in range(4)], axis=0)
        db = jnp.sum(dcur, axis=0, keepdims=True)

        @pl.when(i == 0)
        def _():
            dw_ref[...] = dw
            db_ref[...] = db

        @pl.when(i > 0)
        def _():
            dw_ref[...] += dw
            db_ref[...] += db

    blk = lambda f: pl.BlockSpec((tm, cb), f)
    hb = tm // HALO
    before = pl.BlockSpec((HALO, cb), lambda j, i: (jnp.maximum(i * hb - 1, 0), j))
    after = pl.BlockSpec((HALO, cb), lambda j, i: (jnp.minimum((i + 1) * hb, nt * hb - 1), j))
    return _pcall(body, grid=(C // cb, nt),
                  in_specs=[before, blk(lambda j, i: (i, j)), after,
                            pl.BlockSpec((4, cb), lambda j, i: (0, j)), pl.BlockSpec((1, cb), lambda j, i: (0, j)),
                            blk(lambda j, i: (i, j)), after],
                  out_specs=[blk(lambda j, i: (i, j)), pl.BlockSpec((4, cb), lambda j, i: (0, j)), pl.BlockSpec((1, cb), lambda j, i: (0, j))],
                  out_shape=[_S((T, C)), _S((4, C)), _S((1, C))], name=name)(x, x, x, w, b, dact, dact)


FOX_SCALE = 64 ** -0.5
LOG2E = 1.4426950408889634


def _spare(e, i):
    return (_lane() == 64 * (1 - e) + i).astype(F32)


def _lanes_of(e):
    lane = _lane()
    return ((lane < 64) if e == 0 else (lane >= 64)).astype(F32)


def _col3(col, e, first):
    c1 = col.astype(BF).astype(F32)
    c2 = (col - c1).astype(BF).astype(F32)
    c3 = (col - c1 - c2).astype(BF).astype(F32)
    return c1 * _spare(e, first) + c2 * _spare(e, first + 1) + c3 * _spare(e, first + 2)


def _ones3(e, first):
    return _spare(e, first) + _spare(e, first + 1) + _spare(e, first + 2)


def _causal_bias(n):
    return jnp.where(_iota((n, n), 0) >= _iota((n, n), 1), 0.0, -jnp.inf).astype(F32)


def _c_col(cc, hh):
    return jnp.sum(cc * (_lane() == FF0 + hh).astype(F32), axis=1, keepdims=True) * LOG2E


def _pair_spec(tq, row_of):
    return pl.BlockSpec((None, 2, tq, LANES), lambda hp, a, b: (hp, 0, row_of(a, b), 0))


def fox_prep(qkv, ccol, name):
    T = qkv.shape[0]
    tq = min(T, 512)

    def body(q_ref, k_ref, v_ref, cc_ref, qa_ref, ka_ref, va_ref):
        hp = pl.program_id(0)
        q, k, v, cc = q_ref[...], k_ref[...], v_ref[...], cc_ref[...]
        for e in range(2):
            me = _lanes_of(e)
            c2 = _c_col(cc, 2 * hp + e)
            qa_ref[e] = (q * me * (FOX_SCALE * LOG2E) + _col3(c2, e, 0) + _ones3(e, 3)).astype(BF)
            ka_ref[e] = (k * me + _ones3(e, 0) + _col3(-c2, e, 3) + _ones3(e, 6)).astype(BF)
            va_ref[e] = (v * me + (1.0 - me)).astype(BF)

    blk = lambda off: pl.BlockSpec((tq, LANES), lambda hp, i: (i, off + hp))
    out = pl.BlockSpec((None, 2, tq, LANES), lambda hp, i: (hp, 0, i, 0))
    return _pcall(body, grid=(4, T // tq), in_specs=[blk(0), blk(4), blk(8), pl.BlockSpec((tq, LANES), lambda hp, i: (i, 0))],
                  out_specs=[out] * 3, out_shape=[_S((4, 2, T, LANES), BF)] * 3, name=name)(qkv, qkv, qkv, ccol)


def fox_fwd(qa, ka, va, name, rider=None):
    T = qa.shape[2]
    tq = min(T, 512)
    nq = T // tq

    def body(qa_ref, ka_ref, va_ref, o_ref, lse_ref, m_s, acc, causal):
        i, j = pl.program_id(1), pl.program_id(2)

        @pl.when((pl.program_id(0) == 0) & (i == 0) & (j == 0))
        def _():
            causal[...] = _causal_bias(tq)

        @pl.when(j == 0)
        def _():
            m_s[...] = jnp.full(m_s.shape, -jnp.inf, F32)
            acc[...] = jnp.zeros(acc.shape, F32)

        def step(diagonal):
            for e in range(2):
                s = _dotb(qa_ref[e], ka_ref[e], NT)
                if diagonal:
                    s = s + causal[...]
                m_old = m_s[e]
                m_new = jnp.maximum(m_old, jnp.max(s, axis=1, keepdims=True))
                p = jnp.exp2(s - m_new)
                m_s[e] = m_new
                acc[e] = acc[e] * jnp.exp2(m_old - m_new) + _dotb(p.astype(BF), va_ref[e], NN)

        @pl.when(j < i)
        def _():
            step(False)

        @pl.when(j == i)
        def _():
            step(True)
            lane = _lane()
            o, lse = 0.0, 0.0
            for e in range(2):
                l = jnp.sum(acc[e] * _spare(e, 0), axis=1, keepdims=True)
                o = o + acc[e] * _lanes_of(e) / l
                lse = lse + (m_s[e] + jnp.log2(l)) * (lane == e).astype(F32)
            o_ref[...] = o
            lse_ref[...] = lse

    kv = _pair_spec(tq, lambda i, j: jnp.minimum(j, i))
    return hosted_call(body, rider, grid=(4, nq, nq), in_specs=[_pair_spec(tq, lambda i, j: i), kv, kv],
                       out_specs=[pl.BlockSpec((tq, LANES), lambda hp, i, j: (i, hp)), pl.BlockSpec((None, tq, LANES), lambda hp, i, j: (hp, i, 0))],
                       out_shape=[_S((T, BRANCH_W)), _S((4, T, LANES))],
                       scratch_shapes=[pltpu.VMEM((2, tq, 1), F32), pltpu.VMEM((2, tq, LANES), F32), pltpu.VMEM((tq, tq), F32)],
                       name=name, args=[qa, ka, va])


def fox_prep_bwd(qkv, qa, o, lse, do, name):
    T = qkv.shape[0]
    tq = min(T, 512)

    def body(q_ref, k_ref, qa_ref, o_ref, lse_ref, do_ref, qb_ref, doa_ref, qd_ref, kd_ref):
        q, k, dov = q_ref[...], k_ref[...], do_ref[...]
        dd = dov * o_ref[...]
        lane = _lane()
        for e in range(2):
            me = _lanes_of(e)
            lse_e = jnp.sum(lse_ref[...] * (lane == e).astype(F32), axis=1, keepdims=True)
            qb_ref[e] = (qa_ref[e].astype(F32) + _col3(-lse_e, e, 6)).astype(BF)
            doa_ref[e] = (dov * me + _col3(-jnp.sum(dd * me, axis=1, keepdims=True), e, 0)).astype(BF)
            qd_ref[e] = (q * me * FOX_SCALE + _spare(e, 0)).astype(BF)
            kd_ref[e] = (k * me * FOX_SCALE + _spare(e, 0)).astype(BF)

    blk = lambda off: pl.BlockSpec((tq, LANES), lambda hp, i: (i, off + hp))
    pair = pl.BlockSpec((None, 2, tq, LANES), lambda hp, i: (hp, 0, i, 0))
    return _pcall(body, grid=(4, T // tq),
                  in_specs=[blk(0), blk(4), pair, blk(0), pl.BlockSpec((None, tq, LANES), lambda hp, i: (hp, i, 0)), blk(0)],
                  out_specs=[pair] * 4, out_shape=[_S((4, 2, T, LANES), BF)] * 4, name=name)(qkv, qkv, qa, o, lse, do)


def fox_bwd(qb, ka, va, doa, qd, kd, name, rider=None):
    T = qb.shape[2]
    tq = min(T, 512)
    nq = T // tq

    def body(qb_ref, ka_ref, va_ref, doa_ref, qd_ref, kd_ref, dq_ref, dk_ref, dv_ref, dcc_ref, dq_s, dk_s, dv_s, causal):
        hp, j, ii = pl.program_id(0), pl.program_id(1), pl.program_id(2)
        i = jnp.maximum(ii, j)

        @pl.when((hp == 0) & (j == 0) & (ii == 0))
        def _():
            dcc_ref[...] = jnp.zeros(dcc_ref.shape, F32)
            causal[...] = _causal_bias(tq)

        @pl.when((j == 0) & (ii == 0))
        def _():
            dq_s[...] = jnp.zeros(dq_s.shape, F32)

        @pl.when(ii == 0)
        def _():
            dk_s[...] = jnp.zeros(dk_s.shape, F32)
            dv_s[...] = jnp.zeros(dv_s.shape, F32)

        def step(diagonal):
            rows = pl.ds(pl.multiple_of(i * tq, tq), tq)
            for e in range(2):
                s = _dotb(qb_ref[e], ka_ref[e], NT)
                if diagonal:
                    s = s + causal[...]
                p = jnp.exp2(s)
                ds = (p * _dotb(doa_ref[e], va_ref[e], NT)).astype(BF)
                dv_s[e] += _dotb(p.astype(BF), doa_ref[e], TN)
                dq_s[e, rows, :] += _dotb(ds, kd_ref[e], NN)
                dk_s[e] += _dotb(ds, qd_ref[e], TN)

        @pl.when(ii > j)
        def _():
            step(False)

        @pl.when(ii == j)
        def _():
            step(True)

        def fold(acc, sign):
            grad, dc = 0.0, 0.0
            for e in range(2):
                a = acc[e]
                grad = grad + a * _lanes_of(e)
                dc = dc + sign * jnp.sum(a * _spare(e, 0), axis=1, keepdims=True) * (_lane() == FF0 + 2 * hp + e).astype(F32)
            return grad, dc

        @pl.when(ii == nq - 1)
        def _():
            grad, dc = fold(dk_s, -1.0)
            dk_ref[...] = grad
            dv_ref[...] = dv_s[0] * _lanes_of(0) + dv_s[1] * _lanes_of(1)
            dcc_ref[pl.ds(pl.multiple_of(j * tq, tq), tq), :] += dc

        @pl.when((j == nq - 1) & (ii == nq - 1))
        def _():
            grad, dc = fold(dq_s, 1.0)
            dq_ref[...] = grad
            dcc_ref[...] += dc

    irow, jrow = _pair_spec(tq, lambda j, ii: jnp.maximum(ii, j)), _pair_spec(tq, lambda j, ii: j)
    jout = pl.BlockSpec((tq, LANES), lambda hp, j, ii: (j, hp))
    return hosted_call(body, rider, grid=(4, nq, nq), in_specs=[irow, jrow, jrow, irow, irow, jrow],
                       out_specs=[pl.BlockSpec((T, LANES), lambda hp, j, ii: (0, hp)), jout, jout, pl.BlockSpec((T, LANES), lambda hp, j, ii: (0, 0))],
                       out_shape=[_S((T, BRANCH_W)), _S((T, BRANCH_W)), _S((T, BRANCH_W)), _S((T, LANES))],
                       scratch_shapes=[pltpu.VMEM((2, T, LANES), F32), pltpu.VMEM((2, tq, LANES), F32), pltpu.VMEM((2, tq, LANES), F32),
                                       pltpu.VMEM((tq, tq), F32)],
                       name=name, args=[qb, ka, va, doa, qd, kd])


def _acc_out(ref, val, first):
    @pl.when(first)
    def _():
        ref[...] = val

    @pl.when(jnp.logical_not(first))
    def _():
        ref[...] += val


def _row(tm, c):
    return pl.BlockSpec((tm, c), lambda i: (i, 0))


def _full(shape):
    return pl.BlockSpec(shape, lambda *_: (0,) * len(shape))


def ln_fwd(x, g, b, name, rider=None):
    T, C = x.shape
    tm = min(T, 512)

    def body(x_ref, g_ref, b_ref, o_ref):
        o_ref[...] = _ln(x_ref[...], g_ref[...], b_ref[...])

    return hosted_call(body, rider, grid=(T // tm,), in_specs=[_row(tm, C), _full((1, C)), _full((1, C))], out_specs=[_row(tm, C)],
                       out_shape=[_S((T, C))], scratch_shapes=[], name=name, args=[x, g, b])


def loss_head(h, target, name):
    T, C = h.shape
    tm = min(T, 512)

    def body(h_ref, t_ref, d_ref, l_ref):
        e = h_ref[...] - t_ref[...]
        d_ref[...] = e * (1.0 / C)
        part = jnp.sum(jnp.sum(e * e, axis=1, keepdims=True), axis=0, keepdims=True) * (0.5 / C)
        _acc_out(l_ref, part, pl.program_id(0) == 0)

    return _pcall(body, grid=(T // tm,), in_specs=[_row(tm, C), _row(tm, C)], out_specs=[_row(tm, C), _full((1, 1))],
                  out_shape=[_S((T, C)), _S((1, 1))], name=name)(h, target)


def add3(a, b, c, name):
    T, C = a.shape
    tm = min(T, 512)

    def body(a_ref, b_ref, c_ref, o_ref):
        o_ref[...] = a_ref[...] + b_ref[...] + c_ref[...]

    return _pcall(body, grid=(T // tm,), in_specs=[_row(tm, C)] * 3, out_specs=_row(tm, C), out_shape=_S((T, C)), name=name)(a, b, c)


def _wb_spec(l):
    return pl.BlockSpec((None, 4, BRANCH_W, D_MODEL), lambda *_: (l, 0, 0, 0))


def merge_fwd(ys, gl, gb, wb, l, name):
    T = gl.shape[0]
    tm = min(T, 256)

    def body(y0, y1, y2, y3, gl_ref, gb_ref, wb_ref, o_ref):
        acc = 0.0
        for i, y in enumerate((y0, y1, y2, y3)):
            z = _dotb(y[...].astype(BF), wb_ref[i], NN)
            g = jax.nn.sigmoid(gl_ref[:, i * D_MODEL:(i + 1) * D_MODEL] + gb_ref[i:i + 1, :])
            acc = acc + g * z
        o_ref[...] = acc

    return _pcall(body, grid=(T // tm,), in_specs=[_row(tm, BRANCH_W)] * 4 + [_row(tm, 4 * D_MODEL), _full((4, D_MODEL)), _wb_spec(l)],
                  out_specs=_row(tm, D_MODEL), out_shape=_S((T, D_MODEL)), name=name)(*ys, gl, gb, wb)


def merge_bwd(ys, gl, gb, wb, l, dm, name):
    T = gl.shape[0]
    tm = min(T, 256)

    def body(y0, y1, y2, y3, gl_ref, gb_ref, wb_ref, dm_ref, d0, d1, d2, d3, dgl_ref, dz_ref, dgb_ref):
        dmv = dm_ref[...]
        first = pl.program_id(0) == 0
        for i, (y, d) in enumerate(zip((y0, y1, y2, y3), (d0, d1, d2, d3))):
            cols = slice(i * D_MODEL, (i + 1) * D_MODEL)
            z = _dotb(y[...].astype(BF), wb_ref[i], NN)
            g = jax.nn.sigmoid(gl_ref[:, cols] + gb_ref[i:i + 1, :])
            dgl = dmv * z * (g * (1.0 - g))
            dz = (g * dmv).astype(BF)
            dgl_ref[:, cols] = dgl
            dz_ref[:, cols] = dz
            d[...] = _dotb(dz, wb_ref[i], NT)
            _acc_out(dgb_ref.at[i:i + 1, :], jnp.sum(dgl, axis=0, keepdims=True), first)

    return _pcall(body, grid=(T // tm,),
                  in_specs=[_row(tm, BRANCH_W)] * 4 + [_row(tm, 4 * D_MODEL), _full((4, D_MODEL)), _wb_spec(l), _row(tm, D_MODEL)],
                  out_specs=[_row(tm, BRANCH_W)] * 4 + [_row(tm, 4 * D_MODEL), _row(tm, 4 * D_MODEL), _full((4, D_MODEL))],
                  out_shape=[_S((T, BRANCH_W))] * 4 + [_S((T, 4 * D_MODEL)), _S((T, 4 * D_MODEL), BF), _S((4, D_MODEL))], name=name)(
                      *ys, gl, gb, wb, dm)


def _wout_spec(l):
    return pl.BlockSpec((None, D_MODEL, D_MODEL), lambda *_: (l, 0, 0))


def out_fwd(merged, h, wout, l, g, b, name):
    T = h.shape[0]
    tm = min(T, 512)

    def body(m_ref, h_ref, w_ref, g_ref, b_ref, u_ref, o_ref):
        u = ALPHA * h_ref[...] + _dotb(m_ref[...].astype(BF), w_ref[...], NN)
        u_ref[...] = u
        o_ref[...] = _ln(u, g_ref[...], b_ref[...])

    C = D_MODEL
    return _pcall(body, grid=(T // tm,), in_specs=[_row(tm, C), _row(tm, C), _wout_spec(l), _full((1, C)), _full((1, C))],
                  out_specs=[_row(tm, C), _row(tm, C)], out_shape=[_S((T, C)), _S((T, C))], name=name)(merged, h, wout, g, b)


def out_bwd(u, dy, g, b, wout, l, name):
    T, C = u.shape
    tm = min(T, 512)

    def body(u_ref, dy_ref, g_ref, b_ref, w_ref, du_ref, dm_ref, dg_ref, db_ref):
        _, vjp = jax.vjp(_ln, u_ref[...], g_ref[...], b_ref[...])
        du, dg, db = vjp(dy_ref[...])
        du_ref[...] = du
        dm_ref[...] = _dotb(du.astype(BF), w_ref[...], NT)
        first = pl.program_id(0) == 0
        _acc_out(dg_ref, dg, first)
        _acc_out(db_ref, db, first)

    return _pcall(body, grid=(T // tm,), in_specs=[_row(tm, C), _row(tm, C), _full((1, C)), _full((1, C)), _wout_spec(l)],
                  out_specs=[_row(tm, C), _row(tm, C), _full((1, C)), _full((1, C))],
                  out_shape=[_S((T, C)), _S((T, C)), _S((1, C)), _S((1, C))], name=name)(u, dy, g, b, wout)


def ff_fwd(h, wup, wdown, l, g, b, name):
    T, C = h.shape
    F = wup.shape[2]
    tm, tf = min(T, 1024), 1024
    nf = F // tf

    def body(h_ref, wu_ref, wd_ref, g_ref, b_ref, u_ref, o_ref, acc):
        f = pl.program_id(1)
        a = _dotb(h_ref[...].astype(BF), wu_ref[...], NN)
        r = jnp.square(jnp.maximum(a, 0.0))
        p = _dotb(r.astype(BF), wd_ref[...], NN)
        _acc_out(acc, p, f == 0)

        @pl.when(f == nf - 1)
        def _():
            u = ALPHA * h_ref[...] + acc[...]
            u_ref[...] = u
            o_ref[...] = _ln(u, g_ref[...], b_ref[...])

    row = pl.BlockSpec((tm, C), lambda i, f: (i, 0))
    return _pcall(body, grid=(T // tm, nf),
                  in_specs=[row, pl.BlockSpec((None, C, tf), lambda i, f: (l, 0, f)), pl.BlockSpec((None, tf, C), lambda i, f: (l, f, 0)),
                            _full((1, C)), _full((1, C))],
                  out_specs=[row, row], out_shape=[_S((T, C)), _S((T, C))], scratch_shapes=[pltpu.VMEM((tm, C), F32)], name=name)(h, wup, wdown, g, b)


def ff_bwd(u, dy, h, g, b, wup, wdown, l, name):
    T, C = h.shape
    F = wup.shape[2]
    tm, tf = min(T, 512), 1024
    nf = F // tf

    def body(u_ref, dy_ref, h_ref, g_ref, b_ref, wu_ref, wd_ref, du_ref, dh_ref, da_ref, r_ref, dg_ref, db_ref, du_s, acc):
        i, f = pl.program_id(0), pl.program_id(1)

        @pl.when(f == 0)
        def _():
            _, vjp = jax.vjp(_ln, u_ref[...], g_ref[...], b_ref[...])
            du, dg, db = vjp(dy_ref[...])
            du_s[...] = du
            du_ref[...] = du
            _acc_out(dg_ref, dg, i == 0)
            _acc_out(db_ref, db, i == 0)

        a = _dotb(h_ref[...].astype(BF), wu_ref[...], NN)
        ap = jnp.maximum(a, 0.0)
        dr = _dotb(du_s[...].astype(BF), wd_ref[...], NT)
        da = (dr * (2.0 * ap)).astype(BF)
        da_ref[...] = da
        r_ref[...] = jnp.square(ap).T.astype(BF)
        _acc_out(acc, _dotb(da, wu_ref[...], NT), f == 0)

        @pl.when(f == nf - 1)
        def _():
            dh_ref[...] = ALPHA * du_s[...] + acc[...]

    row = pl.BlockSpec((tm, C), lambda i, f: (i, 0))
    colf = pl.BlockSpec((tm, tf), lambda i, f: (i, f))
    return _pcall(body, grid=(T // tm, nf),
                  in_specs=[row, row, row, _full((1, C)), _full((1, C)), pl.BlockSpec((None, C, tf), lambda i, f: (l, 0, f)),
                            pl.BlockSpec((None, tf, C), lambda i, f: (l, f, 0))],
                  out_specs=[row, row, colf, pl.BlockSpec((tf, tm), lambda i, f: (f, i)), _full((1, C)), _full((1, C))],
                  out_shape=[_S((T, C)), _S((T, C)), _S((T, F), BF), _S((F, T), BF), _S((1, C)), _S((1, C))],
                  scratch_shapes=[pltpu.VMEM((tm, C), F32), pltpu.VMEM((tm, C), F32)], name=name)(u, dy, h, g, b, wup, wdown)


MESH_ID = pl.DeviceIdType.MESH
_ANY = pl.BlockSpec(memory_space=pl.ANY)


def _window(ref, ax, idx, n):
    if n < 0:
        return ref
    sel = idx if n == 0 else pl.ds(pl.multiple_of(idx * n, n), n)
    return ref.at[(slice(None),) * ax + (sel,)]


Rider = collections.namedtuple("Rider", "operands out_shape scratch start wait")


def hosted_call(body, rider, *, grid, in_specs, out_specs, out_shape, scratch_shapes, name, args):
    n_in, n_out, n_scr = len(in_specs), len(out_specs), len(scratch_shapes)
    if rider is None:
        return _pcall(body, grid=grid, in_specs=in_specs, out_specs=out_specs, out_shape=out_shape, scratch_shapes=scratch_shapes, name=name)(*args), []
    ri, ro = len(rider.operands), len(rider.out_shape)

    def wrapped(*refs):
        ins, r_in = refs[:n_in], refs[n_in:n_in + ri]
        o0 = n_in + ri
        outs, r_out = refs[o0:o0 + n_out], refs[o0 + n_out:o0 + n_out + ro]
        s0 = o0 + n_out + ro
        scr, r_scr = refs[s0:s0 + n_scr], refs[s0 + n_scr:]
        ids = [pl.program_id(i) for i in range(len(grid))]
        first = functools.reduce(jnp.logical_and, [i == 0 for i in ids])
        last = functools.reduce(jnp.logical_and, [i == g - 1 for i, g in zip(ids, grid)])

        @pl.when(first)
        def _():
            rider.start(r_in, r_out, r_scr)

        body(*ins, *outs, *scr)

        @pl.when(last)
        def _():
            rider.wait(r_in, r_out, r_scr)

    res = _pcall(wrapped, grid=grid, in_specs=list(in_specs) + [_ANY] * ri, out_specs=list(out_specs) + [_ANY] * ro,
                 out_shape=list(out_shape) + list(rider.out_shape), scratch_shapes=list(scratch_shapes) + list(rider.scratch),
                 name=name)(*args, *rider.operands)
    return res[:n_out], res[n_out:]


def comm_call(rider, name):
    ri = len(rider.operands)

    def body(*refs):
        r_in, r_out, r_scr = refs[:ri], refs[ri:ri + len(rider.out_shape)], refs[ri + len(rider.out_shape):]
        rider.start(r_in, r_out, r_scr)
        rider.wait(r_in, r_out, r_scr)

    return _pcall(body, in_specs=[_ANY] * ri, out_specs=[_ANY] * len(rider.out_shape), out_shape=list(rider.out_shape),
                  scratch_shapes=list(rider.scratch), name=name)(*rider.operands)


def gather_rider(shards, axes):
    K = len(shards)
    widths = [s.shape[a] for s, a in zip(shards, axes)]
    out_shape = [_S(s.shape[:a] + (N_DEV * s.shape[a],) + s.shape[a + 1:], s.dtype) for s, a in zip(shards, axes)]

    def plan(x_refs, o_refs, sems):
        send_sems, recv_sems, local_sems = sems
        mx, my, mc = lax.axis_index("x"), lax.axis_index("y"), lax.axis_index("c")
        me, sibling = (mx, my, mc), (mx, my, 1 - mc)
        chips = [(1 - mx, my), (mx, 1 - my), (1 - mx, 1 - my)]

        def win(k, px, py, pc):
            return _window(o_refs[k], axes[k], 4 * px + 2 * py + pc, widths[k])

        def copy(k, slot, block, to, src=None):
            return pltpu.make_async_remote_copy(src_ref=win(k, *block) if src is None else src, dst_ref=win(k, *block),
                                                send_sem=send_sems.at[7 * k + slot], recv_sem=recv_sems.at[7 * k + slot],
                                                device_id=to, device_id_type=MESH_ID)

        mine = [pltpu.make_async_copy(x_refs[k], win(k, *me), local_sems.at[k]) for k in range(K)]
        first = []
        for k in range(K):
            first.append(copy(k, 0, me, sibling, src=x_refs[k]))
            first += [copy(k, 1 + j, me, (*chip, mc), src=x_refs[k]) for j, chip in enumerate(chips)]
        return me, sibling, chips, copy, mine, first

    def start(x_refs, o_refs, sems):
        _, _, _, _, mine, first = plan(x_refs, o_refs, sems)
        for cp in mine + first:
            cp.start()

    def wait(x_refs, o_refs, sems):
        me, sibling, chips, copy, mine, first = plan(x_refs, o_refs, sems)
        mc = me[2]
        passed = []
        for j, chip in enumerate(chips):
            for k in range(K):
                copy(k, 1 + j, (*chip, mc), me).wait_recv()
                passed.append(copy(k, 4 + j, (*chip, mc), sibling))
                passed[-1].start()
        for k in range(K):
            copy(k, 0, sibling, me).wait_recv()
        for j, chip in enumerate(chips):
            for k in range(K):
                copy(k, 4 + j, (*chip, 1 - mc), me).wait_recv()
        for cp in first + passed:
            cp.wait_send()
        for cp in mine:
            cp.wait()

    scratch = [pltpu.SemaphoreType.DMA((7 * K,)), pltpu.SemaphoreType.DMA((7 * K,)), pltpu.SemaphoreType.DMA((K,))]
    return Rider(list(shards), out_shape, scratch, start, wait)


def exchange_rider(items):
    ns = len(items)
    out_shape = [_S((N_DEV,) + tuple(it[3]), it[0].dtype) for it in items]

    def plan(src_refs, o_refs, sems):
        send_sems, recv_sems, local_sems = sems
        mx, my, mc = lax.axis_index("x"), lax.axis_index("y"), lax.axis_index("c")
        me = 4 * mx + 2 * my + mc
        remote, own = [], []
        for s, (_, ax, n, _) in enumerate(items):
            own.append(pltpu.make_async_copy(_window(src_refs[s], ax, me, n), o_refs[s].at[me], local_sems.at[s]))
            for k in range(1, N_DEV):
                px = 1 - mx if k & 4 else mx
                py = 1 - my if k & 2 else my
                pc = 1 - mc if k & 1 else mc
                remote.append(pltpu.make_async_remote_copy(
                    src_ref=_window(src_refs[s], ax, 4 * px + 2 * py + pc, n), dst_ref=o_refs[s].at[me],
                    send_sem=send_sems.at[7 * s + k - 1], recv_sem=recv_sems.at[7 * s + k - 1],
                    device_id=(px, py, pc), device_id_type=MESH_ID))
        return remote, own

    def start(src_refs, o_refs, sems):
        remote, own = plan(src_refs, o_refs, sems)
        for cp in own + remote:
            cp.start()

    def wait(src_refs, o_refs, sems):
        remote, own = plan(src_refs, o_refs, sems)
        for cp in remote + own:
            cp.wait()

    scratch = [pltpu.SemaphoreType.DMA((7 * ns,)), pltpu.SemaphoreType.DMA((7 * ns,)), pltpu.SemaphoreType.DMA((ns,))]
    return Rider([it[0] for it in items], out_shape, scratch, start, wait)


def reduce_adamw(rcvs, w, m, v, name):
    L = len(rcvs)
    _, A, B, C = rcvs[0].shape
    tb = B
    while tb > 8 and tb * C > (1 << 17):
        tb //= 2

    def body(*refs):
        r_refs, (w_ref, m_ref, v_ref, g_ref, d_ref, mo_ref, vo_ref) = refs[:L], refs[L:]
        for k in range(L):
            @pl.when(pl.program_id(0) == k)
            def _(k=k):
                g = r_refs[k][0].astype(F32)
                for d in range(1, N_DEV):
                    g = g + r_refs[k][d].astype(F32)
                mn = ADAM_B1 * m_ref[...] + (1.0 - ADAM_B1) * g
                vn = ADAM_B2 * v_ref[...] + (1.0 - ADAM_B2) * jnp.square(g)
                m_hat = mn / (1.0 - ADAM_B1 ** ADAM_STEP)
                v_hat = vn / (1.0 - ADAM_B2 ** ADAM_STEP)
                g_ref[...] = g
                d_ref[...] = -ADAM_LR * (m_hat / (jnp.sqrt(v_hat) + ADAM_EPS) + ADAM_WD * w_ref[...])
                mo_ref[...] = mn
                vo_ref[...] = vn

    def rspec(k):
        return pl.BlockSpec((N_DEV, None, tb, C), lambda l, a, i: (0, jnp.where(l == k, a, 0), jnp.where(l == k, i, 0), 0))

    blk = pl.BlockSpec((None, tb, C), lambda l, a, i: (l * A + a, i, 0))
    return _pcall(body, grid=(L, A, B // tb), in_specs=[rspec(k) for k in range(L)] + [blk, blk, blk],
                  out_specs=[blk] * 4, out_shape=[_S((L * A, B, C))] * 4, name=name)(*rcvs, w, m, v)


def _w_in_pieces(g0, g1):
    per = D_IN // N_DEV
    return [(d, max(g0, d * per) - d * per, min(g1, (d + 1) * per) - d * per) for d in range(N_DEV) if max(g0, d * per) < min(g1, (d + 1) * per)]


def repack_w_in(w8, name):
    _, L, R, per = w8.shape
    tr = 256

    def cols(x_ref, g0, g1):
        return [x_ref[d, :, a:b] for d, a, b in _w_in_pieces(g0, g1)]

    def body(x_ref, *o_refs):
        for (name_, i), o_ref in zip(SEGS, o_refs):
            o_ref[...] = jnp.concatenate(cols(x_ref, _OFF[i], _OFF[i + 1]), axis=1)
        parts, at = [], 0
        for i, lane0 in SMALL_SRC:
            assert lane0 == at
            parts += cols(x_ref, _OFF[i], _OFF[i + 1])
            at += IN_SIZES[i]
        parts.append(jnp.zeros((tr, LANES - at), w8.dtype))
        o_refs[-1][...] = jnp.concatenate(parts, axis=1)

    widths = [IN_SIZES[i] for _, i in SEGS] + [LANES]
    outs = _pcall(body, grid=(L, R // tr), in_specs=[pl.BlockSpec((N_DEV, None, tr, per), lambda l, r: (0, l, r, 0))],
                  out_specs=[pl.BlockSpec((None, tr, w), lambda l, r: (l, r, 0)) for w in widths],
                  out_shape=[_S((L, R, w), w8.dtype) for w in widths], name=name)(w8)
    return dict(zip(SEG_NAMES, outs))


def repack_dw_in(dseg, name):
    R = dseg["z"].shape[0]
    per = D_IN // N_DEV
    tr = 128
    src = {i: (k, 0) for k, (_, i) in enumerate(SEGS)}
    src.update({i: (len(SEGS), lane0) for i, lane0 in SMALL_SRC})

    def body(*refs):
        s_refs, o_ref = refs[:-1], refs[-1]
        for d in range(N_DEV):
            parts = []
            for i in range(len(IN_SIZES)):
                g0, g1 = max(_OFF[i], d * per), min(_OFF[i + 1], (d + 1) * per)
                if g0 < g1:
                    k, c0 = src[i]
                    parts.append(s_refs[k][:, c0 + g0 - _OFF[i]:c0 + g1 - _OFF[i]])
            o_ref[d] = jnp.concatenate(parts, axis=1)

    arrs = [dseg[n] for n in SEG_NAMES]
    return _pcall(body, grid=(R // tr,), in_specs=[pl.BlockSpec((tr, a.shape[1]), lambda r: (r, 0)) for a in arrs],
                  out_specs=pl.BlockSpec((N_DEV, tr, per), lambda r: (0, r, 0)), out_shape=_S((N_DEV, R, per), arrs[0].dtype), name=name)(*arrs)


WEIGHTS = ("ln_in_g", "ln_in_b", "w_in", "ssd_conv_w", "ssd_conv_b", "ssd_dt_bias", "ssd_a_log", "ssd_d", "ssd_norm_w", "dn_conv_w",
           "dn_a_log", "dn_dt_bias", "dn_norm_w", "sg_ln_g", "sg_ln_b", "sg_w", "sg_b", "fox_f_bias", "gate_b", "w_branch", "w_out",
           "ln1_g", "ln1_b", "w_up", "w_down", "ln2_g", "ln2_b")
SHARDED = {"w_in": 2, "ssd_conv_w": 2, "dn_conv_w": 2, "gate_b": 2, "w_branch": 3, "w_out": 1, "w_up": 2, "w_down": 1}
SLABBED = ("w_in", "dn_conv_w")
MATMUL_WEIGHTS = ("w_in", "w_branch", "w_out", "w_up", "w_down")
REPLICATED_ENTRY = ("ln_in_g", "ln_in_b")
REPLICATED_LAYER = tuple(n for n in WEIGHTS if n not in SHARDED and n not in REPLICATED_ENTRY and n != "sg_w")
SEG_NAMES = tuple(n for n, _ in SEGS) + ("small",)
PACK_COLS = 1024


def _lanes(vec, off):
    return jnp.pad(vec, (off, LANES - off - vec.shape[0]))[None]


def _pack_small(parts, row_mult):
    flat = jnp.concatenate([q.reshape(-1) for q in parts])
    rows = -(-flat.shape[0] // (PACK_COLS * row_mult)) * row_mult
    return jnp.pad(flat, (0, rows * PACK_COLS - flat.shape[0])).reshape(1, rows, PACK_COLS)


EARLY = ("w_branch", "w_out", "w_up", "w_down", "gate_b")
LATE = ("w_in", "ssd_conv_w", "dn_conv_w")
WHOLE = ("sg_w",)


def _gather_rider(p, l, names):
    shards, axes = [], []
    for n in names:
        s = p[n][l:l + 1]
        s = s.astype(BF) if n in MATMUL_WEIGHTS else s
        shards.append(s[None] if n in SLABBED else s)
        axes.append(0 if n in SLABBED else SHARDED[n])
    return gather_rider(shards, axes)


def _exchange_items(g, p, names):
    items = []
    for n in names:
        local = p[n].shape[1:]
        if n in WHOLE:
            items.append((g[n], 0, -1, local))
        elif n in SLABBED:
            items.append((g[n], 0, 0, local))
        else:
            items.append((g[n], SHARDED[n] - 1, local[SHARDED[n] - 1], local))
    return items


def _use_gathered(w, names, arrays, l):
    for n, arr in zip(names, arrays):
        if n == "w_in":
            w[n] = repack_w_in(arr, f"w_in_repack_{l}")
        elif n == "ssd_conv_w":
            w["ssd_cw"] = arr[0]
        elif n == "dn_conv_w":
            w["dn_cw"] = jnp.moveaxis(arr[:, 0], 0, 1).reshape(4, 3 * BRANCH_W)
        elif n == "gate_b":
            w[n] = arr[0]
        else:
            w[n] = arr


def _layer_weights(p, l):
    w = {}
    w["ssd_cb"] = p["ssd_conv_b"][l][None]
    w["dn_cb"] = jnp.zeros((1, 3 * BRANCH_W), F32)
    w["ssd_ps"] = [_lanes(p["ssd_dt_bias"][l], DT0), _lanes(p["ssd_a_log"][l], DT0), _lanes(p["ssd_d"][l], DT0), p["ssd_norm_w"][l][None]]
    w["dn_ps"] = [_lanes(p["dn_a_log"][l], A0), _lanes(p["dn_dt_bias"][l], A0), p["dn_norm_w"][l][None]]
    w["sg_ps"] = [p["sg_ln_g"][l][None], p["sg_ln_b"][l][None], p["sg_w"][l], jnp.pad(p["sg_b"][l].T, ((0, 0), (0, LANES - 4)))]
    w["fox_ps"] = [_lanes(p["fox_f_bias"][l], FF0)]
    for n in ("ln1_g", "ln1_b", "ln2_g", "ln2_b"):
        w[n] = p[n][l][None]
    return w


def _scan_specs(T, a):
    c0 = lambda c, h: (c, 0)
    ssd = dict(f=ssd_chunk, xs=[(a["z"], (128, 512), c0), (a["xbc_act"], (128, 1024), c0), (a["small"], (128, LANES), c0)],
               ys=[((T, BRANCH_W), (128, BRANCH_W), c0)], state=(4, LANES, LANES), nc=T // 128, nh=1, shared=())
    dr = min(T, DN_ROWS)
    dn = dict(f=dn_chunk, xs=[(a["dn_act"], (dr, 3 * BRANCH_W), c0), (a["dngate"], (dr, BRANCH_W), c0), (a["small"], (dr, LANES), c0)],
              ys=[((T, BRANCH_W), (dr, BRANCH_W), c0)], state=(4, LANES, LANES), nc=T // dr, nh=1, shared=())
    sg = dict(f=sg_chunk, xs=[(a["sguv"], (128, 1024), c0)], ys=[((T, BRANCH_W), (128, BRANCH_W), c0)], state=(1, 8, LANES), nc=T // 128, nh=1, shared=())
    fc = dict(f=foxc_chunk, xs=[(a["small"], (128, LANES), c0)],
              ys=[((T, LANES), (128, LANES), c0)], state=(1, 1, LANES), nc=T // 128, nh=1, shared=())
    return ssd, dn, sg, fc


def _layer_fwd(h, w, l, dn_rider=None, fox_rider=None):
    T = h.shape[0]
    a = {"h": h, **proj_all(h, w["w_in"], f"proj_{l}")}
    a["xbc_act"] = conv_fwd(a["xbc"], w["ssd_cw"], w["ssd_cb"], f"ssd_conv_{l}")
    a["dn_act"] = conv_fwd(a["dnqkv"], w["dn_cw"], w["dn_cb"], f"dn_conv_{l}")
    ssd, dn, sg, fc = _scan_specs(T, a)
    (a["ya"], a["ssd_st"]), _ = scan_fwd(f"ssd_fwd_{l}", ssd["f"], ssd["xs"], w["ssd_ps"], ssd["ys"], ssd["state"], ssd["nc"], ssd["nh"])
    (a["yb"], a["dn_st"]), got = scan_fwd(f"dn_fwd_{l}", dn["f"], dn["xs"], w["dn_ps"], dn["ys"], dn["state"], dn["nc"], dn["nh"], rider=dn_rider)
    _use_gathered(w, EARLY, got, l)
    (a["yc"], a["sg_st"]), _ = scan_fwd(f"sg_fwd_{l}", sg["f"], sg["xs"], w["sg_ps"], sg["ys"], sg["state"], sg["nc"], sg["nh"])
    (a["ccol"], a["fc_st"]), _ = scan_fwd(f"foxc_fwd_{l}", fc["f"], fc["xs"], w["fox_ps"], fc["ys"], fc["state"], fc["nc"], fc["nh"])
    a["fox_qa"], a["fox_ka"], a["fox_va"] = fox_prep(a["foxqkv"], a["ccol"], f"fox_prep_{l}")
    (a["yd"], a["lse"]), carried = fox_fwd(a["fox_qa"], a["fox_ka"], a["fox_va"], f"fox_fwd_{l}", rider=fox_rider)
    a["merged"] = merge_fwd([a["ya"], a["yb"], a["yc"], a["yd"]], a["gates"], w["gate_b"], w["w_branch"], 0, f"merge_fwd_{l}")
    a["u1"], a["h1"] = out_fwd(a["merged"], h, w["w_out"], 0, w["ln1_g"], w["ln1_b"], f"out_fwd_{l}")
    a["u2"], a["h2"] = ff_fwd(a["h1"], w["w_up"], w["w_down"], 0, w["ln2_g"], w["ln2_b"], f"ff_fwd_{l}")
    return a, carried


def _layer_bwd(dh2, a, w, l, p, late_above):
    T = dh2.shape[0]
    g = {}
    du2, dh1, da, r, dg2, db2 = ff_bwd(a["u2"], dh2, a["h1"], w["ln2_g"], w["ln2_b"], w["w_up"], w["w_down"], 0, f"ff_bwd_{l}")
    g["ln2_g"], g["ln2_b"] = dg2[0], db2[0]
    g["w_up"] = matmul_dw(transpose_bf16(a["h1"], f"h1_t_{l}"), da, f"dwup_{l}")
    g["w_down"] = matmul_dw(r, du2, f"dwdown_{l}")
    du1, dmerged, dg1, db1 = out_bwd(a["u1"], dh1, w["ln1_g"], w["ln1_b"], w["w_out"], 0, f"out_bwd_{l}")
    g["ln1_g"], g["ln1_b"] = dg1[0], db1[0]
    g["w_out"] = matmul_dw(transpose_bf16(a["merged"], f"merged_t_{l}"), du1, f"dwout_{l}")
    ys = [a["ya"], a["yb"], a["yc"], a["yd"]]
    dya, dyb, dyc, dyd, dgl, dz, dgb = merge_bwd(ys, a["gates"], w["gate_b"], w["w_branch"], 0, dmerged, f"merge_bwd_{l}")
    g["gate_b"] = dgb
    g["w_branch"] = jnp.stack([matmul_tn(ys[i], dz, f"dwb{i}_{l}", b_col0=i * D_MODEL, n_cols=D_MODEL, out_dtype=BF) for i in range(4)])
    early = exchange_rider(_exchange_items(g, p, EARLY))
    dn_rider = early if late_above is None else exchange_rider(late_above)
    fox_rider = None if late_above is None else early
    ssd, dn, sg, fc = _scan_specs(T, a)
    (dz_ssd, dxbc_act, dsm_ssd, d_dtb, d_alog, d_dsk, d_nw), _ = scan_bwd(f"ssd_bwd_{l}", ssd["f"], ssd["xs"], w["ssd_ps"], ssd["ys"], [dya], a["ssd_st"],
                                                                           ssd["state"], ssd["nc"], ssd["nh"])
    g["ssd_dt_bias"], g["ssd_a_log"], g["ssd_d"], g["ssd_norm_w"] = d_dtb[0, DT0:DT0 + 8], d_alog[0, DT0:DT0 + 8], d_dsk[0, DT0:DT0 + 8], d_nw[0]
    dxbc, g["ssd_conv_w"], dcb = conv_bwd(a["xbc"], w["ssd_cw"], w["ssd_cb"], dxbc_act, f"ssd_conv_bwd_{l}")
    g["ssd_conv_b"] = dcb[0]
    (ddn_act, ddngate, dsm_dn, d_alog, d_dtb, d_nw), got_dn = scan_bwd(f"dn_bwd_{l}", dn["f"], dn["xs"], w["dn_ps"], dn["ys"], [dyb], a["dn_st"],
                                                                        dn["state"], dn["nc"], dn["nh"], rider=dn_rider)
    g["dn_a_log"], g["dn_dt_bias"], g["dn_norm_w"] = d_alog[0, A0:A0 + 4], d_dtb[0, A0:A0 + 4], d_nw[0]
    ddnqkv, g["dn_conv_w"], _ = conv_bwd(a["dnqkv"], w["dn_cw"], w["dn_cb"], ddn_act, f"dn_conv_bwd_{l}")
    (dsguv, d_lng, d_lnb, d_w, d_bt), _ = scan_bwd(f"sg_bwd_{l}", sg["f"], sg["xs"], w["sg_ps"], sg["ys"], [dyc], a["sg_st"], sg["state"], sg["nc"], sg["nh"])
    g["sg_ln_g"], g["sg_ln_b"], g["sg_w"], g["sg_b"] = d_lng[0], d_lnb[0], d_w, d_bt[:, :4].T
    qb, doa, qd, kd = fox_prep_bwd(a["foxqkv"], a["fox_qa"], a["yd"], a["lse"], dyd, f"fox_prep_bwd_{l}")
    (dfq, dfk, dfv, dccol), got_fox = fox_bwd(qb, a["fox_ka"], a["fox_va"], doa, qd, kd, f"fox_bwd_{l}", rider=fox_rider)
    (dsm_fox, d_fb), _ = scan_bwd(f"foxc_bwd_{l}", fc["f"], fc["xs"], w["fox_ps"], fc["ys"], [dccol], a["fc_st"], fc["state"], fc["nc"], fc["nh"])
    g["fox_f_bias"] = d_fb[0, FF0:FF0 + 8]
    dseg = {"z": dz_ssd, "xbc": dxbc, "dnqkv": ddnqkv, "dngate": ddngate, "sguv": dsguv,
            "foxqkv": jnp.concatenate([dfq, dfk, dfv], axis=1), "gates": dgl, "small": add3(dsm_ssd, dsm_dn, dsm_fox, f"dsmall_{l}")}
    h_t = transpose_bf16(a["h"], f"h_t_{l}")
    dwin = {n: matmul_dw(h_t, dseg[n], f"dwin_{n}_{l}") for n in SEG_NAMES}
    g["w_in"] = repack_dw_in(dwin, f"dw_in_repack_{l}")
    g["dn_conv_w"] = jnp.moveaxis(g["dn_conv_w"].reshape(4, N_DEV, 3 * BRANCH_W // N_DEV), 1, 0)
    got = {(EARLY, l): got_dn} if late_above is None else {(LATE + WHOLE, l + 1): got_dn, (EARLY, l): got_fox}
    return dseg, du1, g, got


def proj_all(h, w_in, name):
    T = h.shape[0]
    tm = min(T, 256)
    ns = len(SEG_NAMES)

    def body(*refs):
        h_ref, w_refs, o_refs = refs[0], refs[1:1 + ns], refs[1 + ns:]
        hb = h_ref[...].astype(BF)
        for w_ref, o_ref in zip(w_refs, o_refs):
            o_ref[...] = _dotb(hb, w_ref[...], NN)

    widths = [w_in[n].shape[2] for n in SEG_NAMES]
    in_specs = [_row(tm, D_MODEL)]
    in_specs += [pl.BlockSpec((None,) + w_in[n].shape[1:], lambda i: (0, 0, 0), pipeline_mode=pl.Buffered(1)) for n in SEG_NAMES]
    outs = _pcall(body, grid=(T // tm,), in_specs=in_specs, out_specs=[_row(tm, wd) for wd in widths],
                  out_shape=[_S((T, wd)) for wd in widths], name=name)(h, *[w_in[n] for n in SEG_NAMES])
    return dict(zip(SEG_NAMES, outs))


def dh_all(dseg, w_in, add, name, rider=None, norm=None):
    T = add.shape[0]
    tm = min(T, 256)
    ns = len(SEG_NAMES)

    def body(*refs):
        d_refs, w_refs, add_ref = refs[:ns], refs[ns:2 * ns], refs[2 * ns]
        acc = ALPHA * add_ref[...]
        for d_ref, w_ref in zip(d_refs, w_refs):
            acc = acc + _dotb(d_ref[...].astype(BF), w_ref[...], NT)
        if norm is None:
            refs[2 * ns + 1][...] = acc
        else:
            x_ref, g_ref, b_ref, dx_ref, dg_ref, db_ref = refs[2 * ns + 1:]
            _, vjp = jax.vjp(_ln, x_ref[...], g_ref[...], b_ref[...])
            dx, dg, db = vjp(acc)
            dx_ref[...] = dx
            first = pl.program_id(0) == 0
            _acc_out(dg_ref, dg, first)
            _acc_out(db_ref, db, first)

    C = D_MODEL
    in_specs = [_row(tm, dseg[n].shape[1]) for n in SEG_NAMES]
    in_specs += [pl.BlockSpec((None,) + w_in[n].shape[1:], lambda i: (0, 0, 0), pipeline_mode=pl.Buffered(1)) for n in SEG_NAMES]
    in_specs.append(_row(tm, C))
    args = [*[dseg[n] for n in SEG_NAMES], *[w_in[n] for n in SEG_NAMES], add]
    out_specs, out_shape = [_row(tm, C)], [_S((T, C))]
    if norm is not None:
        in_specs += [_row(tm, C), _full((1, C)), _full((1, C))]
        args += list(norm)
        out_specs += [_full((1, C)), _full((1, C))]
        out_shape += [_S((1, C)), _S((1, C))]
    return hosted_call(body, rider, grid=(T // tm,), in_specs=in_specs, out_specs=out_specs, out_shape=out_shape,
                       scratch_shapes=[], name=name, args=args)


def kernel(x, ln_in_g, ln_in_b, w_in, ssd_conv_w, ssd_conv_b, ssd_dt_bias, ssd_a_log, ssd_d, ssd_norm_w, dn_conv_w, dn_a_log, dn_dt_bias, dn_norm_w, sg_ln_g, sg_ln_b, sg_w, sg_b, fox_f_bias, gate_b, w_branch, w_out, ln1_g, ln1_b, w_up, w_down, ln2_g, ln2_b, loss_target, m_ln_in_g, m_ln_in_b, m_w_in, m_ssd_conv_w, m_ssd_conv_b, m_ssd_dt_bias, m_ssd_a_log, m_ssd_d, m_ssd_norm_w, m_dn_conv_w, m_dn_a_log, m_dn_dt_bias, m_dn_norm_w, m_sg_ln_g, m_sg_ln_b, m_sg_w, m_sg_b, m_fox_f_bias, m_gate_b, m_w_branch, m_w_out, m_ln1_g, m_ln1_b, m_w_up, m_w_down, m_ln2_g, m_ln2_b, v_ln_in_g, v_ln_in_b, v_w_in, v_ssd_conv_w, v_ssd_conv_b, v_ssd_dt_bias, v_ssd_a_log, v_ssd_d, v_ssd_norm_w, v_dn_conv_w, v_dn_a_log, v_dn_dt_bias, v_dn_norm_w, v_sg_ln_g, v_sg_ln_b, v_sg_w, v_sg_b, v_fox_f_bias, v_gate_b, v_w_branch, v_w_out, v_ln1_g, v_ln1_b, v_w_up, v_w_down, v_ln2_g, v_ln2_b):
    args = dict(locals())
    p = {n: args[n] for n in WEIGHTS}
    xt, target = x[0], loss_target[0]
    ws, acts = [_layer_weights(p, l) for l in range(DEPTH)], []
    (h,), gathered = ln_fwd(xt, ln_in_g[None], ln_in_b[None], "ln_in_fwd", rider=_gather_rider(p, 0, LATE))
    _use_gathered(ws[0], LATE, gathered, 0)
    for l in range(DEPTH):
        a, gathered = _layer_fwd(h, ws[l], l, dn_rider=_gather_rider(p, 0, EARLY) if l == 0 else None,
                                 fox_rider=_gather_rider(p, l + 1, LATE + EARLY) if l + 1 < DEPTH else None)
        if l + 1 < DEPTH:
            _use_gathered(ws[l + 1], LATE + EARLY, gathered, l + 1)
        acts.append(a)
        h = a["h2"]
    dh, loss = loss_head(h, target, "loss_head")
    loss = lax.psum(loss[0, 0], ("x", "y", "c"))

    layer_grads, got, late = [None] * DEPTH, {}, None
    for l in reversed(range(DEPTH)):
        dseg, du1, layer_grads[l], got_l = _layer_bwd(dh, acts[l], ws[l], l, p, late)
        got.update(got_l)
        late = _exchange_items(layer_grads[l], p, LATE + WHOLE)
        if l > 0:
            (dh,), _ = dh_all(dseg, ws[l]["w_in"], du1, f"dh_{l}")
    pack = _pack_small([jnp.stack([layer_grads[k][n] for k in range(DEPTH)]) for n in REPLICATED_LAYER], 8)
    (grad_x, dg_in, db_in), carried = dh_all(dseg, ws[0]["w_in"], du1, "dh_0", rider=exchange_rider(late + [(pack[0], 0, -1, pack.shape[1:])]),
                                             norm=(xt, ln_in_g[None], ln_in_b[None]))
    got[(LATE + WHOLE, 0)], got_layer_pack = carried[:-1], carried[-1]
    pack = _pack_small([dg_in[0], db_in[0]], 8)
    got_entry_pack = comm_call(exchange_rider([(pack[0], 0, -1, pack.shape[1:])]), "grads_exchange_entry_norm")[0]
    rcv = {(n, l): arr for (names, l), arrs in got.items() for n, arr in zip(names, arrs)}

    res = [{}, {}, {}, {}]
    for n in tuple(SHARDED) + WHOLE:
        shp = p[n].shape
        lead = math.prod(shp[1:-2])
        to3 = lambda t: t.reshape((-1,) + shp[-2:])
        outs = reduce_adamw([rcv[(n, l)].reshape((N_DEV, lead) + shp[-2:]) for l in range(DEPTH)],
                            to3(p[n]), to3(args["m_" + n]), to3(args["v_" + n]), f"adamw_{n}")
        for k in range(4):
            res[k][n] = outs[k].reshape(shp)
    for names, got_pack, rows, name in ((REPLICATED_LAYER, got_layer_pack, 8, "adamw_replicated"), (REPLICATED_ENTRY, got_entry_pack, 8, "adamw_entry_norm")):
        outs = reduce_adamw([got_pack[:, None]], _pack_small([p[n] for n in names], rows), _pack_small([args["m_" + n] for n in names], rows),
                            _pack_small([args["v_" + n] for n in names], rows), name)
        off = 0
        for n in names:
            shp = p[n].shape
            cnt = math.prod(shp)
            for k in range(4):
                res[k][n] = outs[k].reshape(-1)[off:off + cnt].reshape(shp)
            off += cnt
    return (loss, grad_x[None], *[res[0][n] for n in WEIGHTS], *[res[1][n] for n in WEIGHTS],
            *[res[2][n] for n in WEIGHTS], *[res[3][n] for n in WEIGHTS])
```

```python
import collections
import functools
import math

import jax
import jax.numpy as jnp
from jax import lax
from jax.experimental import pallas as pl
from jax.experimental.pallas import tpu as pltpu

F32 = jnp.float32
BF = jnp.bfloat16

D_MODEL = 1024
DEPTH = 2
BRANCH_W = 512
D_FF = 4096
LN_EPS = 1e-5
NORM_EPS = 1e-6
ALPHA = (2 * DEPTH) ** 0.25
N_DEV = 8
LANES = 128
ADAM_LR, ADAM_B1, ADAM_B2, ADAM_EPS, ADAM_WD, ADAM_STEP = 0.001, 0.9, 0.999, 1e-08, 0.01, 10

DT0, BETA0, A0, FF0 = 0, 8, 12, 16
IN_SIZES = (512, 1024, 8, 1536, 4, 4, 512, 1024, 1536, 8, 4096)
_OFF = [0]
for _s in IN_SIZES:
    _OFF.append(_OFF[-1] + _s)
D_IN = _OFF[-1]
SEGS = (("z", 0), ("xbc", 1), ("dnqkv", 3), ("dngate", 6), ("sguv", 7), ("foxqkv", 8), ("gates", 10))
SMALL_SRC = ((2, DT0), (4, BETA0), (5, A0), (9, FF0))

NN = ((1,), (0,))
NT = ((1,), (1,))
TN = ((0,), (0,))
_DIMS = {"nn": NN, "nt": NT, "tn": TN}


def _pcall(body, **kw):
    return pl.pallas_call(body, **kw)


def _S(shape, dtype=F32):
    return jax.ShapeDtypeStruct(tuple(shape), dtype)


def _iota(shape, dim):
    return lax.broadcasted_iota(jnp.int32, shape, dim)


def _dotb(a, b, dims):
    return lax.dot_general(a, b, (dims, ((), ())), preferred_element_type=F32)


def _split2(a):
    ah = a.astype(BF)
    return ah, (a - ah.astype(F32)).astype(BF)


def _split3(a):
    a1 = a.astype(BF)
    r = a - a1.astype(F32)
    a2 = r.astype(BF)
    a3 = (r - a2.astype(F32)).astype(BF)
    return a1, a2, a3


def _mm_raw(a, b, form, mode):
    d = _DIMS[form]
    if mode == "1":
        return _dotb(a.astype(BF), b.astype(BF), d)
    if mode == "3":
        ah, al = _split2(a)
        bh, bl = _split2(b)
        return _dotb(ah, bh, d) + (_dotb(ah, bl, d) + _dotb(al, bh, d))
    if mode == "xa":
        ab = a.astype(BF)
        b1, b2, b3 = _split3(b)
        return _dotb(ab, b1, d) + (_dotb(ab, b2, d) + _dotb(ab, b3, d))
    bb = b.astype(BF)
    a1, a2, a3 = _split3(a)
    return _dotb(a1, bb, d) + (_dotb(a2, bb, d) + _dotb(a3, bb, d))


@functools.partial(jax.custom_vjp, nondiff_argnums=(2, 3))
def mm(a, b, form, mode):
    return _mm_raw(a, b, form, mode)


def _mm_fwd(a, b, form, mode):
    return _mm_raw(a, b, form, mode), (a, b)


_XA_DB = {"nn": "xa", "nt": "xb", "tn": "xa"}
_XB_DA = {"nn": "xb", "nt": "xb", "tn": "xa"}


def _mm_bwd(form, mode, res, g):
    a, b = res
    ma = _XB_DA[form] if mode == "xb" else mode
    mb = _XA_DB[form] if mode == "xa" else mode
    da = db = None
    if mode != "xa":
        da = {"nn": lambda: mm(g, b, "nt", ma), "nt": lambda: mm(g, b, "nn", ma), "tn": lambda: mm(b, g, "nt", ma)}[form]()
    if mode != "xb":
        db = {"nn": lambda: mm(a, g, "tn", mb), "nt": lambda: mm(g, a, "tn", mb), "tn": lambda: mm(a, g, "nn", mb)}[form]()
    if da is None:
        da = jnp.zeros_like(a)
    if db is None:
        db = jnp.zeros_like(b)
    return da, db


mm.defvjp(_mm_fwd, _mm_bwd)


def _silu(x):
    return x * jax.nn.sigmoid(x)


def _ln(x, g, b):
    mu = jnp.mean(x, -1, keepdims=True)
    xc = x - mu
    var = jnp.mean(xc * xc, -1, keepdims=True)
    return xc * lax.rsqrt(var + LN_EPS) * g + b


def _pick(n, cap):
    if n <= cap:
        return n
    best = LANES
    for t in range(LANES, cap + 1, LANES):
        if n % t == 0:
            best = t
    return best


def transpose_bf16(a, name):
    T, C = a.shape
    tt = min(T, 512)

    def body(a_ref, o_ref):
        o_ref[...] = a_ref[...].T.astype(BF)

    return _pcall(body, grid=(T // tt,), in_specs=[pl.BlockSpec((tt, C), lambda t: (t, 0))], out_specs=pl.BlockSpec((C, tt), lambda t: (0, t)),
                  out_shape=_S((C, T), BF), name=name)(a)


def matmul_dw(a_t, b, name):
    M, K = a_t.shape
    N = b.shape[1]
    tm, tn, tk = min(M, 1024), _pick(N, 1024), _pick(K, 1024)
    nk = K // tk

    def body(a_ref, b_ref, o_ref, acc):
        k = pl.program_id(2)
        p = _dotb(a_ref[...], b_ref[...].astype(BF), NN)

        @pl.when(k == 0)
        def _():
            acc[...] = p

        @pl.when(k > 0)
        def _():
            acc[...] += p

        @pl.when(k == nk - 1)
        def _():
            o_ref[...] = acc[...].astype(BF)

    return _pcall(body, grid=(N // tn, M // tm, nk),
                  in_specs=[pl.BlockSpec((tm, tk), lambda j, i, k: (i, k)), pl.BlockSpec((tk, tn), lambda j, i, k: (k, j))],
                  out_specs=pl.BlockSpec((tm, tn), lambda j, i, k: (i, j)), out_shape=_S((M, N), BF),
                  scratch_shapes=[pltpu.VMEM((tm, tn), F32)], name=name)(a_t, b)


def matmul_tn(a, b, name, b_col0=0, n_cols=None, out_dtype=F32):
    T, M = a.shape
    N = b.shape[1] if n_cols is None else n_cols
    tm, tn, tt = _pick(M, 512), _pick(N, 1024), min(T, 512)
    nt = T // tt
    jb = b_col0 // tn

    def body(a_ref, b_ref, o_ref, acc):
        t = pl.program_id(2)
        p = _dotb(a_ref[...].astype(BF), b_ref[...].astype(BF), TN)

        @pl.when(t == 0)
        def _():
            acc[...] = p

        @pl.when(t > 0)
        def _():
            acc[...] += p

        @pl.when(t == nt - 1)
        def _():
            o_ref[...] = acc[...].astype(out_dtype)

    return _pcall(body, grid=(M // tm, N // tn, nt),
                  in_specs=[pl.BlockSpec((tt, tm), lambda i, j, t: (t, i)), pl.BlockSpec((tt, tn), lambda i, j, t: (t, jb + j))],
                  out_specs=pl.BlockSpec((tm, tn), lambda i, j, t: (i, j)), out_shape=_S((M, N), out_dtype),
                  scratch_shapes=[pltpu.VMEM((tm, tn), F32)], name=name)(a, b)


def _pieces(v):
    if v.ndim == 3:
        return [v[i] for i in range(v.shape[0])]
    n = v.shape[1] // LANES
    if n <= 1:
        return [v]
    return [v[:, i * LANES:(i + 1) * LANES] for i in range(n)]


def _join(ps, like_ndim):
    if like_ndim == 3:
        return jnp.stack(ps, axis=0)
    return ps[0] if len(ps) == 1 else jnp.concatenate(ps, axis=1)


def scan_fwd(name, f, xs, ps, ys, state_shape, nc, nh=1, rider=None):
    nx, npar, ny = len(xs), len(ps), len(ys)

    def body(*refs):
        x_refs, p_refs = refs[:nx], refs[nx:nx + npar]
        y_refs = refs[nx + npar:nx + npar + ny]
        st_out, st = refs[nx + npar + ny], refs[nx + npar + ny + 1]
        c, h = pl.program_id(0), pl.program_id(1)

        @pl.when(c == 0)
        def _():
            st[h] = jnp.zeros(state_shape, F32)

        S = st[h]
        st_out[...] = S
        yv, Sn = f([_pieces(r[...]) for r in x_refs], [_pieces(r[...]) for r in p_refs], _pieces(S), h)
        for r, v in zip(y_refs, yv):
            r[...] = _join(v, 2)
        st[h] = _join(Sn, 3)

    in_specs = [pl.BlockSpec(bs, im) for (_, bs, im) in xs]
    in_specs += [pl.BlockSpec(p.shape, (lambda c, h, n=p.ndim: (0,) * n)) for p in ps]
    out_specs = [pl.BlockSpec(bs, im) for (_, bs, im) in ys]
    out_specs.append(pl.BlockSpec((None, None) + tuple(state_shape), lambda c, h: (c, h, 0, 0, 0)))
    out_shape = [_S(s) for (s, _, _) in ys] + [_S((nc, nh) + tuple(state_shape))]
    return hosted_call(body, rider, grid=(nc, nh), in_specs=in_specs, out_specs=out_specs, out_shape=out_shape,
                       scratch_shapes=[pltpu.VMEM((nh,) + tuple(state_shape), F32)], name=name, args=[*[x[0] for x in xs], *ps])


def scan_bwd(name, f, xs, ps, ys, dys, states, state_shape, nc, nh=1, shared=(), rider=None):
    nx, npar, ny = len(xs), len(ps), len(ys)

    def body(*refs):
        x_refs, p_refs = refs[:nx], refs[nx:nx + npar]
        s_ref = refs[nx + npar]
        dy_refs = refs[nx + npar + 1:nx + npar + 1 + ny]
        o = nx + npar + 1 + ny
        dx_refs, dp_refs, dst = refs[o:o + nx], refs[o + nx:o + nx + npar], refs[o + nx + npar]
        c, h = pl.program_id(0), pl.program_id(1)

        @pl.when(c == 0)
        def _():
            dst[h] = jnp.zeros(state_shape, F32)

        @pl.when((c == 0) & (h == 0))
        def _():
            for r in dp_refs:
                r[...] = jnp.zeros(r.shape, F32)

        xv = [_pieces(r[...]) for r in x_refs]
        pv = [_pieces(r[...]) for r in p_refs]
        _, vjp = jax.vjp(lambda a, b, s: f(a, b, s, h), xv, pv, _pieces(s_ref[...]))
        dxv, dpv, dS = vjp(([_pieces(r[...]) for r in dy_refs], _pieces(dst[h])))
        for i, (r, v) in enumerate(zip(dx_refs, dxv)):
            if i in shared and nh > 1:
                @pl.when(h == 0)
                def _(r=r, v=v):
                    r[...] = _join(v, 2)

                @pl.when(h > 0)
                def _(r=r, v=v):
                    r[...] += _join(v, 2)
            else:
                r[...] = _join(v, 2)
        for r, v in zip(dp_refs, dpv):
            r[...] += _join(v, len(r.shape))
        dst[h] = _join(dS, 3)

    def rev(im):
        return lambda c, h: im(nc - 1 - c, h)

    in_specs = [pl.BlockSpec(bs, rev(im)) for (_, bs, im) in xs]
    in_specs += [pl.BlockSpec(p.shape, (lambda c, h, n=p.ndim: (0,) * n)) for p in ps]
    in_specs.append(pl.BlockSpec((None, None) + tuple(state_shape), lambda c, h: (nc - 1 - c, h, 0, 0, 0)))
    in_specs += [pl.BlockSpec(bs, rev(im)) for (_, bs, im) in ys]
    out_specs = [pl.BlockSpec(bs, rev(im)) for (_, bs, im) in xs]
    out_specs += [pl.BlockSpec(p.shape, (lambda c, h, n=p.ndim: (0,) * n)) for p in ps]
    out_shape = [_S(x[0].shape) for x in xs] + [_S(p.shape) for p in ps]
    return hosted_call(body, rider, grid=(nc, nh), in_specs=in_specs, out_specs=out_specs, out_shape=out_shape,
                       scratch_shapes=[pltpu.VMEM((nh,) + tuple(state_shape), F32)], name=name,
                       args=[*[x[0] for x in xs], *ps, states, *dys])


def _lane():
    return _iota((1, LANES), 1)


def _col(v, idx):
    return jnp.sum(v * (_lane() == idx).astype(F32), axis=1, keepdims=True)


def _last_row(v):
    r = v.shape[0]
    return jnp.sum(v * (_iota((r, 1), 0) == r - 1).astype(F32), axis=0, keepdims=True)


def _tril(n, strict=False):
    r, c = _iota((n, n), 0), _iota((n, n), 1)
    return (r > c) if strict else (r >= c)


def ssd_chunk(xs, ps, S, h):
    zp, xbc, (sm,) = xs
    (bias,), (alog,), (dsk,), nw = ps
    Q = SSD_CHUNK
    n = sm.shape[0] // Q
    I = range(8 * n)
    lane = _lane()
    a128 = jnp.where(lane < 8, -jnp.exp(alog), 0.0)
    tri = _tril(Q)
    zc, xc = [_chunks(x, Q) for x in zp], [_chunks(x, Q) for x in xbc]
    dtl = [jax.nn.softplus(s + bias) for s in _chunks(sm, Q)]
    cum = [mm(tri.astype(F32), d * a128, "nn", "xa") for d in dtl]
    sel8 = (_iota((8, LANES), 0) == _iota((8, LANES), 1)).astype(F32)
    cum_t = [mm(sel8, c, "nt", "xa") for c in cum]
    m0 = (lane < 64).astype(F32)
    rows0 = (_iota((LANES, 1), 0) < 64).astype(F32)
    me = [m0 if i % 2 == 0 else 1.0 - m0 for i in I]
    Bm, Cm = [xc[4 + (i % 8) // 4][i // 8] for i in I], [xc[6 + (i % 8) // 4][i // 8] for i in I]
    cb = [mm(xc[6 + g][c], xc[4 + g][c], "nt", "1") for c in range(n) for g in range(2)]
    col = [_col(cum[i // 8], i % 8) for i in I]
    row = [jnp.sum(cum_t[i // 8] * (_iota((8, 1), 0) == i % 8).astype(F32), axis=0, keepdims=True) for i in I]
    xh = [xc[(i % 8) // 2][i // 8] * me[i] for i in I]
    xdt = [xh[i] * _col(dtl[i // 8], i % 8) for i in I]
    seg = [jnp.exp(jnp.where(tri, col[i] - row[i], -jnp.inf)) for i in I]
    last = [_last_row(col[i]) for i in I]
    y_diag = [mm(cb[2 * (i // 8) + (i % 8) // 4] * seg[i], xdt[i], "nn", "1") for i in I]
    st = [mm(xdt[i], Bm[i] * jnp.exp(last[i] - col[i]), "tn", "1") for i in I]
    c_dec = [Cm[i] * jnp.exp(col[i]) for i in I]
    skip = [_col(dsk, i % 8) * xh[i] for i in I]
    out = []
    for c in range(n):
        J = range(8 * c, 8 * c + 8)
        y = [y_diag[i] + mm(c_dec[i], S[(i % 8) // 2], "nt", "1") * me[i] + skip[i] for i in J]
        S = [S[pr] * (jnp.exp(last[8 * c + 2 * pr]) * rows0 + jnp.exp(last[8 * c + 2 * pr + 1]) * (1.0 - rows0))
             + st[8 * c + 2 * pr] + st[8 * c + 2 * pr + 1] for pr in range(4)]
        yz = [(y[2 * pr] + y[2 * pr + 1]) * _silu(zc[pr][c]) for pr in range(4)]
        ssq = sum(jnp.sum(v * v, axis=1, keepdims=True) for v in yz)
        scale = lax.rsqrt(ssq / BRANCH_W + NORM_EPS)
        out.append([yz[pr] * scale * nw[pr] for pr in range(4)])
    return [[jnp.concatenate([out[c][pr] for c in range(n)], axis=0) if n > 1 else out[0][pr] for pr in range(4)]], S


@jax.custom_vjp
def _halves(x):
    r = x.shape[0] // 2
    return x[:r], x[r:]


_halves.defvjp(lambda x: (_halves(x), None), lambda _, g: (jnp.concatenate(g, axis=0),))
DN_CHUNK = 64
DN_ROWS = 256
SSD_CHUNK = 128
SSD_ROWS = 256


def _chunks(x, rows=DN_CHUNK):
    if x.shape[0] == rows:
        return [x]
    a, b = _halves(x)
    return _chunks(a, rows) + _chunks(b, rows)


def dn_chunk(xs, ps, S, h):
    act, gate, (sm,) = xs
    (alog,), (dtb,), (nw,) = ps
    C = DN_CHUNK
    n = sm.shape[0] // C
    I = range(4 * n)
    lane = _lane()
    tri, strict = _tril(C), _tril(C, True)
    sm2 = _chunks(sm)
    G = [jnp.where((lane >= A0) & (lane < A0 + 4), -jnp.exp(alog) * jax.nn.softplus(s + dtb), 0.0) for s in sm2]
    gcs = [mm(tri.astype(F32), g, "nn", "xa") for g in G]
    sig = [jax.nn.sigmoid(s) for s in sm2]
    parts = [_chunks(x) for x in act]
    q, k, v = ([parts[o + i % 4][i // 4] for i in I] for o in (0, 4, 8))
    gt = [_chunks(x) for x in gate]
    qn = [q[i] * lax.rsqrt(jnp.sum(q[i] * q[i], axis=1, keepdims=True) + NORM_EPS) * (LANES ** -0.5) for i in I]
    kn = [k[i] * lax.rsqrt(jnp.sum(k[i] * k[i], axis=1, keepdims=True) + NORM_EPS) for i in I]
    beta = [_col(sig[i // 4], BETA0 + i % 4) for i in I]
    gcol = [_col(gcs[i // 4], A0 + i % 4) for i in I]
    selr = [((_iota((8, LANES), 0) == 0) & (_iota((8, LANES), 1) == A0 + h)).astype(F32) for h in range(4)]
    grow = [jnp.sum(mm(selr[i % 4], gcs[i // 4], "nt", "xa"), axis=0, keepdims=True) for i in I]
    gamma = [jnp.exp(jnp.where(tri, gcol[i] - grow[i], -jnp.inf)) for i in I]
    kb = [kn[i] * beta[i] for i in I]
    pk = [-(mm(kb[i], kn[i], "nt", "1") * jnp.where(strict, gamma[i], 0.0)) for i in I]
    eye = (_iota((C, C), 0) == _iota((C, C), 1)).astype(F32)
    minv = [eye + pk[i] for i in I]
    for _ in range(5):
        pk = [mm(pk[i], pk[i], "nn", "3") for i in I]
        minv = [minv[i] + mm(minv[i], pk[i], "nn", "3") for i in I]
    eg = [jnp.exp(gcol[i]) for i in I]
    w = [mm(minv[i], kb[i] * eg[i], "nn", "3") for i in I]
    u = [mm(minv[i], v[i] * beta[i], "nn", "3") for i in I]
    glast = [_last_row(gcol[i]) for i in I]
    qg = [qn[i] * eg[i] for i in I]
    qk = [mm(qn[i], kn[i], "nt", "1") * gamma[i] for i in I]
    kdec = [kn[i] * jnp.exp(glast[i] - gcol[i]) for i in I]
    y = []
    for c in range(n):
        J = range(4 * c, 4 * c + 4)
        vnew = [u[i] - mm(w[i], S[i % 4], "nn", "1") for i in J]
        o = [mm(qg[i], S[i % 4], "nn", "1") + mm(qk[i], vn, "nn", "1") for i, vn in zip(J, vnew)]
        S = [S[i % 4] * jnp.exp(glast[i]) + mm(kdec[i], vn, "tn", "1") for i, vn in zip(J, vnew)]
        on = [x * lax.rsqrt(jnp.mean(x * x, axis=1, keepdims=True) + NORM_EPS) * nw for x in o]
        y.append([on[h] * _silu(gt[h][c]) for h in range(4)])
    return [[jnp.concatenate([y[c][h] for c in range(n)], axis=0) for h in range(4)]], S


def sg_chunk(xs, ps, S, h):
    (uv,) = xs
    lng, lnb, W, (bt,) = ps
    u = [jax.nn.gelu(p) for p in uv[:4]]
    v = [jax.nn.gelu(p) for p in uv[4:]]
    mu = sum(jnp.sum(p, axis=1, keepdims=True) for p in v) / BRANCH_W
    vc = [p - mu for p in v]
    var = sum(jnp.sum(p * p, axis=1, keepdims=True) for p in vc) / BRANCH_W
    inv = lax.rsqrt(var + LN_EPS)
    trif = _tril(W[0].shape[0]).astype(F32)
    out = []
    for g in range(4):
        vn = vc[g] * inv * lng[g] + lnb[g]
        out.append(u[g] * (mm(W[g] * trif, vn, "nn", "1") + _col(bt, g)))
    return [out], S


def foxc_chunk(xs, ps, S, h):
    (sm,), ((fb,),), (carry,) = xs[0], ps, S
    lane = _lane()
    ls = jnp.where((lane >= FF0) & (lane < FF0 + 8), jax.nn.log_sigmoid(sm + fb), 0.0)
    c = mm(_tril(sm.shape[0]).astype(F32), ls, "nn", "xa") + carry
    return [[c]], [_last_row(c)]


HALO = 8


def _conv_tiles(T, C):
    return min(T, 512), _pick(C, 512)


def conv_fwd(x, w, b, name):
    T, C = x.shape
    tm, cb = _conv_tiles(T, C)

    def body(xp_ref, x_ref, w_ref, b_ref, o_ref):
        i = pl.program_id(1)
        e = jnp.concatenate([xp_ref[...] * (i > 0).astype(F32), x_ref[...]], axis=0)
        pre = b_ref[...] + sum(w_ref[k:k + 1, :] * e[5 + k:5 + k + tm, :] for k in range(4))
        o_ref[...] = _silu(pre)

    hb = tm // HALO
    return _pcall(body, grid=(C // cb, T // tm),
                  in_specs=[pl.BlockSpec((HALO, cb), lambda j, i: (jnp.maximum(i * hb - 1, 0), j)), pl.BlockSpec((tm, cb), lambda j, i: (i, j)),
                            pl.BlockSpec((4, cb), lambda j, i: (0, j)), pl.BlockSpec((1, cb), lambda j, i: (0, j))],
                  out_specs=pl.BlockSpec((tm, cb), lambda j, i: (i, j)), out_shape=_S((T, C)), name=name)(x, x, w, b)


def conv_bwd(x, w, b, dact, name):
    T, C = x.shape
    tm, cb = _conv_tiles(T, C)
    nt = T // tm

    def body(xp_ref, x_ref, xn_ref, w_ref, b_ref, d_ref, dn_ref, dx_ref, dw_ref, db_ref):
        i = pl.program_id(1)
        has_prev, has_next = (i > 0).astype(F32), (i < nt - 1).astype(F32)
        e = jnp.concatenate([xp_ref[...] * has_prev, x_ref[...], xn_ref[...] * has_next], axis=0)
        pre = b_ref[...] + sum(w_ref[k:k + 1, :] * e[5 + k:5 + k + tm + 8, :] for k in range(4))
        de = jnp.concatenate([d_ref[...], dn_ref[...] * has_next], axis=0)
        sg = jax.nn.sigmoid(pre)
        dpre = de * (sg * (1.0 + pre * (1.0 - sg)))
        dx_ref[...] = sum(w_ref[k:k + 1, :] * dpre[3 - k:3 - k + tm, :] for k in range(4))
        dcur = dpre[0:tm, :]
        dw = jnp.concatenate([jnp.sum(dcur * e[5 + k:5 + k + tm, :], axis=0, keepdims=True) for k in range(4)], axis=0)
        db = jnp.sum(dcur, axis=0, keepdims=True)

        @pl.when(i == 0)
        def _():
            dw_ref[...] = dw
            db_ref[...] = db

        @pl.when(i > 0)
        def _():
            dw_ref[...] += dw
            db_ref[...] += db

    blk = lambda f: pl.BlockSpec((tm, cb), f)
    hb = tm // HALO
    before = pl.BlockSpec((HALO, cb), lambda j, i: (jnp.maximum(i * hb - 1, 0), j))
    after = pl.BlockSpec((HALO, cb), lambda j, i: (jnp.minimum((i + 1) * hb, nt * hb - 1), j))
    return _pcall(body, grid=(C // cb, nt),
                  in_specs=[before, blk(lambda j, i: (i, j)), after,
                            pl.BlockSpec((4, cb), lambda j, i: (0, j)), pl.BlockSpec((1, cb), lambda j, i: (0, j)),
                            blk(lambda j, i: (i, j)), after],
                  out_specs=[blk(lambda j, i: (i, j)), pl.BlockSpec((4, cb), lambda j, i: (0, j)), pl.BlockSpec((1, cb), lambda j, i: (0, j))],
                  out_shape=[_S((T, C)), _S((4, C)), _S((1, C))], name=name)(x, x, x, w, b, dact, dact)


FOX_SCALE = 64 ** -0.5
LOG2E = 1.4426950408889634


def _spare(e, i):
    return (_lane() == 64 * (1 - e) + i).astype(F32)


def _lanes_of(e):
    lane = _lane()
    return ((lane < 64) if e == 0 else (lane >= 64)).astype(F32)


def _col3(col, e, first):
    c1 = col.astype(BF).astype(F32)
    c2 = (col - c1).astype(BF).astype(F32)
    c3 = (col - c1 - c2).astype(BF).astype(F32)
    return c1 * _spare(e, first) + c2 * _spare(e, first + 1) + c3 * _spare(e, first + 2)


def _split3_dense(x):
    x1 = x.astype(BF)
    r = x - x1.astype(F32)
    x2 = r.astype(BF)
    x3 = (r - x2.astype(F32)).astype(BF)
    return jnp.concatenate([x1, x2, x3], axis=1)


def _place(src_lane, e, first, sign):
    r, m = _iota((3 * LANES, LANES), 0), _iota((3 * LANES, LANES), 1)
    hit = (r % LANES == src_lane) & (m == 64 * (1 - e) + first + r // LANES)
    return jnp.where(hit, sign, 0.0).astype(BF)


def _ones3(e, first):
    return _spare(e, first) + _spare(e, first + 1) + _spare(e, first + 2)


def _causal_bias(n):
    return jnp.where(_iota((n, n), 0) >= _iota((n, n), 1), 0.0, -jnp.inf).astype(F32)


def _pair_spec(tq, row_of):
    return pl.BlockSpec((None, 2, tq, LANES), lambda hp, a, b: (hp, 0, row_of(a, b), 0))


def fox_prep(qkv, ccol, name):
    T = qkv.shape[0]
    tq = min(T, 512)

    def body(q_ref, k_ref, v_ref, cc_ref, qa_ref, ka_ref, va_ref):
        hp = pl.program_id(0)
        q, k, v = q_ref[...], k_ref[...], v_ref[...]
        parts = _split3_dense(cc_ref[...] * LOG2E)
        for e in range(2):
            me = _lanes_of(e)
            src = FF0 + 2 * hp + e
            qa_ref[e] = (q * me * (FOX_SCALE * LOG2E) + _dotb(parts, _place(src, e, 0, 1.0), NN) + _ones3(e, 3)).astype(BF)
            ka_ref[e] = (k * me + _ones3(e, 0) + _dotb(parts, _place(src, e, 3, -1.0), NN) + _ones3(e, 6)).astype(BF)
            va_ref[e] = (v * me + (1.0 - me)).astype(BF)

    blk = lambda off: pl.BlockSpec((tq, LANES), lambda hp, i: (i, off + hp))
    out = pl.BlockSpec((None, 2, tq, LANES), lambda hp, i: (hp, 0, i, 0))
    return _pcall(body, grid=(4, T // tq), in_specs=[blk(0), blk(4), blk(8), pl.BlockSpec((tq, LANES), lambda hp, i: (i, 0))],
                  out_specs=[out] * 3, out_shape=[_S((4, 2, T, LANES), BF)] * 3, name=name)(qkv, qkv, qkv, ccol)


def fox_fwd(qa, ka, va, name, rider=None):
    T = qa.shape[2]
    tq = min(T, 512)
    nq = T // tq

    def body(qa_ref, ka_ref, va_ref, o_ref, lse_ref, m_s, acc, causal):
        i, j = pl.program_id(1), pl.program_id(2)

        @pl.when((pl.program_id(0) == 0) & (i == 0) & (j == 0))
        def _():
            causal[...] = _causal_bias(tq)

        @pl.when(j == 0)
        def _():
            m_s[...] = jnp.full(m_s.shape, -jnp.inf, F32)
            acc[...] = jnp.zeros(acc.shape, F32)

        def step(diagonal):
            for e in range(2):
                s = _dotb(qa_ref[e], ka_ref[e], NT)
                if diagonal:
                    s = s + causal[...]
                m_old = m_s[e]
                m_new = jnp.maximum(m_old, jnp.max(s, axis=1, keepdims=True))
                p = jnp.exp2(s - m_new)
                m_s[e] = m_new
                acc[e] = acc[e] * jnp.exp2(m_old - m_new) + _dotb(p.astype(BF), va_ref[e], NN)

        @pl.when(j < i)
        def _():
            step(False)

        @pl.when(j == i)
        def _():
            step(True)
            lane = _lane()
            o, lse = 0.0, 0.0
            for e in range(2):
                l = jnp.sum(acc[e] * _spare(e, 0), axis=1, keepdims=True)
                o = o + acc[e] * _lanes_of(e) / l
                lse = lse + (m_s[e] + jnp.log2(l)) * (lane == e).astype(F32)
            o_ref[...] = o
            lse_ref[...] = lse

    kv = _pair_spec(tq, lambda i, j: jnp.minimum(j, i))
    return hosted_call(body, rider, grid=(4, nq, nq), in_specs=[_pair_spec(tq, lambda i, j: i), kv, kv],
                       out_specs=[pl.BlockSpec((tq, LANES), lambda hp, i, j: (i, hp)), pl.BlockSpec((None, tq, LANES), lambda hp, i, j: (hp, i, 0))],
                       out_shape=[_S((T, BRANCH_W)), _S((4, T, LANES))],
                       scratch_shapes=[pltpu.VMEM((2, tq, 1), F32), pltpu.VMEM((2, tq, LANES), F32), pltpu.VMEM((tq, tq), F32)],
                       name=name, args=[qa, ka, va])


def fox_prep_bwd(qkv, qa, o, lse, do, name):
    T = qkv.shape[0]
    tq = min(T, 512)

    def body(q_ref, k_ref, qa_ref, o_ref, lse_ref, do_ref, qb_ref, doa_ref, qd_ref, kd_ref):
        q, k, dov = q_ref[...], k_ref[...], do_ref[...]
        dd = dov * o_ref[...]
        lse_parts = _split3_dense(lse_ref[...])
        for e in range(2):
            me = _lanes_of(e)
            qb_ref[e] = (qa_ref[e].astype(F32) + _dotb(lse_parts, _place(e, e, 6, -1.0), NN)).astype(BF)
            doa_ref[e] = (dov * me + _col3(-jnp.sum(dd * me, axis=1, keepdims=True), e, 0)).astype(BF)
            qd_ref[e] = (q * me * FOX_SCALE + _spare(e, 0)).astype(BF)
            kd_ref[e] = (k * me * FOX_SCALE + _spare(e, 0)).astype(BF)

    blk = lambda off: pl.BlockSpec((tq, LANES), lambda hp, i: (i, off + hp))
    pair = pl.BlockSpec((None, 2, tq, LANES), lambda hp, i: (hp, 0, i, 0))
    return _pcall(body, grid=(4, T // tq),
                  in_specs=[blk(0), blk(4), pair, blk(0), pl.BlockSpec((None, tq, LANES), lambda hp, i: (hp, i, 0)), blk(0)],
                  out_specs=[pair] * 4, out_shape=[_S((4, 2, T, LANES), BF)] * 4, name=name)(qkv, qkv, qa, o, lse, do)


def fox_bwd(qb, ka, va, doa, qd, kd, name, rider=None):
    T = qb.shape[2]
    tq = min(T, 512)
    nq = T // tq

    def body(qb_ref, ka_ref, va_ref, doa_ref, qd_ref, kd_ref, dq_ref, dk_ref, dv_ref, dcc_ref, dq_s, dk_s, dv_s, causal):
        hp, j, ii = pl.program_id(0), pl.program_id(1), pl.program_id(2)
        i = jnp.maximum(ii, j)

        @pl.when((hp == 0) & (j == 0) & (ii == 0))
        def _():
            dcc_ref[...] = jnp.zeros(dcc_ref.shape, F32)
            causal[...] = _causal_bias(tq)

        @pl.when((j == 0) & (ii == 0))
        def _():
            dq_s[...] = jnp.zeros(dq_s.shape, F32)

        @pl.when(ii == 0)
        def _():
            dk_s[...] = jnp.zeros(dk_s.shape, F32)
            dv_s[...] = jnp.zeros(dv_s.shape, F32)

        def step(diagonal):
            rows = pl.ds(pl.multiple_of(i * tq, tq), tq)
            for e in range(2):
                s = _dotb(qb_ref[e], ka_ref[e], NT)
                if diagonal:
                    s = s + causal[...]
                p = jnp.exp2(s)
                ds = (p * _dotb(doa_ref[e], va_ref[e], NT)).astype(BF)
                dv_s[e] += _dotb(p.astype(BF), doa_ref[e], TN)
                dq_s[e, rows, :] += _dotb(ds, kd_ref[e], NN)
                dk_s[e] += _dotb(ds, qd_ref[e], TN)

        @pl.when(ii > j)
        def _():
            step(False)

        @pl.when(ii == j)
        def _():
            step(True)

        def fold(acc, sign):
            grad, dc = 0.0, 0.0
            for e in range(2):
                a = acc[e]
                grad = grad + a * _lanes_of(e)
                dc = dc + sign * jnp.sum(a * _spare(e, 0), axis=1, keepdims=True) * (_lane() == FF0 + 2 * hp + e).astype(F32)
            return grad, dc

        @pl.when(ii == nq - 1)
        def _():
            grad, dc = fold(dk_s, -1.0)
            dk_ref[...] = grad
            dv_ref[...] = dv_s[0] * _lanes_of(0) + dv_s[1] * _lanes_of(1)
            dcc_ref[pl.ds(pl.multiple_of(j * tq, tq), tq), :] += dc

        @pl.when((j == nq - 1) & (ii == nq - 1))
        def _():
            grad, dc = fold(dq_s, 1.0)
            dq_ref[...] = grad
            dcc_ref[...] += dc

    irow, jrow = _pair_spec(tq, lambda j, ii: jnp.maximum(ii, j)), _pair_spec(tq, lambda j, ii: j)
    jout = pl.BlockSpec((tq, LANES), lambda hp, j, ii: (j, hp))
    return hosted_call(body, rider, grid=(4, nq, nq), in_specs=[irow, jrow, jrow, irow, irow, jrow],
                       out_specs=[pl.BlockSpec((T, LANES), lambda hp, j, ii: (0, hp)), jout, jout, pl.BlockSpec((T, LANES), lambda hp, j, ii: (0, 0))],
                       out_shape=[_S((T, BRANCH_W)), _S((T, BRANCH_W)), _S((T, BRANCH_W)), _S((T, LANES))],
                       scratch_shapes=[pltpu.VMEM((2, T, LANES), F32), pltpu.VMEM((2, tq, LANES), F32), pltpu.VMEM((2, tq, LANES), F32),
                                       pltpu.VMEM((tq, tq), F32)],
                       name=name, args=[qb, ka, va, doa, qd, kd])


def _acc_out(ref, val, first):
    @pl.when(first)
    def _():
        ref[...] = val

    @pl.when(jnp.logical_not(first))
    def _():
        ref[...] += val


def _row(tm, c):
    return pl.BlockSpec((tm, c), lambda i: (i, 0))


def _full(shape):
    return pl.BlockSpec(shape, lambda *_: (0,) * len(shape))


def ln_fwd(x, g, b, name, rider=None):
    T, C = x.shape
    tm = min(T, 512)

    def body(x_ref, g_ref, b_ref, o_ref):
        o_ref[...] = _ln(x_ref[...], g_ref[...], b_ref[...])

    return hosted_call(body, rider, grid=(T // tm,), in_specs=[_row(tm, C), _full((1, C)), _full((1, C))], out_specs=[_row(tm, C)],
                       out_shape=[_S((T, C))], scratch_shapes=[], name=name, args=[x, g, b])


def loss_head(h, target, name):
    T, C = h.shape
    tm = min(T, 512)

    def body(h_ref, t_ref, d_ref, l_ref):
        e = h_ref[...] - t_ref[...]
        d_ref[...] = e * (1.0 / C)
        part = jnp.sum(jnp.sum(e * e, axis=1, keepdims=True), axis=0, keepdims=True) * (0.5 / C)
        _acc_out(l_ref, part, pl.program_id(0) == 0)

    return _pcall(body, grid=(T // tm,), in_specs=[_row(tm, C), _row(tm, C)], out_specs=[_row(tm, C), _full((1, 1))],
                  out_shape=[_S((T, C)), _S((1, 1))], name=name)(h, target)


def add3(a, b, c, name):
    T, C = a.shape
    tm = min(T, 512)

    def body(a_ref, b_ref, c_ref, o_ref):
        o_ref[...] = a_ref[...] + b_ref[...] + c_ref[...]

    return _pcall(body, grid=(T // tm,), in_specs=[_row(tm, C)] * 3, out_specs=_row(tm, C), out_shape=_S((T, C)), name=name)(a, b, c)


def _wb_spec(l):
    return pl.BlockSpec((None, 4, BRANCH_W, D_MODEL), lambda *_: (l, 0, 0, 0))


def merge_fwd(ys, gl, gb, wb, l, name):
    T = gl.shape[0]
    tm = min(T, 256)

    def body(y0, y1, y2, y3, gl_ref, gb_ref, wb_ref, o_ref):
        acc = 0.0
        for i, y in enumerate((y0, y1, y2, y3)):
            z = _dotb(y[...].astype(BF), wb_ref[i], NN)
            g = jax.nn.sigmoid(gl_ref[:, i * D_MODEL:(i + 1) * D_MODEL] + gb_ref[i:i + 1, :])
            acc = acc + g * z
        o_ref[...] = acc

    return _pcall(body, grid=(T // tm,), in_specs=[_row(tm, BRANCH_W)] * 4 + [_row(tm, 4 * D_MODEL), _full((4, D_MODEL)), _wb_spec(l)],
                  out_specs=_row(tm, D_MODEL), out_shape=_S((T, D_MODEL)), name=name)(*ys, gl, gb, wb)


def merge_bwd(ys, gl, gb, wb, l, dm, name):
    T = gl.shape[0]
    tm = min(T, 256)

    def body(y0, y1, y2, y3, gl_ref, gb_ref, wb_ref, dm_ref, d0, d1, d2, d3, dgl_ref, dz_ref, dgb_ref):
        dmv = dm_ref[...]
        first = pl.program_id(0) == 0
        for i, (y, d) in enumerate(zip((y0, y1, y2, y3), (d0, d1, d2, d3))):
            cols = slice(i * D_MODEL, (i + 1) * D_MODEL)
            z = _dotb(y[...].astype(BF), wb_ref[i], NN)
            g = jax.nn.sigmoid(gl_ref[:, cols] + gb_ref[i:i + 1, :])
            dgl = dmv * z * (g * (1.0 - g))
            dz = (g * dmv).astype(BF)
            dgl_ref[:, cols] = dgl
            dz_ref[:, cols] = dz
            d[...] = _dotb(dz, wb_ref[i], NT)
            _acc_out(dgb_ref.at[i:i + 1, :], jnp.sum(dgl, axis=0, keepdims=True), first)

    return _pcall(body, grid=(T // tm,),
                  in_specs=[_row(tm, BRANCH_W)] * 4 + [_row(tm, 4 * D_MODEL), _full((4, D_MODEL)), _wb_spec(l), _row(tm, D_MODEL)],
                  out_specs=[_row(tm, BRANCH_W)] * 4 + [_row(tm, 4 * D_MODEL), _row(tm, 4 * D_MODEL), _full((4, D_MODEL))],
                  out_shape=[_S((T, BRANCH_W))] * 4 + [_S((T, 4 * D_MODEL)), _S((T, 4 * D_MODEL), BF), _S((4, D_MODEL))], name=name)(
                      *ys, gl, gb, wb, dm)


def _wout_spec(l):
    return pl.BlockSpec((None, D_MODEL, D_MODEL), lambda *_: (l, 0, 0))


def out_fwd(merged, h, wout, l, g, b, name):
    T = h.shape[0]
    tm = min(T, 512)

    def body(m_ref, h_ref, w_ref, g_ref, b_ref, u_ref, o_ref):
        u = ALPHA * h_ref[...] + _dotb(m_ref[...].astype(BF), w_ref[...], NN)
        u_ref[...] = u
        o_ref[...] = _ln(u, g_ref[...], b_ref[...])

    C = D_MODEL
    return _pcall(body, grid=(T // tm,), in_specs=[_row(tm, C), _row(tm, C), _wout_spec(l), _full((1, C)), _full((1, C))],
                  out_specs=[_row(tm, C), _row(tm, C)], out_shape=[_S((T, C)), _S((T, C))], name=name)(merged, h, wout, g, b)


def out_bwd(u, dy, g, b, wout, l, name):
    T, C = u.shape
    tm = min(T, 512)

    def body(u_ref, dy_ref, g_ref, b_ref, w_ref, du_ref, dm_ref, dg_ref, db_ref):
        _, vjp = jax.vjp(_ln, u_ref[...], g_ref[...], b_ref[...])
        du, dg, db = vjp(dy_ref[...])
        du_ref[...] = du
        dm_ref[...] = _dotb(du.astype(BF), w_ref[...], NT)
        first = pl.program_id(0) == 0
        _acc_out(dg_ref, dg, first)
        _acc_out(db_ref, db, first)

    return _pcall(body, grid=(T // tm,), in_specs=[_row(tm, C), _row(tm, C), _full((1, C)), _full((1, C)), _wout_spec(l)],
                  out_specs=[_row(tm, C), _row(tm, C), _full((1, C)), _full((1, C))],
                  out_shape=[_S((T, C)), _S((T, C)), _S((1, C)), _S((1, C))], name=name)(u, dy, g, b, wout)


def ff_fwd(h, wup, wdown, l, g, b, name):
    T, C = h.shape
    F = wup.shape[2]
    tm, tf = min(T, 1024), 1024
    nf = F // tf

    def body(h_ref, wu_ref, wd_ref, g_ref, b_ref, u_ref, o_ref, acc):
        f = pl.program_id(1)
        a = _dotb(h_ref[...].astype(BF), wu_ref[...], NN)
        r = jnp.square(jnp.maximum(a, 0.0))
        p = _dotb(r.astype(BF), wd_ref[...], NN)
        _acc_out(acc, p, f == 0)

        @pl.when(f == nf - 1)
        def _():
            u = ALPHA * h_ref[...] + acc[...]
            u_ref[...] = u
            o_ref[...] = _ln(u, g_ref[...], b_ref[...])

    row = pl.BlockSpec((tm, C), lambda i, f: (i, 0))
    return _pcall(body, grid=(T // tm, nf),
                  in_specs=[row, pl.BlockSpec((None, C, tf), lambda i, f: (l, 0, f)), pl.BlockSpec((None, tf, C), lambda i, f: (l, f, 0)),
                            _full((1, C)), _full((1, C))],
                  out_specs=[row, row], out_shape=[_S((T, C)), _S((T, C))], scratch_shapes=[pltpu.VMEM((tm, C), F32)], name=name)(h, wup, wdown, g, b)


def ff_bwd(u, dy, h, g, b, wup, wdown, l, name):
    T, C = h.shape
    F = wup.shape[2]
    tm, tf = min(T, 512), 1024
    nf = F // tf

    def body(u_ref, dy_ref, h_ref, g_ref, b_ref, wu_ref, wd_ref, du_ref, dh_ref, da_ref, r_ref, dg_ref, db_ref, du_s, acc):
        i, f = pl.program_id(0), pl.program_id(1)

        @pl.when(f == 0)
        def _():
            _, vjp = jax.vjp(_ln, u_ref[...], g_ref[...], b_ref[...])
            du, dg, db = vjp(dy_ref[...])
            du_s[...] = du
            du_ref[...] = du
            _acc_out(dg_ref, dg, i == 0)
            _acc_out(db_ref, db, i == 0)

        a = _dotb(h_ref[...].astype(BF), wu_ref[...], NN)
        ap = jnp.maximum(a, 0.0)
        dr = _dotb(du_s[...].astype(BF), wd_ref[...], NT)
        da = (dr * (2.0 * ap)).astype(BF)
        da_ref[...] = da
        r_ref[...] = jnp.square(ap).T.astype(BF)
        _acc_out(acc, _dotb(da, wu_ref[...], NT), f == 0)

        @pl.when(f == nf - 1)
        def _():
            dh_ref[...] = ALPHA * du_s[...] + acc[...]

    row = pl.BlockSpec((tm, C), lambda i, f: (i, 0))
    colf = pl.BlockSpec((tm, tf), lambda i, f: (i, f))
    return _pcall(body, grid=(T // tm, nf),
                  in_specs=[row, row, row, _full((1, C)), _full((1, C)), pl.BlockSpec((None, C, tf), lambda i, f: (l, 0, f)),
                            pl.BlockSpec((None, tf, C), lambda i, f: (l, f, 0))],
                  out_specs=[row, row, colf, pl.BlockSpec((tf, tm), lambda i, f: (f, i)), _full((1, C)), _full((1, C))],
                  out_shape=[_S((T, C)), _S((T, C)), _S((T, F), BF), _S((F, T), BF), _S((1, C)), _S((1, C))],
                  scratch_shapes=[pltpu.VMEM((tm, C), F32), pltpu.VMEM((tm, C), F32)], name=name)(u, dy, h, g, b, wup, wdown)


MESH_ID = pl.DeviceIdType.MESH
_ANY = pl.BlockSpec(memory_space=pl.ANY)


def _window(ref, ax, idx, n):
    if n < 0:
        return ref
    sel = idx if n == 0 else pl.ds(pl.multiple_of(idx * n, n), n)
    return ref.at[(slice(None),) * ax + (sel,)]


Rider = collections.namedtuple("Rider", "operands out_shape scratch start wait")


def hosted_call(body, rider, *, grid, in_specs, out_specs, out_shape, scratch_shapes, name, args):
    n_in, n_out, n_scr = len(in_specs), len(out_specs), len(scratch_shapes)
    if rider is None:
        return _pcall(body, grid=grid, in_specs=in_specs, out_specs=out_specs, out_shape=out_shape, scratch_shapes=scratch_shapes, name=name)(*args), []
    ri, ro = len(rider.operands), len(rider.out_shape)

    def wrapped(*refs):
        ins, r_in = refs[:n_in], refs[n_in:n_in + ri]
        o0 = n_in + ri
        outs, r_out = refs[o0:o0 + n_out], refs[o0 + n_out:o0 + n_out + ro]
        s0 = o0 + n_out + ro
        scr, r_scr = refs[s0:s0 + n_scr], refs[s0 + n_scr:]
        ids = [pl.program_id(i) for i in range(len(grid))]
        first = functools.reduce(jnp.logical_and, [i == 0 for i in ids])
        last = functools.reduce(jnp.logical_and, [i == g - 1 for i, g in zip(ids, grid)])

        @pl.when(first)
        def _():
            rider.start(r_in, r_out, r_scr)

        body(*ins, *outs, *scr)

        @pl.when(last)
        def _():
            rider.wait(r_in, r_out, r_scr)

    res = _pcall(wrapped, grid=grid, in_specs=list(in_specs) + [_ANY] * ri, out_specs=list(out_specs) + [_ANY] * ro,
                 out_shape=list(out_shape) + list(rider.out_shape), scratch_shapes=list(scratch_shapes) + list(rider.scratch),
                 name=name)(*args, *rider.operands)
    return res[:n_out], res[n_out:]


def comm_call(rider, name):
    ri = len(rider.operands)

    def body(*refs):
        r_in, r_out, r_scr = refs[:ri], refs[ri:ri + len(rider.out_shape)], refs[ri + len(rider.out_shape):]
        rider.start(r_in, r_out, r_scr)
        rider.wait(r_in, r_out, r_scr)

    return _pcall(body, in_specs=[_ANY] * ri, out_specs=[_ANY] * len(rider.out_shape), out_shape=list(rider.out_shape),
                  scratch_shapes=list(rider.scratch), name=name)(*rider.operands)


def gather_rider(shards, axes):
    K = len(shards)
    widths = [s.shape[a] for s, a in zip(shards, axes)]
    out_shape = [_S(s.shape[:a] + (N_DEV * s.shape[a],) + s.shape[a + 1:], s.dtype) for s, a in zip(shards, axes)]

    def plan(x_refs, o_refs, sems):
        send_sems, recv_sems, local_sems = sems
        mx, my, mc = lax.axis_index("x"), lax.axis_index("y"), lax.axis_index("c")
        me, sibling = (mx, my, mc), (mx, my, 1 - mc)
        chips = [(1 - mx, my), (mx, 1 - my), (1 - mx, 1 - my)]

        def win(k, px, py, pc):
            return _window(o_refs[k], axes[k], 4 * px + 2 * py + pc, widths[k])

        def copy(k, slot, block, to, src=None):
            return pltpu.make_async_remote_copy(src_ref=win(k, *block) if src is None else src, dst_ref=win(k, *block),
                                                send_sem=send_sems.at[7 * k + slot], recv_sem=recv_sems.at[7 * k + slot],
                                                device_id=to, device_id_type=MESH_ID)

        mine = [pltpu.make_async_copy(x_refs[k], win(k, *me), local_sems.at[k]) for k in range(K)]
        first = []
        for k in range(K):
            first.append(copy(k, 0, me, sibling, src=x_refs[k]))
            first += [copy(k, 1 + j, me, (*chip, mc), src=x_refs[k]) for j, chip in enumerate(chips)]
        return me, sibling, chips, copy, mine, first

    def start(x_refs, o_refs, sems):
        _, _, _, _, mine, first = plan(x_refs, o_refs, sems)
        for cp in mine + first:
            cp.start()

    def wait(x_refs, o_refs, sems):
        me, sibling, chips, copy, mine, first = plan(x_refs, o_refs, sems)
        mc = me[2]
        passed = []
        for j, chip in enumerate(chips):
            for k in range(K):
                copy(k, 1 + j, (*chip, mc), me).wait_recv()
                passed.append(copy(k, 4 + j, (*chip, mc), sibling))
                passed[-1].start()
        for k in range(K):
            copy(k, 0, sibling, me).wait_recv()
        for j, chip in enumerate(chips):
            for k in range(K):
                copy(k, 4 + j, (*chip, 1 - mc), me).wait_recv()
        for cp in first + passed:
            cp.wait_send()
        for cp in mine:
            cp.wait()

    scratch = [pltpu.SemaphoreType.DMA((7 * K,)), pltpu.SemaphoreType.DMA((7 * K,)), pltpu.SemaphoreType.DMA((K,))]
    return Rider(list(shards), out_shape, scratch, start, wait)


def exchange_rider(items):
    ns = len(items)
    out_shape = [_S((N_DEV,) + tuple(it[3]), it[0].dtype) for it in items]

    def plan(src_refs, o_refs, sems):
        send_sems, recv_sems, local_sems = sems
        mx, my, mc = lax.axis_index("x"), lax.axis_index("y"), lax.axis_index("c")
        me = 4 * mx + 2 * my + mc
        remote, own = [], []
        for s, (_, ax, n, _) in enumerate(items):
            own.append(pltpu.make_async_copy(_window(src_refs[s], ax, me, n), o_refs[s].at[me], local_sems.at[s]))
            for k in range(1, N_DEV):
                px = 1 - mx if k & 4 else mx
                py = 1 - my if k & 2 else my
                pc = 1 - mc if k & 1 else mc
                remote.append(pltpu.make_async_remote_copy(
                    src_ref=_window(src_refs[s], ax, 4 * px + 2 * py + pc, n), dst_ref=o_refs[s].at[me],
                    send_sem=send_sems.at[7 * s + k - 1], recv_sem=recv_sems.at[7 * s + k - 1],
                    device_id=(px, py, pc), device_id_type=MESH_ID))
        return remote, own

    def start(src_refs, o_refs, sems):
        remote, own = plan(src_refs, o_refs, sems)
        for cp in own + remote:
            cp.start()

    def wait(src_refs, o_refs, sems):
        remote, own = plan(src_refs, o_refs, sems)
        for cp in remote + own:
            cp.wait()

    scratch = [pltpu.SemaphoreType.DMA((7 * ns,)), pltpu.SemaphoreType.DMA((7 * ns,)), pltpu.SemaphoreType.DMA((ns,))]
    return Rider([it[0] for it in items], out_shape, scratch, start, wait)


def reduce_adamw(rcvs, w, m, v, name):
    L = len(rcvs)
    _, A, B, C = rcvs[0].shape
    tb = B
    while tb > 8 and tb * C > (1 << 17):
        tb //= 2

    def body(*refs):
        r_refs, (w_ref, m_ref, v_ref, g_ref, d_ref, mo_ref, vo_ref) = refs[:L], refs[L:]
        for k in range(L):
            @pl.when(pl.program_id(0) == k)
            def _(k=k):
                g = r_refs[k][0].astype(F32)
                for d in range(1, N_DEV):
                    g = g + r_refs[k][d].astype(F32)
                mn = ADAM_B1 * m_ref[...] + (1.0 - ADAM_B1) * g
                vn = ADAM_B2 * v_ref[...] + (1.0 - ADAM_B2) * jnp.square(g)
                m_hat = mn / (1.0 - ADAM_B1 ** ADAM_STEP)
                v_hat = vn / (1.0 - ADAM_B2 ** ADAM_STEP)
                g_ref[...] = g
                d_ref[...] = -ADAM_LR * (m_hat / (jnp.sqrt(v_hat) + ADAM_EPS) + ADAM_WD * w_ref[...])
                mo_ref[...] = mn
                vo_ref[...] = vn

    def rspec(k):
        return pl.BlockSpec((N_DEV, None, tb, C), lambda l, a, i: (0, jnp.where(l == k, a, 0), jnp.where(l == k, i, 0), 0))

    blk = pl.BlockSpec((None, tb, C), lambda l, a, i: (l * A + a, i, 0))
    return _pcall(body, grid=(L, A, B // tb), in_specs=[rspec(k) for k in range(L)] + [blk, blk, blk],
                  out_specs=[blk] * 4, out_shape=[_S((L * A, B, C))] * 4, name=name)(*rcvs, w, m, v)


def _w_in_pieces(g0, g1):
    per = D_IN // N_DEV
    return [(d, max(g0, d * per) - d * per, min(g1, (d + 1) * per) - d * per) for d in range(N_DEV) if max(g0, d * per) < min(g1, (d + 1) * per)]


def repack_w_in(w8, name):
    _, L, R, per = w8.shape
    tr = 256

    def cols(x_ref, g0, g1):
        return [x_ref[d, :, a:b] for d, a, b in _w_in_pieces(g0, g1)]

    def body(x_ref, *o_refs):
        for (name_, i), o_ref in zip(SEGS, o_refs):
            o_ref[...] = jnp.concatenate(cols(x_ref, _OFF[i], _OFF[i + 1]), axis=1)
        parts, at = [], 0
        for i, lane0 in SMALL_SRC:
            assert lane0 == at
            parts += cols(x_ref, _OFF[i], _OFF[i + 1])
            at += IN_SIZES[i]
        parts.append(jnp.zeros((tr, LANES - at), w8.dtype))
        o_refs[-1][...] = jnp.concatenate(parts, axis=1)

    widths = [IN_SIZES[i] for _, i in SEGS] + [LANES]
    outs = _pcall(body, grid=(L, R // tr), in_specs=[pl.BlockSpec((N_DEV, None, tr, per), lambda l, r: (0, l, r, 0))],
                  out_specs=[pl.BlockSpec((None, tr, w), lambda l, r: (l, r, 0)) for w in widths],
                  out_shape=[_S((L, R, w), w8.dtype) for w in widths], name=name)(w8)
    return dict(zip(SEG_NAMES, outs))


def repack_dw_in(dseg, name):
    R = dseg["z"].shape[0]
    per = D_IN // N_DEV
    tr = 128
    src = {i: (k, 0) for k, (_, i) in enumerate(SEGS)}
    src.update({i: (len(SEGS), lane0) for i, lane0 in SMALL_SRC})

    def body(*refs):
        s_refs, o_ref = refs[:-1], refs[-1]
        for d in range(N_DEV):
            parts = []
            for i in range(len(IN_SIZES)):
                g0, g1 = max(_OFF[i], d * per), min(_OFF[i + 1], (d + 1) * per)
                if g0 < g1:
                    k, c0 = src[i]
                    parts.append(s_refs[k][:, c0 + g0 - _OFF[i]:c0 + g1 - _OFF[i]])
            o_ref[d] = jnp.concatenate(parts, axis=1)

    arrs = [dseg[n] for n in SEG_NAMES]
    return _pcall(body, grid=(R // tr,), in_specs=[pl.BlockSpec((tr, a.shape[1]), lambda r: (r, 0)) for a in arrs],
                  out_specs=pl.BlockSpec((N_DEV, tr, per), lambda r: (0, r, 0)), out_shape=_S((N_DEV, R, per), arrs[0].dtype), name=name)(*arrs)


WEIGHTS = ("ln_in_g", "ln_in_b", "w_in", "ssd_conv_w", "ssd_conv_b", "ssd_dt_bias", "ssd_a_log", "ssd_d", "ssd_norm_w", "dn_conv_w",
           "dn_a_log", "dn_dt_bias", "dn_norm_w", "sg_ln_g", "sg_ln_b", "sg_w", "sg_b", "fox_f_bias", "gate_b", "w_branch", "w_out",
           "ln1_g", "ln1_b", "w_up", "w_down", "ln2_g", "ln2_b")
SHARDED = {"w_in": 2, "ssd_conv_w": 2, "dn_conv_w": 2, "gate_b": 2, "w_branch": 3, "w_out": 1, "w_up": 2, "w_down": 1}
SLABBED = ("w_in", "dn_conv_w")
MATMUL_WEIGHTS = ("w_in", "w_branch", "w_out", "w_up", "w_down")
REPLICATED_ENTRY = ("ln_in_g", "ln_in_b")
REPLICATED_LAYER = tuple(n for n in WEIGHTS if n not in SHARDED and n not in REPLICATED_ENTRY and n != "sg_w")
SEG_NAMES = tuple(n for n, _ in SEGS) + ("small",)
PACK_COLS = 1024


def _lanes(vec, off):
    return jnp.pad(vec, (off, LANES - off - vec.shape[0]))[None]


def _pack_small(parts, row_mult):
    flat = jnp.concatenate([q.reshape(-1) for q in parts])
    rows = -(-flat.shape[0] // (PACK_COLS * row_mult)) * row_mult
    return jnp.pad(flat, (0, rows * PACK_COLS - flat.shape[0])).reshape(1, rows, PACK_COLS)


EARLY = ("w_branch", "w_out", "w_up", "w_down", "gate_b")
LATE = ("w_in", "ssd_conv_w", "dn_conv_w")
WHOLE = ("sg_w",)


def _gather_rider(p, l, names):
    shards, axes = [], []
    for n in names:
        s = p[n][l:l + 1]
        s = s.astype(BF) if n in MATMUL_WEIGHTS else s
        shards.append(s[None] if n in SLABBED else s)
        axes.append(0 if n in SLABBED else SHARDED[n])
    return gather_rider(shards, axes)


def _exchange_items(g, p, names):
    items = []
    for n in names:
        local = p[n].shape[1:]
        if n in WHOLE:
            items.append((g[n], 0, -1, local))
        elif n in SLABBED:
            items.append((g[n], 0, 0, local))
        else:
            items.append((g[n], SHARDED[n] - 1, local[SHARDED[n] - 1], local))
    return items


def _use_gathered(w, names, arrays, l):
    for n, arr in zip(names, arrays):
        if n == "w_in":
            w[n] = repack_w_in(arr, f"w_in_repack_{l}")
        elif n == "ssd_conv_w":
            w["ssd_cw"] = arr[0]
        elif n == "dn_conv_w":
            w["dn_cw"] = jnp.moveaxis(arr[:, 0], 0, 1).reshape(4, 3 * BRANCH_W)
        elif n == "gate_b":
            w[n] = arr[0]
        else:
            w[n] = arr


def _layer_weights(p, l):
    w = {}
    w["ssd_cb"] = p["ssd_conv_b"][l][None]
    w["dn_cb"] = jnp.zeros((1, 3 * BRANCH_W), F32)
    w["ssd_ps"] = [_lanes(p["ssd_dt_bias"][l], DT0), _lanes(p["ssd_a_log"][l], DT0), _lanes(p["ssd_d"][l], DT0), p["ssd_norm_w"][l][None]]
    w["dn_ps"] = [_lanes(p["dn_a_log"][l], A0), _lanes(p["dn_dt_bias"][l], A0), p["dn_norm_w"][l][None]]
    w["sg_ps"] = [p["sg_ln_g"][l][None], p["sg_ln_b"][l][None], p["sg_w"][l], jnp.pad(p["sg_b"][l].T, ((0, 0), (0, LANES - 4)))]
    w["fox_ps"] = [_lanes(p["fox_f_bias"][l], FF0)]
    for n in ("ln1_g", "ln1_b", "ln2_g", "ln2_b"):
        w[n] = p[n][l][None]
    return w


def _scan_specs(T, a):
    c0 = lambda c, h: (c, 0)
    sr = min(T, SSD_ROWS)
    ssd = dict(f=ssd_chunk, xs=[(a["z"], (sr, 512), c0), (a["xbc_act"], (sr, 1024), c0), (a["small"], (sr, LANES), c0)],
               ys=[((T, BRANCH_W), (sr, BRANCH_W), c0)], state=(4, LANES, LANES), nc=T // sr, nh=1, shared=())
    dr = min(T, DN_ROWS)
    dn = dict(f=dn_chunk, xs=[(a["dn_act"], (dr, 3 * BRANCH_W), c0), (a["dngate"], (dr, BRANCH_W), c0), (a["small"], (dr, LANES), c0)],
              ys=[((T, BRANCH_W), (dr, BRANCH_W), c0)], state=(4, LANES, LANES), nc=T // dr, nh=1, shared=())
    sg = dict(f=sg_chunk, xs=[(a["sguv"], (128, 1024), c0)], ys=[((T, BRANCH_W), (128, BRANCH_W), c0)], state=(1, 8, LANES), nc=T // 128, nh=1, shared=())
    fc = dict(f=foxc_chunk, xs=[(a["small"], (128, LANES), c0)],
              ys=[((T, LANES), (128, LANES), c0)], state=(1, 1, LANES), nc=T // 128, nh=1, shared=())
    return ssd, dn, sg, fc


def _layer_fwd(h, w, l, dn_rider=None, fox_rider=None):
    T = h.shape[0]
    a = {"h": h, **proj_all(h, w["w_in"], f"proj_{l}")}
    a["xbc_act"] = conv_fwd(a["xbc"], w["ssd_cw"], w["ssd_cb"], f"ssd_conv_{l}")
    a["dn_act"] = conv_fwd(a["dnqkv"], w["dn_cw"], w["dn_cb"], f"dn_conv_{l}")
    ssd, dn, sg, fc = _scan_specs(T, a)
    (a["ya"], a["ssd_st"]), _ = scan_fwd(f"ssd_fwd_{l}", ssd["f"], ssd["xs"], w["ssd_ps"], ssd["ys"], ssd["state"], ssd["nc"], ssd["nh"])
    (a["yb"], a["dn_st"]), got = scan_fwd(f"dn_fwd_{l}", dn["f"], dn["xs"], w["dn_ps"], dn["ys"], dn["state"], dn["nc"], dn["nh"], rider=dn_rider)
    _use_gathered(w, EARLY, got, l)
    (a["yc"], a["sg_st"]), _ = scan_fwd(f"sg_fwd_{l}", sg["f"], sg["xs"], w["sg_ps"], sg["ys"], sg["state"], sg["nc"], sg["nh"])
    (a["ccol"], a["fc_st"]), _ = scan_fwd(f"foxc_fwd_{l}", fc["f"], fc["xs"], w["fox_ps"], fc["ys"], fc["state"], fc["nc"], fc["nh"])
    a["fox_qa"], a["fox_ka"], a["fox_va"] = fox_prep(a["foxqkv"], a["ccol"], f"fox_prep_{l}")
    (a["yd"], a["lse"]), carried = fox_fwd(a["fox_qa"], a["fox_ka"], a["fox_va"], f"fox_fwd_{l}", rider=fox_rider)
    a["merged"] = merge_fwd([a["ya"], a["yb"], a["yc"], a["yd"]], a["gates"], w["gate_b"], w["w_branch"], 0, f"merge_fwd_{l}")
    a["u1"], a["h1"] = out_fwd(a["merged"], h, w["w_out"], 0, w["ln1_g"], w["ln1_b"], f"out_fwd_{l}")
    a["u2"], a["h2"] = ff_fwd(a["h1"], w["w_up"], w["w_down"], 0, w["ln2_g"], w["ln2_b"], f"ff_fwd_{l}")
    return a, carried


def _layer_bwd(dh2, a, w, l, p, late_above):
    T = dh2.shape[0]
    g = {}
    du2, dh1, da, r, dg2, db2 = ff_bwd(a["u2"], dh2, a["h1"], w["ln2_g"], w["ln2_b"], w["w_up"], w["w_down"], 0, f"ff_bwd_{l}")
    g["ln2_g"], g["ln2_b"] = dg2[0], db2[0]
    g["w_up"] = matmul_dw(transpose_bf16(a["h1"], f"h1_t_{l}"), da, f"dwup_{l}")
    g["w_down"] = matmul_dw(r, du2, f"dwdown_{l}")
    du1, dmerged, dg1, db1 = out_bwd(a["u1"], dh1, w["ln1_g"], w["ln1_b"], w["w_out"], 0, f"out_bwd_{l}")
    g["ln1_g"], g["ln1_b"] = dg1[0], db1[0]
    g["w_out"] = matmul_dw(transpose_bf16(a["merged"], f"merged_t_{l}"), du1, f"dwout_{l}")
    ys = [a["ya"], a["yb"], a["yc"], a["yd"]]
    dya, dyb, dyc, dyd, dgl, dz, dgb = merge_bwd(ys, a["gates"], w["gate_b"], w["w_branch"], 0, dmerged, f"merge_bwd_{l}")
    g["gate_b"] = dgb
    g["w_branch"] = jnp.stack([matmul_tn(ys[i], dz, f"dwb{i}_{l}", b_col0=i * D_MODEL, n_cols=D_MODEL, out_dtype=BF) for i in range(4)])
    early = exchange_rider(_exchange_items(g, p, EARLY))
    dn_rider = early if late_above is None else exchange_rider(late_above)
    fox_rider = None if late_above is None else early
    ssd, dn, sg, fc = _scan_specs(T, a)
    (dz_ssd, dxbc_act, dsm_ssd, d_dtb, d_alog, d_dsk, d_nw), _ = scan_bwd(f"ssd_bwd_{l}", ssd["f"], ssd["xs"], w["ssd_ps"], ssd["ys"], [dya], a["ssd_st"],
                                                                           ssd["state"], ssd["nc"], ssd["nh"])
    g["ssd_dt_bias"], g["ssd_a_log"], g["ssd_d"], g["ssd_norm_w"] = d_dtb[0, DT0:DT0 + 8], d_alog[0, DT0:DT0 + 8], d_dsk[0, DT0:DT0 + 8], d_nw[0]
    dxbc, g["ssd_conv_w"], dcb = conv_bwd(a["xbc"], w["ssd_cw"], w["ssd_cb"], dxbc_act, f"ssd_conv_bwd_{l}")
    g["ssd_conv_b"] = dcb[0]
    (ddn_act, ddngate, dsm_dn, d_alog, d_dtb, d_nw), got_dn = scan_bwd(f"dn_bwd_{l}", dn["f"], dn["xs"], w["dn_ps"], dn["ys"], [dyb], a["dn_st"],
                                                                        dn["state"], dn["nc"], dn["nh"], rider=dn_rider)
    g["dn_a_log"], g["dn_dt_bias"], g["dn_norm_w"] = d_alog[0, A0:A0 + 4], d_dtb[0, A0:A0 + 4], d_nw[0]
    ddnqkv, g["dn_conv_w"], _ = conv_bwd(a["dnqkv"], w["dn_cw"], w["dn_cb"], ddn_act, f"dn_conv_bwd_{l}")
    (dsguv, d_lng, d_lnb, d_w, d_bt), _ = scan_bwd(f"sg_bwd_{l}", sg["f"], sg["xs"], w["sg_ps"], sg["ys"], [dyc], a["sg_st"], sg["state"], sg["nc"], sg["nh"])
    g["sg_ln_g"], g["sg_ln_b"], g["sg_w"], g["sg_b"] = d_lng[0], d_lnb[0], d_w, d_bt[:, :4].T
    qb, doa, qd, kd = fox_prep_bwd(a["foxqkv"], a["fox_qa"], a["yd"], a["lse"], dyd, f"fox_prep_bwd_{l}")
    (dfq, dfk, dfv, dccol), got_fox = fox_bwd(qb, a["fox_ka"], a["fox_va"], doa, qd, kd, f"fox_bwd_{l}", rider=fox_rider)
    (dsm_fox, d_fb), _ = scan_bwd(f"foxc_bwd_{l}", fc["f"], fc["xs"], w["fox_ps"], fc["ys"], [dccol], a["fc_st"], fc["state"], fc["nc"], fc["nh"])
    g["fox_f_bias"] = d_fb[0, FF0:FF0 + 8]
    dseg = {"z": dz_ssd, "xbc": dxbc, "dnqkv": ddnqkv, "dngate": ddngate, "sguv": dsguv,
            "foxqkv": jnp.concatenate([dfq, dfk, dfv], axis=1), "gates": dgl, "small": add3(dsm_ssd, dsm_dn, dsm_fox, f"dsmall_{l}")}
    h_t = transpose_bf16(a["h"], f"h_t_{l}")
    dwin = {n: matmul_dw(h_t, dseg[n], f"dwin_{n}_{l}") for n in SEG_NAMES}
    g["w_in"] = repack_dw_in(dwin, f"dw_in_repack_{l}")
    g["dn_conv_w"] = jnp.moveaxis(g["dn_conv_w"].reshape(4, N_DEV, 3 * BRANCH_W // N_DEV), 1, 0)
    got = {(EARLY, l): got_dn} if late_above is None else {(LATE + WHOLE, l + 1): got_dn, (EARLY, l): got_fox}
    return dseg, du1, g, got


def proj_all(h, w_in, name):
    T = h.shape[0]
    tm = min(T, 256)
    ns = len(SEG_NAMES)

    def body(*refs):
        h_ref, w_refs, o_refs = refs[0], refs[1:1 + ns], refs[1 + ns:]
        hb = h_ref[...].astype(BF)
        for w_ref, o_ref in zip(w_refs, o_refs):
            o_ref[...] = _dotb(hb, w_ref[...], NN)

    widths = [w_in[n].shape[2] for n in SEG_NAMES]
    in_specs = [_row(tm, D_MODEL)]
    in_specs += [pl.BlockSpec((None,) + w_in[n].shape[1:], lambda i: (0, 0, 0), pipeline_mode=pl.Buffered(1)) for n in SEG_NAMES]
    outs = _pcall(body, grid=(T // tm,), in_specs=in_specs, out_specs=[_row(tm, wd) for wd in widths],
                  out_shape=[_S((T, wd)) for wd in widths], name=name)(h, *[w_in[n] for n in SEG_NAMES])
    return dict(zip(SEG_NAMES, outs))


def dh_all(dseg, w_in, add, name, rider=None, norm=None):
    T = add.shape[0]
    tm = min(T, 256)
    ns = len(SEG_NAMES)

    def body(*refs):
        d_refs, w_refs, add_ref = refs[:ns], refs[ns:2 * ns], refs[2 * ns]
        acc = ALPHA * add_ref[...]
        for d_ref, w_ref in zip(d_refs, w_refs):
            acc = acc + _dotb(d_ref[...].astype(BF), w_ref[...], NT)
        if norm is None:
            refs[2 * ns + 1][...] = acc
        else:
            x_ref, g_ref, b_ref, dx_ref, dg_ref, db_ref = refs[2 * ns + 1:]
            _, vjp = jax.vjp(_ln, x_ref[...], g_ref[...], b_ref[...])
            dx, dg, db = vjp(acc)
            dx_ref[...] = dx
            first = pl.program_id(0) == 0
            _acc_out(dg_ref, dg, first)
            _acc_out(db_ref, db, first)

    C = D_MODEL
    in_specs = [_row(tm, dseg[n].shape[1]) for n in SEG_NAMES]
    in_specs += [pl.BlockSpec((None,) + w_in[n].shape[1:], lambda i: (0, 0, 0), pipeline_mode=pl.Buffered(1)) for n in SEG_NAMES]
    in_specs.append(_row(tm, C))
    args = [*[dseg[n] for n in SEG_NAMES], *[w_in[n] for n in SEG_NAMES], add]
    out_specs, out_shape = [_row(tm, C)], [_S((T, C))]
    if norm is not None:
        in_specs += [_row(tm, C), _full((1, C)), _full((1, C))]
        args += list(norm)
        out_specs += [_full((1, C)), _full((1, C))]
        out_shape += [_S((1, C)), _S((1, C))]
    return hosted_call(body, rider, grid=(T // tm,), in_specs=in_specs, out_specs=out_specs, out_shape=out_shape,
                       scratch_shapes=[], name=name, args=args)


def kernel(x, ln_in_g, ln_in_b, w_in, ssd_conv_w, ssd_conv_b, ssd_dt_bias, ssd_a_log, ssd_d, ssd_norm_w, dn_conv_w, dn_a_log, dn_dt_bias, dn_norm_w, sg_ln_g, sg_ln_b, sg_w, sg_b, fox_f_bias, gate_b, w_branch, w_out, ln1_g, ln1_b, w_up, w_down, ln2_g, ln2_b, loss_target, m_ln_in_g, m_ln_in_b, m_w_in, m_ssd_conv_w, m_ssd_conv_b, m_ssd_dt_bias, m_ssd_a_log, m_ssd_d, m_ssd_norm_w, m_dn_conv_w, m_dn_a_log, m_dn_dt_bias, m_dn_norm_w, m_sg_ln_g, m_sg_ln_b, m_sg_w, m_sg_b, m_fox_f_bias, m_gate_b, m_w_branch, m_w_out, m_ln1_g, m_ln1_b, m_w_up, m_w_down, m_ln2_g, m_ln2_b, v_ln_in_g, v_ln_in_b, v_w_in, v_ssd_conv_w, v_ssd_conv_b, v_ssd_dt_bias, v_ssd_a_log, v_ssd_d, v_ssd_norm_w, v_dn_conv_w, v_dn_a_log, v_dn_dt_bias, v_dn_norm_w, v_sg_ln_g, v_sg_ln_b, v_sg_w, v_sg_b, v_fox_f_bias, v_gate_b, v_w_branch, v_w_out, v_ln1_g, v_ln1_b, v_w_up, v_w_down, v_ln2_g, v_ln2_b):
    args = dict(locals())
    p = {n: args[n] for n in WEIGHTS}
    xt, target = x[0], loss_target[0]
    ws, acts = [_layer_weights(p, l) for l in range(DEPTH)], []
    (h,), gathered = ln_fwd(xt, ln_in_g[None], ln_in_b[None], "ln_in_fwd", rider=_gather_rider(p, 0, LATE))
    _use_gathered(ws[0], LATE, gathered, 0)
    for l in range(DEPTH):
        a, gathered = _layer_fwd(h, ws[l], l, dn_rider=_gather_rider(p, 0, EARLY) if l == 0 else None,
                                 fox_rider=_gather_rider(p, l + 1, LATE + EARLY) if l + 1 < DEPTH else None)
        if l + 1 < DEPTH:
            _use_gathered(ws[l + 1], LATE + EARLY, gathered, l + 1)
        acts.append(a)
        h = a["h2"]
    dh, loss = loss_head(h, target, "loss_head")
    loss = lax.psum(loss[0, 0], ("x", "y", "c"))

    layer_grads, got, late = [None] * DEPTH, {}, None
    for l in reversed(range(DEPTH)):
        dseg, du1, layer_grads[l], got_l = _layer_bwd(dh, acts[l], ws[l], l, p, late)
        got.update(got_l)
        late = _exchange_items(layer_grads[l], p, LATE + WHOLE)
        if l > 0:
            (dh,), _ = dh_all(dseg, ws[l]["w_in"], du1, f"dh_{l}")
    pack = _pack_small([jnp.stack([layer_grads[k][n] for k in range(DEPTH)]) for n in REPLICATED_LAYER], 8)
    (grad_x, dg_in, db_in), carried = dh_all(dseg, ws[0]["w_in"], du1, "dh_0", rider=exchange_rider(late + [(pack[0], 0, -1, pack.shape[1:])]),
                                             norm=(xt, ln_in_g[None], ln_in_b[None]))
    got[(LATE + WHOLE, 0)], got_layer_pack = carried[:-1], carried[-1]
    pack = _pack_small([dg_in[0], db_in[0]], 8)
    got_entry_pack = comm_call(exchange_rider([(pack[0], 0, -1, pack.shape[1:])]), "grads_exchange_entry_norm")[0]
    rcv = {(n, l): arr for (names, l), arrs in got.items() for n, arr in zip(names, arrs)}

    res = [{}, {}, {}, {}]
    for n in tuple(SHARDED) + WHOLE:
        shp = p[n].shape
        lead = math.prod(shp[1:-2])
        to3 = lambda t: t.reshape((-1,) + shp[-2:])
        outs = reduce_adamw([rcv[(n, l)].reshape((N_DEV, lead) + shp[-2:]) for l in range(DEPTH)],
                            to3(p[n]), to3(args["m_" + n]), to3(args["v_" + n]), f"adamw_{n}")
        for k in range(4):
            res[k][n] = outs[k].reshape(shp)
    for names, got_pack, rows, name in ((REPLICATED_LAYER, got_layer_pack, 8, "adamw_replicated"), (REPLICATED_ENTRY, got_entry_pack, 8, "adamw_entry_norm")):
        outs = reduce_adamw([got_pack[:, None]], _pack_small([p[n] for n in names], rows), _pack_small([args["m_" + n] for n in names], rows),
                            _pack_small([args["v_" + n] for n in names], rows), name)
        off = 0
        for n in names:
            shp = p[n].shape
            cnt = math.prod(shp)
            for k in range(4):
                res[k][n] = outs[k].reshape(-1)[off:off + cnt].reshape(shp)
            off += cnt
    return (loss, grad_x[None], *[res[0][n] for n in WEIGHTS], *[res[1][n] for n in WEIGHTS],
            *[res[2][n] for n in WEIGHTS], *[res[3][n] for n in WEIGHTS])
```

```python
import collections
import functools
import math

import jax
import jax.numpy as jnp
from jax import lax
from jax.experimental import pallas as pl
from jax.experimental.pallas import tpu as pltpu

F32 = jnp.float32
BF = jnp.bfloat16

D_MODEL = 1024
DEPTH = 2
BRANCH_W = 512
D_FF = 4096
LN_EPS = 1e-5
NORM_EPS = 1e-6
ALPHA = (2 * DEPTH) ** 0.25
N_DEV = 8
LANES = 128
ADAM_LR, ADAM_B1, ADAM_B2, ADAM_EPS, ADAM_WD, ADAM_STEP = 0.001, 0.9, 0.999, 1e-08, 0.01, 10

DT0, BETA0, A0, FF0 = 0, 8, 12, 16
IN_SIZES = (512, 1024, 8, 1536, 4, 4, 512, 1024, 1536, 8, 4096)
_OFF = [0]
for _s in IN_SIZES:
    _OFF.append(_OFF[-1] + _s)
D_IN = _OFF[-1]
SEGS = (("z", 0), ("xbc", 1), ("dnqkv", 3), ("dngate", 6), ("sguv", 7), ("foxqkv", 8), ("gates", 10))
SMALL_SRC = ((2, DT0), (4, BETA0), (5, A0), (9, FF0))

NN = ((1,), (0,))
NT = ((1,), (1,))
TN = ((0,), (0,))
_DIMS = {"nn": NN, "nt": NT, "tn": TN}


def _pcall(body, **kw):
    return pl.pallas_call(body, **kw)


def _S(shape, dtype=F32):
    return jax.ShapeDtypeStruct(tuple(shape), dtype)


def _iota(shape, dim):
    return lax.broadcasted_iota(jnp.int32, shape, dim)


def _dotb(a, b, dims):
    return lax.dot_general(a, b, (dims, ((), ())), preferred_element_type=F32)


def _split2(a):
    ah = a.astype(BF)
    return ah, (a - ah.astype(F32)).astype(BF)


def _split3(a):
    a1 = a.astype(BF)
    r = a - a1.astype(F32)
    a2 = r.astype(BF)
    a3 = (r - a2.astype(F32)).astype(BF)
    return a1, a2, a3


def _mm_raw(a, b, form, mode):
    d = _DIMS[form]
    if mode == "1":
        return _dotb(a.astype(BF), b.astype(BF), d)
    if mode == "3":
        ah, al = _split2(a)
        bh, bl = _split2(b)
        return _dotb(ah, bh, d) + (_dotb(ah, bl, d) + _dotb(al, bh, d))
    if mode == "xa":
        ab = a.astype(BF)
        b1, b2, b3 = _split3(b)
        return _dotb(ab, b1, d) + (_dotb(ab, b2, d) + _dotb(ab, b3, d))
    bb = b.astype(BF)
    a1, a2, a3 = _split3(a)
    return _dotb(a1, bb, d) + (_dotb(a2, bb, d) + _dotb(a3, bb, d))


@functools.partial(jax.custom_vjp, nondiff_argnums=(2, 3))
def mm(a, b, form, mode):
    return _mm_raw(a, b, form, mode)


def _mm_fwd(a, b, form, mode):
    return _mm_raw(a, b, form, mode), (a, b)


_XA_DB = {"nn": "xa", "nt": "xb", "tn": "xa"}
_XB_DA = {"nn": "xb", "nt": "xb", "tn": "xa"}


def _mm_bwd(form, mode, res, g):
    a, b = res
    ma = _XB_DA[form] if mode == "xb" else mode
    mb = _XA_DB[form] if mode == "xa" else mode
    da = db = None
    if mode != "xa":
        da = {"nn": lambda: mm(g, b, "nt", ma), "nt": lambda: mm(g, b, "nn", ma), "tn": lambda: mm(b, g, "nt", ma)}[form]()
    if mode != "xb":
        db = {"nn": lambda: mm(a, g, "tn", mb), "nt": lambda: mm(g, a, "tn", mb), "tn": lambda: mm(a, g, "nn", mb)}[form]()
    if da is None:
        da = jnp.zeros_like(a)
    if db is None:
        db = jnp.zeros_like(b)
    return da, db


mm.defvjp(_mm_fwd, _mm_bwd)


def _silu(x):
    return x * jax.nn.sigmoid(x)


def _ln(x, g, b):
    mu = jnp.mean(x, -1, keepdims=True)
    xc = x - mu
    var = jnp.mean(xc * xc, -1, keepdims=True)
    return xc * lax.rsqrt(var + LN_EPS) * g + b


def _pick(n, cap):
    if n <= cap:
        return n
    best = LANES
    for t in range(LANES, cap + 1, LANES):
        if n % t == 0:
            best = t
    return best


def transpose_bf16(a, name):
    T, C = a.shape
    tt = min(T, 512)

    def body(a_ref, o_ref):
        o_ref[...] = a_ref[...].T.astype(BF)

    return _pcall(body, grid=(T // tt,), in_specs=[pl.BlockSpec((tt, C), lambda t: (t, 0))], out_specs=pl.BlockSpec((C, tt), lambda t: (0, t)),
                  out_shape=_S((C, T), BF), name=name)(a)


def matmul_dw(a_t, b, name):
    M, K = a_t.shape
    N = b.shape[1]
    tm, tn, tk = min(M, 1024), _pick(N, 1024), _pick(K, 1024)
    nk = K // tk

    def body(a_ref, b_ref, o_ref, acc):
        k = pl.program_id(2)
        p = _dotb(a_ref[...], b_ref[...].astype(BF), NN)

        @pl.when(k == 0)
        def _():
            acc[...] = p

        @pl.when(k > 0)
        def _():
            acc[...] += p

        @pl.when(k == nk - 1)
        def _():
            o_ref[...] = acc[...].astype(BF)

    return _pcall(body, grid=(N // tn, M // tm, nk),
                  in_specs=[pl.BlockSpec((tm, tk), lambda j, i, k: (i, k)), pl.BlockSpec((tk, tn), lambda j, i, k: (k, j))],
                  out_specs=pl.BlockSpec((tm, tn), lambda j, i, k: (i, j)), out_shape=_S((M, N), BF),
                  scratch_shapes=[pltpu.VMEM((tm, tn), F32)], name=name)(a_t, b)


def matmul_tn(a, b, name, b_col0=0, n_cols=None, out_dtype=F32):
    T, M = a.shape
    N = b.shape[1] if n_cols is None else n_cols
    tm, tn, tt = _pick(M, 512), _pick(N, 1024), min(T, 512)
    nt = T // tt
    jb = b_col0 // tn

    def body(a_ref, b_ref, o_ref, acc):
        t = pl.program_id(2)
        p = _dotb(a_ref[...].astype(BF), b_ref[...].astype(BF), TN)

        @pl.when(t == 0)
        def _():
            acc[...] = p

        @pl.when(t > 0)
        def _():
            acc[...] += p

        @pl.when(t == nt - 1)
        def _():
            o_ref[...] = acc[...].astype(out_dtype)

    return _pcall(body, grid=(M // tm, N // tn, nt),
                  in_specs=[pl.BlockSpec((tt, tm), lambda i, j, t: (t, i)), pl.BlockSpec((tt, tn), lambda i, j, t: (t, jb + j))],
                  out_specs=pl.BlockSpec((tm, tn), lambda i, j, t: (i, j)), out_shape=_S((M, N), out_dtype),
                  scratch_shapes=[pltpu.VMEM((tm, tn), F32)], name=name)(a, b)


def _pieces(v):
    if v.ndim == 3:
        return [v[i] for i in range(v.shape[0])]
    n = v.shape[1] // LANES
    if n <= 1:
        return [v]
    return [v[:, i * LANES:(i + 1) * LANES] for i in range(n)]


def _join(ps, like_ndim):
    if like_ndim == 3:
        return jnp.stack(ps, axis=0)
    return ps[0] if len(ps) == 1 else jnp.concatenate(ps, axis=1)


def scan_fwd(name, f, xs, ps, ys, state_shape, nc, nh=1, rider=None):
    nx, npar, ny = len(xs), len(ps), len(ys)

    def body(*refs):
        x_refs, p_refs = refs[:nx], refs[nx:nx + npar]
        y_refs = refs[nx + npar:nx + npar + ny]
        st_out, st = refs[nx + npar + ny], refs[nx + npar + ny + 1]
        c, h = pl.program_id(0), pl.program_id(1)

        @pl.when(c == 0)
        def _():
            st[h] = jnp.zeros(state_shape, F32)

        S = st[h]
        st_out[...] = S
        yv, Sn = f([_pieces(r[...]) for r in x_refs], [_pieces(r[...]) for r in p_refs], _pieces(S), h)
        for r, v in zip(y_refs, yv):
            r[...] = _join(v, 2)
        st[h] = _join(Sn, 3)

    in_specs = [pl.BlockSpec(bs, im) for (_, bs, im) in xs]
    in_specs += [pl.BlockSpec(p.shape, (lambda c, h, n=p.ndim: (0,) * n)) for p in ps]
    out_specs = [pl.BlockSpec(bs, im) for (_, bs, im) in ys]
    out_specs.append(pl.BlockSpec((None, None) + tuple(state_shape), lambda c, h: (c, h, 0, 0, 0)))
    out_shape = [_S(s) for (s, _, _) in ys] + [_S((nc, nh) + tuple(state_shape))]
    return hosted_call(body, rider, grid=(nc, nh), in_specs=in_specs, out_specs=out_specs, out_shape=out_shape,
                       scratch_shapes=[pltpu.VMEM((nh,) + tuple(state_shape), F32)], name=name, args=[*[x[0] for x in xs], *ps])


def scan_bwd(name, f, xs, ps, ys, dys, states, state_shape, nc, nh=1, shared=(), rider=None):
    nx, npar, ny = len(xs), len(ps), len(ys)

    def body(*refs):
        x_refs, p_refs = refs[:nx], refs[nx:nx + npar]
        s_ref = refs[nx + npar]
        dy_refs = refs[nx + npar + 1:nx + npar + 1 + ny]
        o = nx + npar + 1 + ny
        dx_refs, dp_refs, dst = refs[o:o + nx], refs[o + nx:o + nx + npar], refs[o + nx + npar]
        c, h = pl.program_id(0), pl.program_id(1)

        @pl.when(c == 0)
        def _():
            dst[h] = jnp.zeros(state_shape, F32)

        @pl.when((c == 0) & (h == 0))
        def _():
            for r in dp_refs:
                r[...] = jnp.zeros(r.shape, F32)

        xv = [_pieces(r[...]) for r in x_refs]
        pv = [_pieces(r[...]) for r in p_refs]
        _, vjp = jax.vjp(lambda a, b, s: f(a, b, s, h), xv, pv, _pieces(s_ref[...]))
        dxv, dpv, dS = vjp(([_pieces(r[...]) for r in dy_refs], _pieces(dst[h])))
        for i, (r, v) in enumerate(zip(dx_refs, dxv)):
            if i in shared and nh > 1:
                @pl.when(h == 0)
                def _(r=r, v=v):
                    r[...] = _join(v, 2)

                @pl.when(h > 0)
                def _(r=r, v=v):
                    r[...] += _join(v, 2)
            else:
                r[...] = _join(v, 2)
        for r, v in zip(dp_refs, dpv):
            r[...] += _join(v, len(r.shape))
        dst[h] = _join(dS, 3)

    def rev(im):
        return lambda c, h: im(nc - 1 - c, h)

    in_specs = [pl.BlockSpec(bs, rev(im)) for (_, bs, im) in xs]
    in_specs += [pl.BlockSpec(p.shape, (lambda c, h, n=p.ndim: (0,) * n)) for p in ps]
    in_specs.append(pl.BlockSpec((None, None) + tuple(state_shape), lambda c, h: (nc - 1 - c, h, 0, 0, 0)))
    in_specs += [pl.BlockSpec(bs, rev(im)) for (_, bs, im) in ys]
    out_specs = [pl.BlockSpec(bs, rev(im)) for (_, bs, im) in xs]
    out_specs += [pl.BlockSpec(p.shape, (lambda c, h, n=p.ndim: (0,) * n)) for p in ps]
    out_shape = [_S(x[0].shape) for x in xs] + [_S(p.shape) for p in ps]
    return hosted_call(body, rider, grid=(nc, nh), in_specs=in_specs, out_specs=out_specs, out_shape=out_shape,
                       scratch_shapes=[pltpu.VMEM((nh,) + tuple(state_shape), F32)], name=name,
                       args=[*[x[0] for x in xs], *ps, states, *dys])


def _lane():
    return _iota((1, LANES), 1)


def _col(v, idx):
    return jnp.sum(v * (_lane() == idx).astype(F32), axis=1, keepdims=True)


def _last_row(v):
    r = v.shape[0]
    return jnp.sum(v * (_iota((r, 1), 0) == r - 1).astype(F32), axis=0, keepdims=True)


def _tril(n, strict=False):
    r, c = _iota((n, n), 0), _iota((n, n), 1)
    return (r > c) if strict else (r >= c)


def ssd_chunk(xs, ps, S, h):
    zp, xbc, (sm,) = xs
    (bias,), (alog,), (dsk,), nw = ps
    Q = SSD_CHUNK
    n = sm.shape[0] // Q
    I = range(8 * n)
    lane = _lane()
    a128 = jnp.where(lane < 8, -jnp.exp(alog), 0.0)
    tri = _tril(Q)
    zc, xc = [_chunks(x, Q) for x in zp], [_chunks(x, Q) for x in xbc]
    dtl = [jax.nn.softplus(s + bias) for s in _chunks(sm, Q)]
    cum = [mm(tri.astype(F32), d * a128, "nn", "xa") for d in dtl]
    sel8 = (_iota((8, LANES), 0) == _iota((8, LANES), 1)).astype(F32)
    cum_t = [mm(sel8, c, "nt", "xa") for c in cum]
    m0 = (lane < 64).astype(F32)
    rows0 = (_iota((LANES, 1), 0) < 64).astype(F32)
    me = [m0 if i % 2 == 0 else 1.0 - m0 for i in I]
    Bm, Cm = [xc[4 + (i % 8) // 4][i // 8] for i in I], [xc[6 + (i % 8) // 4][i // 8] for i in I]
    cb = [mm(xc[6 + g][c], xc[4 + g][c], "nt", "1") for c in range(n) for g in range(2)]
    col = [_col(cum[i // 8], i % 8) for i in I]
    row = [jnp.sum(cum_t[i // 8] * (_iota((8, 1), 0) == i % 8).astype(F32), axis=0, keepdims=True) for i in I]
    xh = [xc[(i % 8) // 2][i // 8] * me[i] for i in I]
    xdt = [xh[i] * _col(dtl[i // 8], i % 8) for i in I]
    seg = [jnp.exp(jnp.where(tri, col[i] - row[i], -jnp.inf)) for i in I]
    last = [_last_row(col[i]) for i in I]
    y_diag = [mm(cb[2 * (i // 8) + (i % 8) // 4] * seg[i], xdt[i], "nn", "1") for i in I]
    st = [mm(xdt[i], Bm[i] * jnp.exp(last[i] - col[i]), "tn", "1") for i in I]
    c_dec = [Cm[i] * jnp.exp(col[i]) for i in I]
    skip = [_col(dsk, i % 8) * xh[i] for i in I]
    out = []
    for c in range(n):
        J = range(8 * c, 8 * c + 8)
        y = [y_diag[i] + mm(c_dec[i], S[(i % 8) // 2], "nt", "1") * me[i] + skip[i] for i in J]
        S = [S[pr] * (jnp.exp(last[8 * c + 2 * pr]) * rows0 + jnp.exp(last[8 * c + 2 * pr + 1]) * (1.0 - rows0))
             + st[8 * c + 2 * pr] + st[8 * c + 2 * pr + 1] for pr in range(4)]
        yz = [(y[2 * pr] + y[2 * pr + 1]) * _silu(zc[pr][c]) for pr in range(4)]
        ssq = sum(jnp.sum(v * v, axis=1, keepdims=True) for v in yz)
        scale = lax.rsqrt(ssq / BRANCH_W + NORM_EPS)
        out.append([yz[pr] * scale * nw[pr] for pr in range(4)])
    return [[jnp.concatenate([out[c][pr] for c in range(n)], axis=0) if n > 1 else out[0][pr] for pr in range(4)]], S


@jax.custom_vjp
def _halves(x):
    r = x.shape[0] // 2
    return x[:r], x[r:]


_halves.defvjp(lambda x: (_halves(x), None), lambda _, g: (jnp.concatenate(g, axis=0),))
DN_CHUNK = 64
DN_ROWS = 256
SSD_CHUNK = 128
SSD_ROWS = 256


def _chunks(x, rows=DN_CHUNK):
    if x.shape[0] == rows:
        return [x]
    a, b = _halves(x)
    return _chunks(a, rows) + _chunks(b, rows)


def dn_chunk(xs, ps, S, h):
    act, gate, (sm,) = xs
    (alog,), (dtb,), (nw,) = ps
    C = DN_CHUNK
    n = sm.shape[0] // C
    I = range(4 * n)
    lane = _lane()
    tri, strict = _tril(C), _tril(C, True)
    sm2 = _chunks(sm)
    G = [jnp.where((lane >= A0) & (lane < A0 + 4), -jnp.exp(alog) * jax.nn.softplus(s + dtb), 0.0) for s in sm2]
    gcs = [mm(tri.astype(F32), g, "nn", "xa") for g in G]
    sig = [jax.nn.sigmoid(s) for s in sm2]
    parts = [_chunks(x) for x in act]
    q, k, v = ([parts[o + i % 4][i // 4] for i in I] for o in (0, 4, 8))
    gt = [_chunks(x) for x in gate]
    qn = [q[i] * lax.rsqrt(jnp.sum(q[i] * q[i], axis=1, keepdims=True) + NORM_EPS) * (LANES ** -0.5) for i in I]
    kn = [k[i] * lax.rsqrt(jnp.sum(k[i] * k[i], axis=1, keepdims=True) + NORM_EPS) for i in I]
    beta = [_col(sig[i // 4], BETA0 + i % 4) for i in I]
    gcol = [_col(gcs[i // 4], A0 + i % 4) for i in I]
    selr = [((_iota((8, LANES), 0) == 0) & (_iota((8, LANES), 1) == A0 + h)).astype(F32) for h in range(4)]
    grow = [jnp.sum(mm(selr[i % 4], gcs[i // 4], "nt", "xa"), axis=0, keepdims=True) for i in I]
    gamma = [jnp.exp(jnp.where(tri, gcol[i] - grow[i], -jnp.inf)) for i in I]
    kb = [kn[i] * beta[i] for i in I]
    pk = [-(mm(kb[i], kn[i], "nt", "1") * jnp.where(strict, gamma[i], 0.0)) for i in I]
    eye = (_iota((C, C), 0) == _iota((C, C), 1)).astype(F32)
    minv = [eye + pk[i] for i in I]
    for _ in range(5):
        pk = [mm(pk[i], pk[i], "nn", "3") for i in I]
        minv = [minv[i] + mm(minv[i], pk[i], "nn", "3") for i in I]
    eg = [jnp.exp(gcol[i]) for i in I]
    w = [mm(minv[i], kb[i] * eg[i], "nn", "3") for i in I]
    u = [mm(minv[i], v[i] * beta[i], "nn", "3") for i in I]
    glast = [_last_row(gcol[i]) for i in I]
    qg = [qn[i] * eg[i] for i in I]
    qk = [mm(qn[i], kn[i], "nt", "1") * gamma[i] for i in I]
    kdec = [kn[i] * jnp.exp(glast[i] - gcol[i]) for i in I]
    y = []
    for c in range(n):
        J = range(4 * c, 4 * c + 4)
        vnew = [u[i] - mm(w[i], S[i % 4], "nn", "1") for i in J]
        o = [mm(qg[i], S[i % 4], "nn", "1") + mm(qk[i], vn, "nn", "1") for i, vn in zip(J, vnew)]
        S = [S[i % 4] * jnp.exp(glast[i]) + mm(kdec[i], vn, "tn", "1") for i, vn in zip(J, vnew)]
        on = [x * lax.rsqrt(jnp.mean(x * x, axis=1, keepdims=True) + NORM_EPS) * nw for x in o]
        y.append([on[h] * _silu(gt[h][c]) for h in range(4)])
    return [[jnp.concatenate([y[c][h] for c in range(n)], axis=0) for h in range(4)]], S


def sg_chunk(xs, ps, S, h):
    (uv,) = xs
    lng, lnb, W, (bt,) = ps
    u = [jax.nn.gelu(p) for p in uv[:4]]
    v = [jax.nn.gelu(p) for p in uv[4:]]
    mu = sum(jnp.sum(p, axis=1, keepdims=True) for p in v) / BRANCH_W
    vc = [p - mu for p in v]
    var = sum(jnp.sum(p * p, axis=1, keepdims=True) for p in vc) / BRANCH_W
    inv = lax.rsqrt(var + LN_EPS)
    trif = _tril(W[0].shape[0]).astype(F32)
    out = []
    for g in range(4):
        vn = vc[g] * inv * lng[g] + lnb[g]
        out.append(u[g] * (mm(W[g] * trif, vn, "nn", "1") + _col(bt, g)))
    return [out], S


def foxc_chunk(xs, ps, S, h):
    (sm,), ((fb,),), (carry,) = xs[0], ps, S
    lane = _lane()
    ls = jnp.where((lane >= FF0) & (lane < FF0 + 8), jax.nn.log_sigmoid(sm + fb), 0.0)
    c = mm(_tril(sm.shape[0]).astype(F32), ls, "nn", "xa") + carry
    return [[c]], [_last_row(c)]


HALO = 8


def _conv_tiles(T, C):
    return min(T, 512), _pick(C, 512)


def conv_fwd(x, w, b, name):
    T, C = x.shape
    tm, cb = _conv_tiles(T, C)

    def body(xp_ref, x_ref, w_ref, b_ref, o_ref):
        i = pl.program_id(1)
        e = jnp.concatenate([xp_ref[...] * (i > 0).astype(F32), x_ref[...]], axis=0)
        pre = b_ref[...] + sum(w_ref[k:k + 1, :] * e[5 + k:5 + k + tm, :] for k in range(4))
        o_ref[...] = _silu(pre)

    hb = tm // HALO
    return _pcall(body, grid=(C // cb, T // tm),
                  in_specs=[pl.BlockSpec((HALO, cb), lambda j, i: (jnp.maximum(i * hb - 1, 0), j)), pl.BlockSpec((tm, cb), lambda j, i: (i, j)),
                            pl.BlockSpec((4, cb), lambda j, i: (0, j)), pl.BlockSpec((1, cb), lambda j, i: (0, j))],
                  out_specs=pl.BlockSpec((tm, cb), lambda j, i: (i, j)), out_shape=_S((T, C)), name=name)(x, x, w, b)


def conv_bwd(x, w, b, dact, name):
    T, C = x.shape
    tm, cb = _conv_tiles(T, C)
    nt = T // tm

    def body(xp_ref, x_ref, xn_ref, w_ref, b_ref, d_ref, dn_ref, dx_ref, dw_ref, db_ref):
        i = pl.program_id(1)
        has_prev, has_next = (i > 0).astype(F32), (i < nt - 1).astype(F32)
        e = jnp.concatenate([xp_ref[...] * has_prev, x_ref[...], xn_ref[...] * has_next], axis=0)
        pre = b_ref[...] + sum(w_ref[k:k + 1, :] * e[5 + k:5 + k + tm + 8, :] for k in range(4))
        de = jnp.concatenate([d_ref[...], dn_ref[...] * has_next], axis=0)
        sg = jax.nn.sigmoid(pre)
        dpre = de * (sg * (1.0 + pre * (1.0 - sg)))
        dx_ref[...] = sum(w_ref[k:k + 1, :] * dpre[3 - k:3 - k + tm, :] for k in range(4))
        dcur = dpre[0:tm, :]
        dw = jnp.concatenate([jnp.sum(dcur * e[5 + k:5 + k + tm, :], axis=0, keepdims=True) for k in range(4)], axis=0)
        db = jnp.sum(dcur, axis=0, keepdims=True)

        @pl.when(i == 0)
        def _():
            dw_ref[...] = dw
            db_ref[...] = db

        @pl.when(i > 0)
        def _():
            dw_ref[...] += dw
            db_ref[...] += db

    blk = lambda f: pl.BlockSpec((tm, cb), f)
    hb = tm // HALO
    before = pl.BlockSpec((HALO, cb), lambda j, i: (jnp.maximum(i * hb - 1, 0), j))
    after = pl.BlockSpec((HALO, cb), lambda j, i: (jnp.minimum((i + 1) * hb, nt * hb - 1), j))
    return _pcall(body, grid=(C // cb, nt),
                  in_specs=[before, blk(lambda j, i: (i, j)), after,
                            pl.BlockSpec((4, cb), lambda j, i: (0, j)), pl.BlockSpec((1, cb), lambda j, i: (0, j)),
                            blk(lambda j, i: (i, j)), after],
                  out_specs=[blk(lambda j, i: (i, j)), pl.BlockSpec((4, cb), lambda j, i: (0, j)), pl.BlockSpec((1, cb), lambda j, i: (0, j))],
                  out_shape=[_S((T, C)), _S((4, C)), _S((1, C))], name=name)(x, x, x, w, b, dact, dact)


FOX_SCALE = 64 ** -0.5
LOG2E = 1.4426950408889634


def _spare(e, i):
    return (_lane() == 64 * (1 - e) + i).astype(F32)


def _lanes_of(e):
    lane = _lane()
    return ((lane < 64) if e == 0 else (lane >= 64)).astype(F32)


def _col3(col, e, first):
    c1 = col.astype(BF).astype(F32)
    c2 = (col - c1).astype(BF).astype(F32)
    c3 = (col - c1 - c2).astype(BF).astype(F32)
    return c1 * _spare(e, first) + c2 * _spare(e, first + 1) + c3 * _spare(e, first + 2)


def _split3_dense(x):
    x1 = x.astype(BF)
    r = x - x1.astype(F32)
    x2 = r.astype(BF)
    x3 = (r - x2.astype(F32)).astype(BF)
    return jnp.concatenate([x1, x2, x3], axis=1)


def _place(src_lane, e, first, sign):
    r, m = _iota((3 * LANES, LANES), 0), _iota((3 * LANES, LANES), 1)
    hit = (r % LANES == src_lane) & (m == 64 * (1 - e) + first + r // LANES)
    return jnp.where(hit, sign, 0.0).astype(BF)


def _ones3(e, first):
    return _spare(e, first) + _spare(e, first + 1) + _spare(e, first + 2)


def _causal_bias(n):
    return jnp.where(_iota((n, n), 0) >= _iota((n, n), 1), 0.0, -jnp.inf).astype(F32)


def _pair_spec(tq, row_of):
    return pl.BlockSpec((None, 2, tq, LANES), lambda hp, a, b: (hp, 0, row_of(a, b), 0))


def fox_prep(qkv, ccol, name):
    T = qkv.shape[0]
    tq = min(T, 512)

    def body(q_ref, k_ref, v_ref, cc_ref, qa_ref, ka_ref, va_ref):
        hp = pl.program_id(0)
        q, k, v = q_ref[...], k_ref[...], v_ref[...]
        parts = _split3_dense(cc_ref[...] * LOG2E)
        for e in range(2):
            me = _lanes_of(e)
            src = FF0 + 2 * hp + e
            qa_ref[e] = (q * me * (FOX_SCALE * LOG2E) + _dotb(parts, _place(src, e, 0, 1.0), NN) + _ones3(e, 3)).astype(BF)
            ka_ref[e] = (k * me + _ones3(e, 0) + _dotb(parts, _place(src, e, 3, -1.0), NN) + _ones3(e, 6)).astype(BF)
            va_ref[e] = (v * me + (1.0 - me)).astype(BF)

    blk = lambda off: pl.BlockSpec((tq, LANES), lambda hp, i: (i, off + hp))
    out = pl.BlockSpec((None, 2, tq, LANES), lambda hp, i: (hp, 0, i, 0))
    return _pcall(body, grid=(4, T // tq), in_specs=[blk(0), blk(4), blk(8), pl.BlockSpec((tq, LANES), lambda hp, i: (i, 0))],
                  out_specs=[out] * 3, out_shape=[_S((4, 2, T, LANES), BF)] * 3, name=name)(qkv, qkv, qkv, ccol)


def fox_fwd(qa, ka, va, name, rider=None):
    T = qa.shape[2]
    tq = min(T, 512)
    nq = T // tq

    def body(qa_ref, ka_ref, va_ref, o_ref, lse_ref, m_s, acc, causal):
        i, j = pl.program_id(1), pl.program_id(2)

        @pl.when((pl.program_id(0) == 0) & (i == 0) & (j == 0))
        def _():
            causal[...] = _causal_bias(tq)

        @pl.when(j == 0)
        def _():
            m_s[...] = jnp.full(m_s.shape, -jnp.inf, F32)
            acc[...] = jnp.zeros(acc.shape, F32)

        def step(diagonal):
            for e in range(2):
                s = _dotb(qa_ref[e], ka_ref[e], NT)
                if diagonal:
                    s = s + causal[...]
                m_old = m_s[e]
                m_new = jnp.maximum(m_old, jnp.max(s, axis=1, keepdims=True))
                p = jnp.exp2(s - m_new)
                m_s[e] = m_new
                acc[e] = acc[e] * jnp.exp2(m_old - m_new) + _dotb(p.astype(BF), va_ref[e], NN)

        @pl.when(j < i)
        def _():
            step(False)

        @pl.when(j == i)
        def _():
            step(True)
            lane = _lane()
            o, lse = 0.0, 0.0
            for e in range(2):
                me = _lanes_of(e)
                a = acc[e]
                l = jnp.where(me > 0.0, pltpu.roll(a, 64, 1), a)
                o = o + a * me / l
                lse = lse + (m_s[e] + jnp.log2(l)) * (lane == e).astype(F32)
            o_ref[...] = o
            lse_ref[...] = lse

    kv = _pair_spec(tq, lambda i, j: jnp.minimum(j, i))
    return hosted_call(body, rider, grid=(4, nq, nq), in_specs=[_pair_spec(tq, lambda i, j: i), kv, kv],
                       out_specs=[pl.BlockSpec((tq, LANES), lambda hp, i, j: (i, hp)), pl.BlockSpec((None, tq, LANES), lambda hp, i, j: (hp, i, 0))],
                       out_shape=[_S((T, BRANCH_W)), _S((4, T, LANES))],
                       scratch_shapes=[pltpu.VMEM((2, tq, 1), F32), pltpu.VMEM((2, tq, LANES), F32), pltpu.VMEM((tq, tq), F32)],
                       name=name, args=[qa, ka, va])


def fox_prep_bwd(qkv, qa, o, lse, do, name):
    T = qkv.shape[0]
    tq = min(T, 512)

    def body(q_ref, k_ref, qa_ref, o_ref, lse_ref, do_ref, qb_ref, doa_ref, qd_ref, kd_ref):
        q, k, dov = q_ref[...], k_ref[...], do_ref[...]
        dd = dov * o_ref[...]
        lse_parts = _split3_dense(lse_ref[...])
        for e in range(2):
            me = _lanes_of(e)
            qb_ref[e] = (qa_ref[e].astype(F32) + _dotb(lse_parts, _place(e, e, 6, -1.0), NN)).astype(BF)
            doa_ref[e] = (dov * me + _col3(-jnp.sum(dd * me, axis=1, keepdims=True), e, 0)).astype(BF)
            qd_ref[e] = (q * me * FOX_SCALE + _spare(e, 0)).astype(BF)
            kd_ref[e] = (k * me * FOX_SCALE + _spare(e, 0)).astype(BF)

    blk = lambda off: pl.BlockSpec((tq, LANES), lambda hp, i: (i, off + hp))
    pair = pl.BlockSpec((None, 2, tq, LANES), lambda hp, i: (hp, 0, i, 0))
    return _pcall(body, grid=(4, T // tq),
                  in_specs=[blk(0), blk(4), pair, blk(0), pl.BlockSpec((None, tq, LANES), lambda hp, i: (hp, i, 0)), blk(0)],
                  out_specs=[pair] * 4, out_shape=[_S((4, 2, T, LANES), BF)] * 4, name=name)(qkv, qkv, qa, o, lse, do)


def fox_bwd(qb, ka, va, doa, qd, kd, name, rider=None):
    T = qb.shape[2]
    tq = min(T, 512)
    nq = T // tq

    def body(qb_ref, ka_ref, va_ref, doa_ref, qd_ref, kd_ref, dq_ref, dk_ref, dv_ref, dcc_ref, dq_s, dk_s, dv_s, causal):
        hp, j, ii = pl.program_id(0), pl.program_id(1), pl.program_id(2)
        i = jnp.maximum(ii, j)

        @pl.when((hp == 0) & (j == 0) & (ii == 0))
        def _():
            dcc_ref[...] = jnp.zeros(dcc_ref.shape, F32)
            causal[...] = _causal_bias(tq)

        @pl.when((j == 0) & (ii == 0))
        def _():
            dq_s[...] = jnp.zeros(dq_s.shape, F32)

        @pl.when(ii == 0)
        def _():
            dk_s[...] = jnp.zeros(dk_s.shape, F32)
            dv_s[...] = jnp.zeros(dv_s.shape, F32)

        def step(diagonal):
            rows = pl.ds(pl.multiple_of(i * tq, tq), tq)
            for e in range(2):
                s = _dotb(qb_ref[e], ka_ref[e], NT)
                if diagonal:
                    s = s + causal[...]
                p = jnp.exp2(s)
                ds = (p * _dotb(doa_ref[e], va_ref[e], NT)).astype(BF)
                dv_s[e] += _dotb(p.astype(BF), doa_ref[e], TN)
                dq_s[e, rows, :] += _dotb(ds, kd_ref[e], NN)
                dk_s[e] += _dotb(ds, qd_ref[e], TN)

        @pl.when(ii > j)
        def _():
            step(False)

        @pl.when(ii == j)
        def _():
            step(True)

        def fold(acc, sign):
            grad, dc = 0.0, 0.0
            for e in range(2):
                a = acc[e]
                grad = grad + a * _lanes_of(e)
                dc = dc + sign * jnp.sum(a * _spare(e, 0), axis=1, keepdims=True) * (_lane() == FF0 + 2 * hp + e).astype(F32)
            return grad, dc

        @pl.when(ii == nq - 1)
        def _():
            grad, dc = fold(dk_s, -1.0)
            dk_ref[...] = grad
            dv_ref[...] = dv_s[0] * _lanes_of(0) + dv_s[1] * _lanes_of(1)
            dcc_ref[pl.ds(pl.multiple_of(j * tq, tq), tq), :] += dc

        @pl.when((j == nq - 1) & (ii == nq - 1))
        def _():
            grad, dc = fold(dq_s, 1.0)
            dq_ref[...] = grad
            dcc_ref[...] += dc

    irow, jrow = _pair_spec(tq, lambda j, ii: jnp.maximum(ii, j)), _pair_spec(tq, lambda j, ii: j)
    jout = pl.BlockSpec((tq, LANES), lambda hp, j, ii: (j, hp))
    return hosted_call(body, rider, grid=(4, nq, nq), in_specs=[irow, jrow, jrow, irow, irow, jrow],
                       out_specs=[pl.BlockSpec((T, LANES), lambda hp, j, ii: (0, hp)), jout, jout, pl.BlockSpec((T, LANES), lambda hp, j, ii: (0, 0))],
                       out_shape=[_S((T, BRANCH_W)), _S((T, BRANCH_W)), _S((T, BRANCH_W)), _S((T, LANES))],
                       scratch_shapes=[pltpu.VMEM((2, T, LANES), F32), pltpu.VMEM((2, tq, LANES), F32), pltpu.VMEM((2, tq, LANES), F32),
                                       pltpu.VMEM((tq, tq), F32)],
                       name=name, args=[qb, ka, va, doa, qd, kd])


def _acc_out(ref, val, first):
    @pl.when(first)
    def _():
        ref[...] = val

    @pl.when(jnp.logical_not(first))
    def _():
        ref[...] += val


def _row(tm, c):
    return pl.BlockSpec((tm, c), lambda i: (i, 0))


def _full(shape):
    return pl.BlockSpec(shape, lambda *_: (0,) * len(shape))


def ln_fwd(x, g, b, name, rider=None):
    T, C = x.shape
    tm = min(T, 512)

    def body(x_ref, g_ref, b_ref, o_ref):
        o_ref[...] = _ln(x_ref[...], g_ref[...], b_ref[...])

    return hosted_call(body, rider, grid=(T // tm,), in_specs=[_row(tm, C), _full((1, C)), _full((1, C))], out_specs=[_row(tm, C)],
                       out_shape=[_S((T, C))], scratch_shapes=[], name=name, args=[x, g, b])


def loss_head(h, target, name):
    T, C = h.shape
    tm = min(T, 512)

    def body(h_ref, t_ref, d_ref, l_ref):
        e = h_ref[...] - t_ref[...]
        d_ref[...] = e * (1.0 / C)
        part = jnp.sum(jnp.sum(e * e, axis=1, keepdims=True), axis=0, keepdims=True) * (0.5 / C)
        _acc_out(l_ref, part, pl.program_id(0) == 0)

    return _pcall(body, grid=(T // tm,), in_specs=[_row(tm, C), _row(tm, C)], out_specs=[_row(tm, C), _full((1, 1))],
                  out_shape=[_S((T, C)), _S((1, 1))], name=name)(h, target)


def add3(a, b, c, name):
    T, C = a.shape
    tm = min(T, 512)

    def body(a_ref, b_ref, c_ref, o_ref):
        o_ref[...] = a_ref[...] + b_ref[...] + c_ref[...]

    return _pcall(body, grid=(T // tm,), in_specs=[_row(tm, C)] * 3, out_specs=_row(tm, C), out_shape=_S((T, C)), name=name)(a, b, c)


def _wb_spec(l):
    return pl.BlockSpec((None, 4, BRANCH_W, D_MODEL), lambda *_: (l, 0, 0, 0))


def merge_fwd(ys, gl, gb, wb, l, name):
    T = gl.shape[0]
    tm = min(T, 256)

    def body(y0, y1, y2, y3, gl_ref, gb_ref, wb_ref, o_ref):
        acc = 0.0
        for i, y in enumerate((y0, y1, y2, y3)):
            z = _dotb(y[...].astype(BF), wb_ref[i], NN)
            g = jax.nn.sigmoid(gl_ref[:, i * D_MODEL:(i + 1) * D_MODEL] + gb_ref[i:i + 1, :])
            acc = acc + g * z
        o_ref[...] = acc

    return _pcall(body, grid=(T // tm,), in_specs=[_row(tm, BRANCH_W)] * 4 + [_row(tm, 4 * D_MODEL), _full((4, D_MODEL)), _wb_spec(l)],
                  out_specs=_row(tm, D_MODEL), out_shape=_S((T, D_MODEL)), name=name)(*ys, gl, gb, wb)


def merge_bwd(ys, gl, gb, wb, l, dm, name):
    T = gl.shape[0]
    tm = min(T, 256)

    def body(y0, y1, y2, y3, gl_ref, gb_ref, wb_ref, dm_ref, d0, d1, d2, d3, dgl_ref, dz_ref, dgb_ref):
        dmv = dm_ref[...]
        first = pl.program_id(0) == 0
        for i, (y, d) in enumerate(zip((y0, y1, y2, y3), (d0, d1, d2, d3))):
            cols = slice(i * D_MODEL, (i + 1) * D_MODEL)
            z = _dotb(y[...].astype(BF), wb_ref[i], NN)
            g = jax.nn.sigmoid(gl_ref[:, cols] + gb_ref[i:i + 1, :])
            dgl = dmv * z * (g * (1.0 - g))
            dz = (g * dmv).astype(BF)
            dgl_ref[:, cols] = dgl
            dz_ref[:, cols] = dz
            d[...] = _dotb(dz, wb_ref[i], NT)
            _acc_out(dgb_ref.at[i:i + 1, :], jnp.sum(dgl, axis=0, keepdims=True), first)

    return _pcall(body, grid=(T // tm,),
                  in_specs=[_row(tm, BRANCH_W)] * 4 + [_row(tm, 4 * D_MODEL), _full((4, D_MODEL)), _wb_spec(l), _row(tm, D_MODEL)],
                  out_specs=[_row(tm, BRANCH_W)] * 4 + [_row(tm, 4 * D_MODEL), _row(tm, 4 * D_MODEL), _full((4, D_MODEL))],
                  out_shape=[_S((T, BRANCH_W))] * 4 + [_S((T, 4 * D_MODEL)), _S((T, 4 * D_MODEL), BF), _S((4, D_MODEL))], name=name)(
                      *ys, gl, gb, wb, dm)


def _wout_spec(l):
    return pl.BlockSpec((None, D_MODEL, D_MODEL), lambda *_: (l, 0, 0))


def out_fwd(merged, h, wout, l, g, b, name):
    T = h.shape[0]
    tm = min(T, 512)

    def body(m_ref, h_ref, w_ref, g_ref, b_ref, u_ref, o_ref):
        u = ALPHA * h_ref[...] + _dotb(m_ref[...].astype(BF), w_ref[...], NN)
        u_ref[...] = u
        o_ref[...] = _ln(u, g_ref[...], b_ref[...])

    C = D_MODEL
    return _pcall(body, grid=(T // tm,), in_specs=[_row(tm, C), _row(tm, C), _wout_spec(l), _full((1, C)), _full((1, C))],
                  out_specs=[_row(tm, C), _row(tm, C)], out_shape=[_S((T, C)), _S((T, C))], name=name)(merged, h, wout, g, b)


def out_bwd(u, dy, g, b, wout, l, name):
    T, C = u.shape
    tm = min(T, 512)

    def body(u_ref, dy_ref, g_ref, b_ref, w_ref, du_ref, dm_ref, dg_ref, db_ref):
        _, vjp = jax.vjp(_ln, u_ref[...], g_ref[...], b_ref[...])
        du, dg, db = vjp(dy_ref[...])
        du_ref[...] = du
        dm_ref[...] = _dotb(du.astype(BF), w_ref[...], NT)
        first = pl.program_id(0) == 0
        _acc_out(dg_ref, dg, first)
        _acc_out(db_ref, db, first)

    return _pcall(body, grid=(T // tm,), in_specs=[_row(tm, C), _row(tm, C), _full((1, C)), _full((1, C)), _wout_spec(l)],
                  out_specs=[_row(tm, C), _row(tm, C), _full((1, C)), _full((1, C))],
                  out_shape=[_S((T, C)), _S((T, C)), _S((1, C)), _S((1, C))], name=name)(u, dy, g, b, wout)


def ff_fwd(h, wup, wdown, l, g, b, name):
    T, C = h.shape
    F = wup.shape[2]
    tm, tf = min(T, 1024), 1024
    nf = F // tf

    def body(h_ref, wu_ref, wd_ref, g_ref, b_ref, u_ref, o_ref, acc):
        f = pl.program_id(1)
        a = _dotb(h_ref[...].astype(BF), wu_ref[...], NN)
        r = jnp.square(jnp.maximum(a, 0.0))
        p = _dotb(r.astype(BF), wd_ref[...], NN)
        _acc_out(acc, p, f == 0)

        @pl.when(f == nf - 1)
        def _():
            u = ALPHA * h_ref[...] + acc[...]
            u_ref[...] = u
            o_ref[...] = _ln(u, g_ref[...], b_ref[...])

    row = pl.BlockSpec((tm, C), lambda i, f: (i, 0))
    return _pcall(body, grid=(T // tm, nf),
                  in_specs=[row, pl.BlockSpec((None, C, tf), lambda i, f: (l, 0, f)), pl.BlockSpec((None, tf, C), lambda i, f: (l, f, 0)),
                            _full((1, C)), _full((1, C))],
                  out_specs=[row, row], out_shape=[_S((T, C)), _S((T, C))], scratch_shapes=[pltpu.VMEM((tm, C), F32)], name=name)(h, wup, wdown, g, b)


def ff_bwd(u, dy, h, g, b, wup, wdown, l, name):
    T, C = h.shape
    F = wup.shape[2]
    tm, tf = min(T, 512), 1024
    nf = F // tf

    def body(u_ref, dy_ref, h_ref, g_ref, b_ref, wu_ref, wd_ref, du_ref, dh_ref, da_ref, r_ref, dg_ref, db_ref, du_s, acc):
        i, f = pl.program_id(0), pl.program_id(1)

        @pl.when(f == 0)
        def _():
            _, vjp = jax.vjp(_ln, u_ref[...], g_ref[...], b_ref[...])
            du, dg, db = vjp(dy_ref[...])
            du_s[...] = du
            du_ref[...] = du
            _acc_out(dg_ref, dg, i == 0)
            _acc_out(db_ref, db, i == 0)

        a = _dotb(h_ref[...].astype(BF), wu_ref[...], NN)
        ap = jnp.maximum(a, 0.0)
        dr = _dotb(du_s[...].astype(BF), wd_ref[...], NT)
        da = (dr * (2.0 * ap)).astype(BF)
        da_ref[...] = da
        r_ref[...] = jnp.square(ap).T.astype(BF)
        _acc_out(acc, _dotb(da, wu_ref[...], NT), f == 0)

        @pl.when(f == nf - 1)
        def _():
            dh_ref[...] = ALPHA * du_s[...] + acc[...]

    row = pl.BlockSpec((tm, C), lambda i, f: (i, 0))
    colf = pl.BlockSpec((tm, tf), lambda i, f: (i, f))
    return _pcall(body, grid=(T // tm, nf),
                  in_specs=[row, row, row, _full((1, C)), _full((1, C)), pl.BlockSpec((None, C, tf), lambda i, f: (l, 0, f)),
                            pl.BlockSpec((None, tf, C), lambda i, f: (l, f, 0))],
                  out_specs=[row, row, colf, pl.BlockSpec((tf, tm), lambda i, f: (f, i)), _full((1, C)), _full((1, C))],
                  out_shape=[_S((T, C)), _S((T, C)), _S((T, F), BF), _S((F, T), BF), _S((1, C)), _S((1, C))],
                  scratch_shapes=[pltpu.VMEM((tm, C), F32), pltpu.VMEM((tm, C), F32)], name=name)(u, dy, h, g, b, wup, wdown)


MESH_ID = pl.DeviceIdType.MESH
_ANY = pl.BlockSpec(memory_space=pl.ANY)


def _window(ref, ax, idx, n):
    if n < 0:
        return ref
    sel = idx if n == 0 else pl.ds(pl.multiple_of(idx * n, n), n)
    return ref.at[(slice(None),) * ax + (sel,)]


Rider = collections.namedtuple("Rider", "operands out_shape scratch start wait")


def hosted_call(body, rider, *, grid, in_specs, out_specs, out_shape, scratch_shapes, name, args):
    n_in, n_out, n_scr = len(in_specs), len(out_specs), len(scratch_shapes)
    if rider is None:
        return _pcall(body, grid=grid, in_specs=in_specs, out_specs=out_specs, out_shape=out_shape, scratch_shapes=scratch_shapes, name=name)(*args), []
    ri, ro = len(rider.operands), len(rider.out_shape)

    def wrapped(*refs):
        ins, r_in = refs[:n_in], refs[n_in:n_in + ri]
        o0 = n_in + ri
        outs, r_out = refs[o0:o0 + n_out], refs[o0 + n_out:o0 + n_out + ro]
        s0 = o0 + n_out + ro
        scr, r_scr = refs[s0:s0 + n_scr], refs[s0 + n_scr:]
        ids = [pl.program_id(i) for i in range(len(grid))]
        first = functools.reduce(jnp.logical_and, [i == 0 for i in ids])
        last = functools.reduce(jnp.logical_and, [i == g - 1 for i, g in zip(ids, grid)])

        @pl.when(first)
        def _():
            rider.start(r_in, r_out, r_scr)

        body(*ins, *outs, *scr)

        @pl.when(last)
        def _():
            rider.wait(r_in, r_out, r_scr)

    res = _pcall(wrapped, grid=grid, in_specs=list(in_specs) + [_ANY] * ri, out_specs=list(out_specs) + [_ANY] * ro,
                 out_shape=list(out_shape) + list(rider.out_shape), scratch_shapes=list(scratch_shapes) + list(rider.scratch),
                 name=name)(*args, *rider.operands)
    return res[:n_out], res[n_out:]


def comm_call(rider, name):
    ri = len(rider.operands)

    def body(*refs):
        r_in, r_out, r_scr = refs[:ri], refs[ri:ri + len(rider.out_shape)], refs[ri + len(rider.out_shape):]
        rider.start(r_in, r_out, r_scr)
        rider.wait(r_in, r_out, r_scr)

    return _pcall(body, in_specs=[_ANY] * ri, out_specs=[_ANY] * len(rider.out_shape), out_shape=list(rider.out_shape),
                  scratch_shapes=list(rider.scratch), name=name)(*rider.operands)


def gather_rider(shards, axes):
    K = len(shards)
    widths = [s.shape[a] for s, a in zip(shards, axes)]
    out_shape = [_S(s.shape[:a] + (N_DEV * s.shape[a],) + s.shape[a + 1:], s.dtype) for s, a in zip(shards, axes)]

    def plan(x_refs, o_refs, sems):
        send_sems, recv_sems, local_sems = sems
        mx, my, mc = lax.axis_index("x"), lax.axis_index("y"), lax.axis_index("c")
        me, sibling = (mx, my, mc), (mx, my, 1 - mc)
        chips = [(1 - mx, my), (mx, 1 - my), (1 - mx, 1 - my)]

        def win(k, px, py, pc):
            return _window(o_refs[k], axes[k], 4 * px + 2 * py + pc, widths[k])

        def copy(k, slot, block, to, src=None):
            return pltpu.make_async_remote_copy(src_ref=win(k, *block) if src is None else src, dst_ref=win(k, *block),
                                                send_sem=send_sems.at[7 * k + slot], recv_sem=recv_sems.at[7 * k + slot],
                                                device_id=to, device_id_type=MESH_ID)

        mine = [pltpu.make_async_copy(x_refs[k], win(k, *me), local_sems.at[k]) for k in range(K)]
        first = []
        for k in range(K):
            first.append(copy(k, 0, me, sibling, src=x_refs[k]))
            first += [copy(k, 1 + j, me, (*chip, mc), src=x_refs[k]) for j, chip in enumerate(chips)]
        return me, sibling, chips, copy, mine, first

    def start(x_refs, o_refs, sems):
        _, _, _, _, mine, first = plan(x_refs, o_refs, sems)
        for cp in mine + first:
            cp.start()

    def wait(x_refs, o_refs, sems):
        me, sibling, chips, copy, mine, first = plan(x_refs, o_refs, sems)
        mc = me[2]
        passed = []
        for j, chip in enumerate(chips):
            for k in range(K):
                copy(k, 1 + j, (*chip, mc), me).wait_recv()
                passed.append(copy(k, 4 + j, (*chip, mc), sibling))
                passed[-1].start()
        for k in range(K):
            copy(k, 0, sibling, me).wait_recv()
        for j, chip in enumerate(chips):
            for k in range(K):
                copy(k, 4 + j, (*chip, 1 - mc), me).wait_recv()
        for cp in first + passed:
            cp.wait_send()
        for cp in mine:
            cp.wait()

    scratch = [pltpu.SemaphoreType.DMA((7 * K,)), pltpu.SemaphoreType.DMA((7 * K,)), pltpu.SemaphoreType.DMA((K,))]
    return Rider(list(shards), out_shape, scratch, start, wait)


def exchange_rider(items):
    ns = len(items)
    out_shape = [_S((N_DEV,) + tuple(it[3]), it[0].dtype) for it in items]

    def plan(src_refs, o_refs, sems):
        send_sems, recv_sems, local_sems = sems
        mx, my, mc = lax.axis_index("x"), lax.axis_index("y"), lax.axis_index("c")
        me = 4 * mx + 2 * my + mc
        remote, own = [], []
        for s, (_, ax, n, _) in enumerate(items):
            own.append(pltpu.make_async_copy(_window(src_refs[s], ax, me, n), o_refs[s].at[me], local_sems.at[s]))
            for k in range(1, N_DEV):
                px = 1 - mx if k & 4 else mx
                py = 1 - my if k & 2 else my
                pc = 1 - mc if k & 1 else mc
                remote.append(pltpu.make_async_remote_copy(
                    src_ref=_window(src_refs[s], ax, 4 * px + 2 * py + pc, n), dst_ref=o_refs[s].at[me],
                    send_sem=send_sems.at[7 * s + k - 1], recv_sem=recv_sems.at[7 * s + k - 1],
                    device_id=(px, py, pc), device_id_type=MESH_ID))
        return remote, own

    def start(src_refs, o_refs, sems):
        remote, own = plan(src_refs, o_refs, sems)
        for cp in own + remote:
            cp.start()

    def wait(src_refs, o_refs, sems):
        remote, own = plan(src_refs, o_refs, sems)
        for cp in remote + own:
            cp.wait()

    scratch = [pltpu.SemaphoreType.DMA((7 * ns,)), pltpu.SemaphoreType.DMA((7 * ns,)), pltpu.SemaphoreType.DMA((ns,))]
    return Rider([it[0] for it in items], out_shape, scratch, start, wait)


def reduce_adamw(rcvs, w, m, v, name):
    L = len(rcvs)
    _, A, B, C = rcvs[0].shape
    tb = B
    while tb > 8 and tb * C > (1 << 17):
        tb //= 2

    def body(*refs):
        r_refs, (w_ref, m_ref, v_ref, g_ref, d_ref, mo_ref, vo_ref) = refs[:L], refs[L:]
        for k in range(L):
            @pl.when(pl.program_id(0) == k)
            def _(k=k):
                g = r_refs[k][0].astype(F32)
                for d in range(1, N_DEV):
                    g = g + r_refs[k][d].astype(F32)
                mn = ADAM_B1 * m_ref[...] + (1.0 - ADAM_B1) * g
                vn = ADAM_B2 * v_ref[...] + (1.0 - ADAM_B2) * jnp.square(g)
                m_hat = mn / (1.0 - ADAM_B1 ** ADAM_STEP)
                v_hat = vn / (1.0 - ADAM_B2 ** ADAM_STEP)
                g_ref[...] = g
                d_ref[...] = -ADAM_LR * (m_hat / (jnp.sqrt(v_hat) + ADAM_EPS) + ADAM_WD * w_ref[...])
                mo_ref[...] = mn
                vo_ref[...] = vn

    def rspec(k):
        return pl.BlockSpec((N_DEV, None, tb, C), lambda l, a, i: (0, jnp.where(l == k, a, 0), jnp.where(l == k, i, 0), 0))

    blk = pl.BlockSpec((None, tb, C), lambda l, a, i: (l * A + a, i, 0))
    return _pcall(body, grid=(L, A, B // tb), in_specs=[rspec(k) for k in range(L)] + [blk, blk, blk],
                  out_specs=[blk] * 4, out_shape=[_S((L * A, B, C))] * 4, name=name)(*rcvs, w, m, v)


def _w_in_pieces(g0, g1):
    per = D_IN // N_DEV
    return [(d, max(g0, d * per) - d * per, min(g1, (d + 1) * per) - d * per) for d in range(N_DEV) if max(g0, d * per) < min(g1, (d + 1) * per)]


def repack_w_in(w8, name):
    _, L, R, per = w8.shape
    tr = 256

    def cols(x_ref, g0, g1):
        return [x_ref[d, :, a:b] for d, a, b in _w_in_pieces(g0, g1)]

    def body(x_ref, *o_refs):
        for (name_, i), o_ref in zip(SEGS, o_refs):
            o_ref[...] = jnp.concatenate(cols(x_ref, _OFF[i], _OFF[i + 1]), axis=1)
        parts, at = [], 0
        for i, lane0 in SMALL_SRC:
            assert lane0 == at
            parts += cols(x_ref, _OFF[i], _OFF[i + 1])
            at += IN_SIZES[i]
        parts.append(jnp.zeros((tr, LANES - at), w8.dtype))
        o_refs[-1][...] = jnp.concatenate(parts, axis=1)

    widths = [IN_SIZES[i] for _, i in SEGS] + [LANES]
    outs = _pcall(body, grid=(L, R // tr), in_specs=[pl.BlockSpec((N_DEV, None, tr, per), lambda l, r: (0, l, r, 0))],
                  out_specs=[pl.BlockSpec((None, tr, w), lambda l, r: (l, r, 0)) for w in widths],
                  out_shape=[_S((L, R, w), w8.dtype) for w in widths], name=name)(w8)
    return dict(zip(SEG_NAMES, outs))


def repack_dw_in(dseg, name):
    R = dseg["z"].shape[0]
    per = D_IN // N_DEV
    tr = 128
    src = {i: (k, 0) for k, (_, i) in enumerate(SEGS)}
    src.update({i: (len(SEGS), lane0) for i, lane0 in SMALL_SRC})

    def body(*refs):
        s_refs, o_ref = refs[:-1], refs[-1]
        for d in range(N_DEV):
            parts = []
            for i in range(len(IN_SIZES)):
                g0, g1 = max(_OFF[i], d * per), min(_OFF[i + 1], (d + 1) * per)
                if g0 < g1:
                    k, c0 = src[i]
                    parts.append(s_refs[k][:, c0 + g0 - _OFF[i]:c0 + g1 - _OFF[i]])
            o_ref[d] = jnp.concatenate(parts, axis=1)

    arrs = [dseg[n] for n in SEG_NAMES]
    return _pcall(body, grid=(R // tr,), in_specs=[pl.BlockSpec((tr, a.shape[1]), lambda r: (r, 0)) for a in arrs],
                  out_specs=pl.BlockSpec((N_DEV, tr, per), lambda r: (0, r, 0)), out_shape=_S((N_DEV, R, per), arrs[0].dtype), name=name)(*arrs)


WEIGHTS = ("ln_in_g", "ln_in_b", "w_in", "ssd_conv_w", "ssd_conv_b", "ssd_dt_bias", "ssd_a_log", "ssd_d", "ssd_norm_w", "dn_conv_w",
           "dn_a_log", "dn_dt_bias", "dn_norm_w", "sg_ln_g", "sg_ln_b", "sg_w", "sg_b", "fox_f_bias", "gate_b", "w_branch", "w_out",
           "ln1_g", "ln1_b", "w_up", "w_down", "ln2_g", "ln2_b")
SHARDED = {"w_in": 2, "ssd_conv_w": 2, "dn_conv_w": 2, "gate_b": 2, "w_branch": 3, "w_out": 1, "w_up": 2, "w_down": 1}
SLABBED = ("w_in", "dn_conv_w")
MATMUL_WEIGHTS = ("w_in", "w_branch", "w_out", "w_up", "w_down")
REPLICATED_ENTRY = ("ln_in_g", "ln_in_b")
REPLICATED_LAYER = tuple(n for n in WEIGHTS if n not in SHARDED and n not in REPLICATED_ENTRY and n != "sg_w")
SEG_NAMES = tuple(n for n, _ in SEGS) + ("small",)
PACK_COLS = 1024


def _lanes(vec, off):
    return jnp.pad(vec, (off, LANES - off - vec.shape[0]))[None]


def _pack_small(parts, row_mult):
    flat = jnp.concatenate([q.reshape(-1) for q in parts])
    rows = -(-flat.shape[0] // (PACK_COLS * row_mult)) * row_mult
    return jnp.pad(flat, (0, rows * PACK_COLS - flat.shape[0])).reshape(1, rows, PACK_COLS)


EARLY = ("w_branch", "w_out", "w_up", "w_down", "gate_b")
LATE = ("w_in", "ssd_conv_w", "dn_conv_w")
WHOLE = ("sg_w",)


def _gather_rider(p, l, names):
    shards, axes = [], []
    for n in names:
        s = p[n][l:l + 1]
        s = s.astype(BF) if n in MATMUL_WEIGHTS else s
        shards.append(s[None] if n in SLABBED else s)
        axes.append(0 if n in SLABBED else SHARDED[n])
    return gather_rider(shards, axes)


def _exchange_items(g, p, names):
    items = []
    for n in names:
        local = p[n].shape[1:]
        if n in WHOLE:
            items.append((g[n], 0, -1, local))
        elif n in SLABBED:
            items.append((g[n], 0, 0, local))
        else:
            items.append((g[n], SHARDED[n] - 1, local[SHARDED[n] - 1], local))
    return items


def _use_gathered(w, names, arrays, l):
    for n, arr in zip(names, arrays):
        if n == "w_in":
            w[n] = repack_w_in(arr, f"w_in_repack_{l}")
        elif n == "ssd_conv_w":
            w["ssd_cw"] = arr[0]
        elif n == "dn_conv_w":
            w["dn_cw"] = jnp.moveaxis(arr[:, 0], 0, 1).reshape(4, 3 * BRANCH_W)
        elif n == "gate_b":
            w[n] = arr[0]
        else:
            w[n] = arr


def _layer_weights(p, l):
    w = {}
    w["ssd_cb"] = p["ssd_conv_b"][l][None]
    w["dn_cb"] = jnp.zeros((1, 3 * BRANCH_W), F32)
    w["ssd_ps"] = [_lanes(p["ssd_dt_bias"][l], DT0), _lanes(p["ssd_a_log"][l], DT0), _lanes(p["ssd_d"][l], DT0), p["ssd_norm_w"][l][None]]
    w["dn_ps"] = [_lanes(p["dn_a_log"][l], A0), _lanes(p["dn_dt_bias"][l], A0), p["dn_norm_w"][l][None]]
    w["sg_ps"] = [p["sg_ln_g"][l][None], p["sg_ln_b"][l][None], p["sg_w"][l], jnp.pad(p["sg_b"][l].T, ((0, 0), (0, LANES - 4)))]
    w["fox_ps"] = [_lanes(p["fox_f_bias"][l], FF0)]
    for n in ("ln1_g", "ln1_b", "ln2_g", "ln2_b"):
        w[n] = p[n][l][None]
    return w


def _scan_specs(T, a):
    c0 = lambda c, h: (c, 0)
    sr = min(T, SSD_ROWS)
    ssd = dict(f=ssd_chunk, xs=[(a["z"], (sr, 512), c0), (a["xbc_act"], (sr, 1024), c0), (a["small"], (sr, LANES), c0)],
               ys=[((T, BRANCH_W), (sr, BRANCH_W), c0)], state=(4, LANES, LANES), nc=T // sr, nh=1, shared=())
    dr = min(T, DN_ROWS)
    dn = dict(f=dn_chunk, xs=[(a["dn_act"], (dr, 3 * BRANCH_W), c0), (a["dngate"], (dr, BRANCH_W), c0), (a["small"], (dr, LANES), c0)],
              ys=[((T, BRANCH_W), (dr, BRANCH_W), c0)], state=(4, LANES, LANES), nc=T // dr, nh=1, shared=())
    sg = dict(f=sg_chunk, xs=[(a["sguv"], (128, 1024), c0)], ys=[((T, BRANCH_W), (128, BRANCH_W), c0)], state=(1, 8, LANES), nc=T // 128, nh=1, shared=())
    fc = dict(f=foxc_chunk, xs=[(a["small"], (128, LANES), c0)],
              ys=[((T, LANES), (128, LANES), c0)], state=(1, 1, LANES), nc=T // 128, nh=1, shared=())
    return ssd, dn, sg, fc


def _layer_fwd(h, w, l, dn_rider=None, fox_rider=None):
    T = h.shape[0]
    a = {"h": h, **proj_all(h, w["w_in"], f"proj_{l}")}
    a["xbc_act"] = conv_fwd(a["xbc"], w["ssd_cw"], w["ssd_cb"], f"ssd_conv_{l}")
    a["dn_act"] = conv_fwd(a["dnqkv"], w["dn_cw"], w["dn_cb"], f"dn_conv_{l}")
    ssd, dn, sg, fc = _scan_specs(T, a)
    (a["ya"], a["ssd_st"]), _ = scan_fwd(f"ssd_fwd_{l}", ssd["f"], ssd["xs"], w["ssd_ps"], ssd["ys"], ssd["state"], ssd["nc"], ssd["nh"])
    (a["yb"], a["dn_st"]), got = scan_fwd(f"dn_fwd_{l}", dn["f"], dn["xs"], w["dn_ps"], dn["ys"], dn["state"], dn["nc"], dn["nh"], rider=dn_rider)
    _use_gathered(w, EARLY, got, l)
    (a["yc"], a["sg_st"]), _ = scan_fwd(f"sg_fwd_{l}", sg["f"], sg["xs"], w["sg_ps"], sg["ys"], sg["state"], sg["nc"], sg["nh"])
    (a["ccol"], a["fc_st"]), _ = scan_fwd(f"foxc_fwd_{l}", fc["f"], fc["xs"], w["fox_ps"], fc["ys"], fc["state"], fc["nc"], fc["nh"])
    a["fox_qa"], a["fox_ka"], a["fox_va"] = fox_prep(a["foxqkv"], a["ccol"], f"fox_prep_{l}")
    (a["yd"], a["lse"]), carried = fox_fwd(a["fox_qa"], a["fox_ka"], a["fox_va"], f"fox_fwd_{l}", rider=fox_rider)
    a["merged"] = merge_fwd([a["ya"], a["yb"], a["yc"], a["yd"]], a["gates"], w["gate_b"], w["w_branch"], 0, f"merge_fwd_{l}")
    a["u1"], a["h1"] = out_fwd(a["merged"], h, w["w_out"], 0, w["ln1_g"], w["ln1_b"], f"out_fwd_{l}")
    a["u2"], a["h2"] = ff_fwd(a["h1"], w["w_up"], w["w_down"], 0, w["ln2_g"], w["ln2_b"], f"ff_fwd_{l}")
    return a, carried


def _layer_bwd(dh2, a, w, l, p, late_above):
    T = dh2.shape[0]
    g = {}
    du2, dh1, da, r, dg2, db2 = ff_bwd(a["u2"], dh2, a["h1"], w["ln2_g"], w["ln2_b"], w["w_up"], w["w_down"], 0, f"ff_bwd_{l}")
    g["ln2_g"], g["ln2_b"] = dg2[0], db2[0]
    g["w_up"] = matmul_dw(transpose_bf16(a["h1"], f"h1_t_{l}"), da, f"dwup_{l}")
    g["w_down"] = matmul_dw(r, du2, f"dwdown_{l}")
    du1, dmerged, dg1, db1 = out_bwd(a["u1"], dh1, w["ln1_g"], w["ln1_b"], w["w_out"], 0, f"out_bwd_{l}")
    g["ln1_g"], g["ln1_b"] = dg1[0], db1[0]
    g["w_out"] = matmul_dw(transpose_bf16(a["merged"], f"merged_t_{l}"), du1, f"dwout_{l}")
    ys = [a["ya"], a["yb"], a["yc"], a["yd"]]
    dya, dyb, dyc, dyd, dgl, dz, dgb = merge_bwd(ys, a["gates"], w["gate_b"], w["w_branch"], 0, dmerged, f"merge_bwd_{l}")
    g["gate_b"] = dgb
    g["w_branch"] = jnp.stack([matmul_tn(ys[i], dz, f"dwb{i}_{l}", b_col0=i * D_MODEL, n_cols=D_MODEL, out_dtype=BF) for i in range(4)])
    early = exchange_rider(_exchange_items(g, p, EARLY))
    dn_rider = early if late_above is None else exchange_rider(late_above)
    fox_rider = None if late_above is None else early
    ssd, dn, sg, fc = _scan_specs(T, a)
    (dz_ssd, dxbc_act, dsm_ssd, d_dtb, d_alog, d_dsk, d_nw), _ = scan_bwd(f"ssd_bwd_{l}", ssd["f"], ssd["xs"], w["ssd_ps"], ssd["ys"], [dya], a["ssd_st"],
                                                                           ssd["state"], ssd["nc"], ssd["nh"])
    g["ssd_dt_bias"], g["ssd_a_log"], g["ssd_d"], g["ssd_norm_w"] = d_dtb[0, DT0:DT0 + 8], d_alog[0, DT0:DT0 + 8], d_dsk[0, DT0:DT0 + 8], d_nw[0]
    dxbc, g["ssd_conv_w"], dcb = conv_bwd(a["xbc"], w["ssd_cw"], w["ssd_cb"], dxbc_act, f"ssd_conv_bwd_{l}")
    g["ssd_conv_b"] = dcb[0]
    (ddn_act, ddngate, dsm_dn, d_alog, d_dtb, d_nw), got_dn = scan_bwd(f"dn_bwd_{l}", dn["f"], dn["xs"], w["dn_ps"], dn["ys"], [dyb], a["dn_st"],
                                                                        dn["state"], dn["nc"], dn["nh"], rider=dn_rider)
    g["dn_a_log"], g["dn_dt_bias"], g["dn_norm_w"] = d_alog[0, A0:A0 + 4], d_dtb[0, A0:A0 + 4], d_nw[0]
    ddnqkv, g["dn_conv_w"], _ = conv_bwd(a["dnqkv"], w["dn_cw"], w["dn_cb"], ddn_act, f"dn_conv_bwd_{l}")
    (dsguv, d_lng, d_lnb, d_w, d_bt), _ = scan_bwd(f"sg_bwd_{l}", sg["f"], sg["xs"], w["sg_ps"], sg["ys"], [dyc], a["sg_st"], sg["state"], sg["nc"], sg["nh"])
    g["sg_ln_g"], g["sg_ln_b"], g["sg_w"], g["sg_b"] = d_lng[0], d_lnb[0], d_w, d_bt[:, :4].T
    qb, doa, qd, kd = fox_prep_bwd(a["foxqkv"], a["fox_qa"], a["yd"], a["lse"], dyd, f"fox_prep_bwd_{l}")
    (dfq, dfk, dfv, dccol), got_fox = fox_bwd(qb, a["fox_ka"], a["fox_va"], doa, qd, kd, f"fox_bwd_{l}", rider=fox_rider)
    (dsm_fox, d_fb), _ = scan_bwd(f"foxc_bwd_{l}", fc["f"], fc["xs"], w["fox_ps"], fc["ys"], [dccol], a["fc_st"], fc["state"], fc["nc"], fc["nh"])
    g["fox_f_bias"] = d_fb[0, FF0:FF0 + 8]
    dseg = {"z": dz_ssd, "xbc": dxbc, "dnqkv": ddnqkv, "dngate": ddngate, "sguv": dsguv,
            "foxqkv": jnp.concatenate([dfq, dfk, dfv], axis=1), "gates": dgl, "small": add3(dsm_ssd, dsm_dn, dsm_fox, f"dsmall_{l}")}
    h_t = transpose_bf16(a["h"], f"h_t_{l}")
    dwin = {n: matmul_dw(h_t, dseg[n], f"dwin_{n}_{l}") for n in SEG_NAMES}
    g["w_in"] = repack_dw_in(dwin, f"dw_in_repack_{l}")
    g["dn_conv_w"] = jnp.moveaxis(g["dn_conv_w"].reshape(4, N_DEV, 3 * BRANCH_W // N_DEV), 1, 0)
    got = {(EARLY, l): got_dn} if late_above is None else {(LATE + WHOLE, l + 1): got_dn, (EARLY, l): got_fox}
    return dseg, du1, g, got


def proj_all(h, w_in, name):
    T = h.shape[0]
    tm = min(T, 256)
    ns = len(SEG_NAMES)

    def body(*refs):
        h_ref, w_refs, o_refs = refs[0], refs[1:1 + ns], refs[1 + ns:]
        hb = h_ref[...].astype(BF)
        for w_ref, o_ref in zip(w_refs, o_refs):
            o_ref[...] = _dotb(hb, w_ref[...], NN)

    widths = [w_in[n].shape[2] for n in SEG_NAMES]
    in_specs = [_row(tm, D_MODEL)]
    in_specs += [pl.BlockSpec((None,) + w_in[n].shape[1:], lambda i: (0, 0, 0), pipeline_mode=pl.Buffered(1)) for n in SEG_NAMES]
    outs = _pcall(body, grid=(T // tm,), in_specs=in_specs, out_specs=[_row(tm, wd) for wd in widths],
                  out_shape=[_S((T, wd)) for wd in widths], name=name)(h, *[w_in[n] for n in SEG_NAMES])
    return dict(zip(SEG_NAMES, outs))


def dh_all(dseg, w_in, add, name, rider=None, norm=None):
    T = add.shape[0]
    tm = min(T, 256)
    ns = len(SEG_NAMES)

    def body(*refs):
        d_refs, w_refs, add_ref = refs[:ns], refs[ns:2 * ns], refs[2 * ns]
        acc = ALPHA * add_ref[...]
        for d_ref, w_ref in zip(d_refs, w_refs):
            acc = acc + _dotb(d_ref[...].astype(BF), w_ref[...], NT)
        if norm is None:
            refs[2 * ns + 1][...] = acc
        else:
            x_ref, g_ref, b_ref, dx_ref, dg_ref, db_ref = refs[2 * ns + 1:]
            _, vjp = jax.vjp(_ln, x_ref[...], g_ref[...], b_ref[...])
            dx, dg, db = vjp(acc)
            dx_ref[...] = dx
            first = pl.program_id(0) == 0
            _acc_out(dg_ref, dg, first)
            _acc_out(db_ref, db, first)

    C = D_MODEL
    in_specs = [_row(tm, dseg[n].shape[1]) for n in SEG_NAMES]
    in_specs += [pl.BlockSpec((None,) + w_in[n].shape[1:], lambda i: (0, 0, 0), pipeline_mode=pl.Buffered(1)) for n in SEG_NAMES]
    in_specs.append(_row(tm, C))
    args = [*[dseg[n] for n in SEG_NAMES], *[w_in[n] for n in SEG_NAMES], add]
    out_specs, out_shape = [_row(tm, C)], [_S((T, C))]
    if norm is not None:
        in_specs += [_row(tm, C), _full((1, C)), _full((1, C))]
        args += list(norm)
        out_specs += [_full((1, C)), _full((1, C))]
        out_shape += [_S((1, C)), _S((1, C))]
    return hosted_call(body, rider, grid=(T // tm,), in_specs=in_specs, out_specs=out_specs, out_shape=out_shape,
                       scratch_shapes=[], name=name, args=args)


def kernel(x, ln_in_g, ln_in_b, w_in, ssd_conv_w, ssd_conv_b, ssd_dt_bias, ssd_a_log, ssd_d, ssd_norm_w, dn_conv_w, dn_a_log, dn_dt_bias, dn_norm_w, sg_ln_g, sg_ln_b, sg_w, sg_b, fox_f_bias, gate_b, w_branch, w_out, ln1_g, ln1_b, w_up, w_down, ln2_g, ln2_b, loss_target, m_ln_in_g, m_ln_in_b, m_w_in, m_ssd_conv_w, m_ssd_conv_b, m_ssd_dt_bias, m_ssd_a_log, m_ssd_d, m_ssd_norm_w, m_dn_conv_w, m_dn_a_log, m_dn_dt_bias, m_dn_norm_w, m_sg_ln_g, m_sg_ln_b, m_sg_w, m_sg_b, m_fox_f_bias, m_gate_b, m_w_branch, m_w_out, m_ln1_g, m_ln1_b, m_w_up, m_w_down, m_ln2_g, m_ln2_b, v_ln_in_g, v_ln_in_b, v_w_in, v_ssd_conv_w, v_ssd_conv_b, v_ssd_dt_bias, v_ssd_a_log, v_ssd_d, v_ssd_norm_w, v_dn_conv_w, v_dn_a_log, v_dn_dt_bias, v_dn_norm_w, v_sg_ln_g, v_sg_ln_b, v_sg_w, v_sg_b, v_fox_f_bias, v_gate_b, v_w_branch, v_w_out, v_ln1_g, v_ln1_b, v_w_up, v_w_down, v_ln2_g, v_ln2_b):
    args = dict(locals())
    p = {n: args[n] for n in WEIGHTS}
    xt, target = x[0], loss_target[0]
    ws, acts = [_layer_weights(p, l) for l in range(DEPTH)], []
    (h,), gathered = ln_fwd(xt, ln_in_g[None], ln_in_b[None], "ln_in_fwd", rider=_gather_rider(p, 0, LATE))
    _use_gathered(ws[0], LATE, gathered, 0)
    for l in range(DEPTH):
        a, gathered = _layer_fwd(h, ws[l], l, dn_rider=_gather_rider(p, 0, EARLY) if l == 0 else None,
                                 fox_rider=_gather_rider(p, l + 1, LATE + EARLY) if l + 1 < DEPTH else None)
        if l + 1 < DEPTH:
            _use_gathered(ws[l + 1], LATE + EARLY, gathered, l + 1)
        acts.append(a)
        h = a["h2"]
    dh, loss = loss_head(h, target, "loss_head")
    loss = lax.psum(loss[0, 0], ("x", "y", "c"))

    layer_grads, got, late = [None] * DEPTH, {}, None
    for l in reversed(range(DEPTH)):
        dseg, du1, layer_grads[l], got_l = _layer_bwd(dh, acts[l], ws[l], l, p, late)
        got.update(got_l)
        late = _exchange_items(layer_grads[l], p, LATE + WHOLE)
        if l > 0:
            (dh,), _ = dh_all(dseg, ws[l]["w_in"], du1, f"dh_{l}")
    pack = _pack_small([jnp.stack([layer_grads[k][n] for k in range(DEPTH)]) for n in REPLICATED_LAYER], 8)
    (grad_x, dg_in, db_in), carried = dh_all(dseg, ws[0]["w_in"], du1, "dh_0", rider=exchange_rider(late + [(pack[0], 0, -1, pack.shape[1:])]),
                                             norm=(xt, ln_in_g[None], ln_in_b[None]))
    got[(LATE + WHOLE, 0)], got_layer_pack = carried[:-1], carried[-1]
    pack = _pack_small([dg_in[0], db_in[0]], 8)
    got_entry_pack = comm_call(exchange_rider([(pack[0], 0, -1, pack.shape[1:])]), "grads_exchange_entry_norm")[0]
    rcv = {(n, l): arr for (names, l), arrs in got.items() for n, arr in zip(names, arrs)}

    res = [{}, {}, {}, {}]
    for n in tuple(SHARDED) + WHOLE:
        shp = p[n].shape
        lead = math.prod(shp[1:-2])
        to3 = lambda t: t.reshape((-1,) + shp[-2:])
        outs = reduce_adamw([rcv[(n, l)].reshape((N_DEV, lead) + shp[-2:]) for l in range(DEPTH)],
                            to3(p[n]), to3(args["m_" + n]), to3(args["v_" + n]), f"adamw_{n}")
        for k in range(4):
            res[k][n] = outs[k].reshape(shp)
    for names, got_pack, rows, name in ((REPLICATED_LAYER, got_layer_pack, 8, "adamw_replicated"), (REPLICATED_ENTRY, got_entry_pack, 8, "adamw_entry_norm")):
        outs = reduce_adamw([got_pack[:, None]], _pack_small([p[n] for n in names], rows), _pack_small([args["m_" + n] for n in names], rows),
                            _pack_small([args["v_" + n] for n in names], rows), name)
        off = 0
        for n in names:
            shp = p[n].shape
            cnt = math.prod(shp)
            for k in range(4):
                res[k][n] = outs[k].reshape(-1)[off:off + cnt].reshape(shp)
            off += cnt
    return (loss, grad_x[None], *[res[0][n] for n in WEIGHTS], *[res[1][n] for n in WEIGHTS],
            *[res[2][n] for n in WEIGHTS], *[res[3][n] for n in WEIGHTS])
```

```python
import collections
import functools
import math

import jax
import jax.numpy as jnp
from jax import lax
from jax.experimental import pallas as pl
from jax.experimental.pallas import tpu as pltpu

F32 = jnp.float32
BF = jnp.bfloat16

D_MODEL = 1024
DEPTH = 2
BRANCH_W = 512
D_FF = 4096
LN_EPS = 1e-5
NORM_EPS = 1e-6
ALPHA = (2 * DEPTH) ** 0.25
N_DEV = 8
LANES = 128
ADAM_LR, ADAM_B1, ADAM_B2, ADAM_EPS, ADAM_WD, ADAM_STEP = 0.001, 0.9, 0.999, 1e-08, 0.01, 10

DT0, BETA0, A0, FF0 = 0, 8, 12, 16
IN_SIZES = (512, 1024, 8, 1536, 4, 4, 512, 1024, 1536, 8, 4096)
_OFF = [0]
for _s in IN_SIZES:
    _OFF.append(_OFF[-1] + _s)
D_IN = _OFF[-1]
SEGS = (("z", 0), ("xbc", 1), ("dnqkv", 3), ("dngate", 6), ("sguv", 7), ("foxqkv", 8), ("gates", 10))
SMALL_SRC = ((2, DT0), (4, BETA0), (5, A0), (9, FF0))

NN = ((1,), (0,))
NT = ((1,), (1,))
TN = ((0,), (0,))
_DIMS = {"nn": NN, "nt": NT, "tn": TN}


def _pcall(body, **kw):
    return pl.pallas_call(body, **kw)


def _S(shape, dtype=F32):
    return jax.ShapeDtypeStruct(tuple(shape), dtype)


def _iota(shape, dim):
    return lax.broadcasted_iota(jnp.int32, shape, dim)


def _dotb(a, b, dims):
    return lax.dot_general(a, b, (dims, ((), ())), preferred_element_type=F32)


def _split2(a):
    ah = a.astype(BF)
    return ah, (a - ah.astype(F32)).astype(BF)


def _split3(a):
    a1 = a.astype(BF)
    r = a - a1.astype(F32)
    a2 = r.astype(BF)
    a3 = (r - a2.astype(F32)).astype(BF)
    return a1, a2, a3


def _mm_raw(a, b, form, mode):
    d = _DIMS[form]
    if mode == "1":
        return _dotb(a.astype(BF), b.astype(BF), d)
    if mode == "3":
        ah, al = _split2(a)
        bh, bl = _split2(b)
        return _dotb(ah, bh, d) + (_dotb(ah, bl, d) + _dotb(al, bh, d))
    if mode == "xa":
        ab = a.astype(BF)
        b1, b2, b3 = _split3(b)
        return _dotb(ab, b1, d) + (_dotb(ab, b2, d) + _dotb(ab, b3, d))
    bb = b.astype(BF)
    a1, a2, a3 = _split3(a)
    return _dotb(a1, bb, d) + (_dotb(a2, bb, d) + _dotb(a3, bb, d))


@functools.partial(jax.custom_vjp, nondiff_argnums=(2, 3))
def mm(a, b, form, mode):
    return _mm_raw(a, b, form, mode)


def _mm_fwd(a, b, form, mode):
    return _mm_raw(a, b, form, mode), (a, b)


_XA_DB = {"nn": "xa", "nt": "xb", "tn": "xa"}
_XB_DA = {"nn": "xb", "nt": "xb", "tn": "xa"}


def _mm_bwd(form, mode, res, g):
    a, b = res
    ma = _XB_DA[form] if mode == "xb" else mode
    mb = _XA_DB[form] if mode == "xa" else mode
    da = db = None
    if mode != "xa":
        da = {"nn": lambda: mm(g, b, "nt", ma), "nt": lambda: mm(g, b, "nn", ma), "tn": lambda: mm(b, g, "nt", ma)}[form]()
    if mode != "xb":
        db = {"nn": lambda: mm(a, g, "tn", mb), "nt": lambda: mm(g, a, "tn", mb), "tn": lambda: mm(a, g, "nn", mb)}[form]()
    if da is None:
        da = jnp.zeros_like(a)
    if db is None:
        db = jnp.zeros_like(b)
    return da, db


mm.defvjp(_mm_fwd, _mm_bwd)


def _silu(x):
    return x * jax.nn.sigmoid(x)


def _ln(x, g, b):
    mu = jnp.mean(x, -1, keepdims=True)
    xc = x - mu
    var = jnp.mean(xc * xc, -1, keepdims=True)
    return xc * lax.rsqrt(var + LN_EPS) * g + b


def _pick(n, cap):
    if n <= cap:
        return n
    best = LANES
    for t in range(LANES, cap + 1, LANES):
        if n % t == 0:
            best = t
    return best


def transpose_bf16(a, name):
    T, C = a.shape
    tt = min(T, 512)

    def body(a_ref, o_ref):
        o_ref[...] = a_ref[...].T.astype(BF)

    return _pcall(body, grid=(T // tt,), in_specs=[pl.BlockSpec((tt, C), lambda t: (t, 0))], out_specs=pl.BlockSpec((C, tt), lambda t: (0, t)),
                  out_shape=_S((C, T), BF), name=name)(a)


def matmul_dw(a_t, b, name):
    M, K = a_t.shape
    N = b.shape[1]
    tm, tn, tk = min(M, 1024), _pick(N, 1024), _pick(K, 1024)
    nk = K // tk

    def body(a_ref, b_ref, o_ref, acc):
        k = pl.program_id(2)
        p = _dotb(a_ref[...], b_ref[...].astype(BF), NN)

        @pl.when(k == 0)
        def _():
            acc[...] = p

        @pl.when(k > 0)
        def _():
            acc[...] += p

        @pl.when(k == nk - 1)
        def _():
            o_ref[...] = acc[...].astype(BF)

    return _pcall(body, grid=(N // tn, M // tm, nk),
                  in_specs=[pl.BlockSpec((tm, tk), lambda j, i, k: (i, k)), pl.BlockSpec((tk, tn), lambda j, i, k: (k, j))],
                  out_specs=pl.BlockSpec((tm, tn), lambda j, i, k: (i, j)), out_shape=_S((M, N), BF),
                  scratch_shapes=[pltpu.VMEM((tm, tn), F32)], name=name)(a_t, b)


def matmul_tn(a, b, name, b_col0=0, n_cols=None, out_dtype=F32):
    T, M = a.shape
    N = b.shape[1] if n_cols is None else n_cols
    tm, tn, tt = _pick(M, 512), _pick(N, 1024), min(T, 512)
    nt = T // tt
    jb = b_col0 // tn

    def body(a_ref, b_ref, o_ref, acc):
        t = pl.program_id(2)
        p = _dotb(a_ref[...].astype(BF), b_ref[...].astype(BF), TN)

        @pl.when(t == 0)
        def _():
            acc[...] = p

        @pl.when(t > 0)
        def _():
            acc[...] += p

        @pl.when(t == nt - 1)
        def _():
            o_ref[...] = acc[...].astype(out_dtype)

    return _pcall(body, grid=(M // tm, N // tn, nt),
                  in_specs=[pl.BlockSpec((tt, tm), lambda i, j, t: (t, i)), pl.BlockSpec((tt, tn), lambda i, j, t: (t, jb + j))],
                  out_specs=pl.BlockSpec((tm, tn), lambda i, j, t: (i, j)), out_shape=_S((M, N), out_dtype),
                  scratch_shapes=[pltpu.VMEM((tm, tn), F32)], name=name)(a, b)


def _pieces(v):
    if v.ndim == 3:
        return [v[i] for i in range(v.shape[0])]
    n = v.shape[1] // LANES
    if n <= 1:
        return [v]
    return [v[:, i * LANES:(i + 1) * LANES] for i in range(n)]


def _join(ps, like_ndim):
    if like_ndim == 3:
        return jnp.stack(ps, axis=0)
    return ps[0] if len(ps) == 1 else jnp.concatenate(ps, axis=1)


def scan_fwd(name, f, xs, ps, ys, state_shape, nc, nh=1, rider=None):
    nx, npar, ny = len(xs), len(ps), len(ys)

    def body(*refs):
        x_refs, p_refs = refs[:nx], refs[nx:nx + npar]
        y_refs = refs[nx + npar:nx + npar + ny]
        st_out, st = refs[nx + npar + ny], refs[nx + npar + ny + 1]
        c, h = pl.program_id(0), pl.program_id(1)

        @pl.when(c == 0)
        def _():
            st[h] = jnp.zeros(state_shape, F32)

        S = st[h]
        st_out[...] = S
        yv, Sn = f([_pieces(r[...]) for r in x_refs], [_pieces(r[...]) for r in p_refs], _pieces(S), h)
        for r, v in zip(y_refs, yv):
            r[...] = _join(v, 2)
        st[h] = _join(Sn, 3)

    in_specs = [pl.BlockSpec(bs, im) for (_, bs, im) in xs]
    in_specs += [pl.BlockSpec(p.shape, (lambda c, h, n=p.ndim: (0,) * n)) for p in ps]
    out_specs = [pl.BlockSpec(bs, im) for (_, bs, im) in ys]
    out_specs.append(pl.BlockSpec((None, None) + tuple(state_shape), lambda c, h: (c, h, 0, 0, 0)))
    out_shape = [_S(s) for (s, _, _) in ys] + [_S((nc, nh) + tuple(state_shape))]
    return hosted_call(body, rider, grid=(nc, nh), in_specs=in_specs, out_specs=out_specs, out_shape=out_shape,
                       scratch_shapes=[pltpu.VMEM((nh,) + tuple(state_shape), F32)], name=name, args=[*[x[0] for x in xs], *ps])


def scan_bwd(name, f, xs, ps, ys, dys, states, state_shape, nc, nh=1, shared=(), rider=None):
    nx, npar, ny = len(xs), len(ps), len(ys)

    def body(*refs):
        x_refs, p_refs = refs[:nx], refs[nx:nx + npar]
        s_ref = refs[nx + npar]
        dy_refs = refs[nx + npar + 1:nx + npar + 1 + ny]
        o = nx + npar + 1 + ny
        dx_refs, dp_refs, dst = refs[o:o + nx], refs[o + nx:o + nx + npar], refs[o + nx + npar]
        c, h = pl.program_id(0), pl.program_id(1)

        @pl.when(c == 0)
        def _():
            dst[h] = jnp.zeros(state_shape, F32)

        @pl.when((c == 0) & (h == 0))
        def _():
            for r in dp_refs:
                r[...] = jnp.zeros(r.shape, F32)

        xv = [_pieces(r[...]) for r in x_refs]
        pv = [_pieces(r[...]) for r in p_refs]
        _, vjp = jax.vjp(lambda a, b, s: f(a, b, s, h), xv, pv, _pieces(s_ref[...]))
        dxv, dpv, dS = vjp(([_pieces(r[...]) for r in dy_refs], _pieces(dst[h])))
        for i, (r, v) in enumerate(zip(dx_refs, dxv)):
            if i in shared and nh > 1:
                @pl.when(h == 0)
                def _(r=r, v=v):
                    r[...] = _join(v, 2)

                @pl.when(h > 0)
                def _(r=r, v=v):
                    r[...] += _join(v, 2)
            else:
                r[...] = _join(v, 2)
        for r, v in zip(dp_refs, dpv):
            r[...] += _join(v, len(r.shape))
        dst[h] = _join(dS, 3)

    def rev(im):
        return lambda c, h: im(nc - 1 - c, h)

    in_specs = [pl.BlockSpec(bs, rev(im)) for (_, bs, im) in xs]
    in_specs += [pl.BlockSpec(p.shape, (lambda c, h, n=p.ndim: (0,) * n)) for p in ps]
    in_specs.append(pl.BlockSpec((None, None) + tuple(state_shape), lambda c, h: (nc - 1 - c, h, 0, 0, 0)))
    in_specs += [pl.BlockSpec(bs, rev(im)) for (_, bs, im) in ys]
    out_specs = [pl.BlockSpec(bs, rev(im)) for (_, bs, im) in xs]
    out_specs += [pl.BlockSpec(p.shape, (lambda c, h, n=p.ndim: (0,) * n)) for p in ps]
    out_shape = [_S(x[0].shape) for x in xs] + [_S(p.shape) for p in ps]
    return hosted_call(body, rider, grid=(nc, nh), in_specs=in_specs, out_specs=out_specs, out_shape=out_shape,
                       scratch_shapes=[pltpu.VMEM((nh,) + tuple(state_shape), F32)], name=name,
                       args=[*[x[0] for x in xs], *ps, states, *dys])


def _lane():
    return _iota((1, LANES), 1)


def _col(v, idx):
    return jnp.sum(v * (_lane() == idx).astype(F32), axis=1, keepdims=True)


def _last_row(v):
    r = v.shape[0]
    return jnp.sum(v * (_iota((r, 1), 0) == r - 1).astype(F32), axis=0, keepdims=True)


def _tril(n, strict=False):
    r, c = _iota((n, n), 0), _iota((n, n), 1)
    return (r > c) if strict else (r >= c)


def ssd_chunk(xs, ps, S, h):
    zp, xbc, (sm,) = xs
    (bias,), (alog,), (dsk,), nw = ps
    Q = SSD_CHUNK
    n = sm.shape[0] // Q
    I = range(8 * n)
    lane = _lane()
    a128 = jnp.where(lane < 8, -jnp.exp(alog), 0.0)
    tri = _tril(Q)
    zc, xc = [_chunks(x, Q) for x in zp], [_chunks(x, Q) for x in xbc]
    dtl = [jax.nn.softplus(s + bias) for s in _chunks(sm, Q)]
    cum = [mm(tri.astype(F32), d * a128, "nn", "xa") for d in dtl]
    sel8 = (_iota((8, LANES), 0) == _iota((8, LANES), 1)).astype(F32)
    cum_t = [mm(sel8, c, "nt", "xa") for c in cum]
    m0 = (lane < 64).astype(F32)
    rows0 = (_iota((LANES, 1), 0) < 64).astype(F32)
    me = [m0 if i % 2 == 0 else 1.0 - m0 for i in I]
    Bm, Cm = [xc[4 + (i % 8) // 4][i // 8] for i in I], [xc[6 + (i % 8) // 4][i // 8] for i in I]
    cb = [mm(xc[6 + g][c], xc[4 + g][c], "nt", "1") for c in range(n) for g in range(2)]
    col = [_col(cum[i // 8], i % 8) for i in I]
    row = [jnp.sum(cum_t[i // 8] * (_iota((8, 1), 0) == i % 8).astype(F32), axis=0, keepdims=True) for i in I]
    xh = [xc[(i % 8) // 2][i // 8] * me[i] for i in I]
    xdt = [xh[i] * _col(dtl[i // 8], i % 8) for i in I]
    seg = [jnp.exp(jnp.where(tri, col[i] - row[i], -jnp.inf)) for i in I]
    last = [_last_row(col[i]) for i in I]
    y_diag = [mm(cb[2 * (i // 8) + (i % 8) // 4] * seg[i], xdt[i], "nn", "1") for i in I]
    st = [mm(xdt[i], Bm[i] * jnp.exp(last[i] - col[i]), "tn", "1") for i in I]
    c_dec = [Cm[i] * jnp.exp(col[i]) for i in I]
    skip = [_col(dsk, i % 8) * xh[i] for i in I]
    out = []
    for c in range(n):
        J = range(8 * c, 8 * c + 8)
        y = [y_diag[i] + mm(c_dec[i], S[(i % 8) // 2], "nt", "1") * me[i] + skip[i] for i in J]
        S = [S[pr] * (jnp.exp(last[8 * c + 2 * pr]) * rows0 + jnp.exp(last[8 * c + 2 * pr + 1]) * (1.0 - rows0))
             + st[8 * c + 2 * pr] + st[8 * c + 2 * pr + 1] for pr in range(4)]
        yz = [(y[2 * pr] + y[2 * pr + 1]) * _silu(zc[pr][c]) for pr in range(4)]
        ssq = sum(jnp.sum(v * v, axis=1, keepdims=True) for v in yz)
        scale = lax.rsqrt(ssq / BRANCH_W + NORM_EPS)
        out.append([yz[pr] * scale * nw[pr] for pr in range(4)])
    return [[jnp.concatenate([out[c][pr] for c in range(n)], axis=0) if n > 1 else out[0][pr] for pr in range(4)]], S


@jax.custom_vjp
def _halves(x):
    r = x.shape[0] // 2
    return x[:r], x[r:]


_halves.defvjp(lambda x: (_halves(x), None), lambda _, g: (jnp.concatenate(g, axis=0),))
DN_CHUNK = 64
DN_ROWS = 256
SSD_CHUNK = 128
SSD_ROWS = 256


def _chunks(x, rows=DN_CHUNK):
    if x.shape[0] == rows:
        return [x]
    a, b = _halves(x)
    return _chunks(a, rows) + _chunks(b, rows)


def dn_chunk(xs, ps, S, h):
    act, gate, (sm,) = xs
    (alog,), (dtb,), (nw,) = ps
    C = DN_CHUNK
    n = sm.shape[0] // C
    I = range(4 * n)
    lane = _lane()
    tri, strict = _tril(C), _tril(C, True)
    sm2 = _chunks(sm)
    G = [jnp.where((lane >= A0) & (lane < A0 + 4), -jnp.exp(alog) * jax.nn.softplus(s + dtb), 0.0) for s in sm2]
    gcs = [mm(tri.astype(F32), g, "nn", "xa") for g in G]
    sig = [jax.nn.sigmoid(s) for s in sm2]
    parts = [_chunks(x) for x in act]
    q, k, v = ([parts[o + i % 4][i // 4] for i in I] for o in (0, 4, 8))
    gt = [_chunks(x) for x in gate]
    qn = [q[i] * lax.rsqrt(jnp.sum(q[i] * q[i], axis=1, keepdims=True) + NORM_EPS) * (LANES ** -0.5) for i in I]
    kn = [k[i] * lax.rsqrt(jnp.sum(k[i] * k[i], axis=1, keepdims=True) + NORM_EPS) for i in I]
    beta = [_col(sig[i // 4], BETA0 + i % 4) for i in I]
    gcol = [_col(gcs[i // 4], A0 + i % 4) for i in I]
    selr = [((_iota((8, LANES), 0) == 0) & (_iota((8, LANES), 1) == A0 + h)).astype(F32) for h in range(4)]
    grow = [jnp.sum(mm(selr[i % 4], gcs[i // 4], "nt", "xa"), axis=0, keepdims=True) for i in I]
    gamma = [jnp.exp(jnp.where(tri, gcol[i] - grow[i], -jnp.inf)) for i in I]
    kb = [kn[i] * beta[i] for i in I]
    pk = [-(mm(kb[i], kn[i], "nt", "1") * jnp.where(strict, gamma[i], 0.0)) for i in I]
    eye = (_iota((C, C), 0) == _iota((C, C), 1)).astype(F32)
    minv = [eye + pk[i] for i in I]
    for _ in range(5):
        pk = [mm(pk[i], pk[i], "nn", "3") for i in I]
        minv = [minv[i] + mm(minv[i], pk[i], "nn", "3") for i in I]
    eg = [jnp.exp(gcol[i]) for i in I]
    w = [mm(minv[i], kb[i] * eg[i], "nn", "3") for i in I]
    u = [mm(minv[i], v[i] * beta[i], "nn", "3") for i in I]
    glast = [_last_row(gcol[i]) for i in I]
    qg = [qn[i] * eg[i] for i in I]
    qk = [mm(qn[i], kn[i], "nt", "1") * gamma[i] for i in I]
    kdec = [kn[i] * jnp.exp(glast[i] - gcol[i]) for i in I]
    y = []
    for c in range(n):
        J = range(4 * c, 4 * c + 4)
        vnew = [u[i] - mm(w[i], S[i % 4], "nn", "1") for i in J]
        o = [mm(qg[i], S[i % 4], "nn", "1") + mm(qk[i], vn, "nn", "1") for i, vn in zip(J, vnew)]
        S = [S[i % 4] * jnp.exp(glast[i]) + mm(kdec[i], vn, "tn", "1") for i, vn in zip(J, vnew)]
        on = [x * lax.rsqrt(jnp.mean(x * x, axis=1, keepdims=True) + NORM_EPS) * nw for x in o]
        y.append([on[h] * _silu(gt[h][c]) for h in range(4)])
    return [[jnp.concatenate([y[c][h] for c in range(n)], axis=0) for h in range(4)]], S


def sg_chunk(xs, ps, S, h):
    (uv,) = xs
    lng, lnb, W, (bt,) = ps
    u = [jax.nn.gelu(p) for p in uv[:4]]
    v = [jax.nn.gelu(p) for p in uv[4:]]
    mu = sum(jnp.sum(p, axis=1, keepdims=True) for p in v) / BRANCH_W
    vc = [p - mu for p in v]
    var = sum(jnp.sum(p * p, axis=1, keepdims=True) for p in vc) / BRANCH_W
    inv = lax.rsqrt(var + LN_EPS)
    trif = _tril(W[0].shape[0]).astype(F32)
    out = []
    for g in range(4):
        vn = vc[g] * inv * lng[g] + lnb[g]
        out.append(u[g] * (mm(W[g] * trif, vn, "nn", "1") + _col(bt, g)))
    return [out], S


def foxc_chunk(xs, ps, S, h):
    (sm,), ((fb,),), (carry,) = xs[0], ps, S
    lane = _lane()
    ls = jnp.where((lane >= FF0) & (lane < FF0 + 8), jax.nn.log_sigmoid(sm + fb), 0.0)
    c = mm(_tril(sm.shape[0]).astype(F32), ls, "nn", "xa") + carry
    return [[c]], [_last_row(c)]


HALO = 8


def _conv_tiles(T, C):
    return min(T, 512), _pick(C, 512)


def conv_fwd(x, w, b, name):
    T, C = x.shape
    tm, cb = _conv_tiles(T, C)

    def body(xp_ref, x_ref, w_ref, b_ref, o_ref):
        i = pl.program_id(1)
        e = jnp.concatenate([xp_ref[...] * (i > 0).astype(F32), x_ref[...]], axis=0)
        pre = b_ref[...] + sum(w_ref[k:k + 1, :] * e[5 + k:5 + k + tm, :] for k in range(4))
        o_ref[...] = _silu(pre)

    hb = tm // HALO
    return _pcall(body, grid=(C // cb, T // tm),
                  in_specs=[pl.BlockSpec((HALO, cb), lambda j, i: (jnp.maximum(i * hb - 1, 0), j)), pl.BlockSpec((tm, cb), lambda j, i: (i, j)),
                            pl.BlockSpec((4, cb), lambda j, i: (0, j)), pl.BlockSpec((1, cb), lambda j, i: (0, j))],
                  out_specs=pl.BlockSpec((tm, cb), lambda j, i: (i, j)), out_shape=_S((T, C)), name=name)(x, x, w, b)


def conv_bwd(x, w, b, dact, name):
    T, C = x.shape
    tm, cb = _conv_tiles(T, C)
    nt = T // tm

    def body(xp_ref, x_ref, xn_ref, w_ref, b_ref, d_ref, dn_ref, dx_ref, dw_ref, db_ref):
        i = pl.program_id(1)
        has_prev, has_next = (i > 0).astype(F32), (i < nt - 1).astype(F32)
        e = jnp.concatenate([xp_ref[...] * has_prev, x_ref[...], xn_ref[...] * has_next], axis=0)
        pre = b_ref[...] + sum(w_ref[k:k + 1, :] * e[5 + k:5 + k + tm + 8, :] for k in range(4))
        de = jnp.concatenate([d_ref[...], dn_ref[...] * has_next], axis=0)
        sg = jax.nn.sigmoid(pre)
        dpre = de * (sg * (1.0 + pre * (1.0 - sg)))
        dx_ref[...] = sum(w_ref[k:k + 1, :] * dpre[3 - k:3 - k + tm, :] for k in range(4))
        dcur = dpre[0:tm, :]
        dw = jnp.concatenate([jnp.sum(dcur * e[5 + k:5 + k + tm, :], axis=0, keepdims=True) for k in range(4)], axis=0)
        db = jnp.sum(dcur, axis=0, keepdims=True)

        @pl.when(i == 0)
        def _():
            dw_ref[...] = dw
            db_ref[...] = db

        @pl.when(i > 0)
        def _():
            dw_ref[...] += dw
            db_ref[...] += db

    blk = lambda f: pl.BlockSpec((tm, cb), f)
    hb = tm // HALO
    before = pl.BlockSpec((HALO, cb), lambda j, i: (jnp.maximum(i * hb - 1, 0), j))
    after = pl.BlockSpec((HALO, cb), lambda j, i: (jnp.minimum((i + 1) * hb, nt * hb - 1), j))
    return _pcall(body, grid=(C // cb, nt),
                  in_specs=[before, blk(lambda j, i: (i, j)), after,
                            pl.BlockSpec((4, cb), lambda j, i: (0, j)), pl.BlockSpec((1, cb), lambda j, i: (0, j)),
                            blk(lambda j, i: (i, j)), after],
                  out_specs=[blk(lambda j, i: (i, j)), pl.BlockSpec((4, cb), lambda j, i: (0, j)), pl.BlockSpec((1, cb), lambda j, i: (0, j))],
                  out_shape=[_S((T, C)), _S((4, C)), _S((1, C))], name=name)(x, x, x, w, b, dact, dact)


FOX_SCALE = 64 ** -0.5
LOG2E = 1.4426950408889634


def _spare(e, i):
    return (_lane() == 64 * (1 - e) + i).astype(F32)


def _lanes_of(e):
    lane = _lane()
    return ((lane < 64) if e == 0 else (lane >= 64)).astype(F32)


def _col3(col, e, first):
    c1 = col.astype(BF).astype(F32)
    c2 = (col - c1).astype(BF).astype(F32)
    c3 = (col - c1 - c2).astype(BF).astype(F32)
    return c1 * _spare(e, first) + c2 * _spare(e, first + 1) + c3 * _spare(e, first + 2)


def _split3_dense(x):
    x1 = x.astype(BF)
    r = x - x1.astype(F32)
    x2 = r.astype(BF)
    x3 = (r - x2.astype(F32)).astype(BF)
    return jnp.concatenate([x1, x2, x3], axis=1)


def _place(src_lane, e, first, sign):
    r, m = _iota((3 * LANES, LANES), 0), _iota((3 * LANES, LANES), 1)
    hit = (r % LANES == src_lane) & (m == 64 * (1 - e) + first + r // LANES)
    return jnp.where(hit, sign, 0.0).astype(BF)


def _ones3(e, first):
    return _spare(e, first) + _spare(e, first + 1) + _spare(e, first + 2)


def _causal_bias(n):
    return jnp.where(_iota((n, n), 0) >= _iota((n, n), 1), 0.0, -jnp.inf).astype(F32)


def _pair_spec(tq, row_of):
    return pl.BlockSpec((None, 2, tq, LANES), lambda hp, a, b: (hp, 0, row_of(a, b), 0))


def fox_prep(qkv, ccol, name):
    T = qkv.shape[0]
    tq = min(T, 512)

    def body(q_ref, k_ref, v_ref, cc_ref, qa_ref, ka_ref, va_ref):
        hp = pl.program_id(0)
        q, k, v = q_ref[...], k_ref[...], v_ref[...]
        parts = _split3_dense(cc_ref[...] * LOG2E)
        for e in range(2):
            me = _lanes_of(e)
            src = FF0 + 2 * hp + e
            qa_ref[e] = (q * me * (FOX_SCALE * LOG2E) + _dotb(parts, _place(src, e, 0, 1.0), NN) + _ones3(e, 3)).astype(BF)
            ka_ref[e] = (k * me + _ones3(e, 0) + _dotb(parts, _place(src, e, 3, -1.0), NN) + _ones3(e, 6)).astype(BF)
            va_ref[e] = (v * me + (1.0 - me)).astype(BF)

    blk = lambda off: pl.BlockSpec((tq, LANES), lambda hp, i: (i, off + hp))
    out = pl.BlockSpec((None, 2, tq, LANES), lambda hp, i: (hp, 0, i, 0))
    return _pcall(body, grid=(4, T // tq), in_specs=[blk(0), blk(4), blk(8), pl.BlockSpec((tq, LANES), lambda hp, i: (i, 0))],
                  out_specs=[out] * 3, out_shape=[_S((4, 2, T, LANES), BF)] * 3, name=name)(qkv, qkv, qkv, ccol)


def fox_fwd(qa, ka, va, name, rider=None):
    T = qa.shape[2]
    tq = min(T, 512)
    nq = T // tq

    def body(qa_ref, ka_ref, va_ref, o_ref, lse_ref, m_s, acc, causal):
        i, j = pl.program_id(1), pl.program_id(2)

        @pl.when((pl.program_id(0) == 0) & (i == 0) & (j == 0))
        def _():
            causal[...] = _causal_bias(tq)

        @pl.when(j == 0)
        def _():
            m_s[...] = jnp.full(m_s.shape, -jnp.inf, F32)
            acc[...] = jnp.zeros(acc.shape, F32)

        def step(diagonal):
            for e in range(2):
                s = _dotb(qa_ref[e], ka_ref[e], NT)
                if diagonal:
                    s = s + causal[...]
                m_old = m_s[e]
                m_new = jnp.maximum(m_old, jnp.max(s, axis=1, keepdims=True))
                p = jnp.exp2(s - m_new)
                m_s[e] = m_new
                acc[e] = acc[e] * jnp.exp2(m_old - m_new) + _dotb(p.astype(BF), va_ref[e], NN)

        @pl.when(j < i)
        def _():
            step(False)

        @pl.when(j == i)
        def _():
            step(True)
            lane = _lane()
            o, lse = 0.0, 0.0
            for e in range(2):
                me = _lanes_of(e)
                a = acc[e]
                l = jnp.where(me > 0.0, pltpu.roll(a, 64, 1), a)
                o = o + a * me / l
                lse = lse + (m_s[e] + jnp.log2(l)) * (lane == e).astype(F32)
            o_ref[...] = o
            lse_ref[...] = lse

    kv = _pair_spec(tq, lambda i, j: jnp.minimum(j, i))
    return hosted_call(body, rider, grid=(4, nq, nq), in_specs=[_pair_spec(tq, lambda i, j: i), kv, kv],
                       out_specs=[pl.BlockSpec((tq, LANES), lambda hp, i, j: (i, hp)), pl.BlockSpec((None, tq, LANES), lambda hp, i, j: (hp, i, 0))],
                       out_shape=[_S((T, BRANCH_W)), _S((4, T, LANES))],
                       scratch_shapes=[pltpu.VMEM((2, tq, 1), F32), pltpu.VMEM((2, tq, LANES), F32), pltpu.VMEM((tq, tq), F32)],
                       name=name, args=[qa, ka, va])


def fox_prep_bwd(qkv, qa, o, lse, do, name):
    T = qkv.shape[0]
    tq = min(T, 512)

    def body(q_ref, k_ref, qa_ref, o_ref, lse_ref, do_ref, qb_ref, doa_ref, qd_ref, kd_ref):
        q, k, dov = q_ref[...], k_ref[...], do_ref[...]
        dd = dov * o_ref[...]
        lse_parts = _split3_dense(lse_ref[...])
        for e in range(2):
            me = _lanes_of(e)
            qb_ref[e] = (qa_ref[e].astype(F32) + _dotb(lse_parts, _place(e, e, 6, -1.0), NN)).astype(BF)
            doa_ref[e] = (dov * me + _col3(-jnp.sum(dd * me, axis=1, keepdims=True), e, 0)).astype(BF)
            qd_ref[e] = (q * me * FOX_SCALE + (1.0 - me)).astype(BF)
            kd_ref[e] = (k * me * FOX_SCALE + (1.0 - me)).astype(BF)

    blk = lambda off: pl.BlockSpec((tq, LANES), lambda hp, i: (i, off + hp))
    pair = pl.BlockSpec((None, 2, tq, LANES), lambda hp, i: (hp, 0, i, 0))
    return _pcall(body, grid=(4, T // tq),
                  in_specs=[blk(0), blk(4), pair, blk(0), pl.BlockSpec((None, tq, LANES), lambda hp, i: (hp, i, 0)), blk(0)],
                  out_specs=[pair] * 4, out_shape=[_S((4, 2, T, LANES), BF)] * 4, name=name)(qkv, qkv, qa, o, lse, do)


def fox_bwd(qb, ka, va, doa, qd, kd, name, rider=None):
    T = qb.shape[2]
    tq = min(T, 512)
    nq = T // tq

    def body(qb_ref, ka_ref, va_ref, doa_ref, qd_ref, kd_ref, dq_ref, dk_ref, dv_ref, dcc_ref, dq_s, dk_s, dv_s, causal):
        hp, j, ii = pl.program_id(0), pl.program_id(1), pl.program_id(2)
        i = jnp.maximum(ii, j)

        @pl.when((hp == 0) & (j == 0) & (ii == 0))
        def _():
            dcc_ref[...] = jnp.zeros(dcc_ref.shape, F32)
            causal[...] = _causal_bias(tq)

        @pl.when((j == 0) & (ii == 0))
        def _():
            dq_s[...] = jnp.zeros(dq_s.shape, F32)

        @pl.when(ii == 0)
        def _():
            dk_s[...] = jnp.zeros(dk_s.shape, F32)
            dv_s[...] = jnp.zeros(dv_s.shape, F32)

        def step(diagonal):
            rows = pl.ds(pl.multiple_of(i * tq, tq), tq)
            for e in range(2):
                s = _dotb(qb_ref[e], ka_ref[e], NT)
                if diagonal:
                    s = s + causal[...]
                p = jnp.exp2(s)
                ds = (p * _dotb(doa_ref[e], va_ref[e], NT)).astype(BF)
                dv_s[e] += _dotb(p.astype(BF), doa_ref[e], TN)
                dq_s[e, rows, :] += _dotb(ds, kd_ref[e], NN)
                dk_s[e] += _dotb(ds, qd_ref[e], TN)

        @pl.when(ii > j)
        def _():
            step(False)

        @pl.when(ii == j)
        def _():
            step(True)

        def fold(acc, sign):
            grad, dc = 0.0, 0.0
            for e in range(2):
                a = acc[e]
                grad = grad + a * _lanes_of(e)
                sums = pltpu.roll(a, 64, 1) if e == 0 else a
                dc = dc + sign * jnp.where(_lane() == FF0 + 2 * hp + e, sums, 0.0)
            return grad, dc

        @pl.when(ii == nq - 1)
        def _():
            grad, dc = fold(dk_s, -1.0)
            dk_ref[...] = grad
            dv_ref[...] = dv_s[0] * _lanes_of(0) + dv_s[1] * _lanes_of(1)
            dcc_ref[pl.ds(pl.multiple_of(j * tq, tq), tq), :] += dc

        @pl.when((j == nq - 1) & (ii == nq - 1))
        def _():
            grad, dc = fold(dq_s, 1.0)
            dq_ref[...] = grad
            dcc_ref[...] += dc

    irow, jrow = _pair_spec(tq, lambda j, ii: jnp.maximum(ii, j)), _pair_spec(tq, lambda j, ii: j)
    jout = pl.BlockSpec((tq, LANES), lambda hp, j, ii: (j, hp))
    return hosted_call(body, rider, grid=(4, nq, nq), in_specs=[irow, jrow, jrow, irow, irow, jrow],
                       out_specs=[pl.BlockSpec((T, LANES), lambda hp, j, ii: (0, hp)), jout, jout, pl.BlockSpec((T, LANES), lambda hp, j, ii: (0, 0))],
                       out_shape=[_S((T, BRANCH_W)), _S((T, BRANCH_W)), _S((T, BRANCH_W)), _S((T, LANES))],
                       scratch_shapes=[pltpu.VMEM((2, T, LANES), F32), pltpu.VMEM((2, tq, LANES), F32), pltpu.VMEM((2, tq, LANES), F32),
                                       pltpu.VMEM((tq, tq), F32)],
                       name=name, args=[qb, ka, va, doa, qd, kd])


def _acc_out(ref, val, first):
    @pl.when(first)
    def _():
        ref[...] = val

    @pl.when(jnp.logical_not(first))
    def _():
        ref[...] += val


def _row(tm, c):
    return pl.BlockSpec((tm, c), lambda i: (i, 0))


def _full(shape):
    return pl.BlockSpec(shape, lambda *_: (0,) * len(shape))


def ln_fwd(x, g, b, name, rider=None):
    T, C = x.shape
    tm = min(T, 512)

    def body(x_ref, g_ref, b_ref, o_ref):
        o_ref[...] = _ln(x_ref[...], g_ref[...], b_ref[...])

    return hosted_call(body, rider, grid=(T // tm,), in_specs=[_row(tm, C), _full((1, C)), _full((1, C))], out_specs=[_row(tm, C)],
                       out_shape=[_S((T, C))], scratch_shapes=[], name=name, args=[x, g, b])


def loss_head(h, target, name):
    T, C = h.shape
    tm = min(T, 512)

    def body(h_ref, t_ref, d_ref, l_ref):
        e = h_ref[...] - t_ref[...]
        d_ref[...] = e * (1.0 / C)
        part = jnp.sum(jnp.sum(e * e, axis=1, keepdims=True), axis=0, keepdims=True) * (0.5 / C)
        _acc_out(l_ref, part, pl.program_id(0) == 0)

    return _pcall(body, grid=(T // tm,), in_specs=[_row(tm, C), _row(tm, C)], out_specs=[_row(tm, C), _full((1, 1))],
                  out_shape=[_S((T, C)), _S((1, 1))], name=name)(h, target)


def add3(a, b, c, name):
    T, C = a.shape
    tm = min(T, 512)

    def body(a_ref, b_ref, c_ref, o_ref):
        o_ref[...] = a_ref[...] + b_ref[...] + c_ref[...]

    return _pcall(body, grid=(T // tm,), in_specs=[_row(tm, C)] * 3, out_specs=_row(tm, C), out_shape=_S((T, C)), name=name)(a, b, c)


def _wb_spec(l):
    return pl.BlockSpec((None, 4, BRANCH_W, D_MODEL), lambda *_: (l, 0, 0, 0))


def merge_fwd(ys, gl, gb, wb, l, name):
    T = gl.shape[0]
    tm = min(T, 256)

    def body(y0, y1, y2, y3, gl_ref, gb_ref, wb_ref, o_ref):
        acc = 0.0
        for i, y in enumerate((y0, y1, y2, y3)):
            z = _dotb(y[...].astype(BF), wb_ref[i], NN)
            g = jax.nn.sigmoid(gl_ref[:, i * D_MODEL:(i + 1) * D_MODEL] + gb_ref[i:i + 1, :])
            acc = acc + g * z
        o_ref[...] = acc

    return _pcall(body, grid=(T // tm,), in_specs=[_row(tm, BRANCH_W)] * 4 + [_row(tm, 4 * D_MODEL), _full((4, D_MODEL)), _wb_spec(l)],
                  out_specs=_row(tm, D_MODEL), out_shape=_S((T, D_MODEL)), name=name)(*ys, gl, gb, wb)


def merge_bwd(ys, gl, gb, wb, l, dm, name):
    T = gl.shape[0]
    tm = min(T, 256)

    def body(y0, y1, y2, y3, gl_ref, gb_ref, wb_ref, dm_ref, d0, d1, d2, d3, dgl_ref, dz_ref, dgb_ref):
        dmv = dm_ref[...]
        first = pl.program_id(0) == 0
        for i, (y, d) in enumerate(zip((y0, y1, y2, y3), (d0, d1, d2, d3))):
            cols = slice(i * D_MODEL, (i + 1) * D_MODEL)
            z = _dotb(y[...].astype(BF), wb_ref[i], NN)
            g = jax.nn.sigmoid(gl_ref[:, cols] + gb_ref[i:i + 1, :])
            dgl = dmv * z * (g * (1.0 - g))
            dz = (g * dmv).astype(BF)
            dgl_ref[:, cols] = dgl
            dz_ref[:, cols] = dz
            d[...] = _dotb(dz, wb_ref[i], NT)
            _acc_out(dgb_ref.at[i:i + 1, :], jnp.sum(dgl, axis=0, keepdims=True), first)

    return _pcall(body, grid=(T // tm,),
                  in_specs=[_row(tm, BRANCH_W)] * 4 + [_row(tm, 4 * D_MODEL), _full((4, D_MODEL)), _wb_spec(l), _row(tm, D_MODEL)],
                  out_specs=[_row(tm, BRANCH_W)] * 4 + [_row(tm, 4 * D_MODEL), _row(tm, 4 * D_MODEL), _full((4, D_MODEL))],
                  out_shape=[_S((T, BRANCH_W))] * 4 + [_S((T, 4 * D_MODEL)), _S((T, 4 * D_MODEL), BF), _S((4, D_MODEL))], name=name)(
                      *ys, gl, gb, wb, dm)


def _wout_spec(l):
    return pl.BlockSpec((None, D_MODEL, D_MODEL), lambda *_: (l, 0, 0))


def out_fwd(merged, h, wout, l, g, b, name):
    T = h.shape[0]
    tm = min(T, 512)

    def body(m_ref, h_ref, w_ref, g_ref, b_ref, u_ref, o_ref):
        u = ALPHA * h_ref[...] + _dotb(m_ref[...].astype(BF), w_ref[...], NN)
        u_ref[...] = u
        o_ref[...] = _ln(u, g_ref[...], b_ref[...])

    C = D_MODEL
    return _pcall(body, grid=(T // tm,), in_specs=[_row(tm, C), _row(tm, C), _wout_spec(l), _full((1, C)), _full((1, C))],
                  out_specs=[_row(tm, C), _row(tm, C)], out_shape=[_S((T, C)), _S((T, C))], name=name)(merged, h, wout, g, b)


def out_bwd(u, dy, g, b, wout, l, name):
    T, C = u.shape
    tm = min(T, 512)

    def body(u_ref, dy_ref, g_ref, b_ref, w_ref, du_ref, dm_ref, dg_ref, db_ref):
        _, vjp = jax.vjp(_ln, u_ref[...], g_ref[...], b_ref[...])
        du, dg, db = vjp(dy_ref[...])
        du_ref[...] = du
        dm_ref[...] = _dotb(du.astype(BF), w_ref[...], NT)
        first = pl.program_id(0) == 0
        _acc_out(dg_ref, dg, first)
        _acc_out(db_ref, db, first)

    return _pcall(body, grid=(T // tm,), in_specs=[_row(tm, C), _row(tm, C), _full((1, C)), _full((1, C)), _wout_spec(l)],
                  out_specs=[_row(tm, C), _row(tm, C), _full((1, C)), _full((1, C))],
                  out_shape=[_S((T, C)), _S((T, C)), _S((1, C)), _S((1, C))], name=name)(u, dy, g, b, wout)


def ff_fwd(h, wup, wdown, l, g, b, name):
    T, C = h.shape
    F = wup.shape[2]
    tm, tf = min(T, 1024), 1024
    nf = F // tf

    def body(h_ref, wu_ref, wd_ref, g_ref, b_ref, u_ref, o_ref, acc):
        f = pl.program_id(1)
        a = _dotb(h_ref[...].astype(BF), wu_ref[...], NN)
        r = jnp.square(jnp.maximum(a, 0.0))
        p = _dotb(r.astype(BF), wd_ref[...], NN)
        _acc_out(acc, p, f == 0)

        @pl.when(f == nf - 1)
        def _():
            u = ALPHA * h_ref[...] + acc[...]
            u_ref[...] = u
            o_ref[...] = _ln(u, g_ref[...], b_ref[...])

    row = pl.BlockSpec((tm, C), lambda i, f: (i, 0))
    return _pcall(body, grid=(T // tm, nf),
                  in_specs=[row, pl.BlockSpec((None, C, tf), lambda i, f: (l, 0, f)), pl.BlockSpec((None, tf, C), lambda i, f: (l, f, 0)),
                            _full((1, C)), _full((1, C))],
                  out_specs=[row, row], out_shape=[_S((T, C)), _S((T, C))], scratch_shapes=[pltpu.VMEM((tm, C), F32)], name=name)(h, wup, wdown, g, b)


def ff_bwd(u, dy, h, g, b, wup, wdown, l, name):
    T, C = h.shape
    F = wup.shape[2]
    tm, tf = min(T, 512), 1024
    nf = F // tf

    def body(u_ref, dy_ref, h_ref, g_ref, b_ref, wu_ref, wd_ref, du_ref, dh_ref, da_ref, r_ref, dg_ref, db_ref, du_s, acc):
        i, f = pl.program_id(0), pl.program_id(1)

        @pl.when(f == 0)
        def _():
            _, vjp = jax.vjp(_ln, u_ref[...], g_ref[...], b_ref[...])
            du, dg, db = vjp(dy_ref[...])
            du_s[...] = du
            du_ref[...] = du
            _acc_out(dg_ref, dg, i == 0)
            _acc_out(db_ref, db, i == 0)

        a = _dotb(h_ref[...].astype(BF), wu_ref[...], NN)
        ap = jnp.maximum(a, 0.0)
        dr = _dotb(du_s[...].astype(BF), wd_ref[...], NT)
        da = (dr * (2.0 * ap)).astype(BF)
        da_ref[...] = da
        r_ref[...] = jnp.square(ap).T.astype(BF)
        _acc_out(acc, _dotb(da, wu_ref[...], NT), f == 0)

        @pl.when(f == nf - 1)
        def _():
            dh_ref[...] = ALPHA * du_s[...] + acc[...]

    row = pl.BlockSpec((tm, C), lambda i, f: (i, 0))
    colf = pl.BlockSpec((tm, tf), lambda i, f: (i, f))
    return _pcall(body, grid=(T // tm, nf),
                  in_specs=[row, row, row, _full((1, C)), _full((1, C)), pl.BlockSpec((None, C, tf), lambda i, f: (l, 0, f)),
                            pl.BlockSpec((None, tf, C), lambda i, f: (l, f, 0))],
                  out_specs=[row, row, colf, pl.BlockSpec((tf, tm), lambda i, f: (f, i)), _full((1, C)), _full((1, C))],
                  out_shape=[_S((T, C)), _S((T, C)), _S((T, F), BF), _S((F, T), BF), _S((1, C)), _S((1, C))],
                  scratch_shapes=[pltpu.VMEM((tm, C), F32), pltpu.VMEM((tm, C), F32)], name=name)(u, dy, h, g, b, wup, wdown)


MESH_ID = pl.DeviceIdType.MESH
_ANY = pl.BlockSpec(memory_space=pl.ANY)


def _window(ref, ax, idx, n):
    if n < 0:
        return ref
    sel = idx if n == 0 else pl.ds(pl.multiple_of(idx * n, n), n)
    return ref.at[(slice(None),) * ax + (sel,)]


Rider = collections.namedtuple("Rider", "operands out_shape scratch start wait")


def hosted_call(body, rider, *, grid, in_specs, out_specs, out_shape, scratch_shapes, name, args):
    n_in, n_out, n_scr = len(in_specs), len(out_specs), len(scratch_shapes)
    if rider is None:
        return _pcall(body, grid=grid, in_specs=in_specs, out_specs=out_specs, out_shape=out_shape, scratch_shapes=scratch_shapes, name=name)(*args), []
    ri, ro = len(rider.operands), len(rider.out_shape)

    def wrapped(*refs):
        ins, r_in = refs[:n_in], refs[n_in:n_in + ri]
        o0 = n_in + ri
        outs, r_out = refs[o0:o0 + n_out], refs[o0 + n_out:o0 + n_out + ro]
        s0 = o0 + n_out + ro
        scr, r_scr = refs[s0:s0 + n_scr], refs[s0 + n_scr:]
        ids = [pl.program_id(i) for i in range(len(grid))]
        first = functools.reduce(jnp.logical_and, [i == 0 for i in ids])
        last = functools.reduce(jnp.logical_and, [i == g - 1 for i, g in zip(ids, grid)])

        @pl.when(first)
        def _():
            rider.start(r_in, r_out, r_scr)

        body(*ins, *outs, *scr)

        @pl.when(last)
        def _():
            rider.wait(r_in, r_out, r_scr)

    res = _pcall(wrapped, grid=grid, in_specs=list(in_specs) + [_ANY] * ri, out_specs=list(out_specs) + [_ANY] * ro,
                 out_shape=list(out_shape) + list(rider.out_shape), scratch_shapes=list(scratch_shapes) + list(rider.scratch),
                 name=name)(*args, *rider.operands)
    return res[:n_out], res[n_out:]


def comm_call(rider, name):
    ri = len(rider.operands)

    def body(*refs):
        r_in, r_out, r_scr = refs[:ri], refs[ri:ri + len(rider.out_shape)], refs[ri + len(rider.out_shape):]
        rider.start(r_in, r_out, r_scr)
        rider.wait(r_in, r_out, r_scr)

    return _pcall(body, in_specs=[_ANY] * ri, out_specs=[_ANY] * len(rider.out_shape), out_shape=list(rider.out_shape),
                  scratch_shapes=list(rider.scratch), name=name)(*rider.operands)


def gather_rider(shards, axes):
    K = len(shards)
    widths = [s.shape[a] for s, a in zip(shards, axes)]
    out_shape = [_S(s.shape[:a] + (N_DEV * s.shape[a],) + s.shape[a + 1:], s.dtype) for s, a in zip(shards, axes)]

    def plan(x_refs, o_refs, sems):
        send_sems, recv_sems, local_sems = sems
        mx, my, mc = lax.axis_index("x"), lax.axis_index("y"), lax.axis_index("c")
        me, sibling = (mx, my, mc), (mx, my, 1 - mc)
        chips = [(1 - mx, my), (mx, 1 - my), (1 - mx, 1 - my)]

        def win(k, px, py, pc):
            return _window(o_refs[k], axes[k], 4 * px + 2 * py + pc, widths[k])

        def copy(k, slot, block, to, src=None):
            return pltpu.make_async_remote_copy(src_ref=win(k, *block) if src is None else src, dst_ref=win(k, *block),
                                                send_sem=send_sems.at[7 * k + slot], recv_sem=recv_sems.at[7 * k + slot],
                                                device_id=to, device_id_type=MESH_ID)

        mine = [pltpu.make_async_copy(x_refs[k], win(k, *me), local_sems.at[k]) for k in range(K)]
        first = []
        for k in range(K):
            first.append(copy(k, 0, me, sibling, src=x_refs[k]))
            first += [copy(k, 1 + j, me, (*chip, mc), src=x_refs[k]) for j, chip in enumerate(chips)]
        return me, sibling, chips, copy, mine, first

    def start(x_refs, o_refs, sems):
        _, _, _, _, mine, first = plan(x_refs, o_refs, sems)
        for cp in mine + first:
            cp.start()

    def wait(x_refs, o_refs, sems):
        me, sibling, chips, copy, mine, first = plan(x_refs, o_refs, sems)
        mc = me[2]
        passed = []
        for j, chip in enumerate(chips):
            for k in range(K):
                copy(k, 1 + j, (*chip, mc), me).wait_recv()
                passed.append(copy(k, 4 + j, (*chip, mc), sibling))
                passed[-1].start()
        for k in range(K):
            copy(k, 0, sibling, me).wait_recv()
        for j, chip in enumerate(chips):
            for k in range(K):
                copy(k, 4 + j, (*chip, 1 - mc), me).wait_recv()
        for cp in first + passed:
            cp.wait_send()
        for cp in mine:
            cp.wait()

    scratch = [pltpu.SemaphoreType.DMA((7 * K,)), pltpu.SemaphoreType.DMA((7 * K,)), pltpu.SemaphoreType.DMA((K,))]
    return Rider(list(shards), out_shape, scratch, start, wait)


def exchange_rider(items):
    ns = len(items)
    out_shape = [_S((N_DEV,) + tuple(it[3]), it[0].dtype) for it in items]

    def plan(src_refs, o_refs, sems):
        send_sems, recv_sems, local_sems = sems
        mx, my, mc = lax.axis_index("x"), lax.axis_index("y"), lax.axis_index("c")
        me = 4 * mx + 2 * my + mc
        remote, own = [], []
        for s, (_, ax, n, _) in enumerate(items):
            own.append(pltpu.make_async_copy(_window(src_refs[s], ax, me, n), o_refs[s].at[me], local_sems.at[s]))
            for k in range(1, N_DEV):
                px = 1 - mx if k & 4 else mx
                py = 1 - my if k & 2 else my
                pc = 1 - mc if k & 1 else mc
                remote.append(pltpu.make_async_remote_copy(
                    src_ref=_window(src_refs[s], ax, 4 * px + 2 * py + pc, n), dst_ref=o_refs[s].at[me],
                    send_sem=send_sems.at[7 * s + k - 1], recv_sem=recv_sems.at[7 * s + k - 1],
                    device_id=(px, py, pc), device_id_type=MESH_ID))
        return remote, own

    def start(src_refs, o_refs, sems):
        remote, own = plan(src_refs, o_refs, sems)
        for cp in own + remote:
            cp.start()

    def wait(src_refs, o_refs, sems):
        remote, own = plan(src_refs, o_refs, sems)
        for cp in remote + own:
            cp.wait()

    scratch = [pltpu.SemaphoreType.DMA((7 * ns,)), pltpu.SemaphoreType.DMA((7 * ns,)), pltpu.SemaphoreType.DMA((ns,))]
    return Rider([it[0] for it in items], out_shape, scratch, start, wait)


def reduce_adamw(rcvs, w, m, v, name):
    L = len(rcvs)
    _, A, B, C = rcvs[0].shape
    tb = B
    while tb > 8 and tb * C > (1 << 17):
        tb //= 2

    def body(*refs):
        r_refs, (w_ref, m_ref, v_ref, g_ref, d_ref, mo_ref, vo_ref) = refs[:L], refs[L:]
        for k in range(L):
            @pl.when(pl.program_id(0) == k)
            def _(k=k):
                g = r_refs[k][0].astype(F32)
                for d in range(1, N_DEV):
                    g = g + r_refs[k][d].astype(F32)
                mn = ADAM_B1 * m_ref[...] + (1.0 - ADAM_B1) * g
                vn = ADAM_B2 * v_ref[...] + (1.0 - ADAM_B2) * jnp.square(g)
                m_hat = mn / (1.0 - ADAM_B1 ** ADAM_STEP)
                v_hat = vn / (1.0 - ADAM_B2 ** ADAM_STEP)
                g_ref[...] = g
                d_ref[...] = -ADAM_LR * (m_hat / (jnp.sqrt(v_hat) + ADAM_EPS) + ADAM_WD * w_ref[...])
                mo_ref[...] = mn
                vo_ref[...] = vn

    def rspec(k):
        return pl.BlockSpec((N_DEV, None, tb, C), lambda l, a, i: (0, jnp.where(l == k, a, 0), jnp.where(l == k, i, 0), 0))

    blk = pl.BlockSpec((None, tb, C), lambda l, a, i: (l * A + a, i, 0))
    return _pcall(body, grid=(L, A, B // tb), in_specs=[rspec(k) for k in range(L)] + [blk, blk, blk],
                  out_specs=[blk] * 4, out_shape=[_S((L * A, B, C))] * 4, name=name)(*rcvs, w, m, v)


def _w_in_pieces(g0, g1):
    per = D_IN // N_DEV
    return [(d, max(g0, d * per) - d * per, min(g1, (d + 1) * per) - d * per) for d in range(N_DEV) if max(g0, d * per) < min(g1, (d + 1) * per)]


def repack_w_in(w8, name):
    _, L, R, per = w8.shape
    tr = 256

    def cols(x_ref, g0, g1):
        return [x_ref[d, :, a:b] for d, a, b in _w_in_pieces(g0, g1)]

    def body(x_ref, *o_refs):
        for (name_, i), o_ref in zip(SEGS, o_refs):
            o_ref[...] = jnp.concatenate(cols(x_ref, _OFF[i], _OFF[i + 1]), axis=1)
        parts, at = [], 0
        for i, lane0 in SMALL_SRC:
            assert lane0 == at
            parts += cols(x_ref, _OFF[i], _OFF[i + 1])
            at += IN_SIZES[i]
        parts.append(jnp.zeros((tr, LANES - at), w8.dtype))
        o_refs[-1][...] = jnp.concatenate(parts, axis=1)

    widths = [IN_SIZES[i] for _, i in SEGS] + [LANES]
    outs = _pcall(body, grid=(L, R // tr), in_specs=[pl.BlockSpec((N_DEV, None, tr, per), lambda l, r: (0, l, r, 0))],
                  out_specs=[pl.BlockSpec((None, tr, w), lambda l, r: (l, r, 0)) for w in widths],
                  out_shape=[_S((L, R, w), w8.dtype) for w in widths], name=name)(w8)
    return dict(zip(SEG_NAMES, outs))


def repack_dw_in(dseg, name):
    R = dseg["z"].shape[0]
    per = D_IN // N_DEV
    tr = 128
    src = {i: (k, 0) for k, (_, i) in enumerate(SEGS)}
    src.update({i: (len(SEGS), lane0) for i, lane0 in SMALL_SRC})

    def body(*refs):
        s_refs, o_ref = refs[:-1], refs[-1]
        for d in range(N_DEV):
            parts = []
            for i in range(len(IN_SIZES)):
                g0, g1 = max(_OFF[i], d * per), min(_OFF[i + 1], (d + 1) * per)
                if g0 < g1:
                    k, c0 = src[i]
                    parts.append(s_refs[k][:, c0 + g0 - _OFF[i]:c0 + g1 - _OFF[i]])
            o_ref[d] = jnp.concatenate(parts, axis=1)

    arrs = [dseg[n] for n in SEG_NAMES]
    return _pcall(body, grid=(R // tr,), in_specs=[pl.BlockSpec((tr, a.shape[1]), lambda r: (r, 0)) for a in arrs],
                  out_specs=pl.BlockSpec((N_DEV, tr, per), lambda r: (0, r, 0)), out_shape=_S((N_DEV, R, per), arrs[0].dtype), name=name)(*arrs)


WEIGHTS = ("ln_in_g", "ln_in_b", "w_in", "ssd_conv_w", "ssd_conv_b", "ssd_dt_bias", "ssd_a_log", "ssd_d", "ssd_norm_w", "dn_conv_w",
           "dn_a_log", "dn_dt_bias", "dn_norm_w", "sg_ln_g", "sg_ln_b", "sg_w", "sg_b", "fox_f_bias", "gate_b", "w_branch", "w_out",
           "ln1_g", "ln1_b", "w_up", "w_down", "ln2_g", "ln2_b")
SHARDED = {"w_in": 2, "ssd_conv_w": 2, "dn_conv_w": 2, "gate_b": 2, "w_branch": 3, "w_out": 1, "w_up": 2, "w_down": 1}
SLABBED = ("w_in", "dn_conv_w")
MATMUL_WEIGHTS = ("w_in", "w_branch", "w_out", "w_up", "w_down")
REPLICATED_ENTRY = ("ln_in_g", "ln_in_b")
REPLICATED_LAYER = tuple(n for n in WEIGHTS if n not in SHARDED and n not in REPLICATED_ENTRY and n != "sg_w")
SEG_NAMES = tuple(n for n, _ in SEGS) + ("small",)
PACK_COLS = 1024


def _lanes(vec, off):
    return jnp.pad(vec, (off, LANES - off - vec.shape[0]))[None]


def _pack_small(parts, row_mult):
    flat = jnp.concatenate([q.reshape(-1) for q in parts])
    rows = -(-flat.shape[0] // (PACK_COLS * row_mult)) * row_mult
    return jnp.pad(flat, (0, rows * PACK_COLS - flat.shape[0])).reshape(1, rows, PACK_COLS)


EARLY = ("w_branch", "w_out", "w_up", "w_down", "gate_b")
LATE = ("w_in", "ssd_conv_w", "dn_conv_w")
WHOLE = ("sg_w",)


def _gather_rider(p, l, names):
    shards, axes = [], []
    for n in names:
        s = p[n][l:l + 1]
        s = s.astype(BF) if n in MATMUL_WEIGHTS else s
        shards.append(s[None] if n in SLABBED else s)
        axes.append(0 if n in SLABBED else SHARDED[n])
    return gather_rider(shards, axes)


def _exchange_items(g, p, names):
    items = []
    for n in names:
        local = p[n].shape[1:]
        if n in WHOLE:
            items.append((g[n], 0, -1, local))
        elif n in SLABBED:
            items.append((g[n], 0, 0, local))
        else:
            items.append((g[n], SHARDED[n] - 1, local[SHARDED[n] - 1], local))
    return items


def _use_gathered(w, names, arrays, l):
    for n, arr in zip(names, arrays):
        if n == "w_in":
            w[n] = repack_w_in(arr, f"w_in_repack_{l}")
        elif n == "ssd_conv_w":
            w["ssd_cw"] = arr[0]
        elif n == "dn_conv_w":
            w["dn_cw"] = jnp.moveaxis(arr[:, 0], 0, 1).reshape(4, 3 * BRANCH_W)
        elif n == "gate_b":
            w[n] = arr[0]
        else:
            w[n] = arr


def _layer_weights(p, l):
    w = {}
    w["ssd_cb"] = p["ssd_conv_b"][l][None]
    w["dn_cb"] = jnp.zeros((1, 3 * BRANCH_W), F32)
    w["ssd_ps"] = [_lanes(p["ssd_dt_bias"][l], DT0), _lanes(p["ssd_a_log"][l], DT0), _lanes(p["ssd_d"][l], DT0), p["ssd_norm_w"][l][None]]
    w["dn_ps"] = [_lanes(p["dn_a_log"][l], A0), _lanes(p["dn_dt_bias"][l], A0), p["dn_norm_w"][l][None]]
    w["sg_ps"] = [p["sg_ln_g"][l][None], p["sg_ln_b"][l][None], p["sg_w"][l], jnp.pad(p["sg_b"][l].T, ((0, 0), (0, LANES - 4)))]
    w["fox_ps"] = [_lanes(p["fox_f_bias"][l], FF0)]
    for n in ("ln1_g", "ln1_b", "ln2_g", "ln2_b"):
        w[n] = p[n][l][None]
    return w


def _scan_specs(T, a):
    c0 = lambda c, h: (c, 0)
    sr = min(T, SSD_ROWS)
    ssd = dict(f=ssd_chunk, xs=[(a["z"], (sr, 512), c0), (a["xbc_act"], (sr, 1024), c0), (a["small"], (sr, LANES), c0)],
               ys=[((T, BRANCH_W), (sr, BRANCH_W), c0)], state=(4, LANES, LANES), nc=T // sr, nh=1, shared=())
    dr = min(T, DN_ROWS)
    dn = dict(f=dn_chunk, xs=[(a["dn_act"], (dr, 3 * BRANCH_W), c0), (a["dngate"], (dr, BRANCH_W), c0), (a["small"], (dr, LANES), c0)],
              ys=[((T, BRANCH_W), (dr, BRANCH_W), c0)], state=(4, LANES, LANES), nc=T // dr, nh=1, shared=())
    sg = dict(f=sg_chunk, xs=[(a["sguv"], (128, 1024), c0)], ys=[((T, BRANCH_W), (128, BRANCH_W), c0)], state=(1, 8, LANES), nc=T // 128, nh=1, shared=())
    fc = dict(f=foxc_chunk, xs=[(a["small"], (128, LANES), c0)],
              ys=[((T, LANES), (128, LANES), c0)], state=(1, 1, LANES), nc=T // 128, nh=1, shared=())
    return ssd, dn, sg, fc


def _layer_fwd(h, w, l, dn_rider=None, fox_rider=None):
    T = h.shape[0]
    a = {"h": h, **proj_all(h, w["w_in"], f"proj_{l}")}
    a["xbc_act"] = conv_fwd(a["xbc"], w["ssd_cw"], w["ssd_cb"], f"ssd_conv_{l}")
    a["dn_act"] = conv_fwd(a["dnqkv"], w["dn_cw"], w["dn_cb"], f"dn_conv_{l}")
    ssd, dn, sg, fc = _scan_specs(T, a)
    (a["ya"], a["ssd_st"]), _ = scan_fwd(f"ssd_fwd_{l}", ssd["f"], ssd["xs"], w["ssd_ps"], ssd["ys"], ssd["state"], ssd["nc"], ssd["nh"])
    (a["yb"], a["dn_st"]), got = scan_fwd(f"dn_fwd_{l}", dn["f"], dn["xs"], w["dn_ps"], dn["ys"], dn["state"], dn["nc"], dn["nh"], rider=dn_rider)
    _use_gathered(w, EARLY, got, l)
    (a["yc"], a["sg_st"]), _ = scan_fwd(f"sg_fwd_{l}", sg["f"], sg["xs"], w["sg_ps"], sg["ys"], sg["state"], sg["nc"], sg["nh"])
    (a["ccol"], a["fc_st"]), _ = scan_fwd(f"foxc_fwd_{l}", fc["f"], fc["xs"], w["fox_ps"], fc["ys"], fc["state"], fc["nc"], fc["nh"])
    a["fox_qa"], a["fox_ka"], a["fox_va"] = fox_prep(a["foxqkv"], a["ccol"], f"fox_prep_{l}")
    (a["yd"], a["lse"]), carried = fox_fwd(a["fox_qa"], a["fox_ka"], a["fox_va"], f"fox_fwd_{l}", rider=fox_rider)
    a["merged"] = merge_fwd([a["ya"], a["yb"], a["yc"], a["yd"]], a["gates"], w["gate_b"], w["w_branch"], 0, f"merge_fwd_{l}")
    a["u1"], a["h1"] = out_fwd(a["merged"], h, w["w_out"], 0, w["ln1_g"], w["ln1_b"], f"out_fwd_{l}")
    a["u2"], a["h2"] = ff_fwd(a["h1"], w["w_up"], w["w_down"], 0, w["ln2_g"], w["ln2_b"], f"ff_fwd_{l}")
    return a, carried


def _layer_bwd(dh2, a, w, l, p, late_above):
    T = dh2.shape[0]
    g = {}
    du2, dh1, da, r, dg2, db2 = ff_bwd(a["u2"], dh2, a["h1"], w["ln2_g"], w["ln2_b"], w["w_up"], w["w_down"], 0, f"ff_bwd_{l}")
    g["ln2_g"], g["ln2_b"] = dg2[0], db2[0]
    g["w_up"] = matmul_dw(transpose_bf16(a["h1"], f"h1_t_{l}"), da, f"dwup_{l}")
    g["w_down"] = matmul_dw(r, du2, f"dwdown_{l}")
    du1, dmerged, dg1, db1 = out_bwd(a["u1"], dh1, w["ln1_g"], w["ln1_b"], w["w_out"], 0, f"out_bwd_{l}")
    g["ln1_g"], g["ln1_b"] = dg1[0], db1[0]
    g["w_out"] = matmul_dw(transpose_bf16(a["merged"], f"merged_t_{l}"), du1, f"dwout_{l}")
    ys = [a["ya"], a["yb"], a["yc"], a["yd"]]
    dya, dyb, dyc, dyd, dgl, dz, dgb = merge_bwd(ys, a["gates"], w["gate_b"], w["w_branch"], 0, dmerged, f"merge_bwd_{l}")
    g["gate_b"] = dgb
    g["w_branch"] = jnp.stack([matmul_tn(ys[i], dz, f"dwb{i}_{l}", b_col0=i * D_MODEL, n_cols=D_MODEL, out_dtype=BF) for i in range(4)])
    early = exchange_rider(_exchange_items(g, p, EARLY))
    dn_rider = early if late_above is None else exchange_rider(late_above)
    fox_rider = None if late_above is None else early
    ssd, dn, sg, fc = _scan_specs(T, a)
    (dz_ssd, dxbc_act, dsm_ssd, d_dtb, d_alog, d_dsk, d_nw), _ = scan_bwd(f"ssd_bwd_{l}", ssd["f"], ssd["xs"], w["ssd_ps"], ssd["ys"], [dya], a["ssd_st"],
                                                                           ssd["state"], ssd["nc"], ssd["nh"])
    g["ssd_dt_bias"], g["ssd_a_log"], g["ssd_d"], g["ssd_norm_w"] = d_dtb[0, DT0:DT0 + 8], d_alog[0, DT0:DT0 + 8], d_dsk[0, DT0:DT0 + 8], d_nw[0]
    dxbc, g["ssd_conv_w"], dcb = conv_bwd(a["xbc"], w["ssd_cw"], w["ssd_cb"], dxbc_act, f"ssd_conv_bwd_{l}")
    g["ssd_conv_b"] = dcb[0]
    (ddn_act, ddngate, dsm_dn, d_alog, d_dtb, d_nw), got_dn = scan_bwd(f"dn_bwd_{l}", dn["f"], dn["xs"], w["dn_ps"], dn["ys"], [dyb], a["dn_st"],
                                                                        dn["state"], dn["nc"], dn["nh"], rider=dn_rider)
    g["dn_a_log"], g["dn_dt_bias"], g["dn_norm_w"] = d_alog[0, A0:A0 + 4], d_dtb[0, A0:A0 + 4], d_nw[0]
    ddnqkv, g["dn_conv_w"], _ = conv_bwd(a["dnqkv"], w["dn_cw"], w["dn_cb"], ddn_act, f"dn_conv_bwd_{l}")
    (dsguv, d_lng, d_lnb, d_w, d_bt), _ = scan_bwd(f"sg_bwd_{l}", sg["f"], sg["xs"], w["sg_ps"], sg["ys"], [dyc], a["sg_st"], sg["state"], sg["nc"], sg["nh"])
    g["sg_ln_g"], g["sg_ln_b"], g["sg_w"], g["sg_b"] = d_lng[0], d_lnb[0], d_w, d_bt[:, :4].T
    qb, doa, qd, kd = fox_prep_bwd(a["foxqkv"], a["fox_qa"], a["yd"], a["lse"], dyd, f"fox_prep_bwd_{l}")
    (dfq, dfk, dfv, dccol), got_fox = fox_bwd(qb, a["fox_ka"], a["fox_va"], doa, qd, kd, f"fox_bwd_{l}", rider=fox_rider)
    (dsm_fox, d_fb), _ = scan_bwd(f"foxc_bwd_{l}", fc["f"], fc["xs"], w["fox_ps"], fc["ys"], [dccol], a["fc_st"], fc["state"], fc["nc"], fc["nh"])
    g["fox_f_bias"] = d_fb[0, FF0:FF0 + 8]
    dseg = {"z": dz_ssd, "xbc": dxbc, "dnqkv": ddnqkv, "dngate": ddngate, "sguv": dsguv,
            "foxqkv": jnp.concatenate([dfq, dfk, dfv], axis=1), "gates": dgl, "small": add3(dsm_ssd, dsm_dn, dsm_fox, f"dsmall_{l}")}
    h_t = transpose_bf16(a["h"], f"h_t_{l}")
    dwin = {n: matmul_dw(h_t, dseg[n], f"dwin_{n}_{l}") for n in SEG_NAMES}
    g["w_in"] = repack_dw_in(dwin, f"dw_in_repack_{l}")
    g["dn_conv_w"] = jnp.moveaxis(g["dn_conv_w"].reshape(4, N_DEV, 3 * BRANCH_W // N_DEV), 1, 0)
    got = {(EARLY, l): got_dn} if late_above is None else {(LATE + WHOLE, l + 1): got_dn, (EARLY, l): got_fox}
    return dseg, du1, g, got


def proj_all(h, w_in, name):
    T = h.shape[0]
    tm = min(T, 256)
    ns = len(SEG_NAMES)

    def body(*refs):
        h_ref, w_refs, o_refs = refs[0], refs[1:1 + ns], refs[1 + ns:]
        hb = h_ref[...].astype(BF)
        for w_ref, o_ref in zip(w_refs, o_refs):
            o_ref[...] = _dotb(hb, w_ref[...], NN)

    widths = [w_in[n].shape[2] for n in SEG_NAMES]
    in_specs = [_row(tm, D_MODEL)]
    in_specs += [pl.BlockSpec((None,) + w_in[n].shape[1:], lambda i: (0, 0, 0), pipeline_mode=pl.Buffered(1)) for n in SEG_NAMES]
    outs = _pcall(body, grid=(T // tm,), in_specs=in_specs, out_specs=[_row(tm, wd) for wd in widths],
                  out_shape=[_S((T, wd)) for wd in widths], name=name)(h, *[w_in[n] for n in SEG_NAMES])
    return dict(zip(SEG_NAMES, outs))


def dh_all(dseg, w_in, add, name, rider=None, norm=None):
    T = add.shape[0]
    tm = min(T, 256)
    ns = len(SEG_NAMES)

    def body(*refs):
        d_refs, w_refs, add_ref = refs[:ns], refs[ns:2 * ns], refs[2 * ns]
        acc = ALPHA * add_ref[...]
        for d_ref, w_ref in zip(d_refs, w_refs):
            acc = acc + _dotb(d_ref[...].astype(BF), w_ref[...], NT)
        if norm is None:
            refs[2 * ns + 1][...] = acc
        else:
            x_ref, g_ref, b_ref, dx_ref, dg_ref, db_ref = refs[2 * ns + 1:]
            _, vjp = jax.vjp(_ln, x_ref[...], g_ref[...], b_ref[...])
            dx, dg, db = vjp(acc)
            dx_ref[...] = dx
            first = pl.program_id(0) == 0
            _acc_out(dg_ref, dg, first)
            _acc_out(db_ref, db, first)

    C = D_MODEL
    in_specs = [_row(tm, dseg[n].shape[1]) for n in SEG_NAMES]
    in_specs += [pl.BlockSpec((None,) + w_in[n].shape[1:], lambda i: (0, 0, 0), pipeline_mode=pl.Buffered(1)) for n in SEG_NAMES]
    in_specs.append(_row(tm, C))
    args = [*[dseg[n] for n in SEG_NAMES], *[w_in[n] for n in SEG_NAMES], add]
    out_specs, out_shape = [_row(tm, C)], [_S((T, C))]
    if norm is not None:
        in_specs += [_row(tm, C), _full((1, C)), _full((1, C))]
        args += list(norm)
        out_specs += [_full((1, C)), _full((1, C))]
        out_shape += [_S((1, C)), _S((1, C))]
    return hosted_call(body, rider, grid=(T // tm,), in_specs=in_specs, out_specs=out_specs, out_shape=out_shape,
                       scratch_shapes=[], name=name, args=args)


def kernel(x, ln_in_g, ln_in_b, w_in, ssd_conv_w, ssd_conv_b, ssd_dt_bias, ssd_a_log, ssd_d, ssd_norm_w, dn_conv_w, dn_a_log, dn_dt_bias, dn_norm_w, sg_ln_g, sg_ln_b, sg_w, sg_b, fox_f_bias, gate_b, w_branch, w_out, ln1_g, ln1_b, w_up, w_down, ln2_g, ln2_b, loss_target, m_ln_in_g, m_ln_in_b, m_w_in, m_ssd_conv_w, m_ssd_conv_b, m_ssd_dt_bias, m_ssd_a_log, m_ssd_d, m_ssd_norm_w, m_dn_conv_w, m_dn_a_log, m_dn_dt_bias, m_dn_norm_w, m_sg_ln_g, m_sg_ln_b, m_sg_w, m_sg_b, m_fox_f_bias, m_gate_b, m_w_branch, m_w_out, m_ln1_g, m_ln1_b, m_w_up, m_w_down, m_ln2_g, m_ln2_b, v_ln_in_g, v_ln_in_b, v_w_in, v_ssd_conv_w, v_ssd_conv_b, v_ssd_dt_bias, v_ssd_a_log, v_ssd_d, v_ssd_norm_w, v_dn_conv_w, v_dn_a_log, v_dn_dt_bias, v_dn_norm_w, v_sg_ln_g, v_sg_ln_b, v_sg_w, v_sg_b, v_fox_f_bias, v_gate_b, v_w_branch, v_w_out, v_ln1_g, v_ln1_b, v_w_up, v_w_down, v_ln2_g, v_ln2_b):
    args = dict(locals())
    p = {n: args[n] for n in WEIGHTS}
    xt, target = x[0], loss_target[0]
    ws, acts = [_layer_weights(p, l) for l in range(DEPTH)], []
    (h,), gathered = ln_fwd(xt, ln_in_g[None], ln_in_b[None], "ln_in_fwd", rider=_gather_rider(p, 0, LATE))
    _use_gathered(ws[0], LATE, gathered, 0)
    for l in range(DEPTH):
        a, gathered = _layer_fwd(h, ws[l], l, dn_rider=_gather_rider(p, 0, EARLY) if l == 0 else None,
                                 fox_rider=_gather_rider(p, l + 1, LATE + EARLY) if l + 1 < DEPTH else None)
        if l + 1 < DEPTH:
            _use_gathered(ws[l + 1], LATE + EARLY, gathered, l + 1)
        acts.append(a)
        h = a["h2"]
    dh, loss = loss_head(h, target, "loss_head")
    loss = lax.psum(loss[0, 0], ("x", "y", "c"))

    layer_grads, got, late = [None] * DEPTH, {}, None
    for l in reversed(range(DEPTH)):
        dseg, du1, layer_grads[l], got_l = _layer_bwd(dh, acts[l], ws[l], l, p, late)
        got.update(got_l)
        late = _exchange_items(layer_grads[l], p, LATE + WHOLE)
        if l > 0:
            (dh,), _ = dh_all(dseg, ws[l]["w_in"], du1, f"dh_{l}")
    pack = _pack_small([jnp.stack([layer_grads[k][n] for k in range(DEPTH)]) for n in REPLICATED_LAYER], 8)
    (grad_x, dg_in, db_in), carried = dh_all(dseg, ws[0]["w_in"], du1, "dh_0", rider=exchange_rider(late + [(pack[0], 0, -1, pack.shape[1:])]),
                                             norm=(xt, ln_in_g[None], ln_in_b[None]))
    got[(LATE + WHOLE, 0)], got_layer_pack = carried[:-1], carried[-1]
    pack = _pack_small([dg_in[0], db_in[0]], 8)
    got_entry_pack = comm_call(exchange_rider([(pack[0], 0, -1, pack.shape[1:])]), "grads_exchange_entry_norm")[0]
    rcv = {(n, l): arr for (names, l), arrs in got.items() for n, arr in zip(names, arrs)}

    res = [{}, {}, {}, {}]
    for n in tuple(SHARDED) + WHOLE:
        shp = p[n].shape
        lead = math.prod(shp[1:-2])
        to3 = lambda t: t.reshape((-1,) + shp[-2:])
        outs = reduce_adamw([rcv[(n, l)].reshape((N_DEV, lead) + shp[-2:]) for l in range(DEPTH)],
                            to3(p[n]), to3(args["m_" + n]), to3(args["v_" + n]), f"adamw_{n}")
        for k in range(4):
            res[k][n] = outs[k].reshape(shp)
    for names, got_pack, rows, name in ((REPLICATED_LAYER, got_layer_pack, 8, "adamw_replicated"), (REPLICATED_ENTRY, got_entry_pack, 8, "adamw_entry_norm")):
        outs = reduce_adamw([got_pack[:, None]], _pack_small([p[n] for n in names], rows), _pack_small([args["m_" + n] for n in names], rows),
                            _pack_small([args["v_" + n] for n in names], rows), name)
        off = 0
        for n in names:
            shp = p[n].shape
            cnt = math.prod(shp)
            for k in range(4):
                res[k][n] = outs[k].reshape(-1)[off:off + cnt].reshape(shp)
            off += cnt
    return (loss, grad_x[None], *[res[0][n] for n in WEIGHTS], *[res[1][n] for n in WEIGHTS],
            *[res[2][n] for n in WEIGHTS], *[res[3][n] for n in WEIGHTS])
```
